```python
import jax, jax.numpy as jnp
from jax import lax
import numpy as np

D_MODEL = 1024
BATCH = 8
SEQ = 2048
DEPTH = 1

HEAD_DIM = 64
N_HEADS_A = 8
N_KV_A = 2
WINDOW_A = 128
N_HEADS_B = 8
DILATED_BRANCHES = ((128, 1), (512, 4), (2048, 16))
BLOCK = 128
ROT_DIM = HEAD_DIM // 4
ROPE_THETA = 500000.0
MIX_A = N_HEADS_A * HEAD_DIM
KV_A = N_KV_A * HEAD_DIM
MIX_B = N_HEADS_B * HEAD_DIM
MIX_WIDTH = MIX_A + MIX_B
IN_WIDTH = MIX_A + 2 * KV_A + 3 * MIX_B
N_EXPERTS = 256
TOP_K = 8
N_GROUPS = 8
TOPK_GROUPS = 4
EXPERT_DIM = 256
SHARED_DIM = 256
ROUTED_SCALE = 2.5
EXPERT_BLOCK = 128
EPS = 1e-6

kernel_name = "hybrid_swa_sink_dilated_moe_adaln"

F32 = jnp.float32


def rms_norm(t, g):
    tf = t.astype(F32)
    return tf * lax.rsqrt(jnp.mean(tf * tf, axis=-1, keepdims=True) + EPS) * g.astype(F32)


def rope_tables(positions):
    inv = ROPE_THETA ** (-jnp.arange(0, ROT_DIM, 2, dtype=F32) / ROT_DIM)
    ang = positions.astype(F32)[..., None] * inv
    return jnp.cos(ang)[:, :, None, :], jnp.sin(ang)[:, :, None, :]


def partial_rotary(t, cos, sin):
    half = ROT_DIM // 2
    t1, t2, rest = t[..., :half], t[..., half:ROT_DIM], t[..., ROT_DIM:]
    return jnp.concatenate([t1 * cos - t2 * sin, t2 * cos + t1 * sin, rest], axis=-1)


def banded_attention(q, k, v, max_dist, sinks=None):
    n, L, hk, g, hd = q.shape
    nb = L // BLOCK
    qb = q.reshape(n, nb, BLOCK, hk, g, hd).astype(F32)

    def with_prev(t):
        tb = t.reshape(n, nb, BLOCK, hk, hd).astype(F32)
        prev = jnp.pad(tb, ((0, 0), (1, 0), (0, 0), (0, 0), (0, 0)))[:, :-1]
        return jnp.concatenate([prev, tb], axis=2)

    kk, vv = with_prev(k), with_prev(v)
    s = jnp.einsum('nbqhgd,nbkhd->nbhgqk', qb, kk) * (hd ** -0.5)
    qpos = jnp.arange(BLOCK)[:, None] + BLOCK
    kpos = jnp.arange(2 * BLOCK)[None, :]
    dist = qpos - kpos
    band = (dist >= 0) & (dist <= max_dist)
    has_prev = (jnp.arange(nb)[:, None, None] > 0) | (kpos[None] >= BLOCK)
    mask = band[None] & has_prev
    s = jnp.where(mask[None, :, None, None], s, -jnp.inf)
    if sinks is None:
        lse = jax.nn.logsumexp(s, axis=-1)
    else:
        sink_col = jnp.broadcast_to(sinks.astype(F32)[None, None, :, :, None, None],
                                    s.shape[:-1] + (1,))
        lse = jax.nn.logsumexp(jnp.concatenate([s, sink_col], axis=-1), axis=-1)
    p = jnp.exp(s - lse[..., None])
    o = jnp.einsum('nbhgqk,nbkhd->nbqhgd', p, vv).reshape(n, L, hk, g, hd)
    lse = lse.transpose(0, 1, 4, 2, 3).reshape(n, L, hk, g)
    return o, lse


def dilated_attention(q, k, v):
    B, S, H, hd = q.shape
    outs, lses = [], []
    for window, dil in DILATED_BRANCHES:
        L = S // dil
        Lp = -(-L // BLOCK) * BLOCK

        def to_sub(t):
            t = t.reshape(B, L, dil, H, hd).transpose(0, 2, 1, 3, 4).reshape(B * dil, L, H, hd)
            return jnp.pad(t, ((0, 0), (0, Lp - L), (0, 0), (0, 0)))

        o, lse = banded_attention(to_sub(q)[:, :, :, None], to_sub(k), to_sub(v), window // dil)
        outs.append(o[:, :L, :, 0].reshape(B, dil, L, H, hd).transpose(0, 2, 1, 3, 4).reshape(B, S, H, hd))
        lses.append(lse[:, :L, :, 0].reshape(B, dil, L, H).transpose(0, 2, 1, 3).reshape(B, S, H))
    w = jax.nn.softmax(jnp.stack(lses), axis=0)
    return jnp.einsum('rbsh,rbshd->bshd', w, jnp.stack(outs))


def hybrid_mixer(h, cos, sin, w_in, g_q_a, g_k_a, sinks_a, g_q_b, g_k_b, g_out_a, g_out_b, w_out):
    B, S, _ = h.shape
    proj = h @ w_in
    cuts = [int(i) for i in np.cumsum([MIX_A, KV_A, KV_A, MIX_B, MIX_B])]
    q_a, k_a, v_a, q_b, k_b, v_b = jnp.split(proj, cuts, axis=-1)
    heads = lambda t: t.reshape(B, S, -1, HEAD_DIM)
    q_a = partial_rotary(rms_norm(heads(q_a), g_q_a), cos, sin)
    k_a = partial_rotary(rms_norm(heads(k_a), g_k_a), cos, sin)
    q_a = q_a.reshape(B, S, N_KV_A, N_HEADS_A // N_KV_A, HEAD_DIM)
    o_a, _ = banded_attention(q_a, k_a, heads(v_a), WINDOW_A - 1,
                              sinks_a.reshape(N_KV_A, N_HEADS_A // N_KV_A))
    o_a = rms_norm(o_a.reshape(B, S, MIX_A), g_out_a)
    q_b = partial_rotary(rms_norm(heads(q_b), g_q_b), cos, sin)
    k_b = partial_rotary(rms_norm(heads(k_b), g_k_b), cos, sin)
    o_b = dilated_attention(q_b, k_b, heads(v_b).astype(F32))
    o_b = rms_norm(o_b.reshape(B, S, MIX_B), g_out_b)
    return jnp.concatenate([o_a, o_b], axis=-1) @ w_out.astype(F32)


def swiglu(t, wg, wu, wd):
    return (jax.nn.silu(t @ wg) * (t @ wu)) @ wd


def routed_experts(xf, top_e, gates, w_gate_e, w_up_e, w_down_e):
    N, D = xf.shape
    e_flat = top_e.reshape(-1)
    NK = e_flat.shape[0]
    tok_flat = jnp.repeat(jnp.arange(N, dtype=jnp.int32), TOP_K)
    g_flat = gates.reshape(-1)
    order = jnp.argsort(e_flat)
    e_sorted = e_flat[order]
    counts = jnp.bincount(e_flat, length=N_EXPERTS)
    starts = jnp.cumsum(counts) - counts
    padded = (counts + EXPERT_BLOCK - 1) // EXPERT_BLOCK * EXPERT_BLOCK
    pends = jnp.cumsum(padded)
    pstarts = pends - padded
    dest = pstarts[e_sorted] + jnp.arange(NK) - starts[e_sorted]
    rows = -(-NK // EXPERT_BLOCK) * EXPERT_BLOCK + N_EXPERTS * EXPERT_BLOCK
    n_blocks = rows // EXPERT_BLOCK
    row_tok = jnp.zeros((rows,), jnp.int32).at[dest].set(tok_flat[order])
    row_gate = jnp.zeros((rows,), F32).at[dest].set(g_flat[order])
    block_e = jnp.minimum(
        jnp.searchsorted(pends, jnp.arange(n_blocks) * EXPERT_BLOCK, side='right'), N_EXPERTS - 1)

    def step(acc, blk):
        e, tok, g = blk
        yb = swiglu(xf[tok], w_gate_e[e], w_up_e[e], w_down_e[e])
        return acc.at[tok].add(yb.astype(F32) * g[:, None]), None

    acc, _ = lax.scan(step, jnp.zeros((N, D), F32),
                      (block_e, row_tok.reshape(n_blocks, EXPERT_BLOCK),
                       row_gate.reshape(n_blocks, EXPERT_BLOCK)))
    return acc


def moe_ffn(h, w_router, router_bias, w_gate_e, w_up_e, w_down_e, w_gate_s, w_up_s, w_down_s):
    B, S, D = h.shape
    N = B * S
    xf = h.reshape(N, D)
    scores = jax.nn.sigmoid(xf.astype(F32) @ w_router.astype(F32))
    biased = scores + router_bias.astype(F32)
    grp = lax.top_k(biased.reshape(N, N_GROUPS, N_EXPERTS // N_GROUPS), 2)[0].sum(-1)
    top_grp = lax.top_k(grp, TOPK_GROUPS)[1]
    gmask = jnp.any(top_grp[..., None] == jnp.arange(N_GROUPS), axis=1)
    emask = jnp.repeat(gmask, N_EXPERTS // N_GROUPS, axis=-1)
    _, top_e = lax.top_k(jnp.where(emask, biased, -jnp.inf), TOP_K)
    gates = jnp.take_along_axis(scores, top_e, axis=-1)
    gates = gates / jnp.sum(gates, axis=-1, keepdims=True) * ROUTED_SCALE
    routed = routed_experts(xf, top_e, gates, w_gate_e, w_up_e, w_down_e)
    shared = swiglu(xf, w_gate_s, w_up_s, w_down_s).astype(F32)
    return (routed + shared).reshape(B, S, D)


def setup_inputs(seed: int = 0) -> dict:
    key = jax.random.key(seed)
    ks = jax.random.split(key, 24)
    nrm = lambda k, shape, s: jax.random.normal(k, shape, F32) * s
    gain = lambda k, shape: 1.0 + 0.02 * jax.random.normal(k, shape, F32)
    x = jax.random.normal(ks[0], (BATCH, SEQ, D_MODEL), F32)
    c = jax.random.normal(ks[1], (BATCH, D_MODEL), F32)
    offset = jax.random.randint(ks[2], (BATCH, 1), 0, 4096, dtype=jnp.int32)
    positions = offset + jnp.arange(SEQ, dtype=jnp.int32)[None, :]
    return {
        "x": x,
        "c": c,
        "positions": positions,
        "w_ada": nrm(ks[3], (DEPTH, D_MODEL, 6 * D_MODEL), 0.25 * D_MODEL ** -0.5),
        "b_ada": nrm(ks[4], (DEPTH, 6 * D_MODEL), 0.02),
        "g_norm_mix": gain(ks[5], (DEPTH, D_MODEL)),
        "w_in": nrm(ks[6], (DEPTH, D_MODEL, IN_WIDTH), D_MODEL ** -0.5),
        "g_q_a": gain(ks[7], (DEPTH, HEAD_DIM)),
        "g_k_a": gain(ks[8], (DEPTH, HEAD_DIM)),
        "sinks_a": nrm(ks[9], (DEPTH, N_HEADS_A), 0.5),
        "g_q_b": gain(ks[10], (DEPTH, HEAD_DIM)),
        "g_k_b": gain(ks[11], (DEPTH, HEAD_DIM)),
        "g_out_a": gain(ks[12], (DEPTH, MIX_A)),
        "g_out_b": gain(ks[13], (DEPTH, MIX_B)),
        "w_out": nrm(ks[14], (DEPTH, MIX_WIDTH, D_MODEL), MIX_WIDTH ** -0.5),
        "g_norm_ffn": gain(ks[15], (DEPTH, D_MODEL)),
        "w_router": nrm(ks[16], (DEPTH, D_MODEL, N_EXPERTS), D_MODEL ** -0.5),
        "router_bias": nrm(ks[17], (DEPTH, N_EXPERTS), 0.01),
        "w_gate_e": nrm(ks[18], (DEPTH, N_EXPERTS, D_MODEL, EXPERT_DIM), D_MODEL ** -0.5),
        "w_up_e": nrm(ks[19], (DEPTH, N_EXPERTS, D_MODEL, EXPERT_DIM), D_MODEL ** -0.5),
        "w_down_e": nrm(ks[20], (DEPTH, N_EXPERTS, EXPERT_DIM, D_MODEL), EXPERT_DIM ** -0.5),
        "w_gate_s": nrm(ks[21], (DEPTH, D_MODEL, SHARED_DIM), D_MODEL ** -0.5),
        "w_up_s": nrm(ks[22], (DEPTH, D_MODEL, SHARED_DIM), D_MODEL ** -0.5),
        "w_down_s": nrm(ks[23], (DEPTH, SHARED_DIM, D_MODEL), SHARED_DIM ** -0.5),
    }


def reference(x, c, positions, w_ada, b_ada, g_norm_mix, w_in, g_q_a, g_k_a, sinks_a,
              g_q_b, g_k_b, g_out_a, g_out_b, w_out, g_norm_ffn, w_router, router_bias,
              w_gate_e, w_up_e, w_down_e, w_gate_s, w_up_s, w_down_s):
    B, S, D = x.shape
    cos, sin = rope_tables(positions)
    cond = jax.nn.silu(c.astype(F32))
    for l in range(DEPTH):
        mod = (cond @ w_ada[l].astype(F32) + b_ada[l].astype(F32)).reshape(B, 6, 1, D)
        shift_a, scale_a, gate_a = mod[:, 0], mod[:, 1], mod[:, 2]
        shift_m, scale_m, gate_m = mod[:, 3], mod[:, 4], mod[:, 5]
        h = (rms_norm(x, g_norm_mix[l]) * (1.0 + scale_a) + shift_a).astype(x.dtype)
        y = hybrid_mixer(h, cos, sin, w_in[l], g_q_a[l], g_k_a[l], sinks_a[l],
                         g_q_b[l], g_k_b[l], g_out_a[l], g_out_b[l], w_out[l])
        x = (x.astype(F32) + gate_a * y).astype(x.dtype)
        h = (rms_norm(x, g_norm_ffn[l]) * (1.0 + scale_m) + shift_m).astype(x.dtype)
        y = moe_ffn(h, w_router[l], router_bias[l], w_gate_e[l], w_up_e[l], w_down_e[l],
                    w_gate_s[l], w_up_s[l], w_down_s[l])
        x = (x.astype(F32) + gate_m * y).astype(x.dtype)
    return x
```

```python
import functools

import numpy as np
import jax
import jax.numpy as jnp
from jax import lax
from jax.experimental import pallas as pl
from jax.experimental.pallas import tpu as pltpu

F32 = jnp.float32
BF16 = jnp.bfloat16
I32 = jnp.int32

HEAD_DIM = 64
N_HEADS_A = 8
N_KV_A = 2
WINDOW_A = 128
N_HEADS_B = 8
DILATED_BRANCHES = ((128, 1), (512, 4), (2048, 16))
BLOCK = 128
ROT_DIM = HEAD_DIM // 4
ROPE_THETA = 500000.0
MIX_A = N_HEADS_A * HEAD_DIM
KV_A = N_KV_A * HEAD_DIM
MIX_B = N_HEADS_B * HEAD_DIM
N_EXPERTS = 256
TOP_K = 8
N_GROUPS = 8
TOPK_GROUPS = 4
GROUP_SIZE = N_EXPERTS // N_GROUPS
ROUTED_SCALE = 2.5
EPS = 1e-6

LANES = 128
HEADS_PER_VREG = LANES // HEAD_DIM
N_PAIRS = MIX_A // LANES
NEG = -1e30
VMEM_LIMIT = 48 * 1024 * 1024

TT_PROJ = 512
TT_ROUTE = 256
TT_DISPATCH = 256
TT_COMBINE = 128
EXPERT_ROWS = 256

PAIR_ORDER_A = tuple(h for p in range(N_PAIRS) for h in (p, p + N_HEADS_A // N_KV_A))


def _params(*sem):
    return pltpu.CompilerParams(dimension_semantics=sem, vmem_limit_bytes=VMEM_LIMIT)


def _silu(t):
    return t / (1.0 + jnp.exp(-t))


def _rms_rows(t):
    return t * lax.rsqrt(jnp.mean(t * t, axis=-1, keepdims=True) + EPS)


def _ada_kernel(c_ref, w_ref, b_ref, o_ref):
    cond = _silu(c_ref[...])
    o_ref[...] = jnp.dot(cond.astype(BF16), w_ref[...].astype(BF16),
                         preferred_element_type=F32) + b_ref[...]


def _adaln(c, w_ada, b_ada):
    nb, d = c.shape
    width = w_ada.shape[1]
    tn = 1024
    return pl.pallas_call(
        _ada_kernel,
        grid=(width // tn,),
        in_specs=[pl.BlockSpec((nb, d), lambda j: (0, 0)),
                  pl.BlockSpec((d, tn), lambda j: (0, j)),
                  pl.BlockSpec((1, tn), lambda j: (0, j))],
        out_specs=pl.BlockSpec((nb, tn), lambda j: (0, j)),
        out_shape=jax.ShapeDtypeStruct((nb, width), F32),
        compiler_params=_params("arbitrary"),
        name="adaln",
    )(c, w_ada, b_ada.reshape(1, width))


COL_QA, COL_KA, COL_VA = 0, MIX_A, MIX_A + KV_A
COL_QB = MIX_A + 2 * KV_A
COL_KB, COL_VB = COL_QB + MIX_B, COL_QB + 2 * MIX_B
IN_WIDTH = COL_VB + MIX_B


def _inproj_kernel(x_ref, mod_ref, pos_ref, gn_ref, w_ref, gcol_ref, invf_ref, bd_ref,
                   qa_ref, ka_ref, va_ref, qb_ref, kb_ref, vb_ref):
    shift, scale = mod_ref[0:1, :], mod_ref[1:2, :]
    h = _rms_rows(x_ref[...]) * gn_ref[...] * (1.0 + scale) + shift
    proj = jnp.dot(h.astype(BF16), w_ref[...], preferred_element_type=F32)

    ang = pos_ref[...].astype(F32) * invf_ref[...]
    cs, sn = jnp.cos(ang), jnp.sin(ang)
    lane = lax.broadcasted_iota(I32, (1, LANES), 1) % HEAD_DIM
    s_lo = jnp.where(lane < ROT_DIM // 2, -sn, 0.0)
    s_hi = jnp.where((lane >= ROT_DIM // 2) & (lane < ROT_DIM), sn, 0.0)
    bd = bd_ref[...]

    def norm_rope(col0, width, out_ref):
        for j in range(width // LANES):
            c = col0 + j * LANES
            t = proj[:, c:c + LANES]
            sq = t * t
            hi = sq.astype(BF16)
            lo = (sq - hi.astype(F32)).astype(BF16)
            ss = (jnp.dot(hi, bd, preferred_element_type=F32)
                  + jnp.dot(lo, bd, preferred_element_type=F32))
            t = t * lax.rsqrt(ss * (1.0 / HEAD_DIM) + EPS) * gcol_ref[:, c:c + LANES]
            t = (t * cs + pltpu.roll(t, LANES - ROT_DIM // 2, 1) * s_lo
                 + pltpu.roll(t, ROT_DIM // 2, 1) * s_hi)
            out_ref[:, j * LANES:(j + 1) * LANES] = t.astype(BF16)

    norm_rope(COL_QA, MIX_A, qa_ref)
    norm_rope(COL_KA, KV_A, ka_ref)
    norm_rope(COL_QB, MIX_B, qb_ref)
    norm_rope(COL_KB, MIX_B, kb_ref)
    va_ref[...] = proj[:, COL_VA:COL_VA + KV_A].astype(BF16)
    vb_ref[...] = proj[:, COL_VB:COL_VB + MIX_B].astype(BF16)


def _inproj(xf, mod, pos, g_norm, w_in_p, gcol, invf, bd, seq):
    n, d = xf.shape
    tt = TT_PROJ
    tiles_per_seq = seq // tt
    widths = (MIX_A, KV_A, KV_A, MIX_B, MIX_B, MIX_B)
    full = lambda shape: pl.BlockSpec(shape, lambda i: (0,) * len(shape))
    return pl.pallas_call(
        _inproj_kernel,
        grid=(n // tt,),
        in_specs=[pl.BlockSpec((tt, d), lambda i: (i, 0)),
                  pl.BlockSpec((None, 6, d), lambda i: (i // tiles_per_seq, 0, 0)),
                  pl.BlockSpec((tt, 1), lambda i: (i, 0)),
                  full((1, d)), full((d, IN_WIDTH)), full((1, IN_WIDTH)),
                  full((1, LANES)), full((LANES, LANES))],
        out_specs=[pl.BlockSpec((tt, w), lambda i: (i, 0)) for w in widths],
        out_shape=[jax.ShapeDtypeStruct((n, w), BF16) for w in widths],
        compiler_params=_params("parallel"),
        name="inproj",
    )(xf, mod, pos, g_norm, w_in_p, gcol, invf, bd)


def _attn_kernel(*refs, kv_shared, max_dist, use_prev, has_sinks, want_lse):
    refs = list(refs)
    sink_ref = refs.pop(0) if has_sinks else None
    q_ref = refs.pop(0)
    kp_ref = refs.pop(0) if use_prev else None
    kc_ref = refs.pop(0)
    vp_ref = refs.pop(0) if use_prev else None
    vc_ref = refs.pop(0)
    o_ref = refs.pop(0)
    lse_ref = refs.pop(0) if want_lse else None

    blk = pl.program_id(2)
    nq = 2 * BLOCK
    nk = 2 * BLOCK if use_prev else BLOCK
    qpos = lax.broadcasted_iota(I32, (nq, nk), 0) % BLOCK
    kpos = lax.broadcasted_iota(I32, (nq, nk), 1)
    if use_prev:
        dist = qpos + BLOCK - kpos
        valid = (dist >= 0) & (dist <= max_dist) & ((kpos >= BLOCK) | (blk > 0))
    else:
        dist = qpos - kpos
        valid = (dist >= 0) & (dist <= max_dist)
    lane = lax.broadcasted_iota(I32, (nq, LANES), 1)
    row = lax.broadcasted_iota(I32, (nq, LANES), 0)
    own_half = (lane < HEAD_DIM) == (row < BLOCK)
    left_lanes = lax.broadcasted_iota(I32, (BLOCK, LANES), 1) < HEAD_DIM
    lane8 = lax.broadcasted_iota(I32, (BLOCK, 2 * N_PAIRS), 1)
    lse_blk = jnp.zeros((BLOCK, 2 * N_PAIRS), F32)

    for p in range(N_PAIRS):
        cq = slice(p * LANES, (p + 1) * LANES)
        ck = slice(0, LANES) if kv_shared else cq
        qp = q_ref[:, cq]
        qs = jnp.concatenate([qp, qp], axis=0)
        qs = jnp.where(own_half, qs, jnp.zeros_like(qs))
        if use_prev:
            k = jnp.concatenate([kp_ref[:, ck], kc_ref[:, ck]], axis=0)
            v = jnp.concatenate([vp_ref[:, ck], vc_ref[:, ck]], axis=0)
        else:
            k, v = kc_ref[:, ck], vc_ref[:, ck]
        s = lax.dot_general(qs, k, (((1,), (1,)), ((), ())), preferred_element_type=F32)
        s = jnp.where(valid, s, NEG)
        m = jnp.max(s, axis=-1, keepdims=True)
        if has_sinks:
            rows1 = lax.broadcasted_iota(I32, (nq, 1), 0)
            sink = jnp.where(rows1 < BLOCK, sink_ref[2 * p], sink_ref[2 * p + 1])
            m = jnp.maximum(m, sink)
        e = jnp.exp(s - m)
        l = jnp.sum(e, axis=-1, keepdims=True)
        if has_sinks:
            l = l + jnp.exp(sink - m)
        o = jnp.dot(e.astype(BF16), v, preferred_element_type=F32) / l
        o_ref[:, cq] = jnp.where(left_lanes, o[:BLOCK], o[BLOCK:]).astype(BF16)
        if want_lse:
            lse = m + jnp.log(l)
            lse_blk = (lse_blk + jnp.where(lane8 == 2 * p, lse[:BLOCK], 0.0)
                       + jnp.where(lane8 == 2 * p + 1, lse[BLOCK:], 0.0))
    if want_lse:
        lse_ref[...] = lse_blk


def _attention(q, k, v, *, nbatch, seq, dil, max_dist, kv_shared, sinks=None, want_lse):
    length = seq // dil
    nblk = length // BLOCK
    use_prev = nblk > 1
    kw = k.shape[1]
    view = lambda t: t.reshape(nbatch, length, dil * t.shape[1])
    cur = lambda b, r, i: (b, i, r)
    prev = lambda b, r, i: (b, jnp.maximum(i - 1, 0), r)
    in_specs, args = [], []
    if sinks is not None:
        in_specs.append(pl.BlockSpec(memory_space=pltpu.SMEM))
        args.append(sinks)
    in_specs.append(pl.BlockSpec((None, BLOCK, MIX_B), cur))
    args.append(view(q))
    for t in (k, v):
        if use_prev:
            in_specs.append(pl.BlockSpec((None, BLOCK, kw), prev))
            args.append(view(t))
        in_specs.append(pl.BlockSpec((None, BLOCK, kw), cur))
        args.append(view(t))
    out_specs = [pl.BlockSpec((None, BLOCK, MIX_B), cur)]
    out_shape = [jax.ShapeDtypeStruct((nbatch, length, dil * MIX_B), BF16)]
    if want_lse:
        out_specs.append(pl.BlockSpec((None, None, BLOCK, N_HEADS_B), lambda b, r, i: (b, r, i, 0)))
        out_shape.append(jax.ShapeDtypeStruct((nbatch, dil, length, N_HEADS_B), F32))
    outs = pl.pallas_call(
        functools.partial(_attn_kernel, kv_shared=kv_shared, max_dist=max_dist, use_prev=use_prev,
                          has_sinks=sinks is not None, want_lse=want_lse),
        grid=(nbatch, dil, nblk),
        in_specs=in_specs, out_specs=out_specs, out_shape=out_shape,
        compiler_params=_params("parallel", "parallel", "arbitrary"),
        name=f"attn_d{dil}" + ("_swa" if kv_shared else ""),
    )(*args)
    o = outs[0].reshape(nbatch * seq, MIX_B)
    if not want_lse:
        return o, None
    lse = outs[1].transpose(0, 2, 1, 3).reshape(nbatch * seq, N_HEADS_B)
    return o, lse


def _expand_heads(w, width):
    head = lax.broadcasted_iota(I32, (1, width), 1) // HEAD_DIM
    out = jnp.zeros((w.shape[0], width), F32)
    for hd in range(w.shape[1]):
        out = jnp.where(head == hd, w[:, hd:hd + 1], out)
    return out


def _outproj_kernel(x_ref, mod_ref, oa_ref, ob1_ref, ob2_ref, ob3_ref, l1_ref, l2_ref, l3_ref,
                    goa_ref, gob_ref, wo_ref, gf_ref, wgs_ref, wus_ref, wds_ref,
                    h2_ref, base_ref):
    gate_a = mod_ref[2:3, :]
    shift_m, scale_m, gate_m = mod_ref[3:4, :], mod_ref[4:5, :], mod_ref[5:6, :]
    l1, l2, l3 = l1_ref[...], l2_ref[...], l3_ref[...]
    mx = jnp.maximum(jnp.maximum(l1, l2), l3)
    e1, e2, e3 = jnp.exp(l1 - mx), jnp.exp(l2 - mx), jnp.exp(l3 - mx)
    den = e1 + e2 + e3
    ob = (_expand_heads(e1 / den, MIX_B) * ob1_ref[...].astype(F32)
          + _expand_heads(e2 / den, MIX_B) * ob2_ref[...].astype(F32)
          + _expand_heads(e3 / den, MIX_B) * ob3_ref[...].astype(F32))
    ob = _rms_rows(ob) * gob_ref[...]
    oa = _rms_rows(oa_ref[...].astype(F32)) * goa_ref[...]
    y = (jnp.dot(oa.astype(BF16), wo_ref[0:MIX_A, :], preferred_element_type=F32)
         + jnp.dot(ob.astype(BF16), wo_ref[MIX_A:MIX_A + MIX_B, :], preferred_element_type=F32))
    x1 = x_ref[...] + gate_a * y
    h2 = _rms_rows(x1) * gf_ref[...] * (1.0 + scale_m) + shift_m
    h2_ref[...] = h2
    hb = h2.astype(BF16)
    act = (_silu(jnp.dot(hb, wgs_ref[...], preferred_element_type=F32))
           * jnp.dot(hb, wus_ref[...], preferred_element_type=F32))
    shared = jnp.dot(act.astype(BF16), wds_ref[...], preferred_element_type=F32)
    base_ref[...] = x1 + gate_m * shared


def _outproj(xf, mod, oa, obs, lses, goa, gob, wo_p, gf, wgs, wus, wds, seq):
    n, d = xf.shape
    tt = TT_PROJ // 2
    tiles_per_seq = seq // tt
    tile = lambda w: pl.BlockSpec((tt, w), lambda i: (i, 0))
    full = lambda shape: pl.BlockSpec(shape, lambda i: (0,) * len(shape))
    sd = wgs.shape[1]
    return pl.pallas_call(
        _outproj_kernel,
        grid=(n // tt,),
        in_specs=[tile(d), pl.BlockSpec((None, 6, d), lambda i: (i // tiles_per_seq, 0, 0)),
                  tile(MIX_A), tile(MIX_B), tile(MIX_B), tile(MIX_B),
                  tile(N_HEADS_B), tile(N_HEADS_B), tile(N_HEADS_B),
                  full((1, MIX_A)), full((1, MIX_B)), full((MIX_A + MIX_B, d)), full((1, d)),
                  full((d, sd)), full((d, sd)), full((sd, d))],
        out_specs=[tile(d), tile(d)],
        out_shape=[jax.ShapeDtypeStruct((n, d), F32), jax.ShapeDtypeStruct((n, d), F32)],
        compiler_params=_params("parallel"),
        name="outproj",
    )(xf, mod, oa, *obs, *lses, goa, gob, wo_p, gf, wgs, wus, wds)


def _router_kernel(h_ref, wrt_ref, bias_ref, e_ref, g_ref, cnt_ref):
    tt = h_ref.shape[0]
    logits = lax.dot_general(wrt_ref[...], h_ref[...].astype(BF16), (((1,), (1,)), ((), ())),
                             preferred_element_type=F32)
    scores = 1.0 / (1.0 + jnp.exp(-logits))
    biased = scores + bias_ref[...]
    ninf = -jnp.inf

    j32 = lax.broadcasted_iota(I32, (GROUP_SIZE, tt), 0).astype(F32)
    grp = []
    for g in range(N_GROUPS):
        bg = biased[g * GROUP_SIZE:(g + 1) * GROUP_SIZE, :]
        m1 = jnp.max(bg, axis=0, keepdims=True)
        i1 = jnp.min(jnp.where(bg == m1, j32, float(GROUP_SIZE)), axis=0, keepdims=True)
        m2 = jnp.max(jnp.where(j32 == i1, ninf, bg), axis=0, keepdims=True)
        grp.append(m1 + m2)
    grp = jnp.concatenate(grp, axis=0)
    g8 = lax.broadcasted_iota(I32, (N_GROUPS, tt), 0).astype(F32)
    chosen = jnp.zeros((N_GROUPS, tt), F32)
    for _ in range(TOPK_GROUPS):
        gm = jnp.max(grp, axis=0, keepdims=True)
        gi = jnp.min(jnp.where(grp == gm, g8, float(N_GROUPS)), axis=0, keepdims=True)
        hit = g8 == gi
        chosen = jnp.where(hit, 1.0, chosen)
        grp = jnp.where(hit, ninf, grp)
    masked = jnp.concatenate(
        [jnp.where(chosen[g:g + 1, :] > 0.0, biased[g * GROUP_SIZE:(g + 1) * GROUP_SIZE, :], ninf)
         for g in range(N_GROUPS)], axis=0)

    eio = lax.broadcasted_iota(I32, (N_EXPERTS, tt), 0).astype(F32)
    picked = jnp.zeros((N_EXPERTS, tt), F32)
    es, gs = [], []
    for _ in range(TOP_K):
        m = jnp.max(masked, axis=0, keepdims=True)
        idx = jnp.min(jnp.where(masked == m, eio, float(N_EXPERTS)), axis=0, keepdims=True)
        hit = eio == idx
        gs.append(jnp.sum(jnp.where(hit, scores, 0.0), axis=0, keepdims=True))
        es.append(idx)
        picked = jnp.where(hit, 1.0, picked)
        masked = jnp.where(hit, ninf, masked)
    gates = jnp.concatenate(gs, axis=0)
    e_ref[...] = jnp.concatenate(es, axis=0).astype(I32)
    g_ref[...] = gates / jnp.sum(gates, axis=0, keepdims=True) * ROUTED_SCALE

    @pl.when(pl.program_id(0) == 0)
    def _():
        cnt_ref[...] = jnp.zeros_like(cnt_ref)
    cnt_ref[...] += jnp.sum(picked, axis=1, keepdims=True)


def _router(h2, wrt, bias_col):
    n, d = h2.shape
    tt = TT_ROUTE
    return pl.pallas_call(
        _router_kernel,
        grid=(n // tt,),
        in_specs=[pl.BlockSpec((tt, d), lambda i: (i, 0)),
                  pl.BlockSpec((N_EXPERTS, d), lambda i: (0, 0)),
                  pl.BlockSpec((N_EXPERTS, 1), lambda i: (0, 0))],
        out_specs=[pl.BlockSpec((TOP_K, tt), lambda i: (0, i)),
                   pl.BlockSpec((TOP_K, tt), lambda i: (0, i)),
                   pl.BlockSpec((N_EXPERTS, 1), lambda i: (0, 0))],
        out_shape=[jax.ShapeDtypeStruct((TOP_K, n), I32),
                   jax.ShapeDtypeStruct((TOP_K, n), F32),
                   jax.ShapeDtypeStruct((N_EXPERTS, 1), F32)],
        compiler_params=_params("arbitrary"),
        name="router",
    )(h2, wrt, bias_col)


def _rank_kernel(e_ref, pstart_ref, tri_ref, dest_ref, carry_ref):
    tt = e_ref.shape[1]

    @pl.when(pl.program_id(0) == 0)
    def _():
        carry_ref[...] = pstart_ref[...]

    e = e_ref[...]
    eio = lax.broadcasted_iota(I32, (N_EXPERTS, tt), 0)
    mask = jnp.zeros((N_EXPERTS, tt), F32)
    for k in range(TOP_K):
        mask = jnp.where(eio == e[k:k + 1, :], 1.0, mask)
    incl = jnp.dot(mask.astype(BF16), tri_ref[...], preferred_element_type=F32)
    pos = incl - mask + carry_ref[...]
    dest = [jnp.sum(jnp.where(eio == e[k:k + 1, :], pos, 0.0), axis=0, keepdims=True)
            for k in range(TOP_K)]
    dest_ref[...] = jnp.concatenate(dest, axis=0).astype(I32)
    carry_ref[...] += incl[:, tt - 1:tt]


def _rank(top_e_t, pstart_col, tri):
    n = top_e_t.shape[1]
    tt = TT_ROUTE
    return pl.pallas_call(
        _rank_kernel,
        grid=(n // tt,),
        in_specs=[pl.BlockSpec((TOP_K, tt), lambda i: (0, i)),
                  pl.BlockSpec((N_EXPERTS, 1), lambda i: (0, 0)),
                  pl.BlockSpec((tt, tt), lambda i: (0, 0))],
        out_specs=pl.BlockSpec((TOP_K, tt), lambda i: (0, i)),
        out_shape=jax.ShapeDtypeStruct((TOP_K, n), I32),
        scratch_shapes=[pltpu.VMEM((N_EXPERTS, 1), F32)],
        compiler_params=_params("arbitrary"),
        name="rank",
    )(top_e_t, pstart_col, tri)


def _row_copy(src_ref, src_row, dst_ref, dst_row, sem):
    return pltpu.make_async_copy(src_ref.at[pl.ds(src_row, 1)], dst_ref.at[pl.ds(dst_row, 1)], sem)


def _dispatch_kernel(dest_ref, h_ref, xs_in_ref, xs_ref, sem):
    del xs_in_ref
    tt = h_ref.shape[0]

    def issue(t, carry):
        for k in range(TOP_K):
            _row_copy(h_ref, t, xs_ref, dest_ref[k, t], sem).start()
        return carry

    def drain(t, carry):
        for k in range(TOP_K):
            _row_copy(h_ref, t, xs_ref, dest_ref[k, t], sem).wait()
        return carry

    lax.fori_loop(0, tt, issue, 0)
    lax.fori_loop(0, tt, drain, 0)


def _dispatch(dest_t, h2, rows):
    n, d = h2.shape
    tt = TT_DISPATCH
    xs0 = jnp.zeros((rows, d), F32)
    return pl.pallas_call(
        _dispatch_kernel,
        grid=(n // tt,),
        in_specs=[pl.BlockSpec((TOP_K, tt), lambda i: (0, i), memory_space=pltpu.SMEM),
                  pl.BlockSpec((tt, d), lambda i: (i, 0)),
                  pl.BlockSpec(memory_space=pl.ANY)],
        out_specs=pl.BlockSpec(memory_space=pl.ANY),
        out_shape=jax.ShapeDtypeStruct((rows, d), F32),
        scratch_shapes=[pltpu.SemaphoreType.DMA],
        input_output_aliases={2: 0},
        compiler_params=_params("arbitrary"),
        name="dispatch",
    )(dest_t, h2, xs0)


def _experts_kernel(be_ref, nv_ref, xs_ref, wg_ref, wu_ref, wd_ref, ys_ref, wgb, wub, wdb):
    j = pl.program_id(0)

    @pl.when(j < nv_ref[0])
    def _():
        fresh = jnp.logical_or(j == 0, be_ref[j] != be_ref[jnp.maximum(j - 1, 0)])

        @pl.when(fresh)
        def _():
            wgb[...] = wg_ref[...].astype(BF16)
            wub[...] = wu_ref[...].astype(BF16)
            wdb[...] = wd_ref[...].astype(BF16)

        xb = xs_ref[...].astype(BF16)
        act = (_silu(jnp.dot(xb, wgb[...], preferred_element_type=F32))
               * jnp.dot(xb, wub[...], preferred_element_type=F32))
        ys_ref[...] = jnp.dot(act.astype(BF16), wdb[...], preferred_element_type=F32)


def _experts(block_e, n_valid, xs, wg, wu, wd):
    rows, d = xs.shape
    bm = EXPERT_ROWS
    f = wg.shape[2]
    row_blk = lambda j, be, nv: (jnp.minimum(j, nv[0] - 1), 0)
    w_blk = lambda j, be, nv: (be[j], 0, 0)
    return pl.pallas_call(
        _experts_kernel,
        grid_spec=pltpu.PrefetchScalarGridSpec(
            num_scalar_prefetch=2,
            grid=(rows // bm,),
            in_specs=[pl.BlockSpec((bm, d), row_blk),
                      pl.BlockSpec((None, d, f), w_blk),
                      pl.BlockSpec((None, d, f), w_blk),
                      pl.BlockSpec((None, f, d), w_blk)],
            out_specs=pl.BlockSpec((bm, d), row_blk),
            scratch_shapes=[pltpu.VMEM((d, f), BF16), pltpu.VMEM((d, f), BF16),
                            pltpu.VMEM((f, d), BF16)]),
        out_shape=jax.ShapeDtypeStruct((rows, d), F32),
        compiler_params=_params("arbitrary"),
        name="experts",
    )(block_e, n_valid, xs, wg, wu, wd)


def _combine_kernel(dest_ref, gates_ref, base_ref, mod_ref, ys_ref, out_ref, buf, sem):
    tt = base_ref.shape[0]

    def issue(t, carry):
        for k in range(TOP_K):
            _row_copy(ys_ref, dest_ref[k, t], buf.at[k], t, sem).start()
        return carry

    def drain(t, carry):
        for k in range(TOP_K):
            _row_copy(ys_ref, dest_ref[k, t], buf.at[k], t, sem).wait()
        return carry

    lax.fori_loop(0, tt, issue, 0)
    lax.fori_loop(0, tt, drain, 0)
    gates = gates_ref[...]
    routed = gates[:, 0:1] * buf[0]
    for k in range(1, TOP_K):
        routed = routed + gates[:, k:k + 1] * buf[k]
    out_ref[...] = base_ref[...] + mod_ref[5:6, :] * routed


def _combine(dest_t, gates, base, mod, ys, seq):
    n, d = base.shape
    tt = TT_COMBINE
    tiles_per_seq = seq // tt
    return pl.pallas_call(
        _combine_kernel,
        grid=(n // tt,),
        in_specs=[pl.BlockSpec((TOP_K, tt), lambda i: (0, i), memory_space=pltpu.SMEM),
                  pl.BlockSpec((tt, TOP_K), lambda i: (i, 0)),
                  pl.BlockSpec((tt, d), lambda i: (i, 0)),
                  pl.BlockSpec((None, 6, d), lambda i: (i // tiles_per_seq, 0, 0)),
                  pl.BlockSpec(memory_space=pl.ANY)],
        out_specs=pl.BlockSpec((tt, d), lambda i: (i, 0)),
        out_shape=jax.ShapeDtypeStruct((n, d), F32),
        scratch_shapes=[pltpu.VMEM((TOP_K, tt, d), F32), pltpu.SemaphoreType.DMA],
        compiler_params=_params("arbitrary"),
        name="combine",
    )(dest_t, gates, base, mod, ys)


def _layer(x, mod, pos, rope, p):
    nbatch, seq, d = x.shape
    n = nbatch * seq
    xf = x.reshape(n, d)
    invf, bd = rope

    perm = np.concatenate([np.arange(h * HEAD_DIM, (h + 1) * HEAD_DIM) for h in PAIR_ORDER_A])
    w_in = p["w_in"]
    w_in_p = jnp.concatenate([w_in[:, :MIX_A][:, perm], w_in[:, MIX_A:]], axis=1).astype(BF16)
    ones = lambda w: jnp.ones((w,), F32)
    qscale = HEAD_DIM ** -0.5
    gcol = jnp.concatenate([jnp.tile(p["g_q_a"], N_HEADS_A) * qscale, jnp.tile(p["g_k_a"], N_KV_A),
                            ones(KV_A), jnp.tile(p["g_q_b"], N_HEADS_B) * qscale,
                            jnp.tile(p["g_k_b"], N_HEADS_B), ones(MIX_B)]).reshape(1, IN_WIDTH)
    qa, ka, va, qb, kb, vb = _inproj(xf, mod, pos, p["g_norm_mix"].reshape(1, d), w_in_p, gcol,
                                     invf, bd, seq)

    sinks_p = p["sinks_a"][np.array(PAIR_ORDER_A)]
    oa, _ = _attention(qa, ka, va, nbatch=nbatch, seq=seq, dil=1, max_dist=WINDOW_A - 1,
                       kv_shared=True, sinks=sinks_p, want_lse=False)
    obs, lses = [], []
    for window, dil in DILATED_BRANCHES:
        o, lse = _attention(qb, kb, vb, nbatch=nbatch, seq=seq, dil=dil, max_dist=window // dil,
                            kv_shared=False, want_lse=True)
        obs.append(o)
        lses.append(lse)

    goa = p["g_out_a"][perm].reshape(1, MIX_A)
    w_out = p["w_out"]
    wo_p = jnp.concatenate([w_out[:MIX_A][perm], w_out[MIX_A:]], axis=0).astype(BF16)
    h2, base = _outproj(xf, mod, oa, obs, lses, goa, p["g_out_b"].reshape(1, MIX_B), wo_p,
                        p["g_norm_ffn"].reshape(1, d), p["w_gate_s"].astype(BF16),
                        p["w_up_s"].astype(BF16), p["w_down_s"].astype(BF16), seq)

    top_e_t, gates_t, counts = _router(h2, p["w_router"].T.astype(BF16),
                                       p["router_bias"].reshape(N_EXPERTS, 1))
    bm = EXPERT_ROWS
    counts = counts.reshape(N_EXPERTS).astype(I32)
    padded = (counts + bm - 1) // bm * bm
    pends = jnp.cumsum(padded)
    pstarts = pends - padded
    rows = n * TOP_K + N_EXPERTS * bm
    n_blocks = rows // bm
    n_valid = (pends[-1] // bm).astype(I32)
    blk = jnp.minimum(jnp.arange(n_blocks, dtype=I32), n_valid - 1)
    block_e = jnp.minimum(jnp.searchsorted(pends, blk * bm, side="right"), N_EXPERTS - 1).astype(I32)

    tri = (np.arange(TT_ROUTE)[:, None] <= np.arange(TT_ROUTE)[None, :])
    dest_t = _rank(top_e_t, pstarts.astype(F32).reshape(N_EXPERTS, 1), jnp.asarray(tri, BF16))
    xs = _dispatch(dest_t, h2, rows)
    ys = _experts(block_e, n_valid.reshape(1), xs, p["w_gate_e"], p["w_up_e"], p["w_down_e"])
    out = _combine(dest_t, gates_t.T, base, mod, ys, seq)
    return out.reshape(nbatch, seq, d)


def kernel(x, c, positions, w_ada, b_ada, g_norm_mix, w_in, g_q_a, g_k_a, sinks_a, g_q_b, g_k_b,
           g_out_a, g_out_b, w_out, g_norm_ffn, w_router, router_bias, w_gate_e, w_up_e, w_down_e,
           w_gate_s, w_up_s, w_down_s):
    nbatch, seq, d = x.shape
    depth = w_ada.shape[0]
    params = dict(g_norm_mix=g_norm_mix, w_in=w_in, g_q_a=g_q_a, g_k_a=g_k_a, sinks_a=sinks_a,
                  g_q_b=g_q_b, g_k_b=g_k_b, g_out_a=g_out_a, g_out_b=g_out_b, w_out=w_out,
                  g_norm_ffn=g_norm_ffn, w_router=w_router, router_bias=router_bias,
                  w_gate_e=w_gate_e, w_up_e=w_up_e, w_down_e=w_down_e, w_gate_s=w_gate_s,
                  w_up_s=w_up_s, w_down_s=w_down_s)
    j = np.arange(LANES) % HEAD_DIM
    inv = ROPE_THETA ** (-jnp.arange(0, ROT_DIM, 2, dtype=F32) / ROT_DIM)
    invf = jnp.where(j < ROT_DIM, inv[j % (ROT_DIM // 2)], 0.0).astype(F32).reshape(1, LANES)
    bd = jnp.asarray((np.arange(LANES)[:, None] // HEAD_DIM) == (np.arange(LANES)[None, :] // HEAD_DIM),
                     BF16)
    pos = positions.reshape(nbatch * seq, 1).astype(I32)
    for l in range(depth):
        mod = _adaln(c.astype(F32), w_ada[l], b_ada[l]).reshape(nbatch, 6, d)
        x = _layer(x, mod, pos, (invf, bd), {k: v[l] for k, v in params.items()})
    return x
```

```python
import functools

import numpy as np
import jax
import jax.numpy as jnp
from jax import lax
from jax.experimental import pallas as pl
from jax.experimental.pallas import tpu as pltpu

F32 = jnp.float32
BF16 = jnp.bfloat16
I32 = jnp.int32

HEAD_DIM = 64
N_HEADS_A = 8
N_KV_A = 2
WINDOW_A = 128
N_HEADS_B = 8
DILATED_BRANCHES = ((128, 1), (512, 4), (2048, 16))
DILS = tuple(dil for _, dil in DILATED_BRANCHES if dil > 1)
BLOCK = 128
ROT_DIM = HEAD_DIM // 4
ROPE_THETA = 500000.0
MIX_A = N_HEADS_A * HEAD_DIM
KV_A = N_KV_A * HEAD_DIM
MIX_B = N_HEADS_B * HEAD_DIM
N_EXPERTS = 256
TOP_K = 8
N_GROUPS = 8
TOPK_GROUPS = 4
GROUP_SIZE = N_EXPERTS // N_GROUPS
ROUTED_SCALE = 2.5
EPS = 1e-6

LANES = 128
HEADS_PER_VREG = LANES // HEAD_DIM
N_PAIRS = MIX_A // LANES
NEG = -1e30
VMEM_LIMIT = 48 * 1024 * 1024

TT_PROJ = 512
TT_ROUTE = 256
TT_DISPATCH = 256
TT_COMBINE = 128
EXPERT_ROWS = 256

PAIR_ORDER_A = tuple(h for p in range(N_PAIRS) for h in (p, p + N_HEADS_A // N_KV_A))


def _params(*sem):
    return pltpu.CompilerParams(dimension_semantics=sem, vmem_limit_bytes=VMEM_LIMIT)


def _silu(t):
    return t / (1.0 + jnp.exp(-t))


def _rms_rows(t):
    return t * lax.rsqrt(jnp.mean(t * t, axis=-1, keepdims=True) + EPS)


def _ada_kernel(c_ref, w_ref, b_ref, o_ref):
    cond = _silu(c_ref[...])
    o_ref[...] = jnp.dot(cond.astype(BF16), w_ref[...].astype(BF16),
                         preferred_element_type=F32) + b_ref[...]


def _adaln(c, w_ada, b_ada):
    nb, d = c.shape
    width = w_ada.shape[1]
    tn = 1024
    return pl.pallas_call(
        _ada_kernel,
        grid=(width // tn,),
        in_specs=[pl.BlockSpec((nb, d), lambda j: (0, 0)),
                  pl.BlockSpec((d, tn), lambda j: (0, j)),
                  pl.BlockSpec((1, tn), lambda j: (0, j))],
        out_specs=pl.BlockSpec((nb, tn), lambda j: (0, j)),
        out_shape=jax.ShapeDtypeStruct((nb, width), F32),
        compiler_params=_params("arbitrary"),
        name="adaln",
    )(c, w_ada, b_ada.reshape(1, width))


COL_QA, COL_KA, COL_VA = 0, MIX_A, MIX_A + KV_A
COL_QB = MIX_A + 2 * KV_A
COL_KB, COL_VB = COL_QB + MIX_B, COL_QB + 2 * MIX_B
IN_WIDTH = COL_VB + MIX_B


def _inproj_kernel(x_ref, mod_ref, pos_ref, gn_ref, w_ref, gcol_ref, invf_ref, bd_ref,
                   qa_ref, ka_ref, va_ref, qb_ref, kb_ref, vb_ref, *rest):
    n_dil = len(DILS)
    dil_refs = [rest[i * n_dil:(i + 1) * n_dil] for i in range(3)]
    qb_scr, kb_scr, vb_scr = rest[3 * n_dil:]
    tt = x_ref.shape[0]
    shift, scale = mod_ref[0:1, :], mod_ref[1:2, :]
    h = _rms_rows(x_ref[...]) * gn_ref[...] * (1.0 + scale) + shift
    proj = jnp.dot(h.astype(BF16), w_ref[...], preferred_element_type=F32)

    ang = pos_ref[...].astype(F32) * invf_ref[...]
    cs, sn = jnp.cos(ang), jnp.sin(ang)
    lane = lax.broadcasted_iota(I32, (1, LANES), 1) % HEAD_DIM
    s_lo = jnp.where(lane < ROT_DIM // 2, -sn, 0.0)
    s_hi = jnp.where((lane >= ROT_DIM // 2) & (lane < ROT_DIM), sn, 0.0)
    bd = bd_ref[...]

    def norm_rope(col0, width, out_ref, scr=None):
        for j in range(width // LANES):
            c = col0 + j * LANES
            t = proj[:, c:c + LANES]
            sq = t * t
            hi = sq.astype(BF16)
            lo = (sq - hi.astype(F32)).astype(BF16)
            ss = (jnp.dot(hi, bd, preferred_element_type=F32)
                  + jnp.dot(lo, bd, preferred_element_type=F32))
            t = t * lax.rsqrt(ss * (1.0 / HEAD_DIM) + EPS) * gcol_ref[:, c:c + LANES]
            t = (t * cs + pltpu.roll(t, LANES - ROT_DIM // 2, 1) * s_lo
                 + pltpu.roll(t, ROT_DIM // 2, 1) * s_hi)
            out_ref[:, j * LANES:(j + 1) * LANES] = t.astype(BF16)
            if scr is not None:
                scr[j] = t

    norm_rope(COL_QA, MIX_A, qa_ref)
    norm_rope(COL_KA, KV_A, ka_ref)
    norm_rope(COL_QB, MIX_B, qb_ref, qb_scr)
    norm_rope(COL_KB, MIX_B, kb_ref, kb_scr)
    va_ref[...] = proj[:, COL_VA:COL_VA + KV_A].astype(BF16)
    vb_ref[...] = proj[:, COL_VB:COL_VB + MIX_B].astype(BF16)
    for j in range(N_PAIRS):
        vb_scr[j] = proj[:, COL_VB + j * LANES:COL_VB + (j + 1) * LANES]
    for scr, outs in zip((qb_scr, kb_scr, vb_scr), dil_refs):
        for dil, out in zip(DILS, outs):
            for r in range(dil):
                for j in range(N_PAIRS):
                    c = r * MIX_B + j * LANES
                    out[:, c:c + LANES] = scr[j, pl.ds(r, tt // dil, stride=dil), :].astype(BF16)


def _inproj(xf, mod, pos, g_norm, w_in_p, gcol, invf, bd, seq):
    n, d = xf.shape
    tt = TT_PROJ
    tiles_per_seq = seq // tt
    shapes = [(n, w) for w in (MIX_A, KV_A, KV_A, MIX_B, MIX_B, MIX_B)]
    shapes += [(n // dil, dil * MIX_B) for _ in range(3) for dil in DILS]
    full = lambda shape: pl.BlockSpec(shape, lambda i: (0,) * len(shape))
    return pl.pallas_call(
        _inproj_kernel,
        grid=(n // tt,),
        in_specs=[pl.BlockSpec((tt, d), lambda i: (i, 0)),
                  pl.BlockSpec((None, 6, d), lambda i: (i // tiles_per_seq, 0, 0)),
                  pl.BlockSpec((tt, 1), lambda i: (i, 0)),
                  full((1, d)), full((d, IN_WIDTH)), full((1, IN_WIDTH)),
                  full((1, LANES)), full((LANES, LANES))],
        out_specs=[pl.BlockSpec((tt * r // n, w), lambda i: (i, 0)) for r, w in shapes],
        out_shape=[jax.ShapeDtypeStruct(s, BF16) for s in shapes],
        scratch_shapes=[pltpu.VMEM((N_PAIRS, tt, LANES), F32)] * 3,
        compiler_params=_params("parallel"),
        name="inproj",
    )(xf, mod, pos, g_norm, w_in_p, gcol, invf, bd)


def _attn_kernel(*refs, kv_shared, max_dist, use_prev, has_sinks, want_lse):
    refs = list(refs)
    sink_ref = refs.pop(0) if has_sinks else None
    q_ref = refs.pop(0)
    kp_ref = refs.pop(0) if use_prev else None
    kc_ref = refs.pop(0)
    vp_ref = refs.pop(0) if use_prev else None
    vc_ref = refs.pop(0)
    o_ref = refs.pop(0)
    lse_ref = refs.pop(0) if want_lse else None

    blk = pl.program_id(2)
    nq = 2 * BLOCK
    nk = 2 * BLOCK if use_prev else BLOCK
    qpos = lax.broadcasted_iota(I32, (nq, nk), 0) % BLOCK
    kpos = lax.broadcasted_iota(I32, (nq, nk), 1)
    if use_prev:
        dist = qpos + BLOCK - kpos
        valid = (dist >= 0) & (dist <= max_dist) & ((kpos >= BLOCK) | (blk > 0))
    else:
        dist = qpos - kpos
        valid = (dist >= 0) & (dist <= max_dist)
    lane = lax.broadcasted_iota(I32, (nq, LANES), 1)
    row = lax.broadcasted_iota(I32, (nq, LANES), 0)
    own_half = (lane < HEAD_DIM) == (row < BLOCK)
    left_lanes = lax.broadcasted_iota(I32, (BLOCK, LANES), 1) < HEAD_DIM
    lane8 = lax.broadcasted_iota(I32, (BLOCK, 2 * N_PAIRS), 1)
    lse_blk = jnp.zeros((BLOCK, 2 * N_PAIRS), F32)

    for p in range(N_PAIRS):
        cq = slice(p * LANES, (p + 1) * LANES)
        ck = slice(0, LANES) if kv_shared else cq
        qp = q_ref[:, cq]
        qs = jnp.concatenate([qp, qp], axis=0)
        qs = jnp.where(own_half, qs, jnp.zeros_like(qs))
        if use_prev:
            k = jnp.concatenate([kp_ref[:, ck], kc_ref[:, ck]], axis=0)
            v = jnp.concatenate([vp_ref[:, ck], vc_ref[:, ck]], axis=0)
        else:
            k, v = kc_ref[:, ck], vc_ref[:, ck]
        s = lax.dot_general(qs, k, (((1,), (1,)), ((), ())), preferred_element_type=F32)
        s = jnp.where(valid, s, NEG)
        m = jnp.max(s, axis=-1, keepdims=True)
        if has_sinks:
            rows1 = lax.broadcasted_iota(I32, (nq, 1), 0)
            sink = jnp.where(rows1 < BLOCK, sink_ref[2 * p], sink_ref[2 * p + 1])
            m = jnp.maximum(m, sink)
        e = jnp.exp(s - m)
        l = jnp.sum(e, axis=-1, keepdims=True)
        if has_sinks:
            l = l + jnp.exp(sink - m)
        o = jnp.dot(e.astype(BF16), v, preferred_element_type=F32) / l
        o_ref[:, cq] = jnp.where(left_lanes, o[:BLOCK], o[BLOCK:]).astype(BF16)
        if want_lse:
            lse = m + jnp.log(l)
            lse_blk = (lse_blk + jnp.where(lane8 == 2 * p, lse[:BLOCK], 0.0)
                       + jnp.where(lane8 == 2 * p + 1, lse[BLOCK:], 0.0))
    if want_lse:
        lse_ref[...] = lse_blk


def _attention(q, k, v, *, nbatch, seq, dil, max_dist, kv_shared, sinks=None, want_lse):
    length = seq // dil
    nblk = length // BLOCK
    use_prev = nblk > 1
    kw = k.shape[1] // dil
    view = lambda t: t.reshape(nbatch, length, t.shape[1])
    cur = lambda b, r, i: (b, i, r)
    prev = lambda b, r, i: (b, jnp.maximum(i - 1, 0), r)
    in_specs, args = [], []
    if sinks is not None:
        in_specs.append(pl.BlockSpec(memory_space=pltpu.SMEM))
        args.append(sinks)
    in_specs.append(pl.BlockSpec((None, BLOCK, MIX_B), cur))
    args.append(view(q))
    for t in (k, v):
        if use_prev:
            in_specs.append(pl.BlockSpec((None, BLOCK, kw), prev))
            args.append(view(t))
        in_specs.append(pl.BlockSpec((None, BLOCK, kw), cur))
        args.append(view(t))
    out_specs = [pl.BlockSpec((None, BLOCK, MIX_B), cur)]
    out_shape = [jax.ShapeDtypeStruct((nbatch, length, dil * MIX_B), BF16)]
    if want_lse:
        out_specs.append(pl.BlockSpec((None, None, BLOCK, N_HEADS_B), lambda b, r, i: (b, r, i, 0)))
        out_shape.append(jax.ShapeDtypeStruct((nbatch, dil, length, N_HEADS_B), F32))
    outs = pl.pallas_call(
        functools.partial(_attn_kernel, kv_shared=kv_shared, max_dist=max_dist, use_prev=use_prev,
                          has_sinks=sinks is not None, want_lse=want_lse),
        grid=(nbatch, dil, nblk),
        in_specs=in_specs, out_specs=out_specs, out_shape=out_shape,
        compiler_params=_params("parallel", "parallel", "arbitrary"),
        name=f"attn_d{dil}" + ("_swa" if kv_shared else ""),
    )(*args)
    o = outs[0].reshape(nbatch * length, dil * MIX_B)
    if not want_lse:
        return o, None
    lse = outs[1].transpose(0, 2, 1, 3).reshape(nbatch * seq, N_HEADS_B)
    return o, lse


def _expand_heads(w, width):
    head = lax.broadcasted_iota(I32, (1, width), 1) // HEAD_DIM
    out = jnp.zeros((w.shape[0], width), F32)
    for hd in range(w.shape[1]):
        out = jnp.where(head == hd, w[:, hd:hd + 1], out)
    return out


def _outproj_kernel(x_ref, mod_ref, oa_ref, ob1_ref, ob2_ref, ob3_ref, l1_ref, l2_ref, l3_ref,
                    goa_ref, gob_ref, wo_ref, gf_ref, wgs_ref, wus_ref, wds_ref,
                    h2_ref, base_ref, ob_scr):
    tt = x_ref.shape[0]
    gate_a = mod_ref[2:3, :]
    shift_m, scale_m, gate_m = mod_ref[3:4, :], mod_ref[4:5, :], mod_ref[5:6, :]

    def token_major(ref, dil):
        if dil == 1:
            return ref[...].astype(F32)
        for r in range(dil):
            for j in range(N_PAIRS):
                c = r * MIX_B + j * LANES
                ob_scr[j, pl.ds(r, tt // dil, stride=dil), :] = ref[:, c:c + LANES].astype(F32)
        return jnp.concatenate([ob_scr[j] for j in range(N_PAIRS)], axis=1)

    l1, l2, l3 = l1_ref[...], l2_ref[...], l3_ref[...]
    mx = jnp.maximum(jnp.maximum(l1, l2), l3)
    e1, e2, e3 = jnp.exp(l1 - mx), jnp.exp(l2 - mx), jnp.exp(l3 - mx)
    den = e1 + e2 + e3
    dils = [dil for _, dil in DILATED_BRANCHES]
    ob = _expand_heads(e1 / den, MIX_B) * token_major(ob1_ref, dils[0])
    ob = ob + _expand_heads(e2 / den, MIX_B) * token_major(ob2_ref, dils[1])
    ob = ob + _expand_heads(e3 / den, MIX_B) * token_major(ob3_ref, dils[2])
    ob = _rms_rows(ob) * gob_ref[...]
    oa = _rms_rows(oa_ref[...].astype(F32)) * goa_ref[...]
    y = (jnp.dot(oa.astype(BF16), wo_ref[0:MIX_A, :], preferred_element_type=F32)
         + jnp.dot(ob.astype(BF16), wo_ref[MIX_A:MIX_A + MIX_B, :], preferred_element_type=F32))
    x1 = x_ref[...] + gate_a * y
    h2 = _rms_rows(x1) * gf_ref[...] * (1.0 + scale_m) + shift_m
    h2_ref[...] = h2
    hb = h2.astype(BF16)
    act = (_silu(jnp.dot(hb, wgs_ref[...], preferred_element_type=F32))
           * jnp.dot(hb, wus_ref[...], preferred_element_type=F32))
    shared = jnp.dot(act.astype(BF16), wds_ref[...], preferred_element_type=F32)
    base_ref[...] = x1 + gate_m * shared


def _outproj(xf, mod, oa, obs, lses, goa, gob, wo_p, gf, wgs, wus, wds, seq):
    n, d = xf.shape
    tt = TT_PROJ // 2
    tiles_per_seq = seq // tt
    tile = lambda w: pl.BlockSpec((tt, w), lambda i: (i, 0))
    full = lambda shape: pl.BlockSpec(shape, lambda i: (0,) * len(shape))
    sd = wgs.shape[1]
    dilated = [pl.BlockSpec((tt // dil, dil * MIX_B), lambda i: (i, 0)) for _, dil in DILATED_BRANCHES]
    return pl.pallas_call(
        _outproj_kernel,
        grid=(n // tt,),
        in_specs=[tile(d), pl.BlockSpec((None, 6, d), lambda i: (i // tiles_per_seq, 0, 0)),
                  tile(MIX_A), *dilated,
                  tile(N_HEADS_B), tile(N_HEADS_B), tile(N_HEADS_B),
                  full((1, MIX_A)), full((1, MIX_B)), full((MIX_A + MIX_B, d)), full((1, d)),
                  full((d, sd)), full((d, sd)), full((sd, d))],
        out_specs=[tile(d), tile(d)],
        out_shape=[jax.ShapeDtypeStruct((n, d), F32), jax.ShapeDtypeStruct((n, d), F32)],
        scratch_shapes=[pltpu.VMEM((N_PAIRS, tt, LANES), F32)],
        compiler_params=_params("parallel"),
        name="outproj",
    )(xf, mod, oa, *obs, *lses, goa, gob, wo_p, gf, wgs, wus, wds)


def _router_kernel(h_ref, wrt_ref, bias_ref, e_ref, g_ref, cnt_ref):
    tt = h_ref.shape[0]
    logits = lax.dot_general(wrt_ref[...], h_ref[...].astype(BF16), (((1,), (1,)), ((), ())),
                             preferred_element_type=F32)
    scores = 1.0 / (1.0 + jnp.exp(-logits))
    biased = scores + bias_ref[...]
    ninf = -jnp.inf

    j32 = lax.broadcasted_iota(I32, (GROUP_SIZE, tt), 0).astype(F32)
    grp = []
    for g in range(N_GROUPS):
        bg = biased[g * GROUP_SIZE:(g + 1) * GROUP_SIZE, :]
        m1 = jnp.max(bg, axis=0, keepdims=True)
        i1 = jnp.min(jnp.where(bg == m1, j32, float(GROUP_SIZE)), axis=0, keepdims=True)
        m2 = jnp.max(jnp.where(j32 == i1, ninf, bg), axis=0, keepdims=True)
        grp.append(m1 + m2)
    grp = jnp.concatenate(grp, axis=0)
    g8 = lax.broadcasted_iota(I32, (N_GROUPS, tt), 0).astype(F32)
    chosen = jnp.zeros((N_GROUPS, tt), F32)
    for _ in range(TOPK_GROUPS):
        gm = jnp.max(grp, axis=0, keepdims=True)
        gi = jnp.min(jnp.where(grp == gm, g8, float(N_GROUPS)), axis=0, keepdims=True)
        hit = g8 == gi
        chosen = jnp.where(hit, 1.0, chosen)
        grp = jnp.where(hit, ninf, grp)
    masked = jnp.concatenate(
        [jnp.where(chosen[g:g + 1, :] > 0.0, biased[g * GROUP_SIZE:(g + 1) * GROUP_SIZE, :], ninf)
         for g in range(N_GROUPS)], axis=0)

    eio = lax.broadcasted_iota(I32, (N_EXPERTS, tt), 0).astype(F32)
    picked = jnp.zeros((N_EXPERTS, tt), F32)
    es, gs = [], []
    for _ in range(TOP_K):
        m = jnp.max(masked, axis=0, keepdims=True)
        idx = jnp.min(jnp.where(masked == m, eio, float(N_EXPERTS)), axis=0, keepdims=True)
        hit = eio == idx
        gs.append(jnp.sum(jnp.where(hit, scores, 0.0), axis=0, keepdims=True))
        es.append(idx)
        picked = jnp.where(hit, 1.0, picked)
        masked = jnp.where(hit, ninf, masked)
    gates = jnp.concatenate(gs, axis=0)
    e_ref[...] = jnp.concatenate(es, axis=0).astype(I32)
    g_ref[...] = gates / jnp.sum(gates, axis=0, keepdims=True) * ROUTED_SCALE

    @pl.when(pl.program_id(0) == 0)
    def _():
        cnt_ref[...] = jnp.zeros_like(cnt_ref)
    cnt_ref[...] += jnp.sum(picked, axis=1, keepdims=True)


def _router(h2, wrt, bias_col):
    n, d = h2.shape
    tt = TT_ROUTE
    return pl.pallas_call(
        _router_kernel,
        grid=(n // tt,),
        in_specs=[pl.BlockSpec((tt, d), lambda i: (i, 0)),
                  pl.BlockSpec((N_EXPERTS, d), lambda i: (0, 0)),
                  pl.BlockSpec((N_EXPERTS, 1), lambda i: (0, 0))],
        out_specs=[pl.BlockSpec((TOP_K, tt), lambda i: (0, i)),
                   pl.BlockSpec((TOP_K, tt), lambda i: (0, i)),
                   pl.BlockSpec((N_EXPERTS, 1), lambda i: (0, 0))],
        out_shape=[jax.ShapeDtypeStruct((TOP_K, n), I32),
                   jax.ShapeDtypeStruct((TOP_K, n), F32),
                   jax.ShapeDtypeStruct((N_EXPERTS, 1), F32)],
        compiler_params=_params("arbitrary"),
        name="router",
    )(h2, wrt, bias_col)


def _rank_kernel(e_ref, pstart_ref, tri_ref, dest_ref, carry_ref):
    tt = e_ref.shape[1]

    @pl.when(pl.program_id(0) == 0)
    def _():
        carry_ref[...] = pstart_ref[...]

    e = e_ref[...]
    eio = lax.broadcasted_iota(I32, (N_EXPERTS, tt), 0)
    mask = jnp.zeros((N_EXPERTS, tt), F32)
    for k in range(TOP_K):
        mask = jnp.where(eio == e[k:k + 1, :], 1.0, mask)
    incl = jnp.dot(mask.astype(BF16), tri_ref[...], preferred_element_type=F32)
    pos = incl - mask + carry_ref[...]
    dest = [jnp.sum(jnp.where(eio == e[k:k + 1, :], pos, 0.0), axis=0, keepdims=True)
            for k in range(TOP_K)]
    dest_ref[...] = jnp.concatenate(dest, axis=0).astype(I32)
    carry_ref[...] += incl[:, tt - 1:tt]


def _rank(top_e_t, pstart_col, tri):
    n = top_e_t.shape[1]
    tt = TT_ROUTE
    return pl.pallas_call(
        _rank_kernel,
        grid=(n // tt,),
        in_specs=[pl.BlockSpec((TOP_K, tt), lambda i: (0, i)),
                  pl.BlockSpec((N_EXPERTS, 1), lambda i: (0, 0)),
                  pl.BlockSpec((tt, tt), lambda i: (0, 0))],
        out_specs=pl.BlockSpec((TOP_K, tt), lambda i: (0, i)),
        out_shape=jax.ShapeDtypeStruct((TOP_K, n), I32),
        scratch_shapes=[pltpu.VMEM((N_EXPERTS, 1), F32)],
        compiler_params=_params("arbitrary"),
        name="rank",
    )(top_e_t, pstart_col, tri)


def _row_copy(src_ref, src_row, dst_ref, dst_row, sem):
    return pltpu.make_async_copy(src_ref.at[pl.ds(src_row, 1)], dst_ref.at[pl.ds(dst_row, 1)], sem)


def _dispatch_kernel(dest_ref, h_ref, xs_in_ref, xs_ref, sem):
    del xs_in_ref
    tt = h_ref.shape[0]

    def issue(t, carry):
        for k in range(TOP_K):
            _row_copy(h_ref, t, xs_ref, dest_ref[k, t], sem).start()
        return carry

    def drain(t, carry):
        for k in range(TOP_K):
            _row_copy(h_ref, t, xs_ref, dest_ref[k, t], sem).wait()
        return carry

    lax.fori_loop(0, tt, issue, 0)
    lax.fori_loop(0, tt, drain, 0)


SUBLANES = 8
PAD_CHUNKS = tuple(SUBLANES << b for b in range((EXPERT_ROWS // SUBLANES - 1).bit_length()))


def _padzero_kernel(off_ref, pad_ref, xs_ref, zeros, sem):
    zeros[...] = jnp.zeros_like(zeros)

    def for_chunks(e, act):
        off, pad = off_ref[e], pad_ref[e]
        head = jnp.minimum((-off) & (SUBLANES - 1), pad)
        for i in range(SUBLANES - 1):
            @pl.when(i < head)
            def _():
                act(_row_copy(zeros, 0, xs_ref, off + i, sem))
        body, rem = off + head, pad - head
        for size in PAD_CHUNKS:
            @pl.when((rem & size) != 0)
            def _():
                start = pl.multiple_of(body + (rem & (size - 1)), SUBLANES)
                act(pltpu.make_async_copy(zeros.at[pl.ds(0, size)], xs_ref.at[pl.ds(start, size)], sem))

    def issue(e, carry):
        for_chunks(e, lambda cp: cp.start())
        return carry

    def drain(e, carry):
        for_chunks(e, lambda cp: cp.wait())
        return carry

    lax.fori_loop(0, N_EXPERTS, issue, 0)
    lax.fori_loop(0, N_EXPERTS, drain, 0)


def _padzero(pad_off, pad_len, rows, d):
    return pl.pallas_call(
        _padzero_kernel,
        grid_spec=pltpu.PrefetchScalarGridSpec(
            num_scalar_prefetch=2, grid=(1,), in_specs=[],
            out_specs=pl.BlockSpec(memory_space=pl.ANY),
            scratch_shapes=[pltpu.VMEM((PAD_CHUNKS[-1], d), F32), pltpu.SemaphoreType.DMA]),
        out_shape=jax.ShapeDtypeStruct((rows, d), F32),
        compiler_params=_params("arbitrary"),
        name="padzero",
    )(pad_off, pad_len)


def _dispatch(dest_t, h2, xs0):
    n, d = h2.shape
    rows = xs0.shape[0]
    tt = TT_DISPATCH
    return pl.pallas_call(
        _dispatch_kernel,
        grid=(n // tt,),
        in_specs=[pl.BlockSpec((TOP_K, tt), lambda i: (0, i), memory_space=pltpu.SMEM),
                  pl.BlockSpec((tt, d), lambda i: (i, 0)),
                  pl.BlockSpec(memory_space=pl.ANY)],
        out_specs=pl.BlockSpec(memory_space=pl.ANY),
        out_shape=jax.ShapeDtypeStruct((rows, d), F32),
        scratch_shapes=[pltpu.SemaphoreType.DMA],
        input_output_aliases={2: 0},
        compiler_params=_params("arbitrary"),
        name="dispatch",
    )(dest_t, h2, xs0)


def _experts_kernel(be_ref, nv_ref, xs_ref, wg_ref, wu_ref, wd_ref, ys_ref, wgb, wub, wdb):
    j = pl.program_id(0)

    @pl.when(j < nv_ref[0])
    def _():
        fresh = jnp.logical_or(j == 0, be_ref[j] != be_ref[jnp.maximum(j - 1, 0)])

        @pl.when(fresh)
        def _():
            wgb[...] = wg_ref[...].astype(BF16)
            wub[...] = wu_ref[...].astype(BF16)
            wdb[...] = wd_ref[...].astype(BF16)

        xb = xs_ref[...].astype(BF16)
        act = (_silu(jnp.dot(xb, wgb[...], preferred_element_type=F32))
               * jnp.dot(xb, wub[...], preferred_element_type=F32))
        ys_ref[...] = jnp.dot(act.astype(BF16), wdb[...], preferred_element_type=F32)


def _experts(block_e, n_valid, xs, wg, wu, wd):
    rows, d = xs.shape
    bm = EXPERT_ROWS
    f = wg.shape[2]
    row_blk = lambda j, be, nv: (jnp.minimum(j, nv[0] - 1), 0)
    w_blk = lambda j, be, nv: (be[j], 0, 0)
    return pl.pallas_call(
        _experts_kernel,
        grid_spec=pltpu.PrefetchScalarGridSpec(
            num_scalar_prefetch=2,
            grid=(rows // bm,),
            in_specs=[pl.BlockSpec((bm, d), row_blk),
                      pl.BlockSpec((None, d, f), w_blk),
                      pl.BlockSpec((None, d, f), w_blk),
                      pl.BlockSpec((None, f, d), w_blk)],
            out_specs=pl.BlockSpec((bm, d), row_blk),
            scratch_shapes=[pltpu.VMEM((d, f), BF16), pltpu.VMEM((d, f), BF16),
                            pltpu.VMEM((f, d), BF16)]),
        out_shape=jax.ShapeDtypeStruct((rows, d), F32),
        compiler_params=_params("arbitrary"),
        name="experts",
    )(block_e, n_valid, xs, wg, wu, wd)


def _combine_kernel(dest_ref, gates_ref, base_ref, mod_ref, ys_ref, out_ref, buf, sem):
    tt = base_ref.shape[0]

    def issue(t, carry):
        for k in range(TOP_K):
            _row_copy(ys_ref, dest_ref[k, t], buf.at[k], t, sem).start()
        return carry

    def drain(t, carry):
        for k in range(TOP_K):
            _row_copy(ys_ref, dest_ref[k, t], buf.at[k], t, sem).wait()
        return carry

    lax.fori_loop(0, tt, issue, 0)
    lax.fori_loop(0, tt, drain, 0)
    gates = gates_ref[...]
    routed = gates[:, 0:1] * buf[0]
    for k in range(1, TOP_K):
        routed = routed + gates[:, k:k + 1] * buf[k]
    out_ref[...] = base_ref[...] + mod_ref[5:6, :] * routed


def _combine(dest_t, gates, base, mod, ys, seq):
    n, d = base.shape
    tt = TT_COMBINE
    tiles_per_seq = seq // tt
    return pl.pallas_call(
        _combine_kernel,
        grid=(n // tt,),
        in_specs=[pl.BlockSpec((TOP_K, tt), lambda i: (0, i), memory_space=pltpu.SMEM),
                  pl.BlockSpec((tt, TOP_K), lambda i: (i, 0)),
                  pl.BlockSpec((tt, d), lambda i: (i, 0)),
                  pl.BlockSpec((None, 6, d), lambda i: (i // tiles_per_seq, 0, 0)),
                  pl.BlockSpec(memory_space=pl.ANY)],
        out_specs=pl.BlockSpec((tt, d), lambda i: (i, 0)),
        out_shape=jax.ShapeDtypeStruct((n, d), F32),
        scratch_shapes=[pltpu.VMEM((TOP_K, tt, d), F32), pltpu.SemaphoreType.DMA],
        compiler_params=_params("arbitrary"),
        name="combine",
    )(dest_t, gates, base, mod, ys)


def _layer(x, mod, pos, rope, p):
    nbatch, seq, d = x.shape
    n = nbatch * seq
    xf = x.reshape(n, d)
    invf, bd = rope

    perm = np.concatenate([np.arange(h * HEAD_DIM, (h + 1) * HEAD_DIM) for h in PAIR_ORDER_A])
    w_in = p["w_in"]
    w_in_p = jnp.concatenate([w_in[:, :MIX_A][:, perm], w_in[:, MIX_A:]], axis=1).astype(BF16)
    ones = lambda w: jnp.ones((w,), F32)
    qscale = HEAD_DIM ** -0.5
    gcol = jnp.concatenate([jnp.tile(p["g_q_a"], N_HEADS_A) * qscale, jnp.tile(p["g_k_a"], N_KV_A),
                            ones(KV_A), jnp.tile(p["g_q_b"], N_HEADS_B) * qscale,
                            jnp.tile(p["g_k_b"], N_HEADS_B), ones(MIX_B)]).reshape(1, IN_WIDTH)
    proj = _inproj(xf, mod, pos, p["g_norm_mix"].reshape(1, d), w_in_p, gcol, invf, bd, seq)
    qa, ka, va = proj[:3]
    qkv_b = {1: proj[3:6]}
    for j, dil in enumerate(DILS):
        qkv_b[dil] = [proj[6 + t * len(DILS) + j] for t in range(3)]

    sinks_p = p["sinks_a"][np.array(PAIR_ORDER_A)]
    oa, _ = _attention(qa, ka, va, nbatch=nbatch, seq=seq, dil=1, max_dist=WINDOW_A - 1,
                       kv_shared=True, sinks=sinks_p, want_lse=False)
    obs, lses = [], []
    for window, dil in DILATED_BRANCHES:
        o, lse = _attention(*qkv_b[dil], nbatch=nbatch, seq=seq, dil=dil, max_dist=window // dil,
                            kv_shared=False, want_lse=True)
        obs.append(o)
        lses.append(lse)

    goa = p["g_out_a"][perm].reshape(1, MIX_A)
    w_out = p["w_out"]
    wo_p = jnp.concatenate([w_out[:MIX_A][perm], w_out[MIX_A:]], axis=0).astype(BF16)
    h2, base = _outproj(xf, mod, oa, obs, lses, goa, p["g_out_b"].reshape(1, MIX_B), wo_p,
                        p["g_norm_ffn"].reshape(1, d), p["w_gate_s"].astype(BF16),
                        p["w_up_s"].astype(BF16), p["w_down_s"].astype(BF16), seq)

    top_e_t, gates_t, counts = _router(h2, p["w_router"].T.astype(BF16),
                                       p["router_bias"].reshape(N_EXPERTS, 1))
    bm = EXPERT_ROWS
    counts = counts.reshape(N_EXPERTS).astype(I32)
    padded = (counts + bm - 1) // bm * bm
    pends = jnp.cumsum(padded)
    pstarts = pends - padded
    rows = n * TOP_K + N_EXPERTS * bm
    n_blocks = rows // bm
    n_valid = (pends[-1] // bm).astype(I32)
    blk = jnp.minimum(jnp.arange(n_blocks, dtype=I32), n_valid - 1)
    block_e = jnp.sum((pends[None, :] <= (blk * bm)[:, None]).astype(I32), axis=1)
    block_e = jnp.minimum(block_e, N_EXPERTS - 1)

    tri = (np.arange(TT_ROUTE)[:, None] <= np.arange(TT_ROUTE)[None, :])
    dest_t = _rank(top_e_t, pstarts.astype(F32).reshape(N_EXPERTS, 1), jnp.asarray(tri, BF16))
    xs = _dispatch(dest_t, h2, _padzero(pstarts + counts, padded - counts, rows, d))
    ys = _experts(block_e, n_valid.reshape(1), xs, p["w_gate_e"], p["w_up_e"], p["w_down_e"])
    out = _combine(dest_t, gates_t.T, base, mod, ys, seq)
    return out.reshape(nbatch, seq, d)


def kernel(x, c, positions, w_ada, b_ada, g_norm_mix, w_in, g_q_a, g_k_a, sinks_a, g_q_b, g_k_b,
           g_out_a, g_out_b, w_out, g_norm_ffn, w_router, router_bias, w_gate_e, w_up_e, w_down_e,
           w_gate_s, w_up_s, w_down_s):
    nbatch, seq, d = x.shape
    depth = w_ada.shape[0]
    params = dict(g_norm_mix=g_norm_mix, w_in=w_in, g_q_a=g_q_a, g_k_a=g_k_a, sinks_a=sinks_a,
                  g_q_b=g_q_b, g_k_b=g_k_b, g_out_a=g_out_a, g_out_b=g_out_b, w_out=w_out,
                  g_norm_ffn=g_norm_ffn, w_router=w_router, router_bias=router_bias,
                  w_gate_e=w_gate_e, w_up_e=w_up_e, w_down_e=w_down_e, w_gate_s=w_gate_s,
                  w_up_s=w_up_s, w_down_s=w_down_s)
    j = np.arange(LANES) % HEAD_DIM
    inv = ROPE_THETA ** (-jnp.arange(0, ROT_DIM, 2, dtype=F32) / ROT_DIM)
    invf = jnp.where(j < ROT_DIM, inv[j % (ROT_DIM // 2)], 0.0).astype(F32).reshape(1, LANES)
    bd = jnp.asarray((np.arange(LANES)[:, None] // HEAD_DIM) == (np.arange(LANES)[None, :] // HEAD_DIM),
                     BF16)
    pos = positions.reshape(nbatch * seq, 1).astype(I32)
    for l in range(depth):
        mod = _adaln(c.astype(F32), w_ada[l], b_ada[l]).reshape(nbatch, 6, d)
        x = _layer(x, mod, pos, (invf, bd), {k: v[l] for k, v in params.items()})
    return x
```

```python
import functools

import numpy as np
import jax
import jax.numpy as jnp
from jax import lax
from jax.experimental import pallas as pl
from jax.experimental.pallas import tpu as pltpu
from jax.experimental.pallas import tpu_sc as plsc

F32 = jnp.float32
BF16 = jnp.bfloat16
I32 = jnp.int32

HEAD_DIM = 64
N_HEADS_A = 8
N_KV_A = 2
WINDOW_A = 128
N_HEADS_B = 8
DILATED_BRANCHES = ((128, 1), (512, 4), (2048, 16))
DILS = tuple(dil for _, dil in DILATED_BRANCHES if dil > 1)
BLOCK = 128
ROT_DIM = HEAD_DIM // 4
ROPE_THETA = 500000.0
MIX_A = N_HEADS_A * HEAD_DIM
KV_A = N_KV_A * HEAD_DIM
MIX_B = N_HEADS_B * HEAD_DIM
N_EXPERTS = 256
TOP_K = 8
N_GROUPS = 8
TOPK_GROUPS = 4
GROUP_SIZE = N_EXPERTS // N_GROUPS
ROUTED_SCALE = 2.5
EPS = 1e-6

LANES = 128
HEADS_PER_VREG = LANES // HEAD_DIM
N_PAIRS = MIX_A // LANES
NEG = -1e30
VMEM_LIMIT = 48 * 1024 * 1024

TT_PROJ = 512
TT_ROUTE = 256
TT_DISPATCH = 256
TT_COMBINE = 128
EXPERT_ROWS = 256

PAIR_ORDER_A = tuple(h for p in range(N_PAIRS) for h in (p, p + N_HEADS_A // N_KV_A))


def _params(*sem):
    return pltpu.CompilerParams(dimension_semantics=sem, vmem_limit_bytes=VMEM_LIMIT)


def _silu(t):
    return t / (1.0 + jnp.exp(-t))


def _rms_rows(t):
    return t * lax.rsqrt(jnp.mean(t * t, axis=-1, keepdims=True) + EPS)


def _ada_kernel(c_ref, w_ref, b_ref, o_ref):
    cond = _silu(c_ref[...])
    o_ref[...] = jnp.dot(cond.astype(BF16), w_ref[...].astype(BF16),
                         preferred_element_type=F32) + b_ref[...]


def _adaln(c, w_ada, b_ada):
    nb, d = c.shape
    width = w_ada.shape[1]
    tn = 1024
    return pl.pallas_call(
        _ada_kernel,
        grid=(width // tn,),
        in_specs=[pl.BlockSpec((nb, d), lambda j: (0, 0)),
                  pl.BlockSpec((d, tn), lambda j: (0, j)),
                  pl.BlockSpec((1, tn), lambda j: (0, j))],
        out_specs=pl.BlockSpec((nb, tn), lambda j: (0, j)),
        out_shape=jax.ShapeDtypeStruct((nb, width), F32),
        compiler_params=_params("arbitrary"),
        name="adaln",
    )(c, w_ada, b_ada.reshape(1, width))


COL_QA, COL_KA, COL_VA = 0, MIX_A, MIX_A + KV_A
COL_QB = MIX_A + 2 * KV_A
COL_KB, COL_VB = COL_QB + MIX_B, COL_QB + 2 * MIX_B
IN_WIDTH = COL_VB + MIX_B


def _inproj_kernel(x_ref, mod_ref, pos_ref, gn_ref, w_ref, gcol_ref, invf_ref, bd_ref,
                   qa_ref, ka_ref, va_ref, qb_ref, kb_ref, vb_ref, *rest):
    n_dil = len(DILS)
    dil_refs = [rest[i * n_dil:(i + 1) * n_dil] for i in range(3)]
    qb_scr, kb_scr, vb_scr = rest[3 * n_dil:]
    tt = x_ref.shape[0]
    shift, scale = mod_ref[0:1, :], mod_ref[1:2, :]
    h = _rms_rows(x_ref[...]) * gn_ref[...] * (1.0 + scale) + shift
    proj = jnp.dot(h.astype(BF16), w_ref[...], preferred_element_type=F32)

    ang = pos_ref[...].astype(F32) * invf_ref[...]
    cs, sn = jnp.cos(ang), jnp.sin(ang)
    lane = lax.broadcasted_iota(I32, (1, LANES), 1) % HEAD_DIM
    s_lo = jnp.where(lane < ROT_DIM // 2, -sn, 0.0)
    s_hi = jnp.where((lane >= ROT_DIM // 2) & (lane < ROT_DIM), sn, 0.0)
    bd = bd_ref[...]

    def norm_rope(col0, width, out_ref, scr=None):
        for j in range(width // LANES):
            c = col0 + j * LANES
            t = proj[:, c:c + LANES]
            sq = t * t
            hi = sq.astype(BF16)
            lo = (sq - hi.astype(F32)).astype(BF16)
            ss = (jnp.dot(hi, bd, preferred_element_type=F32)
                  + jnp.dot(lo, bd, preferred_element_type=F32))
            t = t * lax.rsqrt(ss * (1.0 / HEAD_DIM) + EPS) * gcol_ref[:, c:c + LANES]
            t = (t * cs + pltpu.roll(t, LANES - ROT_DIM // 2, 1) * s_lo
                 + pltpu.roll(t, ROT_DIM // 2, 1) * s_hi)
            out_ref[:, j * LANES:(j + 1) * LANES] = t.astype(BF16)
            if scr is not None:
                scr[j] = t

    norm_rope(COL_QA, MIX_A, qa_ref)
    norm_rope(COL_KA, KV_A, ka_ref)
    norm_rope(COL_QB, MIX_B, qb_ref, qb_scr)
    norm_rope(COL_KB, MIX_B, kb_ref, kb_scr)
    va_ref[...] = proj[:, COL_VA:COL_VA + KV_A].astype(BF16)
    vb_ref[...] = proj[:, COL_VB:COL_VB + MIX_B].astype(BF16)
    for j in range(N_PAIRS):
        vb_scr[j] = proj[:, COL_VB + j * LANES:COL_VB + (j + 1) * LANES]
    for scr, outs in zip((qb_scr, kb_scr, vb_scr), dil_refs):
        for dil, out in zip(DILS, outs):
            for r in range(dil):
                for j in range(N_PAIRS):
                    c = r * MIX_B + j * LANES
                    out[:, c:c + LANES] = scr[j, pl.ds(r, tt // dil, stride=dil), :].astype(BF16)


def _inproj(xf, mod, pos, g_norm, w_in_p, gcol, invf, bd, seq):
    n, d = xf.shape
    tt = TT_PROJ
    tiles_per_seq = seq // tt
    shapes = [(n, w) for w in (MIX_A, KV_A, KV_A, MIX_B, MIX_B, MIX_B)]
    shapes += [(n // dil, dil * MIX_B) for _ in range(3) for dil in DILS]
    full = lambda shape: pl.BlockSpec(shape, lambda i: (0,) * len(shape))
    return pl.pallas_call(
        _inproj_kernel,
        grid=(n // tt,),
        in_specs=[pl.BlockSpec((tt, d), lambda i: (i, 0)),
                  pl.BlockSpec((None, 6, d), lambda i: (i // tiles_per_seq, 0, 0)),
                  pl.BlockSpec((tt, 1), lambda i: (i, 0)),
                  full((1, d)), full((d, IN_WIDTH)), full((1, IN_WIDTH)),
                  full((1, LANES)), full((LANES, LANES))],
        out_specs=[pl.BlockSpec((tt * r // n, w), lambda i: (i, 0)) for r, w in shapes],
        out_shape=[jax.ShapeDtypeStruct(s, BF16) for s in shapes],
        scratch_shapes=[pltpu.VMEM((N_PAIRS, tt, LANES), F32)] * 3,
        compiler_params=_params("parallel"),
        name="inproj",
    )(xf, mod, pos, g_norm, w_in_p, gcol, invf, bd)


def _attn_kernel(*refs, kv_shared, max_dist, use_prev, has_sinks, want_lse):
    refs = list(refs)
    sink_ref = refs.pop(0) if has_sinks else None
    q_ref = refs.pop(0)
    kp_ref = refs.pop(0) if use_prev else None
    kc_ref = refs.pop(0)
    vp_ref = refs.pop(0) if use_prev else None
    vc_ref = refs.pop(0)
    o_ref = refs.pop(0)
    lse_ref = refs.pop(0) if want_lse else None

    blk = pl.program_id(2)
    nq = 2 * BLOCK
    nk = 2 * BLOCK if use_prev else BLOCK
    qpos = lax.broadcasted_iota(I32, (nq, nk), 0) % BLOCK
    kpos = lax.broadcasted_iota(I32, (nq, nk), 1)
    if use_prev:
        dist = qpos + BLOCK - kpos
        valid = (dist >= 0) & (dist <= max_dist) & ((kpos >= BLOCK) | (blk > 0))
    else:
        dist = qpos - kpos
        valid = (dist >= 0) & (dist <= max_dist)
    lane = lax.broadcasted_iota(I32, (nq, LANES), 1)
    row = lax.broadcasted_iota(I32, (nq, LANES), 0)
    own_half = (lane < HEAD_DIM) == (row < BLOCK)
    left_lanes = lax.broadcasted_iota(I32, (BLOCK, LANES), 1) < HEAD_DIM
    lane8 = lax.broadcasted_iota(I32, (BLOCK, 2 * N_PAIRS), 1)
    lse_blk = jnp.zeros((BLOCK, 2 * N_PAIRS), F32)

    for p in range(N_PAIRS):
        cq = slice(p * LANES, (p + 1) * LANES)
        ck = slice(0, LANES) if kv_shared else cq
        qp = q_ref[:, cq]
        qs = jnp.concatenate([qp, qp], axis=0)
        qs = jnp.where(own_half, qs, jnp.zeros_like(qs))
        if use_prev:
            k = jnp.concatenate([kp_ref[:, ck], kc_ref[:, ck]], axis=0)
            v = jnp.concatenate([vp_ref[:, ck], vc_ref[:, ck]], axis=0)
        else:
            k, v = kc_ref[:, ck], vc_ref[:, ck]
        s = lax.dot_general(qs, k, (((1,), (1,)), ((), ())), preferred_element_type=F32)
        s = jnp.where(valid, s, NEG)
        m = jnp.max(s, axis=-1, keepdims=True)
        if has_sinks:
            rows1 = lax.broadcasted_iota(I32, (nq, 1), 0)
            sink = jnp.where(rows1 < BLOCK, sink_ref[2 * p], sink_ref[2 * p + 1])
            m = jnp.maximum(m, sink)
        e = jnp.exp(s - m)
        l = jnp.sum(e, axis=-1, keepdims=True)
        if has_sinks:
            l = l + jnp.exp(sink - m)
        o = jnp.dot(e.astype(BF16), v, preferred_element_type=F32) / l
        o_ref[:, cq] = jnp.where(left_lanes, o[:BLOCK], o[BLOCK:]).astype(BF16)
        if want_lse:
            lse = m + jnp.log(l)
            lse_blk = (lse_blk + jnp.where(lane8 == 2 * p, lse[:BLOCK], 0.0)
                       + jnp.where(lane8 == 2 * p + 1, lse[BLOCK:], 0.0))
    if want_lse:
        lse_ref[...] = lse_blk


def _attention(q, k, v, *, nbatch, seq, dil, max_dist, kv_shared, sinks=None, want_lse):
    length = seq // dil
    nblk = length // BLOCK
    use_prev = nblk > 1
    kw = k.shape[1] // dil
    view = lambda t: t.reshape(nbatch, length, t.shape[1])
    cur = lambda b, r, i: (b, i, r)
    prev = lambda b, r, i: (b, jnp.maximum(i - 1, 0), r)
    in_specs, args = [], []
    if sinks is not None:
        in_specs.append(pl.BlockSpec(memory_space=pltpu.SMEM))
        args.append(sinks)
    in_specs.append(pl.BlockSpec((None, BLOCK, MIX_B), cur))
    args.append(view(q))
    for t in (k, v):
        if use_prev:
            in_specs.append(pl.BlockSpec((None, BLOCK, kw), prev))
            args.append(view(t))
        in_specs.append(pl.BlockSpec((None, BLOCK, kw), cur))
        args.append(view(t))
    out_specs = [pl.BlockSpec((None, BLOCK, MIX_B), cur)]
    out_shape = [jax.ShapeDtypeStruct((nbatch, length, dil * MIX_B), BF16)]
    if want_lse:
        out_specs.append(pl.BlockSpec((None, None, BLOCK, N_HEADS_B), lambda b, r, i: (b, r, i, 0)))
        out_shape.append(jax.ShapeDtypeStruct((nbatch, dil, length, N_HEADS_B), F32))
    outs = pl.pallas_call(
        functools.partial(_attn_kernel, kv_shared=kv_shared, max_dist=max_dist, use_prev=use_prev,
                          has_sinks=sinks is not None, want_lse=want_lse),
        grid=(nbatch, dil, nblk),
        in_specs=in_specs, out_specs=out_specs, out_shape=out_shape,
        compiler_params=_params("parallel", "parallel", "arbitrary"),
        name=f"attn_d{dil}" + ("_swa" if kv_shared else ""),
    )(*args)
    o = outs[0].reshape(nbatch * length, dil * MIX_B)
    if not want_lse:
        return o, None
    lse = outs[1].transpose(0, 2, 1, 3).reshape(nbatch * seq, N_HEADS_B)
    return o, lse


def _expand_heads(w, width):
    head = lax.broadcasted_iota(I32, (1, width), 1) // HEAD_DIM
    out = jnp.zeros((w.shape[0], width), F32)
    for hd in range(w.shape[1]):
        out = jnp.where(head == hd, w[:, hd:hd + 1], out)
    return out


def _outproj_kernel(x_ref, mod_ref, oa_ref, ob1_ref, ob2_ref, ob3_ref, l1_ref, l2_ref, l3_ref,
                    goa_ref, gob_ref, wo_ref, gf_ref, wgs_ref, wus_ref, wds_ref,
                    h2_ref, base_ref, ob_scr):
    tt = x_ref.shape[0]
    gate_a = mod_ref[2:3, :]
    shift_m, scale_m, gate_m = mod_ref[3:4, :], mod_ref[4:5, :], mod_ref[5:6, :]

    def token_major(ref, dil):
        if dil == 1:
            return ref[...].astype(F32)
        for r in range(dil):
            for j in range(N_PAIRS):
                c = r * MIX_B + j * LANES
                ob_scr[j, pl.ds(r, tt // dil, stride=dil), :] = ref[:, c:c + LANES].astype(F32)
        return jnp.concatenate([ob_scr[j] for j in range(N_PAIRS)], axis=1)

    l1, l2, l3 = l1_ref[...], l2_ref[...], l3_ref[...]
    mx = jnp.maximum(jnp.maximum(l1, l2), l3)
    e1, e2, e3 = jnp.exp(l1 - mx), jnp.exp(l2 - mx), jnp.exp(l3 - mx)
    den = e1 + e2 + e3
    dils = [dil for _, dil in DILATED_BRANCHES]
    ob = _expand_heads(e1 / den, MIX_B) * token_major(ob1_ref, dils[0])
    ob = ob + _expand_heads(e2 / den, MIX_B) * token_major(ob2_ref, dils[1])
    ob = ob + _expand_heads(e3 / den, MIX_B) * token_major(ob3_ref, dils[2])
    ob = _rms_rows(ob) * gob_ref[...]
    oa = _rms_rows(oa_ref[...].astype(F32)) * goa_ref[...]
    y = (jnp.dot(oa.astype(BF16), wo_ref[0:MIX_A, :], preferred_element_type=F32)
         + jnp.dot(ob.astype(BF16), wo_ref[MIX_A:MIX_A + MIX_B, :], preferred_element_type=F32))
    x1 = x_ref[...] + gate_a * y
    h2 = _rms_rows(x1) * gf_ref[...] * (1.0 + scale_m) + shift_m
    h2_ref[...] = h2
    hb = h2.astype(BF16)
    act = (_silu(jnp.dot(hb, wgs_ref[...], preferred_element_type=F32))
           * jnp.dot(hb, wus_ref[...], preferred_element_type=F32))
    shared = jnp.dot(act.astype(BF16), wds_ref[...], preferred_element_type=F32)
    base_ref[...] = x1 + gate_m * shared


def _outproj(xf, mod, oa, obs, lses, goa, gob, wo_p, gf, wgs, wus, wds, seq):
    n, d = xf.shape
    tt = TT_PROJ // 2
    tiles_per_seq = seq // tt
    tile = lambda w: pl.BlockSpec((tt, w), lambda i: (i, 0))
    full = lambda shape: pl.BlockSpec(shape, lambda i: (0,) * len(shape))
    sd = wgs.shape[1]
    dilated = [pl.BlockSpec((tt // dil, dil * MIX_B), lambda i: (i, 0)) for _, dil in DILATED_BRANCHES]
    return pl.pallas_call(
        _outproj_kernel,
        grid=(n // tt,),
        in_specs=[tile(d), pl.BlockSpec((None, 6, d), lambda i: (i // tiles_per_seq, 0, 0)),
                  tile(MIX_A), *dilated,
                  tile(N_HEADS_B), tile(N_HEADS_B), tile(N_HEADS_B),
                  full((1, MIX_A)), full((1, MIX_B)), full((MIX_A + MIX_B, d)), full((1, d)),
                  full((d, sd)), full((d, sd)), full((sd, d))],
        out_specs=[tile(d), tile(d)],
        out_shape=[jax.ShapeDtypeStruct((n, d), F32), jax.ShapeDtypeStruct((n, d), F32)],
        scratch_shapes=[pltpu.VMEM((N_PAIRS, tt, LANES), F32)],
        compiler_params=_params("parallel"),
        name="outproj",
    )(xf, mod, oa, *obs, *lses, goa, gob, wo_p, gf, wgs, wus, wds)


def _router_kernel(h_ref, wrt_ref, bias_ref, e_ref, g_ref, cnt_ref):
    tt = h_ref.shape[0]
    logits = lax.dot_general(wrt_ref[...], h_ref[...].astype(BF16), (((1,), (1,)), ((), ())),
                             preferred_element_type=F32)
    scores = 1.0 / (1.0 + jnp.exp(-logits))
    biased = scores + bias_ref[...]
    ninf = -jnp.inf

    j32 = lax.broadcasted_iota(I32, (GROUP_SIZE, tt), 0).astype(F32)
    grp = []
    for g in range(N_GROUPS):
        bg = biased[g * GROUP_SIZE:(g + 1) * GROUP_SIZE, :]
        m1 = jnp.max(bg, axis=0, keepdims=True)
        i1 = jnp.min(jnp.where(bg == m1, j32, float(GROUP_SIZE)), axis=0, keepdims=True)
        m2 = jnp.max(jnp.where(j32 == i1, ninf, bg), axis=0, keepdims=True)
        grp.append(m1 + m2)
    grp = jnp.concatenate(grp, axis=0)
    g8 = lax.broadcasted_iota(I32, (N_GROUPS, tt), 0).astype(F32)
    chosen = jnp.zeros((N_GROUPS, tt), F32)
    for _ in range(TOPK_GROUPS):
        gm = jnp.max(grp, axis=0, keepdims=True)
        gi = jnp.min(jnp.where(grp == gm, g8, float(N_GROUPS)), axis=0, keepdims=True)
        hit = g8 == gi
        chosen = jnp.where(hit, 1.0, chosen)
        grp = jnp.where(hit, ninf, grp)
    masked = jnp.concatenate(
        [jnp.where(chosen[g:g + 1, :] > 0.0, biased[g * GROUP_SIZE:(g + 1) * GROUP_SIZE, :], ninf)
         for g in range(N_GROUPS)], axis=0)

    eio = lax.broadcasted_iota(I32, (N_EXPERTS, tt), 0).astype(F32)
    picked = jnp.zeros((N_EXPERTS, tt), F32)
    es, gs = [], []
    for _ in range(TOP_K):
        m = jnp.max(masked, axis=0, keepdims=True)
        idx = jnp.min(jnp.where(masked == m, eio, float(N_EXPERTS)), axis=0, keepdims=True)
        hit = eio == idx
        gs.append(jnp.sum(jnp.where(hit, scores, 0.0), axis=0, keepdims=True))
        es.append(idx)
        picked = jnp.where(hit, 1.0, picked)
        masked = jnp.where(hit, ninf, masked)
    gates = jnp.concatenate(gs, axis=0)
    e_ref[...] = jnp.concatenate(es, axis=0).astype(I32)
    g_ref[...] = gates / jnp.sum(gates, axis=0, keepdims=True) * ROUTED_SCALE

    @pl.when(pl.program_id(0) == 0)
    def _():
        cnt_ref[...] = jnp.zeros_like(cnt_ref)
    cnt_ref[...] += jnp.sum(picked, axis=1, keepdims=True)


def _router(h2, wrt, bias_col):
    n, d = h2.shape
    tt = TT_ROUTE
    return pl.pallas_call(
        _router_kernel,
        grid=(n // tt,),
        in_specs=[pl.BlockSpec((tt, d), lambda i: (i, 0)),
                  pl.BlockSpec((N_EXPERTS, d), lambda i: (0, 0)),
                  pl.BlockSpec((N_EXPERTS, 1), lambda i: (0, 0))],
        out_specs=[pl.BlockSpec((TOP_K, tt), lambda i: (0, i)),
                   pl.BlockSpec((TOP_K, tt), lambda i: (0, i)),
                   pl.BlockSpec((N_EXPERTS, 1), lambda i: (0, 0))],
        out_shape=[jax.ShapeDtypeStruct((TOP_K, n), I32),
                   jax.ShapeDtypeStruct((TOP_K, n), F32),
                   jax.ShapeDtypeStruct((N_EXPERTS, 1), F32)],
        compiler_params=_params("arbitrary"),
        name="router",
    )(h2, wrt, bias_col)


def _rank_kernel(e_ref, pstart_ref, tri_ref, dest_ref, carry_ref):
    tt = e_ref.shape[1]

    @pl.when(pl.program_id(0) == 0)
    def _():
        carry_ref[...] = pstart_ref[...]

    e = e_ref[...]
    eio = lax.broadcasted_iota(I32, (N_EXPERTS, tt), 0)
    mask = jnp.zeros((N_EXPERTS, tt), F32)
    for k in range(TOP_K):
        mask = jnp.where(eio == e[k:k + 1, :], 1.0, mask)
    incl = jnp.dot(mask.astype(BF16), tri_ref[...], preferred_element_type=F32)
    pos = incl - mask + carry_ref[...]
    dest = [jnp.sum(jnp.where(eio == e[k:k + 1, :], pos, 0.0), axis=0, keepdims=True)
            for k in range(TOP_K)]
    dest_ref[...] = jnp.concatenate(dest, axis=0).astype(I32)
    carry_ref[...] += incl[:, tt - 1:tt]


def _rank(top_e_t, pstart_col, tri):
    n = top_e_t.shape[1]
    tt = TT_ROUTE
    return pl.pallas_call(
        _rank_kernel,
        grid=(n // tt,),
        in_specs=[pl.BlockSpec((TOP_K, tt), lambda i: (0, i)),
                  pl.BlockSpec((N_EXPERTS, 1), lambda i: (0, 0)),
                  pl.BlockSpec((tt, tt), lambda i: (0, 0))],
        out_specs=pl.BlockSpec((TOP_K, tt), lambda i: (0, i)),
        out_shape=jax.ShapeDtypeStruct((TOP_K, n), I32),
        scratch_shapes=[pltpu.VMEM((N_EXPERTS, 1), F32)],
        compiler_params=_params("arbitrary"),
        name="rank",
    )(top_e_t, pstart_col, tri)


def _row_copy(src_ref, src_row, dst_ref, dst_row, sem):
    return pltpu.make_async_copy(src_ref.at[pl.ds(src_row, 1)], dst_ref.at[pl.ds(dst_row, 1)], sem)


SC_CORES, SC_SUBCORES, SC_LANES = 2, 16, 16
SC_WORKERS = SC_CORES * SC_SUBCORES
INVERT_CHUNK = 8192


def _invert(dest_flat, n_tokens, rows):
    per = rows // SC_WORKERS
    n_assign = dest_flat.shape[0]
    assert rows % (SC_WORKERS * SC_LANES) == 0 and n_assign % INVERT_CHUNK == 0
    assert n_tokens & (n_tokens - 1) == 0

    @functools.partial(
        pl.kernel, mesh=plsc.VectorSubcoreMesh(core_axis_name="c", subcore_axis_name="s"),
        out_type=jax.ShapeDtypeStruct((rows,), I32),
        scratch_types=[pltpu.VMEM((INVERT_CHUNK,), I32), pltpu.VMEM((per,), I32)],
        compiler_params=pltpu.CompilerParams(needs_layout_passes=False))
    def invert(dest_hbm, out_hbm, staged, local):
        base = (lax.axis_index("s") * SC_CORES + lax.axis_index("c")) * per

        @pl.loop(0, per, step=SC_LANES)
        def _(i):
            local[pl.ds(i, SC_LANES)] = jnp.zeros((SC_LANES,), I32)

        lane = lax.iota(I32, SC_LANES)

        @pl.loop(0, n_assign // INVERT_CHUNK)
        def _(c):
            pltpu.sync_copy(dest_hbm.at[pl.ds(c * INVERT_CHUNK, INVERT_CHUNK)], staged)

            @pl.loop(0, INVERT_CHUNK, step=SC_LANES)
            def _(i):
                rel = staged[pl.ds(i, SC_LANES)] - base
                mine = (rel >= 0) & (rel < per)
                tok = (c * INVERT_CHUNK + i + lane) & (n_tokens - 1)
                plsc.store_scatter(local, [jnp.where(mine, rel, 0)], tok, mask=mine)

        pltpu.sync_copy(local, out_hbm.at[pl.ds(base, per)])

    return invert(dest_flat)


def _dispatch_kernel(nv_ref, tok_ref, h_ref, xs_ref, sem):
    j = pl.program_id(0)
    bm = tok_ref.shape[0]

    @pl.when(j < nv_ref[0])
    def _():
        def issue(i, carry):
            _row_copy(h_ref, tok_ref[i], xs_ref, j * bm + i, sem).start()
            return carry

        def drain(i, carry):
            _row_copy(h_ref, tok_ref[i], xs_ref, j * bm + i, sem).wait()
            return carry

        lax.fori_loop(0, bm, issue, 0, unroll=8)
        lax.fori_loop(0, bm, drain, 0, unroll=8)


def _dispatch(n_valid, row_tok, h2):
    n, d = h2.shape
    rows = row_tok.shape[0]
    bm = EXPERT_ROWS
    return pl.pallas_call(
        _dispatch_kernel,
        grid_spec=pltpu.PrefetchScalarGridSpec(
            num_scalar_prefetch=1, grid=(rows // bm,),
            in_specs=[pl.BlockSpec((bm,), lambda j, nv: (j,), memory_space=pltpu.SMEM),
                      pl.BlockSpec(memory_space=pl.ANY)],
            out_specs=pl.BlockSpec(memory_space=pl.ANY),
            scratch_shapes=[pltpu.SemaphoreType.DMA]),
        out_shape=jax.ShapeDtypeStruct((rows, d), F32),
        compiler_params=_params("arbitrary"),
        name="dispatch",
    )(n_valid, row_tok, h2)


def _experts_kernel(be_ref, nv_ref, xs_ref, wg_ref, wu_ref, wd_ref, ys_ref, wgb, wub, wdb):
    j = pl.program_id(0)

    @pl.when(j < nv_ref[0])
    def _():
        fresh = jnp.logical_or(j == 0, be_ref[j] != be_ref[jnp.maximum(j - 1, 0)])

        @pl.when(fresh)
        def _():
            wgb[...] = wg_ref[...].astype(BF16)
            wub[...] = wu_ref[...].astype(BF16)
            wdb[...] = wd_ref[...].astype(BF16)

        xb = xs_ref[...].astype(BF16)
        act = (_silu(jnp.dot(xb, wgb[...], preferred_element_type=F32))
               * jnp.dot(xb, wub[...], preferred_element_type=F32))
        ys_ref[...] = jnp.dot(act.astype(BF16), wdb[...], preferred_element_type=F32)


def _experts(block_e, n_valid, xs, wg, wu, wd):
    rows, d = xs.shape
    bm = EXPERT_ROWS
    f = wg.shape[2]
    row_blk = lambda j, be, nv: (jnp.minimum(j, nv[0] - 1), 0)
    w_blk = lambda j, be, nv: (be[j], 0, 0)
    return pl.pallas_call(
        _experts_kernel,
        grid_spec=pltpu.PrefetchScalarGridSpec(
            num_scalar_prefetch=2,
            grid=(rows // bm,),
            in_specs=[pl.BlockSpec((bm, d), row_blk),
                      pl.BlockSpec((None, d, f), w_blk),
                      pl.BlockSpec((None, d, f), w_blk),
                      pl.BlockSpec((None, f, d), w_blk)],
            out_specs=pl.BlockSpec((bm, d), row_blk),
            scratch_shapes=[pltpu.VMEM((d, f), BF16), pltpu.VMEM((d, f), BF16),
                            pltpu.VMEM((f, d), BF16)]),
        out_shape=jax.ShapeDtypeStruct((rows, d), F32),
        compiler_params=_params("arbitrary"),
        name="experts",
    )(block_e, n_valid, xs, wg, wu, wd)


def _combine_kernel(dest_ref, gates_ref, base_ref, mod_ref, ys_ref, out_ref, buf, sem):
    tt = base_ref.shape[0]

    def issue(t, carry):
        for k in range(TOP_K):
            _row_copy(ys_ref, dest_ref[k, t], buf.at[k], t, sem).start()
        return carry

    def drain(t, carry):
        for k in range(TOP_K):
            _row_copy(ys_ref, dest_ref[k, t], buf.at[k], t, sem).wait()
        return carry

    lax.fori_loop(0, tt, issue, 0)
    lax.fori_loop(0, tt, drain, 0)
    gates = gates_ref[...]
    routed = gates[:, 0:1] * buf[0]
    for k in range(1, TOP_K):
        routed = routed + gates[:, k:k + 1] * buf[k]
    out_ref[...] = base_ref[...] + mod_ref[5:6, :] * routed


def _combine(dest_t, gates, base, mod, ys, seq):
    n, d = base.shape
    tt = TT_COMBINE
    tiles_per_seq = seq // tt
    return pl.pallas_call(
        _combine_kernel,
        grid=(n // tt,),
        in_specs=[pl.BlockSpec((TOP_K, tt), lambda i: (0, i), memory_space=pltpu.SMEM),
                  pl.BlockSpec((tt, TOP_K), lambda i: (i, 0)),
                  pl.BlockSpec((tt, d), lambda i: (i, 0)),
                  pl.BlockSpec((None, 6, d), lambda i: (i // tiles_per_seq, 0, 0)),
                  pl.BlockSpec(memory_space=pl.ANY)],
        out_specs=pl.BlockSpec((tt, d), lambda i: (i, 0)),
        out_shape=jax.ShapeDtypeStruct((n, d), F32),
        scratch_shapes=[pltpu.VMEM((TOP_K, tt, d), F32), pltpu.SemaphoreType.DMA],
        compiler_params=_params("arbitrary"),
        name="combine",
    )(dest_t, gates, base, mod, ys)


def _layer(x, mod, pos, rope, p):
    nbatch, seq, d = x.shape
    n = nbatch * seq
    xf = x.reshape(n, d)
    invf, bd = rope

    perm = np.concatenate([np.arange(h * HEAD_DIM, (h + 1) * HEAD_DIM) for h in PAIR_ORDER_A])
    w_in = p["w_in"]
    w_in_p = jnp.concatenate([w_in[:, :MIX_A][:, perm], w_in[:, MIX_A:]], axis=1).astype(BF16)
    ones = lambda w: jnp.ones((w,), F32)
    qscale = HEAD_DIM ** -0.5
    gcol = jnp.concatenate([jnp.tile(p["g_q_a"], N_HEADS_A) * qscale, jnp.tile(p["g_k_a"], N_KV_A),
                            ones(KV_A), jnp.tile(p["g_q_b"], N_HEADS_B) * qscale,
                            jnp.tile(p["g_k_b"], N_HEADS_B), ones(MIX_B)]).reshape(1, IN_WIDTH)
    proj = _inproj(xf, mod, pos, p["g_norm_mix"].reshape(1, d), w_in_p, gcol, invf, bd, seq)
    qa, ka, va = proj[:3]
    qkv_b = {1: proj[3:6]}
    for j, dil in enumerate(DILS):
        qkv_b[dil] = [proj[6 + t * len(DILS) + j] for t in range(3)]

    sinks_p = p["sinks_a"][np.array(PAIR_ORDER_A)]
    oa, _ = _attention(qa, ka, va, nbatch=nbatch, seq=seq, dil=1, max_dist=WINDOW_A - 1,
                       kv_shared=True, sinks=sinks_p, want_lse=False)
    obs, lses = [], []
    for window, dil in DILATED_BRANCHES:
        o, lse = _attention(*qkv_b[dil], nbatch=nbatch, seq=seq, dil=dil, max_dist=window // dil,
                            kv_shared=False, want_lse=True)
        obs.append(o)
        lses.append(lse)

    goa = p["g_out_a"][perm].reshape(1, MIX_A)
    w_out = p["w_out"]
    wo_p = jnp.concatenate([w_out[:MIX_A][perm], w_out[MIX_A:]], axis=0).astype(BF16)
    h2, base = _outproj(xf, mod, oa, obs, lses, goa, p["g_out_b"].reshape(1, MIX_B), wo_p,
                        p["g_norm_ffn"].reshape(1, d), p["w_gate_s"].astype(BF16),
                        p["w_up_s"].astype(BF16), p["w_down_s"].astype(BF16), seq)

    top_e_t, gates_t, counts = _router(h2, p["w_router"].T.astype(BF16),
                                       p["router_bias"].reshape(N_EXPERTS, 1))
    bm = EXPERT_ROWS
    counts = counts.reshape(N_EXPERTS).astype(I32)
    padded = (counts + bm - 1) // bm * bm
    pends = jnp.cumsum(padded)
    pstarts = pends - padded
    rows = n * TOP_K + N_EXPERTS * bm
    n_blocks = rows // bm
    n_valid = (pends[-1] // bm).astype(I32)
    blk = jnp.minimum(jnp.arange(n_blocks, dtype=I32), n_valid - 1)
    block_e = jnp.sum((pends[None, :] <= (blk * bm)[:, None]).astype(I32), axis=1)
    block_e = jnp.minimum(block_e, N_EXPERTS - 1)

    tri = (np.arange(TT_ROUTE)[:, None] <= np.arange(TT_ROUTE)[None, :])
    dest_t = _rank(top_e_t, pstarts.astype(F32).reshape(N_EXPERTS, 1), jnp.asarray(tri, BF16))
    n_valid = n_valid.reshape(1)
    row_tok = _invert(dest_t.reshape(n * TOP_K), n, rows)
    xs = _dispatch(n_valid, row_tok, h2)
    ys = _experts(block_e, n_valid, xs, p["w_gate_e"], p["w_up_e"], p["w_down_e"])
    out = _combine(dest_t, gates_t.T, base, mod, ys, seq)
    return out.reshape(nbatch, seq, d)


def kernel(x, c, positions, w_ada, b_ada, g_norm_mix, w_in, g_q_a, g_k_a, sinks_a, g_q_b, g_k_b,
           g_out_a, g_out_b, w_out, g_norm_ffn, w_router, router_bias, w_gate_e, w_up_e, w_down_e,
           w_gate_s, w_up_s, w_down_s):
    nbatch, seq, d = x.shape
    depth = w_ada.shape[0]
    params = dict(g_norm_mix=g_norm_mix, w_in=w_in, g_q_a=g_q_a, g_k_a=g_k_a, sinks_a=sinks_a,
                  g_q_b=g_q_b, g_k_b=g_k_b, g_out_a=g_out_a, g_out_b=g_out_b, w_out=w_out,
                  g_norm_ffn=g_norm_ffn, w_router=w_router, router_bias=router_bias,
                  w_gate_e=w_gate_e, w_up_e=w_up_e, w_down_e=w_down_e, w_gate_s=w_gate_s,
                  w_up_s=w_up_s, w_down_s=w_down_s)
    j = np.arange(LANES) % HEAD_DIM
    inv = ROPE_THETA ** (-jnp.arange(0, ROT_DIM, 2, dtype=F32) / ROT_DIM)
    invf = jnp.where(j < ROT_DIM, inv[j % (ROT_DIM // 2)], 0.0).astype(F32).reshape(1, LANES)
    bd = jnp.asarray((np.arange(LANES)[:, None] // HEAD_DIM) == (np.arange(LANES)[None, :] // HEAD_DIM),
                     BF16)
    pos = positions.reshape(nbatch * seq, 1).astype(I32)
    for l in range(depth):
        mod = _adaln(c.astype(F32), w_ada[l], b_ada[l]).reshape(nbatch, 6, d)
        x = _layer(x, mod, pos, (invf, bd), {k: v[l] for k, v in params.items()})
    return x
```

```python
import functools

import numpy as np
import jax
import jax.numpy as jnp
from jax import lax
from jax.experimental import pallas as pl
from jax.experimental.pallas import tpu as pltpu
from jax.experimental.pallas import tpu_sc as plsc

F32 = jnp.float32
BF16 = jnp.bfloat16
I32 = jnp.int32

HEAD_DIM = 64
N_HEADS_A = 8
N_KV_A = 2
WINDOW_A = 128
N_HEADS_B = 8
DILATED_BRANCHES = ((128, 1), (512, 4), (2048, 16))
DILS = tuple(dil for _, dil in DILATED_BRANCHES if dil > 1)
BLOCK = 128
ROT_DIM = HEAD_DIM // 4
ROPE_THETA = 500000.0
MIX_A = N_HEADS_A * HEAD_DIM
KV_A = N_KV_A * HEAD_DIM
MIX_B = N_HEADS_B * HEAD_DIM
N_EXPERTS = 256
TOP_K = 8
N_GROUPS = 8
TOPK_GROUPS = 4
GROUP_SIZE = N_EXPERTS // N_GROUPS
ROUTED_SCALE = 2.5
EPS = 1e-6

LANES = 128
HEADS_PER_VREG = LANES // HEAD_DIM
N_PAIRS = MIX_A // LANES
NEG = -1e30
VMEM_LIMIT = 48 * 1024 * 1024

TT_PROJ = 512
TT_ROUTE = 256
TT_DISPATCH = 256
TT_COMBINE = 128
EXPERT_ROWS = 256

PAIR_ORDER_A = tuple(h for p in range(N_PAIRS) for h in (p, p + N_HEADS_A // N_KV_A))


def _params(*sem):
    return pltpu.CompilerParams(dimension_semantics=sem, vmem_limit_bytes=VMEM_LIMIT)


def _silu(t):
    return t / (1.0 + jnp.exp(-t))


def _rms_rows(t):
    return t * lax.rsqrt(jnp.mean(t * t, axis=-1, keepdims=True) + EPS)


def _ada_kernel(c_ref, w_ref, b_ref, o_ref):
    cond = _silu(c_ref[...])
    o_ref[...] = jnp.dot(cond.astype(BF16), w_ref[...].astype(BF16),
                         preferred_element_type=F32) + b_ref[...]


def _adaln(c, w_ada, b_ada):
    nb, d = c.shape
    width = w_ada.shape[1]
    tn = 1024
    return pl.pallas_call(
        _ada_kernel,
        grid=(width // tn,),
        in_specs=[pl.BlockSpec((nb, d), lambda j: (0, 0)),
                  pl.BlockSpec((d, tn), lambda j: (0, j)),
                  pl.BlockSpec((1, tn), lambda j: (0, j))],
        out_specs=pl.BlockSpec((nb, tn), lambda j: (0, j)),
        out_shape=jax.ShapeDtypeStruct((nb, width), F32),
        compiler_params=_params("arbitrary"),
        name="adaln",
    )(c, w_ada, b_ada.reshape(1, width))


COL_QA, COL_KA, COL_VA = 0, MIX_A, MIX_A + KV_A
COL_QB = MIX_A + 2 * KV_A
COL_KB, COL_VB = COL_QB + MIX_B, COL_QB + 2 * MIX_B
IN_WIDTH = COL_VB + MIX_B


def _inproj_kernel(x_ref, mod_ref, pos_ref, gn_ref, w_ref, gcol_ref, invf_ref, bd_ref,
                   qa_ref, ka_ref, va_ref, qb_ref, kb_ref, vb_ref, *rest):
    n_dil = len(DILS)
    dil_refs = [rest[i * n_dil:(i + 1) * n_dil] for i in range(3)]
    qb_scr, kb_scr, vb_scr = rest[3 * n_dil:]
    tt = x_ref.shape[0]
    shift, scale = mod_ref[0:1, :], mod_ref[1:2, :]
    h = _rms_rows(x_ref[...]) * gn_ref[...] * (1.0 + scale) + shift
    proj = jnp.dot(h.astype(BF16), w_ref[...], preferred_element_type=F32)

    ang = pos_ref[...].astype(F32) * invf_ref[...]
    cs, sn = jnp.cos(ang), jnp.sin(ang)
    lane = lax.broadcasted_iota(I32, (1, LANES), 1) % HEAD_DIM
    s_lo = jnp.where(lane < ROT_DIM // 2, -sn, 0.0)
    s_hi = jnp.where((lane >= ROT_DIM // 2) & (lane < ROT_DIM), sn, 0.0)
    bd = bd_ref[...]

    def norm_rope(col0, width, out_ref, scr=None):
        for j in range(width // LANES):
            c = col0 + j * LANES
            t = proj[:, c:c + LANES]
            sq = t * t
            hi = sq.astype(BF16)
            lo = (sq - hi.astype(F32)).astype(BF16)
            ss = (jnp.dot(hi, bd, preferred_element_type=F32)
                  + jnp.dot(lo, bd, preferred_element_type=F32))
            t = t * lax.rsqrt(ss * (1.0 / HEAD_DIM) + EPS) * gcol_ref[:, c:c + LANES]
            t = (t * cs + pltpu.roll(t, LANES - ROT_DIM // 2, 1) * s_lo
                 + pltpu.roll(t, ROT_DIM // 2, 1) * s_hi)
            out_ref[:, j * LANES:(j + 1) * LANES] = t.astype(BF16)
            if scr is not None:
                scr[j] = t

    norm_rope(COL_QA, MIX_A, qa_ref)
    norm_rope(COL_KA, KV_A, ka_ref)
    norm_rope(COL_QB, MIX_B, qb_ref, qb_scr)
    norm_rope(COL_KB, MIX_B, kb_ref, kb_scr)
    va_ref[...] = proj[:, COL_VA:COL_VA + KV_A].astype(BF16)
    vb_ref[...] = proj[:, COL_VB:COL_VB + MIX_B].astype(BF16)
    for j in range(N_PAIRS):
        vb_scr[j] = proj[:, COL_VB + j * LANES:COL_VB + (j + 1) * LANES]
    for scr, outs in zip((qb_scr, kb_scr, vb_scr), dil_refs):
        for dil, out in zip(DILS, outs):
            for r in range(dil):
                for j in range(N_PAIRS):
                    c = r * MIX_B + j * LANES
                    out[:, c:c + LANES] = scr[j, pl.ds(r, tt // dil, stride=dil), :].astype(BF16)


def _inproj(xf, mod, pos, g_norm, w_in_p, gcol, invf, bd, seq):
    n, d = xf.shape
    tt = TT_PROJ
    tiles_per_seq = seq // tt
    shapes = [(n, w) for w in (MIX_A, KV_A, KV_A, MIX_B, MIX_B, MIX_B)]
    shapes += [(n // dil, dil * MIX_B) for _ in range(3) for dil in DILS]
    full = lambda shape: pl.BlockSpec(shape, lambda i: (0,) * len(shape))
    return pl.pallas_call(
        _inproj_kernel,
        grid=(n // tt,),
        in_specs=[pl.BlockSpec((tt, d), lambda i: (i, 0)),
                  pl.BlockSpec((None, 6, d), lambda i: (i // tiles_per_seq, 0, 0)),
                  pl.BlockSpec((tt, 1), lambda i: (i, 0)),
                  full((1, d)), full((d, IN_WIDTH)), full((1, IN_WIDTH)),
                  full((1, LANES)), full((LANES, LANES))],
        out_specs=[pl.BlockSpec((tt * r // n, w), lambda i: (i, 0)) for r, w in shapes],
        out_shape=[jax.ShapeDtypeStruct(s, BF16) for s in shapes],
        scratch_shapes=[pltpu.VMEM((N_PAIRS, tt, LANES), F32)] * 3,
        compiler_params=_params("parallel"),
        name="inproj",
    )(xf, mod, pos, g_norm, w_in_p, gcol, invf, bd)


def _attn_kernel(*refs, kv_shared, max_dist, use_prev, has_sinks, want_lse):
    refs = list(refs)
    sink_ref = refs.pop(0) if has_sinks else None
    q_ref = refs.pop(0)
    kp_ref = refs.pop(0) if use_prev else None
    kc_ref = refs.pop(0)
    vp_ref = refs.pop(0) if use_prev else None
    vc_ref = refs.pop(0)
    o_ref = refs.pop(0)
    lse_ref = refs.pop(0) if want_lse else None

    blk = pl.program_id(2)
    nq = 2 * BLOCK
    nk = 2 * BLOCK if use_prev else BLOCK
    qpos = lax.broadcasted_iota(I32, (nq, nk), 0) % BLOCK
    kpos = lax.broadcasted_iota(I32, (nq, nk), 1)
    if use_prev:
        dist = qpos + BLOCK - kpos
        valid = (dist >= 0) & (dist <= max_dist) & ((kpos >= BLOCK) | (blk > 0))
    else:
        dist = qpos - kpos
        valid = (dist >= 0) & (dist <= max_dist)
    lane = lax.broadcasted_iota(I32, (nq, LANES), 1)
    row = lax.broadcasted_iota(I32, (nq, LANES), 0)
    own_half = (lane < HEAD_DIM) == (row < BLOCK)
    left_lanes = lax.broadcasted_iota(I32, (BLOCK, LANES), 1) < HEAD_DIM
    lane8 = lax.broadcasted_iota(I32, (BLOCK, 2 * N_PAIRS), 1)
    lse_blk = jnp.zeros((BLOCK, 2 * N_PAIRS), F32)

    for p in range(N_PAIRS):
        cq = slice(p * LANES, (p + 1) * LANES)
        ck = slice(0, LANES) if kv_shared else cq
        qp = q_ref[:, cq]
        qs = jnp.concatenate([qp, qp], axis=0)
        qs = jnp.where(own_half, qs, jnp.zeros_like(qs))
        if use_prev:
            k = jnp.concatenate([kp_ref[:, ck], kc_ref[:, ck]], axis=0)
            v = jnp.concatenate([vp_ref[:, ck], vc_ref[:, ck]], axis=0)
        else:
            k, v = kc_ref[:, ck], vc_ref[:, ck]
        s = lax.dot_general(qs, k, (((1,), (1,)), ((), ())), preferred_element_type=F32)
        s = jnp.where(valid, s, NEG)
        m = jnp.max(s, axis=-1, keepdims=True)
        if has_sinks:
            rows1 = lax.broadcasted_iota(I32, (nq, 1), 0)
            sink = jnp.where(rows1 < BLOCK, sink_ref[2 * p], sink_ref[2 * p + 1])
            m = jnp.maximum(m, sink)
        e = jnp.exp(s - m)
        l = jnp.sum(e, axis=-1, keepdims=True)
        if has_sinks:
            l = l + jnp.exp(sink - m)
        o = jnp.dot(e.astype(BF16), v, preferred_element_type=F32) / l
        o_ref[:, cq] = jnp.where(left_lanes, o[:BLOCK], o[BLOCK:]).astype(BF16)
        if want_lse:
            lse = m + jnp.log(l)
            lse_blk = (lse_blk + jnp.where(lane8 == 2 * p, lse[:BLOCK], 0.0)
                       + jnp.where(lane8 == 2 * p + 1, lse[BLOCK:], 0.0))
    if want_lse:
        lse_ref[...] = lse_blk


def _attention(q, k, v, *, nbatch, seq, dil, max_dist, kv_shared, sinks=None, want_lse):
    length = seq // dil
    nblk = length // BLOCK
    use_prev = nblk > 1
    kw = k.shape[1] // dil
    view = lambda t: t.reshape(nbatch, length, t.shape[1])
    cur = lambda b, r, i: (b, i, r)
    prev = lambda b, r, i: (b, jnp.maximum(i - 1, 0), r)
    in_specs, args = [], []
    if sinks is not None:
        in_specs.append(pl.BlockSpec(memory_space=pltpu.SMEM))
        args.append(sinks)
    in_specs.append(pl.BlockSpec((None, BLOCK, MIX_B), cur))
    args.append(view(q))
    for t in (k, v):
        if use_prev:
            in_specs.append(pl.BlockSpec((None, BLOCK, kw), prev))
            args.append(view(t))
        in_specs.append(pl.BlockSpec((None, BLOCK, kw), cur))
        args.append(view(t))
    out_specs = [pl.BlockSpec((None, BLOCK, MIX_B), cur)]
    out_shape = [jax.ShapeDtypeStruct((nbatch, length, dil * MIX_B), BF16)]
    if want_lse:
        out_specs.append(pl.BlockSpec((None, None, BLOCK, N_HEADS_B), lambda b, r, i: (b, r, i, 0)))
        out_shape.append(jax.ShapeDtypeStruct((nbatch, dil, length, N_HEADS_B), F32))
    outs = pl.pallas_call(
        functools.partial(_attn_kernel, kv_shared=kv_shared, max_dist=max_dist, use_prev=use_prev,
                          has_sinks=sinks is not None, want_lse=want_lse),
        grid=(nbatch, dil, nblk),
        in_specs=in_specs, out_specs=out_specs, out_shape=out_shape,
        compiler_params=_params("parallel", "parallel", "arbitrary"),
        name=f"attn_d{dil}" + ("_swa" if kv_shared else ""),
    )(*args)
    o = outs[0].reshape(nbatch * length, dil * MIX_B)
    if not want_lse:
        return o, None
    lse = outs[1].transpose(0, 2, 1, 3).reshape(nbatch * seq, N_HEADS_B)
    return o, lse


def _expand_heads(w, width):
    head = lax.broadcasted_iota(I32, (1, width), 1) // HEAD_DIM
    out = jnp.zeros((w.shape[0], width), F32)
    for hd in range(w.shape[1]):
        out = jnp.where(head == hd, w[:, hd:hd + 1], out)
    return out


def _outproj_kernel(x_ref, mod_ref, oa_ref, ob1_ref, ob2_ref, ob3_ref, l1_ref, l2_ref, l3_ref,
                    goa_ref, gob_ref, wo_ref, gf_ref, wgs_ref, wus_ref, wds_ref,
                    h2_ref, base_ref, ob_scr):
    tt = x_ref.shape[0]
    gate_a = mod_ref[2:3, :]
    shift_m, scale_m, gate_m = mod_ref[3:4, :], mod_ref[4:5, :], mod_ref[5:6, :]

    def token_major(ref, dil):
        if dil == 1:
            return ref[...].astype(F32)
        for r in range(dil):
            for j in range(N_PAIRS):
                c = r * MIX_B + j * LANES
                ob_scr[j, pl.ds(r, tt // dil, stride=dil), :] = ref[:, c:c + LANES].astype(F32)
        return jnp.concatenate([ob_scr[j] for j in range(N_PAIRS)], axis=1)

    l1, l2, l3 = l1_ref[...], l2_ref[...], l3_ref[...]
    mx = jnp.maximum(jnp.maximum(l1, l2), l3)
    e1, e2, e3 = jnp.exp(l1 - mx), jnp.exp(l2 - mx), jnp.exp(l3 - mx)
    den = e1 + e2 + e3
    dils = [dil for _, dil in DILATED_BRANCHES]
    ob = _expand_heads(e1 / den, MIX_B) * token_major(ob1_ref, dils[0])
    ob = ob + _expand_heads(e2 / den, MIX_B) * token_major(ob2_ref, dils[1])
    ob = ob + _expand_heads(e3 / den, MIX_B) * token_major(ob3_ref, dils[2])
    ob = _rms_rows(ob) * gob_ref[...]
    oa = _rms_rows(oa_ref[...].astype(F32)) * goa_ref[...]
    y = (jnp.dot(oa.astype(BF16), wo_ref[0:MIX_A, :], preferred_element_type=F32)
         + jnp.dot(ob.astype(BF16), wo_ref[MIX_A:MIX_A + MIX_B, :], preferred_element_type=F32))
    x1 = x_ref[...] + gate_a * y
    h2 = _rms_rows(x1) * gf_ref[...] * (1.0 + scale_m) + shift_m
    h2_ref[...] = h2
    hb = h2.astype(BF16)
    act = (_silu(jnp.dot(hb, wgs_ref[...], preferred_element_type=F32))
           * jnp.dot(hb, wus_ref[...], preferred_element_type=F32))
    shared = jnp.dot(act.astype(BF16), wds_ref[...], preferred_element_type=F32)
    base_ref[...] = x1 + gate_m * shared


def _outproj(xf, mod, oa, obs, lses, goa, gob, wo_p, gf, wgs, wus, wds, seq):
    n, d = xf.shape
    tt = TT_PROJ // 2
    tiles_per_seq = seq // tt
    tile = lambda w: pl.BlockSpec((tt, w), lambda i: (i, 0))
    full = lambda shape: pl.BlockSpec(shape, lambda i: (0,) * len(shape))
    sd = wgs.shape[1]
    dilated = [pl.BlockSpec((tt // dil, dil * MIX_B), lambda i: (i, 0)) for _, dil in DILATED_BRANCHES]
    return pl.pallas_call(
        _outproj_kernel,
        grid=(n // tt,),
        in_specs=[tile(d), pl.BlockSpec((None, 6, d), lambda i: (i // tiles_per_seq, 0, 0)),
                  tile(MIX_A), *dilated,
                  tile(N_HEADS_B), tile(N_HEADS_B), tile(N_HEADS_B),
                  full((1, MIX_A)), full((1, MIX_B)), full((MIX_A + MIX_B, d)), full((1, d)),
                  full((d, sd)), full((d, sd)), full((sd, d))],
        out_specs=[tile(d), tile(d)],
        out_shape=[jax.ShapeDtypeStruct((n, d), F32), jax.ShapeDtypeStruct((n, d), F32)],
        scratch_shapes=[pltpu.VMEM((N_PAIRS, tt, LANES), F32)],
        compiler_params=_params("parallel"),
        name="outproj",
    )(xf, mod, oa, *obs, *lses, goa, gob, wo_p, gf, wgs, wus, wds)


def _router_kernel(h_ref, wrt_ref, bias_ref, e_ref, g_ref, cnt_ref):
    tt = h_ref.shape[0]
    logits = lax.dot_general(wrt_ref[...], h_ref[...].astype(BF16), (((1,), (1,)), ((), ())),
                             preferred_element_type=F32)
    scores = 1.0 / (1.0 + jnp.exp(-logits))
    biased = scores + bias_ref[...]
    ninf = -jnp.inf

    j32 = lax.broadcasted_iota(I32, (GROUP_SIZE, tt), 0).astype(F32)
    grp = []
    for g in range(N_GROUPS):
        bg = biased[g * GROUP_SIZE:(g + 1) * GROUP_SIZE, :]
        m1 = jnp.max(bg, axis=0, keepdims=True)
        i1 = jnp.min(jnp.where(bg == m1, j32, float(GROUP_SIZE)), axis=0, keepdims=True)
        m2 = jnp.max(jnp.where(j32 == i1, ninf, bg), axis=0, keepdims=True)
        grp.append(m1 + m2)
    grp = jnp.concatenate(grp, axis=0)
    g8 = lax.broadcasted_iota(I32, (N_GROUPS, tt), 0).astype(F32)
    chosen = jnp.zeros((N_GROUPS, tt), F32)
    for _ in range(TOPK_GROUPS):
        gm = jnp.max(grp, axis=0, keepdims=True)
        gi = jnp.min(jnp.where(grp == gm, g8, float(N_GROUPS)), axis=0, keepdims=True)
        hit = g8 == gi
        chosen = jnp.where(hit, 1.0, chosen)
        grp = jnp.where(hit, ninf, grp)
    masked = jnp.concatenate(
        [jnp.where(chosen[g:g + 1, :] > 0.0, biased[g * GROUP_SIZE:(g + 1) * GROUP_SIZE, :], ninf)
         for g in range(N_GROUPS)], axis=0)

    eio = lax.broadcasted_iota(I32, (N_EXPERTS, tt), 0).astype(F32)
    picked = jnp.zeros((N_EXPERTS, tt), F32)
    es, gs = [], []
    for _ in range(TOP_K):
        m = jnp.max(masked, axis=0, keepdims=True)
        idx = jnp.min(jnp.where(masked == m, eio, float(N_EXPERTS)), axis=0, keepdims=True)
        hit = eio == idx
        gs.append(jnp.sum(jnp.where(hit, scores, 0.0), axis=0, keepdims=True))
        es.append(idx)
        picked = jnp.where(hit, 1.0, picked)
        masked = jnp.where(hit, ninf, masked)
    gates = jnp.concatenate(gs, axis=0)
    e_ref[...] = jnp.concatenate(es, axis=0).astype(I32)
    g_ref[...] = gates / jnp.sum(gates, axis=0, keepdims=True) * ROUTED_SCALE

    @pl.when(pl.program_id(0) == 0)
    def _():
        cnt_ref[...] = jnp.zeros_like(cnt_ref)
    cnt_ref[...] += jnp.sum(picked, axis=1, keepdims=True)


def _router(h2, wrt, bias_col):
    n, d = h2.shape
    tt = TT_ROUTE
    return pl.pallas_call(
        _router_kernel,
        grid=(n // tt,),
        in_specs=[pl.BlockSpec((tt, d), lambda i: (i, 0)),
                  pl.BlockSpec((N_EXPERTS, d), lambda i: (0, 0)),
                  pl.BlockSpec((N_EXPERTS, 1), lambda i: (0, 0))],
        out_specs=[pl.BlockSpec((TOP_K, tt), lambda i: (0, i)),
                   pl.BlockSpec((TOP_K, tt), lambda i: (0, i)),
                   pl.BlockSpec((N_EXPERTS, 1), lambda i: (0, 0))],
        out_shape=[jax.ShapeDtypeStruct((TOP_K, n), I32),
                   jax.ShapeDtypeStruct((TOP_K, n), F32),
                   jax.ShapeDtypeStruct((N_EXPERTS, 1), F32)],
        compiler_params=_params("arbitrary"),
        name="router",
    )(h2, wrt, bias_col)


def _rank_kernel(e_ref, pstart_ref, tri_ref, dest_ref, carry_ref):
    tt = e_ref.shape[1]

    @pl.when(pl.program_id(0) == 0)
    def _():
        carry_ref[...] = pstart_ref[...]

    e = e_ref[...]
    eio = lax.broadcasted_iota(I32, (N_EXPERTS, tt), 0)
    mask = jnp.zeros((N_EXPERTS, tt), F32)
    for k in range(TOP_K):
        mask = jnp.where(eio == e[k:k + 1, :], 1.0, mask)
    incl = jnp.dot(mask.astype(BF16), tri_ref[...], preferred_element_type=F32)
    pos = incl - mask + carry_ref[...]
    dest = [jnp.sum(jnp.where(eio == e[k:k + 1, :], pos, 0.0), axis=0, keepdims=True)
            for k in range(TOP_K)]
    dest_ref[...] = jnp.concatenate(dest, axis=0).astype(I32)
    carry_ref[...] += incl[:, tt - 1:tt]


def _rank(top_e_t, pstart_col, tri):
    n = top_e_t.shape[1]
    tt = TT_ROUTE
    return pl.pallas_call(
        _rank_kernel,
        grid=(n // tt,),
        in_specs=[pl.BlockSpec((TOP_K, tt), lambda i: (0, i)),
                  pl.BlockSpec((N_EXPERTS, 1), lambda i: (0, 0)),
                  pl.BlockSpec((tt, tt), lambda i: (0, 0))],
        out_specs=pl.BlockSpec((TOP_K, tt), lambda i: (0, i)),
        out_shape=jax.ShapeDtypeStruct((TOP_K, n), I32),
        scratch_shapes=[pltpu.VMEM((N_EXPERTS, 1), F32)],
        compiler_params=_params("arbitrary"),
        name="rank",
    )(top_e_t, pstart_col, tri)


def _row_copy(src_ref, src_row, dst_ref, dst_row, sem):
    return pltpu.make_async_copy(src_ref.at[pl.ds(src_row, 1)], dst_ref.at[pl.ds(dst_row, 1)], sem)


SC_CORES, SC_SUBCORES, SC_LANES = 2, 16, 16
SC_WORKERS = SC_CORES * SC_SUBCORES
INVERT_CHUNK = 8192


def _invert(dest_flat, n_tokens, rows):
    per = rows // SC_WORKERS
    n_assign = dest_flat.shape[0]
    assert rows % (SC_WORKERS * SC_LANES) == 0 and n_assign % INVERT_CHUNK == 0
    assert n_tokens & (n_tokens - 1) == 0

    @functools.partial(
        pl.kernel, mesh=plsc.VectorSubcoreMesh(core_axis_name="c", subcore_axis_name="s"),
        out_type=jax.ShapeDtypeStruct((rows,), I32),
        scratch_types=[pltpu.VMEM((INVERT_CHUNK,), I32), pltpu.VMEM((per,), I32)],
        compiler_params=pltpu.CompilerParams(needs_layout_passes=False))
    def invert(dest_hbm, out_hbm, staged, local):
        base = (lax.axis_index("s") * SC_CORES + lax.axis_index("c")) * per

        @pl.loop(0, per, step=SC_LANES)
        def _(i):
            local[pl.ds(i, SC_LANES)] = jnp.zeros((SC_LANES,), I32)

        lane = lax.iota(I32, SC_LANES)

        @pl.loop(0, n_assign // INVERT_CHUNK)
        def _(c):
            pltpu.sync_copy(dest_hbm.at[pl.ds(c * INVERT_CHUNK, INVERT_CHUNK)], staged)

            @pl.loop(0, INVERT_CHUNK, step=SC_LANES)
            def _(i):
                rel = staged[pl.ds(i, SC_LANES)] - base
                mine = (rel >= 0) & (rel < per)
                tok = (c * INVERT_CHUNK + i + lane) & (n_tokens - 1)
                plsc.store_scatter(local, [jnp.where(mine, rel, 0)], tok, mask=mine)

        pltpu.sync_copy(local, out_hbm.at[pl.ds(base, per)])

    return invert(dest_flat)


def _experts_kernel(be_ref, nv_ref, tok0_ref, tokn_ref, h_ref, wg_ref, wu_ref, wd_ref, ys_ref,
                    x0, x1, wgb, wub, wdb, sem):
    j = pl.program_id(0)
    nv = nv_ref[0]
    bm = x0.shape[0]

    def gather(tok_ref, buf, s):
        for i in range(bm):
            _row_copy(h_ref, tok_ref[i], buf, i, sem.at[s]).start()

    def wait_rows(buf, s):
        pltpu.make_async_copy(h_ref.at[pl.ds(0, bm)], buf, sem.at[s]).wait()

    @pl.when(j == 0)
    def _():
        gather(tok0_ref, x0, 0)

    fresh = jnp.logical_or(j == 0, be_ref[j] != be_ref[jnp.maximum(j - 1, 0)])

    @pl.when(jnp.logical_and(j < nv, fresh))
    def _():
        wgb[...] = wg_ref[...].astype(BF16)
        wub[...] = wu_ref[...].astype(BF16)
        wdb[...] = wd_ref[...].astype(BF16)

    def step(cur, nxt, s):
        wait_rows(cur, s)
        gather(tokn_ref, nxt, 1 - s)
        xb = cur[...].astype(BF16)
        act = (_silu(jnp.dot(xb, wgb[...], preferred_element_type=F32))
               * jnp.dot(xb, wub[...], preferred_element_type=F32))
        ys_ref[...] = jnp.dot(act.astype(BF16), wdb[...], preferred_element_type=F32)

    @pl.when(jnp.logical_and(j < nv, j % 2 == 0))
    def _():
        step(x0, x1, 0)

    @pl.when(jnp.logical_and(j < nv, j % 2 == 1))
    def _():
        step(x1, x0, 1)

    @pl.when(jnp.logical_and(j == nv, j % 2 == 0))
    def _():
        wait_rows(x0, 0)

    @pl.when(jnp.logical_and(j == nv, j % 2 == 1))
    def _():
        wait_rows(x1, 1)


def _experts(block_e, n_valid, row_tok, h2, wg, wu, wd):
    rows = row_tok.shape[0]
    d = h2.shape[1]
    bm = EXPERT_ROWS
    n_blocks = rows // bm
    f = wg.shape[2]
    row_blk = lambda j, be, nv: (jnp.minimum(j, nv[0] - 1), 0)
    w_blk = lambda j, be, nv: (be[j], 0, 0)
    return pl.pallas_call(
        _experts_kernel,
        grid_spec=pltpu.PrefetchScalarGridSpec(
            num_scalar_prefetch=2,
            grid=(n_blocks,),
            in_specs=[pl.BlockSpec((bm,), lambda j, be, nv: (0,), memory_space=pltpu.SMEM),
                      pl.BlockSpec((bm,), lambda j, be, nv: (jnp.minimum(j + 1, n_blocks - 1),),
                                   memory_space=pltpu.SMEM),
                      pl.BlockSpec(memory_space=pl.ANY),
                      pl.BlockSpec((None, d, f), w_blk),
                      pl.BlockSpec((None, d, f), w_blk),
                      pl.BlockSpec((None, f, d), w_blk)],
            out_specs=pl.BlockSpec((bm, d), row_blk),
            scratch_shapes=[pltpu.VMEM((bm, d), F32), pltpu.VMEM((bm, d), F32),
                            pltpu.VMEM((d, f), BF16), pltpu.VMEM((d, f), BF16),
                            pltpu.VMEM((f, d), BF16), pltpu.SemaphoreType.DMA((2,))]),
        out_shape=jax.ShapeDtypeStruct((rows, d), F32),
        compiler_params=_params("arbitrary"),
        name="experts",
    )(block_e, n_valid, row_tok, row_tok, h2, wg, wu, wd)


def _combine_kernel(dest_ref, gates_ref, base_ref, mod_ref, ys_ref, out_ref, buf, sem):
    tt = base_ref.shape[0]

    def issue(t, carry):
        for k in range(TOP_K):
            _row_copy(ys_ref, dest_ref[k, t], buf.at[k], t, sem).start()
        return carry

    def drain(t, carry):
        for k in range(TOP_K):
            _row_copy(ys_ref, dest_ref[k, t], buf.at[k], t, sem).wait()
        return carry

    lax.fori_loop(0, tt, issue, 0)
    lax.fori_loop(0, tt, drain, 0)
    gates = gates_ref[...]
    routed = gates[:, 0:1] * buf[0]
    for k in range(1, TOP_K):
        routed = routed + gates[:, k:k + 1] * buf[k]
    out_ref[...] = base_ref[...] + mod_ref[5:6, :] * routed


def _combine(dest_t, gates, base, mod, ys, seq):
    n, d = base.shape
    tt = TT_COMBINE
    tiles_per_seq = seq // tt
    return pl.pallas_call(
        _combine_kernel,
        grid=(n // tt,),
        in_specs=[pl.BlockSpec((TOP_K, tt), lambda i: (0, i), memory_space=pltpu.SMEM),
                  pl.BlockSpec((tt, TOP_K), lambda i: (i, 0)),
                  pl.BlockSpec((tt, d), lambda i: (i, 0)),
                  pl.BlockSpec((None, 6, d), lambda i: (i // tiles_per_seq, 0, 0)),
                  pl.BlockSpec(memory_space=pl.ANY)],
        out_specs=pl.BlockSpec((tt, d), lambda i: (i, 0)),
        out_shape=jax.ShapeDtypeStruct((n, d), F32),
        scratch_shapes=[pltpu.VMEM((TOP_K, tt, d), F32), pltpu.SemaphoreType.DMA],
        compiler_params=_params("arbitrary"),
        name="combine",
    )(dest_t, gates, base, mod, ys)


def _layer(x, mod, pos, rope, p):
    nbatch, seq, d = x.shape
    n = nbatch * seq
    xf = x.reshape(n, d)
    invf, bd = rope

    perm = np.concatenate([np.arange(h * HEAD_DIM, (h + 1) * HEAD_DIM) for h in PAIR_ORDER_A])
    w_in = p["w_in"]
    w_in_p = jnp.concatenate([w_in[:, :MIX_A][:, perm], w_in[:, MIX_A:]], axis=1).astype(BF16)
    ones = lambda w: jnp.ones((w,), F32)
    qscale = HEAD_DIM ** -0.5
    gcol = jnp.concatenate([jnp.tile(p["g_q_a"], N_HEADS_A) * qscale, jnp.tile(p["g_k_a"], N_KV_A),
                            ones(KV_A), jnp.tile(p["g_q_b"], N_HEADS_B) * qscale,
                            jnp.tile(p["g_k_b"], N_HEADS_B), ones(MIX_B)]).reshape(1, IN_WIDTH)
    proj = _inproj(xf, mod, pos, p["g_norm_mix"].reshape(1, d), w_in_p, gcol, invf, bd, seq)
    qa, ka, va = proj[:3]
    qkv_b = {1: proj[3:6]}
    for j, dil in enumerate(DILS):
        qkv_b[dil] = [proj[6 + t * len(DILS) + j] for t in range(3)]

    sinks_p = p["sinks_a"][np.array(PAIR_ORDER_A)]
    oa, _ = _attention(qa, ka, va, nbatch=nbatch, seq=seq, dil=1, max_dist=WINDOW_A - 1,
                       kv_shared=True, sinks=sinks_p, want_lse=False)
    obs, lses = [], []
    for window, dil in DILATED_BRANCHES:
        o, lse = _attention(*qkv_b[dil], nbatch=nbatch, seq=seq, dil=dil, max_dist=window // dil,
                            kv_shared=False, want_lse=True)
        obs.append(o)
        lses.append(lse)

    goa = p["g_out_a"][perm].reshape(1, MIX_A)
    w_out = p["w_out"]
    wo_p = jnp.concatenate([w_out[:MIX_A][perm], w_out[MIX_A:]], axis=0).astype(BF16)
    h2, base = _outproj(xf, mod, oa, obs, lses, goa, p["g_out_b"].reshape(1, MIX_B), wo_p,
                        p["g_norm_ffn"].reshape(1, d), p["w_gate_s"].astype(BF16),
                        p["w_up_s"].astype(BF16), p["w_down_s"].astype(BF16), seq)

    top_e_t, gates_t, counts = _router(h2, p["w_router"].T.astype(BF16),
                                       p["router_bias"].reshape(N_EXPERTS, 1))
    bm = EXPERT_ROWS
    counts = counts.reshape(N_EXPERTS).astype(I32)
    padded = (counts + bm - 1) // bm * bm
    pends = jnp.cumsum(padded)
    pstarts = pends - padded
    rows = n * TOP_K + N_EXPERTS * bm
    n_blocks = rows // bm
    n_valid = (pends[-1] // bm).astype(I32)
    blk = jnp.minimum(jnp.arange(n_blocks, dtype=I32), n_valid - 1)
    block_e = jnp.sum((pends[None, :] <= (blk * bm)[:, None]).astype(I32), axis=1)
    block_e = jnp.minimum(block_e, N_EXPERTS - 1)

    tri = (np.arange(TT_ROUTE)[:, None] <= np.arange(TT_ROUTE)[None, :])
    dest_t = _rank(top_e_t, pstarts.astype(F32).reshape(N_EXPERTS, 1), jnp.asarray(tri, BF16))
    n_valid = n_valid.reshape(1)
    row_tok = _invert(dest_t.reshape(n * TOP_K), n, rows)
    ys = _experts(block_e, n_valid, row_tok, h2, p["w_gate_e"], p["w_up_e"], p["w_down_e"])
    out = _combine(dest_t, gates_t.T, base, mod, ys, seq)
    return out.reshape(nbatch, seq, d)


def kernel(x, c, positions, w_ada, b_ada, g_norm_mix, w_in, g_q_a, g_k_a, sinks_a, g_q_b, g_k_b,
           g_out_a, g_out_b, w_out, g_norm_ffn, w_router, router_bias, w_gate_e, w_up_e, w_down_e,
           w_gate_s, w_up_s, w_down_s):
    nbatch, seq, d = x.shape
    depth = w_ada.shape[0]
    params = dict(g_norm_mix=g_norm_mix, w_in=w_in, g_q_a=g_q_a, g_k_a=g_k_a, sinks_a=sinks_a,
                  g_q_b=g_q_b, g_k_b=g_k_b, g_out_a=g_out_a, g_out_b=g_out_b, w_out=w_out,
                  g_norm_ffn=g_norm_ffn, w_router=w_router, router_bias=router_bias,
                  w_gate_e=w_gate_e, w_up_e=w_up_e, w_down_e=w_down_e, w_gate_s=w_gate_s,
                  w_up_s=w_up_s, w_down_s=w_down_s)
    j = np.arange(LANES) % HEAD_DIM
    inv = ROPE_THETA ** (-jnp.arange(0, ROT_DIM, 2, dtype=F32) / ROT_DIM)
    invf = jnp.where(j < ROT_DIM, inv[j % (ROT_DIM // 2)], 0.0).astype(F32).reshape(1, LANES)
    bd = jnp.asarray((np.arange(LANES)[:, None] // HEAD_DIM) == (np.arange(LANES)[None, :] // HEAD_DIM),
                     BF16)
    pos = positions.reshape(nbatch * seq, 1).astype(I32)
    for l in range(depth):
        mod = _adaln(c.astype(F32), w_ada[l], b_ada[l]).reshape(nbatch, 6, d)
        x = _layer(x, mod, pos, (invf, bd), {k: v[l] for k, v in params.items()})
    return x
```

```python
import functools

import numpy as np
import jax
import jax.numpy as jnp
from jax import lax
from jax.experimental import pallas as pl
from jax.experimental.pallas import tpu as pltpu
from jax.experimental.pallas import tpu_sc as plsc

F32 = jnp.float32
BF16 = jnp.bfloat16
I32 = jnp.int32

HEAD_DIM = 64
N_HEADS_A = 8
N_KV_A = 2
WINDOW_A = 128
N_HEADS_B = 8
DILATED_BRANCHES = ((128, 1), (512, 4), (2048, 16))
DILS = tuple(dil for _, dil in DILATED_BRANCHES if dil > 1)
BLOCK = 128
ROT_DIM = HEAD_DIM // 4
ROPE_THETA = 500000.0
MIX_A = N_HEADS_A * HEAD_DIM
KV_A = N_KV_A * HEAD_DIM
MIX_B = N_HEADS_B * HEAD_DIM
N_EXPERTS = 256
TOP_K = 8
N_GROUPS = 8
TOPK_GROUPS = 4
GROUP_SIZE = N_EXPERTS // N_GROUPS
ROUTED_SCALE = 2.5
EPS = 1e-6

LANES = 128
HEADS_PER_VREG = LANES // HEAD_DIM
N_PAIRS = MIX_A // LANES
NEG = -1e30
VMEM_LIMIT = 48 * 1024 * 1024

TT_PROJ = 512
TT_ROUTE = 256
TT_DISPATCH = 256
TT_COMBINE = 128
EXPERT_ROWS = 256
TOK_CHUNK_BLOCKS = 32

PAIR_ORDER_A = tuple(h for p in range(N_PAIRS) for h in (p, p + N_HEADS_A // N_KV_A))


def _params(*sem):
    return pltpu.CompilerParams(dimension_semantics=sem, vmem_limit_bytes=VMEM_LIMIT)


def _silu(t):
    return t / (1.0 + jnp.exp(-t))


def _rms_rows(t):
    return t * lax.rsqrt(jnp.mean(t * t, axis=-1, keepdims=True) + EPS)


def _ada_kernel(c_ref, w_ref, b_ref, o_ref):
    cond = _silu(c_ref[...])
    o_ref[...] = jnp.dot(cond.astype(BF16), w_ref[...].astype(BF16),
                         preferred_element_type=F32) + b_ref[...]


def _adaln(c, w_ada, b_ada):
    nb, d = c.shape
    width = w_ada.shape[1]
    tn = 1024
    return pl.pallas_call(
        _ada_kernel,
        grid=(width // tn,),
        in_specs=[pl.BlockSpec((nb, d), lambda j: (0, 0)),
                  pl.BlockSpec((d, tn), lambda j: (0, j)),
                  pl.BlockSpec((1, tn), lambda j: (0, j))],
        out_specs=pl.BlockSpec((nb, tn), lambda j: (0, j)),
        out_shape=jax.ShapeDtypeStruct((nb, width), F32),
        compiler_params=_params("arbitrary"),
        name="adaln",
    )(c, w_ada, b_ada.reshape(1, width))


COL_QA, COL_KA, COL_VA = 0, MIX_A, MIX_A + KV_A
COL_QB = MIX_A + 2 * KV_A
COL_KB, COL_VB = COL_QB + MIX_B, COL_QB + 2 * MIX_B
IN_WIDTH = COL_VB + MIX_B


def _inproj_kernel(x_ref, mod_ref, pos_ref, gn_ref, w_ref, gcol_ref, invf_ref, bd_ref,
                   qa_ref, ka_ref, va_ref, qb_ref, kb_ref, vb_ref, *rest):
    n_dil = len(DILS)
    dil_refs = [rest[i * n_dil:(i + 1) * n_dil] for i in range(3)]
    qb_scr, kb_scr, vb_scr = rest[3 * n_dil:]
    tt = x_ref.shape[0]
    shift, scale = mod_ref[0:1, :], mod_ref[1:2, :]
    h = _rms_rows(x_ref[...]) * gn_ref[...] * (1.0 + scale) + shift
    proj = jnp.dot(h.astype(BF16), w_ref[...], preferred_element_type=F32)

    ang = pos_ref[...].astype(F32) * invf_ref[...]
    cs, sn = jnp.cos(ang), jnp.sin(ang)
    lane = lax.broadcasted_iota(I32, (1, LANES), 1) % HEAD_DIM
    s_lo = jnp.where(lane < ROT_DIM // 2, -sn, 0.0)
    s_hi = jnp.where((lane >= ROT_DIM // 2) & (lane < ROT_DIM), sn, 0.0)
    bd = bd_ref[...]

    def norm_rope(col0, width, out_ref, scr=None):
        for j in range(width // LANES):
            c = col0 + j * LANES
            t = proj[:, c:c + LANES]
            sq = t * t
            hi = sq.astype(BF16)
            lo = (sq - hi.astype(F32)).astype(BF16)
            ss = (jnp.dot(hi, bd, preferred_element_type=F32)
                  + jnp.dot(lo, bd, preferred_element_type=F32))
            t = t * lax.rsqrt(ss * (1.0 / HEAD_DIM) + EPS) * gcol_ref[:, c:c + LANES]
            t = (t * cs + pltpu.roll(t, LANES - ROT_DIM // 2, 1) * s_lo
                 + pltpu.roll(t, ROT_DIM // 2, 1) * s_hi)
            out_ref[:, j * LANES:(j + 1) * LANES] = t.astype(BF16)
            if scr is not None:
                scr[j] = t

    norm_rope(COL_QA, MIX_A, qa_ref)
    norm_rope(COL_KA, KV_A, ka_ref)
    norm_rope(COL_QB, MIX_B, qb_ref, qb_scr)
    norm_rope(COL_KB, MIX_B, kb_ref, kb_scr)
    va_ref[...] = proj[:, COL_VA:COL_VA + KV_A].astype(BF16)
    vb_ref[...] = proj[:, COL_VB:COL_VB + MIX_B].astype(BF16)
    for j in range(N_PAIRS):
        vb_scr[j] = proj[:, COL_VB + j * LANES:COL_VB + (j + 1) * LANES]
    for scr, outs in zip((qb_scr, kb_scr, vb_scr), dil_refs):
        for dil, out in zip(DILS, outs):
            for r in range(dil):
                for j in range(N_PAIRS):
                    c = r * MIX_B + j * LANES
                    out[:, c:c + LANES] = scr[j, pl.ds(r, tt // dil, stride=dil), :].astype(BF16)


def _inproj(xf, mod, pos, g_norm, w_in_p, gcol, invf, bd, seq):
    n, d = xf.shape
    tt = TT_PROJ
    tiles_per_seq = seq // tt
    shapes = [(n, w) for w in (MIX_A, KV_A, KV_A, MIX_B, MIX_B, MIX_B)]
    shapes += [(n // dil, dil * MIX_B) for _ in range(3) for dil in DILS]
    full = lambda shape: pl.BlockSpec(shape, lambda i: (0,) * len(shape))
    return pl.pallas_call(
        _inproj_kernel,
        grid=(n // tt,),
        in_specs=[pl.BlockSpec((tt, d), lambda i: (i, 0)),
                  pl.BlockSpec((None, 6, d), lambda i: (i // tiles_per_seq, 0, 0)),
                  pl.BlockSpec((tt, 1), lambda i: (i, 0)),
                  full((1, d)), full((d, IN_WIDTH)), full((1, IN_WIDTH)),
                  full((1, LANES)), full((LANES, LANES))],
        out_specs=[pl.BlockSpec((tt * r // n, w), lambda i: (i, 0)) for r, w in shapes],
        out_shape=[jax.ShapeDtypeStruct(s, BF16) for s in shapes],
        scratch_shapes=[pltpu.VMEM((N_PAIRS, tt, LANES), F32)] * 3,
        compiler_params=_params("parallel"),
        name="inproj",
    )(xf, mod, pos, g_norm, w_in_p, gcol, invf, bd)


def _attn_kernel(*refs, kv_shared, max_dist, use_prev, has_sinks, want_lse):
    refs = list(refs)
    sink_ref = refs.pop(0) if has_sinks else None
    q_ref = refs.pop(0)
    kp_ref = refs.pop(0) if use_prev else None
    kc_ref = refs.pop(0)
    vp_ref = refs.pop(0) if use_prev else None
    vc_ref = refs.pop(0)
    o_ref = refs.pop(0)
    lse_ref = refs.pop(0) if want_lse else None

    blk = pl.program_id(2)
    nq = 2 * BLOCK
    nk = 2 * BLOCK if use_prev else BLOCK
    qpos = lax.broadcasted_iota(I32, (nq, nk), 0) % BLOCK
    kpos = lax.broadcasted_iota(I32, (nq, nk), 1)
    if use_prev:
        dist = qpos + BLOCK - kpos
        valid = (dist >= 0) & (dist <= max_dist) & ((kpos >= BLOCK) | (blk > 0))
    else:
        dist = qpos - kpos
        valid = (dist >= 0) & (dist <= max_dist)
    lane = lax.broadcasted_iota(I32, (nq, LANES), 1)
    row = lax.broadcasted_iota(I32, (nq, LANES), 0)
    own_half = (lane < HEAD_DIM) == (row < BLOCK)
    left_lanes = lax.broadcasted_iota(I32, (BLOCK, LANES), 1) < HEAD_DIM
    lane8 = lax.broadcasted_iota(I32, (BLOCK, 2 * N_PAIRS), 1)
    lse_blk = jnp.zeros((BLOCK, 2 * N_PAIRS), F32)

    for p in range(N_PAIRS):
        cq = slice(p * LANES, (p + 1) * LANES)
        ck = slice(0, LANES) if kv_shared else cq
        qp = q_ref[:, cq]
        qs = jnp.concatenate([qp, qp], axis=0)
        qs = jnp.where(own_half, qs, jnp.zeros_like(qs))
        if use_prev:
            k = jnp.concatenate([kp_ref[:, ck], kc_ref[:, ck]], axis=0)
            v = jnp.concatenate([vp_ref[:, ck], vc_ref[:, ck]], axis=0)
        else:
            k, v = kc_ref[:, ck], vc_ref[:, ck]
        s = lax.dot_general(qs, k, (((1,), (1,)), ((), ())), preferred_element_type=F32)
        s = jnp.where(valid, s, NEG)
        m = jnp.max(s, axis=-1, keepdims=True)
        if has_sinks:
            rows1 = lax.broadcasted_iota(I32, (nq, 1), 0)
            sink = jnp.where(rows1 < BLOCK, sink_ref[2 * p], sink_ref[2 * p + 1])
            m = jnp.maximum(m, sink)
        e = jnp.exp(s - m)
        l = jnp.sum(e, axis=-1, keepdims=True)
        if has_sinks:
            l = l + jnp.exp(sink - m)
        o = jnp.dot(e.astype(BF16), v, preferred_element_type=F32) / l
        o_ref[:, cq] = jnp.where(left_lanes, o[:BLOCK], o[BLOCK:]).astype(BF16)
        if want_lse:
            lse = m + jnp.log(l)
            lse_blk = (lse_blk + jnp.where(lane8 == 2 * p, lse[:BLOCK], 0.0)
                       + jnp.where(lane8 == 2 * p + 1, lse[BLOCK:], 0.0))
    if want_lse:
        lse_ref[...] = lse_blk


def _attention(q, k, v, *, nbatch, seq, dil, max_dist, kv_shared, sinks=None, want_lse):
    length = seq // dil
    nblk = length // BLOCK
    use_prev = nblk > 1
    kw = k.shape[1] // dil
    view = lambda t: t.reshape(nbatch, length, t.shape[1])
    cur = lambda b, r, i: (b, i, r)
    prev = lambda b, r, i: (b, jnp.maximum(i - 1, 0), r)
    in_specs, args = [], []
    if sinks is not None:
        in_specs.append(pl.BlockSpec(memory_space=pltpu.SMEM))
        args.append(sinks)
    in_specs.append(pl.BlockSpec((None, BLOCK, MIX_B), cur))
    args.append(view(q))
    for t in (k, v):
        if use_prev:
            in_specs.append(pl.BlockSpec((None, BLOCK, kw), prev))
            args.append(view(t))
        in_specs.append(pl.BlockSpec((None, BLOCK, kw), cur))
        args.append(view(t))
    out_specs = [pl.BlockSpec((None, BLOCK, MIX_B), cur)]
    out_shape = [jax.ShapeDtypeStruct((nbatch, length, dil * MIX_B), BF16)]
    if want_lse:
        out_specs.append(pl.BlockSpec((None, None, BLOCK, N_HEADS_B), lambda b, r, i: (b, r, i, 0)))
        out_shape.append(jax.ShapeDtypeStruct((nbatch, dil, length, N_HEADS_B), F32))
    outs = pl.pallas_call(
        functools.partial(_attn_kernel, kv_shared=kv_shared, max_dist=max_dist, use_prev=use_prev,
                          has_sinks=sinks is not None, want_lse=want_lse),
        grid=(nbatch, dil, nblk),
        in_specs=in_specs, out_specs=out_specs, out_shape=out_shape,
        compiler_params=_params("parallel", "parallel", "arbitrary"),
        name=f"attn_d{dil}" + ("_swa" if kv_shared else ""),
    )(*args)
    o = outs[0].reshape(nbatch * length, dil * MIX_B)
    if not want_lse:
        return o, None
    lse = outs[1].transpose(0, 2, 1, 3).reshape(nbatch * seq, N_HEADS_B)
    return o, lse


def _expand_heads(w, width):
    head = lax.broadcasted_iota(I32, (1, width), 1) // HEAD_DIM
    out = jnp.zeros((w.shape[0], width), F32)
    for hd in range(w.shape[1]):
        out = jnp.where(head == hd, w[:, hd:hd + 1], out)
    return out


def _outproj_kernel(x_ref, mod_ref, oa_ref, ob1_ref, ob2_ref, ob3_ref, l1_ref, l2_ref, l3_ref,
                    goa_ref, gob_ref, wo_ref, gf_ref, wgs_ref, wus_ref, wds_ref,
                    h2_ref, base_ref, ob_scr):
    tt = x_ref.shape[0]
    gate_a = mod_ref[2:3, :]
    shift_m, scale_m, gate_m = mod_ref[3:4, :], mod_ref[4:5, :], mod_ref[5:6, :]

    def token_major(ref, dil):
        if dil == 1:
            return ref[...].astype(F32)
        for r in range(dil):
            for j in range(N_PAIRS):
                c = r * MIX_B + j * LANES
                ob_scr[j, pl.ds(r, tt // dil, stride=dil), :] = ref[:, c:c + LANES].astype(F32)
        return jnp.concatenate([ob_scr[j] for j in range(N_PAIRS)], axis=1)

    l1, l2, l3 = l1_ref[...], l2_ref[...], l3_ref[...]
    mx = jnp.maximum(jnp.maximum(l1, l2), l3)
    e1, e2, e3 = jnp.exp(l1 - mx), jnp.exp(l2 - mx), jnp.exp(l3 - mx)
    den = e1 + e2 + e3
    dils = [dil for _, dil in DILATED_BRANCHES]
    ob = _expand_heads(e1 / den, MIX_B) * token_major(ob1_ref, dils[0])
    ob = ob + _expand_heads(e2 / den, MIX_B) * token_major(ob2_ref, dils[1])
    ob = ob + _expand_heads(e3 / den, MIX_B) * token_major(ob3_ref, dils[2])
    ob = _rms_rows(ob) * gob_ref[...]
    oa = _rms_rows(oa_ref[...].astype(F32)) * goa_ref[...]
    y = (jnp.dot(oa.astype(BF16), wo_ref[0:MIX_A, :], preferred_element_type=F32)
         + jnp.dot(ob.astype(BF16), wo_ref[MIX_A:MIX_A + MIX_B, :], preferred_element_type=F32))
    x1 = x_ref[...] + gate_a * y
    h2 = _rms_rows(x1) * gf_ref[...] * (1.0 + scale_m) + shift_m
    h2_ref[...] = h2
    hb = h2.astype(BF16)
    act = (_silu(jnp.dot(hb, wgs_ref[...], preferred_element_type=F32))
           * jnp.dot(hb, wus_ref[...], preferred_element_type=F32))
    shared = jnp.dot(act.astype(BF16), wds_ref[...], preferred_element_type=F32)
    base_ref[...] = x1 + gate_m * shared


def _outproj(xf, mod, oa, obs, lses, goa, gob, wo_p, gf, wgs, wus, wds, seq):
    n, d = xf.shape
    tt = TT_PROJ // 2
    tiles_per_seq = seq // tt
    tile = lambda w: pl.BlockSpec((tt, w), lambda i: (i, 0))
    full = lambda shape: pl.BlockSpec(shape, lambda i: (0,) * len(shape))
    sd = wgs.shape[1]
    dilated = [pl.BlockSpec((tt // dil, dil * MIX_B), lambda i: (i, 0)) for _, dil in DILATED_BRANCHES]
    return pl.pallas_call(
        _outproj_kernel,
        grid=(n // tt,),
        in_specs=[tile(d), pl.BlockSpec((None, 6, d), lambda i: (i // tiles_per_seq, 0, 0)),
                  tile(MIX_A), *dilated,
                  tile(N_HEADS_B), tile(N_HEADS_B), tile(N_HEADS_B),
                  full((1, MIX_A)), full((1, MIX_B)), full((MIX_A + MIX_B, d)), full((1, d)),
                  full((d, sd)), full((d, sd)), full((sd, d))],
        out_specs=[tile(d), tile(d)],
        out_shape=[jax.ShapeDtypeStruct((n, d), F32), jax.ShapeDtypeStruct((n, d), F32)],
        scratch_shapes=[pltpu.VMEM((N_PAIRS, tt, LANES), F32)],
        compiler_params=_params("parallel"),
        name="outproj",
    )(xf, mod, oa, *obs, *lses, goa, gob, wo_p, gf, wgs, wus, wds)


def _router_kernel(h_ref, wrt_ref, bias_ref, e_ref, g_ref, cnt_ref):
    tt = h_ref.shape[0]
    logits = lax.dot_general(wrt_ref[...], h_ref[...].astype(BF16), (((1,), (1,)), ((), ())),
                             preferred_element_type=F32)
    scores = 1.0 / (1.0 + jnp.exp(-logits))
    biased = scores + bias_ref[...]
    ninf = -jnp.inf

    j32 = lax.broadcasted_iota(I32, (GROUP_SIZE, tt), 0).astype(F32)
    grp = []
    for g in range(N_GROUPS):
        bg = biased[g * GROUP_SIZE:(g + 1) * GROUP_SIZE, :]
        m1 = jnp.max(bg, axis=0, keepdims=True)
        i1 = jnp.min(jnp.where(bg == m1, j32, float(GROUP_SIZE)), axis=0, keepdims=True)
        m2 = jnp.max(jnp.where(j32 == i1, ninf, bg), axis=0, keepdims=True)
        grp.append(m1 + m2)
    grp = jnp.concatenate(grp, axis=0)
    g8 = lax.broadcasted_iota(I32, (N_GROUPS, tt), 0).astype(F32)
    chosen = jnp.zeros((N_GROUPS, tt), F32)
    for _ in range(TOPK_GROUPS):
        gm = jnp.max(grp, axis=0, keepdims=True)
        gi = jnp.min(jnp.where(grp == gm, g8, float(N_GROUPS)), axis=0, keepdims=True)
        hit = g8 == gi
        chosen = jnp.where(hit, 1.0, chosen)
        grp = jnp.where(hit, ninf, grp)
    masked = jnp.concatenate(
        [jnp.where(chosen[g:g + 1, :] > 0.0, biased[g * GROUP_SIZE:(g + 1) * GROUP_SIZE, :], ninf)
         for g in range(N_GROUPS)], axis=0)

    eio = lax.broadcasted_iota(I32, (N_EXPERTS, tt), 0).astype(F32)
    picked = jnp.zeros((N_EXPERTS, tt), F32)
    es, gs = [], []
    for _ in range(TOP_K):
        m = jnp.max(masked, axis=0, keepdims=True)
        idx = jnp.min(jnp.where(masked == m, eio, float(N_EXPERTS)), axis=0, keepdims=True)
        hit = eio == idx
        gs.append(jnp.sum(jnp.where(hit, scores, 0.0), axis=0, keepdims=True))
        es.append(idx)
        picked = jnp.where(hit, 1.0, picked)
        masked = jnp.where(hit, ninf, masked)
    gates = jnp.concatenate(gs, axis=0)
    e_ref[...] = jnp.concatenate(es, axis=0).astype(I32)
    g_ref[...] = gates / jnp.sum(gates, axis=0, keepdims=True) * ROUTED_SCALE

    @pl.when(pl.program_id(0) == 0)
    def _():
        cnt_ref[...] = jnp.zeros_like(cnt_ref)
    cnt_ref[...] += jnp.sum(picked, axis=1, keepdims=True)


def _router(h2, wrt, bias_col):
    n, d = h2.shape
    tt = TT_ROUTE
    return pl.pallas_call(
        _router_kernel,
        grid=(n // tt,),
        in_specs=[pl.BlockSpec((tt, d), lambda i: (i, 0)),
                  pl.BlockSpec((N_EXPERTS, d), lambda i: (0, 0)),
                  pl.BlockSpec((N_EXPERTS, 1), lambda i: (0, 0))],
        out_specs=[pl.BlockSpec((TOP_K, tt), lambda i: (0, i)),
                   pl.BlockSpec((TOP_K, tt), lambda i: (0, i)),
                   pl.BlockSpec((N_EXPERTS, 1), lambda i: (0, 0))],
        out_shape=[jax.ShapeDtypeStruct((TOP_K, n), I32),
                   jax.ShapeDtypeStruct((TOP_K, n), F32),
                   jax.ShapeDtypeStruct((N_EXPERTS, 1), F32)],
        compiler_params=_params("arbitrary"),
        name="router",
    )(h2, wrt, bias_col)


def _rank_kernel(e_ref, pstart_ref, tri_ref, dest_ref, carry_ref):
    tt = e_ref.shape[1]

    @pl.when(pl.program_id(0) == 0)
    def _():
        carry_ref[...] = pstart_ref[...]

    e = e_ref[...]
    eio = lax.broadcasted_iota(I32, (N_EXPERTS, tt), 0)
    mask = jnp.zeros((N_EXPERTS, tt), F32)
    for k in range(TOP_K):
        mask = jnp.where(eio == e[k:k + 1, :], 1.0, mask)
    incl = jnp.dot(mask.astype(BF16), tri_ref[...], preferred_element_type=F32)
    pos = incl - mask + carry_ref[...]
    dest = [jnp.sum(jnp.where(eio == e[k:k + 1, :], pos, 0.0), axis=0, keepdims=True)
            for k in range(TOP_K)]
    dest_ref[...] = jnp.concatenate(dest, axis=0).astype(I32)
    carry_ref[...] += incl[:, tt - 1:tt]


def _rank(top_e_t, pstart_col, tri):
    n = top_e_t.shape[1]
    tt = TT_ROUTE
    return pl.pallas_call(
        _rank_kernel,
        grid=(n // tt,),
        in_specs=[pl.BlockSpec((TOP_K, tt), lambda i: (0, i)),
                  pl.BlockSpec((N_EXPERTS, 1), lambda i: (0, 0)),
                  pl.BlockSpec((tt, tt), lambda i: (0, 0))],
        out_specs=pl.BlockSpec((TOP_K, tt), lambda i: (0, i)),
        out_shape=jax.ShapeDtypeStruct((TOP_K, n), I32),
        scratch_shapes=[pltpu.VMEM((N_EXPERTS, 1), F32)],
        compiler_params=_params("arbitrary"),
        name="rank",
    )(top_e_t, pstart_col, tri)


def _row_copy(src_ref, src_row, dst_ref, dst_row, sem):
    return pltpu.make_async_copy(src_ref.at[pl.ds(src_row, 1)], dst_ref.at[pl.ds(dst_row, 1)], sem)


SC_CORES, SC_SUBCORES, SC_LANES = 2, 16, 16
SC_WORKERS = SC_CORES * SC_SUBCORES
INVERT_CHUNK = 8192


def _invert(dest_flat, n_tokens, rows):
    per = rows // SC_WORKERS
    n_assign = dest_flat.shape[0]
    assert rows % (SC_WORKERS * SC_LANES) == 0 and n_assign % INVERT_CHUNK == 0
    assert n_tokens & (n_tokens - 1) == 0

    @functools.partial(
        pl.kernel, mesh=plsc.VectorSubcoreMesh(core_axis_name="c", subcore_axis_name="s"),
        out_type=jax.ShapeDtypeStruct((rows,), I32),
        scratch_types=[pltpu.VMEM((INVERT_CHUNK,), I32), pltpu.VMEM((per,), I32)],
        compiler_params=pltpu.CompilerParams(needs_layout_passes=False))
    def invert(dest_hbm, out_hbm, staged, local):
        base = (lax.axis_index("s") * SC_CORES + lax.axis_index("c")) * per

        @pl.loop(0, per, step=SC_LANES)
        def _(i):
            local[pl.ds(i, SC_LANES)] = jnp.zeros((SC_LANES,), I32)

        lane = lax.iota(I32, SC_LANES)

        @pl.loop(0, n_assign // INVERT_CHUNK)
        def _(c):
            pltpu.sync_copy(dest_hbm.at[pl.ds(c * INVERT_CHUNK, INVERT_CHUNK)], staged)

            @pl.loop(0, INVERT_CHUNK, step=SC_LANES)
            def _(i):
                rel = staged[pl.ds(i, SC_LANES)] - base
                mine = (rel >= 0) & (rel < per)
                tok = (c * INVERT_CHUNK + i + lane) & (n_tokens - 1)
                plsc.store_scatter(local, [jnp.where(mine, rel, 0)], tok, mask=mine)

        pltpu.sync_copy(local, out_hbm.at[pl.ds(base, per)])

    return invert(dest_flat)


def _experts_kernel(be_ref, nv_ref, tok_hbm, h_ref, wg_ref, wu_ref, wd_ref, ys_ref,
                    x0, x1, wgb, wub, wdb, tok_s, sem, tok_sem):
    j = pl.program_id(0)
    nv = nv_ref[0]
    bm = x0.shape[0]
    chunk = tok_s.shape[1]
    cb = chunk // bm
    n_chunks = tok_hbm.shape[0] // chunk

    def tok_copy(c):
        return pltpu.make_async_copy(tok_hbm.at[pl.ds(pl.multiple_of(c * chunk, chunk), chunk)],
                                     tok_s.at[c % 2], tok_sem.at[c % 2])

    def gather(block, buf, s):
        slot, off = (block // cb) % 2, (block % cb) * bm
        for i in range(bm):
            _row_copy(h_ref, tok_s[slot, off + i], buf, i, sem.at[s]).start()

    def wait_rows(buf, s):
        pltpu.make_async_copy(h_ref.at[pl.ds(0, bm)], buf, sem.at[s]).wait()

    @pl.when(j == 0)
    def _():
        tok_copy(0).start()
        tok_copy(0).wait()
        gather(0, x0, 0)

    @pl.when(jnp.logical_and(j % cb == 0, j // cb + 1 < n_chunks))
    def _():
        tok_copy(j // cb + 1).start()

    @pl.when(jnp.logical_and((j + 1) % cb == 0, (j + 1) // cb < n_chunks))
    def _():
        tok_copy((j + 1) // cb).wait()

    fresh = jnp.logical_or(j == 0, be_ref[j] != be_ref[jnp.maximum(j - 1, 0)])

    @pl.when(jnp.logical_and(j < nv, fresh))
    def _():
        wgb[...] = wg_ref[...].astype(BF16)
        wub[...] = wu_ref[...].astype(BF16)
        wdb[...] = wd_ref[...].astype(BF16)

    def step(cur, nxt, s):
        wait_rows(cur, s)
        gather(j + 1, nxt, 1 - s)
        xb = cur[...].astype(BF16)
        act = (_silu(jnp.dot(xb, wgb[...], preferred_element_type=F32))
               * jnp.dot(xb, wub[...], preferred_element_type=F32))
        ys_ref[...] = jnp.dot(act.astype(BF16), wdb[...], preferred_element_type=F32)

    @pl.when(jnp.logical_and(j < nv, j % 2 == 0))
    def _():
        step(x0, x1, 0)

    @pl.when(jnp.logical_and(j < nv, j % 2 == 1))
    def _():
        step(x1, x0, 1)

    @pl.when(jnp.logical_and(j == nv, j % 2 == 0))
    def _():
        wait_rows(x0, 0)

    @pl.when(jnp.logical_and(j == nv, j % 2 == 1))
    def _():
        wait_rows(x1, 1)


def _experts(block_e, n_valid, row_tok, h2, wg, wu, wd):
    rows = row_tok.shape[0]
    d = h2.shape[1]
    bm = EXPERT_ROWS
    n_blocks = rows // bm
    assert n_blocks % TOK_CHUNK_BLOCKS == 0
    f = wg.shape[2]
    row_blk = lambda j, be, nv: (jnp.minimum(j, nv[0] - 1), 0)
    w_blk = lambda j, be, nv: (be[j], 0, 0)
    return pl.pallas_call(
        _experts_kernel,
        grid_spec=pltpu.PrefetchScalarGridSpec(
            num_scalar_prefetch=2,
            grid=(n_blocks,),
            in_specs=[pl.BlockSpec(memory_space=pl.ANY),
                      pl.BlockSpec(memory_space=pl.ANY),
                      pl.BlockSpec((None, d, f), w_blk),
                      pl.BlockSpec((None, d, f), w_blk),
                      pl.BlockSpec((None, f, d), w_blk)],
            out_specs=pl.BlockSpec((bm, d), row_blk),
            scratch_shapes=[pltpu.VMEM((bm, d), F32), pltpu.VMEM((bm, d), F32),
                            pltpu.VMEM((d, f), BF16), pltpu.VMEM((d, f), BF16),
                            pltpu.VMEM((f, d), BF16),
                            pltpu.SMEM((2, TOK_CHUNK_BLOCKS * bm), I32),
                            pltpu.SemaphoreType.DMA((2,)), pltpu.SemaphoreType.DMA((2,))]),
        out_shape=jax.ShapeDtypeStruct((rows, d), F32),
        compiler_params=_params("arbitrary"),
        name="experts",
    )(block_e, n_valid, row_tok, h2, wg, wu, wd)


def _combine_kernel(dest_ref, gates_ref, base_ref, mod_ref, ys_ref, out_ref, buf, sem):
    tt = base_ref.shape[0]

    def issue(t, carry):
        for k in range(TOP_K):
            _row_copy(ys_ref, dest_ref[k, t], buf.at[k], t, sem).start()
        return carry

    def drain(t, carry):
        for k in range(TOP_K):
            _row_copy(ys_ref, dest_ref[k, t], buf.at[k], t, sem).wait()
        return carry

    lax.fori_loop(0, tt, issue, 0)
    lax.fori_loop(0, tt, drain, 0)
    gates = gates_ref[...]
    routed = gates[:, 0:1] * buf[0]
    for k in range(1, TOP_K):
        routed = routed + gates[:, k:k + 1] * buf[k]
    out_ref[...] = base_ref[...] + mod_ref[5:6, :] * routed


def _combine(dest_t, gates, base, mod, ys, seq):
    n, d = base.shape
    tt = TT_COMBINE
    tiles_per_seq = seq // tt
    return pl.pallas_call(
        _combine_kernel,
        grid=(n // tt,),
        in_specs=[pl.BlockSpec((TOP_K, tt), lambda i: (0, i), memory_space=pltpu.SMEM),
                  pl.BlockSpec((tt, TOP_K), lambda i: (i, 0)),
                  pl.BlockSpec((tt, d), lambda i: (i, 0)),
                  pl.BlockSpec((None, 6, d), lambda i: (i // tiles_per_seq, 0, 0)),
                  pl.BlockSpec(memory_space=pl.ANY)],
        out_specs=pl.BlockSpec((tt, d), lambda i: (i, 0)),
        out_shape=jax.ShapeDtypeStruct((n, d), F32),
        scratch_shapes=[pltpu.VMEM((TOP_K, tt, d), F32), pltpu.SemaphoreType.DMA],
        compiler_params=_params("arbitrary"),
        name="combine",
    )(dest_t, gates, base, mod, ys)


def _layer(x, mod, pos, rope, p):
    nbatch, seq, d = x.shape
    n = nbatch * seq
    xf = x.reshape(n, d)
    invf, bd = rope

    perm = np.concatenate([np.arange(h * HEAD_DIM, (h + 1) * HEAD_DIM) for h in PAIR_ORDER_A])
    w_in = p["w_in"]
    w_in_p = jnp.concatenate([w_in[:, :MIX_A][:, perm], w_in[:, MIX_A:]], axis=1).astype(BF16)
    ones = lambda w: jnp.ones((w,), F32)
    qscale = HEAD_DIM ** -0.5
    gcol = jnp.concatenate([jnp.tile(p["g_q_a"], N_HEADS_A) * qscale, jnp.tile(p["g_k_a"], N_KV_A),
                            ones(KV_A), jnp.tile(p["g_q_b"], N_HEADS_B) * qscale,
                            jnp.tile(p["g_k_b"], N_HEADS_B), ones(MIX_B)]).reshape(1, IN_WIDTH)
    proj = _inproj(xf, mod, pos, p["g_norm_mix"].reshape(1, d), w_in_p, gcol, invf, bd, seq)
    qa, ka, va = proj[:3]
    qkv_b = {1: proj[3:6]}
    for j, dil in enumerate(DILS):
        qkv_b[dil] = [proj[6 + t * len(DILS) + j] for t in range(3)]

    sinks_p = p["sinks_a"][np.array(PAIR_ORDER_A)]
    oa, _ = _attention(qa, ka, va, nbatch=nbatch, seq=seq, dil=1, max_dist=WINDOW_A - 1,
                       kv_shared=True, sinks=sinks_p, want_lse=False)
    obs, lses = [], []
    for window, dil in DILATED_BRANCHES:
        o, lse = _attention(*qkv_b[dil], nbatch=nbatch, seq=seq, dil=dil, max_dist=window // dil,
                            kv_shared=False, want_lse=True)
        obs.append(o)
        lses.append(lse)

    goa = p["g_out_a"][perm].reshape(1, MIX_A)
    w_out = p["w_out"]
    wo_p = jnp.concatenate([w_out[:MIX_A][perm], w_out[MIX_A:]], axis=0).astype(BF16)
    h2, base = _outproj(xf, mod, oa, obs, lses, goa, p["g_out_b"].reshape(1, MIX_B), wo_p,
                        p["g_norm_ffn"].reshape(1, d), p["w_gate_s"].astype(BF16),
                        p["w_up_s"].astype(BF16), p["w_down_s"].astype(BF16), seq)

    top_e_t, gates_t, counts = _router(h2, p["w_router"].T.astype(BF16),
                                       p["router_bias"].reshape(N_EXPERTS, 1))
    bm = EXPERT_ROWS
    counts = counts.reshape(N_EXPERTS).astype(I32)
    padded = (counts + bm - 1) // bm * bm
    pends = jnp.cumsum(padded)
    pstarts = pends - padded
    rows = n * TOP_K + N_EXPERTS * bm
    n_blocks = rows // bm
    n_valid = (pends[-1] // bm).astype(I32)
    blk = jnp.minimum(jnp.arange(n_blocks, dtype=I32), n_valid - 1)
    block_e = jnp.sum((pends[None, :] <= (blk * bm)[:, None]).astype(I32), axis=1)
    block_e = jnp.minimum(block_e, N_EXPERTS - 1)

    tri = (np.arange(TT_ROUTE)[:, None] <= np.arange(TT_ROUTE)[None, :])
    dest_t = _rank(top_e_t, pstarts.astype(F32).reshape(N_EXPERTS, 1), jnp.asarray(tri, BF16))
    n_valid = n_valid.reshape(1)
    row_tok = _invert(dest_t.reshape(n * TOP_K), n, rows)
    ys = _experts(block_e, n_valid, row_tok, h2, p["w_gate_e"], p["w_up_e"], p["w_down_e"])
    out = _combine(dest_t, gates_t.T, base, mod, ys, seq)
    return out.reshape(nbatch, seq, d)


def kernel(x, c, positions, w_ada, b_ada, g_norm_mix, w_in, g_q_a, g_k_a, sinks_a, g_q_b, g_k_b,
           g_out_a, g_out_b, w_out, g_norm_ffn, w_router, router_bias, w_gate_e, w_up_e, w_down_e,
           w_gate_s, w_up_s, w_down_s):
    nbatch, seq, d = x.shape
    depth = w_ada.shape[0]
    params = dict(g_norm_mix=g_norm_mix, w_in=w_in, g_q_a=g_q_a, g_k_a=g_k_a, sinks_a=sinks_a,
                  g_q_b=g_q_b, g_k_b=g_k_b, g_out_a=g_out_a, g_out_b=g_out_b, w_out=w_out,
                  g_norm_ffn=g_norm_ffn, w_router=w_router, router_bias=router_bias,
                  w_gate_e=w_gate_e, w_up_e=w_up_e, w_down_e=w_down_e, w_gate_s=w_gate_s,
                  w_up_s=w_up_s, w_down_s=w_down_s)
    j = np.arange(LANES) % HEAD_DIM
    inv = ROPE_THETA ** (-jnp.arange(0, ROT_DIM, 2, dtype=F32) / ROT_DIM)
    invf = jnp.where(j < ROT_DIM, inv[j % (ROT_DIM // 2)], 0.0).astype(F32).reshape(1, LANES)
    bd = jnp.asarray((np.arange(LANES)[:, None] // HEAD_DIM) == (np.arange(LANES)[None, :] // HEAD_DIM),
                     BF16)
    pos = positions.reshape(nbatch * seq, 1).astype(I32)
    for l in range(depth):
        mod = _adaln(c.astype(F32), w_ada[l], b_ada[l]).reshape(nbatch, 6, d)
        x = _layer(x, mod, pos, (invf, bd), {k: v[l] for k, v in params.items()})
    return x
```

```python
import functools

import numpy as np
import jax
import jax.numpy as jnp
from jax import lax
from jax.experimental import pallas as pl
from jax.experimental.pallas import tpu as pltpu
from jax.experimental.pallas import tpu_sc as plsc

F32 = jnp.float32
BF16 = jnp.bfloat16
I32 = jnp.int32

HEAD_DIM = 64
N_HEADS_A = 8
N_KV_A = 2
WINDOW_A = 128
N_HEADS_B = 8
DILATED_BRANCHES = ((128, 1), (512, 4), (2048, 16))
DILS = tuple(dil for _, dil in DILATED_BRANCHES if dil > 1)
BLOCK = 128
ROT_DIM = HEAD_DIM // 4
ROPE_THETA = 500000.0
MIX_A = N_HEADS_A * HEAD_DIM
KV_A = N_KV_A * HEAD_DIM
MIX_B = N_HEADS_B * HEAD_DIM
N_EXPERTS = 256
TOP_K = 8
N_GROUPS = 8
TOPK_GROUPS = 4
GROUP_SIZE = N_EXPERTS // N_GROUPS
ROUTED_SCALE = 2.5
EPS = 1e-6

LANES = 128
HEADS_PER_VREG = LANES // HEAD_DIM
N_PAIRS = MIX_A // LANES
NEG = -1e30
VMEM_LIMIT = 48 * 1024 * 1024

TT_PROJ = 512
TT_ROUTE = 256
TT_DISPATCH = 256
TT_COMBINE = 128
EXPERT_ROWS = 256
TOK_CHUNK_BLOCKS = 32

PAIR_ORDER_A = tuple(h for p in range(N_PAIRS) for h in (p, p + N_HEADS_A // N_KV_A))


def _params(*sem):
    return pltpu.CompilerParams(dimension_semantics=sem, vmem_limit_bytes=VMEM_LIMIT)


def _silu(t):
    return t / (1.0 + jnp.exp(-t))


def _rms_rows(t):
    return t * lax.rsqrt(jnp.mean(t * t, axis=-1, keepdims=True) + EPS)


def _ada_kernel(c_ref, w_ref, b_ref, o_ref):
    cond = _silu(c_ref[...])
    o_ref[...] = jnp.dot(cond.astype(BF16), w_ref[...].astype(BF16),
                         preferred_element_type=F32) + b_ref[...]


def _adaln(c, w_ada, b_ada):
    nb, d = c.shape
    width = w_ada.shape[1]
    tn = 1024
    return pl.pallas_call(
        _ada_kernel,
        grid=(width // tn,),
        in_specs=[pl.BlockSpec((nb, d), lambda j: (0, 0)),
                  pl.BlockSpec((d, tn), lambda j: (0, j)),
                  pl.BlockSpec((1, tn), lambda j: (0, j))],
        out_specs=pl.BlockSpec((nb, tn), lambda j: (0, j)),
        out_shape=jax.ShapeDtypeStruct((nb, width), F32),
        compiler_params=_params("arbitrary"),
        name="adaln",
    )(c, w_ada, b_ada.reshape(1, width))


COL_QA, COL_KA, COL_VA = 0, MIX_A, MIX_A + KV_A
COL_QB = MIX_A + 2 * KV_A
COL_KB, COL_VB = COL_QB + MIX_B, COL_QB + 2 * MIX_B
IN_WIDTH = COL_VB + MIX_B


def _inproj_kernel(x_ref, mod_ref, pos_ref, gn_ref, w_ref, gcol_ref, invf_ref, bd_ref,
                   qa_ref, ka_ref, va_ref, qb_ref, kb_ref, vb_ref, *rest):
    n_dil = len(DILS)
    dil_refs = [rest[i * n_dil:(i + 1) * n_dil] for i in range(3)]
    qb_scr, kb_scr, vb_scr = rest[3 * n_dil:]
    tt = x_ref.shape[0]
    shift, scale = mod_ref[0:1, :], mod_ref[1:2, :]
    h = _rms_rows(x_ref[...]) * gn_ref[...] * (1.0 + scale) + shift
    proj = jnp.dot(h.astype(BF16), w_ref[...], preferred_element_type=F32)

    ang = pos_ref[...].astype(F32) * invf_ref[...]
    cs, sn = jnp.cos(ang), jnp.sin(ang)
    lane = lax.broadcasted_iota(I32, (1, LANES), 1) % HEAD_DIM
    s_lo = jnp.where(lane < ROT_DIM // 2, -sn, 0.0)
    s_hi = jnp.where((lane >= ROT_DIM // 2) & (lane < ROT_DIM), sn, 0.0)
    bd = bd_ref[...]

    def norm_rope(col0, width, out_ref, scr=None):
        for j in range(width // LANES):
            c = col0 + j * LANES
            t = proj[:, c:c + LANES]
            sq = t * t
            hi = sq.astype(BF16)
            lo = (sq - hi.astype(F32)).astype(BF16)
            ss = (jnp.dot(hi, bd, preferred_element_type=F32)
                  + jnp.dot(lo, bd, preferred_element_type=F32))
            t = t * lax.rsqrt(ss * (1.0 / HEAD_DIM) + EPS) * gcol_ref[:, c:c + LANES]
            t = (t * cs + pltpu.roll(t, LANES - ROT_DIM // 2, 1) * s_lo
                 + pltpu.roll(t, ROT_DIM // 2, 1) * s_hi)
            out_ref[:, j * LANES:(j + 1) * LANES] = t.astype(BF16)
            if scr is not None:
                scr[j] = t

    norm_rope(COL_QA, MIX_A, qa_ref)
    norm_rope(COL_KA, KV_A, ka_ref)
    norm_rope(COL_QB, MIX_B, qb_ref, qb_scr)
    norm_rope(COL_KB, MIX_B, kb_ref, kb_scr)
    va_ref[...] = proj[:, COL_VA:COL_VA + KV_A].astype(BF16)
    vb_ref[...] = proj[:, COL_VB:COL_VB + MIX_B].astype(BF16)
    for j in range(N_PAIRS):
        vb_scr[j] = proj[:, COL_VB + j * LANES:COL_VB + (j + 1) * LANES]
    for scr, outs in zip((qb_scr, kb_scr, vb_scr), dil_refs):
        for dil, out in zip(DILS, outs):
            for r in range(dil):
                for j in range(N_PAIRS):
                    c = r * MIX_B + j * LANES
                    out[:, c:c + LANES] = scr[j, pl.ds(r, tt // dil, stride=dil), :].astype(BF16)


def _inproj(xf, mod, pos, g_norm, w_in_p, gcol, invf, bd, seq):
    n, d = xf.shape
    tt = TT_PROJ
    tiles_per_seq = seq // tt
    shapes = [(n, w) for w in (MIX_A, KV_A, KV_A, MIX_B, MIX_B, MIX_B)]
    shapes += [(n // dil, dil * MIX_B) for _ in range(3) for dil in DILS]
    full = lambda shape: pl.BlockSpec(shape, lambda i: (0,) * len(shape))
    return pl.pallas_call(
        _inproj_kernel,
        grid=(n // tt,),
        in_specs=[pl.BlockSpec((tt, d), lambda i: (i, 0)),
                  pl.BlockSpec((None, 6, d), lambda i: (i // tiles_per_seq, 0, 0)),
                  pl.BlockSpec((tt, 1), lambda i: (i, 0)),
                  full((1, d)), full((d, IN_WIDTH)), full((1, IN_WIDTH)),
                  full((1, LANES)), full((LANES, LANES))],
        out_specs=[pl.BlockSpec((tt * r // n, w), lambda i: (i, 0)) for r, w in shapes],
        out_shape=[jax.ShapeDtypeStruct(s, BF16) for s in shapes],
        scratch_shapes=[pltpu.VMEM((N_PAIRS, tt, LANES), F32)] * 3,
        compiler_params=_params("parallel"),
        name="inproj",
    )(xf, mod, pos, g_norm, w_in_p, gcol, invf, bd)


def _attn_kernel(*refs, kv_shared, max_dist, use_prev, has_sinks, want_lse):
    refs = list(refs)
    sink_ref = refs.pop(0) if has_sinks else None
    q_ref = refs.pop(0)
    kp_ref = refs.pop(0) if use_prev else None
    kc_ref = refs.pop(0)
    vp_ref = refs.pop(0) if use_prev else None
    vc_ref = refs.pop(0)
    o_ref = refs.pop(0)
    lse_ref = refs.pop(0) if want_lse else None

    blk = pl.program_id(2)
    nq = 2 * BLOCK
    nk = 2 * BLOCK if use_prev else BLOCK
    qpos = lax.broadcasted_iota(I32, (nq, nk), 0) % BLOCK
    kpos = lax.broadcasted_iota(I32, (nq, nk), 1)
    if use_prev:
        dist = qpos + BLOCK - kpos
        valid = (dist >= 0) & (dist <= max_dist) & ((kpos >= BLOCK) | (blk > 0))
    else:
        dist = qpos - kpos
        valid = (dist >= 0) & (dist <= max_dist)
    lane = lax.broadcasted_iota(I32, (nq, LANES), 1)
    row = lax.broadcasted_iota(I32, (nq, LANES), 0)
    own_half = (lane < HEAD_DIM) == (row < BLOCK)
    left_lanes = lax.broadcasted_iota(I32, (BLOCK, LANES), 1) < HEAD_DIM
    lane8 = lax.broadcasted_iota(I32, (BLOCK, 2 * N_PAIRS), 1)
    lse_blk = jnp.zeros((BLOCK, 2 * N_PAIRS), F32)

    for p in range(N_PAIRS):
        cq = slice(p * LANES, (p + 1) * LANES)
        ck = slice(0, LANES) if kv_shared else cq
        qp = q_ref[:, cq]
        qs = jnp.concatenate([qp, qp], axis=0)
        qs = jnp.where(own_half, qs, jnp.zeros_like(qs))
        if use_prev:
            k = jnp.concatenate([kp_ref[:, ck], kc_ref[:, ck]], axis=0)
            v = jnp.concatenate([vp_ref[:, ck], vc_ref[:, ck]], axis=0)
        else:
            k, v = kc_ref[:, ck], vc_ref[:, ck]
        s = lax.dot_general(qs, k, (((1,), (1,)), ((), ())), preferred_element_type=F32)
        s = jnp.where(valid, s, NEG)
        m = jnp.max(s, axis=-1, keepdims=True)
        if has_sinks:
            rows1 = lax.broadcasted_iota(I32, (nq, 1), 0)
            sink = jnp.where(rows1 < BLOCK, sink_ref[2 * p], sink_ref[2 * p + 1])
            m = jnp.maximum(m, sink)
        e = jnp.exp(s - m)
        l = jnp.sum(e, axis=-1, keepdims=True)
        if has_sinks:
            l = l + jnp.exp(sink - m)
        o = jnp.dot(e.astype(BF16), v, preferred_element_type=F32) / l
        o_ref[:, cq] = jnp.where(left_lanes, o[:BLOCK], o[BLOCK:]).astype(BF16)
        if want_lse:
            lse = m + jnp.log(l)
            lse_blk = (lse_blk + jnp.where(lane8 == 2 * p, lse[:BLOCK], 0.0)
                       + jnp.where(lane8 == 2 * p + 1, lse[BLOCK:], 0.0))
    if want_lse:
        lse_ref[...] = lse_blk


def _attention(q, k, v, *, nbatch, seq, dil, max_dist, kv_shared, sinks=None, want_lse):
    length = seq // dil
    nblk = length // BLOCK
    use_prev = nblk > 1
    kw = k.shape[1] // dil
    view = lambda t: t.reshape(nbatch, length, t.shape[1])
    cur = lambda b, r, i: (b, i, r)
    prev = lambda b, r, i: (b, jnp.maximum(i - 1, 0), r)
    in_specs, args = [], []
    if sinks is not None:
        in_specs.append(pl.BlockSpec(memory_space=pltpu.SMEM))
        args.append(sinks)
    in_specs.append(pl.BlockSpec((None, BLOCK, MIX_B), cur))
    args.append(view(q))
    for t in (k, v):
        if use_prev:
            in_specs.append(pl.BlockSpec((None, BLOCK, kw), prev))
            args.append(view(t))
        in_specs.append(pl.BlockSpec((None, BLOCK, kw), cur))
        args.append(view(t))
    out_specs = [pl.BlockSpec((None, BLOCK, MIX_B), cur)]
    out_shape = [jax.ShapeDtypeStruct((nbatch, length, dil * MIX_B), BF16)]
    if want_lse:
        out_specs.append(pl.BlockSpec((None, None, BLOCK, N_HEADS_B), lambda b, r, i: (b, r, i, 0)))
        out_shape.append(jax.ShapeDtypeStruct((nbatch, dil, length, N_HEADS_B), F32))
    outs = pl.pallas_call(
        functools.partial(_attn_kernel, kv_shared=kv_shared, max_dist=max_dist, use_prev=use_prev,
                          has_sinks=sinks is not None, want_lse=want_lse),
        grid=(nbatch, dil, nblk),
        in_specs=in_specs, out_specs=out_specs, out_shape=out_shape,
        compiler_params=_params("parallel", "parallel", "arbitrary"),
        name=f"attn_d{dil}" + ("_swa" if kv_shared else ""),
    )(*args)
    o = outs[0].reshape(nbatch * length, dil * MIX_B)
    if not want_lse:
        return o, None
    lse = outs[1].transpose(0, 2, 1, 3).reshape(nbatch * seq, N_HEADS_B)
    return o, lse


def _expand_heads(w, width):
    head = lax.broadcasted_iota(I32, (1, width), 1) // HEAD_DIM
    out = jnp.zeros((w.shape[0], width), F32)
    for hd in range(w.shape[1]):
        out = jnp.where(head == hd, w[:, hd:hd + 1], out)
    return out


def _outproj_kernel(x_ref, mod_ref, oa_ref, ob1_ref, ob2_ref, ob3_ref, l1_ref, l2_ref, l3_ref,
                    goa_ref, gob_ref, wo_ref, gf_ref, wgs_ref, wus_ref, wds_ref,
                    h2_ref, base_ref, ob_scr):
    tt = x_ref.shape[0]
    gate_a = mod_ref[2:3, :]
    shift_m, scale_m, gate_m = mod_ref[3:4, :], mod_ref[4:5, :], mod_ref[5:6, :]

    def token_major(ref, dil):
        if dil == 1:
            return ref[...].astype(F32)
        for r in range(dil):
            for j in range(N_PAIRS):
                c = r * MIX_B + j * LANES
                ob_scr[j, pl.ds(r, tt // dil, stride=dil), :] = ref[:, c:c + LANES].astype(F32)
        return jnp.concatenate([ob_scr[j] for j in range(N_PAIRS)], axis=1)

    l1, l2, l3 = l1_ref[...], l2_ref[...], l3_ref[...]
    mx = jnp.maximum(jnp.maximum(l1, l2), l3)
    e1, e2, e3 = jnp.exp(l1 - mx), jnp.exp(l2 - mx), jnp.exp(l3 - mx)
    den = e1 + e2 + e3
    dils = [dil for _, dil in DILATED_BRANCHES]
    ob = _expand_heads(e1 / den, MIX_B) * token_major(ob1_ref, dils[0])
    ob = ob + _expand_heads(e2 / den, MIX_B) * token_major(ob2_ref, dils[1])
    ob = ob + _expand_heads(e3 / den, MIX_B) * token_major(ob3_ref, dils[2])
    ob = _rms_rows(ob) * gob_ref[...]
    oa = _rms_rows(oa_ref[...].astype(F32)) * goa_ref[...]
    y = (jnp.dot(oa.astype(BF16), wo_ref[0:MIX_A, :], preferred_element_type=F32)
         + jnp.dot(ob.astype(BF16), wo_ref[MIX_A:MIX_A + MIX_B, :], preferred_element_type=F32))
    x1 = x_ref[...] + gate_a * y
    h2 = _rms_rows(x1) * gf_ref[...] * (1.0 + scale_m) + shift_m
    h2_ref[...] = h2
    hb = h2.astype(BF16)
    act = (_silu(jnp.dot(hb, wgs_ref[...], preferred_element_type=F32))
           * jnp.dot(hb, wus_ref[...], preferred_element_type=F32))
    shared = jnp.dot(act.astype(BF16), wds_ref[...], preferred_element_type=F32)
    base_ref[...] = x1 + gate_m * shared


def _outproj(xf, mod, oa, obs, lses, goa, gob, wo_p, gf, wgs, wus, wds, seq):
    n, d = xf.shape
    tt = TT_PROJ // 2
    tiles_per_seq = seq // tt
    tile = lambda w: pl.BlockSpec((tt, w), lambda i: (i, 0))
    full = lambda shape: pl.BlockSpec(shape, lambda i: (0,) * len(shape))
    sd = wgs.shape[1]
    dilated = [pl.BlockSpec((tt // dil, dil * MIX_B), lambda i: (i, 0)) for _, dil in DILATED_BRANCHES]
    return pl.pallas_call(
        _outproj_kernel,
        grid=(n // tt,),
        in_specs=[tile(d), pl.BlockSpec((None, 6, d), lambda i: (i // tiles_per_seq, 0, 0)),
                  tile(MIX_A), *dilated,
                  tile(N_HEADS_B), tile(N_HEADS_B), tile(N_HEADS_B),
                  full((1, MIX_A)), full((1, MIX_B)), full((MIX_A + MIX_B, d)), full((1, d)),
                  full((d, sd)), full((d, sd)), full((sd, d))],
        out_specs=[tile(d), tile(d)],
        out_shape=[jax.ShapeDtypeStruct((n, d), F32), jax.ShapeDtypeStruct((n, d), F32)],
        scratch_shapes=[pltpu.VMEM((N_PAIRS, tt, LANES), F32)],
        compiler_params=_params("parallel"),
        name="outproj",
    )(xf, mod, oa, *obs, *lses, goa, gob, wo_p, gf, wgs, wus, wds)


def _router_kernel(h_ref, wrt_ref, bias_ref, e_ref, g_ref, cnt_ref):
    tt = h_ref.shape[0]
    logits = lax.dot_general(wrt_ref[...], h_ref[...].astype(BF16), (((1,), (1,)), ((), ())),
                             preferred_element_type=F32)
    scores = 1.0 / (1.0 + jnp.exp(-logits))
    biased = scores + bias_ref[...]
    ninf = -jnp.inf

    j32 = lax.broadcasted_iota(I32, (GROUP_SIZE, tt), 0).astype(F32)
    grp = []
    for g in range(N_GROUPS):
        bg = biased[g * GROUP_SIZE:(g + 1) * GROUP_SIZE, :]
        m1 = jnp.max(bg, axis=0, keepdims=True)
        i1 = jnp.min(jnp.where(bg == m1, j32, float(GROUP_SIZE)), axis=0, keepdims=True)
        m2 = jnp.max(jnp.where(j32 == i1, ninf, bg), axis=0, keepdims=True)
        grp.append(m1 + m2)
    grp = jnp.concatenate(grp, axis=0)
    g8 = lax.broadcasted_iota(I32, (N_GROUPS, tt), 0).astype(F32)
    chosen = jnp.zeros((N_GROUPS, tt), F32)
    for _ in range(TOPK_GROUPS):
        gm = jnp.max(grp, axis=0, keepdims=True)
        gi = jnp.min(jnp.where(grp == gm, g8, float(N_GROUPS)), axis=0, keepdims=True)
        hit = g8 == gi
        chosen = jnp.where(hit, 1.0, chosen)
        grp = jnp.where(hit, ninf, grp)
    masked = jnp.concatenate(
        [jnp.where(chosen[g:g + 1, :] > 0.0, biased[g * GROUP_SIZE:(g + 1) * GROUP_SIZE, :], ninf)
         for g in range(N_GROUPS)], axis=0)

    eio = lax.broadcasted_iota(I32, (N_EXPERTS, tt), 0).astype(F32)
    picked = jnp.zeros((N_EXPERTS, tt), F32)
    es, gs = [], []
    for _ in range(TOP_K):
        m = jnp.max(masked, axis=0, keepdims=True)
        idx = jnp.min(jnp.where(masked == m, eio, float(N_EXPERTS)), axis=0, keepdims=True)
        hit = eio == idx
        gs.append(jnp.sum(jnp.where(hit, scores, 0.0), axis=0, keepdims=True))
        es.append(idx)
        picked = jnp.where(hit, 1.0, picked)
        masked = jnp.where(hit, ninf, masked)
    gates = jnp.concatenate(gs, axis=0)
    e_ref[...] = jnp.concatenate(es, axis=0).astype(I32)
    g_ref[...] = gates / jnp.sum(gates, axis=0, keepdims=True) * ROUTED_SCALE

    @pl.when(pl.program_id(0) == 0)
    def _():
        cnt_ref[...] = jnp.zeros_like(cnt_ref)
    cnt_ref[...] += jnp.sum(picked, axis=1, keepdims=True)


def _router(h2, wrt, bias_col):
    n, d = h2.shape
    tt = TT_ROUTE
    return pl.pallas_call(
        _router_kernel,
        grid=(n // tt,),
        in_specs=[pl.BlockSpec((tt, d), lambda i: (i, 0)),
                  pl.BlockSpec((N_EXPERTS, d), lambda i: (0, 0)),
                  pl.BlockSpec((N_EXPERTS, 1), lambda i: (0, 0))],
        out_specs=[pl.BlockSpec((TOP_K, tt), lambda i: (0, i)),
                   pl.BlockSpec((TOP_K, tt), lambda i: (0, i)),
                   pl.BlockSpec((N_EXPERTS, 1), lambda i: (0, 0))],
        out_shape=[jax.ShapeDtypeStruct((TOP_K, n), I32),
                   jax.ShapeDtypeStruct((TOP_K, n), F32),
                   jax.ShapeDtypeStruct((N_EXPERTS, 1), F32)],
        compiler_params=_params("arbitrary"),
        name="router",
    )(h2, wrt, bias_col)


def _rank_kernel(e_ref, pstart_ref, tri_ref, dest_ref, carry_ref):
    tt = e_ref.shape[1]

    @pl.when(pl.program_id(0) == 0)
    def _():
        carry_ref[...] = pstart_ref[...]

    e = e_ref[...]
    eio = lax.broadcasted_iota(I32, (N_EXPERTS, tt), 0)
    mask = jnp.zeros((N_EXPERTS, tt), F32)
    for k in range(TOP_K):
        mask = jnp.where(eio == e[k:k + 1, :], 1.0, mask)
    incl = jnp.dot(mask.astype(BF16), tri_ref[...], preferred_element_type=F32)
    pos = incl - mask + carry_ref[...]
    dest = [jnp.sum(jnp.where(eio == e[k:k + 1, :], pos, 0.0), axis=0, keepdims=True)
            for k in range(TOP_K)]
    dest_ref[...] = jnp.concatenate(dest, axis=0).astype(I32)
    carry_ref[...] += incl[:, tt - 1:tt]


def _rank(top_e_t, pstart_col, tri):
    n = top_e_t.shape[1]
    tt = TT_ROUTE
    return pl.pallas_call(
        _rank_kernel,
        grid=(n // tt,),
        in_specs=[pl.BlockSpec((TOP_K, tt), lambda i: (0, i)),
                  pl.BlockSpec((N_EXPERTS, 1), lambda i: (0, 0)),
                  pl.BlockSpec((tt, tt), lambda i: (0, 0))],
        out_specs=pl.BlockSpec((TOP_K, tt), lambda i: (0, i)),
        out_shape=jax.ShapeDtypeStruct((TOP_K, n), I32),
        scratch_shapes=[pltpu.VMEM((N_EXPERTS, 1), F32)],
        compiler_params=_params("arbitrary"),
        name="rank",
    )(top_e_t, pstart_col, tri)


def _row_copy(src_ref, src_row, dst_ref, dst_row, sem):
    return pltpu.make_async_copy(src_ref.at[pl.ds(src_row, 1)], dst_ref.at[pl.ds(dst_row, 1)], sem)


SC_CORES, SC_SUBCORES, SC_LANES = 2, 16, 16
SC_WORKERS = SC_CORES * SC_SUBCORES
INVERT_CHUNK = 8192


def _invert(dest_flat, n_tokens, rows):
    per = rows // SC_WORKERS
    n_assign = dest_flat.shape[0]
    assert rows % (SC_WORKERS * SC_LANES) == 0 and n_assign % INVERT_CHUNK == 0
    assert n_tokens & (n_tokens - 1) == 0

    @functools.partial(
        pl.kernel, mesh=plsc.VectorSubcoreMesh(core_axis_name="c", subcore_axis_name="s"),
        out_type=jax.ShapeDtypeStruct((rows,), I32),
        scratch_types=[pltpu.VMEM((INVERT_CHUNK,), I32), pltpu.VMEM((per,), I32)],
        compiler_params=pltpu.CompilerParams(needs_layout_passes=False))
    def invert(dest_hbm, out_hbm, staged, local):
        base = (lax.axis_index("s") * SC_CORES + lax.axis_index("c")) * per

        @pl.loop(0, per, step=SC_LANES)
        def _(i):
            local[pl.ds(i, SC_LANES)] = jnp.zeros((SC_LANES,), I32)

        lane = lax.iota(I32, SC_LANES)

        @pl.loop(0, n_assign // INVERT_CHUNK)
        def _(c):
            pltpu.sync_copy(dest_hbm.at[pl.ds(c * INVERT_CHUNK, INVERT_CHUNK)], staged)

            @pl.loop(0, INVERT_CHUNK, step=SC_LANES)
            def _(i):
                rel = staged[pl.ds(i, SC_LANES)] - base
                mine = (rel >= 0) & (rel < per)
                tok = (c * INVERT_CHUNK + i + lane) & (n_tokens - 1)
                plsc.store_scatter(local, [jnp.where(mine, rel, 0)], tok, mask=mine)

        pltpu.sync_copy(local, out_hbm.at[pl.ds(base, per)])

    return invert(dest_flat)


def _experts_kernel(be_ref, nv_ref, tok_hbm, h_ref, wg_ref, wu_ref, wd_ref, ys_ref,
                    x0, x1, wgb, wub, wdb, tok_s, sem, tok_sem):
    j = pl.program_id(0)
    nv = nv_ref[0]
    bm = x0.shape[0]
    chunk = tok_s.shape[1]
    cb = chunk // bm
    n_chunks = tok_hbm.shape[0] // chunk

    def tok_copy(c):
        return pltpu.make_async_copy(tok_hbm.at[pl.ds(pl.multiple_of(c * chunk, chunk), chunk)],
                                     tok_s.at[c % 2], tok_sem.at[c % 2])

    def gather(block, buf, s):
        slot, off = (block // cb) % 2, (block % cb) * bm

        def issue(i, carry):
            _row_copy(h_ref, tok_s[slot, off + i], buf, i, sem.at[s]).start()
            return carry

        lax.fori_loop(0, bm, issue, 0, unroll=8)

    def wait_rows(buf, s):
        pltpu.make_async_copy(h_ref.at[pl.ds(0, bm)], buf, sem.at[s]).wait()

    @pl.when(j == 0)
    def _():
        tok_copy(0).start()
        tok_copy(0).wait()
        gather(0, x0, 0)

    @pl.when(jnp.logical_and(j % cb == 0, j // cb + 1 < n_chunks))
    def _():
        tok_copy(j // cb + 1).start()

    @pl.when(jnp.logical_and((j + 1) % cb == 0, (j + 1) // cb < n_chunks))
    def _():
        tok_copy((j + 1) // cb).wait()

    fresh = jnp.logical_or(j == 0, be_ref[j] != be_ref[jnp.maximum(j - 1, 0)])

    @pl.when(jnp.logical_and(j < nv, fresh))
    def _():
        wgb[...] = wg_ref[...].astype(BF16)
        wub[...] = wu_ref[...].astype(BF16)
        wdb[...] = wd_ref[...].astype(BF16)

    def step(cur, nxt, s):
        wait_rows(cur, s)
        gather(j + 1, nxt, 1 - s)
        xb = cur[...].astype(BF16)
        act = (_silu(jnp.dot(xb, wgb[...], preferred_element_type=F32))
               * jnp.dot(xb, wub[...], preferred_element_type=F32))
        ys_ref[...] = jnp.dot(act.astype(BF16), wdb[...], preferred_element_type=F32)

    @pl.when(jnp.logical_and(j < nv, j % 2 == 0))
    def _():
        step(x0, x1, 0)

    @pl.when(jnp.logical_and(j < nv, j % 2 == 1))
    def _():
        step(x1, x0, 1)

    @pl.when(jnp.logical_and(j == nv, j % 2 == 0))
    def _():
        wait_rows(x0, 0)

    @pl.when(jnp.logical_and(j == nv, j % 2 == 1))
    def _():
        wait_rows(x1, 1)


def _experts(block_e, n_valid, row_tok, h2, wg, wu, wd):
    rows = row_tok.shape[0]
    d = h2.shape[1]
    bm = EXPERT_ROWS
    n_blocks = rows // bm
    assert n_blocks % TOK_CHUNK_BLOCKS == 0
    f = wg.shape[2]
    row_blk = lambda j, be, nv: (jnp.minimum(j, nv[0] - 1), 0)
    w_blk = lambda j, be, nv: (be[j], 0, 0)
    return pl.pallas_call(
        _experts_kernel,
        grid_spec=pltpu.PrefetchScalarGridSpec(
            num_scalar_prefetch=2,
            grid=(n_blocks,),
            in_specs=[pl.BlockSpec(memory_space=pl.ANY),
                      pl.BlockSpec(memory_space=pl.ANY),
                      pl.BlockSpec((None, d, f), w_blk),
                      pl.BlockSpec((None, d, f), w_blk),
                      pl.BlockSpec((None, f, d), w_blk)],
            out_specs=pl.BlockSpec((bm, d), row_blk),
            scratch_shapes=[pltpu.VMEM((bm, d), F32), pltpu.VMEM((bm, d), F32),
                            pltpu.VMEM((d, f), BF16), pltpu.VMEM((d, f), BF16),
                            pltpu.VMEM((f, d), BF16),
                            pltpu.SMEM((2, TOK_CHUNK_BLOCKS * bm), I32),
                            pltpu.SemaphoreType.DMA((2,)), pltpu.SemaphoreType.DMA((2,))]),
        out_shape=jax.ShapeDtypeStruct((rows, d), F32),
        compiler_params=_params("arbitrary"),
        name="experts",
    )(block_e, n_valid, row_tok, h2, wg, wu, wd)


def _combine_kernel(dest_ref, gates_ref, base_ref, mod_ref, ys_ref, out_ref, buf, sem):
    tt = base_ref.shape[0]

    def issue(t, carry):
        for k in range(TOP_K):
            _row_copy(ys_ref, dest_ref[k, t], buf.at[k], t, sem).start()
        return carry

    def drain(t, carry):
        for k in range(TOP_K):
            _row_copy(ys_ref, dest_ref[k, t], buf.at[k], t, sem).wait()
        return carry

    lax.fori_loop(0, tt, issue, 0)
    lax.fori_loop(0, tt, drain, 0)
    gates = gates_ref[...]
    routed = gates[:, 0:1] * buf[0]
    for k in range(1, TOP_K):
        routed = routed + gates[:, k:k + 1] * buf[k]
    out_ref[...] = base_ref[...] + mod_ref[5:6, :] * routed


def _combine(dest_t, gates, base, mod, ys, seq):
    n, d = base.shape
    tt = TT_COMBINE
    tiles_per_seq = seq // tt
    return pl.pallas_call(
        _combine_kernel,
        grid=(n // tt,),
        in_specs=[pl.BlockSpec((TOP_K, tt), lambda i: (0, i), memory_space=pltpu.SMEM),
                  pl.BlockSpec((tt, TOP_K), lambda i: (i, 0)),
                  pl.BlockSpec((tt, d), lambda i: (i, 0)),
                  pl.BlockSpec((None, 6, d), lambda i: (i // tiles_per_seq, 0, 0)),
                  pl.BlockSpec(memory_space=pl.ANY)],
        out_specs=pl.BlockSpec((tt, d), lambda i: (i, 0)),
        out_shape=jax.ShapeDtypeStruct((n, d), F32),
        scratch_shapes=[pltpu.VMEM((TOP_K, tt, d), F32), pltpu.SemaphoreType.DMA],
        compiler_params=_params("arbitrary"),
        name="combine",
    )(dest_t, gates, base, mod, ys)


def _layer(x, mod, pos, rope, p):
    nbatch, seq, d = x.shape
    n = nbatch * seq
    xf = x.reshape(n, d)
    invf, bd = rope

    perm = np.concatenate([np.arange(h * HEAD_DIM, (h + 1) * HEAD_DIM) for h in PAIR_ORDER_A])
    w_in = p["w_in"]
    w_in_p = jnp.concatenate([w_in[:, :MIX_A][:, perm], w_in[:, MIX_A:]], axis=1).astype(BF16)
    ones = lambda w: jnp.ones((w,), F32)
    qscale = HEAD_DIM ** -0.5
    gcol = jnp.concatenate([jnp.tile(p["g_q_a"], N_HEADS_A) * qscale, jnp.tile(p["g_k_a"], N_KV_A),
                            ones(KV_A), jnp.tile(p["g_q_b"], N_HEADS_B) * qscale,
                            jnp.tile(p["g_k_b"], N_HEADS_B), ones(MIX_B)]).reshape(1, IN_WIDTH)
    proj = _inproj(xf, mod, pos, p["g_norm_mix"].reshape(1, d), w_in_p, gcol, invf, bd, seq)
    qa, ka, va = proj[:3]
    qkv_b = {1: proj[3:6]}
    for j, dil in enumerate(DILS):
        qkv_b[dil] = [proj[6 + t * len(DILS) + j] for t in range(3)]

    sinks_p = p["sinks_a"][np.array(PAIR_ORDER_A)]
    oa, _ = _attention(qa, ka, va, nbatch=nbatch, seq=seq, dil=1, max_dist=WINDOW_A - 1,
                       kv_shared=True, sinks=sinks_p, want_lse=False)
    obs, lses = [], []
    for window, dil in DILATED_BRANCHES:
        o, lse = _attention(*qkv_b[dil], nbatch=nbatch, seq=seq, dil=dil, max_dist=window // dil,
                            kv_shared=False, want_lse=True)
        obs.append(o)
        lses.append(lse)

    goa = p["g_out_a"][perm].reshape(1, MIX_A)
    w_out = p["w_out"]
    wo_p = jnp.concatenate([w_out[:MIX_A][perm], w_out[MIX_A:]], axis=0).astype(BF16)
    h2, base = _outproj(xf, mod, oa, obs, lses, goa, p["g_out_b"].reshape(1, MIX_B), wo_p,
                        p["g_norm_ffn"].reshape(1, d), p["w_gate_s"].astype(BF16),
                        p["w_up_s"].astype(BF16), p["w_down_s"].astype(BF16), seq)

    top_e_t, gates_t, counts = _router(h2, p["w_router"].T.astype(BF16),
                                       p["router_bias"].reshape(N_EXPERTS, 1))
    bm = EXPERT_ROWS
    counts = counts.reshape(N_EXPERTS).astype(I32)
    padded = (counts + bm - 1) // bm * bm
    pends = jnp.cumsum(padded)
    pstarts = pends - padded
    rows = n * TOP_K + N_EXPERTS * bm
    n_blocks = rows // bm
    n_valid = (pends[-1] // bm).astype(I32)
    blk = jnp.minimum(jnp.arange(n_blocks, dtype=I32), n_valid - 1)
    block_e = jnp.sum((pends[None, :] <= (blk * bm)[:, None]).astype(I32), axis=1)
    block_e = jnp.minimum(block_e, N_EXPERTS - 1)

    tri = (np.arange(TT_ROUTE)[:, None] <= np.arange(TT_ROUTE)[None, :])
    dest_t = _rank(top_e_t, pstarts.astype(F32).reshape(N_EXPERTS, 1), jnp.asarray(tri, BF16))
    n_valid = n_valid.reshape(1)
    row_tok = _invert(dest_t.reshape(n * TOP_K), n, rows)
    ys = _experts(block_e, n_valid, row_tok, h2, p["w_gate_e"], p["w_up_e"], p["w_down_e"])
    out = _combine(dest_t, gates_t.T, base, mod, ys, seq)
    return out.reshape(nbatch, seq, d)


def kernel(x, c, positions, w_ada, b_ada, g_norm_mix, w_in, g_q_a, g_k_a, sinks_a, g_q_b, g_k_b,
           g_out_a, g_out_b, w_out, g_norm_ffn, w_router, router_bias, w_gate_e, w_up_e, w_down_e,
           w_gate_s, w_up_s, w_down_s):
    nbatch, seq, d = x.shape
    depth = w_ada.shape[0]
    params = dict(g_norm_mix=g_norm_mix, w_in=w_in, g_q_a=g_q_a, g_k_a=g_k_a, sinks_a=sinks_a,
                  g_q_b=g_q_b, g_k_b=g_k_b, g_out_a=g_out_a, g_out_b=g_out_b, w_out=w_out,
                  g_norm_ffn=g_norm_ffn, w_router=w_router, router_bias=router_bias,
                  w_gate_e=w_gate_e, w_up_e=w_up_e, w_down_e=w_down_e, w_gate_s=w_gate_s,
                  w_up_s=w_up_s, w_down_s=w_down_s)
    j = np.arange(LANES) % HEAD_DIM
    inv = ROPE_THETA ** (-jnp.arange(0, ROT_DIM, 2, dtype=F32) / ROT_DIM)
    invf = jnp.where(j < ROT_DIM, inv[j % (ROT_DIM // 2)], 0.0).astype(F32).reshape(1, LANES)
    bd = jnp.asarray((np.arange(LANES)[:, None] // HEAD_DIM) == (np.arange(LANES)[None, :] // HEAD_DIM),
                     BF16)
    pos = positions.reshape(nbatch * seq, 1).astype(I32)
    for l in range(depth):
        mod = _adaln(c.astype(F32), w_ada[l], b_ada[l]).reshape(nbatch, 6, d)
        x = _layer(x, mod, pos, (invf, bd), {k: v[l] for k, v in params.items()})
    return x
```

```python
import functools

import numpy as np
import jax
import jax.numpy as jnp
from jax import lax
from jax.experimental import pallas as pl
from jax.experimental.pallas import tpu as pltpu
from jax.experimental.pallas import tpu_sc as plsc

F32 = jnp.float32
BF16 = jnp.bfloat16
I32 = jnp.int32

HEAD_DIM = 64
N_HEADS_A = 8
N_KV_A = 2
WINDOW_A = 128
N_HEADS_B = 8
DILATED_BRANCHES = ((128, 1), (512, 4), (2048, 16))
DILS = tuple(dil for _, dil in DILATED_BRANCHES if dil > 1)
BLOCK = 128
ROT_DIM = HEAD_DIM // 4
ROPE_THETA = 500000.0
MIX_A = N_HEADS_A * HEAD_DIM
KV_A = N_KV_A * HEAD_DIM
MIX_B = N_HEADS_B * HEAD_DIM
N_EXPERTS = 256
TOP_K = 8
N_GROUPS = 8
TOPK_GROUPS = 4
GROUP_SIZE = N_EXPERTS // N_GROUPS
ROUTED_SCALE = 2.5
EPS = 1e-6

LANES = 128
HEADS_PER_VREG = LANES // HEAD_DIM
N_PAIRS = MIX_A // LANES
NEG = -1e30
VMEM_LIMIT = 48 * 1024 * 1024

TT_PROJ = 512
TT_ROUTE = 256
TT_DISPATCH = 256
TT_COMBINE = 128
EXPERT_ROWS = 256
TOK_CHUNK_BLOCKS = 32

PAIR_ORDER_A = tuple(h for p in range(N_PAIRS) for h in (p, p + N_HEADS_A // N_KV_A))


def _params(*sem):
    return pltpu.CompilerParams(dimension_semantics=sem, vmem_limit_bytes=VMEM_LIMIT)


def _silu(t):
    return t / (1.0 + jnp.exp(-t))


def _rms_rows(t):
    return t * lax.rsqrt(jnp.mean(t * t, axis=-1, keepdims=True) + EPS)


def _ada_kernel(c_ref, w_ref, b_ref, o_ref):
    cond = _silu(c_ref[...])
    o_ref[...] = jnp.dot(cond.astype(BF16), w_ref[...].astype(BF16),
                         preferred_element_type=F32) + b_ref[...]


def _adaln(c, w_ada, b_ada):
    nb, d = c.shape
    width = w_ada.shape[1]
    tn = 1024
    return pl.pallas_call(
        _ada_kernel,
        grid=(width // tn,),
        in_specs=[pl.BlockSpec((nb, d), lambda j: (0, 0)),
                  pl.BlockSpec((d, tn), lambda j: (0, j)),
                  pl.BlockSpec((1, tn), lambda j: (0, j))],
        out_specs=pl.BlockSpec((nb, tn), lambda j: (0, j)),
        out_shape=jax.ShapeDtypeStruct((nb, width), F32),
        compiler_params=_params("arbitrary"),
        name="adaln",
    )(c, w_ada, b_ada.reshape(1, width))


COL_QA, COL_KA, COL_VA = 0, MIX_A, MIX_A + KV_A
COL_QB = MIX_A + 2 * KV_A
COL_KB, COL_VB = COL_QB + MIX_B, COL_QB + 2 * MIX_B
IN_WIDTH = COL_VB + MIX_B


def _inproj_kernel(x_ref, mod_ref, pos_ref, gn_ref, w_ref, gcol_ref, invf_ref, bd_ref,
                   qa_ref, ka_ref, va_ref, qb_ref, kb_ref, vb_ref, *rest):
    n_dil = len(DILS)
    dil_refs = [rest[i * n_dil:(i + 1) * n_dil] for i in range(3)]
    qb_scr, kb_scr, vb_scr = rest[3 * n_dil:]
    tt = x_ref.shape[0]
    shift, scale = mod_ref[0:1, :], mod_ref[1:2, :]
    h = _rms_rows(x_ref[...]) * gn_ref[...] * (1.0 + scale) + shift
    proj = jnp.dot(h.astype(BF16), w_ref[...], preferred_element_type=F32)

    ang = pos_ref[...].astype(F32) * invf_ref[...]
    cs, sn = jnp.cos(ang), jnp.sin(ang)
    lane = lax.broadcasted_iota(I32, (1, LANES), 1) % HEAD_DIM
    s_lo = jnp.where(lane < ROT_DIM // 2, -sn, 0.0)
    s_hi = jnp.where((lane >= ROT_DIM // 2) & (lane < ROT_DIM), sn, 0.0)
    bd = bd_ref[...]

    def norm_rope(col0, width, out_ref, scr=None):
        for j in range(width // LANES):
            c = col0 + j * LANES
            t = proj[:, c:c + LANES]
            sq = t * t
            hi = sq.astype(BF16)
            lo = (sq - hi.astype(F32)).astype(BF16)
            ss = (jnp.dot(hi, bd, preferred_element_type=F32)
                  + jnp.dot(lo, bd, preferred_element_type=F32))
            t = t * lax.rsqrt(ss * (1.0 / HEAD_DIM) + EPS) * gcol_ref[:, c:c + LANES]
            t = (t * cs + pltpu.roll(t, LANES - ROT_DIM // 2, 1) * s_lo
                 + pltpu.roll(t, ROT_DIM // 2, 1) * s_hi)
            out_ref[:, j * LANES:(j + 1) * LANES] = t.astype(BF16)
            if scr is not None:
                scr[j] = t

    norm_rope(COL_QA, MIX_A, qa_ref)
    norm_rope(COL_KA, KV_A, ka_ref)
    norm_rope(COL_QB, MIX_B, qb_ref, qb_scr)
    norm_rope(COL_KB, MIX_B, kb_ref, kb_scr)
    va_ref[...] = proj[:, COL_VA:COL_VA + KV_A].astype(BF16)
    vb_ref[...] = proj[:, COL_VB:COL_VB + MIX_B].astype(BF16)
    for j in range(N_PAIRS):
        vb_scr[j] = proj[:, COL_VB + j * LANES:COL_VB + (j + 1) * LANES]
    for scr, outs in zip((qb_scr, kb_scr, vb_scr), dil_refs):
        for dil, out in zip(DILS, outs):
            for r in range(dil):
                for j in range(N_PAIRS):
                    c = r * MIX_B + j * LANES
                    out[:, c:c + LANES] = scr[j, pl.ds(r, tt // dil, stride=dil), :].astype(BF16)


def _inproj(xf, mod, pos, g_norm, w_in_p, gcol, invf, bd, seq):
    n, d = xf.shape
    tt = TT_PROJ
    tiles_per_seq = seq // tt
    shapes = [(n, w) for w in (MIX_A, KV_A, KV_A, MIX_B, MIX_B, MIX_B)]
    shapes += [(n // dil, dil * MIX_B) for _ in range(3) for dil in DILS]
    full = lambda shape: pl.BlockSpec(shape, lambda i: (0,) * len(shape))
    return pl.pallas_call(
        _inproj_kernel,
        grid=(n // tt,),
        in_specs=[pl.BlockSpec((tt, d), lambda i: (i, 0)),
                  pl.BlockSpec((None, 6, d), lambda i: (i // tiles_per_seq, 0, 0)),
                  pl.BlockSpec((tt, 1), lambda i: (i, 0)),
                  full((1, d)), full((d, IN_WIDTH)), full((1, IN_WIDTH)),
                  full((1, LANES)), full((LANES, LANES))],
        out_specs=[pl.BlockSpec((tt * r // n, w), lambda i: (i, 0)) for r, w in shapes],
        out_shape=[jax.ShapeDtypeStruct(s, BF16) for s in shapes],
        scratch_shapes=[pltpu.VMEM((N_PAIRS, tt, LANES), F32)] * 3,
        compiler_params=_params("parallel"),
        name="inproj",
    )(xf, mod, pos, g_norm, w_in_p, gcol, invf, bd)


def _attn_kernel(*refs, kv_shared, max_dist, use_prev, has_sinks, want_lse):
    refs = list(refs)
    sink_ref = refs.pop(0) if has_sinks else None
    q_ref = refs.pop(0)
    kp_ref = refs.pop(0) if use_prev else None
    kc_ref = refs.pop(0)
    vp_ref = refs.pop(0) if use_prev else None
    vc_ref = refs.pop(0)
    o_ref = refs.pop(0)
    lse_ref = refs.pop(0) if want_lse else None

    blk = pl.program_id(2)
    nq = 2 * BLOCK
    nk = 2 * BLOCK if use_prev else BLOCK
    qpos = lax.broadcasted_iota(I32, (nq, nk), 0) % BLOCK
    kpos = lax.broadcasted_iota(I32, (nq, nk), 1)
    if use_prev:
        dist = qpos + BLOCK - kpos
        valid = (dist >= 0) & (dist <= max_dist) & ((kpos >= BLOCK) | (blk > 0))
    else:
        dist = qpos - kpos
        valid = (dist >= 0) & (dist <= max_dist)
    lane = lax.broadcasted_iota(I32, (nq, LANES), 1)
    row = lax.broadcasted_iota(I32, (nq, LANES), 0)
    own_half = (lane < HEAD_DIM) == (row < BLOCK)
    left_lanes = lax.broadcasted_iota(I32, (BLOCK, LANES), 1) < HEAD_DIM
    lane8 = lax.broadcasted_iota(I32, (BLOCK, 2 * N_PAIRS), 1)
    lse_blk = jnp.zeros((BLOCK, 2 * N_PAIRS), F32)

    for p in range(N_PAIRS):
        cq = slice(p * LANES, (p + 1) * LANES)
        ck = slice(0, LANES) if kv_shared else cq
        qp = q_ref[:, cq]
        qs = jnp.concatenate([qp, qp], axis=0)
        qs = jnp.where(own_half, qs, jnp.zeros_like(qs))
        if use_prev:
            k = jnp.concatenate([kp_ref[:, ck], kc_ref[:, ck]], axis=0)
            v = jnp.concatenate([vp_ref[:, ck], vc_ref[:, ck]], axis=0)
        else:
            k, v = kc_ref[:, ck], vc_ref[:, ck]
        s = lax.dot_general(qs, k, (((1,), (1,)), ((), ())), preferred_element_type=F32)
        s = jnp.where(valid, s, NEG)
        m = jnp.max(s, axis=-1, keepdims=True)
        if has_sinks:
            rows1 = lax.broadcasted_iota(I32, (nq, 1), 0)
            sink = jnp.where(rows1 < BLOCK, sink_ref[2 * p], sink_ref[2 * p + 1])
            m = jnp.maximum(m, sink)
        e = jnp.exp(s - m)
        l = jnp.sum(e, axis=-1, keepdims=True)
        if has_sinks:
            l = l + jnp.exp(sink - m)
        o = jnp.dot(e.astype(BF16), v, preferred_element_type=F32) / l
        o_ref[:, cq] = jnp.where(left_lanes, o[:BLOCK], o[BLOCK:]).astype(BF16)
        if want_lse:
            lse = m + jnp.log(l)
            lse_blk = (lse_blk + jnp.where(lane8 == 2 * p, lse[:BLOCK], 0.0)
                       + jnp.where(lane8 == 2 * p + 1, lse[BLOCK:], 0.0))
    if want_lse:
        lse_ref[...] = lse_blk


def _attention(q, k, v, *, nbatch, seq, dil, max_dist, kv_shared, sinks=None, want_lse):
    length = seq // dil
    nblk = length // BLOCK
    use_prev = nblk > 1
    kw = k.shape[1] // dil
    view = lambda t: t.reshape(nbatch, length, t.shape[1])
    cur = lambda b, r, i: (b, i, r)
    prev = lambda b, r, i: (b, jnp.maximum(i - 1, 0), r)
    in_specs, args = [], []
    if sinks is not None:
        in_specs.append(pl.BlockSpec(memory_space=pltpu.SMEM))
        args.append(sinks)
    in_specs.append(pl.BlockSpec((None, BLOCK, MIX_B), cur))
    args.append(view(q))
    for t in (k, v):
        if use_prev:
            in_specs.append(pl.BlockSpec((None, BLOCK, kw), prev))
            args.append(view(t))
        in_specs.append(pl.BlockSpec((None, BLOCK, kw), cur))
        args.append(view(t))
    out_specs = [pl.BlockSpec((None, BLOCK, MIX_B), cur)]
    out_shape = [jax.ShapeDtypeStruct((nbatch, length, dil * MIX_B), BF16)]
    if want_lse:
        out_specs.append(pl.BlockSpec((None, None, BLOCK, N_HEADS_B), lambda b, r, i: (b, r, i, 0)))
        out_shape.append(jax.ShapeDtypeStruct((nbatch, dil, length, N_HEADS_B), F32))
    outs = pl.pallas_call(
        functools.partial(_attn_kernel, kv_shared=kv_shared, max_dist=max_dist, use_prev=use_prev,
                          has_sinks=sinks is not None, want_lse=want_lse),
        grid=(nbatch, dil, nblk),
        in_specs=in_specs, out_specs=out_specs, out_shape=out_shape,
        compiler_params=_params("parallel", "parallel", "arbitrary"),
        name=f"attn_d{dil}" + ("_swa" if kv_shared else ""),
    )(*args)
    o = outs[0].reshape(nbatch * length, dil * MIX_B)
    if not want_lse:
        return o, None
    lse = outs[1].transpose(0, 2, 1, 3).reshape(nbatch * seq, N_HEADS_B)
    return o, lse


def _expand_heads(w, width):
    head = lax.broadcasted_iota(I32, (1, width), 1) // HEAD_DIM
    out = jnp.zeros((w.shape[0], width), F32)
    for hd in range(w.shape[1]):
        out = jnp.where(head == hd, w[:, hd:hd + 1], out)
    return out


def _outproj_kernel(x_ref, mod_ref, oa_ref, ob1_ref, ob2_ref, ob3_ref, l1_ref, l2_ref, l3_ref,
                    goa_ref, gob_ref, wo_ref, gf_ref, wgs_ref, wus_ref, wds_ref,
                    h2_ref, base_ref, ob_scr):
    tt = x_ref.shape[0]
    gate_a = mod_ref[2:3, :]
    shift_m, scale_m, gate_m = mod_ref[3:4, :], mod_ref[4:5, :], mod_ref[5:6, :]

    def token_major(ref, dil):
        if dil == 1:
            return ref[...].astype(F32)
        for r in range(dil):
            for j in range(N_PAIRS):
                c = r * MIX_B + j * LANES
                ob_scr[j, pl.ds(r, tt // dil, stride=dil), :] = ref[:, c:c + LANES].astype(F32)
        return jnp.concatenate([ob_scr[j] for j in range(N_PAIRS)], axis=1)

    l1, l2, l3 = l1_ref[...], l2_ref[...], l3_ref[...]
    mx = jnp.maximum(jnp.maximum(l1, l2), l3)
    e1, e2, e3 = jnp.exp(l1 - mx), jnp.exp(l2 - mx), jnp.exp(l3 - mx)
    den = e1 + e2 + e3
    dils = [dil for _, dil in DILATED_BRANCHES]
    ob = _expand_heads(e1 / den, MIX_B) * token_major(ob1_ref, dils[0])
    ob = ob + _expand_heads(e2 / den, MIX_B) * token_major(ob2_ref, dils[1])
    ob = ob + _expand_heads(e3 / den, MIX_B) * token_major(ob3_ref, dils[2])
    ob = _rms_rows(ob) * gob_ref[...]
    oa = _rms_rows(oa_ref[...].astype(F32)) * goa_ref[...]
    y = (jnp.dot(oa.astype(BF16), wo_ref[0:MIX_A, :], preferred_element_type=F32)
         + jnp.dot(ob.astype(BF16), wo_ref[MIX_A:MIX_A + MIX_B, :], preferred_element_type=F32))
    x1 = x_ref[...] + gate_a * y
    h2 = _rms_rows(x1) * gf_ref[...] * (1.0 + scale_m) + shift_m
    h2_ref[...] = h2
    hb = h2.astype(BF16)
    act = (_silu(jnp.dot(hb, wgs_ref[...], preferred_element_type=F32))
           * jnp.dot(hb, wus_ref[...], preferred_element_type=F32))
    shared = jnp.dot(act.astype(BF16), wds_ref[...], preferred_element_type=F32)
    base_ref[...] = x1 + gate_m * shared


def _outproj(xf, mod, oa, obs, lses, goa, gob, wo_p, gf, wgs, wus, wds, seq):
    n, d = xf.shape
    tt = TT_PROJ // 2
    tiles_per_seq = seq // tt
    tile = lambda w: pl.BlockSpec((tt, w), lambda i: (i, 0))
    full = lambda shape: pl.BlockSpec(shape, lambda i: (0,) * len(shape))
    sd = wgs.shape[1]
    dilated = [pl.BlockSpec((tt // dil, dil * MIX_B), lambda i: (i, 0)) for _, dil in DILATED_BRANCHES]
    return pl.pallas_call(
        _outproj_kernel,
        grid=(n // tt,),
        in_specs=[tile(d), pl.BlockSpec((None, 6, d), lambda i: (i // tiles_per_seq, 0, 0)),
                  tile(MIX_A), *dilated,
                  tile(N_HEADS_B), tile(N_HEADS_B), tile(N_HEADS_B),
                  full((1, MIX_A)), full((1, MIX_B)), full((MIX_A + MIX_B, d)), full((1, d)),
                  full((d, sd)), full((d, sd)), full((sd, d))],
        out_specs=[tile(d), tile(d)],
        out_shape=[jax.ShapeDtypeStruct((n, d), F32), jax.ShapeDtypeStruct((n, d), F32)],
        scratch_shapes=[pltpu.VMEM((N_PAIRS, tt, LANES), F32)],
        compiler_params=_params("parallel"),
        name="outproj",
    )(xf, mod, oa, *obs, *lses, goa, gob, wo_p, gf, wgs, wus, wds)


def _router_kernel(h_ref, wrt_ref, bias_ref, e_ref, g_ref, cnt_ref):
    tt = h_ref.shape[0]
    logits = lax.dot_general(wrt_ref[...], h_ref[...].astype(BF16), (((1,), (1,)), ((), ())),
                             preferred_element_type=F32)
    scores = 1.0 / (1.0 + jnp.exp(-logits))
    biased = scores + bias_ref[...]
    ninf = -jnp.inf

    j32 = lax.broadcasted_iota(I32, (GROUP_SIZE, tt), 0).astype(F32)
    grp = []
    for g in range(N_GROUPS):
        bg = biased[g * GROUP_SIZE:(g + 1) * GROUP_SIZE, :]
        m1 = jnp.max(bg, axis=0, keepdims=True)
        i1 = jnp.min(jnp.where(bg == m1, j32, float(GROUP_SIZE)), axis=0, keepdims=True)
        m2 = jnp.max(jnp.where(j32 == i1, ninf, bg), axis=0, keepdims=True)
        grp.append(m1 + m2)
    grp = jnp.concatenate(grp, axis=0)
    g8 = lax.broadcasted_iota(I32, (N_GROUPS, tt), 0).astype(F32)
    chosen = jnp.zeros((N_GROUPS, tt), F32)
    for _ in range(TOPK_GROUPS):
        gm = jnp.max(grp, axis=0, keepdims=True)
        gi = jnp.min(jnp.where(grp == gm, g8, float(N_GROUPS)), axis=0, keepdims=True)
        hit = g8 == gi
        chosen = jnp.where(hit, 1.0, chosen)
        grp = jnp.where(hit, ninf, grp)
    masked = jnp.concatenate(
        [jnp.where(chosen[g:g + 1, :] > 0.0, biased[g * GROUP_SIZE:(g + 1) * GROUP_SIZE, :], ninf)
         for g in range(N_GROUPS)], axis=0)

    eio = lax.broadcasted_iota(I32, (N_EXPERTS, tt), 0).astype(F32)
    picked = jnp.zeros((N_EXPERTS, tt), F32)
    es, gs = [], []
    for _ in range(TOP_K):
        m = jnp.max(masked, axis=0, keepdims=True)
        idx = jnp.min(jnp.where(masked == m, eio, float(N_EXPERTS)), axis=0, keepdims=True)
        hit = eio == idx
        gs.append(jnp.sum(jnp.where(hit, scores, 0.0), axis=0, keepdims=True))
        es.append(idx)
        picked = jnp.where(hit, 1.0, picked)
        masked = jnp.where(hit, ninf, masked)
    gates = jnp.concatenate(gs, axis=0)
    e_ref[...] = jnp.concatenate(es, axis=0).astype(I32)
    g_ref[...] = gates / jnp.sum(gates, axis=0, keepdims=True) * ROUTED_SCALE

    @pl.when(pl.program_id(0) == 0)
    def _():
        cnt_ref[...] = jnp.zeros_like(cnt_ref)
    cnt_ref[...] += jnp.sum(picked, axis=1, keepdims=True)


def _router(h2, wrt, bias_col):
    n, d = h2.shape
    tt = TT_ROUTE
    return pl.pallas_call(
        _router_kernel,
        grid=(n // tt,),
        in_specs=[pl.BlockSpec((tt, d), lambda i: (i, 0)),
                  pl.BlockSpec((N_EXPERTS, d), lambda i: (0, 0)),
                  pl.BlockSpec((N_EXPERTS, 1), lambda i: (0, 0))],
        out_specs=[pl.BlockSpec((TOP_K, tt), lambda i: (0, i)),
                   pl.BlockSpec((TOP_K, tt), lambda i: (0, i)),
                   pl.BlockSpec((N_EXPERTS, 1), lambda i: (0, 0))],
        out_shape=[jax.ShapeDtypeStruct((TOP_K, n), I32),
                   jax.ShapeDtypeStruct((TOP_K, n), F32),
                   jax.ShapeDtypeStruct((N_EXPERTS, 1), F32)],
        compiler_params=_params("arbitrary"),
        name="router",
    )(h2, wrt, bias_col)


def _rank_kernel(e_ref, pstart_ref, tri_ref, dest_ref, carry_ref):
    tt = e_ref.shape[1]

    @pl.when(pl.program_id(0) == 0)
    def _():
        carry_ref[...] = pstart_ref[...]

    e = e_ref[...]
    eio = lax.broadcasted_iota(I32, (N_EXPERTS, tt), 0)
    mask = jnp.zeros((N_EXPERTS, tt), F32)
    for k in range(TOP_K):
        mask = jnp.where(eio == e[k:k + 1, :], 1.0, mask)
    incl = jnp.dot(mask.astype(BF16), tri_ref[...], preferred_element_type=F32)
    pos = incl - mask + carry_ref[...]
    dest = [jnp.sum(jnp.where(eio == e[k:k + 1, :], pos, 0.0), axis=0, keepdims=True)
            for k in range(TOP_K)]
    dest_ref[...] = jnp.concatenate(dest, axis=0).astype(I32)
    carry_ref[...] += incl[:, tt - 1:tt]


def _rank(top_e_t, pstart_col, tri):
    n = top_e_t.shape[1]
    tt = TT_ROUTE
    return pl.pallas_call(
        _rank_kernel,
        grid=(n // tt,),
        in_specs=[pl.BlockSpec((TOP_K, tt), lambda i: (0, i)),
                  pl.BlockSpec((N_EXPERTS, 1), lambda i: (0, 0)),
                  pl.BlockSpec((tt, tt), lambda i: (0, 0))],
        out_specs=pl.BlockSpec((TOP_K, tt), lambda i: (0, i)),
        out_shape=jax.ShapeDtypeStruct((TOP_K, n), I32),
        scratch_shapes=[pltpu.VMEM((N_EXPERTS, 1), F32)],
        compiler_params=_params("arbitrary"),
        name="rank",
    )(top_e_t, pstart_col, tri)


def _row_copy(src_ref, src_row, dst_ref, dst_row, sem):
    return pltpu.make_async_copy(src_ref.at[pl.ds(src_row, 1)], dst_ref.at[pl.ds(dst_row, 1)], sem)


SC_CORES, SC_SUBCORES, SC_LANES = 2, 16, 16
SC_WORKERS = SC_CORES * SC_SUBCORES
INVERT_CHUNK = 8192


def _invert(dest_flat, n_tokens, rows):
    per = rows // SC_WORKERS
    n_assign = dest_flat.shape[0]
    assert rows % (SC_WORKERS * SC_LANES) == 0 and n_assign % INVERT_CHUNK == 0
    assert n_tokens & (n_tokens - 1) == 0

    @functools.partial(
        pl.kernel, mesh=plsc.VectorSubcoreMesh(core_axis_name="c", subcore_axis_name="s"),
        out_type=jax.ShapeDtypeStruct((rows,), I32),
        scratch_types=[pltpu.VMEM((INVERT_CHUNK,), I32), pltpu.VMEM((per,), I32)],
        compiler_params=pltpu.CompilerParams(needs_layout_passes=False))
    def invert(dest_hbm, out_hbm, staged, local):
        base = (lax.axis_index("s") * SC_CORES + lax.axis_index("c")) * per

        @pl.loop(0, per, step=SC_LANES)
        def _(i):
            local[pl.ds(i, SC_LANES)] = jnp.zeros((SC_LANES,), I32)

        lane = lax.iota(I32, SC_LANES)

        @pl.loop(0, n_assign // INVERT_CHUNK)
        def _(c):
            pltpu.sync_copy(dest_hbm.at[pl.ds(c * INVERT_CHUNK, INVERT_CHUNK)], staged)

            @pl.loop(0, INVERT_CHUNK, step=SC_LANES)
            def _(i):
                rel = staged[pl.ds(i, SC_LANES)] - base
                mine = (rel >= 0) & (rel < per)
                tok = (c * INVERT_CHUNK + i + lane) & (n_tokens - 1)
                plsc.store_scatter(local, [jnp.where(mine, rel, 0)], tok, mask=mine)

        pltpu.sync_copy(local, out_hbm.at[pl.ds(base, per)])

    return invert(dest_flat)


def _experts_kernel(be_ref, nv_ref, tok_hbm, h_ref, wg_ref, wu_ref, wd_ref, ys_ref,
                    x0, x1, wgb, wub, wdb, tok_s, sem, tok_sem):
    j = pl.program_id(0)
    nv = nv_ref[0]
    bm = x0.shape[0]
    chunk = tok_s.shape[1]
    cb = chunk // bm
    n_chunks = tok_hbm.shape[0] // chunk

    def tok_copy(c):
        return pltpu.make_async_copy(tok_hbm.at[pl.ds(pl.multiple_of(c * chunk, chunk), chunk)],
                                     tok_s.at[c % 2], tok_sem.at[c % 2])

    def gather(block, buf, s):
        slot, off = (block // cb) % 2, (block % cb) * bm

        def issue(i, carry):
            pltpu.async_copy(h_ref.at[pl.ds(tok_s[slot, off + i], 1)], buf.at[pl.ds(i, 1)], sem.at[s],
                             priority=1)
            return carry

        lax.fori_loop(0, bm, issue, 0, unroll=8)

    def wait_rows(buf, s):
        pltpu.make_async_copy(h_ref.at[pl.ds(0, bm)], buf, sem.at[s]).wait()

    @pl.when(j == 0)
    def _():
        tok_copy(0).start()
        tok_copy(0).wait()
        gather(0, x0, 0)

    @pl.when(jnp.logical_and(j % cb == 0, j // cb + 1 < n_chunks))
    def _():
        tok_copy(j // cb + 1).start()

    @pl.when(jnp.logical_and((j + 1) % cb == 0, (j + 1) // cb < n_chunks))
    def _():
        tok_copy((j + 1) // cb).wait()

    fresh = jnp.logical_or(j == 0, be_ref[j] != be_ref[jnp.maximum(j - 1, 0)])

    @pl.when(jnp.logical_and(j < nv, fresh))
    def _():
        wgb[...] = wg_ref[...].astype(BF16)
        wub[...] = wu_ref[...].astype(BF16)
        wdb[...] = wd_ref[...].astype(BF16)

    def step(cur, nxt, s):
        wait_rows(cur, s)
        gather(j + 1, nxt, 1 - s)
        xb = cur[...].astype(BF16)
        act = (_silu(jnp.dot(xb, wgb[...], preferred_element_type=F32))
               * jnp.dot(xb, wub[...], preferred_element_type=F32))
        ys_ref[...] = jnp.dot(act.astype(BF16), wdb[...], preferred_element_type=F32)

    @pl.when(jnp.logical_and(j < nv, j % 2 == 0))
    def _():
        step(x0, x1, 0)

    @pl.when(jnp.logical_and(j < nv, j % 2 == 1))
    def _():
        step(x1, x0, 1)

    @pl.when(jnp.logical_and(j == nv, j % 2 == 0))
    def _():
        wait_rows(x0, 0)

    @pl.when(jnp.logical_and(j == nv, j % 2 == 1))
    def _():
        wait_rows(x1, 1)


def _experts(block_e, n_valid, row_tok, h2, wg, wu, wd):
    rows = row_tok.shape[0]
    d = h2.shape[1]
    bm = EXPERT_ROWS
    n_blocks = rows // bm
    assert n_blocks % TOK_CHUNK_BLOCKS == 0
    f = wg.shape[2]
    row_blk = lambda j, be, nv: (jnp.minimum(j, nv[0] - 1), 0)
    w_blk = lambda j, be, nv: (be[j], 0, 0)
    return pl.pallas_call(
        _experts_kernel,
        grid_spec=pltpu.PrefetchScalarGridSpec(
            num_scalar_prefetch=2,
            grid=(n_blocks,),
            in_specs=[pl.BlockSpec(memory_space=pl.ANY),
                      pl.BlockSpec(memory_space=pl.ANY),
                      pl.BlockSpec((None, d, f), w_blk),
                      pl.BlockSpec((None, d, f), w_blk),
                      pl.BlockSpec((None, f, d), w_blk)],
            out_specs=pl.BlockSpec((bm, d), row_blk),
            scratch_shapes=[pltpu.VMEM((bm, d), F32), pltpu.VMEM((bm, d), F32),
                            pltpu.VMEM((d, f), BF16), pltpu.VMEM((d, f), BF16),
                            pltpu.VMEM((f, d), BF16),
                            pltpu.SMEM((2, TOK_CHUNK_BLOCKS * bm), I32),
                            pltpu.SemaphoreType.DMA((2,)), pltpu.SemaphoreType.DMA((2,))]),
        out_shape=jax.ShapeDtypeStruct((rows, d), F32),
        compiler_params=_params("arbitrary"),
        name="experts",
    )(block_e, n_valid, row_tok, h2, wg, wu, wd)


def _combine_kernel(dest_ref, gates_ref, base_ref, mod_ref, ys_ref, out_ref, buf, sem):
    tt = base_ref.shape[0]

    def issue(t, carry):
        for k in range(TOP_K):
            _row_copy(ys_ref, dest_ref[k, t], buf.at[k], t, sem).start()
        return carry

    def drain(t, carry):
        for k in range(TOP_K):
            _row_copy(ys_ref, dest_ref[k, t], buf.at[k], t, sem).wait()
        return carry

    lax.fori_loop(0, tt, issue, 0)
    lax.fori_loop(0, tt, drain, 0)
    gates = gates_ref[...]
    routed = gates[:, 0:1] * buf[0]
    for k in range(1, TOP_K):
        routed = routed + gates[:, k:k + 1] * buf[k]
    out_ref[...] = base_ref[...] + mod_ref[5:6, :] * routed


def _combine(dest_t, gates, base, mod, ys, seq):
    n, d = base.shape
    tt = TT_COMBINE
    tiles_per_seq = seq // tt
    return pl.pallas_call(
        _combine_kernel,
        grid=(n // tt,),
        in_specs=[pl.BlockSpec((TOP_K, tt), lambda i: (0, i), memory_space=pltpu.SMEM),
                  pl.BlockSpec((tt, TOP_K), lambda i: (i, 0)),
                  pl.BlockSpec((tt, d), lambda i: (i, 0)),
                  pl.BlockSpec((None, 6, d), lambda i: (i // tiles_per_seq, 0, 0)),
                  pl.BlockSpec(memory_space=pl.ANY)],
        out_specs=pl.BlockSpec((tt, d), lambda i: (i, 0)),
        out_shape=jax.ShapeDtypeStruct((n, d), F32),
        scratch_shapes=[pltpu.VMEM((TOP_K, tt, d), F32), pltpu.SemaphoreType.DMA],
        compiler_params=_params("arbitrary"),
        name="combine",
    )(dest_t, gates, base, mod, ys)


def _layer(x, mod, pos, rope, p):
    nbatch, seq, d = x.shape
    n = nbatch * seq
    xf = x.reshape(n, d)
    invf, bd = rope

    perm = np.concatenate([np.arange(h * HEAD_DIM, (h + 1) * HEAD_DIM) for h in PAIR_ORDER_A])
    w_in = p["w_in"]
    w_in_p = jnp.concatenate([w_in[:, :MIX_A][:, perm], w_in[:, MIX_A:]], axis=1).astype(BF16)
    ones = lambda w: jnp.ones((w,), F32)
    qscale = HEAD_DIM ** -0.5
    gcol = jnp.concatenate([jnp.tile(p["g_q_a"], N_HEADS_A) * qscale, jnp.tile(p["g_k_a"], N_KV_A),
                            ones(KV_A), jnp.tile(p["g_q_b"], N_HEADS_B) * qscale,
                            jnp.tile(p["g_k_b"], N_HEADS_B), ones(MIX_B)]).reshape(1, IN_WIDTH)
    proj = _inproj(xf, mod, pos, p["g_norm_mix"].reshape(1, d), w_in_p, gcol, invf, bd, seq)
    qa, ka, va = proj[:3]
    qkv_b = {1: proj[3:6]}
    for j, dil in enumerate(DILS):
        qkv_b[dil] = [proj[6 + t * len(DILS) + j] for t in range(3)]

    sinks_p = p["sinks_a"][np.array(PAIR_ORDER_A)]
    oa, _ = _attention(qa, ka, va, nbatch=nbatch, seq=seq, dil=1, max_dist=WINDOW_A - 1,
                       kv_shared=True, sinks=sinks_p, want_lse=False)
    obs, lses = [], []
    for window, dil in DILATED_BRANCHES:
        o, lse = _attention(*qkv_b[dil], nbatch=nbatch, seq=seq, dil=dil, max_dist=window // dil,
                            kv_shared=False, want_lse=True)
        obs.append(o)
        lses.append(lse)

    goa = p["g_out_a"][perm].reshape(1, MIX_A)
    w_out = p["w_out"]
    wo_p = jnp.concatenate([w_out[:MIX_A][perm], w_out[MIX_A:]], axis=0).astype(BF16)
    h2, base = _outproj(xf, mod, oa, obs, lses, goa, p["g_out_b"].reshape(1, MIX_B), wo_p,
                        p["g_norm_ffn"].reshape(1, d), p["w_gate_s"].astype(BF16),
                        p["w_up_s"].astype(BF16), p["w_down_s"].astype(BF16), seq)

    top_e_t, gates_t, counts = _router(h2, p["w_router"].T.astype(BF16),
                                       p["router_bias"].reshape(N_EXPERTS, 1))
    bm = EXPERT_ROWS
    counts = counts.reshape(N_EXPERTS).astype(I32)
    padded = (counts + bm - 1) // bm * bm
    pends = jnp.cumsum(padded)
    pstarts = pends - padded
    rows = n * TOP_K + N_EXPERTS * bm
    n_blocks = rows // bm
    n_valid = (pends[-1] // bm).astype(I32)
    blk = jnp.minimum(jnp.arange(n_blocks, dtype=I32), n_valid - 1)
    block_e = jnp.sum((pends[None, :] <= (blk * bm)[:, None]).astype(I32), axis=1)
    block_e = jnp.minimum(block_e, N_EXPERTS - 1)

    tri = (np.arange(TT_ROUTE)[:, None] <= np.arange(TT_ROUTE)[None, :])
    dest_t = _rank(top_e_t, pstarts.astype(F32).reshape(N_EXPERTS, 1), jnp.asarray(tri, BF16))
    n_valid = n_valid.reshape(1)
    row_tok = _invert(dest_t.reshape(n * TOP_K), n, rows)
    ys = _experts(block_e, n_valid, row_tok, h2, p["w_gate_e"], p["w_up_e"], p["w_down_e"])
    out = _combine(dest_t, gates_t.T, base, mod, ys, seq)
    return out.reshape(nbatch, seq, d)


def kernel(x, c, positions, w_ada, b_ada, g_norm_mix, w_in, g_q_a, g_k_a, sinks_a, g_q_b, g_k_b,
           g_out_a, g_out_b, w_out, g_norm_ffn, w_router, router_bias, w_gate_e, w_up_e, w_down_e,
           w_gate_s, w_up_s, w_down_s):
    nbatch, seq, d = x.shape
    depth = w_ada.shape[0]
    params = dict(g_norm_mix=g_norm_mix, w_in=w_in, g_q_a=g_q_a, g_k_a=g_k_a, sinks_a=sinks_a,
                  g_q_b=g_q_b, g_k_b=g_k_b, g_out_a=g_out_a, g_out_b=g_out_b, w_out=w_out,
                  g_norm_ffn=g_norm_ffn, w_router=w_router, router_bias=router_bias,
                  w_gate_e=w_gate_e, w_up_e=w_up_e, w_down_e=w_down_e, w_gate_s=w_gate_s,
                  w_up_s=w_up_s, w_down_s=w_down_s)
    j = np.arange(LANES) % HEAD_DIM
    inv = ROPE_THETA ** (-jnp.arange(0, ROT_DIM, 2, dtype=F32) / ROT_DIM)
    invf = jnp.where(j < ROT_DIM, inv[j % (ROT_DIM // 2)], 0.0).astype(F32).reshape(1, LANES)
    bd = jnp.asarray((np.arange(LANES)[:, None] // HEAD_DIM) == (np.arange(LANES)[None, :] // HEAD_DIM),
                     BF16)
    pos = positions.reshape(nbatch * seq, 1).astype(I32)
    for l in range(depth):
        mod = _adaln(c.astype(F32), w_ada[l], b_ada[l]).reshape(nbatch, 6, d)
        x = _layer(x, mod, pos, (invf, bd), {k: v[l] for k, v in params.items()})
    return x
```

```python
import functools

import numpy as np
import jax
import jax.numpy as jnp
from jax import lax
from jax.experimental import pallas as pl
from jax.experimental.pallas import tpu as pltpu
from jax.experimental.pallas import tpu_sc as plsc

F32 = jnp.float32
BF16 = jnp.bfloat16
I32 = jnp.int32

HEAD_DIM = 64
N_HEADS_A = 8
N_KV_A = 2
WINDOW_A = 128
N_HEADS_B = 8
DILATED_BRANCHES = ((128, 1), (512, 4), (2048, 16))
DILS = tuple(dil for _, dil in DILATED_BRANCHES if dil > 1)
BLOCK = 128
ROT_DIM = HEAD_DIM // 4
ROPE_THETA = 500000.0
MIX_A = N_HEADS_A * HEAD_DIM
KV_A = N_KV_A * HEAD_DIM
MIX_B = N_HEADS_B * HEAD_DIM
N_EXPERTS = 256
TOP_K = 8
N_GROUPS = 8
TOPK_GROUPS = 4
GROUP_SIZE = N_EXPERTS // N_GROUPS
ROUTED_SCALE = 2.5
EPS = 1e-6

LANES = 128
HEADS_PER_VREG = LANES // HEAD_DIM
N_PAIRS = MIX_A // LANES
NEG = -1e30
VMEM_LIMIT = 48 * 1024 * 1024

TT_PROJ = 512
TT_ROUTE = 256
TT_DISPATCH = 256
TT_COMBINE = 128
EXPERT_ROWS = 256
TOK_CHUNK_BLOCKS = 32
GATHER_DEPTH = 4

PAIR_ORDER_A = tuple(h for p in range(N_PAIRS) for h in (p, p + N_HEADS_A // N_KV_A))


def _params(*sem):
    return pltpu.CompilerParams(dimension_semantics=sem, vmem_limit_bytes=VMEM_LIMIT)


def _silu(t):
    return t / (1.0 + jnp.exp(-t))


def _rms_rows(t):
    return t * lax.rsqrt(jnp.mean(t * t, axis=-1, keepdims=True) + EPS)


def _ada_kernel(c_ref, w_ref, b_ref, o_ref):
    cond = _silu(c_ref[...])
    o_ref[...] = jnp.dot(cond.astype(BF16), w_ref[...].astype(BF16),
                         preferred_element_type=F32) + b_ref[...]


def _adaln(c, w_ada, b_ada):
    nb, d = c.shape
    width = w_ada.shape[1]
    tn = 1024
    return pl.pallas_call(
        _ada_kernel,
        grid=(width // tn,),
        in_specs=[pl.BlockSpec((nb, d), lambda j: (0, 0)),
                  pl.BlockSpec((d, tn), lambda j: (0, j)),
                  pl.BlockSpec((1, tn), lambda j: (0, j))],
        out_specs=pl.BlockSpec((nb, tn), lambda j: (0, j)),
        out_shape=jax.ShapeDtypeStruct((nb, width), F32),
        compiler_params=_params("arbitrary"),
        name="adaln",
    )(c, w_ada, b_ada.reshape(1, width))


COL_QA, COL_KA, COL_VA = 0, MIX_A, MIX_A + KV_A
COL_QB = MIX_A + 2 * KV_A
COL_KB, COL_VB = COL_QB + MIX_B, COL_QB + 2 * MIX_B
IN_WIDTH = COL_VB + MIX_B


def _inproj_kernel(x_ref, mod_ref, pos_ref, gn_ref, w_ref, gcol_ref, invf_ref, bd_ref,
                   qa_ref, ka_ref, va_ref, qb_ref, kb_ref, vb_ref, *rest):
    n_dil = len(DILS)
    dil_refs = [rest[i * n_dil:(i + 1) * n_dil] for i in range(3)]
    qb_scr, kb_scr, vb_scr = rest[3 * n_dil:]
    tt = x_ref.shape[0]
    shift, scale = mod_ref[0:1, :], mod_ref[1:2, :]
    h = _rms_rows(x_ref[...]) * gn_ref[...] * (1.0 + scale) + shift
    proj = jnp.dot(h.astype(BF16), w_ref[...], preferred_element_type=F32)

    ang = pos_ref[...].astype(F32) * invf_ref[...]
    cs, sn = jnp.cos(ang), jnp.sin(ang)
    lane = lax.broadcasted_iota(I32, (1, LANES), 1) % HEAD_DIM
    s_lo = jnp.where(lane < ROT_DIM // 2, -sn, 0.0)
    s_hi = jnp.where((lane >= ROT_DIM // 2) & (lane < ROT_DIM), sn, 0.0)
    bd = bd_ref[...]

    def norm_rope(col0, width, out_ref, scr=None):
        for j in range(width // LANES):
            c = col0 + j * LANES
            t = proj[:, c:c + LANES]
            sq = t * t
            hi = sq.astype(BF16)
            lo = (sq - hi.astype(F32)).astype(BF16)
            ss = (jnp.dot(hi, bd, preferred_element_type=F32)
                  + jnp.dot(lo, bd, preferred_element_type=F32))
            t = t * lax.rsqrt(ss * (1.0 / HEAD_DIM) + EPS) * gcol_ref[:, c:c + LANES]
            t = (t * cs + pltpu.roll(t, LANES - ROT_DIM // 2, 1) * s_lo
                 + pltpu.roll(t, ROT_DIM // 2, 1) * s_hi)
            out_ref[:, j * LANES:(j + 1) * LANES] = t.astype(BF16)
            if scr is not None:
                scr[j] = t

    norm_rope(COL_QA, MIX_A, qa_ref)
    norm_rope(COL_KA, KV_A, ka_ref)
    norm_rope(COL_QB, MIX_B, qb_ref, qb_scr)
    norm_rope(COL_KB, MIX_B, kb_ref, kb_scr)
    va_ref[...] = proj[:, COL_VA:COL_VA + KV_A].astype(BF16)
    vb_ref[...] = proj[:, COL_VB:COL_VB + MIX_B].astype(BF16)
    for j in range(N_PAIRS):
        vb_scr[j] = proj[:, COL_VB + j * LANES:COL_VB + (j + 1) * LANES]
    for scr, outs in zip((qb_scr, kb_scr, vb_scr), dil_refs):
        for dil, out in zip(DILS, outs):
            for r in range(dil):
                for j in range(N_PAIRS):
                    c = r * MIX_B + j * LANES
                    out[:, c:c + LANES] = scr[j, pl.ds(r, tt // dil, stride=dil), :].astype(BF16)


def _inproj(xf, mod, pos, g_norm, w_in_p, gcol, invf, bd, seq):
    n, d = xf.shape
    tt = TT_PROJ
    tiles_per_seq = seq // tt
    shapes = [(n, w) for w in (MIX_A, KV_A, KV_A, MIX_B, MIX_B, MIX_B)]
    shapes += [(n // dil, dil * MIX_B) for _ in range(3) for dil in DILS]
    full = lambda shape: pl.BlockSpec(shape, lambda i: (0,) * len(shape))
    return pl.pallas_call(
        _inproj_kernel,
        grid=(n // tt,),
        in_specs=[pl.BlockSpec((tt, d), lambda i: (i, 0)),
                  pl.BlockSpec((None, 6, d), lambda i: (i // tiles_per_seq, 0, 0)),
                  pl.BlockSpec((tt, 1), lambda i: (i, 0)),
                  full((1, d)), full((d, IN_WIDTH)), full((1, IN_WIDTH)),
                  full((1, LANES)), full((LANES, LANES))],
        out_specs=[pl.BlockSpec((tt * r // n, w), lambda i: (i, 0)) for r, w in shapes],
        out_shape=[jax.ShapeDtypeStruct(s, BF16) for s in shapes],
        scratch_shapes=[pltpu.VMEM((N_PAIRS, tt, LANES), F32)] * 3,
        compiler_params=_params("parallel"),
        name="inproj",
    )(xf, mod, pos, g_norm, w_in_p, gcol, invf, bd)


def _attn_kernel(*refs, kv_shared, max_dist, use_prev, has_sinks, want_lse):
    refs = list(refs)
    sink_ref = refs.pop(0) if has_sinks else None
    q_ref = refs.pop(0)
    kp_ref = refs.pop(0) if use_prev else None
    kc_ref = refs.pop(0)
    vp_ref = refs.pop(0) if use_prev else None
    vc_ref = refs.pop(0)
    o_ref = refs.pop(0)
    lse_ref = refs.pop(0) if want_lse else None

    blk = pl.program_id(2)
    nq = 2 * BLOCK
    nk = 2 * BLOCK if use_prev else BLOCK
    qpos = lax.broadcasted_iota(I32, (nq, nk), 0) % BLOCK
    kpos = lax.broadcasted_iota(I32, (nq, nk), 1)
    if use_prev:
        dist = qpos + BLOCK - kpos
        valid = (dist >= 0) & (dist <= max_dist) & ((kpos >= BLOCK) | (blk > 0))
    else:
        dist = qpos - kpos
        valid = (dist >= 0) & (dist <= max_dist)
    lane = lax.broadcasted_iota(I32, (nq, LANES), 1)
    row = lax.broadcasted_iota(I32, (nq, LANES), 0)
    own_half = (lane < HEAD_DIM) == (row < BLOCK)
    left_lanes = lax.broadcasted_iota(I32, (BLOCK, LANES), 1) < HEAD_DIM
    lane8 = lax.broadcasted_iota(I32, (BLOCK, 2 * N_PAIRS), 1)
    lse_blk = jnp.zeros((BLOCK, 2 * N_PAIRS), F32)

    for p in range(N_PAIRS):
        cq = slice(p * LANES, (p + 1) * LANES)
        ck = slice(0, LANES) if kv_shared else cq
        qp = q_ref[:, cq]
        qs = jnp.concatenate([qp, qp], axis=0)
        qs = jnp.where(own_half, qs, jnp.zeros_like(qs))
        if use_prev:
            k = jnp.concatenate([kp_ref[:, ck], kc_ref[:, ck]], axis=0)
            v = jnp.concatenate([vp_ref[:, ck], vc_ref[:, ck]], axis=0)
        else:
            k, v = kc_ref[:, ck], vc_ref[:, ck]
        s = lax.dot_general(qs, k, (((1,), (1,)), ((), ())), preferred_element_type=F32)
        s = jnp.where(valid, s, NEG)
        m = jnp.max(s, axis=-1, keepdims=True)
        if has_sinks:
            rows1 = lax.broadcasted_iota(I32, (nq, 1), 0)
            sink = jnp.where(rows1 < BLOCK, sink_ref[2 * p], sink_ref[2 * p + 1])
            m = jnp.maximum(m, sink)
        e = jnp.exp(s - m)
        l = jnp.sum(e, axis=-1, keepdims=True)
        if has_sinks:
            l = l + jnp.exp(sink - m)
        o = jnp.dot(e.astype(BF16), v, preferred_element_type=F32) / l
        o_ref[:, cq] = jnp.where(left_lanes, o[:BLOCK], o[BLOCK:]).astype(BF16)
        if want_lse:
            lse = m + jnp.log(l)
            lse_blk = (lse_blk + jnp.where(lane8 == 2 * p, lse[:BLOCK], 0.0)
                       + jnp.where(lane8 == 2 * p + 1, lse[BLOCK:], 0.0))
    if want_lse:
        lse_ref[...] = lse_blk


def _attention(q, k, v, *, nbatch, seq, dil, max_dist, kv_shared, sinks=None, want_lse):
    length = seq // dil
    nblk = length // BLOCK
    use_prev = nblk > 1
    kw = k.shape[1] // dil
    view = lambda t: t.reshape(nbatch, length, t.shape[1])
    cur = lambda b, r, i: (b, i, r)
    prev = lambda b, r, i: (b, jnp.maximum(i - 1, 0), r)
    in_specs, args = [], []
    if sinks is not None:
        in_specs.append(pl.BlockSpec(memory_space=pltpu.SMEM))
        args.append(sinks)
    in_specs.append(pl.BlockSpec((None, BLOCK, MIX_B), cur))
    args.append(view(q))
    for t in (k, v):
        if use_prev:
            in_specs.append(pl.BlockSpec((None, BLOCK, kw), prev))
            args.append(view(t))
        in_specs.append(pl.BlockSpec((None, BLOCK, kw), cur))
        args.append(view(t))
    out_specs = [pl.BlockSpec((None, BLOCK, MIX_B), cur)]
    out_shape = [jax.ShapeDtypeStruct((nbatch, length, dil * MIX_B), BF16)]
    if want_lse:
        out_specs.append(pl.BlockSpec((None, None, BLOCK, N_HEADS_B), lambda b, r, i: (b, r, i, 0)))
        out_shape.append(jax.ShapeDtypeStruct((nbatch, dil, length, N_HEADS_B), F32))
    outs = pl.pallas_call(
        functools.partial(_attn_kernel, kv_shared=kv_shared, max_dist=max_dist, use_prev=use_prev,
                          has_sinks=sinks is not None, want_lse=want_lse),
        grid=(nbatch, dil, nblk),
        in_specs=in_specs, out_specs=out_specs, out_shape=out_shape,
        compiler_params=_params("parallel", "parallel", "arbitrary"),
        name=f"attn_d{dil}" + ("_swa" if kv_shared else ""),
    )(*args)
    o = outs[0].reshape(nbatch * length, dil * MIX_B)
    if not want_lse:
        return o, None
    lse = outs[1].transpose(0, 2, 1, 3).reshape(nbatch * seq, N_HEADS_B)
    return o, lse


def _expand_heads(w, width):
    head = lax.broadcasted_iota(I32, (1, width), 1) // HEAD_DIM
    out = jnp.zeros((w.shape[0], width), F32)
    for hd in range(w.shape[1]):
        out = jnp.where(head == hd, w[:, hd:hd + 1], out)
    return out


def _outproj_kernel(x_ref, mod_ref, oa_ref, ob1_ref, ob2_ref, ob3_ref, l1_ref, l2_ref, l3_ref,
                    goa_ref, gob_ref, wo_ref, gf_ref, wgs_ref, wus_ref, wds_ref,
                    h2_ref, base_ref, ob_scr):
    tt = x_ref.shape[0]
    gate_a = mod_ref[2:3, :]
    shift_m, scale_m, gate_m = mod_ref[3:4, :], mod_ref[4:5, :], mod_ref[5:6, :]

    def token_major(ref, dil):
        if dil == 1:
            return ref[...].astype(F32)
        for r in range(dil):
            for j in range(N_PAIRS):
                c = r * MIX_B + j * LANES
                ob_scr[j, pl.ds(r, tt // dil, stride=dil), :] = ref[:, c:c + LANES].astype(F32)
        return jnp.concatenate([ob_scr[j] for j in range(N_PAIRS)], axis=1)

    l1, l2, l3 = l1_ref[...], l2_ref[...], l3_ref[...]
    mx = jnp.maximum(jnp.maximum(l1, l2), l3)
    e1, e2, e3 = jnp.exp(l1 - mx), jnp.exp(l2 - mx), jnp.exp(l3 - mx)
    den = e1 + e2 + e3
    dils = [dil for _, dil in DILATED_BRANCHES]
    ob = _expand_heads(e1 / den, MIX_B) * token_major(ob1_ref, dils[0])
    ob = ob + _expand_heads(e2 / den, MIX_B) * token_major(ob2_ref, dils[1])
    ob = ob + _expand_heads(e3 / den, MIX_B) * token_major(ob3_ref, dils[2])
    ob = _rms_rows(ob) * gob_ref[...]
    oa = _rms_rows(oa_ref[...].astype(F32)) * goa_ref[...]
    y = (jnp.dot(oa.astype(BF16), wo_ref[0:MIX_A, :], preferred_element_type=F32)
         + jnp.dot(ob.astype(BF16), wo_ref[MIX_A:MIX_A + MIX_B, :], preferred_element_type=F32))
    x1 = x_ref[...] + gate_a * y
    h2 = _rms_rows(x1) * gf_ref[...] * (1.0 + scale_m) + shift_m
    h2_ref[...] = h2
    hb = h2.astype(BF16)
    act = (_silu(jnp.dot(hb, wgs_ref[...], preferred_element_type=F32))
           * jnp.dot(hb, wus_ref[...], preferred_element_type=F32))
    shared = jnp.dot(act.astype(BF16), wds_ref[...], preferred_element_type=F32)
    base_ref[...] = x1 + gate_m * shared


def _outproj(xf, mod, oa, obs, lses, goa, gob, wo_p, gf, wgs, wus, wds, seq):
    n, d = xf.shape
    tt = TT_PROJ // 2
    tiles_per_seq = seq // tt
    tile = lambda w: pl.BlockSpec((tt, w), lambda i: (i, 0))
    full = lambda shape: pl.BlockSpec(shape, lambda i: (0,) * len(shape))
    sd = wgs.shape[1]
    dilated = [pl.BlockSpec((tt // dil, dil * MIX_B), lambda i: (i, 0)) for _, dil in DILATED_BRANCHES]
    return pl.pallas_call(
        _outproj_kernel,
        grid=(n // tt,),
        in_specs=[tile(d), pl.BlockSpec((None, 6, d), lambda i: (i // tiles_per_seq, 0, 0)),
                  tile(MIX_A), *dilated,
                  tile(N_HEADS_B), tile(N_HEADS_B), tile(N_HEADS_B),
                  full((1, MIX_A)), full((1, MIX_B)), full((MIX_A + MIX_B, d)), full((1, d)),
                  full((d, sd)), full((d, sd)), full((sd, d))],
        out_specs=[tile(d), tile(d)],
        out_shape=[jax.ShapeDtypeStruct((n, d), F32), jax.ShapeDtypeStruct((n, d), F32)],
        scratch_shapes=[pltpu.VMEM((N_PAIRS, tt, LANES), F32)],
        compiler_params=_params("parallel"),
        name="outproj",
    )(xf, mod, oa, *obs, *lses, goa, gob, wo_p, gf, wgs, wus, wds)


def _router_kernel(h_ref, wrt_ref, bias_ref, e_ref, g_ref, cnt_ref):
    tt = h_ref.shape[0]
    logits = lax.dot_general(wrt_ref[...], h_ref[...].astype(BF16), (((1,), (1,)), ((), ())),
                             preferred_element_type=F32)
    scores = 1.0 / (1.0 + jnp.exp(-logits))
    biased = scores + bias_ref[...]
    ninf = -jnp.inf

    j32 = lax.broadcasted_iota(I32, (GROUP_SIZE, tt), 0).astype(F32)
    grp = []
    for g in range(N_GROUPS):
        bg = biased[g * GROUP_SIZE:(g + 1) * GROUP_SIZE, :]
        m1 = jnp.max(bg, axis=0, keepdims=True)
        i1 = jnp.min(jnp.where(bg == m1, j32, float(GROUP_SIZE)), axis=0, keepdims=True)
        m2 = jnp.max(jnp.where(j32 == i1, ninf, bg), axis=0, keepdims=True)
        grp.append(m1 + m2)
    grp = jnp.concatenate(grp, axis=0)
    g8 = lax.broadcasted_iota(I32, (N_GROUPS, tt), 0).astype(F32)
    chosen = jnp.zeros((N_GROUPS, tt), F32)
    for _ in range(TOPK_GROUPS):
        gm = jnp.max(grp, axis=0, keepdims=True)
        gi = jnp.min(jnp.where(grp == gm, g8, float(N_GROUPS)), axis=0, keepdims=True)
        hit = g8 == gi
        chosen = jnp.where(hit, 1.0, chosen)
        grp = jnp.where(hit, ninf, grp)
    masked = jnp.concatenate(
        [jnp.where(chosen[g:g + 1, :] > 0.0, biased[g * GROUP_SIZE:(g + 1) * GROUP_SIZE, :], ninf)
         for g in range(N_GROUPS)], axis=0)

    eio = lax.broadcasted_iota(I32, (N_EXPERTS, tt), 0).astype(F32)
    picked = jnp.zeros((N_EXPERTS, tt), F32)
    es, gs = [], []
    for _ in range(TOP_K):
        m = jnp.max(masked, axis=0, keepdims=True)
        idx = jnp.min(jnp.where(masked == m, eio, float(N_EXPERTS)), axis=0, keepdims=True)
        hit = eio == idx
        gs.append(jnp.sum(jnp.where(hit, scores, 0.0), axis=0, keepdims=True))
        es.append(idx)
        picked = jnp.where(hit, 1.0, picked)
        masked = jnp.where(hit, ninf, masked)
    gates = jnp.concatenate(gs, axis=0)
    e_ref[...] = jnp.concatenate(es, axis=0).astype(I32)
    g_ref[...] = gates / jnp.sum(gates, axis=0, keepdims=True) * ROUTED_SCALE

    @pl.when(pl.program_id(0) == 0)
    def _():
        cnt_ref[...] = jnp.zeros_like(cnt_ref)
    cnt_ref[...] += jnp.sum(picked, axis=1, keepdims=True)


def _router(h2, wrt, bias_col):
    n, d = h2.shape
    tt = TT_ROUTE
    return pl.pallas_call(
        _router_kernel,
        grid=(n // tt,),
        in_specs=[pl.BlockSpec((tt, d), lambda i: (i, 0)),
                  pl.BlockSpec((N_EXPERTS, d), lambda i: (0, 0)),
                  pl.BlockSpec((N_EXPERTS, 1), lambda i: (0, 0))],
        out_specs=[pl.BlockSpec((TOP_K, tt), lambda i: (0, i)),
                   pl.BlockSpec((TOP_K, tt), lambda i: (0, i)),
                   pl.BlockSpec((N_EXPERTS, 1), lambda i: (0, 0))],
        out_shape=[jax.ShapeDtypeStruct((TOP_K, n), I32),
                   jax.ShapeDtypeStruct((TOP_K, n), F32),
                   jax.ShapeDtypeStruct((N_EXPERTS, 1), F32)],
        compiler_params=_params("arbitrary"),
        name="router",
    )(h2, wrt, bias_col)


def _rank_kernel(e_ref, pstart_ref, tri_ref, dest_ref, carry_ref):
    tt = e_ref.shape[1]

    @pl.when(pl.program_id(0) == 0)
    def _():
        carry_ref[...] = pstart_ref[...]

    e = e_ref[...]
    eio = lax.broadcasted_iota(I32, (N_EXPERTS, tt), 0)
    mask = jnp.zeros((N_EXPERTS, tt), F32)
    for k in range(TOP_K):
        mask = jnp.where(eio == e[k:k + 1, :], 1.0, mask)
    incl = jnp.dot(mask.astype(BF16), tri_ref[...], preferred_element_type=F32)
    pos = incl - mask + carry_ref[...]
    dest = [jnp.sum(jnp.where(eio == e[k:k + 1, :], pos, 0.0), axis=0, keepdims=True)
            for k in range(TOP_K)]
    dest_ref[...] = jnp.concatenate(dest, axis=0).astype(I32)
    carry_ref[...] += incl[:, tt - 1:tt]


def _rank(top_e_t, pstart_col, tri):
    n = top_e_t.shape[1]
    tt = TT_ROUTE
    return pl.pallas_call(
        _rank_kernel,
        grid=(n // tt,),
        in_specs=[pl.BlockSpec((TOP_K, tt), lambda i: (0, i)),
                  pl.BlockSpec((N_EXPERTS, 1), lambda i: (0, 0)),
                  pl.BlockSpec((tt, tt), lambda i: (0, 0))],
        out_specs=pl.BlockSpec((TOP_K, tt), lambda i: (0, i)),
        out_shape=jax.ShapeDtypeStruct((TOP_K, n), I32),
        scratch_shapes=[pltpu.VMEM((N_EXPERTS, 1), F32)],
        compiler_params=_params("arbitrary"),
        name="rank",
    )(top_e_t, pstart_col, tri)


def _row_copy(src_ref, src_row, dst_ref, dst_row, sem):
    return pltpu.make_async_copy(src_ref.at[pl.ds(src_row, 1)], dst_ref.at[pl.ds(dst_row, 1)], sem)


SC_CORES, SC_SUBCORES, SC_LANES = 2, 16, 16
SC_WORKERS = SC_CORES * SC_SUBCORES
INVERT_CHUNK = 8192


def _invert(dest_flat, n_tokens, rows):
    per = rows // SC_WORKERS
    n_assign = dest_flat.shape[0]
    assert rows % (SC_WORKERS * SC_LANES) == 0 and n_assign % INVERT_CHUNK == 0
    assert n_tokens & (n_tokens - 1) == 0

    @functools.partial(
        pl.kernel, mesh=plsc.VectorSubcoreMesh(core_axis_name="c", subcore_axis_name="s"),
        out_type=jax.ShapeDtypeStruct((rows,), I32),
        scratch_types=[pltpu.VMEM((INVERT_CHUNK,), I32), pltpu.VMEM((per,), I32)],
        compiler_params=pltpu.CompilerParams(needs_layout_passes=False))
    def invert(dest_hbm, out_hbm, staged, local):
        base = (lax.axis_index("s") * SC_CORES + lax.axis_index("c")) * per

        @pl.loop(0, per, step=SC_LANES)
        def _(i):
            local[pl.ds(i, SC_LANES)] = jnp.zeros((SC_LANES,), I32)

        lane = lax.iota(I32, SC_LANES)

        @pl.loop(0, n_assign // INVERT_CHUNK)
        def _(c):
            pltpu.sync_copy(dest_hbm.at[pl.ds(c * INVERT_CHUNK, INVERT_CHUNK)], staged)

            @pl.loop(0, INVERT_CHUNK, step=SC_LANES)
            def _(i):
                rel = staged[pl.ds(i, SC_LANES)] - base
                mine = (rel >= 0) & (rel < per)
                tok = (c * INVERT_CHUNK + i + lane) & (n_tokens - 1)
                plsc.store_scatter(local, [jnp.where(mine, rel, 0)], tok, mask=mine)

        pltpu.sync_copy(local, out_hbm.at[pl.ds(base, per)])

    return invert(dest_flat)


def _experts_kernel(be_ref, nv_ref, tok_hbm, h_ref, wg_ref, wu_ref, wd_ref, ys_ref,
                    xring, wgb, wub, wdb, tok_s, sem, tok_sem):
    j = pl.program_id(0)
    nv = nv_ref[0]
    depth, bm = xring.shape[0], xring.shape[1]
    ahead = depth - 1
    chunk = tok_s.shape[1]
    cb = chunk // bm
    n_chunks = tok_hbm.shape[0] // chunk
    n_blocks = n_chunks * cb

    def tok_copy(c):
        return pltpu.make_async_copy(tok_hbm.at[pl.ds(pl.multiple_of(c * chunk, chunk), chunk)],
                                     tok_s.at[c % 2], tok_sem.at[c % 2])

    def gather(block):
        ring = block % depth
        block = jnp.minimum(block, n_blocks - 1)
        slot, off = (block // cb) % 2, (block % cb) * bm

        def issue(i, carry):
            _row_copy(h_ref, tok_s[slot, off + i], xring.at[ring], i, sem.at[ring]).start()
            return carry

        lax.fori_loop(0, bm, issue, 0, unroll=8)

    def wait_rows(ring):
        pltpu.make_async_copy(h_ref.at[pl.ds(0, bm)], xring.at[ring], sem.at[ring]).wait()

    @pl.when(j == 0)
    def _():
        tok_copy(0).start()
        tok_copy(0).wait()
        tok_copy(1).start()
        for b in range(ahead):
            gather(b)

    first = j + ahead
    c_need = first // cb

    @pl.when(jnp.logical_and(jnp.logical_and(j > 0, first % cb == 0), c_need < n_chunks))
    def _():
        tok_copy(c_need).wait()

        @pl.when(c_need + 1 < n_chunks)
        def _():
            tok_copy(c_need + 1).start()

    jb = jnp.minimum(j, n_blocks - 1)
    fresh = jnp.logical_or(j == 0, be_ref[jb] != be_ref[jnp.maximum(jb - 1, 0)])

    @pl.when(jnp.logical_and(j < nv, fresh))
    def _():
        wgb[...] = wg_ref[...].astype(BF16)
        wub[...] = wu_ref[...].astype(BF16)
        wdb[...] = wd_ref[...].astype(BF16)

    @pl.when(j < nv)
    def _():
        wait_rows(j % depth)
        gather(j + ahead)
        xb = xring[j % depth].astype(BF16)
        act = (_silu(jnp.dot(xb, wgb[...], preferred_element_type=F32))
               * jnp.dot(xb, wub[...], preferred_element_type=F32))
        ys_ref[...] = jnp.dot(act.astype(BF16), wdb[...], preferred_element_type=F32)

    @pl.when(jnp.logical_and(j >= nv, j < nv + ahead))
    def _():
        wait_rows(j % depth)


def _experts(block_e, n_valid, row_tok, h2, wg, wu, wd):
    rows = row_tok.shape[0]
    d = h2.shape[1]
    bm = EXPERT_ROWS
    n_blocks = rows // bm
    assert n_blocks % TOK_CHUNK_BLOCKS == 0 and n_blocks // TOK_CHUNK_BLOCKS >= 2
    assert GATHER_DEPTH - 1 < TOK_CHUNK_BLOCKS
    f = wg.shape[2]
    row_blk = lambda j, be, nv: (jnp.minimum(j, nv[0] - 1), 0)
    w_blk = lambda j, be, nv: (be[jnp.minimum(j, n_blocks - 1)], 0, 0)
    return pl.pallas_call(
        _experts_kernel,
        grid_spec=pltpu.PrefetchScalarGridSpec(
            num_scalar_prefetch=2,
            grid=(n_blocks + GATHER_DEPTH,),
            in_specs=[pl.BlockSpec(memory_space=pl.ANY),
                      pl.BlockSpec(memory_space=pl.ANY),
                      pl.BlockSpec((None, d, f), w_blk),
                      pl.BlockSpec((None, d, f), w_blk),
                      pl.BlockSpec((None, f, d), w_blk)],
            out_specs=pl.BlockSpec((bm, d), row_blk),
            scratch_shapes=[pltpu.VMEM((GATHER_DEPTH, bm, d), F32),
                            pltpu.VMEM((d, f), BF16), pltpu.VMEM((d, f), BF16),
                            pltpu.VMEM((f, d), BF16),
                            pltpu.SMEM((2, TOK_CHUNK_BLOCKS * bm), I32),
                            pltpu.SemaphoreType.DMA((GATHER_DEPTH,)), pltpu.SemaphoreType.DMA((2,))]),
        out_shape=jax.ShapeDtypeStruct((rows, d), F32),
        compiler_params=_params("arbitrary"),
        name="experts",
    )(block_e, n_valid, row_tok, h2, wg, wu, wd)


def _combine_kernel(dest_ref, gates_ref, base_ref, mod_ref, ys_ref, out_ref, buf, sem):
    tt = base_ref.shape[0]

    def issue(t, carry):
        for k in range(TOP_K):
            _row_copy(ys_ref, dest_ref[k, t], buf.at[k], t, sem).start()
        return carry

    def drain(t, carry):
        for k in range(TOP_K):
            _row_copy(ys_ref, dest_ref[k, t], buf.at[k], t, sem).wait()
        return carry

    lax.fori_loop(0, tt, issue, 0)
    lax.fori_loop(0, tt, drain, 0)
    gates = gates_ref[...]
    routed = gates[:, 0:1] * buf[0]
    for k in range(1, TOP_K):
        routed = routed + gates[:, k:k + 1] * buf[k]
    out_ref[...] = base_ref[...] + mod_ref[5:6, :] * routed


def _combine(dest_t, gates, base, mod, ys, seq):
    n, d = base.shape
    tt = TT_COMBINE
    tiles_per_seq = seq // tt
    return pl.pallas_call(
        _combine_kernel,
        grid=(n // tt,),
        in_specs=[pl.BlockSpec((TOP_K, tt), lambda i: (0, i), memory_space=pltpu.SMEM),
                  pl.BlockSpec((tt, TOP_K), lambda i: (i, 0)),
                  pl.BlockSpec((tt, d), lambda i: (i, 0)),
                  pl.BlockSpec((None, 6, d), lambda i: (i // tiles_per_seq, 0, 0)),
                  pl.BlockSpec(memory_space=pl.ANY)],
        out_specs=pl.BlockSpec((tt, d), lambda i: (i, 0)),
        out_shape=jax.ShapeDtypeStruct((n, d), F32),
        scratch_shapes=[pltpu.VMEM((TOP_K, tt, d), F32), pltpu.SemaphoreType.DMA],
        compiler_params=_params("arbitrary"),
        name="combine",
    )(dest_t, gates, base, mod, ys)


def _layer(x, mod, pos, rope, p):
    nbatch, seq, d = x.shape
    n = nbatch * seq
    xf = x.reshape(n, d)
    invf, bd = rope

    perm = np.concatenate([np.arange(h * HEAD_DIM, (h + 1) * HEAD_DIM) for h in PAIR_ORDER_A])
    w_in = p["w_in"]
    w_in_p = jnp.concatenate([w_in[:, :MIX_A][:, perm], w_in[:, MIX_A:]], axis=1).astype(BF16)
    ones = lambda w: jnp.ones((w,), F32)
    qscale = HEAD_DIM ** -0.5
    gcol = jnp.concatenate([jnp.tile(p["g_q_a"], N_HEADS_A) * qscale, jnp.tile(p["g_k_a"], N_KV_A),
                            ones(KV_A), jnp.tile(p["g_q_b"], N_HEADS_B) * qscale,
                            jnp.tile(p["g_k_b"], N_HEADS_B), ones(MIX_B)]).reshape(1, IN_WIDTH)
    proj = _inproj(xf, mod, pos, p["g_norm_mix"].reshape(1, d), w_in_p, gcol, invf, bd, seq)
    qa, ka, va = proj[:3]
    qkv_b = {1: proj[3:6]}
    for j, dil in enumerate(DILS):
        qkv_b[dil] = [proj[6 + t * len(DILS) + j] for t in range(3)]

    sinks_p = p["sinks_a"][np.array(PAIR_ORDER_A)]
    oa, _ = _attention(qa, ka, va, nbatch=nbatch, seq=seq, dil=1, max_dist=WINDOW_A - 1,
                       kv_shared=True, sinks=sinks_p, want_lse=False)
    obs, lses = [], []
    for window, dil in DILATED_BRANCHES:
        o, lse = _attention(*qkv_b[dil], nbatch=nbatch, seq=seq, dil=dil, max_dist=window // dil,
                            kv_shared=False, want_lse=True)
        obs.append(o)
        lses.append(lse)

    goa = p["g_out_a"][perm].reshape(1, MIX_A)
    w_out = p["w_out"]
    wo_p = jnp.concatenate([w_out[:MIX_A][perm], w_out[MIX_A:]], axis=0).astype(BF16)
    h2, base = _outproj(xf, mod, oa, obs, lses, goa, p["g_out_b"].reshape(1, MIX_B), wo_p,
                        p["g_norm_ffn"].reshape(1, d), p["w_gate_s"].astype(BF16),
                        p["w_up_s"].astype(BF16), p["w_down_s"].astype(BF16), seq)

    top_e_t, gates_t, counts = _router(h2, p["w_router"].T.astype(BF16),
                                       p["router_bias"].reshape(N_EXPERTS, 1))
    bm = EXPERT_ROWS
    counts = counts.reshape(N_EXPERTS).astype(I32)
    padded = (counts + bm - 1) // bm * bm
    pends = jnp.cumsum(padded)
    pstarts = pends - padded
    rows = n * TOP_K + N_EXPERTS * bm
    n_blocks = rows // bm
    n_valid = (pends[-1] // bm).astype(I32)
    blk = jnp.minimum(jnp.arange(n_blocks, dtype=I32), n_valid - 1)
    block_e = jnp.sum((pends[None, :] <= (blk * bm)[:, None]).astype(I32), axis=1)
    block_e = jnp.minimum(block_e, N_EXPERTS - 1)

    tri = (np.arange(TT_ROUTE)[:, None] <= np.arange(TT_ROUTE)[None, :])
    dest_t = _rank(top_e_t, pstarts.astype(F32).reshape(N_EXPERTS, 1), jnp.asarray(tri, BF16))
    n_valid = n_valid.reshape(1)
    row_tok = _invert(dest_t.reshape(n * TOP_K), n, rows)
    ys = _experts(block_e, n_valid, row_tok, h2, p["w_gate_e"], p["w_up_e"], p["w_down_e"])
    out = _combine(dest_t, gates_t.T, base, mod, ys, seq)
    return out.reshape(nbatch, seq, d)


def kernel(x, c, positions, w_ada, b_ada, g_norm_mix, w_in, g_q_a, g_k_a, sinks_a, g_q_b, g_k_b,
           g_out_a, g_out_b, w_out, g_norm_ffn, w_router, router_bias, w_gate_e, w_up_e, w_down_e,
           w_gate_s, w_up_s, w_down_s):
    nbatch, seq, d = x.shape
    depth = w_ada.shape[0]
    params = dict(g_norm_mix=g_norm_mix, w_in=w_in, g_q_a=g_q_a, g_k_a=g_k_a, sinks_a=sinks_a,
                  g_q_b=g_q_b, g_k_b=g_k_b, g_out_a=g_out_a, g_out_b=g_out_b, w_out=w_out,
                  g_norm_ffn=g_norm_ffn, w_router=w_router, router_bias=router_bias,
                  w_gate_e=w_gate_e, w_up_e=w_up_e, w_down_e=w_down_e, w_gate_s=w_gate_s,
                  w_up_s=w_up_s, w_down_s=w_down_s)
    j = np.arange(LANES) % HEAD_DIM
    inv = ROPE_THETA ** (-jnp.arange(0, ROT_DIM, 2, dtype=F32) / ROT_DIM)
    invf = jnp.where(j < ROT_DIM, inv[j % (ROT_DIM // 2)], 0.0).astype(F32).reshape(1, LANES)
    bd = jnp.asarray((np.arange(LANES)[:, None] // HEAD_DIM) == (np.arange(LANES)[None, :] // HEAD_DIM),
                     BF16)
    pos = positions.reshape(nbatch * seq, 1).astype(I32)
    for l in range(depth):
        mod = _adaln(c.astype(F32), w_ada[l], b_ada[l]).reshape(nbatch, 6, d)
        x = _layer(x, mod, pos, (invf, bd), {k: v[l] for k, v in params.items()})
    return x
```

```python
import functools

import numpy as np
import jax
import jax.numpy as jnp
from jax import lax
from jax.experimental import pallas as pl
from jax.experimental.pallas import tpu as pltpu
from jax.experimental.pallas import tpu_sc as plsc

F32 = jnp.float32
BF16 = jnp.bfloat16
I32 = jnp.int32

HEAD_DIM = 64
N_HEADS_A = 8
N_KV_A = 2
WINDOW_A = 128
N_HEADS_B = 8
DILATED_BRANCHES = ((128, 1), (512, 4), (2048, 16))
DILS = tuple(dil for _, dil in DILATED_BRANCHES if dil > 1)
BLOCK = 128
ROT_DIM = HEAD_DIM // 4
ROPE_THETA = 500000.0
MIX_A = N_HEADS_A * HEAD_DIM
KV_A = N_KV_A * HEAD_DIM
MIX_B = N_HEADS_B * HEAD_DIM
N_EXPERTS = 256
TOP_K = 8
N_GROUPS = 8
TOPK_GROUPS = 4
GROUP_SIZE = N_EXPERTS // N_GROUPS
ROUTED_SCALE = 2.5
EPS = 1e-6

LANES = 128
HEADS_PER_VREG = LANES // HEAD_DIM
N_PAIRS = MIX_A // LANES
NEG = -1e30
VMEM_LIMIT = 48 * 1024 * 1024

TT_PROJ = 512
TT_ROUTE = 256
TT_DISPATCH = 256
TT_COMBINE = 128
EXPERT_ROWS = 256
TOK_CHUNK_BLOCKS = 32
GATHER_DEPTH = 4
SUBLANES = 8

PAIR_ORDER_A = tuple(h for p in range(N_PAIRS) for h in (p, p + N_HEADS_A // N_KV_A))


def _params(*sem):
    return pltpu.CompilerParams(dimension_semantics=sem, vmem_limit_bytes=VMEM_LIMIT)


def _silu(t):
    return t / (1.0 + jnp.exp(-t))


def _rms_rows(t):
    return t * lax.rsqrt(jnp.mean(t * t, axis=-1, keepdims=True) + EPS)


def _ada_kernel(c_ref, w_ref, b_ref, o_ref):
    cond = _silu(c_ref[...])
    o_ref[...] = jnp.dot(cond.astype(BF16), w_ref[...].astype(BF16),
                         preferred_element_type=F32) + b_ref[...]


def _adaln(c, w_ada, b_ada):
    nb, d = c.shape
    width = w_ada.shape[1]
    tn = 1024
    return pl.pallas_call(
        _ada_kernel,
        grid=(width // tn,),
        in_specs=[pl.BlockSpec((nb, d), lambda j: (0, 0)),
                  pl.BlockSpec((d, tn), lambda j: (0, j)),
                  pl.BlockSpec((1, tn), lambda j: (0, j))],
        out_specs=pl.BlockSpec((nb, tn), lambda j: (0, j)),
        out_shape=jax.ShapeDtypeStruct((nb, width), F32),
        compiler_params=_params("arbitrary"),
        name="adaln",
    )(c, w_ada, b_ada.reshape(1, width))


COL_QA, COL_KA, COL_VA = 0, MIX_A, MIX_A + KV_A
COL_QB = MIX_A + 2 * KV_A
COL_KB, COL_VB = COL_QB + MIX_B, COL_QB + 2 * MIX_B
IN_WIDTH = COL_VB + MIX_B


def _inproj_kernel(x_ref, mod_ref, pos_ref, gn_ref, w_ref, gcol_ref, invf_ref, bd_ref,
                   qa_ref, ka_ref, va_ref, qb_ref, kb_ref, vb_ref, *rest):
    n_dil = len(DILS)
    dil_refs = [rest[i * n_dil:(i + 1) * n_dil] for i in range(3)]
    qb_scr, kb_scr, vb_scr = rest[3 * n_dil:]
    tt = x_ref.shape[0]
    shift, scale = mod_ref[0:1, :], mod_ref[1:2, :]
    h = _rms_rows(x_ref[...]) * gn_ref[...] * (1.0 + scale) + shift
    proj = jnp.dot(h.astype(BF16), w_ref[...], preferred_element_type=F32)

    ang = pos_ref[...].astype(F32) * invf_ref[...]
    cs, sn = jnp.cos(ang), jnp.sin(ang)
    lane = lax.broadcasted_iota(I32, (1, LANES), 1) % HEAD_DIM
    s_lo = jnp.where(lane < ROT_DIM // 2, -sn, 0.0)
    s_hi = jnp.where((lane >= ROT_DIM // 2) & (lane < ROT_DIM), sn, 0.0)
    bd = bd_ref[...]

    def norm_rope(col0, width, out_ref, scr=None):
        for j in range(width // LANES):
            c = col0 + j * LANES
            t = proj[:, c:c + LANES]
            sq = t * t
            hi = sq.astype(BF16)
            lo = (sq - hi.astype(F32)).astype(BF16)
            ss = (jnp.dot(hi, bd, preferred_element_type=F32)
                  + jnp.dot(lo, bd, preferred_element_type=F32))
            t = t * lax.rsqrt(ss * (1.0 / HEAD_DIM) + EPS) * gcol_ref[:, c:c + LANES]
            t = (t * cs + pltpu.roll(t, LANES - ROT_DIM // 2, 1) * s_lo
                 + pltpu.roll(t, ROT_DIM // 2, 1) * s_hi)
            out_ref[:, j * LANES:(j + 1) * LANES] = t.astype(BF16)
            if scr is not None:
                scr[j] = t

    norm_rope(COL_QA, MIX_A, qa_ref)
    norm_rope(COL_KA, KV_A, ka_ref)
    norm_rope(COL_QB, MIX_B, qb_ref, qb_scr)
    norm_rope(COL_KB, MIX_B, kb_ref, kb_scr)
    va_ref[...] = proj[:, COL_VA:COL_VA + KV_A].astype(BF16)
    vb_ref[...] = proj[:, COL_VB:COL_VB + MIX_B].astype(BF16)
    for j in range(N_PAIRS):
        vb_scr[j] = proj[:, COL_VB + j * LANES:COL_VB + (j + 1) * LANES]
    for scr, outs in zip((qb_scr, kb_scr, vb_scr), dil_refs):
        for dil, out in zip(DILS, outs):
            for r in range(dil):
                for j in range(N_PAIRS):
                    c = r * MIX_B + j * LANES
                    out[:, c:c + LANES] = scr[j, pl.ds(r, tt // dil, stride=dil), :].astype(BF16)


def _inproj(xf, mod, pos, g_norm, w_in_p, gcol, invf, bd, seq):
    n, d = xf.shape
    tt = TT_PROJ
    tiles_per_seq = seq // tt
    shapes = [(n, w) for w in (MIX_A, KV_A, KV_A, MIX_B, MIX_B, MIX_B)]
    shapes += [(n // dil, dil * MIX_B) for _ in range(3) for dil in DILS]
    full = lambda shape: pl.BlockSpec(shape, lambda i: (0,) * len(shape))
    return pl.pallas_call(
        _inproj_kernel,
        grid=(n // tt,),
        in_specs=[pl.BlockSpec((tt, d), lambda i: (i, 0)),
                  pl.BlockSpec((None, 6, d), lambda i: (i // tiles_per_seq, 0, 0)),
                  pl.BlockSpec((tt, 1), lambda i: (i, 0)),
                  full((1, d)), full((d, IN_WIDTH)), full((1, IN_WIDTH)),
                  full((1, LANES)), full((LANES, LANES))],
        out_specs=[pl.BlockSpec((tt * r // n, w), lambda i: (i, 0)) for r, w in shapes],
        out_shape=[jax.ShapeDtypeStruct(s, BF16) for s in shapes],
        scratch_shapes=[pltpu.VMEM((N_PAIRS, tt, LANES), F32)] * 3,
        compiler_params=_params("parallel"),
        name="inproj",
    )(xf, mod, pos, g_norm, w_in_p, gcol, invf, bd)


def _attn_kernel(*refs, kv_shared, max_dist, use_prev, has_sinks, want_lse):
    refs = list(refs)
    sink_ref = refs.pop(0) if has_sinks else None
    q_ref = refs.pop(0)
    kp_ref = refs.pop(0) if use_prev else None
    kc_ref = refs.pop(0)
    vp_ref = refs.pop(0) if use_prev else None
    vc_ref = refs.pop(0)
    o_ref = refs.pop(0)
    lse_ref = refs.pop(0) if want_lse else None

    blk = pl.program_id(2)
    nq = 2 * BLOCK
    nk = 2 * BLOCK if use_prev else BLOCK
    qpos = lax.broadcasted_iota(I32, (nq, nk), 0) % BLOCK
    kpos = lax.broadcasted_iota(I32, (nq, nk), 1)
    if use_prev:
        dist = qpos + BLOCK - kpos
        valid = (dist >= 0) & (dist <= max_dist) & ((kpos >= BLOCK) | (blk > 0))
    else:
        dist = qpos - kpos
        valid = (dist >= 0) & (dist <= max_dist)
    lane = lax.broadcasted_iota(I32, (nq, LANES), 1)
    row = lax.broadcasted_iota(I32, (nq, LANES), 0)
    own_half = (lane < HEAD_DIM) == (row < BLOCK)
    left_lanes = lax.broadcasted_iota(I32, (BLOCK, LANES), 1) < HEAD_DIM
    lane8 = lax.broadcasted_iota(I32, (BLOCK, 2 * N_PAIRS), 1)
    lse_blk = jnp.zeros((BLOCK, 2 * N_PAIRS), F32)

    for p in range(N_PAIRS):
        cq = slice(p * LANES, (p + 1) * LANES)
        ck = slice(0, LANES) if kv_shared else cq
        qp = q_ref[:, cq]
        qs = jnp.concatenate([qp, qp], axis=0)
        qs = jnp.where(own_half, qs, jnp.zeros_like(qs))
        if use_prev:
            k = jnp.concatenate([kp_ref[:, ck], kc_ref[:, ck]], axis=0)
            v = jnp.concatenate([vp_ref[:, ck], vc_ref[:, ck]], axis=0)
        else:
            k, v = kc_ref[:, ck], vc_ref[:, ck]
        s = lax.dot_general(qs, k, (((1,), (1,)), ((), ())), preferred_element_type=F32)
        s = jnp.where(valid, s, NEG)
        m = jnp.max(s, axis=-1, keepdims=True)
        if has_sinks:
            rows1 = lax.broadcasted_iota(I32, (nq, 1), 0)
            sink = jnp.where(rows1 < BLOCK, sink_ref[2 * p], sink_ref[2 * p + 1])
            m = jnp.maximum(m, sink)
        e = jnp.exp(s - m)
        l = jnp.sum(e, axis=-1, keepdims=True)
        if has_sinks:
            l = l + jnp.exp(sink - m)
        o = jnp.dot(e.astype(BF16), v, preferred_element_type=F32) / l
        o_ref[:, cq] = jnp.where(left_lanes, o[:BLOCK], o[BLOCK:]).astype(BF16)
        if want_lse:
            lse = m + jnp.log(l)
            lse_blk = (lse_blk + jnp.where(lane8 == 2 * p, lse[:BLOCK], 0.0)
                       + jnp.where(lane8 == 2 * p + 1, lse[BLOCK:], 0.0))
    if want_lse:
        lse_ref[...] = lse_blk


def _attention(q, k, v, *, nbatch, seq, dil, max_dist, kv_shared, sinks=None, want_lse):
    length = seq // dil
    nblk = length // BLOCK
    use_prev = nblk > 1
    kw = k.shape[1] // dil
    view = lambda t: t.reshape(nbatch, length, t.shape[1])
    cur = lambda b, r, i: (b, i, r)
    prev = lambda b, r, i: (b, jnp.maximum(i - 1, 0), r)
    in_specs, args = [], []
    if sinks is not None:
        in_specs.append(pl.BlockSpec(memory_space=pltpu.SMEM))
        args.append(sinks)
    in_specs.append(pl.BlockSpec((None, BLOCK, MIX_B), cur))
    args.append(view(q))
    for t in (k, v):
        if use_prev:
            in_specs.append(pl.BlockSpec((None, BLOCK, kw), prev))
            args.append(view(t))
        in_specs.append(pl.BlockSpec((None, BLOCK, kw), cur))
        args.append(view(t))
    out_specs = [pl.BlockSpec((None, BLOCK, MIX_B), cur)]
    out_shape = [jax.ShapeDtypeStruct((nbatch, length, dil * MIX_B), BF16)]
    if want_lse:
        out_specs.append(pl.BlockSpec((None, None, BLOCK, N_HEADS_B), lambda b, r, i: (b, r, i, 0)))
        out_shape.append(jax.ShapeDtypeStruct((nbatch, dil, length, N_HEADS_B), F32))
    outs = pl.pallas_call(
        functools.partial(_attn_kernel, kv_shared=kv_shared, max_dist=max_dist, use_prev=use_prev,
                          has_sinks=sinks is not None, want_lse=want_lse),
        grid=(nbatch, dil, nblk),
        in_specs=in_specs, out_specs=out_specs, out_shape=out_shape,
        compiler_params=_params("parallel", "parallel", "arbitrary"),
        name=f"attn_d{dil}" + ("_swa" if kv_shared else ""),
    )(*args)
    o = outs[0].reshape(nbatch * length, dil * MIX_B)
    if not want_lse:
        return o, None
    lse = outs[1].transpose(0, 2, 1, 3).reshape(nbatch * seq, N_HEADS_B)
    return o, lse


def _expand_heads(w, width):
    head = lax.broadcasted_iota(I32, (1, width), 1) // HEAD_DIM
    out = jnp.zeros((w.shape[0], width), F32)
    for hd in range(w.shape[1]):
        out = jnp.where(head == hd, w[:, hd:hd + 1], out)
    return out


def _outproj_kernel(x_ref, mod_ref, oa_ref, ob1_ref, ob2_ref, ob3_ref, l1_ref, l2_ref, l3_ref,
                    goa_ref, gob_ref, wo_ref, gf_ref, wgs_ref, wus_ref, wds_ref,
                    h2_ref, base_ref, ob_scr):
    tt = x_ref.shape[0]
    gate_a = mod_ref[2:3, :]
    shift_m, scale_m, gate_m = mod_ref[3:4, :], mod_ref[4:5, :], mod_ref[5:6, :]

    def token_major(ref, dil):
        if dil == 1:
            return ref[...].astype(F32)
        for r in range(dil):
            for j in range(N_PAIRS):
                c = r * MIX_B + j * LANES
                ob_scr[j, pl.ds(r, tt // dil, stride=dil), :] = ref[:, c:c + LANES].astype(F32)
        return jnp.concatenate([ob_scr[j] for j in range(N_PAIRS)], axis=1)

    l1, l2, l3 = l1_ref[...], l2_ref[...], l3_ref[...]
    mx = jnp.maximum(jnp.maximum(l1, l2), l3)
    e1, e2, e3 = jnp.exp(l1 - mx), jnp.exp(l2 - mx), jnp.exp(l3 - mx)
    den = e1 + e2 + e3
    dils = [dil for _, dil in DILATED_BRANCHES]
    ob = _expand_heads(e1 / den, MIX_B) * token_major(ob1_ref, dils[0])
    ob = ob + _expand_heads(e2 / den, MIX_B) * token_major(ob2_ref, dils[1])
    ob = ob + _expand_heads(e3 / den, MIX_B) * token_major(ob3_ref, dils[2])
    ob = _rms_rows(ob) * gob_ref[...]
    oa = _rms_rows(oa_ref[...].astype(F32)) * goa_ref[...]
    y = (jnp.dot(oa.astype(BF16), wo_ref[0:MIX_A, :], preferred_element_type=F32)
         + jnp.dot(ob.astype(BF16), wo_ref[MIX_A:MIX_A + MIX_B, :], preferred_element_type=F32))
    x1 = x_ref[...] + gate_a * y
    h2 = _rms_rows(x1) * gf_ref[...] * (1.0 + scale_m) + shift_m
    h2_ref[...] = h2
    hb = h2.astype(BF16)
    act = (_silu(jnp.dot(hb, wgs_ref[...], preferred_element_type=F32))
           * jnp.dot(hb, wus_ref[...], preferred_element_type=F32))
    shared = jnp.dot(act.astype(BF16), wds_ref[...], preferred_element_type=F32)
    base_ref[...] = x1 + gate_m * shared


def _outproj(xf, mod, oa, obs, lses, goa, gob, wo_p, gf, wgs, wus, wds, seq):
    n, d = xf.shape
    tt = TT_PROJ // 2
    tiles_per_seq = seq // tt
    tile = lambda w: pl.BlockSpec((tt, w), lambda i: (i, 0))
    full = lambda shape: pl.BlockSpec(shape, lambda i: (0,) * len(shape))
    sd = wgs.shape[1]
    dilated = [pl.BlockSpec((tt // dil, dil * MIX_B), lambda i: (i, 0)) for _, dil in DILATED_BRANCHES]
    return pl.pallas_call(
        _outproj_kernel,
        grid=(n // tt,),
        in_specs=[tile(d), pl.BlockSpec((None, 6, d), lambda i: (i // tiles_per_seq, 0, 0)),
                  tile(MIX_A), *dilated,
                  tile(N_HEADS_B), tile(N_HEADS_B), tile(N_HEADS_B),
                  full((1, MIX_A)), full((1, MIX_B)), full((MIX_A + MIX_B, d)), full((1, d)),
                  full((d, sd)), full((d, sd)), full((sd, d))],
        out_specs=[tile(d), tile(d)],
        out_shape=[jax.ShapeDtypeStruct((n, d), F32), jax.ShapeDtypeStruct((n, d), F32)],
        scratch_shapes=[pltpu.VMEM((N_PAIRS, tt, LANES), F32)],
        compiler_params=_params("parallel"),
        name="outproj",
    )(xf, mod, oa, *obs, *lses, goa, gob, wo_p, gf, wgs, wus, wds)


def _router_kernel(h_ref, wrt_ref, bias_ref, e_ref, g_ref, cnt_ref):
    tt = h_ref.shape[0]
    logits = lax.dot_general(wrt_ref[...], h_ref[...].astype(BF16), (((1,), (1,)), ((), ())),
                             preferred_element_type=F32)
    scores = 1.0 / (1.0 + jnp.exp(-logits))
    biased = scores + bias_ref[...]
    ninf = -jnp.inf

    j32 = lax.broadcasted_iota(I32, (GROUP_SIZE, tt), 0).astype(F32)
    grp = []
    for g in range(N_GROUPS):
        bg = biased[g * GROUP_SIZE:(g + 1) * GROUP_SIZE, :]
        m1 = jnp.max(bg, axis=0, keepdims=True)
        i1 = jnp.min(jnp.where(bg == m1, j32, float(GROUP_SIZE)), axis=0, keepdims=True)
        m2 = jnp.max(jnp.where(j32 == i1, ninf, bg), axis=0, keepdims=True)
        grp.append(m1 + m2)
    grp = jnp.concatenate(grp, axis=0)
    g8 = lax.broadcasted_iota(I32, (N_GROUPS, tt), 0).astype(F32)
    chosen = jnp.zeros((N_GROUPS, tt), F32)
    for _ in range(TOPK_GROUPS):
        gm = jnp.max(grp, axis=0, keepdims=True)
        gi = jnp.min(jnp.where(grp == gm, g8, float(N_GROUPS)), axis=0, keepdims=True)
        hit = g8 == gi
        chosen = jnp.where(hit, 1.0, chosen)
        grp = jnp.where(hit, ninf, grp)
    masked = jnp.concatenate(
        [jnp.where(chosen[g:g + 1, :] > 0.0, biased[g * GROUP_SIZE:(g + 1) * GROUP_SIZE, :], ninf)
         for g in range(N_GROUPS)], axis=0)

    eio = lax.broadcasted_iota(I32, (N_EXPERTS, tt), 0).astype(F32)
    picked = jnp.zeros((N_EXPERTS, tt), F32)
    es, gs = [], []
    for _ in range(TOP_K):
        m = jnp.max(masked, axis=0, keepdims=True)
        idx = jnp.min(jnp.where(masked == m, eio, float(N_EXPERTS)), axis=0, keepdims=True)
        hit = eio == idx
        gs.append(jnp.sum(jnp.where(hit, scores, 0.0), axis=0, keepdims=True))
        es.append(idx)
        picked = jnp.where(hit, 1.0, picked)
        masked = jnp.where(hit, ninf, masked)
    gates = jnp.concatenate(gs, axis=0)
    e_ref[...] = jnp.concatenate(es, axis=0).astype(I32)
    g_ref[...] = gates / jnp.sum(gates, axis=0, keepdims=True) * ROUTED_SCALE

    @pl.when(pl.program_id(0) == 0)
    def _():
        cnt_ref[...] = jnp.zeros_like(cnt_ref)
    cnt_ref[...] += jnp.sum(picked, axis=1, keepdims=True)


def _router(h2, wrt, bias_col):
    n, d = h2.shape
    tt = TT_ROUTE
    return pl.pallas_call(
        _router_kernel,
        grid=(n // tt,),
        in_specs=[pl.BlockSpec((tt, d), lambda i: (i, 0)),
                  pl.BlockSpec((N_EXPERTS, d), lambda i: (0, 0)),
                  pl.BlockSpec((N_EXPERTS, 1), lambda i: (0, 0))],
        out_specs=[pl.BlockSpec((TOP_K, tt), lambda i: (0, i)),
                   pl.BlockSpec((TOP_K, tt), lambda i: (0, i)),
                   pl.BlockSpec((N_EXPERTS, 1), lambda i: (0, 0))],
        out_shape=[jax.ShapeDtypeStruct((TOP_K, n), I32),
                   jax.ShapeDtypeStruct((TOP_K, n), F32),
                   jax.ShapeDtypeStruct((N_EXPERTS, 1), F32)],
        compiler_params=_params("arbitrary"),
        name="router",
    )(h2, wrt, bias_col)


def _rank_kernel(e_ref, pstart_ref, tri_ref, dest_ref, carry_ref):
    tt = e_ref.shape[1]

    @pl.when(pl.program_id(0) == 0)
    def _():
        carry_ref[...] = pstart_ref[...]

    e = e_ref[...]
    eio = lax.broadcasted_iota(I32, (N_EXPERTS, tt), 0)
    mask = jnp.zeros((N_EXPERTS, tt), F32)
    for k in range(TOP_K):
        mask = jnp.where(eio == e[k:k + 1, :], 1.0, mask)
    incl = jnp.dot(mask.astype(BF16), tri_ref[...], preferred_element_type=F32)
    pos = incl - mask + carry_ref[...]
    dest = [jnp.sum(jnp.where(eio == e[k:k + 1, :], pos, 0.0), axis=0, keepdims=True)
            for k in range(TOP_K)]
    dest_ref[...] = jnp.concatenate(dest, axis=0).astype(I32)
    carry_ref[...] += incl[:, tt - 1:tt]


def _rank(top_e_t, pstart_col, tri):
    n = top_e_t.shape[1]
    tt = TT_ROUTE
    return pl.pallas_call(
        _rank_kernel,
        grid=(n // tt,),
        in_specs=[pl.BlockSpec((TOP_K, tt), lambda i: (0, i)),
                  pl.BlockSpec((N_EXPERTS, 1), lambda i: (0, 0)),
                  pl.BlockSpec((tt, tt), lambda i: (0, 0))],
        out_specs=pl.BlockSpec((TOP_K, tt), lambda i: (0, i)),
        out_shape=jax.ShapeDtypeStruct((TOP_K, n), I32),
        scratch_shapes=[pltpu.VMEM((N_EXPERTS, 1), F32)],
        compiler_params=_params("arbitrary"),
        name="rank",
    )(top_e_t, pstart_col, tri)


def _row_copy(src_ref, src_row, dst_ref, dst_row, sem):
    return pltpu.make_async_copy(src_ref.at[pl.ds(src_row, 1)], dst_ref.at[pl.ds(dst_row, 1)], sem)


SC_CORES, SC_SUBCORES, SC_LANES = 2, 16, 16
SC_WORKERS = SC_CORES * SC_SUBCORES
INVERT_CHUNK = 8192


def _invert(dest_flat, n_tokens, rows):
    per = rows // SC_WORKERS
    n_assign = dest_flat.shape[0]
    assert rows % (SC_WORKERS * SC_LANES) == 0 and n_assign % INVERT_CHUNK == 0
    assert n_tokens & (n_tokens - 1) == 0

    @functools.partial(
        pl.kernel, mesh=plsc.VectorSubcoreMesh(core_axis_name="c", subcore_axis_name="s"),
        out_type=jax.ShapeDtypeStruct((rows,), I32),
        scratch_types=[pltpu.VMEM((INVERT_CHUNK,), I32), pltpu.VMEM((per,), I32)],
        compiler_params=pltpu.CompilerParams(needs_layout_passes=False))
    def invert(dest_hbm, out_hbm, staged, local):
        base = (lax.axis_index("s") * SC_CORES + lax.axis_index("c")) * per
        lane = lax.iota(I32, SC_LANES)

        @pl.loop(0, per, step=SC_LANES)
        def _(i):
            local[pl.ds(i, SC_LANES)] = (base + i + lane) & (n_tokens - 1)

        @pl.loop(0, n_assign // INVERT_CHUNK)
        def _(c):
            pltpu.sync_copy(dest_hbm.at[pl.ds(c * INVERT_CHUNK, INVERT_CHUNK)], staged)

            @pl.loop(0, INVERT_CHUNK, step=SC_LANES)
            def _(i):
                rel = staged[pl.ds(i, SC_LANES)] - base
                mine = (rel >= 0) & (rel < per)
                tok = (c * INVERT_CHUNK + i + lane) & (n_tokens - 1)
                plsc.store_scatter(local, [jnp.where(mine, rel, 0)], tok, mask=mine)

        pltpu.sync_copy(local, out_hbm.at[pl.ds(base, per)])

    return invert(dest_flat)


def _experts_kernel(be_ref, nv_ref, tok_hbm, h_ref, wg_ref, wu_ref, wd_ref, ys_ref,
                    xring, wgb, wub, wdb, tok_s, sem, tok_sem):
    j = pl.program_id(0)
    nv = nv_ref[0]
    depth, bm = xring.shape[0], xring.shape[1]
    ahead = depth - 1
    chunk = tok_s.shape[1]
    cb = chunk // bm
    n_chunks = tok_hbm.shape[0] // chunk
    n_blocks = n_chunks * cb

    def tok_copy(c):
        return pltpu.make_async_copy(tok_hbm.at[pl.ds(pl.multiple_of(c * chunk, chunk), chunk)],
                                     tok_s.at[c % 2], tok_sem.at[c % 2])

    def gather(block, unrolled):
        ring = block % depth
        block = jnp.minimum(block, n_blocks - 1)
        slot, off = (block // cb) % 2, (block % cb) * bm
        buf, buf_sem = xring.at[ring], sem.at[ring]

        def issue(i, carry=None):
            _row_copy(h_ref, tok_s[slot, off + i], buf, i, buf_sem).start()
            return carry

        if unrolled:
            for i in range(bm):
                issue(i)
        else:
            lax.fori_loop(0, bm, issue, 0, unroll=8)

    def wait_rows(ring):
        pltpu.make_async_copy(h_ref.at[pl.ds(0, bm)], xring.at[ring], sem.at[ring]).wait()

    @pl.when(j == 0)
    def _():
        tok_copy(0).start()
        tok_copy(0).wait()
        tok_copy(1).start()
        for b in range(ahead):
            gather(b, unrolled=False)

    first = j + ahead
    c_need = first // cb

    @pl.when(jnp.logical_and(jnp.logical_and(j > 0, first % cb == 0), c_need < n_chunks))
    def _():
        tok_copy(c_need).wait()

        @pl.when(c_need + 1 < n_chunks)
        def _():
            tok_copy(c_need + 1).start()

    jb = jnp.minimum(j, n_blocks - 1)
    fresh = jnp.logical_or(j == 0, be_ref[jb] != be_ref[jnp.maximum(jb - 1, 0)])

    @pl.when(jnp.logical_and(j < nv, fresh))
    def _():
        wgb[...] = wg_ref[...].astype(BF16)
        wub[...] = wu_ref[...].astype(BF16)
        wdb[...] = wd_ref[...].astype(BF16)

    @pl.when(j < nv)
    def _():
        wait_rows(j % depth)
        gather(j + ahead, unrolled=True)
        xb = xring[j % depth].astype(BF16)
        act = (_silu(jnp.dot(xb, wgb[...], preferred_element_type=F32))
               * jnp.dot(xb, wub[...], preferred_element_type=F32))
        ys_ref[...] = jnp.dot(act.astype(BF16), wdb[...], preferred_element_type=F32)

    @pl.when(jnp.logical_and(j >= nv, j < nv + ahead))
    def _():
        wait_rows(j % depth)


def _experts(block_e, n_valid, row_tok, h2, wg, wu, wd):
    rows = row_tok.shape[0]
    d = h2.shape[1]
    bm = EXPERT_ROWS
    n_blocks = rows // bm
    assert n_blocks % TOK_CHUNK_BLOCKS == 0 and n_blocks // TOK_CHUNK_BLOCKS >= 2
    assert GATHER_DEPTH - 1 < TOK_CHUNK_BLOCKS
    f = wg.shape[2]
    row_blk = lambda j, be, nv: (jnp.minimum(j, nv[0] - 1), 0)
    w_blk = lambda j, be, nv: (be[jnp.minimum(j, n_blocks - 1)], 0, 0)
    return pl.pallas_call(
        _experts_kernel,
        grid_spec=pltpu.PrefetchScalarGridSpec(
            num_scalar_prefetch=2,
            grid=(n_blocks + GATHER_DEPTH,),
            in_specs=[pl.BlockSpec(memory_space=pl.ANY),
                      pl.BlockSpec(memory_space=pl.ANY),
                      pl.BlockSpec((None, d, f), w_blk),
                      pl.BlockSpec((None, d, f), w_blk),
                      pl.BlockSpec((None, f, d), w_blk)],
            out_specs=pl.BlockSpec((bm, d), row_blk),
            scratch_shapes=[pltpu.VMEM((GATHER_DEPTH, bm, d), F32),
                            pltpu.VMEM((d, f), BF16), pltpu.VMEM((d, f), BF16),
                            pltpu.VMEM((f, d), BF16),
                            pltpu.SMEM((2, TOK_CHUNK_BLOCKS * bm), I32),
                            pltpu.SemaphoreType.DMA((GATHER_DEPTH,)), pltpu.SemaphoreType.DMA((2,))]),
        out_shape=jax.ShapeDtypeStruct((rows, d), F32),
        compiler_params=_params("arbitrary"),
        name="experts",
    )(block_e, n_valid, row_tok, h2, wg, wu, wd)


def _combine_kernel(dest_ref, gates_ref, base_ref, mod_ref, ys_ref, out_ref, buf, sem):
    tt = base_ref.shape[0]

    def issue(t, carry):
        for k in range(TOP_K):
            _row_copy(ys_ref, dest_ref[k, t], buf.at[k], t, sem).start()
        return carry

    def drain(t, carry):
        for k in range(TOP_K):
            _row_copy(ys_ref, dest_ref[k, t], buf.at[k], t, sem).wait()
        return carry

    lax.fori_loop(0, tt, issue, 0)
    lax.fori_loop(0, tt, drain, 0)
    gates = gates_ref[...]
    routed = gates[:, 0:1] * buf[0]
    for k in range(1, TOP_K):
        routed = routed + gates[:, k:k + 1] * buf[k]
    out_ref[...] = base_ref[...] + mod_ref[5:6, :] * routed


def _combine(dest_t, gates, base, mod, ys, seq):
    n, d = base.shape
    tt = TT_COMBINE
    tiles_per_seq = seq // tt
    return pl.pallas_call(
        _combine_kernel,
        grid=(n // tt,),
        in_specs=[pl.BlockSpec((TOP_K, tt), lambda i: (0, i), memory_space=pltpu.SMEM),
                  pl.BlockSpec((tt, TOP_K), lambda i: (i, 0)),
                  pl.BlockSpec((tt, d), lambda i: (i, 0)),
                  pl.BlockSpec((None, 6, d), lambda i: (i // tiles_per_seq, 0, 0)),
                  pl.BlockSpec(memory_space=pl.ANY)],
        out_specs=pl.BlockSpec((tt, d), lambda i: (i, 0)),
        out_shape=jax.ShapeDtypeStruct((n, d), F32),
        scratch_shapes=[pltpu.VMEM((TOP_K, tt, d), F32), pltpu.SemaphoreType.DMA],
        compiler_params=_params("arbitrary"),
        name="combine",
    )(dest_t, gates, base, mod, ys)


def _layer(x, mod, pos, rope, p):
    nbatch, seq, d = x.shape
    n = nbatch * seq
    xf = x.reshape(n, d)
    invf, bd = rope

    perm = np.concatenate([np.arange(h * HEAD_DIM, (h + 1) * HEAD_DIM) for h in PAIR_ORDER_A])
    w_in = p["w_in"]
    w_in_p = jnp.concatenate([w_in[:, :MIX_A][:, perm], w_in[:, MIX_A:]], axis=1).astype(BF16)
    ones = lambda w: jnp.ones((w,), F32)
    qscale = HEAD_DIM ** -0.5
    gcol = jnp.concatenate([jnp.tile(p["g_q_a"], N_HEADS_A) * qscale, jnp.tile(p["g_k_a"], N_KV_A),
                            ones(KV_A), jnp.tile(p["g_q_b"], N_HEADS_B) * qscale,
                            jnp.tile(p["g_k_b"], N_HEADS_B), ones(MIX_B)]).reshape(1, IN_WIDTH)
    proj = _inproj(xf, mod, pos, p["g_norm_mix"].reshape(1, d), w_in_p, gcol, invf, bd, seq)
    qa, ka, va = proj[:3]
    qkv_b = {1: proj[3:6]}
    for j, dil in enumerate(DILS):
        qkv_b[dil] = [proj[6 + t * len(DILS) + j] for t in range(3)]

    sinks_p = p["sinks_a"][np.array(PAIR_ORDER_A)]
    oa, _ = _attention(qa, ka, va, nbatch=nbatch, seq=seq, dil=1, max_dist=WINDOW_A - 1,
                       kv_shared=True, sinks=sinks_p, want_lse=False)
    obs, lses = [], []
    for window, dil in DILATED_BRANCHES:
        o, lse = _attention(*qkv_b[dil], nbatch=nbatch, seq=seq, dil=dil, max_dist=window // dil,
                            kv_shared=False, want_lse=True)
        obs.append(o)
        lses.append(lse)

    goa = p["g_out_a"][perm].reshape(1, MIX_A)
    w_out = p["w_out"]
    wo_p = jnp.concatenate([w_out[:MIX_A][perm], w_out[MIX_A:]], axis=0).astype(BF16)
    h2, base = _outproj(xf, mod, oa, obs, lses, goa, p["g_out_b"].reshape(1, MIX_B), wo_p,
                        p["g_norm_ffn"].reshape(1, d), p["w_gate_s"].astype(BF16),
                        p["w_up_s"].astype(BF16), p["w_down_s"].astype(BF16), seq)

    top_e_t, gates_t, counts = _router(h2, p["w_router"].T.astype(BF16),
                                       p["router_bias"].reshape(N_EXPERTS, 1))
    bm = EXPERT_ROWS
    counts = counts.reshape(N_EXPERTS).astype(I32)
    padded = (counts + bm - 1) // bm * bm
    pends = jnp.cumsum(padded)
    pstarts = pends - padded
    rows = n * TOP_K + N_EXPERTS * bm
    n_blocks = rows // bm
    n_valid = (pends[-1] // bm).astype(I32)
    blk = jnp.minimum(jnp.arange(n_blocks, dtype=I32), n_valid - 1)
    block_e = jnp.sum((pends[None, :] <= (blk * bm)[:, None]).astype(I32), axis=1)
    block_e = jnp.minimum(block_e, N_EXPERTS - 1)

    tri = (np.arange(TT_ROUTE)[:, None] <= np.arange(TT_ROUTE)[None, :])
    dest_t = _rank(top_e_t, pstarts.astype(F32).reshape(N_EXPERTS, 1), jnp.asarray(tri, BF16))
    n_valid = n_valid.reshape(1)
    row_tok = _invert(dest_t.reshape(n * TOP_K), n, rows)
    ys = _experts(block_e, n_valid, row_tok, h2, p["w_gate_e"], p["w_up_e"], p["w_down_e"])
    out = _combine(dest_t, gates_t.T, base, mod, ys, seq)
    return out.reshape(nbatch, seq, d)


def kernel(x, c, positions, w_ada, b_ada, g_norm_mix, w_in, g_q_a, g_k_a, sinks_a, g_q_b, g_k_b,
           g_out_a, g_out_b, w_out, g_norm_ffn, w_router, router_bias, w_gate_e, w_up_e, w_down_e,
           w_gate_s, w_up_s, w_down_s):
    nbatch, seq, d = x.shape
    depth = w_ada.shape[0]
    params = dict(g_norm_mix=g_norm_mix, w_in=w_in, g_q_a=g_q_a, g_k_a=g_k_a, sinks_a=sinks_a,
                  g_q_b=g_q_b, g_k_b=g_k_b, g_out_a=g_out_a, g_out_b=g_out_b, w_out=w_out,
                  g_norm_ffn=g_norm_ffn, w_router=w_router, router_bias=router_bias,
                  w_gate_e=w_gate_e, w_up_e=w_up_e, w_down_e=w_down_e, w_gate_s=w_gate_s,
                  w_up_s=w_up_s, w_down_s=w_down_s)
    j = np.arange(LANES) % HEAD_DIM
    inv = ROPE_THETA ** (-jnp.arange(0, ROT_DIM, 2, dtype=F32) / ROT_DIM)
    invf = jnp.where(j < ROT_DIM, inv[j % (ROT_DIM // 2)], 0.0).astype(F32).reshape(1, LANES)
    bd = jnp.asarray((np.arange(LANES)[:, None] // HEAD_DIM) == (np.arange(LANES)[None, :] // HEAD_DIM),
                     BF16)
    pos = positions.reshape(nbatch * seq, 1).astype(I32)
    for l in range(depth):
        mod = _adaln(c.astype(F32), w_ada[l], b_ada[l]).reshape(nbatch, 6, d)
        x = _layer(x, mod, pos, (invf, bd), {k: v[l] for k, v in params.items()})
    return x
```

```python
import functools

import numpy as np
import jax
import jax.numpy as jnp
from jax import lax
from jax.experimental import pallas as pl
from jax.experimental.pallas import tpu as pltpu
from jax.experimental.pallas import tpu_sc as plsc

F32 = jnp.float32
BF16 = jnp.bfloat16
I32 = jnp.int32

HEAD_DIM = 64
N_HEADS_A = 8
N_KV_A = 2
WINDOW_A = 128
N_HEADS_B = 8
DILATED_BRANCHES = ((128, 1), (512, 4), (2048, 16))
DILS = tuple(dil for _, dil in DILATED_BRANCHES if dil > 1)
BLOCK = 128
ROT_DIM = HEAD_DIM // 4
ROPE_THETA = 500000.0
MIX_A = N_HEADS_A * HEAD_DIM
KV_A = N_KV_A * HEAD_DIM
MIX_B = N_HEADS_B * HEAD_DIM
N_EXPERTS = 256
TOP_K = 8
N_GROUPS = 8
TOPK_GROUPS = 4
GROUP_SIZE = N_EXPERTS // N_GROUPS
ROUTED_SCALE = 2.5
EPS = 1e-6

LANES = 128
HEADS_PER_VREG = LANES // HEAD_DIM
N_PAIRS = MIX_A // LANES
NEG = -1e30
VMEM_LIMIT = 48 * 1024 * 1024

TT_PROJ = 512
TT_ROUTE = 256
TT_DISPATCH = 256
TT_COMBINE = 128
EXPERT_ROWS = 256
TOK_CHUNK_BLOCKS = 32
GATHER_DEPTH = 4
SUBLANES = 8

PAIR_ORDER_A = tuple(h for p in range(N_PAIRS) for h in (p, p + N_HEADS_A // N_KV_A))


def _params(*sem):
    return pltpu.CompilerParams(dimension_semantics=sem, vmem_limit_bytes=VMEM_LIMIT)


def _silu(t):
    return t / (1.0 + jnp.exp(-t))


def _rms_rows(t):
    return t * lax.rsqrt(jnp.mean(t * t, axis=-1, keepdims=True) + EPS)


def _ada_kernel(c_ref, w_ref, b_ref, o_ref):
    cond = _silu(c_ref[...])
    o_ref[...] = jnp.dot(cond.astype(BF16), w_ref[...].astype(BF16),
                         preferred_element_type=F32) + b_ref[...]


def _adaln(c, w_ada, b_ada):
    nb, d = c.shape
    width = w_ada.shape[1]
    tn = 1024
    return pl.pallas_call(
        _ada_kernel,
        grid=(width // tn,),
        in_specs=[pl.BlockSpec((nb, d), lambda j: (0, 0)),
                  pl.BlockSpec((d, tn), lambda j: (0, j)),
                  pl.BlockSpec((1, tn), lambda j: (0, j))],
        out_specs=pl.BlockSpec((nb, tn), lambda j: (0, j)),
        out_shape=jax.ShapeDtypeStruct((nb, width), F32),
        compiler_params=_params("arbitrary"),
        name="adaln",
    )(c, w_ada, b_ada.reshape(1, width))


COL_QA, COL_KA, COL_VA = 0, MIX_A, MIX_A + KV_A
COL_QB = MIX_A + 2 * KV_A
COL_KB, COL_VB = COL_QB + MIX_B, COL_QB + 2 * MIX_B
IN_WIDTH = COL_VB + MIX_B


def _inproj_kernel(x_ref, mod_ref, pos_ref, gn_ref, w_ref, gcol_ref, invf_ref, bd_ref,
                   qa_ref, ka_ref, va_ref, qb_ref, kb_ref, vb_ref, *rest):
    n_dil = len(DILS)
    dil_refs = [rest[i * n_dil:(i + 1) * n_dil] for i in range(3)]
    qb_scr, kb_scr, vb_scr = rest[3 * n_dil:]
    tt = x_ref.shape[0]
    shift, scale = mod_ref[0:1, :], mod_ref[1:2, :]
    h = _rms_rows(x_ref[...]) * gn_ref[...] * (1.0 + scale) + shift
    proj = jnp.dot(h.astype(BF16), w_ref[...], preferred_element_type=F32)

    ang = pos_ref[...].astype(F32) * invf_ref[...]
    cs, sn = jnp.cos(ang), jnp.sin(ang)
    lane = lax.broadcasted_iota(I32, (1, LANES), 1) % HEAD_DIM
    s_lo = jnp.where(lane < ROT_DIM // 2, -sn, 0.0)
    s_hi = jnp.where((lane >= ROT_DIM // 2) & (lane < ROT_DIM), sn, 0.0)
    bd = bd_ref[...]

    def norm_rope(col0, width, out_ref, scr=None):
        for j in range(width // LANES):
            c = col0 + j * LANES
            t = proj[:, c:c + LANES]
            sq = t * t
            hi = sq.astype(BF16)
            lo = (sq - hi.astype(F32)).astype(BF16)
            ss = (jnp.dot(hi, bd, preferred_element_type=F32)
                  + jnp.dot(lo, bd, preferred_element_type=F32))
            t = t * lax.rsqrt(ss * (1.0 / HEAD_DIM) + EPS) * gcol_ref[:, c:c + LANES]
            t = (t * cs + pltpu.roll(t, LANES - ROT_DIM // 2, 1) * s_lo
                 + pltpu.roll(t, ROT_DIM // 2, 1) * s_hi)
            out_ref[:, j * LANES:(j + 1) * LANES] = t.astype(BF16)
            if scr is not None:
                scr[j] = t

    norm_rope(COL_QA, MIX_A, qa_ref)
    norm_rope(COL_KA, KV_A, ka_ref)
    norm_rope(COL_QB, MIX_B, qb_ref, qb_scr)
    norm_rope(COL_KB, MIX_B, kb_ref, kb_scr)
    va_ref[...] = proj[:, COL_VA:COL_VA + KV_A].astype(BF16)
    vb_ref[...] = proj[:, COL_VB:COL_VB + MIX_B].astype(BF16)
    for j in range(N_PAIRS):
        vb_scr[j] = proj[:, COL_VB + j * LANES:COL_VB + (j + 1) * LANES]
    for scr, outs in zip((qb_scr, kb_scr, vb_scr), dil_refs):
        for dil, out in zip(DILS, outs):
            for r in range(dil):
                for j in range(N_PAIRS):
                    c = r * MIX_B + j * LANES
                    out[:, c:c + LANES] = scr[j, pl.ds(r, tt // dil, stride=dil), :].astype(BF16)


def _inproj(xf, mod, pos, g_norm, w_in_p, gcol, invf, bd, seq):
    n, d = xf.shape
    tt = TT_PROJ
    tiles_per_seq = seq // tt
    shapes = [(n, w) for w in (MIX_A, KV_A, KV_A, MIX_B, MIX_B, MIX_B)]
    shapes += [(n // dil, dil * MIX_B) for _ in range(3) for dil in DILS]
    full = lambda shape: pl.BlockSpec(shape, lambda i: (0,) * len(shape))
    return pl.pallas_call(
        _inproj_kernel,
        grid=(n // tt,),
        in_specs=[pl.BlockSpec((tt, d), lambda i: (i, 0)),
                  pl.BlockSpec((None, 6, d), lambda i: (i // tiles_per_seq, 0, 0)),
                  pl.BlockSpec((tt, 1), lambda i: (i, 0)),
                  full((1, d)), full((d, IN_WIDTH)), full((1, IN_WIDTH)),
                  full((1, LANES)), full((LANES, LANES))],
        out_specs=[pl.BlockSpec((tt * r // n, w), lambda i: (i, 0)) for r, w in shapes],
        out_shape=[jax.ShapeDtypeStruct(s, BF16) for s in shapes],
        scratch_shapes=[pltpu.VMEM((N_PAIRS, tt, LANES), F32)] * 3,
        compiler_params=_params("parallel"),
        name="inproj",
    )(xf, mod, pos, g_norm, w_in_p, gcol, invf, bd)


def _attn_kernel(*refs, kv_shared, max_dist, use_prev, has_sinks, want_lse):
    refs = list(refs)
    sink_ref = refs.pop(0) if has_sinks else None
    q_ref = refs.pop(0)
    kp_ref = refs.pop(0) if use_prev else None
    kc_ref = refs.pop(0)
    vp_ref = refs.pop(0) if use_prev else None
    vc_ref = refs.pop(0)
    o_ref = refs.pop(0)
    lse_ref = refs.pop(0) if want_lse else None

    blk = pl.program_id(2)
    nq = 2 * BLOCK
    nk = 2 * BLOCK if use_prev else BLOCK
    qpos = lax.broadcasted_iota(I32, (nq, nk), 0) % BLOCK
    kpos = lax.broadcasted_iota(I32, (nq, nk), 1)
    if use_prev:
        dist = qpos + BLOCK - kpos
        valid = (dist >= 0) & (dist <= max_dist) & ((kpos >= BLOCK) | (blk > 0))
    else:
        dist = qpos - kpos
        valid = (dist >= 0) & (dist <= max_dist)
    lane = lax.broadcasted_iota(I32, (nq, LANES), 1)
    row = lax.broadcasted_iota(I32, (nq, LANES), 0)
    own_half = (lane < HEAD_DIM) == (row < BLOCK)
    left_lanes = lax.broadcasted_iota(I32, (BLOCK, LANES), 1) < HEAD_DIM
    lane8 = lax.broadcasted_iota(I32, (BLOCK, 2 * N_PAIRS), 1)
    lse_blk = jnp.zeros((BLOCK, 2 * N_PAIRS), F32)

    for p in range(N_PAIRS):
        cq = slice(p * LANES, (p + 1) * LANES)
        ck = slice(0, LANES) if kv_shared else cq
        qp = q_ref[:, cq]
        qs = jnp.concatenate([qp, qp], axis=0)
        qs = jnp.where(own_half, qs, jnp.zeros_like(qs))
        if use_prev:
            k = jnp.concatenate([kp_ref[:, ck], kc_ref[:, ck]], axis=0)
            v = jnp.concatenate([vp_ref[:, ck], vc_ref[:, ck]], axis=0)
        else:
            k, v = kc_ref[:, ck], vc_ref[:, ck]
        s = lax.dot_general(qs, k, (((1,), (1,)), ((), ())), preferred_element_type=F32)
        s = jnp.where(valid, s, NEG)
        m = jnp.max(s, axis=-1, keepdims=True)
        if has_sinks:
            rows1 = lax.broadcasted_iota(I32, (nq, 1), 0)
            sink = jnp.where(rows1 < BLOCK, sink_ref[2 * p], sink_ref[2 * p + 1])
            m = jnp.maximum(m, sink)
        e = jnp.exp(s - m)
        l = jnp.sum(e, axis=-1, keepdims=True)
        if has_sinks:
            l = l + jnp.exp(sink - m)
        o = jnp.dot(e.astype(BF16), v, preferred_element_type=F32) / l
        o_ref[:, cq] = jnp.where(left_lanes, o[:BLOCK], o[BLOCK:]).astype(BF16)
        if want_lse:
            lse = m + jnp.log(l)
            lse_blk = (lse_blk + jnp.where(lane8 == 2 * p, lse[:BLOCK], 0.0)
                       + jnp.where(lane8 == 2 * p + 1, lse[BLOCK:], 0.0))
    if want_lse:
        lse_ref[...] = lse_blk


def _attention(q, k, v, *, nbatch, seq, dil, max_dist, kv_shared, sinks=None, want_lse):
    length = seq // dil
    nblk = length // BLOCK
    use_prev = nblk > 1
    kw = k.shape[1] // dil
    view = lambda t: t.reshape(nbatch, length, t.shape[1])
    cur = lambda b, r, i: (b, i, r)
    prev = lambda b, r, i: (b, jnp.maximum(i - 1, 0), r)
    in_specs, args = [], []
    if sinks is not None:
        in_specs.append(pl.BlockSpec(memory_space=pltpu.SMEM))
        args.append(sinks)
    in_specs.append(pl.BlockSpec((None, BLOCK, MIX_B), cur))
    args.append(view(q))
    for t in (k, v):
        if use_prev:
            in_specs.append(pl.BlockSpec((None, BLOCK, kw), prev))
            args.append(view(t))
        in_specs.append(pl.BlockSpec((None, BLOCK, kw), cur))
        args.append(view(t))
    out_specs = [pl.BlockSpec((None, BLOCK, MIX_B), cur)]
    out_shape = [jax.ShapeDtypeStruct((nbatch, length, dil * MIX_B), BF16)]
    if want_lse:
        out_specs.append(pl.BlockSpec((None, None, BLOCK, N_HEADS_B), lambda b, r, i: (b, r, i, 0)))
        out_shape.append(jax.ShapeDtypeStruct((nbatch, dil, length, N_HEADS_B), F32))
    outs = pl.pallas_call(
        functools.partial(_attn_kernel, kv_shared=kv_shared, max_dist=max_dist, use_prev=use_prev,
                          has_sinks=sinks is not None, want_lse=want_lse),
        grid=(nbatch, dil, nblk),
        in_specs=in_specs, out_specs=out_specs, out_shape=out_shape,
        compiler_params=_params("parallel", "parallel", "arbitrary"),
        name=f"attn_d{dil}" + ("_swa" if kv_shared else ""),
    )(*args)
    o = outs[0].reshape(nbatch * length, dil * MIX_B)
    if not want_lse:
        return o, None
    lse = outs[1].transpose(0, 2, 1, 3).reshape(nbatch * seq, N_HEADS_B)
    return o, lse


def _expand_heads(w, width):
    head = lax.broadcasted_iota(I32, (1, width), 1) // HEAD_DIM
    out = jnp.zeros((w.shape[0], width), F32)
    for hd in range(w.shape[1]):
        out = jnp.where(head == hd, w[:, hd:hd + 1], out)
    return out


def _outproj_kernel(x_ref, mod_ref, oa_ref, ob1_ref, ob2_ref, ob3_ref, l1_ref, l2_ref, l3_ref,
                    goa_ref, gob_ref, wo_ref, gf_ref, wgs_ref, wus_ref, wds_ref,
                    h2_ref, base_ref, ob_scr):
    tt = x_ref.shape[0]
    gate_a = mod_ref[2:3, :]
    shift_m, scale_m, gate_m = mod_ref[3:4, :], mod_ref[4:5, :], mod_ref[5:6, :]

    def token_major(ref, dil):
        if dil == 1:
            return ref[...].astype(F32)
        for r in range(dil):
            for j in range(N_PAIRS):
                c = r * MIX_B + j * LANES
                ob_scr[j, pl.ds(r, tt // dil, stride=dil), :] = ref[:, c:c + LANES].astype(F32)
        return jnp.concatenate([ob_scr[j] for j in range(N_PAIRS)], axis=1)

    l1, l2, l3 = l1_ref[...], l2_ref[...], l3_ref[...]
    mx = jnp.maximum(jnp.maximum(l1, l2), l3)
    e1, e2, e3 = jnp.exp(l1 - mx), jnp.exp(l2 - mx), jnp.exp(l3 - mx)
    den = e1 + e2 + e3
    dils = [dil for _, dil in DILATED_BRANCHES]
    ob = _expand_heads(e1 / den, MIX_B) * token_major(ob1_ref, dils[0])
    ob = ob + _expand_heads(e2 / den, MIX_B) * token_major(ob2_ref, dils[1])
    ob = ob + _expand_heads(e3 / den, MIX_B) * token_major(ob3_ref, dils[2])
    ob = _rms_rows(ob) * gob_ref[...]
    oa = _rms_rows(oa_ref[...].astype(F32)) * goa_ref[...]
    y = (jnp.dot(oa.astype(BF16), wo_ref[0:MIX_A, :], preferred_element_type=F32)
         + jnp.dot(ob.astype(BF16), wo_ref[MIX_A:MIX_A + MIX_B, :], preferred_element_type=F32))
    x1 = x_ref[...] + gate_a * y
    h2 = _rms_rows(x1) * gf_ref[...] * (1.0 + scale_m) + shift_m
    h2_ref[...] = h2
    hb = h2.astype(BF16)
    act = (_silu(jnp.dot(hb, wgs_ref[...], preferred_element_type=F32))
           * jnp.dot(hb, wus_ref[...], preferred_element_type=F32))
    shared = jnp.dot(act.astype(BF16), wds_ref[...], preferred_element_type=F32)
    base_ref[...] = x1 + gate_m * shared


def _outproj(xf, mod, oa, obs, lses, goa, gob, wo_p, gf, wgs, wus, wds, seq):
    n, d = xf.shape
    tt = TT_PROJ // 2
    tiles_per_seq = seq // tt
    tile = lambda w: pl.BlockSpec((tt, w), lambda i: (i, 0))
    full = lambda shape: pl.BlockSpec(shape, lambda i: (0,) * len(shape))
    sd = wgs.shape[1]
    dilated = [pl.BlockSpec((tt // dil, dil * MIX_B), lambda i: (i, 0)) for _, dil in DILATED_BRANCHES]
    return pl.pallas_call(
        _outproj_kernel,
        grid=(n // tt,),
        in_specs=[tile(d), pl.BlockSpec((None, 6, d), lambda i: (i // tiles_per_seq, 0, 0)),
                  tile(MIX_A), *dilated,
                  tile(N_HEADS_B), tile(N_HEADS_B), tile(N_HEADS_B),
                  full((1, MIX_A)), full((1, MIX_B)), full((MIX_A + MIX_B, d)), full((1, d)),
                  full((d, sd)), full((d, sd)), full((sd, d))],
        out_specs=[tile(d), tile(d)],
        out_shape=[jax.ShapeDtypeStruct((n, d), F32), jax.ShapeDtypeStruct((n, d), F32)],
        scratch_shapes=[pltpu.VMEM((N_PAIRS, tt, LANES), F32)],
        compiler_params=_params("parallel"),
        name="outproj",
    )(xf, mod, oa, *obs, *lses, goa, gob, wo_p, gf, wgs, wus, wds)


def _router_kernel(h_ref, wrt_ref, bias_ref, e_ref, g_ref, cnt_ref):
    tt = h_ref.shape[0]
    logits = lax.dot_general(wrt_ref[...], h_ref[...].astype(BF16), (((1,), (1,)), ((), ())),
                             preferred_element_type=F32)
    scores = 1.0 / (1.0 + jnp.exp(-logits))
    biased = scores + bias_ref[...]
    ninf = -jnp.inf

    j32 = lax.broadcasted_iota(I32, (GROUP_SIZE, tt), 0).astype(F32)
    grp = []
    for g in range(N_GROUPS):
        bg = biased[g * GROUP_SIZE:(g + 1) * GROUP_SIZE, :]
        m1 = jnp.max(bg, axis=0, keepdims=True)
        i1 = jnp.min(jnp.where(bg == m1, j32, float(GROUP_SIZE)), axis=0, keepdims=True)
        m2 = jnp.max(jnp.where(j32 == i1, ninf, bg), axis=0, keepdims=True)
        grp.append(m1 + m2)
    grp = jnp.concatenate(grp, axis=0)
    g8 = lax.broadcasted_iota(I32, (N_GROUPS, tt), 0).astype(F32)
    chosen = jnp.zeros((N_GROUPS, tt), F32)
    for _ in range(TOPK_GROUPS):
        gm = jnp.max(grp, axis=0, keepdims=True)
        gi = jnp.min(jnp.where(grp == gm, g8, float(N_GROUPS)), axis=0, keepdims=True)
        hit = g8 == gi
        chosen = jnp.where(hit, 1.0, chosen)
        grp = jnp.where(hit, ninf, grp)
    masked = jnp.concatenate(
        [jnp.where(chosen[g:g + 1, :] > 0.0, biased[g * GROUP_SIZE:(g + 1) * GROUP_SIZE, :], ninf)
         for g in range(N_GROUPS)], axis=0)

    eio = lax.broadcasted_iota(I32, (N_EXPERTS, tt), 0).astype(F32)
    picked = jnp.zeros((N_EXPERTS, tt), F32)
    es, gs = [], []
    for _ in range(TOP_K):
        m = jnp.max(masked, axis=0, keepdims=True)
        idx = jnp.min(jnp.where(masked == m, eio, float(N_EXPERTS)), axis=0, keepdims=True)
        hit = eio == idx
        gs.append(jnp.sum(jnp.where(hit, scores, 0.0), axis=0, keepdims=True))
        es.append(idx)
        picked = jnp.where(hit, 1.0, picked)
        masked = jnp.where(hit, ninf, masked)
    gates = jnp.concatenate(gs, axis=0)
    e_ref[...] = jnp.concatenate(es, axis=0).astype(I32)
    g_ref[...] = gates / jnp.sum(gates, axis=0, keepdims=True) * ROUTED_SCALE

    @pl.when(pl.program_id(0) == 0)
    def _():
        cnt_ref[...] = jnp.zeros_like(cnt_ref)
    cnt_ref[...] += jnp.sum(picked, axis=1, keepdims=True)


def _router(h2, wrt, bias_col):
    n, d = h2.shape
    tt = TT_ROUTE
    return pl.pallas_call(
        _router_kernel,
        grid=(n // tt,),
        in_specs=[pl.BlockSpec((tt, d), lambda i: (i, 0)),
                  pl.BlockSpec((N_EXPERTS, d), lambda i: (0, 0)),
                  pl.BlockSpec((N_EXPERTS, 1), lambda i: (0, 0))],
        out_specs=[pl.BlockSpec((TOP_K, tt), lambda i: (0, i)),
                   pl.BlockSpec((TOP_K, tt), lambda i: (0, i)),
                   pl.BlockSpec((N_EXPERTS, 1), lambda i: (0, 0))],
        out_shape=[jax.ShapeDtypeStruct((TOP_K, n), I32),
                   jax.ShapeDtypeStruct((TOP_K, n), F32),
                   jax.ShapeDtypeStruct((N_EXPERTS, 1), F32)],
        compiler_params=_params("arbitrary"),
        name="router",
    )(h2, wrt, bias_col)


def _rank_kernel(e_ref, pstart_ref, tri_ref, dest_ref, carry_ref):
    tt = e_ref.shape[1]

    @pl.when(pl.program_id(0) == 0)
    def _():
        carry_ref[...] = pstart_ref[...]

    e = e_ref[...]
    eio = lax.broadcasted_iota(I32, (N_EXPERTS, tt), 0)
    mask = jnp.zeros((N_EXPERTS, tt), F32)
    for k in range(TOP_K):
        mask = jnp.where(eio == e[k:k + 1, :], 1.0, mask)
    incl = jnp.dot(mask.astype(BF16), tri_ref[...], preferred_element_type=F32)
    pos = incl - mask + carry_ref[...]
    dest = [jnp.sum(jnp.where(eio == e[k:k + 1, :], pos, 0.0), axis=0, keepdims=True)
            for k in range(TOP_K)]
    dest_ref[...] = jnp.concatenate(dest, axis=0).astype(I32)
    carry_ref[...] += incl[:, tt - 1:tt]


def _rank(top_e_t, pstart_col, tri):
    n = top_e_t.shape[1]
    tt = TT_ROUTE
    return pl.pallas_call(
        _rank_kernel,
        grid=(n // tt,),
        in_specs=[pl.BlockSpec((TOP_K, tt), lambda i: (0, i)),
                  pl.BlockSpec((N_EXPERTS, 1), lambda i: (0, 0)),
                  pl.BlockSpec((tt, tt), lambda i: (0, 0))],
        out_specs=pl.BlockSpec((TOP_K, tt), lambda i: (0, i)),
        out_shape=jax.ShapeDtypeStruct((TOP_K, n), I32),
        scratch_shapes=[pltpu.VMEM((N_EXPERTS, 1), F32)],
        compiler_params=_params("arbitrary"),
        name="rank",
    )(top_e_t, pstart_col, tri)


def _row_copy(src_ref, src_row, dst_ref, dst_row, sem):
    return pltpu.make_async_copy(src_ref.at[pl.ds(src_row, 1)], dst_ref.at[pl.ds(dst_row, 1)], sem)


SC_CORES, SC_SUBCORES, SC_LANES = 2, 16, 16
SC_WORKERS = SC_CORES * SC_SUBCORES
INVERT_CHUNK = 8192


def _invert(dest_flat, n_tokens, rows):
    per = rows // SC_WORKERS
    n_assign = dest_flat.shape[0]
    assert rows % (SC_WORKERS * SC_LANES) == 0 and n_assign % INVERT_CHUNK == 0
    assert n_tokens & (n_tokens - 1) == 0

    @functools.partial(
        pl.kernel, mesh=plsc.VectorSubcoreMesh(core_axis_name="c", subcore_axis_name="s"),
        out_type=jax.ShapeDtypeStruct((rows,), I32),
        scratch_types=[pltpu.VMEM((INVERT_CHUNK,), I32), pltpu.VMEM((per,), I32)],
        compiler_params=pltpu.CompilerParams(needs_layout_passes=False))
    def invert(dest_hbm, out_hbm, staged, local):
        base = (lax.axis_index("s") * SC_CORES + lax.axis_index("c")) * per
        lane = lax.iota(I32, SC_LANES)

        @pl.loop(0, per, step=SC_LANES)
        def _(i):
            local[pl.ds(i, SC_LANES)] = (base + i + lane) & (n_tokens - 1)

        @pl.loop(0, n_assign // INVERT_CHUNK)
        def _(c):
            pltpu.sync_copy(dest_hbm.at[pl.ds(c * INVERT_CHUNK, INVERT_CHUNK)], staged)

            @pl.loop(0, INVERT_CHUNK, step=SC_LANES)
            def _(i):
                rel = staged[pl.ds(i, SC_LANES)] - base
                mine = (rel >= 0) & (rel < per)
                tok = (c * INVERT_CHUNK + i + lane) & (n_tokens - 1)
                plsc.store_scatter(local, [jnp.where(mine, rel, 0)], tok, mask=mine)

        pltpu.sync_copy(local, out_hbm.at[pl.ds(base, per)])

    return invert(dest_flat)


def _experts_kernel(be_ref, nv_ref, tok_hbm, h_ref, wg_ref, wu_ref, wd_ref, ys_ref,
                    xring, wgb, wub, wdb, tok_s, sem, tok_sem):
    j = pl.program_id(0)
    nv = nv_ref[0]
    depth, bm = xring.shape[0], xring.shape[1]
    ahead = depth - 1
    chunk = tok_s.shape[1]
    cb = chunk // bm
    n_chunks = tok_hbm.shape[0] // chunk
    n_blocks = n_chunks * cb

    def tok_copy(c):
        return pltpu.make_async_copy(tok_hbm.at[pl.ds(pl.multiple_of(c * chunk, chunk), chunk)],
                                     tok_s.at[c % 2], tok_sem.at[c % 2])

    def gather(block, unrolled):
        ring = block % depth
        block = jnp.minimum(block, n_blocks - 1)
        slot, off = (block // cb) % 2, (block % cb) * bm
        buf, buf_sem = xring.at[ring], sem.at[ring]

        def issue(i, carry=None):
            _row_copy(h_ref, tok_s[slot, off + i], buf, i, buf_sem).start()
            return carry

        if unrolled:
            for i in range(bm):
                issue(i)
        else:
            lax.fori_loop(0, bm, issue, 0, unroll=8)

    def wait_rows(ring):
        pltpu.make_async_copy(h_ref.at[pl.ds(0, bm)], xring.at[ring], sem.at[ring]).wait()

    @pl.when(j == 0)
    def _():
        tok_copy(0).start()
        tok_copy(0).wait()
        tok_copy(1).start()
        for b in range(ahead):
            gather(b, unrolled=False)

    first = j + ahead
    c_need = first // cb

    @pl.when(jnp.logical_and(jnp.logical_and(j > 0, first % cb == 0), c_need < n_chunks))
    def _():
        tok_copy(c_need).wait()

        @pl.when(c_need + 1 < n_chunks)
        def _():
            tok_copy(c_need + 1).start()

    jb = jnp.minimum(j, n_blocks - 1)
    fresh = jnp.logical_or(j == 0, be_ref[jb] != be_ref[jnp.maximum(jb - 1, 0)])

    @pl.when(jnp.logical_and(j < nv, fresh))
    def _():
        wgb[...] = wg_ref[...].astype(BF16)
        wub[...] = wu_ref[...].astype(BF16)
        wdb[...] = wd_ref[...].astype(BF16)

    @pl.when(j < nv)
    def _():
        wait_rows(j % depth)
        gather(j + ahead, unrolled=True)
        xb = xring[j % depth].astype(BF16)
        act = (_silu(jnp.dot(xb, wgb[...], preferred_element_type=F32))
               * jnp.dot(xb, wub[...], preferred_element_type=F32))
        ys_ref[...] = jnp.dot(act.astype(BF16), wdb[...], preferred_element_type=F32)

    @pl.when(jnp.logical_and(j >= nv, j < nv + ahead))
    def _():
        wait_rows(j % depth)


def _experts(block_e, n_valid, row_tok, h2, wg, wu, wd):
    rows = row_tok.shape[0]
    d = h2.shape[1]
    bm = EXPERT_ROWS
    n_blocks = rows // bm
    assert n_blocks % TOK_CHUNK_BLOCKS == 0 and n_blocks // TOK_CHUNK_BLOCKS >= 2
    assert GATHER_DEPTH - 1 < TOK_CHUNK_BLOCKS
    f = wg.shape[2]
    row_blk = lambda j, be, nv: (jnp.minimum(j, nv[0] - 1), 0)
    w_blk = lambda j, be, nv: (be[jnp.minimum(j, n_blocks - 1)], 0, 0)
    return pl.pallas_call(
        _experts_kernel,
        grid_spec=pltpu.PrefetchScalarGridSpec(
            num_scalar_prefetch=2,
            grid=(n_blocks + GATHER_DEPTH,),
            in_specs=[pl.BlockSpec(memory_space=pl.ANY),
                      pl.BlockSpec(memory_space=pl.ANY),
                      pl.BlockSpec((None, d, f), w_blk),
                      pl.BlockSpec((None, d, f), w_blk),
                      pl.BlockSpec((None, f, d), w_blk)],
            out_specs=pl.BlockSpec((bm, d), row_blk),
            scratch_shapes=[pltpu.VMEM((GATHER_DEPTH, bm, d), F32),
                            pltpu.VMEM((d, f), BF16), pltpu.VMEM((d, f), BF16),
                            pltpu.VMEM((f, d), BF16),
                            pltpu.SMEM((2, TOK_CHUNK_BLOCKS * bm), I32),
                            pltpu.SemaphoreType.DMA((GATHER_DEPTH,)), pltpu.SemaphoreType.DMA((2,))]),
        out_shape=jax.ShapeDtypeStruct((rows, d), F32),
        compiler_params=_params("arbitrary"),
        name="experts",
    )(block_e, n_valid, row_tok, h2, wg, wu, wd)


def _combine_kernel(dest_ref, gates_ref, base_ref, mod_ref, ys_ref, out_ref, buf, sem):
    tt = base_ref.shape[0]
    for t in range(tt):
        for k in range(TOP_K):
            _row_copy(ys_ref, dest_ref[k, t], buf.at[k], t, sem).start()
    for k in range(TOP_K):
        pltpu.make_async_copy(ys_ref.at[pl.ds(0, tt)], buf.at[k], sem).wait()
    gates = gates_ref[...]
    routed = gates[:, 0:1] * buf[0]
    for k in range(1, TOP_K):
        routed = routed + gates[:, k:k + 1] * buf[k]
    out_ref[...] = base_ref[...] + mod_ref[5:6, :] * routed


def _combine(dest_t, gates, base, mod, ys, seq):
    n, d = base.shape
    tt = TT_COMBINE
    tiles_per_seq = seq // tt
    return pl.pallas_call(
        _combine_kernel,
        grid=(n // tt,),
        in_specs=[pl.BlockSpec((TOP_K, tt), lambda i: (0, i), memory_space=pltpu.SMEM),
                  pl.BlockSpec((tt, TOP_K), lambda i: (i, 0)),
                  pl.BlockSpec((tt, d), lambda i: (i, 0)),
                  pl.BlockSpec((None, 6, d), lambda i: (i // tiles_per_seq, 0, 0)),
                  pl.BlockSpec(memory_space=pl.ANY)],
        out_specs=pl.BlockSpec((tt, d), lambda i: (i, 0)),
        out_shape=jax.ShapeDtypeStruct((n, d), F32),
        scratch_shapes=[pltpu.VMEM((TOP_K, tt, d), F32), pltpu.SemaphoreType.DMA],
        compiler_params=_params("arbitrary"),
        name="combine",
    )(dest_t, gates, base, mod, ys)


def _layer(x, mod, pos, rope, p):
    nbatch, seq, d = x.shape
    n = nbatch * seq
    xf = x.reshape(n, d)
    invf, bd = rope

    perm = np.concatenate([np.arange(h * HEAD_DIM, (h + 1) * HEAD_DIM) for h in PAIR_ORDER_A])
    w_in = p["w_in"]
    w_in_p = jnp.concatenate([w_in[:, :MIX_A][:, perm], w_in[:, MIX_A:]], axis=1).astype(BF16)
    ones = lambda w: jnp.ones((w,), F32)
    qscale = HEAD_DIM ** -0.5
    gcol = jnp.concatenate([jnp.tile(p["g_q_a"], N_HEADS_A) * qscale, jnp.tile(p["g_k_a"], N_KV_A),
                            ones(KV_A), jnp.tile(p["g_q_b"], N_HEADS_B) * qscale,
                            jnp.tile(p["g_k_b"], N_HEADS_B), ones(MIX_B)]).reshape(1, IN_WIDTH)
    proj = _inproj(xf, mod, pos, p["g_norm_mix"].reshape(1, d), w_in_p, gcol, invf, bd, seq)
    qa, ka, va = proj[:3]
    qkv_b = {1: proj[3:6]}
    for j, dil in enumerate(DILS):
        qkv_b[dil] = [proj[6 + t * len(DILS) + j] for t in range(3)]

    sinks_p = p["sinks_a"][np.array(PAIR_ORDER_A)]
    oa, _ = _attention(qa, ka, va, nbatch=nbatch, seq=seq, dil=1, max_dist=WINDOW_A - 1,
                       kv_shared=True, sinks=sinks_p, want_lse=False)
    obs, lses = [], []
    for window, dil in DILATED_BRANCHES:
        o, lse = _attention(*qkv_b[dil], nbatch=nbatch, seq=seq, dil=dil, max_dist=window // dil,
                            kv_shared=False, want_lse=True)
        obs.append(o)
        lses.append(lse)

    goa = p["g_out_a"][perm].reshape(1, MIX_A)
    w_out = p["w_out"]
    wo_p = jnp.concatenate([w_out[:MIX_A][perm], w_out[MIX_A:]], axis=0).astype(BF16)
    h2, base = _outproj(xf, mod, oa, obs, lses, goa, p["g_out_b"].reshape(1, MIX_B), wo_p,
                        p["g_norm_ffn"].reshape(1, d), p["w_gate_s"].astype(BF16),
                        p["w_up_s"].astype(BF16), p["w_down_s"].astype(BF16), seq)

    top_e_t, gates_t, counts = _router(h2, p["w_router"].T.astype(BF16),
                                       p["router_bias"].reshape(N_EXPERTS, 1))
    bm = EXPERT_ROWS
    counts = counts.reshape(N_EXPERTS).astype(I32)
    padded = (counts + bm - 1) // bm * bm
    pends = jnp.cumsum(padded)
    pstarts = pends - padded
    rows = n * TOP_K + N_EXPERTS * bm
    n_blocks = rows // bm
    n_valid = (pends[-1] // bm).astype(I32)
    blk = jnp.minimum(jnp.arange(n_blocks, dtype=I32), n_valid - 1)
    block_e = jnp.sum((pends[None, :] <= (blk * bm)[:, None]).astype(I32), axis=1)
    block_e = jnp.minimum(block_e, N_EXPERTS - 1)

    tri = (np.arange(TT_ROUTE)[:, None] <= np.arange(TT_ROUTE)[None, :])
    dest_t = _rank(top_e_t, pstarts.astype(F32).reshape(N_EXPERTS, 1), jnp.asarray(tri, BF16))
    n_valid = n_valid.reshape(1)
    row_tok = _invert(dest_t.reshape(n * TOP_K), n, rows)
    ys = _experts(block_e, n_valid, row_tok, h2, p["w_gate_e"], p["w_up_e"], p["w_down_e"])
    out = _combine(dest_t, gates_t.T, base, mod, ys, seq)
    return out.reshape(nbatch, seq, d)


def kernel(x, c, positions, w_ada, b_ada, g_norm_mix, w_in, g_q_a, g_k_a, sinks_a, g_q_b, g_k_b,
           g_out_a, g_out_b, w_out, g_norm_ffn, w_router, router_bias, w_gate_e, w_up_e, w_down_e,
           w_gate_s, w_up_s, w_down_s):
    nbatch, seq, d = x.shape
    depth = w_ada.shape[0]
    params = dict(g_norm_mix=g_norm_mix, w_in=w_in, g_q_a=g_q_a, g_k_a=g_k_a, sinks_a=sinks_a,
                  g_q_b=g_q_b, g_k_b=g_k_b, g_out_a=g_out_a, g_out_b=g_out_b, w_out=w_out,
                  g_norm_ffn=g_norm_ffn, w_router=w_router, router_bias=router_bias,
                  w_gate_e=w_gate_e, w_up_e=w_up_e, w_down_e=w_down_e, w_gate_s=w_gate_s,
                  w_up_s=w_up_s, w_down_s=w_down_s)
    j = np.arange(LANES) % HEAD_DIM
    inv = ROPE_THETA ** (-jnp.arange(0, ROT_DIM, 2, dtype=F32) / ROT_DIM)
    invf = jnp.where(j < ROT_DIM, inv[j % (ROT_DIM // 2)], 0.0).astype(F32).reshape(1, LANES)
    bd = jnp.asarray((np.arange(LANES)[:, None] // HEAD_DIM) == (np.arange(LANES)[None, :] // HEAD_DIM),
                     BF16)
    pos = positions.reshape(nbatch * seq, 1).astype(I32)
    for l in range(depth):
        mod = _adaln(c.astype(F32), w_ada[l], b_ada[l]).reshape(nbatch, 6, d)
        x = _layer(x, mod, pos, (invf, bd), {k: v[l] for k, v in params.items()})
    return x
```

```python
import functools

import numpy as np
import jax
import jax.numpy as jnp
from jax import lax
from jax.experimental import pallas as pl
from jax.experimental.pallas import tpu as pltpu
from jax.experimental.pallas import tpu_sc as plsc

F32 = jnp.float32
BF16 = jnp.bfloat16
I32 = jnp.int32

HEAD_DIM = 64
N_HEADS_A = 8
N_KV_A = 2
WINDOW_A = 128
N_HEADS_B = 8
DILATED_BRANCHES = ((128, 1), (512, 4), (2048, 16))
DILS = tuple(dil for _, dil in DILATED_BRANCHES if dil > 1)
BLOCK = 128
ROT_DIM = HEAD_DIM // 4
ROPE_THETA = 500000.0
MIX_A = N_HEADS_A * HEAD_DIM
KV_A = N_KV_A * HEAD_DIM
MIX_B = N_HEADS_B * HEAD_DIM
N_EXPERTS = 256
TOP_K = 8
N_GROUPS = 8
TOPK_GROUPS = 4
GROUP_SIZE = N_EXPERTS // N_GROUPS
ROUTED_SCALE = 2.5
EPS = 1e-6

LANES = 128
HEADS_PER_VREG = LANES // HEAD_DIM
N_PAIRS = MIX_A // LANES
NEG = -1e30
VMEM_LIMIT = 48 * 1024 * 1024

TT_PROJ = 512
TT_ROUTE = 256
TT_DISPATCH = 256
TT_COMBINE = 128
EXPERT_ROWS = 256
TOK_CHUNK_BLOCKS = 32
GATHER_DEPTH = 4
SUBLANES = 8

PAIR_ORDER_A = tuple(h for p in range(N_PAIRS) for h in (p, p + N_HEADS_A // N_KV_A))


def _params(*sem):
    return pltpu.CompilerParams(dimension_semantics=sem, vmem_limit_bytes=VMEM_LIMIT)


def _silu(t):
    return t / (1.0 + jnp.exp(-t))


def _rms_rows(t):
    return t * lax.rsqrt(jnp.mean(t * t, axis=-1, keepdims=True) + EPS)


def _store_row_tiles(ref, value):
    rows, d = value.shape
    per = d // LANES
    for s in range(per):
        ref[pl.ds(s, rows, stride=per), :] = value[:, s * LANES:(s + 1) * LANES]


def _load_row_tiles(ref, rows, per, dtype):
    return jnp.concatenate([ref[pl.ds(s, rows, stride=per), :].astype(dtype) for s in range(per)], axis=1)


def _tile_copy(src_ref, src_row, dst_ref, dst_row, per, sem):
    start = lambda row: row * per if isinstance(row, int) else pl.multiple_of(row * per, per)
    return pltpu.make_async_copy(src_ref.at[pl.ds(start(src_row), per)],
                                 dst_ref.at[pl.ds(start(dst_row), per)], sem)


def _ada_kernel(c_ref, w_ref, b_ref, o_ref):
    cond = _silu(c_ref[...])
    o_ref[...] = jnp.dot(cond.astype(BF16), w_ref[...].astype(BF16),
                         preferred_element_type=F32) + b_ref[...]


def _adaln(c, w_ada, b_ada):
    nb, d = c.shape
    width = w_ada.shape[1]
    tn = 1024
    return pl.pallas_call(
        _ada_kernel,
        grid=(width // tn,),
        in_specs=[pl.BlockSpec((nb, d), lambda j: (0, 0)),
                  pl.BlockSpec((d, tn), lambda j: (0, j)),
                  pl.BlockSpec((1, tn), lambda j: (0, j))],
        out_specs=pl.BlockSpec((nb, tn), lambda j: (0, j)),
        out_shape=jax.ShapeDtypeStruct((nb, width), F32),
        compiler_params=_params("arbitrary"),
        name="adaln",
    )(c, w_ada, b_ada.reshape(1, width))


COL_QA, COL_KA, COL_VA = 0, MIX_A, MIX_A + KV_A
COL_QB = MIX_A + 2 * KV_A
COL_KB, COL_VB = COL_QB + MIX_B, COL_QB + 2 * MIX_B
IN_WIDTH = COL_VB + MIX_B


def _inproj_kernel(x_ref, mod_ref, pos_ref, gn_ref, w_ref, gcol_ref, invf_ref, bd_ref,
                   qa_ref, ka_ref, va_ref, qb_ref, kb_ref, vb_ref, *rest):
    n_dil = len(DILS)
    dil_refs = [rest[i * n_dil:(i + 1) * n_dil] for i in range(3)]
    qb_scr, kb_scr, vb_scr = rest[3 * n_dil:]
    tt = x_ref.shape[0]
    shift, scale = mod_ref[0:1, :], mod_ref[1:2, :]
    h = _rms_rows(x_ref[...]) * gn_ref[...] * (1.0 + scale) + shift
    proj = jnp.dot(h.astype(BF16), w_ref[...], preferred_element_type=F32)

    ang = pos_ref[...].astype(F32) * invf_ref[...]
    cs, sn = jnp.cos(ang), jnp.sin(ang)
    lane = lax.broadcasted_iota(I32, (1, LANES), 1) % HEAD_DIM
    s_lo = jnp.where(lane < ROT_DIM // 2, -sn, 0.0)
    s_hi = jnp.where((lane >= ROT_DIM // 2) & (lane < ROT_DIM), sn, 0.0)
    bd = bd_ref[...]

    def norm_rope(col0, width, out_ref, scr=None):
        for j in range(width // LANES):
            c = col0 + j * LANES
            t = proj[:, c:c + LANES]
            sq = t * t
            hi = sq.astype(BF16)
            lo = (sq - hi.astype(F32)).astype(BF16)
            ss = (jnp.dot(hi, bd, preferred_element_type=F32)
                  + jnp.dot(lo, bd, preferred_element_type=F32))
            t = t * lax.rsqrt(ss * (1.0 / HEAD_DIM) + EPS) * gcol_ref[:, c:c + LANES]
            t = (t * cs + pltpu.roll(t, LANES - ROT_DIM // 2, 1) * s_lo
                 + pltpu.roll(t, ROT_DIM // 2, 1) * s_hi)
            out_ref[:, j * LANES:(j + 1) * LANES] = t.astype(BF16)
            if scr is not None:
                scr[j] = t

    norm_rope(COL_QA, MIX_A, qa_ref)
    norm_rope(COL_KA, KV_A, ka_ref)
    norm_rope(COL_QB, MIX_B, qb_ref, qb_scr)
    norm_rope(COL_KB, MIX_B, kb_ref, kb_scr)
    va_ref[...] = proj[:, COL_VA:COL_VA + KV_A].astype(BF16)
    vb_ref[...] = proj[:, COL_VB:COL_VB + MIX_B].astype(BF16)
    for j in range(N_PAIRS):
        vb_scr[j] = proj[:, COL_VB + j * LANES:COL_VB + (j + 1) * LANES]
    for scr, outs in zip((qb_scr, kb_scr, vb_scr), dil_refs):
        for dil, out in zip(DILS, outs):
            for r in range(dil):
                for j in range(N_PAIRS):
                    c = r * MIX_B + j * LANES
                    out[:, c:c + LANES] = scr[j, pl.ds(r, tt // dil, stride=dil), :].astype(BF16)


def _inproj(xf, mod, pos, g_norm, w_in_p, gcol, invf, bd, seq):
    n, d = xf.shape
    tt = TT_PROJ
    tiles_per_seq = seq // tt
    shapes = [(n, w) for w in (MIX_A, KV_A, KV_A, MIX_B, MIX_B, MIX_B)]
    shapes += [(n // dil, dil * MIX_B) for _ in range(3) for dil in DILS]
    full = lambda shape: pl.BlockSpec(shape, lambda i: (0,) * len(shape))
    return pl.pallas_call(
        _inproj_kernel,
        grid=(n // tt,),
        in_specs=[pl.BlockSpec((tt, d), lambda i: (i, 0)),
                  pl.BlockSpec((None, 6, d), lambda i: (i // tiles_per_seq, 0, 0)),
                  pl.BlockSpec((tt, 1), lambda i: (i, 0)),
                  full((1, d)), full((d, IN_WIDTH)), full((1, IN_WIDTH)),
                  full((1, LANES)), full((LANES, LANES))],
        out_specs=[pl.BlockSpec((tt * r // n, w), lambda i: (i, 0)) for r, w in shapes],
        out_shape=[jax.ShapeDtypeStruct(s, BF16) for s in shapes],
        scratch_shapes=[pltpu.VMEM((N_PAIRS, tt, LANES), F32)] * 3,
        compiler_params=_params("parallel"),
        name="inproj",
    )(xf, mod, pos, g_norm, w_in_p, gcol, invf, bd)


def _attn_kernel(*refs, kv_shared, max_dist, use_prev, has_sinks, want_lse):
    refs = list(refs)
    sink_ref = refs.pop(0) if has_sinks else None
    q_ref = refs.pop(0)
    kp_ref = refs.pop(0) if use_prev else None
    kc_ref = refs.pop(0)
    vp_ref = refs.pop(0) if use_prev else None
    vc_ref = refs.pop(0)
    o_ref = refs.pop(0)
    lse_ref = refs.pop(0) if want_lse else None

    blk = pl.program_id(2)
    nq = 2 * BLOCK
    nk = 2 * BLOCK if use_prev else BLOCK
    qpos = lax.broadcasted_iota(I32, (nq, nk), 0) % BLOCK
    kpos = lax.broadcasted_iota(I32, (nq, nk), 1)
    if use_prev:
        dist = qpos + BLOCK - kpos
        valid = (dist >= 0) & (dist <= max_dist) & ((kpos >= BLOCK) | (blk > 0))
    else:
        dist = qpos - kpos
        valid = (dist >= 0) & (dist <= max_dist)
    lane = lax.broadcasted_iota(I32, (nq, LANES), 1)
    row = lax.broadcasted_iota(I32, (nq, LANES), 0)
    own_half = (lane < HEAD_DIM) == (row < BLOCK)
    left_lanes = lax.broadcasted_iota(I32, (BLOCK, LANES), 1) < HEAD_DIM
    lane8 = lax.broadcasted_iota(I32, (BLOCK, 2 * N_PAIRS), 1)
    lse_blk = jnp.zeros((BLOCK, 2 * N_PAIRS), F32)

    for p in range(N_PAIRS):
        cq = slice(p * LANES, (p + 1) * LANES)
        ck = slice(0, LANES) if kv_shared else cq
        qp = q_ref[:, cq]
        qs = jnp.concatenate([qp, qp], axis=0)
        qs = jnp.where(own_half, qs, jnp.zeros_like(qs))
        if use_prev:
            k = jnp.concatenate([kp_ref[:, ck], kc_ref[:, ck]], axis=0)
            v = jnp.concatenate([vp_ref[:, ck], vc_ref[:, ck]], axis=0)
        else:
            k, v = kc_ref[:, ck], vc_ref[:, ck]
        s = lax.dot_general(qs, k, (((1,), (1,)), ((), ())), preferred_element_type=F32)
        s = jnp.where(valid, s, NEG)
        m = jnp.max(s, axis=-1, keepdims=True)
        if has_sinks:
            rows1 = lax.broadcasted_iota(I32, (nq, 1), 0)
            sink = jnp.where(rows1 < BLOCK, sink_ref[2 * p], sink_ref[2 * p + 1])
            m = jnp.maximum(m, sink)
        e = jnp.exp(s - m)
        l = jnp.sum(e, axis=-1, keepdims=True)
        if has_sinks:
            l = l + jnp.exp(sink - m)
        o = jnp.dot(e.astype(BF16), v, preferred_element_type=F32) / l
        o_ref[:, cq] = jnp.where(left_lanes, o[:BLOCK], o[BLOCK:]).astype(BF16)
        if want_lse:
            lse = m + jnp.log(l)
            lse_blk = (lse_blk + jnp.where(lane8 == 2 * p, lse[:BLOCK], 0.0)
                       + jnp.where(lane8 == 2 * p + 1, lse[BLOCK:], 0.0))
    if want_lse:
        lse_ref[...] = lse_blk


def _attention(q, k, v, *, nbatch, seq, dil, max_dist, kv_shared, sinks=None, want_lse):
    length = seq // dil
    nblk = length // BLOCK
    use_prev = nblk > 1
    kw = k.shape[1] // dil
    view = lambda t: t.reshape(nbatch, length, t.shape[1])
    cur = lambda b, r, i: (b, i, r)
    prev = lambda b, r, i: (b, jnp.maximum(i - 1, 0), r)
    in_specs, args = [], []
    if sinks is not None:
        in_specs.append(pl.BlockSpec(memory_space=pltpu.SMEM))
        args.append(sinks)
    in_specs.append(pl.BlockSpec((None, BLOCK, MIX_B), cur))
    args.append(view(q))
    for t in (k, v):
        if use_prev:
            in_specs.append(pl.BlockSpec((None, BLOCK, kw), prev))
            args.append(view(t))
        in_specs.append(pl.BlockSpec((None, BLOCK, kw), cur))
        args.append(view(t))
    out_specs = [pl.BlockSpec((None, BLOCK, MIX_B), cur)]
    out_shape = [jax.ShapeDtypeStruct((nbatch, length, dil * MIX_B), BF16)]
    if want_lse:
        out_specs.append(pl.BlockSpec((None, None, BLOCK, N_HEADS_B), lambda b, r, i: (b, r, i, 0)))
        out_shape.append(jax.ShapeDtypeStruct((nbatch, dil, length, N_HEADS_B), F32))
    outs = pl.pallas_call(
        functools.partial(_attn_kernel, kv_shared=kv_shared, max_dist=max_dist, use_prev=use_prev,
                          has_sinks=sinks is not None, want_lse=want_lse),
        grid=(nbatch, dil, nblk),
        in_specs=in_specs, out_specs=out_specs, out_shape=out_shape,
        compiler_params=_params("parallel", "parallel", "arbitrary"),
        name=f"attn_d{dil}" + ("_swa" if kv_shared else ""),
    )(*args)
    o = outs[0].reshape(nbatch * length, dil * MIX_B)
    if not want_lse:
        return o, None
    lse = outs[1].transpose(0, 2, 1, 3).reshape(nbatch * seq, N_HEADS_B)
    return o, lse


def _expand_heads(w, width):
    head = lax.broadcasted_iota(I32, (1, width), 1) // HEAD_DIM
    out = jnp.zeros((w.shape[0], width), F32)
    for hd in range(w.shape[1]):
        out = jnp.where(head == hd, w[:, hd:hd + 1], out)
    return out


def _outproj_kernel(x_ref, mod_ref, oa_ref, ob1_ref, ob2_ref, ob3_ref, l1_ref, l2_ref, l3_ref,
                    goa_ref, gob_ref, wo_ref, gf_ref, wgs_ref, wus_ref, wds_ref,
                    h2_ref, h2t_ref, base_ref, ob_scr):
    tt = x_ref.shape[0]
    gate_a = mod_ref[2:3, :]
    shift_m, scale_m, gate_m = mod_ref[3:4, :], mod_ref[4:5, :], mod_ref[5:6, :]

    def token_major(ref, dil):
        if dil == 1:
            return ref[...].astype(F32)
        for r in range(dil):
            for j in range(N_PAIRS):
                c = r * MIX_B + j * LANES
                ob_scr[j, pl.ds(r, tt // dil, stride=dil), :] = ref[:, c:c + LANES].astype(F32)
        return jnp.concatenate([ob_scr[j] for j in range(N_PAIRS)], axis=1)

    l1, l2, l3 = l1_ref[...], l2_ref[...], l3_ref[...]
    mx = jnp.maximum(jnp.maximum(l1, l2), l3)
    e1, e2, e3 = jnp.exp(l1 - mx), jnp.exp(l2 - mx), jnp.exp(l3 - mx)
    den = e1 + e2 + e3
    dils = [dil for _, dil in DILATED_BRANCHES]
    ob = _expand_heads(e1 / den, MIX_B) * token_major(ob1_ref, dils[0])
    ob = ob + _expand_heads(e2 / den, MIX_B) * token_major(ob2_ref, dils[1])
    ob = ob + _expand_heads(e3 / den, MIX_B) * token_major(ob3_ref, dils[2])
    ob = _rms_rows(ob) * gob_ref[...]
    oa = _rms_rows(oa_ref[...].astype(F32)) * goa_ref[...]
    y = (jnp.dot(oa.astype(BF16), wo_ref[0:MIX_A, :], preferred_element_type=F32)
         + jnp.dot(ob.astype(BF16), wo_ref[MIX_A:MIX_A + MIX_B, :], preferred_element_type=F32))
    x1 = x_ref[...] + gate_a * y
    h2 = _rms_rows(x1) * gf_ref[...] * (1.0 + scale_m) + shift_m
    h2_ref[...] = h2
    _store_row_tiles(h2t_ref, h2)
    hb = h2.astype(BF16)
    act = (_silu(jnp.dot(hb, wgs_ref[...], preferred_element_type=F32))
           * jnp.dot(hb, wus_ref[...], preferred_element_type=F32))
    shared = jnp.dot(act.astype(BF16), wds_ref[...], preferred_element_type=F32)
    base_ref[...] = x1 + gate_m * shared


def _outproj(xf, mod, oa, obs, lses, goa, gob, wo_p, gf, wgs, wus, wds, seq):
    n, d = xf.shape
    tt = TT_PROJ // 2
    tiles_per_seq = seq // tt
    tile = lambda w: pl.BlockSpec((tt, w), lambda i: (i, 0))
    full = lambda shape: pl.BlockSpec(shape, lambda i: (0,) * len(shape))
    sd = wgs.shape[1]
    dilated = [pl.BlockSpec((tt // dil, dil * MIX_B), lambda i: (i, 0)) for _, dil in DILATED_BRANCHES]
    return pl.pallas_call(
        _outproj_kernel,
        grid=(n // tt,),
        in_specs=[tile(d), pl.BlockSpec((None, 6, d), lambda i: (i // tiles_per_seq, 0, 0)),
                  tile(MIX_A), *dilated,
                  tile(N_HEADS_B), tile(N_HEADS_B), tile(N_HEADS_B),
                  full((1, MIX_A)), full((1, MIX_B)), full((MIX_A + MIX_B, d)), full((1, d)),
                  full((d, sd)), full((d, sd)), full((sd, d))],
        out_specs=[tile(d), pl.BlockSpec((tt * d // LANES, LANES), lambda i: (i, 0)), tile(d)],
        out_shape=[jax.ShapeDtypeStruct((n, d), F32), jax.ShapeDtypeStruct((n * d // LANES, LANES), F32),
                   jax.ShapeDtypeStruct((n, d), F32)],
        scratch_shapes=[pltpu.VMEM((N_PAIRS, tt, LANES), F32)],
        compiler_params=_params("parallel"),
        name="outproj",
    )(xf, mod, oa, *obs, *lses, goa, gob, wo_p, gf, wgs, wus, wds)


def _router_kernel(h_ref, wrt_ref, bias_ref, e_ref, g_ref, cnt_ref):
    tt = h_ref.shape[0]
    logits = lax.dot_general(wrt_ref[...], h_ref[...].astype(BF16), (((1,), (1,)), ((), ())),
                             preferred_element_type=F32)
    scores = 1.0 / (1.0 + jnp.exp(-logits))
    biased = scores + bias_ref[...]
    ninf = -jnp.inf

    j32 = lax.broadcasted_iota(I32, (GROUP_SIZE, tt), 0).astype(F32)
    grp = []
    for g in range(N_GROUPS):
        bg = biased[g * GROUP_SIZE:(g + 1) * GROUP_SIZE, :]
        m1 = jnp.max(bg, axis=0, keepdims=True)
        i1 = jnp.min(jnp.where(bg == m1, j32, float(GROUP_SIZE)), axis=0, keepdims=True)
        m2 = jnp.max(jnp.where(j32 == i1, ninf, bg), axis=0, keepdims=True)
        grp.append(m1 + m2)
    grp = jnp.concatenate(grp, axis=0)
    g8 = lax.broadcasted_iota(I32, (N_GROUPS, tt), 0).astype(F32)
    chosen = jnp.zeros((N_GROUPS, tt), F32)
    for _ in range(TOPK_GROUPS):
        gm = jnp.max(grp, axis=0, keepdims=True)
        gi = jnp.min(jnp.where(grp == gm, g8, float(N_GROUPS)), axis=0, keepdims=True)
        hit = g8 == gi
        chosen = jnp.where(hit, 1.0, chosen)
        grp = jnp.where(hit, ninf, grp)
    masked = jnp.concatenate(
        [jnp.where(chosen[g:g + 1, :] > 0.0, biased[g * GROUP_SIZE:(g + 1) * GROUP_SIZE, :], ninf)
         for g in range(N_GROUPS)], axis=0)

    eio = lax.broadcasted_iota(I32, (N_EXPERTS, tt), 0).astype(F32)
    picked = jnp.zeros((N_EXPERTS, tt), F32)
    es, gs = [], []
    for _ in range(TOP_K):
        m = jnp.max(masked, axis=0, keepdims=True)
        idx = jnp.min(jnp.where(masked == m, eio, float(N_EXPERTS)), axis=0, keepdims=True)
        hit = eio == idx
        gs.append(jnp.sum(jnp.where(hit, scores, 0.0), axis=0, keepdims=True))
        es.append(idx)
        picked = jnp.where(hit, 1.0, picked)
        masked = jnp.where(hit, ninf, masked)
    gates = jnp.concatenate(gs, axis=0)
    e_ref[...] = jnp.concatenate(es, axis=0).astype(I32)
    g_ref[...] = gates / jnp.sum(gates, axis=0, keepdims=True) * ROUTED_SCALE

    @pl.when(pl.program_id(0) == 0)
    def _():
        cnt_ref[...] = jnp.zeros_like(cnt_ref)
    cnt_ref[...] += jnp.sum(picked, axis=1, keepdims=True)


def _router(h2, wrt, bias_col):
    n, d = h2.shape
    tt = TT_ROUTE
    return pl.pallas_call(
        _router_kernel,
        grid=(n // tt,),
        in_specs=[pl.BlockSpec((tt, d), lambda i: (i, 0)),
                  pl.BlockSpec((N_EXPERTS, d), lambda i: (0, 0)),
                  pl.BlockSpec((N_EXPERTS, 1), lambda i: (0, 0))],
        out_specs=[pl.BlockSpec((TOP_K, tt), lambda i: (0, i)),
                   pl.BlockSpec((TOP_K, tt), lambda i: (0, i)),
                   pl.BlockSpec((N_EXPERTS, 1), lambda i: (0, 0))],
        out_shape=[jax.ShapeDtypeStruct((TOP_K, n), I32),
                   jax.ShapeDtypeStruct((TOP_K, n), F32),
                   jax.ShapeDtypeStruct((N_EXPERTS, 1), F32)],
        compiler_params=_params("arbitrary"),
        name="router",
    )(h2, wrt, bias_col)


def _rank_kernel(e_ref, pstart_ref, tri_ref, dest_ref, carry_ref):
    tt = e_ref.shape[1]

    @pl.when(pl.program_id(0) == 0)
    def _():
        carry_ref[...] = pstart_ref[...]

    e = e_ref[...]
    eio = lax.broadcasted_iota(I32, (N_EXPERTS, tt), 0)
    mask = jnp.zeros((N_EXPERTS, tt), F32)
    for k in range(TOP_K):
        mask = jnp.where(eio == e[k:k + 1, :], 1.0, mask)
    incl = jnp.dot(mask.astype(BF16), tri_ref[...], preferred_element_type=F32)
    pos = incl - mask + carry_ref[...]
    dest = [jnp.sum(jnp.where(eio == e[k:k + 1, :], pos, 0.0), axis=0, keepdims=True)
            for k in range(TOP_K)]
    dest_ref[...] = jnp.concatenate(dest, axis=0).astype(I32)
    carry_ref[...] += incl[:, tt - 1:tt]


def _rank(top_e_t, pstart_col, tri):
    n = top_e_t.shape[1]
    tt = TT_ROUTE
    return pl.pallas_call(
        _rank_kernel,
        grid=(n // tt,),
        in_specs=[pl.BlockSpec((TOP_K, tt), lambda i: (0, i)),
                  pl.BlockSpec((N_EXPERTS, 1), lambda i: (0, 0)),
                  pl.BlockSpec((tt, tt), lambda i: (0, 0))],
        out_specs=pl.BlockSpec((TOP_K, tt), lambda i: (0, i)),
        out_shape=jax.ShapeDtypeStruct((TOP_K, n), I32),
        scratch_shapes=[pltpu.VMEM((N_EXPERTS, 1), F32)],
        compiler_params=_params("arbitrary"),
        name="rank",
    )(top_e_t, pstart_col, tri)


def _row_copy(src_ref, src_row, dst_ref, dst_row, sem):
    return pltpu.make_async_copy(src_ref.at[pl.ds(src_row, 1)], dst_ref.at[pl.ds(dst_row, 1)], sem)


SC_CORES, SC_SUBCORES, SC_LANES = 2, 16, 16
SC_WORKERS = SC_CORES * SC_SUBCORES
INVERT_CHUNK = 8192


def _invert(dest_flat, n_tokens, rows):
    per = rows // SC_WORKERS
    n_assign = dest_flat.shape[0]
    assert rows % (SC_WORKERS * SC_LANES) == 0 and n_assign % INVERT_CHUNK == 0
    assert n_tokens & (n_tokens - 1) == 0

    @functools.partial(
        pl.kernel, mesh=plsc.VectorSubcoreMesh(core_axis_name="c", subcore_axis_name="s"),
        out_type=jax.ShapeDtypeStruct((rows,), I32),
        scratch_types=[pltpu.VMEM((INVERT_CHUNK,), I32), pltpu.VMEM((per,), I32)],
        compiler_params=pltpu.CompilerParams(needs_layout_passes=False))
    def invert(dest_hbm, out_hbm, staged, local):
        base = (lax.axis_index("s") * SC_CORES + lax.axis_index("c")) * per
        lane = lax.iota(I32, SC_LANES)

        @pl.loop(0, per, step=SC_LANES)
        def _(i):
            local[pl.ds(i, SC_LANES)] = (base + i + lane) & (n_tokens - 1)

        @pl.loop(0, n_assign // INVERT_CHUNK)
        def _(c):
            pltpu.sync_copy(dest_hbm.at[pl.ds(c * INVERT_CHUNK, INVERT_CHUNK)], staged)

            @pl.loop(0, INVERT_CHUNK, step=SC_LANES)
            def _(i):
                rel = staged[pl.ds(i, SC_LANES)] - base
                mine = (rel >= 0) & (rel < per)
                tok = (c * INVERT_CHUNK + i + lane) & (n_tokens - 1)
                plsc.store_scatter(local, [jnp.where(mine, rel, 0)], tok, mask=mine)

        pltpu.sync_copy(local, out_hbm.at[pl.ds(base, per)])

    return invert(dest_flat)


def _experts_kernel(be_ref, nv_ref, tok_hbm, h_ref, wg_ref, wu_ref, wd_ref, ys_ref,
                    xring, wgb, wub, wdb, tok_s, sem, tok_sem):
    j = pl.program_id(0)
    nv = nv_ref[0]
    per = wgb.shape[0] // LANES
    depth, bm = xring.shape[0], xring.shape[1] // per
    ahead = depth - 1
    chunk = tok_s.shape[1]
    cb = chunk // bm
    n_chunks = tok_hbm.shape[0] // chunk
    n_blocks = n_chunks * cb

    def tok_copy(c):
        return pltpu.make_async_copy(tok_hbm.at[pl.ds(pl.multiple_of(c * chunk, chunk), chunk)],
                                     tok_s.at[c % 2], tok_sem.at[c % 2])

    def gather(block, unrolled):
        ring = block % depth
        block = jnp.minimum(block, n_blocks - 1)
        slot, off = (block // cb) % 2, (block % cb) * bm
        buf, buf_sem = xring.at[ring], sem.at[ring]

        def issue(i, carry=None):
            _tile_copy(h_ref, tok_s[slot, off + i], buf, i, per, buf_sem).start()
            return carry

        if unrolled:
            for i in range(bm):
                issue(i)
        else:
            lax.fori_loop(0, bm, issue, 0, unroll=8)

    def wait_rows(ring):
        pltpu.make_async_copy(h_ref.at[pl.ds(0, bm * per)], xring.at[ring], sem.at[ring]).wait()

    @pl.when(j == 0)
    def _():
        tok_copy(0).start()
        tok_copy(0).wait()
        tok_copy(1).start()
        for b in range(ahead):
            gather(b, unrolled=False)

    first = j + ahead
    c_need = first // cb

    @pl.when(jnp.logical_and(jnp.logical_and(j > 0, first % cb == 0), c_need < n_chunks))
    def _():
        tok_copy(c_need).wait()

        @pl.when(c_need + 1 < n_chunks)
        def _():
            tok_copy(c_need + 1).start()

    jb = jnp.minimum(j, n_blocks - 1)
    fresh = jnp.logical_or(j == 0, be_ref[jb] != be_ref[jnp.maximum(jb - 1, 0)])

    @pl.when(jnp.logical_and(j < nv, fresh))
    def _():
        wgb[...] = wg_ref[...].astype(BF16)
        wub[...] = wu_ref[...].astype(BF16)
        wdb[...] = wd_ref[...].astype(BF16)

    @pl.when(j < nv)
    def _():
        wait_rows(j % depth)
        gather(j + ahead, unrolled=True)
        xb = _load_row_tiles(xring.at[j % depth], bm, per, BF16)
        act = (_silu(jnp.dot(xb, wgb[...], preferred_element_type=F32))
               * jnp.dot(xb, wub[...], preferred_element_type=F32))
        _store_row_tiles(ys_ref, jnp.dot(act.astype(BF16), wdb[...], preferred_element_type=F32))

    @pl.when(jnp.logical_and(j >= nv, j < nv + ahead))
    def _():
        wait_rows(j % depth)


def _experts(block_e, n_valid, row_tok, h2_tiles, wg, wu, wd):
    rows = row_tok.shape[0]
    d = wg.shape[1]
    per = d // LANES
    bm = EXPERT_ROWS
    n_blocks = rows // bm
    assert n_blocks % TOK_CHUNK_BLOCKS == 0 and n_blocks // TOK_CHUNK_BLOCKS >= 2
    assert GATHER_DEPTH - 1 < TOK_CHUNK_BLOCKS
    f = wg.shape[2]
    row_blk = lambda j, be, nv: (jnp.minimum(j, nv[0] - 1), 0)
    w_blk = lambda j, be, nv: (be[jnp.minimum(j, n_blocks - 1)], 0, 0)
    return pl.pallas_call(
        _experts_kernel,
        grid_spec=pltpu.PrefetchScalarGridSpec(
            num_scalar_prefetch=2,
            grid=(n_blocks + GATHER_DEPTH,),
            in_specs=[pl.BlockSpec(memory_space=pl.ANY),
                      pl.BlockSpec(memory_space=pl.ANY),
                      pl.BlockSpec((None, d, f), w_blk),
                      pl.BlockSpec((None, d, f), w_blk),
                      pl.BlockSpec((None, f, d), w_blk)],
            out_specs=pl.BlockSpec((bm * per, LANES), row_blk),
            scratch_shapes=[pltpu.VMEM((GATHER_DEPTH, bm * per, LANES), F32),
                            pltpu.VMEM((d, f), BF16), pltpu.VMEM((d, f), BF16),
                            pltpu.VMEM((f, d), BF16),
                            pltpu.SMEM((2, TOK_CHUNK_BLOCKS * bm), I32),
                            pltpu.SemaphoreType.DMA((GATHER_DEPTH,)), pltpu.SemaphoreType.DMA((2,))]),
        out_shape=jax.ShapeDtypeStruct((rows * per, LANES), F32),
        compiler_params=_params("arbitrary"),
        name="experts",
    )(block_e, n_valid, row_tok, h2_tiles, wg, wu, wd)


def _combine_kernel(dest_ref, gates_ref, base_ref, mod_ref, ys_ref, out_ref, buf, sem):
    tt, d = base_ref.shape
    per = d // LANES
    for t in range(tt):
        for k in range(TOP_K):
            _tile_copy(ys_ref, dest_ref[k, t], buf.at[k], t, per, sem).start()
    for k in range(TOP_K):
        pltpu.make_async_copy(ys_ref.at[pl.ds(0, tt * per)], buf.at[k], sem).wait()
    gates = gates_ref[...]
    routed = gates[:, 0:1] * _load_row_tiles(buf.at[0], tt, per, F32)
    for k in range(1, TOP_K):
        routed = routed + gates[:, k:k + 1] * _load_row_tiles(buf.at[k], tt, per, F32)
    out_ref[...] = base_ref[...] + mod_ref[5:6, :] * routed


def _combine(dest_t, gates, base, mod, ys, seq):
    n, d = base.shape
    tt = TT_COMBINE
    tiles_per_seq = seq // tt
    return pl.pallas_call(
        _combine_kernel,
        grid=(n // tt,),
        in_specs=[pl.BlockSpec((TOP_K, tt), lambda i: (0, i), memory_space=pltpu.SMEM),
                  pl.BlockSpec((tt, TOP_K), lambda i: (i, 0)),
                  pl.BlockSpec((tt, d), lambda i: (i, 0)),
                  pl.BlockSpec((None, 6, d), lambda i: (i // tiles_per_seq, 0, 0)),
                  pl.BlockSpec(memory_space=pl.ANY)],
        out_specs=pl.BlockSpec((tt, d), lambda i: (i, 0)),
        out_shape=jax.ShapeDtypeStruct((n, d), F32),
        scratch_shapes=[pltpu.VMEM((TOP_K, tt * d // LANES, LANES), F32), pltpu.SemaphoreType.DMA],
        compiler_params=_params("arbitrary"),
        name="combine",
    )(dest_t, gates, base, mod, ys)


def _layer(x, mod, pos, rope, p):
    nbatch, seq, d = x.shape
    n = nbatch * seq
    xf = x.reshape(n, d)
    invf, bd = rope

    perm = np.concatenate([np.arange(h * HEAD_DIM, (h + 1) * HEAD_DIM) for h in PAIR_ORDER_A])
    w_in = p["w_in"]
    w_in_p = jnp.concatenate([w_in[:, :MIX_A][:, perm], w_in[:, MIX_A:]], axis=1).astype(BF16)
    ones = lambda w: jnp.ones((w,), F32)
    qscale = HEAD_DIM ** -0.5
    gcol = jnp.concatenate([jnp.tile(p["g_q_a"], N_HEADS_A) * qscale, jnp.tile(p["g_k_a"], N_KV_A),
                            ones(KV_A), jnp.tile(p["g_q_b"], N_HEADS_B) * qscale,
                            jnp.tile(p["g_k_b"], N_HEADS_B), ones(MIX_B)]).reshape(1, IN_WIDTH)
    proj = _inproj(xf, mod, pos, p["g_norm_mix"].reshape(1, d), w_in_p, gcol, invf, bd, seq)
    qa, ka, va = proj[:3]
    qkv_b = {1: proj[3:6]}
    for j, dil in enumerate(DILS):
        qkv_b[dil] = [proj[6 + t * len(DILS) + j] for t in range(3)]

    sinks_p = p["sinks_a"][np.array(PAIR_ORDER_A)]
    oa, _ = _attention(qa, ka, va, nbatch=nbatch, seq=seq, dil=1, max_dist=WINDOW_A - 1,
                       kv_shared=True, sinks=sinks_p, want_lse=False)
    obs, lses = [], []
    for window, dil in DILATED_BRANCHES:
        o, lse = _attention(*qkv_b[dil], nbatch=nbatch, seq=seq, dil=dil, max_dist=window // dil,
                            kv_shared=False, want_lse=True)
        obs.append(o)
        lses.append(lse)

    goa = p["g_out_a"][perm].reshape(1, MIX_A)
    w_out = p["w_out"]
    wo_p = jnp.concatenate([w_out[:MIX_A][perm], w_out[MIX_A:]], axis=0).astype(BF16)
    h2, h2_tiles, base = _outproj(xf, mod, oa, obs, lses, goa, p["g_out_b"].reshape(1, MIX_B), wo_p,
                        p["g_norm_ffn"].reshape(1, d), p["w_gate_s"].astype(BF16),
                        p["w_up_s"].astype(BF16), p["w_down_s"].astype(BF16), seq)

    top_e_t, gates_t, counts = _router(h2, p["w_router"].T.astype(BF16),
                                       p["router_bias"].reshape(N_EXPERTS, 1))
    bm = EXPERT_ROWS
    counts = counts.reshape(N_EXPERTS).astype(I32)
    padded = (counts + bm - 1) // bm * bm
    pends = jnp.cumsum(padded)
    pstarts = pends - padded
    rows = n * TOP_K + N_EXPERTS * bm
    n_blocks = rows // bm
    n_valid = (pends[-1] // bm).astype(I32)
    blk = jnp.minimum(jnp.arange(n_blocks, dtype=I32), n_valid - 1)
    block_e = jnp.sum((pends[None, :] <= (blk * bm)[:, None]).astype(I32), axis=1)
    block_e = jnp.minimum(block_e, N_EXPERTS - 1)

    tri = (np.arange(TT_ROUTE)[:, None] <= np.arange(TT_ROUTE)[None, :])
    dest_t = _rank(top_e_t, pstarts.astype(F32).reshape(N_EXPERTS, 1), jnp.asarray(tri, BF16))
    n_valid = n_valid.reshape(1)
    row_tok = _invert(dest_t.reshape(n * TOP_K), n, rows)
    ys = _experts(block_e, n_valid, row_tok, h2_tiles, p["w_gate_e"], p["w_up_e"], p["w_down_e"])
    out = _combine(dest_t, gates_t.T, base, mod, ys, seq)
    return out.reshape(nbatch, seq, d)


def kernel(x, c, positions, w_ada, b_ada, g_norm_mix, w_in, g_q_a, g_k_a, sinks_a, g_q_b, g_k_b,
           g_out_a, g_out_b, w_out, g_norm_ffn, w_router, router_bias, w_gate_e, w_up_e, w_down_e,
           w_gate_s, w_up_s, w_down_s):
    nbatch, seq, d = x.shape
    depth = w_ada.shape[0]
    params = dict(g_norm_mix=g_norm_mix, w_in=w_in, g_q_a=g_q_a, g_k_a=g_k_a, sinks_a=sinks_a,
                  g_q_b=g_q_b, g_k_b=g_k_b, g_out_a=g_out_a, g_out_b=g_out_b, w_out=w_out,
                  g_norm_ffn=g_norm_ffn, w_router=w_router, router_bias=router_bias,
                  w_gate_e=w_gate_e, w_up_e=w_up_e, w_down_e=w_down_e, w_gate_s=w_gate_s,
                  w_up_s=w_up_s, w_down_s=w_down_s)
    j = np.arange(LANES) % HEAD_DIM
    inv = ROPE_THETA ** (-jnp.arange(0, ROT_DIM, 2, dtype=F32) / ROT_DIM)
    invf = jnp.where(j < ROT_DIM, inv[j % (ROT_DIM // 2)], 0.0).astype(F32).reshape(1, LANES)
    bd = jnp.asarray((np.arange(LANES)[:, None] // HEAD_DIM) == (np.arange(LANES)[None, :] // HEAD_DIM),
                     BF16)
    pos = positions.reshape(nbatch * seq, 1).astype(I32)
    for l in range(depth):
        mod = _adaln(c.astype(F32), w_ada[l], b_ada[l]).reshape(nbatch, 6, d)
        x = _layer(x, mod, pos, (invf, bd), {k: v[l] for k, v in params.items()})
    return x
```

```python
import functools

import numpy as np
import jax
import jax.numpy as jnp
from jax import lax
from jax.experimental import pallas as pl
from jax.experimental.pallas import tpu as pltpu
from jax.experimental.pallas import tpu_sc as plsc

F32 = jnp.float32
BF16 = jnp.bfloat16
I32 = jnp.int32

HEAD_DIM = 64
N_HEADS_A = 8
N_KV_A = 2
WINDOW_A = 128
N_HEADS_B = 8
DILATED_BRANCHES = ((128, 1), (512, 4), (2048, 16))
DILS = tuple(dil for _, dil in DILATED_BRANCHES if dil > 1)
BLOCK = 128
ROT_DIM = HEAD_DIM // 4
ROPE_THETA = 500000.0
MIX_A = N_HEADS_A * HEAD_DIM
KV_A = N_KV_A * HEAD_DIM
MIX_B = N_HEADS_B * HEAD_DIM
N_EXPERTS = 256
TOP_K = 8
N_GROUPS = 8
TOPK_GROUPS = 4
GROUP_SIZE = N_EXPERTS // N_GROUPS
ROUTED_SCALE = 2.5
EPS = 1e-6

LANES = 128
HEADS_PER_VREG = LANES // HEAD_DIM
N_PAIRS = MIX_A // LANES
NEG = -1e30
VMEM_LIMIT = 48 * 1024 * 1024

TT_PROJ = 512
TT_ROUTE = 256
TT_DISPATCH = 256
TT_COMBINE = 128
EXPERT_ROWS = 256
TOK_CHUNK_BLOCKS = 32
GATHER_DEPTH = 4
SUBLANES = 8

PAIR_ORDER_A = tuple(h for p in range(N_PAIRS) for h in (p, p + N_HEADS_A // N_KV_A))


def _params(*sem):
    return pltpu.CompilerParams(dimension_semantics=sem, vmem_limit_bytes=VMEM_LIMIT)


def _silu(t):
    return t / (1.0 + jnp.exp(-t))


def _rms_rows(t):
    return t * lax.rsqrt(jnp.mean(t * t, axis=-1, keepdims=True) + EPS)


def _store_row_tiles(ref, value):
    rows, d = value.shape
    per = d // LANES
    for s in range(per):
        ref[pl.ds(s, rows, stride=per), :] = value[:, s * LANES:(s + 1) * LANES]


def _load_row_tiles(ref, rows, per, dtype):
    return jnp.concatenate([ref[pl.ds(s, rows, stride=per), :].astype(dtype) for s in range(per)], axis=1)


def _tile_copy(src_ref, src_row, dst_ref, dst_row, per, sem):
    start = lambda row: row * per if isinstance(row, int) else pl.multiple_of(row * per, per)
    return pltpu.make_async_copy(src_ref.at[pl.ds(start(src_row), per)],
                                 dst_ref.at[pl.ds(start(dst_row), per)], sem)


def _ada_kernel(c_ref, w_ref, b_ref, o_ref):
    cond = _silu(c_ref[...])
    o_ref[...] = jnp.dot(cond.astype(BF16), w_ref[...].astype(BF16),
                         preferred_element_type=F32) + b_ref[...]


def _adaln(c, w_ada, b_ada):
    nb, d = c.shape
    width = w_ada.shape[1]
    tn = 1024
    return pl.pallas_call(
        _ada_kernel,
        grid=(width // tn,),
        in_specs=[pl.BlockSpec((nb, d), lambda j: (0, 0)),
                  pl.BlockSpec((d, tn), lambda j: (0, j)),
                  pl.BlockSpec((1, tn), lambda j: (0, j))],
        out_specs=pl.BlockSpec((nb, tn), lambda j: (0, j)),
        out_shape=jax.ShapeDtypeStruct((nb, width), F32),
        compiler_params=_params("arbitrary"),
        name="adaln",
    )(c, w_ada, b_ada.reshape(1, width))


COL_QA, COL_KA, COL_VA = 0, MIX_A, MIX_A + KV_A
COL_QB = MIX_A + 2 * KV_A
COL_KB, COL_VB = COL_QB + MIX_B, COL_QB + 2 * MIX_B
IN_WIDTH = COL_VB + MIX_B


def _inproj_kernel(x_ref, mod_ref, pos_ref, gn_ref, w_ref, gcol_ref, invf_ref, bd_ref,
                   qa_ref, ka_ref, va_ref, qb_ref, kb_ref, vb_ref, *rest):
    n_dil = len(DILS)
    dil_refs = [rest[i * n_dil:(i + 1) * n_dil] for i in range(3)]
    qb_scr, kb_scr, vb_scr = rest[3 * n_dil:]
    tt = x_ref.shape[0]
    shift, scale = mod_ref[0:1, :], mod_ref[1:2, :]
    h = _rms_rows(x_ref[...]) * gn_ref[...] * (1.0 + scale) + shift
    proj = jnp.dot(h.astype(BF16), w_ref[...], preferred_element_type=F32)

    ang = pos_ref[...].astype(F32) * invf_ref[...]
    cs, sn = jnp.cos(ang), jnp.sin(ang)
    lane = lax.broadcasted_iota(I32, (1, LANES), 1) % HEAD_DIM
    s_lo = jnp.where(lane < ROT_DIM // 2, -sn, 0.0)
    s_hi = jnp.where((lane >= ROT_DIM // 2) & (lane < ROT_DIM), sn, 0.0)
    bd = bd_ref[...]

    def norm_rope(col0, width, out_ref, scr=None):
        for j in range(width // LANES):
            c = col0 + j * LANES
            t = proj[:, c:c + LANES]
            sq = t * t
            hi = sq.astype(BF16)
            lo = (sq - hi.astype(F32)).astype(BF16)
            ss = (jnp.dot(hi, bd, preferred_element_type=F32)
                  + jnp.dot(lo, bd, preferred_element_type=F32))
            t = t * lax.rsqrt(ss * (1.0 / HEAD_DIM) + EPS) * gcol_ref[:, c:c + LANES]
            t = (t * cs + pltpu.roll(t, LANES - ROT_DIM // 2, 1) * s_lo
                 + pltpu.roll(t, ROT_DIM // 2, 1) * s_hi)
            out_ref[:, j * LANES:(j + 1) * LANES] = t.astype(BF16)
            if scr is not None:
                scr[j] = t

    norm_rope(COL_QA, MIX_A, qa_ref)
    norm_rope(COL_KA, KV_A, ka_ref)
    norm_rope(COL_QB, MIX_B, qb_ref, qb_scr)
    norm_rope(COL_KB, MIX_B, kb_ref, kb_scr)
    va_ref[...] = proj[:, COL_VA:COL_VA + KV_A].astype(BF16)
    vb_ref[...] = proj[:, COL_VB:COL_VB + MIX_B].astype(BF16)
    for j in range(N_PAIRS):
        vb_scr[j] = proj[:, COL_VB + j * LANES:COL_VB + (j + 1) * LANES]
    for scr, outs in zip((qb_scr, kb_scr, vb_scr), dil_refs):
        for dil, out in zip(DILS, outs):
            for r in range(dil):
                for j in range(N_PAIRS):
                    c = r * MIX_B + j * LANES
                    out[:, c:c + LANES] = scr[j, pl.ds(r, tt // dil, stride=dil), :].astype(BF16)


def _inproj(xf, mod, pos, g_norm, w_in_p, gcol, invf, bd, seq):
    n, d = xf.shape
    tt = TT_PROJ
    tiles_per_seq = seq // tt
    shapes = [(n, w) for w in (MIX_A, KV_A, KV_A, MIX_B, MIX_B, MIX_B)]
    shapes += [(n // dil, dil * MIX_B) for _ in range(3) for dil in DILS]
    full = lambda shape: pl.BlockSpec(shape, lambda i: (0,) * len(shape))
    return pl.pallas_call(
        _inproj_kernel,
        grid=(n // tt,),
        in_specs=[pl.BlockSpec((tt, d), lambda i: (i, 0)),
                  pl.BlockSpec((None, 6, d), lambda i: (i // tiles_per_seq, 0, 0)),
                  pl.BlockSpec((tt, 1), lambda i: (i, 0)),
                  full((1, d)), full((d, IN_WIDTH)), full((1, IN_WIDTH)),
                  full((1, LANES)), full((LANES, LANES))],
        out_specs=[pl.BlockSpec((tt * r // n, w), lambda i: (i, 0)) for r, w in shapes],
        out_shape=[jax.ShapeDtypeStruct(s, BF16) for s in shapes],
        scratch_shapes=[pltpu.VMEM((N_PAIRS, tt, LANES), F32)] * 3,
        compiler_params=_params("parallel"),
        name="inproj",
    )(xf, mod, pos, g_norm, w_in_p, gcol, invf, bd)


def _attn_kernel(*refs, kv_shared, max_dist, use_prev, has_sinks, want_lse):
    refs = list(refs)
    sink_ref = refs.pop(0) if has_sinks else None
    q_ref = refs.pop(0)
    kp_ref = refs.pop(0) if use_prev else None
    kc_ref = refs.pop(0)
    vp_ref = refs.pop(0) if use_prev else None
    vc_ref = refs.pop(0)
    o_ref = refs.pop(0)
    lse_ref = refs.pop(0) if want_lse else None

    blk = pl.program_id(2)
    nq = 2 * BLOCK
    nk = 2 * BLOCK if use_prev else BLOCK
    qpos = lax.broadcasted_iota(I32, (nq, nk), 0) % BLOCK
    kpos = lax.broadcasted_iota(I32, (nq, nk), 1)
    if use_prev:
        dist = qpos + BLOCK - kpos
        valid = (dist >= 0) & (dist <= max_dist) & ((kpos >= BLOCK) | (blk > 0))
    else:
        dist = qpos - kpos
        valid = (dist >= 0) & (dist <= max_dist)
    lane = lax.broadcasted_iota(I32, (nq, LANES), 1)
    row = lax.broadcasted_iota(I32, (nq, LANES), 0)
    own_half = (lane < HEAD_DIM) == (row < BLOCK)
    left_lanes = lax.broadcasted_iota(I32, (BLOCK, LANES), 1) < HEAD_DIM
    lane8 = lax.broadcasted_iota(I32, (BLOCK, 2 * N_PAIRS), 1)
    lse_blk = jnp.zeros((BLOCK, 2 * N_PAIRS), F32)

    for p in range(N_PAIRS):
        cq = slice(p * LANES, (p + 1) * LANES)
        ck = slice(0, LANES) if kv_shared else cq
        qp = q_ref[:, cq]
        qs = jnp.concatenate([qp, qp], axis=0)
        qs = jnp.where(own_half, qs, jnp.zeros_like(qs))
        if use_prev:
            k = jnp.concatenate([kp_ref[:, ck], kc_ref[:, ck]], axis=0)
            v = jnp.concatenate([vp_ref[:, ck], vc_ref[:, ck]], axis=0)
        else:
            k, v = kc_ref[:, ck], vc_ref[:, ck]
        s = lax.dot_general(qs, k, (((1,), (1,)), ((), ())), preferred_element_type=F32)
        s = jnp.where(valid, s, NEG)
        m = jnp.max(s, axis=-1, keepdims=True)
        if has_sinks:
            rows1 = lax.broadcasted_iota(I32, (nq, 1), 0)
            sink = jnp.where(rows1 < BLOCK, sink_ref[2 * p], sink_ref[2 * p + 1])
            m = jnp.maximum(m, sink)
        e = jnp.exp(s - m)
        l = jnp.sum(e, axis=-1, keepdims=True)
        if has_sinks:
            l = l + jnp.exp(sink - m)
        o = jnp.dot(e.astype(BF16), v, preferred_element_type=F32) / l
        o_ref[:, cq] = jnp.where(left_lanes, o[:BLOCK], o[BLOCK:]).astype(BF16)
        if want_lse:
            lse = m + jnp.log(l)
            lse_blk = (lse_blk + jnp.where(lane8 == 2 * p, lse[:BLOCK], 0.0)
                       + jnp.where(lane8 == 2 * p + 1, lse[BLOCK:], 0.0))
    if want_lse:
        lse_ref[...] = lse_blk


def _attention(q, k, v, *, nbatch, seq, dil, max_dist, kv_shared, sinks=None, want_lse):
    length = seq // dil
    nblk = length // BLOCK
    use_prev = nblk > 1
    kw = k.shape[1] // dil
    view = lambda t: t.reshape(nbatch, length, t.shape[1])
    cur = lambda b, r, i: (b, i, r)
    prev = lambda b, r, i: (b, jnp.maximum(i - 1, 0), r)
    in_specs, args = [], []
    if sinks is not None:
        in_specs.append(pl.BlockSpec(memory_space=pltpu.SMEM))
        args.append(sinks)
    in_specs.append(pl.BlockSpec((None, BLOCK, MIX_B), cur))
    args.append(view(q))
    for t in (k, v):
        if use_prev:
            in_specs.append(pl.BlockSpec((None, BLOCK, kw), prev))
            args.append(view(t))
        in_specs.append(pl.BlockSpec((None, BLOCK, kw), cur))
        args.append(view(t))
    out_specs = [pl.BlockSpec((None, BLOCK, MIX_B), cur)]
    out_shape = [jax.ShapeDtypeStruct((nbatch, length, dil * MIX_B), BF16)]
    if want_lse:
        out_specs.append(pl.BlockSpec((None, None, BLOCK, N_HEADS_B), lambda b, r, i: (b, r, i, 0)))
        out_shape.append(jax.ShapeDtypeStruct((nbatch, dil, length, N_HEADS_B), F32))
    outs = pl.pallas_call(
        functools.partial(_attn_kernel, kv_shared=kv_shared, max_dist=max_dist, use_prev=use_prev,
                          has_sinks=sinks is not None, want_lse=want_lse),
        grid=(nbatch, dil, nblk),
        in_specs=in_specs, out_specs=out_specs, out_shape=out_shape,
        compiler_params=_params("parallel", "parallel", "arbitrary"),
        name=f"attn_d{dil}" + ("_swa" if kv_shared else ""),
    )(*args)
    o = outs[0].reshape(nbatch * length, dil * MIX_B)
    if not want_lse:
        return o, None
    lse = outs[1].transpose(0, 2, 1, 3).reshape(nbatch * seq, N_HEADS_B)
    return o, lse


def _expand_heads(w, width):
    head = lax.broadcasted_iota(I32, (1, width), 1) // HEAD_DIM
    out = jnp.zeros((w.shape[0], width), F32)
    for hd in range(w.shape[1]):
        out = jnp.where(head == hd, w[:, hd:hd + 1], out)
    return out


def _outproj_kernel(x_ref, mod_ref, oa_ref, ob1_ref, ob2_ref, ob3_ref, l1_ref, l2_ref, l3_ref,
                    goa_ref, gob_ref, wo_ref, gf_ref, wgs_ref, wus_ref, wds_ref,
                    h2_ref, h2t_ref, base_ref, ob_scr):
    tt = x_ref.shape[0]
    gate_a = mod_ref[2:3, :]
    shift_m, scale_m, gate_m = mod_ref[3:4, :], mod_ref[4:5, :], mod_ref[5:6, :]

    def token_major(ref, dil):
        if dil == 1:
            return ref[...].astype(F32)
        for r in range(dil):
            for j in range(N_PAIRS):
                c = r * MIX_B + j * LANES
                ob_scr[j, pl.ds(r, tt // dil, stride=dil), :] = ref[:, c:c + LANES].astype(F32)
        return jnp.concatenate([ob_scr[j] for j in range(N_PAIRS)], axis=1)

    l1, l2, l3 = l1_ref[...], l2_ref[...], l3_ref[...]
    mx = jnp.maximum(jnp.maximum(l1, l2), l3)
    e1, e2, e3 = jnp.exp(l1 - mx), jnp.exp(l2 - mx), jnp.exp(l3 - mx)
    den = e1 + e2 + e3
    dils = [dil for _, dil in DILATED_BRANCHES]
    ob = _expand_heads(e1 / den, MIX_B) * token_major(ob1_ref, dils[0])
    ob = ob + _expand_heads(e2 / den, MIX_B) * token_major(ob2_ref, dils[1])
    ob = ob + _expand_heads(e3 / den, MIX_B) * token_major(ob3_ref, dils[2])
    ob = _rms_rows(ob) * gob_ref[...]
    oa = _rms_rows(oa_ref[...].astype(F32)) * goa_ref[...]
    y = (jnp.dot(oa.astype(BF16), wo_ref[0:MIX_A, :], preferred_element_type=F32)
         + jnp.dot(ob.astype(BF16), wo_ref[MIX_A:MIX_A + MIX_B, :], preferred_element_type=F32))
    x1 = x_ref[...] + gate_a * y
    h2 = _rms_rows(x1) * gf_ref[...] * (1.0 + scale_m) + shift_m
    h2_ref[...] = h2
    _store_row_tiles(h2t_ref, h2)
    hb = h2.astype(BF16)
    act = (_silu(jnp.dot(hb, wgs_ref[...], preferred_element_type=F32))
           * jnp.dot(hb, wus_ref[...], preferred_element_type=F32))
    shared = jnp.dot(act.astype(BF16), wds_ref[...], preferred_element_type=F32)
    base_ref[...] = x1 + gate_m * shared


def _outproj(xf, mod, oa, obs, lses, goa, gob, wo_p, gf, wgs, wus, wds, seq):
    n, d = xf.shape
    tt = TT_PROJ // 2
    tiles_per_seq = seq // tt
    tile = lambda w: pl.BlockSpec((tt, w), lambda i: (i, 0))
    full = lambda shape: pl.BlockSpec(shape, lambda i: (0,) * len(shape))
    sd = wgs.shape[1]
    dilated = [pl.BlockSpec((tt // dil, dil * MIX_B), lambda i: (i, 0)) for _, dil in DILATED_BRANCHES]
    return pl.pallas_call(
        _outproj_kernel,
        grid=(n // tt,),
        in_specs=[tile(d), pl.BlockSpec((None, 6, d), lambda i: (i // tiles_per_seq, 0, 0)),
                  tile(MIX_A), *dilated,
                  tile(N_HEADS_B), tile(N_HEADS_B), tile(N_HEADS_B),
                  full((1, MIX_A)), full((1, MIX_B)), full((MIX_A + MIX_B, d)), full((1, d)),
                  full((d, sd)), full((d, sd)), full((sd, d))],
        out_specs=[tile(d), pl.BlockSpec((tt * d // LANES, LANES), lambda i: (i, 0)), tile(d)],
        out_shape=[jax.ShapeDtypeStruct((n, d), F32), jax.ShapeDtypeStruct((n * d // LANES, LANES), F32),
                   jax.ShapeDtypeStruct((n, d), F32)],
        scratch_shapes=[pltpu.VMEM((N_PAIRS, tt, LANES), F32)],
        compiler_params=_params("parallel"),
        name="outproj",
    )(xf, mod, oa, *obs, *lses, goa, gob, wo_p, gf, wgs, wus, wds)


def _router_kernel(h_ref, wrt_ref, bias_ref, e_ref, g_ref, cnt_ref):
    tt = h_ref.shape[0]
    logits = lax.dot_general(wrt_ref[...], h_ref[...].astype(BF16), (((1,), (1,)), ((), ())),
                             preferred_element_type=F32)
    scores = 1.0 / (1.0 + jnp.exp(-logits))
    biased = scores + bias_ref[...]
    ninf = -jnp.inf

    j32 = lax.broadcasted_iota(I32, (GROUP_SIZE, tt), 0).astype(F32)
    grp = []
    for g in range(N_GROUPS):
        bg = biased[g * GROUP_SIZE:(g + 1) * GROUP_SIZE, :]
        m1 = jnp.max(bg, axis=0, keepdims=True)
        i1 = jnp.min(jnp.where(bg == m1, j32, float(GROUP_SIZE)), axis=0, keepdims=True)
        m2 = jnp.max(jnp.where(j32 == i1, ninf, bg), axis=0, keepdims=True)
        grp.append(m1 + m2)
    grp = jnp.concatenate(grp, axis=0)
    g8 = lax.broadcasted_iota(I32, (N_GROUPS, tt), 0).astype(F32)
    chosen = jnp.zeros((N_GROUPS, tt), F32)
    for _ in range(TOPK_GROUPS):
        gm = jnp.max(grp, axis=0, keepdims=True)
        gi = jnp.min(jnp.where(grp == gm, g8, float(N_GROUPS)), axis=0, keepdims=True)
        hit = g8 == gi
        chosen = jnp.where(hit, 1.0, chosen)
        grp = jnp.where(hit, ninf, grp)
    masked = jnp.concatenate(
        [jnp.where(chosen[g:g + 1, :] > 0.0, biased[g * GROUP_SIZE:(g + 1) * GROUP_SIZE, :], ninf)
         for g in range(N_GROUPS)], axis=0)

    eio = lax.broadcasted_iota(I32, (N_EXPERTS, tt), 0).astype(F32)
    picked = jnp.zeros((N_EXPERTS, tt), F32)
    es, gs = [], []
    for _ in range(TOP_K):
        m = jnp.max(masked, axis=0, keepdims=True)
        idx = jnp.min(jnp.where(masked == m, eio, float(N_EXPERTS)), axis=0, keepdims=True)
        hit = eio == idx
        gs.append(jnp.sum(jnp.where(hit, scores, 0.0), axis=0, keepdims=True))
        es.append(idx)
        picked = jnp.where(hit, 1.0, picked)
        masked = jnp.where(hit, ninf, masked)
    gates = jnp.concatenate(gs, axis=0)
    e_ref[...] = jnp.concatenate(es, axis=0).astype(I32)
    g_ref[...] = gates / jnp.sum(gates, axis=0, keepdims=True) * ROUTED_SCALE

    @pl.when(pl.program_id(0) == 0)
    def _():
        cnt_ref[...] = jnp.zeros_like(cnt_ref)
    cnt_ref[...] += jnp.sum(picked, axis=1, keepdims=True)


def _router(h2, wrt, bias_col):
    n, d = h2.shape
    tt = TT_ROUTE
    return pl.pallas_call(
        _router_kernel,
        grid=(n // tt,),
        in_specs=[pl.BlockSpec((tt, d), lambda i: (i, 0)),
                  pl.BlockSpec((N_EXPERTS, d), lambda i: (0, 0)),
                  pl.BlockSpec((N_EXPERTS, 1), lambda i: (0, 0))],
        out_specs=[pl.BlockSpec((TOP_K, tt), lambda i: (0, i)),
                   pl.BlockSpec((TOP_K, tt), lambda i: (0, i)),
                   pl.BlockSpec((N_EXPERTS, 1), lambda i: (0, 0))],
        out_shape=[jax.ShapeDtypeStruct((TOP_K, n), I32),
                   jax.ShapeDtypeStruct((TOP_K, n), F32),
                   jax.ShapeDtypeStruct((N_EXPERTS, 1), F32)],
        compiler_params=_params("arbitrary"),
        name="router",
    )(h2, wrt, bias_col)


def _rank_kernel(e_ref, pstart_ref, tri_ref, dest_ref, carry_ref):
    tt = e_ref.shape[1]

    @pl.when(pl.program_id(0) == 0)
    def _():
        carry_ref[...] = pstart_ref[...]

    e = e_ref[...]
    eio = lax.broadcasted_iota(I32, (N_EXPERTS, tt), 0)
    mask = jnp.zeros((N_EXPERTS, tt), F32)
    for k in range(TOP_K):
        mask = jnp.where(eio == e[k:k + 1, :], 1.0, mask)
    incl = jnp.dot(mask.astype(BF16), tri_ref[...], preferred_element_type=F32)
    pos = incl - mask + carry_ref[...]
    dest = [jnp.sum(jnp.where(eio == e[k:k + 1, :], pos, 0.0), axis=0, keepdims=True)
            for k in range(TOP_K)]
    dest_ref[...] = jnp.concatenate(dest, axis=0).astype(I32)
    carry_ref[...] += incl[:, tt - 1:tt]


def _rank(top_e_t, pstart_col, tri):
    n = top_e_t.shape[1]
    tt = TT_ROUTE
    return pl.pallas_call(
        _rank_kernel,
        grid=(n // tt,),
        in_specs=[pl.BlockSpec((TOP_K, tt), lambda i: (0, i)),
                  pl.BlockSpec((N_EXPERTS, 1), lambda i: (0, 0)),
                  pl.BlockSpec((tt, tt), lambda i: (0, 0))],
        out_specs=pl.BlockSpec((TOP_K, tt), lambda i: (0, i)),
        out_shape=jax.ShapeDtypeStruct((TOP_K, n), I32),
        scratch_shapes=[pltpu.VMEM((N_EXPERTS, 1), F32)],
        compiler_params=_params("arbitrary"),
        name="rank",
    )(top_e_t, pstart_col, tri)


def _row_copy(src_ref, src_row, dst_ref, dst_row, sem):
    return pltpu.make_async_copy(src_ref.at[pl.ds(src_row, 1)], dst_ref.at[pl.ds(dst_row, 1)], sem)


SC_CORES, SC_SUBCORES, SC_LANES = 2, 16, 16
SC_WORKERS = SC_CORES * SC_SUBCORES
INVERT_CHUNK = 8192


def _invert(dest_flat, n_tokens, rows):
    per = rows // SC_WORKERS
    n_assign = dest_flat.shape[0]
    assert rows % (SC_WORKERS * SC_LANES) == 0 and n_assign % INVERT_CHUNK == 0
    assert n_tokens & (n_tokens - 1) == 0

    @functools.partial(
        pl.kernel, mesh=plsc.VectorSubcoreMesh(core_axis_name="c", subcore_axis_name="s"),
        out_type=jax.ShapeDtypeStruct((rows,), I32),
        scratch_types=[pltpu.VMEM((INVERT_CHUNK,), I32), pltpu.VMEM((per,), I32)],
        compiler_params=pltpu.CompilerParams(needs_layout_passes=False))
    def invert(dest_hbm, out_hbm, staged, local):
        base = (lax.axis_index("s") * SC_CORES + lax.axis_index("c")) * per
        lane = lax.iota(I32, SC_LANES)

        @pl.loop(0, per, step=SC_LANES)
        def _(i):
            local[pl.ds(i, SC_LANES)] = (base + i + lane) & (n_tokens - 1)

        @pl.loop(0, n_assign // INVERT_CHUNK)
        def _(c):
            pltpu.sync_copy(dest_hbm.at[pl.ds(c * INVERT_CHUNK, INVERT_CHUNK)], staged)

            @pl.loop(0, INVERT_CHUNK, step=SC_LANES)
            def _(i):
                rel = staged[pl.ds(i, SC_LANES)] - base
                mine = (rel >= 0) & (rel < per)
                tok = (c * INVERT_CHUNK + i + lane) & (n_tokens - 1)
                plsc.store_scatter(local, [jnp.where(mine, rel, 0)], tok, mask=mine)

        pltpu.sync_copy(local, out_hbm.at[pl.ds(base, per)])

    return invert(dest_flat)


def _experts_kernel(be_ref, nv_ref, tok_hbm, h_ref, wg_ref, wu_ref, wd_ref, ys_ref,
                    xring, wgb, wub, wdb, tok_s, sem, tok_sem):
    j = pl.program_id(0)
    nv = nv_ref[0]
    per = wgb.shape[0] // LANES
    depth, bm = xring.shape[0], xring.shape[1] // per
    ahead = depth - 1
    chunk = tok_s.shape[1]
    cb = chunk // bm
    n_chunks = tok_hbm.shape[0] // chunk
    n_blocks = n_chunks * cb

    def tok_copy(c):
        return pltpu.make_async_copy(tok_hbm.at[pl.ds(pl.multiple_of(c * chunk, chunk), chunk)],
                                     tok_s.at[c % 2], tok_sem.at[c % 2])

    def gather(block, unrolled):
        ring = block % depth
        block = jnp.minimum(block, n_blocks - 1)
        slot, off = (block // cb) % 2, (block % cb) * bm
        buf, buf_sem = xring.at[ring], sem.at[ring]

        def issue(i, carry=None):
            _tile_copy(h_ref, tok_s[slot, off + i], buf, i, per, buf_sem).start()
            return carry

        if unrolled:
            for i in range(bm):
                _tile_copy(h_ref, tok_s[slot, off + i], buf, i, per, buf_sem).start(priority=i % 2)
        else:
            lax.fori_loop(0, bm, issue, 0, unroll=8)

    def wait_rows(ring):
        pltpu.make_async_copy(h_ref.at[pl.ds(0, bm * per)], xring.at[ring], sem.at[ring]).wait()

    @pl.when(j == 0)
    def _():
        tok_copy(0).start()
        tok_copy(0).wait()
        tok_copy(1).start()
        for b in range(ahead):
            gather(b, unrolled=False)

    first = j + ahead
    c_need = first // cb

    @pl.when(jnp.logical_and(jnp.logical_and(j > 0, first % cb == 0), c_need < n_chunks))
    def _():
        tok_copy(c_need).wait()

        @pl.when(c_need + 1 < n_chunks)
        def _():
            tok_copy(c_need + 1).start()

    jb = jnp.minimum(j, n_blocks - 1)
    fresh = jnp.logical_or(j == 0, be_ref[jb] != be_ref[jnp.maximum(jb - 1, 0)])

    @pl.when(jnp.logical_and(j < nv, fresh))
    def _():
        wgb[...] = wg_ref[...].astype(BF16)
        wub[...] = wu_ref[...].astype(BF16)
        wdb[...] = wd_ref[...].astype(BF16)

    @pl.when(j < nv)
    def _():
        wait_rows(j % depth)
        gather(j + ahead, unrolled=True)
        xb = _load_row_tiles(xring.at[j % depth], bm, per, BF16)
        act = (_silu(jnp.dot(xb, wgb[...], preferred_element_type=F32))
               * jnp.dot(xb, wub[...], preferred_element_type=F32))
        _store_row_tiles(ys_ref, jnp.dot(act.astype(BF16), wdb[...], preferred_element_type=F32))

    @pl.when(jnp.logical_and(j >= nv, j < nv + ahead))
    def _():
        wait_rows(j % depth)


def _experts(block_e, n_valid, row_tok, h2_tiles, wg, wu, wd):
    rows = row_tok.shape[0]
    d = wg.shape[1]
    per = d // LANES
    bm = EXPERT_ROWS
    n_blocks = rows // bm
    assert n_blocks % TOK_CHUNK_BLOCKS == 0 and n_blocks // TOK_CHUNK_BLOCKS >= 2
    assert GATHER_DEPTH - 1 < TOK_CHUNK_BLOCKS
    f = wg.shape[2]
    row_blk = lambda j, be, nv: (jnp.minimum(j, nv[0] - 1), 0)
    w_blk = lambda j, be, nv: (be[jnp.minimum(j, n_blocks - 1)], 0, 0)
    return pl.pallas_call(
        _experts_kernel,
        grid_spec=pltpu.PrefetchScalarGridSpec(
            num_scalar_prefetch=2,
            grid=(n_blocks + GATHER_DEPTH,),
            in_specs=[pl.BlockSpec(memory_space=pl.ANY),
                      pl.BlockSpec(memory_space=pl.ANY),
                      pl.BlockSpec((None, d, f), w_blk),
                      pl.BlockSpec((None, d, f), w_blk),
                      pl.BlockSpec((None, f, d), w_blk)],
            out_specs=pl.BlockSpec((bm * per, LANES), row_blk),
            scratch_shapes=[pltpu.VMEM((GATHER_DEPTH, bm * per, LANES), F32),
                            pltpu.VMEM((d, f), BF16), pltpu.VMEM((d, f), BF16),
                            pltpu.VMEM((f, d), BF16),
                            pltpu.SMEM((2, TOK_CHUNK_BLOCKS * bm), I32),
                            pltpu.SemaphoreType.DMA((GATHER_DEPTH,)), pltpu.SemaphoreType.DMA((2,))]),
        out_shape=jax.ShapeDtypeStruct((rows * per, LANES), F32),
        compiler_params=_params("arbitrary"),
        name="experts",
    )(block_e, n_valid, row_tok, h2_tiles, wg, wu, wd)


def _combine_kernel(dest_ref, gates_ref, base_ref, mod_ref, ys_ref, out_ref, buf, sem):
    tt, d = base_ref.shape
    per = d // LANES
    for t in range(tt):
        for k in range(TOP_K):
            _tile_copy(ys_ref, dest_ref[k, t], buf.at[k], t, per, sem).start(priority=k % 2)
    for k in range(TOP_K):
        pltpu.make_async_copy(ys_ref.at[pl.ds(0, tt * per)], buf.at[k], sem).wait()
    gates = gates_ref[...]
    routed = gates[:, 0:1] * _load_row_tiles(buf.at[0], tt, per, F32)
    for k in range(1, TOP_K):
        routed = routed + gates[:, k:k + 1] * _load_row_tiles(buf.at[k], tt, per, F32)
    out_ref[...] = base_ref[...] + mod_ref[5:6, :] * routed


def _combine(dest_t, gates, base, mod, ys, seq):
    n, d = base.shape
    tt = TT_COMBINE
    tiles_per_seq = seq // tt
    return pl.pallas_call(
        _combine_kernel,
        grid=(n // tt,),
        in_specs=[pl.BlockSpec((TOP_K, tt), lambda i: (0, i), memory_space=pltpu.SMEM),
                  pl.BlockSpec((tt, TOP_K), lambda i: (i, 0)),
                  pl.BlockSpec((tt, d), lambda i: (i, 0)),
                  pl.BlockSpec((None, 6, d), lambda i: (i // tiles_per_seq, 0, 0)),
                  pl.BlockSpec(memory_space=pl.ANY)],
        out_specs=pl.BlockSpec((tt, d), lambda i: (i, 0)),
        out_shape=jax.ShapeDtypeStruct((n, d), F32),
        scratch_shapes=[pltpu.VMEM((TOP_K, tt * d // LANES, LANES), F32), pltpu.SemaphoreType.DMA],
        compiler_params=_params("arbitrary"),
        name="combine",
    )(dest_t, gates, base, mod, ys)


def _layer(x, mod, pos, rope, p):
    nbatch, seq, d = x.shape
    n = nbatch * seq
    xf = x.reshape(n, d)
    invf, bd = rope

    perm = np.concatenate([np.arange(h * HEAD_DIM, (h + 1) * HEAD_DIM) for h in PAIR_ORDER_A])
    w_in = p["w_in"]
    w_in_p = jnp.concatenate([w_in[:, :MIX_A][:, perm], w_in[:, MIX_A:]], axis=1).astype(BF16)
    ones = lambda w: jnp.ones((w,), F32)
    qscale = HEAD_DIM ** -0.5
    gcol = jnp.concatenate([jnp.tile(p["g_q_a"], N_HEADS_A) * qscale, jnp.tile(p["g_k_a"], N_KV_A),
                            ones(KV_A), jnp.tile(p["g_q_b"], N_HEADS_B) * qscale,
                            jnp.tile(p["g_k_b"], N_HEADS_B), ones(MIX_B)]).reshape(1, IN_WIDTH)
    proj = _inproj(xf, mod, pos, p["g_norm_mix"].reshape(1, d), w_in_p, gcol, invf, bd, seq)
    qa, ka, va = proj[:3]
    qkv_b = {1: proj[3:6]}
    for j, dil in enumerate(DILS):
        qkv_b[dil] = [proj[6 + t * len(DILS) + j] for t in range(3)]

    sinks_p = p["sinks_a"][np.array(PAIR_ORDER_A)]
    oa, _ = _attention(qa, ka, va, nbatch=nbatch, seq=seq, dil=1, max_dist=WINDOW_A - 1,
                       kv_shared=True, sinks=sinks_p, want_lse=False)
    obs, lses = [], []
    for window, dil in DILATED_BRANCHES:
        o, lse = _attention(*qkv_b[dil], nbatch=nbatch, seq=seq, dil=dil, max_dist=window // dil,
                            kv_shared=False, want_lse=True)
        obs.append(o)
        lses.append(lse)

    goa = p["g_out_a"][perm].reshape(1, MIX_A)
    w_out = p["w_out"]
    wo_p = jnp.concatenate([w_out[:MIX_A][perm], w_out[MIX_A:]], axis=0).astype(BF16)
    h2, h2_tiles, base = _outproj(xf, mod, oa, obs, lses, goa, p["g_out_b"].reshape(1, MIX_B), wo_p,
                        p["g_norm_ffn"].reshape(1, d), p["w_gate_s"].astype(BF16),
                        p["w_up_s"].astype(BF16), p["w_down_s"].astype(BF16), seq)

    top_e_t, gates_t, counts = _router(h2, p["w_router"].T.astype(BF16),
                                       p["router_bias"].reshape(N_EXPERTS, 1))
    bm = EXPERT_ROWS
    counts = counts.reshape(N_EXPERTS).astype(I32)
    padded = (counts + bm - 1) // bm * bm
    pends = jnp.cumsum(padded)
    pstarts = pends - padded
    rows = n * TOP_K + N_EXPERTS * bm
    n_blocks = rows // bm
    n_valid = (pends[-1] // bm).astype(I32)
    blk = jnp.minimum(jnp.arange(n_blocks, dtype=I32), n_valid - 1)
    block_e = jnp.sum((pends[None, :] <= (blk * bm)[:, None]).astype(I32), axis=1)
    block_e = jnp.minimum(block_e, N_EXPERTS - 1)

    tri = (np.arange(TT_ROUTE)[:, None] <= np.arange(TT_ROUTE)[None, :])
    dest_t = _rank(top_e_t, pstarts.astype(F32).reshape(N_EXPERTS, 1), jnp.asarray(tri, BF16))
    n_valid = n_valid.reshape(1)
    row_tok = _invert(dest_t.reshape(n * TOP_K), n, rows)
    ys = _experts(block_e, n_valid, row_tok, h2_tiles, p["w_gate_e"], p["w_up_e"], p["w_down_e"])
    out = _combine(dest_t, gates_t.T, base, mod, ys, seq)
    return out.reshape(nbatch, seq, d)


def kernel(x, c, positions, w_ada, b_ada, g_norm_mix, w_in, g_q_a, g_k_a, sinks_a, g_q_b, g_k_b,
           g_out_a, g_out_b, w_out, g_norm_ffn, w_router, router_bias, w_gate_e, w_up_e, w_down_e,
           w_gate_s, w_up_s, w_down_s):
    nbatch, seq, d = x.shape
    depth = w_ada.shape[0]
    params = dict(g_norm_mix=g_norm_mix, w_in=w_in, g_q_a=g_q_a, g_k_a=g_k_a, sinks_a=sinks_a,
                  g_q_b=g_q_b, g_k_b=g_k_b, g_out_a=g_out_a, g_out_b=g_out_b, w_out=w_out,
                  g_norm_ffn=g_norm_ffn, w_router=w_router, router_bias=router_bias,
                  w_gate_e=w_gate_e, w_up_e=w_up_e, w_down_e=w_down_e, w_gate_s=w_gate_s,
                  w_up_s=w_up_s, w_down_s=w_down_s)
    j = np.arange(LANES) % HEAD_DIM
    inv = ROPE_THETA ** (-jnp.arange(0, ROT_DIM, 2, dtype=F32) / ROT_DIM)
    invf = jnp.where(j < ROT_DIM, inv[j % (ROT_DIM // 2)], 0.0).astype(F32).reshape(1, LANES)
    bd = jnp.asarray((np.arange(LANES)[:, None] // HEAD_DIM) == (np.arange(LANES)[None, :] // HEAD_DIM),
                     BF16)
    pos = positions.reshape(nbatch * seq, 1).astype(I32)
    for l in range(depth):
        mod = _adaln(c.astype(F32), w_ada[l], b_ada[l]).reshape(nbatch, 6, d)
        x = _layer(x, mod, pos, (invf, bd), {k: v[l] for k, v in params.items()})
    return x
```

```python
import functools

import numpy as np
import jax
import jax.numpy as jnp
from jax import lax
from jax.experimental import pallas as pl
from jax.experimental.pallas import tpu as pltpu
from jax.experimental.pallas import tpu_sc as plsc

F32 = jnp.float32
BF16 = jnp.bfloat16
I32 = jnp.int32

HEAD_DIM = 64
N_HEADS_A = 8
N_KV_A = 2
WINDOW_A = 128
N_HEADS_B = 8
DILATED_BRANCHES = ((128, 1), (512, 4), (2048, 16))
DILS = tuple(dil for _, dil in DILATED_BRANCHES if dil > 1)
BLOCK = 128
ROT_DIM = HEAD_DIM // 4
ROPE_THETA = 500000.0
MIX_A = N_HEADS_A * HEAD_DIM
KV_A = N_KV_A * HEAD_DIM
MIX_B = N_HEADS_B * HEAD_DIM
N_EXPERTS = 256
TOP_K = 8
N_GROUPS = 8
TOPK_GROUPS = 4
GROUP_SIZE = N_EXPERTS // N_GROUPS
ROUTED_SCALE = 2.5
EPS = 1e-6

LANES = 128
HEADS_PER_VREG = LANES // HEAD_DIM
N_PAIRS = MIX_A // LANES
NEG = -1e30
VMEM_LIMIT = 48 * 1024 * 1024

TT_PROJ = 512
TT_ROUTE = 256
TT_DISPATCH = 256
TT_COMBINE = 128
EXPERT_ROWS = 256
TOK_CHUNK_BLOCKS = 32
GATHER_DEPTH = 4
SUBLANES = 8

PAIR_ORDER_A = tuple(h for p in range(N_PAIRS) for h in (p, p + N_HEADS_A // N_KV_A))


def _params(*sem):
    return pltpu.CompilerParams(dimension_semantics=sem, vmem_limit_bytes=VMEM_LIMIT)


def _silu(t):
    return t / (1.0 + jnp.exp(-t))


def _rms_rows(t):
    return t * lax.rsqrt(jnp.mean(t * t, axis=-1, keepdims=True) + EPS)


def _store_row_tiles(ref, value):
    rows, d = value.shape
    per = d // LANES
    for s in range(per):
        ref[pl.ds(s, rows, stride=per), :] = value[:, s * LANES:(s + 1) * LANES]


def _load_row_tiles(ref, rows, per, dtype):
    return jnp.concatenate([ref[pl.ds(s, rows, stride=per), :].astype(dtype) for s in range(per)], axis=1)


def _tile_copy(src_ref, src_row, dst_ref, dst_row, per, sem):
    start = lambda row: row * per if isinstance(row, int) else pl.multiple_of(row * per, per)
    return pltpu.make_async_copy(src_ref.at[pl.ds(start(src_row), per)],
                                 dst_ref.at[pl.ds(start(dst_row), per)], sem)


def _ada_kernel(c_ref, w_ref, b_ref, o_ref):
    cond = _silu(c_ref[...])
    o_ref[...] = jnp.dot(cond.astype(BF16), w_ref[...].astype(BF16),
                         preferred_element_type=F32) + b_ref[...]


def _adaln(c, w_ada, b_ada):
    nb, d = c.shape
    width = w_ada.shape[1]
    tn = 1024
    return pl.pallas_call(
        _ada_kernel,
        grid=(width // tn,),
        in_specs=[pl.BlockSpec((nb, d), lambda j: (0, 0)),
                  pl.BlockSpec((d, tn), lambda j: (0, j)),
                  pl.BlockSpec((1, tn), lambda j: (0, j))],
        out_specs=pl.BlockSpec((nb, tn), lambda j: (0, j)),
        out_shape=jax.ShapeDtypeStruct((nb, width), F32),
        compiler_params=_params("arbitrary"),
        name="adaln",
    )(c, w_ada, b_ada.reshape(1, width))


COL_QA, COL_KA, COL_VA = 0, MIX_A, MIX_A + KV_A
COL_QB = MIX_A + 2 * KV_A
COL_KB, COL_VB = COL_QB + MIX_B, COL_QB + 2 * MIX_B
IN_WIDTH = COL_VB + MIX_B


def _inproj_kernel(x_ref, mod_ref, pos_ref, gn_ref, w_ref, gcol_ref, invf_ref, bd_ref,
                   qa_ref, ka_ref, va_ref, qb_ref, kb_ref, vb_ref, *rest):
    n_dil = len(DILS)
    dil_refs = [rest[i * n_dil:(i + 1) * n_dil] for i in range(3)]
    qb_scr, kb_scr, vb_scr = rest[3 * n_dil:]
    tt = x_ref.shape[0]
    shift, scale = mod_ref[0:1, :], mod_ref[1:2, :]
    h = _rms_rows(x_ref[...]) * gn_ref[...] * (1.0 + scale) + shift
    proj = jnp.dot(h.astype(BF16), w_ref[...], preferred_element_type=F32)

    ang = pos_ref[...].astype(F32) * invf_ref[...]
    cs, sn = jnp.cos(ang), jnp.sin(ang)
    lane = lax.broadcasted_iota(I32, (1, LANES), 1) % HEAD_DIM
    s_lo = jnp.where(lane < ROT_DIM // 2, -sn, 0.0)
    s_hi = jnp.where((lane >= ROT_DIM // 2) & (lane < ROT_DIM), sn, 0.0)
    bd = bd_ref[...]

    def norm_rope(col0, width, out_ref, scr=None):
        for j in range(width // LANES):
            c = col0 + j * LANES
            t = proj[:, c:c + LANES]
            sq = t * t
            hi = sq.astype(BF16)
            lo = (sq - hi.astype(F32)).astype(BF16)
            ss = (jnp.dot(hi, bd, preferred_element_type=F32)
                  + jnp.dot(lo, bd, preferred_element_type=F32))
            t = t * lax.rsqrt(ss * (1.0 / HEAD_DIM) + EPS) * gcol_ref[:, c:c + LANES]
            t = (t * cs + pltpu.roll(t, LANES - ROT_DIM // 2, 1) * s_lo
                 + pltpu.roll(t, ROT_DIM // 2, 1) * s_hi)
            out_ref[:, j * LANES:(j + 1) * LANES] = t.astype(BF16)
            if scr is not None:
                scr[j] = t

    norm_rope(COL_QA, MIX_A, qa_ref)
    norm_rope(COL_KA, KV_A, ka_ref)
    norm_rope(COL_QB, MIX_B, qb_ref, qb_scr)
    norm_rope(COL_KB, MIX_B, kb_ref, kb_scr)
    va_ref[...] = proj[:, COL_VA:COL_VA + KV_A].astype(BF16)
    vb_ref[...] = proj[:, COL_VB:COL_VB + MIX_B].astype(BF16)
    for j in range(N_PAIRS):
        vb_scr[j] = proj[:, COL_VB + j * LANES:COL_VB + (j + 1) * LANES]
    for scr, outs in zip((qb_scr, kb_scr, vb_scr), dil_refs):
        for dil, out in zip(DILS, outs):
            for r in range(dil):
                for j in range(N_PAIRS):
                    c = r * MIX_B + j * LANES
                    out[:, c:c + LANES] = scr[j, pl.ds(r, tt // dil, stride=dil), :].astype(BF16)


def _inproj(xf, mod, pos, g_norm, w_in_p, gcol, invf, bd, seq):
    n, d = xf.shape
    tt = TT_PROJ
    tiles_per_seq = seq // tt
    shapes = [(n, w) for w in (MIX_A, KV_A, KV_A, MIX_B, MIX_B, MIX_B)]
    shapes += [(n // dil, dil * MIX_B) for _ in range(3) for dil in DILS]
    full = lambda shape: pl.BlockSpec(shape, lambda i: (0,) * len(shape))
    return pl.pallas_call(
        _inproj_kernel,
        grid=(n // tt,),
        in_specs=[pl.BlockSpec((tt, d), lambda i: (i, 0)),
                  pl.BlockSpec((None, 6, d), lambda i: (i // tiles_per_seq, 0, 0)),
                  pl.BlockSpec((tt, 1), lambda i: (i, 0)),
                  full((1, d)), full((d, IN_WIDTH)), full((1, IN_WIDTH)),
                  full((1, LANES)), full((LANES, LANES))],
        out_specs=[pl.BlockSpec((tt * r // n, w), lambda i: (i, 0)) for r, w in shapes],
        out_shape=[jax.ShapeDtypeStruct(s, BF16) for s in shapes],
        scratch_shapes=[pltpu.VMEM((N_PAIRS, tt, LANES), F32)] * 3,
        compiler_params=_params("parallel"),
        name="inproj",
    )(xf, mod, pos, g_norm, w_in_p, gcol, invf, bd)


def _attn_kernel(*refs, kv_shared, max_dist, use_prev, has_sinks, want_lse):
    refs = list(refs)
    sink_ref = refs.pop(0) if has_sinks else None
    q_ref = refs.pop(0)
    kp_ref = refs.pop(0) if use_prev else None
    kc_ref = refs.pop(0)
    vp_ref = refs.pop(0) if use_prev else None
    vc_ref = refs.pop(0)
    o_ref = refs.pop(0)
    lse_ref = refs.pop(0) if want_lse else None

    blk = pl.program_id(2)
    nq = 2 * BLOCK
    nk = 2 * BLOCK if use_prev else BLOCK
    qpos = lax.broadcasted_iota(I32, (nq, nk), 0) % BLOCK
    kpos = lax.broadcasted_iota(I32, (nq, nk), 1)
    if use_prev:
        dist = qpos + BLOCK - kpos
        valid = (dist >= 0) & (dist <= max_dist) & ((kpos >= BLOCK) | (blk > 0))
    else:
        dist = qpos - kpos
        valid = (dist >= 0) & (dist <= max_dist)
    lane = lax.broadcasted_iota(I32, (nq, LANES), 1)
    row = lax.broadcasted_iota(I32, (nq, LANES), 0)
    own_half = (lane < HEAD_DIM) == (row < BLOCK)
    left_lanes = lax.broadcasted_iota(I32, (BLOCK, LANES), 1) < HEAD_DIM
    lane8 = lax.broadcasted_iota(I32, (BLOCK, 2 * N_PAIRS), 1)
    lse_blk = jnp.zeros((BLOCK, 2 * N_PAIRS), F32)

    for p in range(N_PAIRS):
        cq = slice(p * LANES, (p + 1) * LANES)
        ck = slice(0, LANES) if kv_shared else cq
        qp = q_ref[:, cq]
        qs = jnp.concatenate([qp, qp], axis=0)
        qs = jnp.where(own_half, qs, jnp.zeros_like(qs))
        if use_prev:
            k = jnp.concatenate([kp_ref[:, ck], kc_ref[:, ck]], axis=0)
            v = jnp.concatenate([vp_ref[:, ck], vc_ref[:, ck]], axis=0)
        else:
            k, v = kc_ref[:, ck], vc_ref[:, ck]
        s = lax.dot_general(qs, k, (((1,), (1,)), ((), ())), preferred_element_type=F32)
        s = jnp.where(valid, s, NEG)
        m = jnp.max(s, axis=-1, keepdims=True)
        if has_sinks:
            rows1 = lax.broadcasted_iota(I32, (nq, 1), 0)
            sink = jnp.where(rows1 < BLOCK, sink_ref[2 * p], sink_ref[2 * p + 1])
            m = jnp.maximum(m, sink)
        e = jnp.exp(s - m)
        l = jnp.sum(e, axis=-1, keepdims=True)
        if has_sinks:
            l = l + jnp.exp(sink - m)
        o = jnp.dot(e.astype(BF16), v, preferred_element_type=F32) / l
        o_ref[:, cq] = jnp.where(left_lanes, o[:BLOCK], o[BLOCK:]).astype(BF16)
        if want_lse:
            lse = m + jnp.log(l)
            lse_blk = (lse_blk + jnp.where(lane8 == 2 * p, lse[:BLOCK], 0.0)
                       + jnp.where(lane8 == 2 * p + 1, lse[BLOCK:], 0.0))
    if want_lse:
        lse_ref[...] = lse_blk


def _attention(q, k, v, *, nbatch, seq, dil, max_dist, kv_shared, sinks=None, want_lse):
    length = seq // dil
    nblk = length // BLOCK
    use_prev = nblk > 1
    kw = k.shape[1] // dil
    view = lambda t: t.reshape(nbatch, length, t.shape[1])
    cur = lambda b, r, i: (b, i, r)
    prev = lambda b, r, i: (b, jnp.maximum(i - 1, 0), r)
    in_specs, args = [], []
    if sinks is not None:
        in_specs.append(pl.BlockSpec(memory_space=pltpu.SMEM))
        args.append(sinks)
    in_specs.append(pl.BlockSpec((None, BLOCK, MIX_B), cur))
    args.append(view(q))
    for t in (k, v):
        if use_prev:
            in_specs.append(pl.BlockSpec((None, BLOCK, kw), prev))
            args.append(view(t))
        in_specs.append(pl.BlockSpec((None, BLOCK, kw), cur))
        args.append(view(t))
    out_specs = [pl.BlockSpec((None, BLOCK, MIX_B), cur)]
    out_shape = [jax.ShapeDtypeStruct((nbatch, length, dil * MIX_B), BF16)]
    if want_lse:
        out_specs.append(pl.BlockSpec((None, None, BLOCK, N_HEADS_B), lambda b, r, i: (b, r, i, 0)))
        out_shape.append(jax.ShapeDtypeStruct((nbatch, dil, length, N_HEADS_B), F32))
    outs = pl.pallas_call(
        functools.partial(_attn_kernel, kv_shared=kv_shared, max_dist=max_dist, use_prev=use_prev,
                          has_sinks=sinks is not None, want_lse=want_lse),
        grid=(nbatch, dil, nblk),
        in_specs=in_specs, out_specs=out_specs, out_shape=out_shape,
        compiler_params=_params("parallel", "parallel", "arbitrary"),
        name=f"attn_d{dil}" + ("_swa" if kv_shared else ""),
    )(*args)
    o = outs[0].reshape(nbatch * length, dil * MIX_B)
    if not want_lse:
        return o, None
    lse = outs[1].transpose(0, 2, 1, 3).reshape(nbatch * seq, N_HEADS_B)
    return o, lse


def _expand_heads(w, width):
    head = lax.broadcasted_iota(I32, (1, width), 1) // HEAD_DIM
    out = jnp.zeros((w.shape[0], width), F32)
    for hd in range(w.shape[1]):
        out = jnp.where(head == hd, w[:, hd:hd + 1], out)
    return out


def _outproj_kernel(x_ref, mod_ref, oa_ref, ob1_ref, ob2_ref, ob3_ref, l1_ref, l2_ref, l3_ref,
                    goa_ref, gob_ref, wo_ref, gf_ref, wgs_ref, wus_ref, wds_ref,
                    h2_ref, h2t_ref, base_ref, ob_scr):
    tt = x_ref.shape[0]
    gate_a = mod_ref[2:3, :]
    shift_m, scale_m, gate_m = mod_ref[3:4, :], mod_ref[4:5, :], mod_ref[5:6, :]

    def token_major(ref, dil):
        if dil == 1:
            return ref[...].astype(F32)
        for r in range(dil):
            for j in range(N_PAIRS):
                c = r * MIX_B + j * LANES
                ob_scr[j, pl.ds(r, tt // dil, stride=dil), :] = ref[:, c:c + LANES].astype(F32)
        return jnp.concatenate([ob_scr[j] for j in range(N_PAIRS)], axis=1)

    l1, l2, l3 = l1_ref[...], l2_ref[...], l3_ref[...]
    mx = jnp.maximum(jnp.maximum(l1, l2), l3)
    e1, e2, e3 = jnp.exp(l1 - mx), jnp.exp(l2 - mx), jnp.exp(l3 - mx)
    den = e1 + e2 + e3
    dils = [dil for _, dil in DILATED_BRANCHES]
    ob = _expand_heads(e1 / den, MIX_B) * token_major(ob1_ref, dils[0])
    ob = ob + _expand_heads(e2 / den, MIX_B) * token_major(ob2_ref, dils[1])
    ob = ob + _expand_heads(e3 / den, MIX_B) * token_major(ob3_ref, dils[2])
    ob = _rms_rows(ob) * gob_ref[...]
    oa = _rms_rows(oa_ref[...].astype(F32)) * goa_ref[...]
    y = (jnp.dot(oa.astype(BF16), wo_ref[0:MIX_A, :], preferred_element_type=F32)
         + jnp.dot(ob.astype(BF16), wo_ref[MIX_A:MIX_A + MIX_B, :], preferred_element_type=F32))
    x1 = x_ref[...] + gate_a * y
    h2 = _rms_rows(x1) * gf_ref[...] * (1.0 + scale_m) + shift_m
    h2_ref[...] = h2
    _store_row_tiles(h2t_ref, h2)
    hb = h2.astype(BF16)
    act = (_silu(jnp.dot(hb, wgs_ref[...], preferred_element_type=F32))
           * jnp.dot(hb, wus_ref[...], preferred_element_type=F32))
    shared = jnp.dot(act.astype(BF16), wds_ref[...], preferred_element_type=F32)
    base_ref[...] = x1 + gate_m * shared


def _outproj(xf, mod, oa, obs, lses, goa, gob, wo_p, gf, wgs, wus, wds, seq):
    n, d = xf.shape
    tt = TT_PROJ // 2
    tiles_per_seq = seq // tt
    tile = lambda w: pl.BlockSpec((tt, w), lambda i: (i, 0))
    full = lambda shape: pl.BlockSpec(shape, lambda i: (0,) * len(shape))
    sd = wgs.shape[1]
    dilated = [pl.BlockSpec((tt // dil, dil * MIX_B), lambda i: (i, 0)) for _, dil in DILATED_BRANCHES]
    return pl.pallas_call(
        _outproj_kernel,
        grid=(n // tt,),
        in_specs=[tile(d), pl.BlockSpec((None, 6, d), lambda i: (i // tiles_per_seq, 0, 0)),
                  tile(MIX_A), *dilated,
                  tile(N_HEADS_B), tile(N_HEADS_B), tile(N_HEADS_B),
                  full((1, MIX_A)), full((1, MIX_B)), full((MIX_A + MIX_B, d)), full((1, d)),
                  full((d, sd)), full((d, sd)), full((sd, d))],
        out_specs=[tile(d), pl.BlockSpec((tt * d // LANES, LANES), lambda i: (i, 0)), tile(d)],
        out_shape=[jax.ShapeDtypeStruct((n, d), F32), jax.ShapeDtypeStruct((n * d // LANES, LANES), F32),
                   jax.ShapeDtypeStruct((n, d), F32)],
        scratch_shapes=[pltpu.VMEM((N_PAIRS, tt, LANES), F32)],
        compiler_params=_params("parallel"),
        name="outproj",
    )(xf, mod, oa, *obs, *lses, goa, gob, wo_p, gf, wgs, wus, wds)


def _router_kernel(h_ref, wrt_ref, bias_ref, e_ref, g_ref, cnt_ref):
    tt = h_ref.shape[0]
    logits = lax.dot_general(wrt_ref[...], h_ref[...].astype(BF16), (((1,), (1,)), ((), ())),
                             preferred_element_type=F32)
    scores = 1.0 / (1.0 + jnp.exp(-logits))
    biased = scores + bias_ref[...]
    ninf = -jnp.inf

    j32 = lax.broadcasted_iota(I32, (GROUP_SIZE, tt), 0).astype(F32)
    grp = []
    for g in range(N_GROUPS):
        bg = biased[g * GROUP_SIZE:(g + 1) * GROUP_SIZE, :]
        m1 = jnp.max(bg, axis=0, keepdims=True)
        i1 = jnp.min(jnp.where(bg == m1, j32, float(GROUP_SIZE)), axis=0, keepdims=True)
        m2 = jnp.max(jnp.where(j32 == i1, ninf, bg), axis=0, keepdims=True)
        grp.append(m1 + m2)
    grp = jnp.concatenate(grp, axis=0)
    g8 = lax.broadcasted_iota(I32, (N_GROUPS, tt), 0).astype(F32)
    chosen = jnp.zeros((N_GROUPS, tt), F32)
    for _ in range(TOPK_GROUPS):
        gm = jnp.max(grp, axis=0, keepdims=True)
        gi = jnp.min(jnp.where(grp == gm, g8, float(N_GROUPS)), axis=0, keepdims=True)
        hit = g8 == gi
        chosen = jnp.where(hit, 1.0, chosen)
        grp = jnp.where(hit, ninf, grp)
    masked = jnp.concatenate(
        [jnp.where(chosen[g:g + 1, :] > 0.0, biased[g * GROUP_SIZE:(g + 1) * GROUP_SIZE, :], ninf)
         for g in range(N_GROUPS)], axis=0)

    eio = lax.broadcasted_iota(I32, (N_EXPERTS, tt), 0).astype(F32)
    picked = jnp.zeros((N_EXPERTS, tt), F32)
    es, gs = [], []
    for _ in range(TOP_K):
        m = jnp.max(masked, axis=0, keepdims=True)
        idx = jnp.min(jnp.where(masked == m, eio, float(N_EXPERTS)), axis=0, keepdims=True)
        hit = eio == idx
        gs.append(jnp.sum(jnp.where(hit, scores, 0.0), axis=0, keepdims=True))
        es.append(idx)
        picked = jnp.where(hit, 1.0, picked)
        masked = jnp.where(hit, ninf, masked)
    gates = jnp.concatenate(gs, axis=0)
    e_ref[...] = jnp.concatenate(es, axis=0).astype(I32)
    g_ref[...] = gates / jnp.sum(gates, axis=0, keepdims=True) * ROUTED_SCALE

    @pl.when(pl.program_id(0) == 0)
    def _():
        cnt_ref[...] = jnp.zeros_like(cnt_ref)
    cnt_ref[...] += jnp.sum(picked, axis=1, keepdims=True)


def _router(h2, wrt, bias_col):
    n, d = h2.shape
    tt = TT_ROUTE
    return pl.pallas_call(
        _router_kernel,
        grid=(n // tt,),
        in_specs=[pl.BlockSpec((tt, d), lambda i: (i, 0)),
                  pl.BlockSpec((N_EXPERTS, d), lambda i: (0, 0)),
                  pl.BlockSpec((N_EXPERTS, 1), lambda i: (0, 0))],
        out_specs=[pl.BlockSpec((TOP_K, tt), lambda i: (0, i)),
                   pl.BlockSpec((TOP_K, tt), lambda i: (0, i)),
                   pl.BlockSpec((N_EXPERTS, 1), lambda i: (0, 0))],
        out_shape=[jax.ShapeDtypeStruct((TOP_K, n), I32),
                   jax.ShapeDtypeStruct((TOP_K, n), F32),
                   jax.ShapeDtypeStruct((N_EXPERTS, 1), F32)],
        compiler_params=_params("arbitrary"),
        name="router",
    )(h2, wrt, bias_col)


def _rank_kernel(e_ref, pstart_ref, tri_ref, dest_ref, carry_ref):
    tt = e_ref.shape[1]

    @pl.when(pl.program_id(0) == 0)
    def _():
        carry_ref[...] = pstart_ref[...]

    e = e_ref[...]
    eio = lax.broadcasted_iota(I32, (N_EXPERTS, tt), 0)
    mask = jnp.zeros((N_EXPERTS, tt), F32)
    for k in range(TOP_K):
        mask = jnp.where(eio == e[k:k + 1, :], 1.0, mask)
    incl = jnp.dot(mask.astype(BF16), tri_ref[...], preferred_element_type=F32)
    pos = incl - mask + carry_ref[...]
    dest = [jnp.sum(jnp.where(eio == e[k:k + 1, :], pos, 0.0), axis=0, keepdims=True)
            for k in range(TOP_K)]
    dest_ref[...] = jnp.concatenate(dest, axis=0).astype(I32)
    carry_ref[...] += incl[:, tt - 1:tt]


def _rank(top_e_t, pstart_col, tri):
    n = top_e_t.shape[1]
    tt = TT_ROUTE
    return pl.pallas_call(
        _rank_kernel,
        grid=(n // tt,),
        in_specs=[pl.BlockSpec((TOP_K, tt), lambda i: (0, i)),
                  pl.BlockSpec((N_EXPERTS, 1), lambda i: (0, 0)),
                  pl.BlockSpec((tt, tt), lambda i: (0, 0))],
        out_specs=pl.BlockSpec((TOP_K, tt), lambda i: (0, i)),
        out_shape=jax.ShapeDtypeStruct((TOP_K, n), I32),
        scratch_shapes=[pltpu.VMEM((N_EXPERTS, 1), F32)],
        compiler_params=_params("arbitrary"),
        name="rank",
    )(top_e_t, pstart_col, tri)


def _row_copy(src_ref, src_row, dst_ref, dst_row, sem):
    return pltpu.make_async_copy(src_ref.at[pl.ds(src_row, 1)], dst_ref.at[pl.ds(dst_row, 1)], sem)


SC_CORES, SC_SUBCORES, SC_LANES = 2, 16, 16
SC_WORKERS = SC_CORES * SC_SUBCORES
INVERT_CHUNK = 8192


def _invert(dest_flat, n_tokens, rows):
    per = rows // SC_WORKERS
    n_assign = dest_flat.shape[0]
    assert rows % (SC_WORKERS * SC_LANES) == 0 and n_assign % INVERT_CHUNK == 0
    assert n_tokens & (n_tokens - 1) == 0

    @functools.partial(
        pl.kernel, mesh=plsc.VectorSubcoreMesh(core_axis_name="c", subcore_axis_name="s"),
        out_type=jax.ShapeDtypeStruct((rows,), I32),
        scratch_types=[pltpu.VMEM((INVERT_CHUNK,), I32), pltpu.VMEM((per,), I32)],
        compiler_params=pltpu.CompilerParams(needs_layout_passes=False))
    def invert(dest_hbm, out_hbm, staged, local):
        base = (lax.axis_index("s") * SC_CORES + lax.axis_index("c")) * per
        lane = lax.iota(I32, SC_LANES)

        @pl.loop(0, per, step=SC_LANES)
        def _(i):
            local[pl.ds(i, SC_LANES)] = (base + i + lane) & (n_tokens - 1)

        @pl.loop(0, n_assign // INVERT_CHUNK)
        def _(c):
            pltpu.sync_copy(dest_hbm.at[pl.ds(c * INVERT_CHUNK, INVERT_CHUNK)], staged)

            @pl.loop(0, INVERT_CHUNK, step=SC_LANES)
            def _(i):
                rel = staged[pl.ds(i, SC_LANES)] - base
                mine = (rel >= 0) & (rel < per)
                tok = (c * INVERT_CHUNK + i + lane) & (n_tokens - 1)
                plsc.store_scatter(local, [jnp.where(mine, rel, 0)], tok, mask=mine)

        pltpu.sync_copy(local, out_hbm.at[pl.ds(base, per)])

    return invert(dest_flat)


def _experts_kernel(be_ref, nv_ref, tok_hbm, h_ref, wg_ref, wu_ref, wd_ref, ys_ref,
                    xring, wgb, wub, wdb, tok_s, sem, tok_sem):
    j = pl.program_id(0)
    nv = nv_ref[0]
    per = wgb.shape[0] // LANES
    depth, bm = xring.shape[0], xring.shape[1] // per
    ahead = depth - 1
    chunk = tok_s.shape[1]
    cb = chunk // bm
    n_chunks = tok_hbm.shape[0] // chunk
    n_blocks = n_chunks * cb

    def tok_copy(c):
        return pltpu.make_async_copy(tok_hbm.at[pl.ds(pl.multiple_of(c * chunk, chunk), chunk)],
                                     tok_s.at[c % 2], tok_sem.at[c % 2])

    def gather(block, unrolled):
        ring = block % depth
        block = jnp.minimum(block, n_blocks - 1)
        slot, off = (block // cb) % 2, (block % cb) * bm
        buf, buf_sem = xring.at[ring], sem.at[ring]

        def issue(i, carry=None):
            _tile_copy(h_ref, tok_s[slot, off + i], buf, i, per, buf_sem).start()
            return carry

        if unrolled:
            for i in range(bm):
                _tile_copy(h_ref, tok_s[slot, off + i], buf, i, per, buf_sem).start(priority=i % 2)
        else:
            lax.fori_loop(0, bm, issue, 0, unroll=8)

    def wait_rows(ring):
        pltpu.make_async_copy(h_ref.at[pl.ds(0, bm * per)], xring.at[ring], sem.at[ring]).wait()

    @pl.when(j == 0)
    def _():
        tok_copy(0).start()
        tok_copy(0).wait()
        tok_copy(1).start()
        for b in range(ahead):
            gather(b, unrolled=False)

    first = j + ahead
    c_need = first // cb

    @pl.when(jnp.logical_and(jnp.logical_and(j > 0, first % cb == 0), c_need < n_chunks))
    def _():
        tok_copy(c_need).wait()

        @pl.when(c_need + 1 < n_chunks)
        def _():
            tok_copy(c_need + 1).start()

    jb = jnp.minimum(j, n_blocks - 1)
    fresh = jnp.logical_or(j == 0, be_ref[jb] != be_ref[jnp.maximum(jb - 1, 0)])

    @pl.when(jnp.logical_and(j < nv, fresh))
    def _():
        wgb[...] = wg_ref[...].astype(BF16)
        wub[...] = wu_ref[...].astype(BF16)
        wdb[...] = wd_ref[...].astype(BF16)

    @pl.when(j < nv)
    def _():
        wait_rows(j % depth)
        xb = _load_row_tiles(xring.at[j % depth], bm, per, BF16)
        act = (_silu(jnp.dot(xb, wgb[...], preferred_element_type=F32))
               * jnp.dot(xb, wub[...], preferred_element_type=F32))
        _store_row_tiles(ys_ref, jnp.dot(act.astype(BF16), wdb[...], preferred_element_type=F32))
        gather(j + ahead, unrolled=True)

    @pl.when(jnp.logical_and(j >= nv, j < nv + ahead))
    def _():
        wait_rows(j % depth)


def _experts(block_e, n_valid, row_tok, h2_tiles, wg, wu, wd):
    rows = row_tok.shape[0]
    d = wg.shape[1]
    per = d // LANES
    bm = EXPERT_ROWS
    n_blocks = rows // bm
    assert n_blocks % TOK_CHUNK_BLOCKS == 0 and n_blocks // TOK_CHUNK_BLOCKS >= 2
    assert GATHER_DEPTH - 1 < TOK_CHUNK_BLOCKS
    f = wg.shape[2]
    row_blk = lambda j, be, nv: (jnp.minimum(j, nv[0] - 1), 0)
    w_blk = lambda j, be, nv: (be[jnp.minimum(j, n_blocks - 1)], 0, 0)
    return pl.pallas_call(
        _experts_kernel,
        grid_spec=pltpu.PrefetchScalarGridSpec(
            num_scalar_prefetch=2,
            grid=(n_blocks + GATHER_DEPTH,),
            in_specs=[pl.BlockSpec(memory_space=pl.ANY),
                      pl.BlockSpec(memory_space=pl.ANY),
                      pl.BlockSpec((None, d, f), w_blk),
                      pl.BlockSpec((None, d, f), w_blk),
                      pl.BlockSpec((None, f, d), w_blk)],
            out_specs=pl.BlockSpec((bm * per, LANES), row_blk),
            scratch_shapes=[pltpu.VMEM((GATHER_DEPTH, bm * per, LANES), F32),
                            pltpu.VMEM((d, f), BF16), pltpu.VMEM((d, f), BF16),
                            pltpu.VMEM((f, d), BF16),
                            pltpu.SMEM((2, TOK_CHUNK_BLOCKS * bm), I32),
                            pltpu.SemaphoreType.DMA((GATHER_DEPTH,)), pltpu.SemaphoreType.DMA((2,))]),
        out_shape=jax.ShapeDtypeStruct((rows * per, LANES), F32),
        compiler_params=_params("arbitrary"),
        name="experts",
    )(block_e, n_valid, row_tok, h2_tiles, wg, wu, wd)


def _combine_kernel(dest_ref, gates_ref, base_ref, mod_ref, ys_ref, out_ref, buf, sem):
    tt, d = base_ref.shape
    per = d // LANES
    for t in range(tt):
        for k in range(TOP_K):
            _tile_copy(ys_ref, dest_ref[k, t], buf.at[k], t, per, sem).start(priority=k % 2)
    for k in range(TOP_K):
        pltpu.make_async_copy(ys_ref.at[pl.ds(0, tt * per)], buf.at[k], sem).wait()
    gates = gates_ref[...]
    routed = gates[:, 0:1] * _load_row_tiles(buf.at[0], tt, per, F32)
    for k in range(1, TOP_K):
        routed = routed + gates[:, k:k + 1] * _load_row_tiles(buf.at[k], tt, per, F32)
    out_ref[...] = base_ref[...] + mod_ref[5:6, :] * routed


def _combine(dest_t, gates, base, mod, ys, seq):
    n, d = base.shape
    tt = TT_COMBINE
    tiles_per_seq = seq // tt
    return pl.pallas_call(
        _combine_kernel,
        grid=(n // tt,),
        in_specs=[pl.BlockSpec((TOP_K, tt), lambda i: (0, i), memory_space=pltpu.SMEM),
                  pl.BlockSpec((tt, TOP_K), lambda i: (i, 0)),
                  pl.BlockSpec((tt, d), lambda i: (i, 0)),
                  pl.BlockSpec((None, 6, d), lambda i: (i // tiles_per_seq, 0, 0)),
                  pl.BlockSpec(memory_space=pl.ANY)],
        out_specs=pl.BlockSpec((tt, d), lambda i: (i, 0)),
        out_shape=jax.ShapeDtypeStruct((n, d), F32),
        scratch_shapes=[pltpu.VMEM((TOP_K, tt * d // LANES, LANES), F32), pltpu.SemaphoreType.DMA],
        compiler_params=_params("arbitrary"),
        name="combine",
    )(dest_t, gates, base, mod, ys)


def _layer(x, mod, pos, rope, p):
    nbatch, seq, d = x.shape
    n = nbatch * seq
    xf = x.reshape(n, d)
    invf, bd = rope

    perm = np.concatenate([np.arange(h * HEAD_DIM, (h + 1) * HEAD_DIM) for h in PAIR_ORDER_A])
    w_in = p["w_in"]
    w_in_p = jnp.concatenate([w_in[:, :MIX_A][:, perm], w_in[:, MIX_A:]], axis=1).astype(BF16)
    ones = lambda w: jnp.ones((w,), F32)
    qscale = HEAD_DIM ** -0.5
    gcol = jnp.concatenate([jnp.tile(p["g_q_a"], N_HEADS_A) * qscale, jnp.tile(p["g_k_a"], N_KV_A),
                            ones(KV_A), jnp.tile(p["g_q_b"], N_HEADS_B) * qscale,
                            jnp.tile(p["g_k_b"], N_HEADS_B), ones(MIX_B)]).reshape(1, IN_WIDTH)
    proj = _inproj(xf, mod, pos, p["g_norm_mix"].reshape(1, d), w_in_p, gcol, invf, bd, seq)
    qa, ka, va = proj[:3]
    qkv_b = {1: proj[3:6]}
    for j, dil in enumerate(DILS):
        qkv_b[dil] = [proj[6 + t * len(DILS) + j] for t in range(3)]

    sinks_p = p["sinks_a"][np.array(PAIR_ORDER_A)]
    oa, _ = _attention(qa, ka, va, nbatch=nbatch, seq=seq, dil=1, max_dist=WINDOW_A - 1,
                       kv_shared=True, sinks=sinks_p, want_lse=False)
    obs, lses = [], []
    for window, dil in DILATED_BRANCHES:
        o, lse = _attention(*qkv_b[dil], nbatch=nbatch, seq=seq, dil=dil, max_dist=window // dil,
                            kv_shared=False, want_lse=True)
        obs.append(o)
        lses.append(lse)

    goa = p["g_out_a"][perm].reshape(1, MIX_A)
    w_out = p["w_out"]
    wo_p = jnp.concatenate([w_out[:MIX_A][perm], w_out[MIX_A:]], axis=0).astype(BF16)
    h2, h2_tiles, base = _outproj(xf, mod, oa, obs, lses, goa, p["g_out_b"].reshape(1, MIX_B), wo_p,
                        p["g_norm_ffn"].reshape(1, d), p["w_gate_s"].astype(BF16),
                        p["w_up_s"].astype(BF16), p["w_down_s"].astype(BF16), seq)

    top_e_t, gates_t, counts = _router(h2, p["w_router"].T.astype(BF16),
                                       p["router_bias"].reshape(N_EXPERTS, 1))
    bm = EXPERT_ROWS
    counts = counts.reshape(N_EXPERTS).astype(I32)
    padded = (counts + bm - 1) // bm * bm
    pends = jnp.cumsum(padded)
    pstarts = pends - padded
    rows = n * TOP_K + N_EXPERTS * bm
    n_blocks = rows // bm
    n_valid = (pends[-1] // bm).astype(I32)
    blk = jnp.minimum(jnp.arange(n_blocks, dtype=I32), n_valid - 1)
    block_e = jnp.sum((pends[None, :] <= (blk * bm)[:, None]).astype(I32), axis=1)
    block_e = jnp.minimum(block_e, N_EXPERTS - 1)

    tri = (np.arange(TT_ROUTE)[:, None] <= np.arange(TT_ROUTE)[None, :])
    dest_t = _rank(top_e_t, pstarts.astype(F32).reshape(N_EXPERTS, 1), jnp.asarray(tri, BF16))
    n_valid = n_valid.reshape(1)
    row_tok = _invert(dest_t.reshape(n * TOP_K), n, rows)
    ys = _experts(block_e, n_valid, row_tok, h2_tiles, p["w_gate_e"], p["w_up_e"], p["w_down_e"])
    out = _combine(dest_t, gates_t.T, base, mod, ys, seq)
    return out.reshape(nbatch, seq, d)


def kernel(x, c, positions, w_ada, b_ada, g_norm_mix, w_in, g_q_a, g_k_a, sinks_a, g_q_b, g_k_b,
           g_out_a, g_out_b, w_out, g_norm_ffn, w_router, router_bias, w_gate_e, w_up_e, w_down_e,
           w_gate_s, w_up_s, w_down_s):
    nbatch, seq, d = x.shape
    depth = w_ada.shape[0]
    params = dict(g_norm_mix=g_norm_mix, w_in=w_in, g_q_a=g_q_a, g_k_a=g_k_a, sinks_a=sinks_a,
                  g_q_b=g_q_b, g_k_b=g_k_b, g_out_a=g_out_a, g_out_b=g_out_b, w_out=w_out,
                  g_norm_ffn=g_norm_ffn, w_router=w_router, router_bias=router_bias,
                  w_gate_e=w_gate_e, w_up_e=w_up_e, w_down_e=w_down_e, w_gate_s=w_gate_s,
                  w_up_s=w_up_s, w_down_s=w_down_s)
    j = np.arange(LANES) % HEAD_DIM
    inv = ROPE_THETA ** (-jnp.arange(0, ROT_DIM, 2, dtype=F32) / ROT_DIM)
    invf = jnp.where(j < ROT_DIM, inv[j % (ROT_DIM // 2)], 0.0).astype(F32).reshape(1, LANES)
    bd = jnp.asarray((np.arange(LANES)[:, None] // HEAD_DIM) == (np.arange(LANES)[None, :] // HEAD_DIM),
                     BF16)
    pos = positions.reshape(nbatch * seq, 1).astype(I32)
    for l in range(depth):
        mod = _adaln(c.astype(F32), w_ada[l], b_ada[l]).reshape(nbatch, 6, d)
        x = _layer(x, mod, pos, (invf, bd), {k: v[l] for k, v in params.items()})
    return x
```

```python
import functools

import numpy as np
import jax
import jax.numpy as jnp
from jax import lax
from jax.experimental import pallas as pl
from jax.experimental.pallas import tpu as pltpu
from jax.experimental.pallas import tpu_sc as plsc

F32 = jnp.float32
BF16 = jnp.bfloat16
I32 = jnp.int32

HEAD_DIM = 64
N_HEADS_A = 8
N_KV_A = 2
WINDOW_A = 128
N_HEADS_B = 8
DILATED_BRANCHES = ((128, 1), (512, 4), (2048, 16))
DILS = tuple(dil for _, dil in DILATED_BRANCHES if dil > 1)
BLOCK = 128
ROT_DIM = HEAD_DIM // 4
ROPE_THETA = 500000.0
MIX_A = N_HEADS_A * HEAD_DIM
KV_A = N_KV_A * HEAD_DIM
MIX_B = N_HEADS_B * HEAD_DIM
N_EXPERTS = 256
TOP_K = 8
N_GROUPS = 8
TOPK_GROUPS = 4
GROUP_SIZE = N_EXPERTS // N_GROUPS
ROUTED_SCALE = 2.5
EPS = 1e-6

LANES = 128
HEADS_PER_VREG = LANES // HEAD_DIM
N_PAIRS = MIX_A // LANES
NEG = -1e30
VMEM_LIMIT = 48 * 1024 * 1024

TT_PROJ = 512
TT_ROUTE = 256
TT_DISPATCH = 256
TT_COMBINE = 128
EXPERT_ROWS = 256
TOK_CHUNK_BLOCKS = 32
GATHER_DEPTH = 4
SUBLANES = 8

PAIR_ORDER_A = tuple(h for p in range(N_PAIRS) for h in (p, p + N_HEADS_A // N_KV_A))


def _params(*sem):
    return pltpu.CompilerParams(dimension_semantics=sem, vmem_limit_bytes=VMEM_LIMIT)


def _silu(t):
    return t / (1.0 + jnp.exp(-t))


def _rms_rows(t):
    return t * lax.rsqrt(jnp.mean(t * t, axis=-1, keepdims=True) + EPS)


def _store_row_tiles(ref, value):
    rows, d = value.shape
    per = d // LANES
    for s in range(per):
        ref[pl.ds(s, rows, stride=per), :] = value[:, s * LANES:(s + 1) * LANES]


def _load_row_tiles(ref, rows, per, dtype):
    return jnp.concatenate([ref[pl.ds(s, rows, stride=per), :].astype(dtype) for s in range(per)], axis=1)


def _tile_copy(src_ref, src_row, dst_ref, dst_row, per, sem):
    start = lambda row: row * per if isinstance(row, int) else pl.multiple_of(row * per, per)
    return pltpu.make_async_copy(src_ref.at[pl.ds(start(src_row), per)],
                                 dst_ref.at[pl.ds(start(dst_row), per)], sem)


def _ada_kernel(c_ref, w_ref, b_ref, o_ref):
    cond = _silu(c_ref[...])
    o_ref[...] = jnp.dot(cond.astype(BF16), w_ref[...].astype(BF16),
                         preferred_element_type=F32) + b_ref[...]


def _adaln(c, w_ada, b_ada):
    nb, d = c.shape
    width = w_ada.shape[1]
    tn = 1024
    return pl.pallas_call(
        _ada_kernel,
        grid=(width // tn,),
        in_specs=[pl.BlockSpec((nb, d), lambda j: (0, 0)),
                  pl.BlockSpec((d, tn), lambda j: (0, j)),
                  pl.BlockSpec((1, tn), lambda j: (0, j))],
        out_specs=pl.BlockSpec((nb, tn), lambda j: (0, j)),
        out_shape=jax.ShapeDtypeStruct((nb, width), F32),
        compiler_params=_params("arbitrary"),
        name="adaln",
    )(c, w_ada, b_ada.reshape(1, width))


COL_QA, COL_KA, COL_VA = 0, MIX_A, MIX_A + KV_A
COL_QB = MIX_A + 2 * KV_A
COL_KB, COL_VB = COL_QB + MIX_B, COL_QB + 2 * MIX_B
IN_WIDTH = COL_VB + MIX_B


def _inproj_kernel(x_ref, mod_ref, pos_ref, gn_ref, w_ref, gcol_ref, invf_ref, bd_ref,
                   qa_ref, ka_ref, va_ref, qb_ref, kb_ref, vb_ref, *rest):
    n_dil = len(DILS)
    dil_refs = [rest[i * n_dil:(i + 1) * n_dil] for i in range(3)]
    qb_scr, kb_scr, vb_scr = rest[3 * n_dil:]
    tt = x_ref.shape[0]
    shift, scale = mod_ref[0:1, :], mod_ref[1:2, :]
    h = _rms_rows(x_ref[...]) * gn_ref[...] * (1.0 + scale) + shift
    proj = jnp.dot(h.astype(BF16), w_ref[...], preferred_element_type=F32)

    ang = pos_ref[...].astype(F32) * invf_ref[...]
    cs, sn = jnp.cos(ang), jnp.sin(ang)
    lane = lax.broadcasted_iota(I32, (1, LANES), 1) % HEAD_DIM
    s_lo = jnp.where(lane < ROT_DIM // 2, -sn, 0.0)
    s_hi = jnp.where((lane >= ROT_DIM // 2) & (lane < ROT_DIM), sn, 0.0)
    bd = bd_ref[...]

    def norm_rope(col0, width, out_ref, scr=None):
        for j in range(width // LANES):
            c = col0 + j * LANES
            t = proj[:, c:c + LANES]
            sq = t * t
            hi = sq.astype(BF16)
            lo = (sq - hi.astype(F32)).astype(BF16)
            ss = (jnp.dot(hi, bd, preferred_element_type=F32)
                  + jnp.dot(lo, bd, preferred_element_type=F32))
            t = t * lax.rsqrt(ss * (1.0 / HEAD_DIM) + EPS) * gcol_ref[:, c:c + LANES]
            t = (t * cs + pltpu.roll(t, LANES - ROT_DIM // 2, 1) * s_lo
                 + pltpu.roll(t, ROT_DIM // 2, 1) * s_hi)
            out_ref[:, j * LANES:(j + 1) * LANES] = t.astype(BF16)
            if scr is not None:
                scr[j] = t

    norm_rope(COL_QA, MIX_A, qa_ref)
    norm_rope(COL_KA, KV_A, ka_ref)
    norm_rope(COL_QB, MIX_B, qb_ref, qb_scr)
    norm_rope(COL_KB, MIX_B, kb_ref, kb_scr)
    va_ref[...] = proj[:, COL_VA:COL_VA + KV_A].astype(BF16)
    vb_ref[...] = proj[:, COL_VB:COL_VB + MIX_B].astype(BF16)
    for j in range(N_PAIRS):
        vb_scr[j] = proj[:, COL_VB + j * LANES:COL_VB + (j + 1) * LANES]
    for scr, outs in zip((qb_scr, kb_scr, vb_scr), dil_refs):
        for dil, out in zip(DILS, outs):
            for r in range(dil):
                for j in range(N_PAIRS):
                    c = r * MIX_B + j * LANES
                    out[:, c:c + LANES] = scr[j, pl.ds(r, tt // dil, stride=dil), :].astype(BF16)


def _inproj(xf, mod, pos, g_norm, w_in_p, gcol, invf, bd, seq):
    n, d = xf.shape
    tt = TT_PROJ
    tiles_per_seq = seq // tt
    shapes = [(n, w) for w in (MIX_A, KV_A, KV_A, MIX_B, MIX_B, MIX_B)]
    shapes += [(n // dil, dil * MIX_B) for _ in range(3) for dil in DILS]
    full = lambda shape: pl.BlockSpec(shape, lambda i: (0,) * len(shape))
    return pl.pallas_call(
        _inproj_kernel,
        grid=(n // tt,),
        in_specs=[pl.BlockSpec((tt, d), lambda i: (i, 0)),
                  pl.BlockSpec((None, 6, d), lambda i: (i // tiles_per_seq, 0, 0)),
                  pl.BlockSpec((tt, 1), lambda i: (i, 0)),
                  full((1, d)), full((d, IN_WIDTH)), full((1, IN_WIDTH)),
                  full((1, LANES)), full((LANES, LANES))],
        out_specs=[pl.BlockSpec((tt * r // n, w), lambda i: (i, 0)) for r, w in shapes],
        out_shape=[jax.ShapeDtypeStruct(s, BF16) for s in shapes],
        scratch_shapes=[pltpu.VMEM((N_PAIRS, tt, LANES), F32)] * 3,
        compiler_params=_params("parallel"),
        name="inproj",
    )(xf, mod, pos, g_norm, w_in_p, gcol, invf, bd)


def _attn_kernel(*refs, kv_shared, max_dist, use_prev, has_sinks, want_lse):
    refs = list(refs)
    sink_ref = refs.pop(0) if has_sinks else None
    q_ref = refs.pop(0)
    kp_ref = refs.pop(0) if use_prev else None
    kc_ref = refs.pop(0)
    vp_ref = refs.pop(0) if use_prev else None
    vc_ref = refs.pop(0)
    o_ref = refs.pop(0)
    lse_ref = refs.pop(0) if want_lse else None

    blk = pl.program_id(2)
    nq = 2 * BLOCK
    nk = 2 * BLOCK if use_prev else BLOCK
    qpos = lax.broadcasted_iota(I32, (nq, nk), 0) % BLOCK
    kpos = lax.broadcasted_iota(I32, (nq, nk), 1)
    if use_prev:
        dist = qpos + BLOCK - kpos
        valid = (dist >= 0) & (dist <= max_dist) & ((kpos >= BLOCK) | (blk > 0))
    else:
        dist = qpos - kpos
        valid = (dist >= 0) & (dist <= max_dist)
    lane = lax.broadcasted_iota(I32, (nq, LANES), 1)
    row = lax.broadcasted_iota(I32, (nq, LANES), 0)
    own_half = (lane < HEAD_DIM) == (row < BLOCK)
    left_lanes = lax.broadcasted_iota(I32, (BLOCK, LANES), 1) < HEAD_DIM
    lane8 = lax.broadcasted_iota(I32, (BLOCK, 2 * N_PAIRS), 1)
    lse_blk = jnp.zeros((BLOCK, 2 * N_PAIRS), F32)

    for p in range(N_PAIRS):
        cq = slice(p * LANES, (p + 1) * LANES)
        ck = slice(0, LANES) if kv_shared else cq
        qp = q_ref[:, cq]
        qs = jnp.concatenate([qp, qp], axis=0)
        qs = jnp.where(own_half, qs, jnp.zeros_like(qs))
        if use_prev:
            k = jnp.concatenate([kp_ref[:, ck], kc_ref[:, ck]], axis=0)
            v = jnp.concatenate([vp_ref[:, ck], vc_ref[:, ck]], axis=0)
        else:
            k, v = kc_ref[:, ck], vc_ref[:, ck]
        s = lax.dot_general(qs, k, (((1,), (1,)), ((), ())), preferred_element_type=F32)
        s = jnp.where(valid, s, NEG)
        m = jnp.max(s, axis=-1, keepdims=True)
        if has_sinks:
            rows1 = lax.broadcasted_iota(I32, (nq, 1), 0)
            sink = jnp.where(rows1 < BLOCK, sink_ref[2 * p], sink_ref[2 * p + 1])
            m = jnp.maximum(m, sink)
        e = jnp.exp(s - m)
        l = jnp.sum(e, axis=-1, keepdims=True)
        if has_sinks:
            l = l + jnp.exp(sink - m)
        o = jnp.dot(e.astype(BF16), v, preferred_element_type=F32) / l
        o_ref[:, cq] = jnp.where(left_lanes, o[:BLOCK], o[BLOCK:]).astype(BF16)
        if want_lse:
            lse = m + jnp.log(l)
            lse_blk = (lse_blk + jnp.where(lane8 == 2 * p, lse[:BLOCK], 0.0)
                       + jnp.where(lane8 == 2 * p + 1, lse[BLOCK:], 0.0))
    if want_lse:
        lse_ref[...] = lse_blk


def _attention(q, k, v, *, nbatch, seq, dil, max_dist, kv_shared, sinks=None, want_lse):
    length = seq // dil
    nblk = length // BLOCK
    use_prev = nblk > 1
    kw = k.shape[1] // dil
    view = lambda t: t.reshape(nbatch, length, t.shape[1])
    cur = lambda b, r, i: (b, i, r)
    prev = lambda b, r, i: (b, jnp.maximum(i - 1, 0), r)
    in_specs, args = [], []
    if sinks is not None:
        in_specs.append(pl.BlockSpec(memory_space=pltpu.SMEM))
        args.append(sinks)
    in_specs.append(pl.BlockSpec((None, BLOCK, MIX_B), cur))
    args.append(view(q))
    for t in (k, v):
        if use_prev:
            in_specs.append(pl.BlockSpec((None, BLOCK, kw), prev))
            args.append(view(t))
        in_specs.append(pl.BlockSpec((None, BLOCK, kw), cur))
        args.append(view(t))
    out_specs = [pl.BlockSpec((None, BLOCK, MIX_B), cur)]
    out_shape = [jax.ShapeDtypeStruct((nbatch, length, dil * MIX_B), BF16)]
    if want_lse:
        out_specs.append(pl.BlockSpec((None, None, BLOCK, N_HEADS_B), lambda b, r, i: (b, r, i, 0)))
        out_shape.append(jax.ShapeDtypeStruct((nbatch, dil, length, N_HEADS_B), F32))
    outs = pl.pallas_call(
        functools.partial(_attn_kernel, kv_shared=kv_shared, max_dist=max_dist, use_prev=use_prev,
                          has_sinks=sinks is not None, want_lse=want_lse),
        grid=(nbatch, dil, nblk),
        in_specs=in_specs, out_specs=out_specs, out_shape=out_shape,
        compiler_params=_params("parallel", "parallel", "arbitrary"),
        name=f"attn_d{dil}" + ("_swa" if kv_shared else ""),
    )(*args)
    o = outs[0].reshape(nbatch * length, dil * MIX_B)
    if not want_lse:
        return o, None
    lse = outs[1].transpose(0, 2, 1, 3).reshape(nbatch * seq, N_HEADS_B)
    return o, lse


def _expand_heads(w, width):
    head = lax.broadcasted_iota(I32, (1, width), 1) // HEAD_DIM
    out = jnp.zeros((w.shape[0], width), F32)
    for hd in range(w.shape[1]):
        out = jnp.where(head == hd, w[:, hd:hd + 1], out)
    return out


def _outproj_kernel(x_ref, mod_ref, oa_ref, ob1_ref, ob2_ref, ob3_ref, l1_ref, l2_ref, l3_ref,
                    goa_ref, gob_ref, wo_ref, gf_ref, wgs_ref, wus_ref, wds_ref,
                    h2_ref, h2t_ref, base_ref, ob_scr):
    tt = x_ref.shape[0]
    gate_a = mod_ref[2:3, :]
    shift_m, scale_m, gate_m = mod_ref[3:4, :], mod_ref[4:5, :], mod_ref[5:6, :]

    def token_major(ref, dil):
        if dil == 1:
            return ref[...].astype(F32)
        for r in range(dil):
            for j in range(N_PAIRS):
                c = r * MIX_B + j * LANES
                ob_scr[j, pl.ds(r, tt // dil, stride=dil), :] = ref[:, c:c + LANES].astype(F32)
        return jnp.concatenate([ob_scr[j] for j in range(N_PAIRS)], axis=1)

    l1, l2, l3 = l1_ref[...], l2_ref[...], l3_ref[...]
    mx = jnp.maximum(jnp.maximum(l1, l2), l3)
    e1, e2, e3 = jnp.exp(l1 - mx), jnp.exp(l2 - mx), jnp.exp(l3 - mx)
    den = e1 + e2 + e3
    dils = [dil for _, dil in DILATED_BRANCHES]
    ob = _expand_heads(e1 / den, MIX_B) * token_major(ob1_ref, dils[0])
    ob = ob + _expand_heads(e2 / den, MIX_B) * token_major(ob2_ref, dils[1])
    ob = ob + _expand_heads(e3 / den, MIX_B) * token_major(ob3_ref, dils[2])
    ob = _rms_rows(ob) * gob_ref[...]
    oa = _rms_rows(oa_ref[...].astype(F32)) * goa_ref[...]
    y = (jnp.dot(oa.astype(BF16), wo_ref[0:MIX_A, :], preferred_element_type=F32)
         + jnp.dot(ob.astype(BF16), wo_ref[MIX_A:MIX_A + MIX_B, :], preferred_element_type=F32))
    x1 = x_ref[...] + gate_a * y
    h2 = _rms_rows(x1) * gf_ref[...] * (1.0 + scale_m) + shift_m
    h2_ref[...] = h2
    _store_row_tiles(h2t_ref, h2)
    hb = h2.astype(BF16)
    act = (_silu(jnp.dot(hb, wgs_ref[...], preferred_element_type=F32))
           * jnp.dot(hb, wus_ref[...], preferred_element_type=F32))
    shared = jnp.dot(act.astype(BF16), wds_ref[...], preferred_element_type=F32)
    base_ref[...] = x1 + gate_m * shared


def _outproj(xf, mod, oa, obs, lses, goa, gob, wo_p, gf, wgs, wus, wds, seq):
    n, d = xf.shape
    tt = TT_PROJ // 2
    tiles_per_seq = seq // tt
    tile = lambda w: pl.BlockSpec((tt, w), lambda i: (i, 0))
    full = lambda shape: pl.BlockSpec(shape, lambda i: (0,) * len(shape))
    sd = wgs.shape[1]
    dilated = [pl.BlockSpec((tt // dil, dil * MIX_B), lambda i: (i, 0)) for _, dil in DILATED_BRANCHES]
    return pl.pallas_call(
        _outproj_kernel,
        grid=(n // tt,),
        in_specs=[tile(d), pl.BlockSpec((None, 6, d), lambda i: (i // tiles_per_seq, 0, 0)),
                  tile(MIX_A), *dilated,
                  tile(N_HEADS_B), tile(N_HEADS_B), tile(N_HEADS_B),
                  full((1, MIX_A)), full((1, MIX_B)), full((MIX_A + MIX_B, d)), full((1, d)),
                  full((d, sd)), full((d, sd)), full((sd, d))],
        out_specs=[tile(d), pl.BlockSpec((tt * d // LANES, LANES), lambda i: (i, 0)), tile(d)],
        out_shape=[jax.ShapeDtypeStruct((n, d), F32), jax.ShapeDtypeStruct((n * d // LANES, LANES), F32),
                   jax.ShapeDtypeStruct((n, d), F32)],
        scratch_shapes=[pltpu.VMEM((N_PAIRS, tt, LANES), F32)],
        compiler_params=_params("parallel"),
        name="outproj",
    )(xf, mod, oa, *obs, *lses, goa, gob, wo_p, gf, wgs, wus, wds)


def _router_kernel(h_ref, wrt_ref, bias_ref, e_ref, g_ref, cnt_ref):
    tt = h_ref.shape[0]
    logits = lax.dot_general(wrt_ref[...], h_ref[...].astype(BF16), (((1,), (1,)), ((), ())),
                             preferred_element_type=F32)
    scores = 1.0 / (1.0 + jnp.exp(-logits))
    biased = scores + bias_ref[...]
    ninf = -jnp.inf

    j32 = lax.broadcasted_iota(I32, (GROUP_SIZE, tt), 0).astype(F32)
    grp = []
    for g in range(N_GROUPS):
        bg = biased[g * GROUP_SIZE:(g + 1) * GROUP_SIZE, :]
        m1 = jnp.max(bg, axis=0, keepdims=True)
        i1 = jnp.min(jnp.where(bg == m1, j32, float(GROUP_SIZE)), axis=0, keepdims=True)
        m2 = jnp.max(jnp.where(j32 == i1, ninf, bg), axis=0, keepdims=True)
        grp.append(m1 + m2)
    grp = jnp.concatenate(grp, axis=0)
    g8 = lax.broadcasted_iota(I32, (N_GROUPS, tt), 0).astype(F32)
    chosen = jnp.zeros((N_GROUPS, tt), F32)
    for _ in range(TOPK_GROUPS):
        gm = jnp.max(grp, axis=0, keepdims=True)
        gi = jnp.min(jnp.where(grp == gm, g8, float(N_GROUPS)), axis=0, keepdims=True)
        hit = g8 == gi
        chosen = jnp.where(hit, 1.0, chosen)
        grp = jnp.where(hit, ninf, grp)
    masked = jnp.concatenate(
        [jnp.where(chosen[g:g + 1, :] > 0.0, biased[g * GROUP_SIZE:(g + 1) * GROUP_SIZE, :], ninf)
         for g in range(N_GROUPS)], axis=0)

    eio = lax.broadcasted_iota(I32, (N_EXPERTS, tt), 0).astype(F32)
    picked = jnp.zeros((N_EXPERTS, tt), F32)
    es, gs = [], []
    for _ in range(TOP_K):
        m = jnp.max(masked, axis=0, keepdims=True)
        idx = jnp.min(jnp.where(masked == m, eio, float(N_EXPERTS)), axis=0, keepdims=True)
        hit = eio == idx
        gs.append(jnp.sum(jnp.where(hit, scores, 0.0), axis=0, keepdims=True))
        es.append(idx)
        picked = jnp.where(hit, 1.0, picked)
        masked = jnp.where(hit, ninf, masked)
    gates = jnp.concatenate(gs, axis=0)
    e_ref[...] = jnp.concatenate(es, axis=0).astype(I32)
    g_ref[...] = gates / jnp.sum(gates, axis=0, keepdims=True) * ROUTED_SCALE

    @pl.when(pl.program_id(0) == 0)
    def _():
        cnt_ref[...] = jnp.zeros_like(cnt_ref)
    cnt_ref[...] += jnp.sum(picked, axis=1, keepdims=True)


def _router(h2, wrt, bias_col):
    n, d = h2.shape
    tt = TT_ROUTE
    return pl.pallas_call(
        _router_kernel,
        grid=(n // tt,),
        in_specs=[pl.BlockSpec((tt, d), lambda i: (i, 0)),
                  pl.BlockSpec((N_EXPERTS, d), lambda i: (0, 0)),
                  pl.BlockSpec((N_EXPERTS, 1), lambda i: (0, 0))],
        out_specs=[pl.BlockSpec((TOP_K, tt), lambda i: (0, i)),
                   pl.BlockSpec((TOP_K, tt), lambda i: (0, i)),
                   pl.BlockSpec((N_EXPERTS, 1), lambda i: (0, 0))],
        out_shape=[jax.ShapeDtypeStruct((TOP_K, n), I32),
                   jax.ShapeDtypeStruct((TOP_K, n), F32),
                   jax.ShapeDtypeStruct((N_EXPERTS, 1), F32)],
        compiler_params=_params("arbitrary"),
        name="router",
    )(h2, wrt, bias_col)


def _rank_kernel(e_ref, pstart_ref, tri_ref, dest_ref, carry_ref):
    tt = e_ref.shape[1]

    @pl.when(pl.program_id(0) == 0)
    def _():
        carry_ref[...] = pstart_ref[...]

    e = e_ref[...]
    eio = lax.broadcasted_iota(I32, (N_EXPERTS, tt), 0)
    mask = jnp.zeros((N_EXPERTS, tt), F32)
    for k in range(TOP_K):
        mask = jnp.where(eio == e[k:k + 1, :], 1.0, mask)
    incl = jnp.dot(mask.astype(BF16), tri_ref[...], preferred_element_type=F32)
    pos = incl - mask + carry_ref[...]
    dest = [jnp.sum(jnp.where(eio == e[k:k + 1, :], pos, 0.0), axis=0, keepdims=True)
            for k in range(TOP_K)]
    dest_ref[...] = jnp.concatenate(dest, axis=0).astype(I32)
    carry_ref[...] += incl[:, tt - 1:tt]


def _rank(top_e_t, pstart_col, tri):
    n = top_e_t.shape[1]
    tt = TT_ROUTE
    return pl.pallas_call(
        _rank_kernel,
        grid=(n // tt,),
        in_specs=[pl.BlockSpec((TOP_K, tt), lambda i: (0, i)),
                  pl.BlockSpec((N_EXPERTS, 1), lambda i: (0, 0)),
                  pl.BlockSpec((tt, tt), lambda i: (0, 0))],
        out_specs=pl.BlockSpec((TOP_K, tt), lambda i: (0, i)),
        out_shape=jax.ShapeDtypeStruct((TOP_K, n), I32),
        scratch_shapes=[pltpu.VMEM((N_EXPERTS, 1), F32)],
        compiler_params=_params("arbitrary"),
        name="rank",
    )(top_e_t, pstart_col, tri)


def _row_copy(src_ref, src_row, dst_ref, dst_row, sem):
    return pltpu.make_async_copy(src_ref.at[pl.ds(src_row, 1)], dst_ref.at[pl.ds(dst_row, 1)], sem)


SC_CORES, SC_SUBCORES, SC_LANES = 2, 16, 16
SC_WORKERS = SC_CORES * SC_SUBCORES
INVERT_CHUNK = 8192


def _invert(dest_flat, n_tokens, rows):
    per = rows // SC_WORKERS
    n_assign = dest_flat.shape[0]
    assert rows % (SC_WORKERS * SC_LANES) == 0 and n_assign % INVERT_CHUNK == 0
    assert n_tokens & (n_tokens - 1) == 0

    @functools.partial(
        pl.kernel, mesh=plsc.VectorSubcoreMesh(core_axis_name="c", subcore_axis_name="s"),
        out_type=jax.ShapeDtypeStruct((rows,), I32),
        scratch_types=[pltpu.VMEM((INVERT_CHUNK,), I32), pltpu.VMEM((per,), I32)],
        compiler_params=pltpu.CompilerParams(needs_layout_passes=False))
    def invert(dest_hbm, out_hbm, staged, local):
        base = (lax.axis_index("s") * SC_CORES + lax.axis_index("c")) * per
        lane = lax.iota(I32, SC_LANES)

        @pl.loop(0, per, step=SC_LANES)
        def _(i):
            local[pl.ds(i, SC_LANES)] = (base + i + lane) & (n_tokens - 1)

        @pl.loop(0, n_assign // INVERT_CHUNK)
        def _(c):
            pltpu.sync_copy(dest_hbm.at[pl.ds(c * INVERT_CHUNK, INVERT_CHUNK)], staged)

            @pl.loop(0, INVERT_CHUNK, step=SC_LANES)
            def _(i):
                rel = staged[pl.ds(i, SC_LANES)] - base
                mine = (rel >= 0) & (rel < per)
                tok = (c * INVERT_CHUNK + i + lane) & (n_tokens - 1)
                plsc.store_scatter(local, [jnp.where(mine, rel, 0)], tok, mask=mine)

        pltpu.sync_copy(local, out_hbm.at[pl.ds(base, per)])

    return invert(dest_flat)


def _experts_kernel(be_ref, nv_ref, tok_hbm, h_ref, wg_ref, wu_ref, wd_ref, ys_ref, *scratch):
    xbufs = scratch[:GATHER_DEPTH]
    wgb, wub, wdb, tok_s, sem, tok_sem = scratch[GATHER_DEPTH:]
    j = pl.program_id(0)
    nv = nv_ref[0]
    per = wgb.shape[0] // LANES
    depth, bm = len(xbufs), xbufs[0].shape[0] // per
    ahead = depth - 1
    chunk = tok_s.shape[1]
    cb = chunk // bm
    n_chunks = tok_hbm.shape[0] // chunk
    n_blocks = n_chunks * cb

    def tok_copy(c):
        return pltpu.make_async_copy(tok_hbm.at[pl.ds(pl.multiple_of(c * chunk, chunk), chunk)],
                                     tok_s.at[c % 2], tok_sem.at[c % 2])

    def gather(block, ring, unrolled):
        block = jnp.minimum(block, n_blocks - 1)
        slot, off = (block // cb) % 2, (block % cb) * bm
        buf, buf_sem = xbufs[ring], sem.at[ring]

        def issue(i, carry=None):
            _tile_copy(h_ref, tok_s[slot, off + i], buf, i, per, buf_sem).start()
            return carry

        if unrolled:
            for i in range(bm):
                _tile_copy(h_ref, tok_s[slot, off + i], buf, i, per, buf_sem).start(priority=i % 2)
        else:
            lax.fori_loop(0, bm, issue, 0, unroll=8)

    def wait_rows(ring):
        pltpu.make_async_copy(h_ref.at[pl.ds(0, bm * per)], xbufs[ring], sem.at[ring]).wait()

    @pl.when(j == 0)
    def _():
        tok_copy(0).start()
        tok_copy(0).wait()
        tok_copy(1).start()
        for b in range(ahead):
            gather(b, b, unrolled=False)

    first = j + ahead
    c_need = first // cb

    @pl.when(jnp.logical_and(jnp.logical_and(j > 0, first % cb == 0), c_need < n_chunks))
    def _():
        tok_copy(c_need).wait()

        @pl.when(c_need + 1 < n_chunks)
        def _():
            tok_copy(c_need + 1).start()

    jb = jnp.minimum(j, n_blocks - 1)
    fresh = jnp.logical_or(j == 0, be_ref[jb] != be_ref[jnp.maximum(jb - 1, 0)])

    @pl.when(jnp.logical_and(j < nv, fresh))
    def _():
        wgb[...] = wg_ref[...].astype(BF16)
        wub[...] = wu_ref[...].astype(BF16)
        wdb[...] = wd_ref[...].astype(BF16)

    for ring in range(depth):
        mine = j % depth == ring

        @pl.when(jnp.logical_and(j < nv, mine))
        def _():
            wait_rows(ring)
            gather(j + ahead, (ring + ahead) % depth, unrolled=True)
            xb = _load_row_tiles(xbufs[ring], bm, per, BF16)
            act = (_silu(jnp.dot(xb, wgb[...], preferred_element_type=F32))
                   * jnp.dot(xb, wub[...], preferred_element_type=F32))
            _store_row_tiles(ys_ref, jnp.dot(act.astype(BF16), wdb[...], preferred_element_type=F32))

        @pl.when(jnp.logical_and(jnp.logical_and(j >= nv, j < nv + ahead), mine))
        def _():
            wait_rows(ring)


def _experts(block_e, n_valid, row_tok, h2_tiles, wg, wu, wd):
    rows = row_tok.shape[0]
    d = wg.shape[1]
    per = d // LANES
    bm = EXPERT_ROWS
    n_blocks = rows // bm
    assert n_blocks % TOK_CHUNK_BLOCKS == 0 and n_blocks // TOK_CHUNK_BLOCKS >= 2
    assert GATHER_DEPTH - 1 < TOK_CHUNK_BLOCKS
    f = wg.shape[2]
    row_blk = lambda j, be, nv: (jnp.minimum(j, nv[0] - 1), 0)
    w_blk = lambda j, be, nv: (be[jnp.minimum(j, n_blocks - 1)], 0, 0)
    return pl.pallas_call(
        _experts_kernel,
        grid_spec=pltpu.PrefetchScalarGridSpec(
            num_scalar_prefetch=2,
            grid=(n_blocks + GATHER_DEPTH,),
            in_specs=[pl.BlockSpec(memory_space=pl.ANY),
                      pl.BlockSpec(memory_space=pl.ANY),
                      pl.BlockSpec((None, d, f), w_blk),
                      pl.BlockSpec((None, d, f), w_blk),
                      pl.BlockSpec((None, f, d), w_blk)],
            out_specs=pl.BlockSpec((bm * per, LANES), row_blk),
            scratch_shapes=[pltpu.VMEM((bm * per, LANES), F32)] * GATHER_DEPTH + [
                            pltpu.VMEM((d, f), BF16), pltpu.VMEM((d, f), BF16),
                            pltpu.VMEM((f, d), BF16),
                            pltpu.SMEM((2, TOK_CHUNK_BLOCKS * bm), I32),
                            pltpu.SemaphoreType.DMA((GATHER_DEPTH,)), pltpu.SemaphoreType.DMA((2,))]),
        out_shape=jax.ShapeDtypeStruct((rows * per, LANES), F32),
        compiler_params=_params("arbitrary"),
        name="experts",
    )(block_e, n_valid, row_tok, h2_tiles, wg, wu, wd)


def _combine_kernel(dest_ref, gates_ref, base_ref, mod_ref, ys_ref, out_ref, buf, sem):
    tt, d = base_ref.shape
    per = d // LANES
    for t in range(tt):
        for k in range(TOP_K):
            _tile_copy(ys_ref, dest_ref[k, t], buf.at[k], t, per, sem).start(priority=k % 2)
    for k in range(TOP_K):
        pltpu.make_async_copy(ys_ref.at[pl.ds(0, tt * per)], buf.at[k], sem).wait()
    gates = gates_ref[...]
    routed = gates[:, 0:1] * _load_row_tiles(buf.at[0], tt, per, F32)
    for k in range(1, TOP_K):
        routed = routed + gates[:, k:k + 1] * _load_row_tiles(buf.at[k], tt, per, F32)
    out_ref[...] = base_ref[...] + mod_ref[5:6, :] * routed


def _combine(dest_t, gates, base, mod, ys, seq):
    n, d = base.shape
    tt = TT_COMBINE
    tiles_per_seq = seq // tt
    return pl.pallas_call(
        _combine_kernel,
        grid=(n // tt,),
        in_specs=[pl.BlockSpec((TOP_K, tt), lambda i: (0, i), memory_space=pltpu.SMEM),
                  pl.BlockSpec((tt, TOP_K), lambda i: (i, 0)),
                  pl.BlockSpec((tt, d), lambda i: (i, 0)),
                  pl.BlockSpec((None, 6, d), lambda i: (i // tiles_per_seq, 0, 0)),
                  pl.BlockSpec(memory_space=pl.ANY)],
        out_specs=pl.BlockSpec((tt, d), lambda i: (i, 0)),
        out_shape=jax.ShapeDtypeStruct((n, d), F32),
        scratch_shapes=[pltpu.VMEM((TOP_K, tt * d // LANES, LANES), F32), pltpu.SemaphoreType.DMA],
        compiler_params=_params("arbitrary"),
        name="combine",
    )(dest_t, gates, base, mod, ys)


def _layer(x, mod, pos, rope, p):
    nbatch, seq, d = x.shape
    n = nbatch * seq
    xf = x.reshape(n, d)
    invf, bd = rope

    perm = np.concatenate([np.arange(h * HEAD_DIM, (h + 1) * HEAD_DIM) for h in PAIR_ORDER_A])
    w_in = p["w_in"]
    w_in_p = jnp.concatenate([w_in[:, :MIX_A][:, perm], w_in[:, MIX_A:]], axis=1).astype(BF16)
    ones = lambda w: jnp.ones((w,), F32)
    qscale = HEAD_DIM ** -0.5
    gcol = jnp.concatenate([jnp.tile(p["g_q_a"], N_HEADS_A) * qscale, jnp.tile(p["g_k_a"], N_KV_A),
                            ones(KV_A), jnp.tile(p["g_q_b"], N_HEADS_B) * qscale,
                            jnp.tile(p["g_k_b"], N_HEADS_B), ones(MIX_B)]).reshape(1, IN_WIDTH)
    proj = _inproj(xf, mod, pos, p["g_norm_mix"].reshape(1, d), w_in_p, gcol, invf, bd, seq)
    qa, ka, va = proj[:3]
    qkv_b = {1: proj[3:6]}
    for j, dil in enumerate(DILS):
        qkv_b[dil] = [proj[6 + t * len(DILS) + j] for t in range(3)]

    sinks_p = p["sinks_a"][np.array(PAIR_ORDER_A)]
    oa, _ = _attention(qa, ka, va, nbatch=nbatch, seq=seq, dil=1, max_dist=WINDOW_A - 1,
                       kv_shared=True, sinks=sinks_p, want_lse=False)
    obs, lses = [], []
    for window, dil in DILATED_BRANCHES:
        o, lse = _attention(*qkv_b[dil], nbatch=nbatch, seq=seq, dil=dil, max_dist=window // dil,
                            kv_shared=False, want_lse=True)
        obs.append(o)
        lses.append(lse)

    goa = p["g_out_a"][perm].reshape(1, MIX_A)
    w_out = p["w_out"]
    wo_p = jnp.concatenate([w_out[:MIX_A][perm], w_out[MIX_A:]], axis=0).astype(BF16)
    h2, h2_tiles, base = _outproj(xf, mod, oa, obs, lses, goa, p["g_out_b"].reshape(1, MIX_B), wo_p,
                        p["g_norm_ffn"].reshape(1, d), p["w_gate_s"].astype(BF16),
                        p["w_up_s"].astype(BF16), p["w_down_s"].astype(BF16), seq)

    top_e_t, gates_t, counts = _router(h2, p["w_router"].T.astype(BF16),
                                       p["router_bias"].reshape(N_EXPERTS, 1))
    bm = EXPERT_ROWS
    counts = counts.reshape(N_EXPERTS).astype(I32)
    padded = (counts + bm - 1) // bm * bm
    pends = jnp.cumsum(padded)
    pstarts = pends - padded
    rows = n * TOP_K + N_EXPERTS * bm
    n_blocks = rows // bm
    n_valid = (pends[-1] // bm).astype(I32)
    blk = jnp.minimum(jnp.arange(n_blocks, dtype=I32), n_valid - 1)
    block_e = jnp.sum((pends[None, :] <= (blk * bm)[:, None]).astype(I32), axis=1)
    block_e = jnp.minimum(block_e, N_EXPERTS - 1)

    tri = (np.arange(TT_ROUTE)[:, None] <= np.arange(TT_ROUTE)[None, :])
    dest_t = _rank(top_e_t, pstarts.astype(F32).reshape(N_EXPERTS, 1), jnp.asarray(tri, BF16))
    n_valid = n_valid.reshape(1)
    row_tok = _invert(dest_t.reshape(n * TOP_K), n, rows)
    ys = _experts(block_e, n_valid, row_tok, h2_tiles, p["w_gate_e"], p["w_up_e"], p["w_down_e"])
    out = _combine(dest_t, gates_t.T, base, mod, ys, seq)
    return out.reshape(nbatch, seq, d)


def kernel(x, c, positions, w_ada, b_ada, g_norm_mix, w_in, g_q_a, g_k_a, sinks_a, g_q_b, g_k_b,
           g_out_a, g_out_b, w_out, g_norm_ffn, w_router, router_bias, w_gate_e, w_up_e, w_down_e,
           w_gate_s, w_up_s, w_down_s):
    nbatch, seq, d = x.shape
    depth = w_ada.shape[0]
    params = dict(g_norm_mix=g_norm_mix, w_in=w_in, g_q_a=g_q_a, g_k_a=g_k_a, sinks_a=sinks_a,
                  g_q_b=g_q_b, g_k_b=g_k_b, g_out_a=g_out_a, g_out_b=g_out_b, w_out=w_out,
                  g_norm_ffn=g_norm_ffn, w_router=w_router, router_bias=router_bias,
                  w_gate_e=w_gate_e, w_up_e=w_up_e, w_down_e=w_down_e, w_gate_s=w_gate_s,
                  w_up_s=w_up_s, w_down_s=w_down_s)
    j = np.arange(LANES) % HEAD_DIM
    inv = ROPE_THETA ** (-jnp.arange(0, ROT_DIM, 2, dtype=F32) / ROT_DIM)
    invf = jnp.where(j < ROT_DIM, inv[j % (ROT_DIM // 2)], 0.0).astype(F32).reshape(1, LANES)
    bd = jnp.asarray((np.arange(LANES)[:, None] // HEAD_DIM) == (np.arange(LANES)[None, :] // HEAD_DIM),
                     BF16)
    pos = positions.reshape(nbatch * seq, 1).astype(I32)
    for l in range(depth):
        mod = _adaln(c.astype(F32), w_ada[l], b_ada[l]).reshape(nbatch, 6, d)
        x = _layer(x, mod, pos, (invf, bd), {k: v[l] for k, v in params.items()})
    return x
```

```python
import functools

import numpy as np
import jax
import jax.numpy as jnp
from jax import lax
from jax.experimental import pallas as pl
from jax.experimental.pallas import tpu as pltpu
from jax.experimental.pallas import tpu_sc as plsc

F32 = jnp.float32
BF16 = jnp.bfloat16
I32 = jnp.int32

HEAD_DIM = 64
N_HEADS_A = 8
N_KV_A = 2
WINDOW_A = 128
N_HEADS_B = 8
DILATED_BRANCHES = ((128, 1), (512, 4), (2048, 16))
DILS = tuple(dil for _, dil in DILATED_BRANCHES if dil > 1)
BLOCK = 128
ROT_DIM = HEAD_DIM // 4
ROPE_THETA = 500000.0
MIX_A = N_HEADS_A * HEAD_DIM
KV_A = N_KV_A * HEAD_DIM
MIX_B = N_HEADS_B * HEAD_DIM
N_EXPERTS = 256
TOP_K = 8
N_GROUPS = 8
TOPK_GROUPS = 4
GROUP_SIZE = N_EXPERTS // N_GROUPS
ROUTED_SCALE = 2.5
EPS = 1e-6

LANES = 128
HEADS_PER_VREG = LANES // HEAD_DIM
N_PAIRS = MIX_A // LANES
NEG = -1e30
VMEM_LIMIT = 48 * 1024 * 1024

TT_PROJ = 512
TT_ROUTE = 256
TT_DISPATCH = 256
TT_COMBINE = 128
EXPERT_ROWS = 256
TOK_CHUNK_BLOCKS = 32
GATHER_DEPTH = 4
WEIGHT_DEPTH = 3
SUBLANES = 8
ROW_QUEUE_SPLIT = 8

PAIR_ORDER_A = tuple(h for p in range(N_PAIRS) for h in (p, p + N_HEADS_A // N_KV_A))


def _params(*sem):
    return pltpu.CompilerParams(dimension_semantics=sem, vmem_limit_bytes=VMEM_LIMIT)


def _silu(t):
    return t / (1.0 + jnp.exp(-t))


def _rms_rows(t):
    return t * lax.rsqrt(jnp.mean(t * t, axis=-1, keepdims=True) + EPS)


def _store_row_tiles(ref, value):
    rows, d = value.shape
    per = d // LANES
    for s in range(per):
        ref[pl.ds(s, rows, stride=per), :] = value[:, s * LANES:(s + 1) * LANES]


def _load_row_tiles(ref, rows, per, dtype):
    return jnp.concatenate([ref[pl.ds(s, rows, stride=per), :].astype(dtype) for s in range(per)], axis=1)


def _tile_copy(src_ref, src_row, dst_ref, dst_row, per, sem):
    start = lambda row: row * per if isinstance(row, int) else pl.multiple_of(row * per, per)
    return pltpu.make_async_copy(src_ref.at[pl.ds(start(src_row), per)],
                                 dst_ref.at[pl.ds(start(dst_row), per)], sem)


def _ada_kernel(c_ref, w_ref, b_ref, o_ref):
    cond = _silu(c_ref[...])
    o_ref[...] = jnp.dot(cond.astype(BF16), w_ref[...].astype(BF16),
                         preferred_element_type=F32) + b_ref[...]


def _adaln(c, w_ada, b_ada):
    nb, d = c.shape
    width = w_ada.shape[1]
    tn = 1024
    return pl.pallas_call(
        _ada_kernel,
        grid=(width // tn,),
        in_specs=[pl.BlockSpec((nb, d), lambda j: (0, 0)),
                  pl.BlockSpec((d, tn), lambda j: (0, j)),
                  pl.BlockSpec((1, tn), lambda j: (0, j))],
        out_specs=pl.BlockSpec((nb, tn), lambda j: (0, j)),
        out_shape=jax.ShapeDtypeStruct((nb, width), F32),
        compiler_params=_params("arbitrary"),
        name="adaln",
    )(c, w_ada, b_ada.reshape(1, width))


COL_QA, COL_KA, COL_VA = 0, MIX_A, MIX_A + KV_A
COL_QB = MIX_A + 2 * KV_A
COL_KB, COL_VB = COL_QB + MIX_B, COL_QB + 2 * MIX_B
IN_WIDTH = COL_VB + MIX_B


def _inproj_kernel(x_ref, mod_ref, pos_ref, gn_ref, w_ref, gcol_ref, invf_ref, bd_ref,
                   qa_ref, ka_ref, va_ref, qb_ref, kb_ref, vb_ref, *rest):
    n_dil = len(DILS)
    dil_refs = [rest[i * n_dil:(i + 1) * n_dil] for i in range(3)]
    qb_scr, kb_scr, vb_scr = rest[3 * n_dil:]
    tt = x_ref.shape[0]
    shift, scale = mod_ref[0:1, :], mod_ref[1:2, :]
    h = _rms_rows(x_ref[...]) * gn_ref[...] * (1.0 + scale) + shift
    proj = jnp.dot(h.astype(BF16), w_ref[...], preferred_element_type=F32)

    ang = pos_ref[...].astype(F32) * invf_ref[...]
    cs, sn = jnp.cos(ang), jnp.sin(ang)
    lane = lax.broadcasted_iota(I32, (1, LANES), 1) % HEAD_DIM
    s_lo = jnp.where(lane < ROT_DIM // 2, -sn, 0.0)
    s_hi = jnp.where((lane >= ROT_DIM // 2) & (lane < ROT_DIM), sn, 0.0)
    bd = bd_ref[...]

    def norm_rope(col0, width, out_ref, scr=None):
        for j in range(width // LANES):
            c = col0 + j * LANES
            t = proj[:, c:c + LANES]
            sq = t * t
            hi = sq.astype(BF16)
            lo = (sq - hi.astype(F32)).astype(BF16)
            ss = (jnp.dot(hi, bd, preferred_element_type=F32)
                  + jnp.dot(lo, bd, preferred_element_type=F32))
            t = t * lax.rsqrt(ss * (1.0 / HEAD_DIM) + EPS) * gcol_ref[:, c:c + LANES]
            t = (t * cs + pltpu.roll(t, LANES - ROT_DIM // 2, 1) * s_lo
                 + pltpu.roll(t, ROT_DIM // 2, 1) * s_hi)
            out_ref[:, j * LANES:(j + 1) * LANES] = t.astype(BF16)
            if scr is not None:
                scr[j] = t

    norm_rope(COL_QA, MIX_A, qa_ref)
    norm_rope(COL_KA, KV_A, ka_ref)
    norm_rope(COL_QB, MIX_B, qb_ref, qb_scr)
    norm_rope(COL_KB, MIX_B, kb_ref, kb_scr)
    va_ref[...] = proj[:, COL_VA:COL_VA + KV_A].astype(BF16)
    vb_ref[...] = proj[:, COL_VB:COL_VB + MIX_B].astype(BF16)
    for j in range(N_PAIRS):
        vb_scr[j] = proj[:, COL_VB + j * LANES:COL_VB + (j + 1) * LANES]
    for scr, outs in zip((qb_scr, kb_scr, vb_scr), dil_refs):
        for dil, out in zip(DILS, outs):
            for r in range(dil):
                for j in range(N_PAIRS):
                    c = r * MIX_B + j * LANES
                    out[:, c:c + LANES] = scr[j, pl.ds(r, tt // dil, stride=dil), :].astype(BF16)


def _inproj(xf, mod, pos, g_norm, w_in_p, gcol, invf, bd, seq):
    n, d = xf.shape
    tt = TT_PROJ
    tiles_per_seq = seq // tt
    shapes = [(n, w) for w in (MIX_A, KV_A, KV_A, MIX_B, MIX_B, MIX_B)]
    shapes += [(n // dil, dil * MIX_B) for _ in range(3) for dil in DILS]
    full = lambda shape: pl.BlockSpec(shape, lambda i: (0,) * len(shape))
    return pl.pallas_call(
        _inproj_kernel,
        grid=(n // tt,),
        in_specs=[pl.BlockSpec((tt, d), lambda i: (i, 0)),
                  pl.BlockSpec((None, 6, d), lambda i: (i // tiles_per_seq, 0, 0)),
                  pl.BlockSpec((tt, 1), lambda i: (i, 0)),
                  full((1, d)), full((d, IN_WIDTH)), full((1, IN_WIDTH)),
                  full((1, LANES)), full((LANES, LANES))],
        out_specs=[pl.BlockSpec((tt * r // n, w), lambda i: (i, 0)) for r, w in shapes],
        out_shape=[jax.ShapeDtypeStruct(s, BF16) for s in shapes],
        scratch_shapes=[pltpu.VMEM((N_PAIRS, tt, LANES), F32)] * 3,
        compiler_params=_params("parallel"),
        name="inproj",
    )(xf, mod, pos, g_norm, w_in_p, gcol, invf, bd)


def _attn_kernel(*refs, kv_shared, max_dist, use_prev, has_sinks, want_lse):
    refs = list(refs)
    sink_ref = refs.pop(0) if has_sinks else None
    q_ref = refs.pop(0)
    kp_ref = refs.pop(0) if use_prev else None
    kc_ref = refs.pop(0)
    vp_ref = refs.pop(0) if use_prev else None
    vc_ref = refs.pop(0)
    o_ref = refs.pop(0)
    lse_ref = refs.pop(0) if want_lse else None

    blk = pl.program_id(2)
    nq = 2 * BLOCK
    nk = 2 * BLOCK if use_prev else BLOCK
    qpos = lax.broadcasted_iota(I32, (nq, nk), 0) % BLOCK
    kpos = lax.broadcasted_iota(I32, (nq, nk), 1)
    if use_prev:
        dist = qpos + BLOCK - kpos
        valid = (dist >= 0) & (dist <= max_dist) & ((kpos >= BLOCK) | (blk > 0))
    else:
        dist = qpos - kpos
        valid = (dist >= 0) & (dist <= max_dist)
    lane = lax.broadcasted_iota(I32, (nq, LANES), 1)
    row = lax.broadcasted_iota(I32, (nq, LANES), 0)
    own_half = (lane < HEAD_DIM) == (row < BLOCK)
    left_lanes = lax.broadcasted_iota(I32, (BLOCK, LANES), 1) < HEAD_DIM
    lane8 = lax.broadcasted_iota(I32, (BLOCK, 2 * N_PAIRS), 1)
    lse_blk = jnp.zeros((BLOCK, 2 * N_PAIRS), F32)

    for p in range(N_PAIRS):
        cq = slice(p * LANES, (p + 1) * LANES)
        ck = slice(0, LANES) if kv_shared else cq
        qp = q_ref[:, cq]
        qs = jnp.concatenate([qp, qp], axis=0)
        qs = jnp.where(own_half, qs, jnp.zeros_like(qs))
        if use_prev:
            k = jnp.concatenate([kp_ref[:, ck], kc_ref[:, ck]], axis=0)
            v = jnp.concatenate([vp_ref[:, ck], vc_ref[:, ck]], axis=0)
        else:
            k, v = kc_ref[:, ck], vc_ref[:, ck]
        s = lax.dot_general(qs, k, (((1,), (1,)), ((), ())), preferred_element_type=F32)
        s = jnp.where(valid, s, NEG)
        m = jnp.max(s, axis=-1, keepdims=True)
        if has_sinks:
            rows1 = lax.broadcasted_iota(I32, (nq, 1), 0)
            sink = jnp.where(rows1 < BLOCK, sink_ref[2 * p], sink_ref[2 * p + 1])
            m = jnp.maximum(m, sink)
        e = jnp.exp(s - m)
        l = jnp.sum(e, axis=-1, keepdims=True)
        if has_sinks:
            l = l + jnp.exp(sink - m)
        o = jnp.dot(e.astype(BF16), v, preferred_element_type=F32) / l
        o_ref[:, cq] = jnp.where(left_lanes, o[:BLOCK], o[BLOCK:]).astype(BF16)
        if want_lse:
            lse = m + jnp.log(l)
            lse_blk = (lse_blk + jnp.where(lane8 == 2 * p, lse[:BLOCK], 0.0)
                       + jnp.where(lane8 == 2 * p + 1, lse[BLOCK:], 0.0))
    if want_lse:
        lse_ref[...] = lse_blk


def _attention(q, k, v, *, nbatch, seq, dil, max_dist, kv_shared, sinks=None, want_lse):
    length = seq // dil
    nblk = length // BLOCK
    use_prev = nblk > 1
    kw = k.shape[1] // dil
    view = lambda t: t.reshape(nbatch, length, t.shape[1])
    cur = lambda b, r, i: (b, i, r)
    prev = lambda b, r, i: (b, jnp.maximum(i - 1, 0), r)
    in_specs, args = [], []
    if sinks is not None:
        in_specs.append(pl.BlockSpec(memory_space=pltpu.SMEM))
        args.append(sinks)
    in_specs.append(pl.BlockSpec((None, BLOCK, MIX_B), cur))
    args.append(view(q))
    for t in (k, v):
        if use_prev:
            in_specs.append(pl.BlockSpec((None, BLOCK, kw), prev))
            args.append(view(t))
        in_specs.append(pl.BlockSpec((None, BLOCK, kw), cur))
        args.append(view(t))
    out_specs = [pl.BlockSpec((None, BLOCK, MIX_B), cur)]
    out_shape = [jax.ShapeDtypeStruct((nbatch, length, dil * MIX_B), BF16)]
    if want_lse:
        out_specs.append(pl.BlockSpec((None, None, BLOCK, N_HEADS_B), lambda b, r, i: (b, r, i, 0)))
        out_shape.append(jax.ShapeDtypeStruct((nbatch, dil, length, N_HEADS_B), F32))
    outs = pl.pallas_call(
        functools.partial(_attn_kernel, kv_shared=kv_shared, max_dist=max_dist, use_prev=use_prev,
                          has_sinks=sinks is not None, want_lse=want_lse),
        grid=(nbatch, dil, nblk),
        in_specs=in_specs, out_specs=out_specs, out_shape=out_shape,
        compiler_params=_params("parallel", "parallel", "arbitrary"),
        name=f"attn_d{dil}" + ("_swa" if kv_shared else ""),
    )(*args)
    o = outs[0].reshape(nbatch * length, dil * MIX_B)
    if not want_lse:
        return o, None
    lse = outs[1].transpose(0, 2, 1, 3).reshape(nbatch * seq, N_HEADS_B)
    return o, lse


def _expand_heads(w, width):
    head = lax.broadcasted_iota(I32, (1, width), 1) // HEAD_DIM
    out = jnp.zeros((w.shape[0], width), F32)
    for hd in range(w.shape[1]):
        out = jnp.where(head == hd, w[:, hd:hd + 1], out)
    return out


def _outproj_kernel(x_ref, mod_ref, oa_ref, ob1_ref, ob2_ref, ob3_ref, l1_ref, l2_ref, l3_ref,
                    goa_ref, gob_ref, wo_ref, gf_ref, wgs_ref, wus_ref, wds_ref,
                    h2_ref, h2t_ref, base_ref, ob_scr):
    tt = x_ref.shape[0]
    gate_a = mod_ref[2:3, :]
    shift_m, scale_m, gate_m = mod_ref[3:4, :], mod_ref[4:5, :], mod_ref[5:6, :]

    def token_major(ref, dil):
        if dil == 1:
            return ref[...].astype(F32)
        for r in range(dil):
            for j in range(N_PAIRS):
                c = r * MIX_B + j * LANES
                ob_scr[j, pl.ds(r, tt // dil, stride=dil), :] = ref[:, c:c + LANES].astype(F32)
        return jnp.concatenate([ob_scr[j] for j in range(N_PAIRS)], axis=1)

    l1, l2, l3 = l1_ref[...], l2_ref[...], l3_ref[...]
    mx = jnp.maximum(jnp.maximum(l1, l2), l3)
    e1, e2, e3 = jnp.exp(l1 - mx), jnp.exp(l2 - mx), jnp.exp(l3 - mx)
    den = e1 + e2 + e3
    dils = [dil for _, dil in DILATED_BRANCHES]
    ob = _expand_heads(e1 / den, MIX_B) * token_major(ob1_ref, dils[0])
    ob = ob + _expand_heads(e2 / den, MIX_B) * token_major(ob2_ref, dils[1])
    ob = ob + _expand_heads(e3 / den, MIX_B) * token_major(ob3_ref, dils[2])
    ob = _rms_rows(ob) * gob_ref[...]
    oa = _rms_rows(oa_ref[...].astype(F32)) * goa_ref[...]
    y = (jnp.dot(oa.astype(BF16), wo_ref[0:MIX_A, :], preferred_element_type=F32)
         + jnp.dot(ob.astype(BF16), wo_ref[MIX_A:MIX_A + MIX_B, :], preferred_element_type=F32))
    x1 = x_ref[...] + gate_a * y
    h2 = _rms_rows(x1) * gf_ref[...] * (1.0 + scale_m) + shift_m
    h2_ref[...] = h2
    _store_row_tiles(h2t_ref, h2)
    hb = h2.astype(BF16)
    act = (_silu(jnp.dot(hb, wgs_ref[...], preferred_element_type=F32))
           * jnp.dot(hb, wus_ref[...], preferred_element_type=F32))
    shared = jnp.dot(act.astype(BF16), wds_ref[...], preferred_element_type=F32)
    base_ref[...] = x1 + gate_m * shared


def _outproj(xf, mod, oa, obs, lses, goa, gob, wo_p, gf, wgs, wus, wds, seq):
    n, d = xf.shape
    tt = TT_PROJ // 2
    tiles_per_seq = seq // tt
    tile = lambda w: pl.BlockSpec((tt, w), lambda i: (i, 0))
    full = lambda shape: pl.BlockSpec(shape, lambda i: (0,) * len(shape))
    sd = wgs.shape[1]
    dilated = [pl.BlockSpec((tt // dil, dil * MIX_B), lambda i: (i, 0)) for _, dil in DILATED_BRANCHES]
    return pl.pallas_call(
        _outproj_kernel,
        grid=(n // tt,),
        in_specs=[tile(d), pl.BlockSpec((None, 6, d), lambda i: (i // tiles_per_seq, 0, 0)),
                  tile(MIX_A), *dilated,
                  tile(N_HEADS_B), tile(N_HEADS_B), tile(N_HEADS_B),
                  full((1, MIX_A)), full((1, MIX_B)), full((MIX_A + MIX_B, d)), full((1, d)),
                  full((d, sd)), full((d, sd)), full((sd, d))],
        out_specs=[tile(d), pl.BlockSpec((tt * d // LANES, LANES), lambda i: (i, 0)), tile(d)],
        out_shape=[jax.ShapeDtypeStruct((n, d), F32), jax.ShapeDtypeStruct((n * d // LANES, LANES), F32),
                   jax.ShapeDtypeStruct((n, d), F32)],
        scratch_shapes=[pltpu.VMEM((N_PAIRS, tt, LANES), F32)],
        compiler_params=_params("parallel"),
        name="outproj",
    )(xf, mod, oa, *obs, *lses, goa, gob, wo_p, gf, wgs, wus, wds)


def _router_kernel(h_ref, wrt_ref, bias_ref, e_ref, g_ref, cnt_ref):
    tt = h_ref.shape[0]
    logits = lax.dot_general(wrt_ref[...], h_ref[...].astype(BF16), (((1,), (1,)), ((), ())),
                             preferred_element_type=F32)
    scores = 1.0 / (1.0 + jnp.exp(-logits))
    biased = scores + bias_ref[...]
    ninf = -jnp.inf

    j32 = lax.broadcasted_iota(I32, (GROUP_SIZE, tt), 0).astype(F32)
    grp = []
    for g in range(N_GROUPS):
        bg = biased[g * GROUP_SIZE:(g + 1) * GROUP_SIZE, :]
        m1 = jnp.max(bg, axis=0, keepdims=True)
        i1 = jnp.min(jnp.where(bg == m1, j32, float(GROUP_SIZE)), axis=0, keepdims=True)
        m2 = jnp.max(jnp.where(j32 == i1, ninf, bg), axis=0, keepdims=True)
        grp.append(m1 + m2)
    grp = jnp.concatenate(grp, axis=0)
    g8 = lax.broadcasted_iota(I32, (N_GROUPS, tt), 0).astype(F32)
    chosen = jnp.zeros((N_GROUPS, tt), F32)
    for _ in range(TOPK_GROUPS):
        gm = jnp.max(grp, axis=0, keepdims=True)
        gi = jnp.min(jnp.where(grp == gm, g8, float(N_GROUPS)), axis=0, keepdims=True)
        hit = g8 == gi
        chosen = jnp.where(hit, 1.0, chosen)
        grp = jnp.where(hit, ninf, grp)
    masked = jnp.concatenate(
        [jnp.where(chosen[g:g + 1, :] > 0.0, biased[g * GROUP_SIZE:(g + 1) * GROUP_SIZE, :], ninf)
         for g in range(N_GROUPS)], axis=0)

    eio = lax.broadcasted_iota(I32, (N_EXPERTS, tt), 0).astype(F32)
    picked = jnp.zeros((N_EXPERTS, tt), F32)
    es, gs = [], []
    for _ in range(TOP_K):
        m = jnp.max(masked, axis=0, keepdims=True)
        idx = jnp.min(jnp.where(masked == m, eio, float(N_EXPERTS)), axis=0, keepdims=True)
        hit = eio == idx
        gs.append(jnp.sum(jnp.where(hit, scores, 0.0), axis=0, keepdims=True))
        es.append(idx)
        picked = jnp.where(hit, 1.0, picked)
        masked = jnp.where(hit, ninf, masked)
    gates = jnp.concatenate(gs, axis=0)
    e_ref[...] = jnp.concatenate(es, axis=0).astype(I32)
    g_ref[...] = gates / jnp.sum(gates, axis=0, keepdims=True) * ROUTED_SCALE

    @pl.when(pl.program_id(0) == 0)
    def _():
        cnt_ref[...] = jnp.zeros_like(cnt_ref)
    cnt_ref[...] += jnp.sum(picked, axis=1, keepdims=True)


def _router(h2, wrt, bias_col):
    n, d = h2.shape
    tt = TT_ROUTE
    return pl.pallas_call(
        _router_kernel,
        grid=(n // tt,),
        in_specs=[pl.BlockSpec((tt, d), lambda i: (i, 0)),
                  pl.BlockSpec((N_EXPERTS, d), lambda i: (0, 0)),
                  pl.BlockSpec((N_EXPERTS, 1), lambda i: (0, 0))],
        out_specs=[pl.BlockSpec((TOP_K, tt), lambda i: (0, i)),
                   pl.BlockSpec((TOP_K, tt), lambda i: (0, i)),
                   pl.BlockSpec((N_EXPERTS, 1), lambda i: (0, 0))],
        out_shape=[jax.ShapeDtypeStruct((TOP_K, n), I32),
                   jax.ShapeDtypeStruct((TOP_K, n), F32),
                   jax.ShapeDtypeStruct((N_EXPERTS, 1), F32)],
        compiler_params=_params("arbitrary"),
        name="router",
    )(h2, wrt, bias_col)


def _rank_kernel(e_ref, pstart_ref, tri_ref, dest_ref, carry_ref):
    tt = e_ref.shape[1]

    @pl.when(pl.program_id(0) == 0)
    def _():
        carry_ref[...] = pstart_ref[...]

    e = e_ref[...]
    eio = lax.broadcasted_iota(I32, (N_EXPERTS, tt), 0)
    mask = jnp.zeros((N_EXPERTS, tt), F32)
    for k in range(TOP_K):
        mask = jnp.where(eio == e[k:k + 1, :], 1.0, mask)
    incl = jnp.dot(mask.astype(BF16), tri_ref[...], preferred_element_type=F32)
    pos = incl - mask + carry_ref[...]
    dest = [jnp.sum(jnp.where(eio == e[k:k + 1, :], pos, 0.0), axis=0, keepdims=True)
            for k in range(TOP_K)]
    dest_ref[...] = jnp.concatenate(dest, axis=0).astype(I32)
    carry_ref[...] += incl[:, tt - 1:tt]


def _rank(top_e_t, pstart_col, tri):
    n = top_e_t.shape[1]
    tt = TT_ROUTE
    return pl.pallas_call(
        _rank_kernel,
        grid=(n // tt,),
        in_specs=[pl.BlockSpec((TOP_K, tt), lambda i: (0, i)),
                  pl.BlockSpec((N_EXPERTS, 1), lambda i: (0, 0)),
                  pl.BlockSpec((tt, tt), lambda i: (0, 0))],
        out_specs=pl.BlockSpec((TOP_K, tt), lambda i: (0, i)),
        out_shape=jax.ShapeDtypeStruct((TOP_K, n), I32),
        scratch_shapes=[pltpu.VMEM((N_EXPERTS, 1), F32)],
        compiler_params=_params("arbitrary"),
        name="rank",
    )(top_e_t, pstart_col, tri)


def _row_copy(src_ref, src_row, dst_ref, dst_row, sem):
    return pltpu.make_async_copy(src_ref.at[pl.ds(src_row, 1)], dst_ref.at[pl.ds(dst_row, 1)], sem)


SC_CORES, SC_SUBCORES, SC_LANES = 2, 16, 16
SC_WORKERS = SC_CORES * SC_SUBCORES
INVERT_CHUNK = 8192


def _invert(dest_flat, n_tokens, rows):
    per = rows // SC_WORKERS
    n_assign = dest_flat.shape[0]
    assert rows % (SC_WORKERS * SC_LANES) == 0 and n_assign % INVERT_CHUNK == 0
    assert n_tokens & (n_tokens - 1) == 0

    @functools.partial(
        pl.kernel, mesh=plsc.VectorSubcoreMesh(core_axis_name="c", subcore_axis_name="s"),
        out_type=jax.ShapeDtypeStruct((rows,), I32),
        scratch_types=[pltpu.VMEM((INVERT_CHUNK,), I32), pltpu.VMEM((per,), I32)],
        compiler_params=pltpu.CompilerParams(needs_layout_passes=False))
    def invert(dest_hbm, out_hbm, staged, local):
        base = (lax.axis_index("s") * SC_CORES + lax.axis_index("c")) * per
        lane = lax.iota(I32, SC_LANES)

        @pl.loop(0, per, step=SC_LANES)
        def _(i):
            local[pl.ds(i, SC_LANES)] = (base + i + lane) & (n_tokens - 1)

        @pl.loop(0, n_assign // INVERT_CHUNK)
        def _(c):
            pltpu.sync_copy(dest_hbm.at[pl.ds(c * INVERT_CHUNK, INVERT_CHUNK)], staged)

            @pl.loop(0, INVERT_CHUNK, step=SC_LANES)
            def _(i):
                rel = staged[pl.ds(i, SC_LANES)] - base
                mine = (rel >= 0) & (rel < per)
                tok = (c * INVERT_CHUNK + i + lane) & (n_tokens - 1)
                plsc.store_scatter(local, [jnp.where(mine, rel, 0)], tok, mask=mine)

        pltpu.sync_copy(local, out_hbm.at[pl.ds(base, per)])

    return invert(dest_flat)


def _experts_kernel(rank_ref, eor_ref, nv_ref, nr_ref, tok_hbm, h_ref, wg_hbm, wu_hbm, wd_hbm,
                    ys_ref, *scratch):
    xbufs = scratch[:GATHER_DEPTH]
    wgf, wuf, wdf, wgb, wub, wdb, tok_s, sem, tok_sem, w_sem = scratch[GATHER_DEPTH:]
    j = pl.program_id(0)
    nv, n_ranks = nv_ref[0], nr_ref[0]
    per = wgb.shape[0] // LANES
    depth, bm = len(xbufs), xbufs[0].shape[0] // per
    ahead = depth - 1
    chunk = tok_s.shape[1]
    cb = chunk // bm
    n_chunks = tok_hbm.shape[0] // chunk
    n_blocks = n_chunks * cb

    def tok_copy(c):
        return pltpu.make_async_copy(tok_hbm.at[pl.ds(pl.multiple_of(c * chunk, chunk), chunk)],
                                     tok_s.at[c % 2], tok_sem.at[c % 2])

    def gather(block, ring, unrolled):
        block = jnp.minimum(block, n_blocks - 1)
        slot, off = (block // cb) % 2, (block % cb) * bm
        buf, buf_sem = xbufs[ring], sem.at[ring]

        def issue(i, carry=None):
            _tile_copy(h_ref, tok_s[slot, off + i], buf, i, per, buf_sem).start()
            return carry

        if unrolled:
            for i in range(bm):
                _tile_copy(h_ref, tok_s[slot, off + i], buf, i, per, buf_sem).start(
                    priority=int(i % ROW_QUEUE_SPLIT != 0))
        else:
            lax.fori_loop(0, bm, issue, 0, unroll=8)

    def wait_rows(ring):
        pltpu.make_async_copy(h_ref.at[pl.ds(0, bm * per)], xbufs[ring], sem.at[ring]).wait()

    def weight_copies(r):
        e, s = eor_ref[r], r % WEIGHT_DEPTH
        return [pltpu.make_async_copy(src.at[e], dst.at[s], w_sem.at[s])
                for src, dst in ((wg_hbm, wgf), (wu_hbm, wuf), (wd_hbm, wdf))]

    @pl.when(j == 0)
    def _():
        tok_copy(0).start()
        tok_copy(0).wait()
        tok_copy(1).start()
        for r in range(WEIGHT_DEPTH - 1):
            @pl.when(r < n_ranks)
            def _():
                for cp in weight_copies(r):
                    cp.start()
        for b in range(ahead):
            gather(b, b, unrolled=False)

    first = j + ahead
    c_need = first // cb

    @pl.when(jnp.logical_and(jnp.logical_and(j > 0, first % cb == 0), c_need < n_chunks))
    def _():
        tok_copy(c_need).wait()

        @pl.when(c_need + 1 < n_chunks)
        def _():
            tok_copy(c_need + 1).start()

    jb = jnp.minimum(j, n_blocks - 1)
    rank = rank_ref[jb]
    fresh = jnp.logical_or(j == 0, rank != rank_ref[jnp.maximum(jb - 1, 0)])

    @pl.when(jnp.logical_and(j < nv, fresh))
    def _():
        for cp in weight_copies(rank):
            cp.wait()
        s = rank % WEIGHT_DEPTH
        wgb[...] = wgf[s].astype(BF16)
        wub[...] = wuf[s].astype(BF16)
        wdb[...] = wdf[s].astype(BF16)
        nxt = rank + WEIGHT_DEPTH - 1

        @pl.when(nxt < n_ranks)
        def _():
            for cp in weight_copies(nxt):
                cp.start()

    for ring in range(depth):
        mine = j % depth == ring

        @pl.when(jnp.logical_and(j < nv, mine))
        def _():
            wait_rows(ring)
            gather(j + ahead, (ring + ahead) % depth, unrolled=True)
            xb = _load_row_tiles(xbufs[ring], bm, per, BF16)
            act = (_silu(jnp.dot(xb, wgb[...], preferred_element_type=F32))
                   * jnp.dot(xb, wub[...], preferred_element_type=F32))
            _store_row_tiles(ys_ref, jnp.dot(act.astype(BF16), wdb[...], preferred_element_type=F32))

        @pl.when(jnp.logical_and(jnp.logical_and(j >= nv, j < nv + ahead), mine))
        def _():
            wait_rows(ring)


def _experts(block_rank, expert_of_rank, n_valid, n_ranks, row_tok, h2_tiles, wg, wu, wd):
    rows = row_tok.shape[0]
    d = wg.shape[1]
    per = d // LANES
    bm = EXPERT_ROWS
    n_blocks = rows // bm
    assert n_blocks % TOK_CHUNK_BLOCKS == 0 and n_blocks // TOK_CHUNK_BLOCKS >= 2
    assert GATHER_DEPTH - 1 < TOK_CHUNK_BLOCKS
    f = wg.shape[2]
    hbm = pl.BlockSpec(memory_space=pl.ANY)
    return pl.pallas_call(
        _experts_kernel,
        grid_spec=pltpu.PrefetchScalarGridSpec(
            num_scalar_prefetch=4,
            grid=(n_blocks + GATHER_DEPTH,),
            in_specs=[hbm, hbm, hbm, hbm, hbm],
            out_specs=pl.BlockSpec((bm * per, LANES),
                                   lambda j, rk, eor, nv, nr: (jnp.minimum(j, nv[0] - 1), 0)),
            scratch_shapes=[pltpu.VMEM((bm * per, LANES), F32)] * GATHER_DEPTH + [
                            pltpu.VMEM((WEIGHT_DEPTH, d, f), F32), pltpu.VMEM((WEIGHT_DEPTH, d, f), F32),
                            pltpu.VMEM((WEIGHT_DEPTH, f, d), F32),
                            pltpu.VMEM((d, f), BF16), pltpu.VMEM((d, f), BF16),
                            pltpu.VMEM((f, d), BF16),
                            pltpu.SMEM((2, TOK_CHUNK_BLOCKS * bm), I32),
                            pltpu.SemaphoreType.DMA((GATHER_DEPTH,)), pltpu.SemaphoreType.DMA((2,)),
                            pltpu.SemaphoreType.DMA((WEIGHT_DEPTH,))]),
        out_shape=jax.ShapeDtypeStruct((rows * per, LANES), F32),
        compiler_params=_params("arbitrary"),
        name="experts",
    )(block_rank, expert_of_rank, n_valid, n_ranks, row_tok, h2_tiles, wg, wu, wd)


def _combine_kernel(dest_ref, gates_ref, base_ref, mod_ref, ys_ref, out_ref, buf, sem):
    tt, d = base_ref.shape
    per = d // LANES
    for t in range(tt):
        for k in range(TOP_K):
            _tile_copy(ys_ref, dest_ref[k, t], buf.at[k], t, per, sem).start(priority=k % 2)
    for k in range(TOP_K):
        pltpu.make_async_copy(ys_ref.at[pl.ds(0, tt * per)], buf.at[k], sem).wait()
    gates = gates_ref[...]
    routed = gates[:, 0:1] * _load_row_tiles(buf.at[0], tt, per, F32)
    for k in range(1, TOP_K):
        routed = routed + gates[:, k:k + 1] * _load_row_tiles(buf.at[k], tt, per, F32)
    out_ref[...] = base_ref[...] + mod_ref[5:6, :] * routed


def _combine(dest_t, gates, base, mod, ys, seq):
    n, d = base.shape
    tt = TT_COMBINE
    tiles_per_seq = seq // tt
    return pl.pallas_call(
        _combine_kernel,
        grid=(n // tt,),
        in_specs=[pl.BlockSpec((TOP_K, tt), lambda i: (0, i), memory_space=pltpu.SMEM),
                  pl.BlockSpec((tt, TOP_K), lambda i: (i, 0)),
                  pl.BlockSpec((tt, d), lambda i: (i, 0)),
                  pl.BlockSpec((None, 6, d), lambda i: (i // tiles_per_seq, 0, 0)),
                  pl.BlockSpec(memory_space=pl.ANY)],
        out_specs=pl.BlockSpec((tt, d), lambda i: (i, 0)),
        out_shape=jax.ShapeDtypeStruct((n, d), F32),
        scratch_shapes=[pltpu.VMEM((TOP_K, tt * d // LANES, LANES), F32), pltpu.SemaphoreType.DMA],
        compiler_params=_params("arbitrary"),
        name="combine",
    )(dest_t, gates, base, mod, ys)


def _layer(x, mod, pos, rope, p):
    nbatch, seq, d = x.shape
    n = nbatch * seq
    xf = x.reshape(n, d)
    invf, bd = rope

    perm = np.concatenate([np.arange(h * HEAD_DIM, (h + 1) * HEAD_DIM) for h in PAIR_ORDER_A])
    w_in = p["w_in"]
    w_in_p = jnp.concatenate([w_in[:, :MIX_A][:, perm], w_in[:, MIX_A:]], axis=1).astype(BF16)
    ones = lambda w: jnp.ones((w,), F32)
    qscale = HEAD_DIM ** -0.5
    gcol = jnp.concatenate([jnp.tile(p["g_q_a"], N_HEADS_A) * qscale, jnp.tile(p["g_k_a"], N_KV_A),
                            ones(KV_A), jnp.tile(p["g_q_b"], N_HEADS_B) * qscale,
                            jnp.tile(p["g_k_b"], N_HEADS_B), ones(MIX_B)]).reshape(1, IN_WIDTH)
    proj = _inproj(xf, mod, pos, p["g_norm_mix"].reshape(1, d), w_in_p, gcol, invf, bd, seq)
    qa, ka, va = proj[:3]
    qkv_b = {1: proj[3:6]}
    for j, dil in enumerate(DILS):
        qkv_b[dil] = [proj[6 + t * len(DILS) + j] for t in range(3)]

    sinks_p = p["sinks_a"][np.array(PAIR_ORDER_A)]
    oa, _ = _attention(qa, ka, va, nbatch=nbatch, seq=seq, dil=1, max_dist=WINDOW_A - 1,
                       kv_shared=True, sinks=sinks_p, want_lse=False)
    obs, lses = [], []
    for window, dil in DILATED_BRANCHES:
        o, lse = _attention(*qkv_b[dil], nbatch=nbatch, seq=seq, dil=dil, max_dist=window // dil,
                            kv_shared=False, want_lse=True)
        obs.append(o)
        lses.append(lse)

    goa = p["g_out_a"][perm].reshape(1, MIX_A)
    w_out = p["w_out"]
    wo_p = jnp.concatenate([w_out[:MIX_A][perm], w_out[MIX_A:]], axis=0).astype(BF16)
    h2, h2_tiles, base = _outproj(xf, mod, oa, obs, lses, goa, p["g_out_b"].reshape(1, MIX_B), wo_p,
                        p["g_norm_ffn"].reshape(1, d), p["w_gate_s"].astype(BF16),
                        p["w_up_s"].astype(BF16), p["w_down_s"].astype(BF16), seq)

    top_e_t, gates_t, counts = _router(h2, p["w_router"].T.astype(BF16),
                                       p["router_bias"].reshape(N_EXPERTS, 1))
    bm = EXPERT_ROWS
    counts = counts.reshape(N_EXPERTS).astype(I32)
    padded = (counts + bm - 1) // bm * bm
    pends = jnp.cumsum(padded)
    pstarts = pends - padded
    rows = n * TOP_K + N_EXPERTS * bm
    n_blocks = rows // bm
    n_valid = (pends[-1] // bm).astype(I32)
    blk = jnp.minimum(jnp.arange(n_blocks, dtype=I32), n_valid - 1)
    block_e = jnp.sum((pends[None, :] <= (blk * bm)[:, None]).astype(I32), axis=1)
    block_e = jnp.minimum(block_e, N_EXPERTS - 1)
    nonempty = counts > 0
    rank_of_e = jnp.cumsum(nonempty.astype(I32)) - 1
    e_ids = jnp.arange(N_EXPERTS, dtype=I32)
    block_rank = jnp.sum(jnp.where(block_e[:, None] == e_ids[None, :], rank_of_e[None, :], 0), axis=1)
    hit = (rank_of_e[None, :] == jnp.arange(n_blocks, dtype=I32)[:, None]) & nonempty[None, :]
    expert_of_rank = jnp.sum(jnp.where(hit, e_ids[None, :], 0), axis=1)
    n_ranks = jnp.sum(nonempty.astype(I32)).reshape(1)

    tri =(np.arange(TT_ROUTE)[:, None] <= np.arange(TT_ROUTE)[None, :])
    dest_t = _rank(top_e_t, pstarts.astype(F32).reshape(N_EXPERTS, 1), jnp.asarray(tri, BF16))
    n_valid = n_valid.reshape(1)
    row_tok = _invert(dest_t.reshape(n * TOP_K), n, rows)
    ys = _experts(block_rank.astype(I32), expert_of_rank.astype(I32), n_valid, n_ranks, row_tok, h2_tiles,
                  p["w_gate_e"], p["w_up_e"], p["w_down_e"])
    out = _combine(dest_t, gates_t.T, base, mod, ys, seq)
    return out.reshape(nbatch, seq, d)


def kernel(x, c, positions, w_ada, b_ada, g_norm_mix, w_in, g_q_a, g_k_a, sinks_a, g_q_b, g_k_b,
           g_out_a, g_out_b, w_out, g_norm_ffn, w_router, router_bias, w_gate_e, w_up_e, w_down_e,
           w_gate_s, w_up_s, w_down_s):
    nbatch, seq, d = x.shape
    depth = w_ada.shape[0]
    params = dict(g_norm_mix=g_norm_mix, w_in=w_in, g_q_a=g_q_a, g_k_a=g_k_a, sinks_a=sinks_a,
                  g_q_b=g_q_b, g_k_b=g_k_b, g_out_a=g_out_a, g_out_b=g_out_b, w_out=w_out,
                  g_norm_ffn=g_norm_ffn, w_router=w_router, router_bias=router_bias,
                  w_gate_e=w_gate_e, w_up_e=w_up_e, w_down_e=w_down_e, w_gate_s=w_gate_s,
                  w_up_s=w_up_s, w_down_s=w_down_s)
    j = np.arange(LANES) % HEAD_DIM
    inv = ROPE_THETA ** (-jnp.arange(0, ROT_DIM, 2, dtype=F32) / ROT_DIM)
    invf = jnp.where(j < ROT_DIM, inv[j % (ROT_DIM // 2)], 0.0).astype(F32).reshape(1, LANES)
    bd = jnp.asarray((np.arange(LANES)[:, None] // HEAD_DIM) == (np.arange(LANES)[None, :] // HEAD_DIM),
                     BF16)
    pos = positions.reshape(nbatch * seq, 1).astype(I32)
    for l in range(depth):
        mod = _adaln(c.astype(F32), w_ada[l], b_ada[l]).reshape(nbatch, 6, d)
        x = _layer(x, mod, pos, (invf, bd), {k: v[l] for k, v in params.items()})
    return x
```

```python
import functools

import numpy as np
import jax
import jax.numpy as jnp
from jax import lax
from jax.experimental import pallas as pl
from jax.experimental.pallas import tpu as pltpu
from jax.experimental.pallas import tpu_sc as plsc

F32 = jnp.float32
BF16 = jnp.bfloat16
I32 = jnp.int32

HEAD_DIM = 64
N_HEADS_A = 8
N_KV_A = 2
WINDOW_A = 128
N_HEADS_B = 8
DILATED_BRANCHES = ((128, 1), (512, 4), (2048, 16))
DILS = tuple(dil for _, dil in DILATED_BRANCHES if dil > 1)
BLOCK = 128
ROT_DIM = HEAD_DIM // 4
ROPE_THETA = 500000.0
MIX_A = N_HEADS_A * HEAD_DIM
KV_A = N_KV_A * HEAD_DIM
MIX_B = N_HEADS_B * HEAD_DIM
N_EXPERTS = 256
TOP_K = 8
N_GROUPS = 8
TOPK_GROUPS = 4
GROUP_SIZE = N_EXPERTS // N_GROUPS
ROUTED_SCALE = 2.5
EPS = 1e-6

LANES = 128
HEADS_PER_VREG = LANES // HEAD_DIM
N_PAIRS = MIX_A // LANES
NEG = -1e30
VMEM_LIMIT = 48 * 1024 * 1024

TT_PROJ = 512
ATTN_BLOCKS_PER_STEP = 4
TT_ROUTE = 256
TT_DISPATCH = 256
TT_COMBINE = 128
EXPERT_ROWS = 256
TOK_CHUNK_BLOCKS = 32
GATHER_DEPTH = 4
WEIGHT_DEPTH = 3
SUBLANES = 8
ROW_QUEUE_SPLIT = 8

PAIR_ORDER_A = tuple(h for p in range(N_PAIRS) for h in (p, p + N_HEADS_A // N_KV_A))


def _params(*sem):
    return pltpu.CompilerParams(dimension_semantics=sem, vmem_limit_bytes=VMEM_LIMIT)


def _silu(t):
    return t / (1.0 + jnp.exp(-t))


def _rms_rows(t):
    return t * lax.rsqrt(jnp.mean(t * t, axis=-1, keepdims=True) + EPS)


def _store_row_tiles(ref, value):
    rows, d = value.shape
    per = d // LANES
    for s in range(per):
        ref[pl.ds(s, rows, stride=per), :] = value[:, s * LANES:(s + 1) * LANES]


def _load_row_tiles(ref, rows, per, dtype):
    return jnp.concatenate([ref[pl.ds(s, rows, stride=per), :].astype(dtype) for s in range(per)], axis=1)


def _tile_copy(src_ref, src_row, dst_ref, dst_row, per, sem):
    start = lambda row: row * per if isinstance(row, int) else pl.multiple_of(row * per, per)
    return pltpu.make_async_copy(src_ref.at[pl.ds(start(src_row), per)],
                                 dst_ref.at[pl.ds(start(dst_row), per)], sem)


def _ada_kernel(c_ref, w_ref, b_ref, o_ref):
    cond = _silu(c_ref[...])
    o_ref[...] = jnp.dot(cond.astype(BF16), w_ref[...].astype(BF16),
                         preferred_element_type=F32) + b_ref[...]


def _adaln(c, w_ada, b_ada):
    nb, d = c.shape
    width = w_ada.shape[1]
    tn = 1024
    return pl.pallas_call(
        _ada_kernel,
        grid=(width // tn,),
        in_specs=[pl.BlockSpec((nb, d), lambda j: (0, 0)),
                  pl.BlockSpec((d, tn), lambda j: (0, j)),
                  pl.BlockSpec((1, tn), lambda j: (0, j))],
        out_specs=pl.BlockSpec((nb, tn), lambda j: (0, j)),
        out_shape=jax.ShapeDtypeStruct((nb, width), F32),
        compiler_params=_params("arbitrary"),
        name="adaln",
    )(c, w_ada, b_ada.reshape(1, width))


COL_QA, COL_KA, COL_VA = 0, MIX_A, MIX_A + KV_A
COL_QB = MIX_A + 2 * KV_A
COL_KB, COL_VB = COL_QB + MIX_B, COL_QB + 2 * MIX_B
IN_WIDTH = COL_VB + MIX_B


def _inproj_kernel(x_ref, mod_ref, pos_ref, gn_ref, w_ref, gcol_ref, invf_ref, bd_ref,
                   qa_ref, ka_ref, va_ref, qb_ref, kb_ref, vb_ref, *rest):
    n_dil = len(DILS)
    dil_refs = [rest[i * n_dil:(i + 1) * n_dil] for i in range(3)]
    qb_scr, kb_scr, vb_scr = rest[3 * n_dil:]
    tt = x_ref.shape[0]
    shift, scale = mod_ref[0:1, :], mod_ref[1:2, :]
    h = _rms_rows(x_ref[...]) * gn_ref[...] * (1.0 + scale) + shift
    proj = jnp.dot(h.astype(BF16), w_ref[...], preferred_element_type=F32)

    ang = pos_ref[...].astype(F32) * invf_ref[...]
    cs, sn = jnp.cos(ang), jnp.sin(ang)
    lane = lax.broadcasted_iota(I32, (1, LANES), 1) % HEAD_DIM
    s_lo = jnp.where(lane < ROT_DIM // 2, -sn, 0.0)
    s_hi = jnp.where((lane >= ROT_DIM // 2) & (lane < ROT_DIM), sn, 0.0)
    bd = bd_ref[...]

    def norm_rope(col0, width, out_ref, scr=None):
        for j in range(width // LANES):
            c = col0 + j * LANES
            t = proj[:, c:c + LANES]
            sq = t * t
            hi = sq.astype(BF16)
            lo = (sq - hi.astype(F32)).astype(BF16)
            ss = (jnp.dot(hi, bd, preferred_element_type=F32)
                  + jnp.dot(lo, bd, preferred_element_type=F32))
            t = t * lax.rsqrt(ss * (1.0 / HEAD_DIM) + EPS) * gcol_ref[:, c:c + LANES]
            t = (t * cs + pltpu.roll(t, LANES - ROT_DIM // 2, 1) * s_lo
                 + pltpu.roll(t, ROT_DIM // 2, 1) * s_hi)
            out_ref[:, j * LANES:(j + 1) * LANES] = t.astype(BF16)
            if scr is not None:
                scr[j] = t

    norm_rope(COL_QA, MIX_A, qa_ref)
    norm_rope(COL_KA, KV_A, ka_ref)
    norm_rope(COL_QB, MIX_B, qb_ref, qb_scr)
    norm_rope(COL_KB, MIX_B, kb_ref, kb_scr)
    va_ref[...] = proj[:, COL_VA:COL_VA + KV_A].astype(BF16)
    vb_ref[...] = proj[:, COL_VB:COL_VB + MIX_B].astype(BF16)
    for j in range(N_PAIRS):
        vb_scr[j] = proj[:, COL_VB + j * LANES:COL_VB + (j + 1) * LANES]
    for scr, outs in zip((qb_scr, kb_scr, vb_scr), dil_refs):
        for dil, out in zip(DILS, outs):
            for r in range(dil):
                for j in range(N_PAIRS):
                    c = r * MIX_B + j * LANES
                    out[:, c:c + LANES] = scr[j, pl.ds(r, tt // dil, stride=dil), :].astype(BF16)


def _inproj(xf, mod, pos, g_norm, w_in_p, gcol, invf, bd, seq):
    n, d = xf.shape
    tt = TT_PROJ
    tiles_per_seq = seq // tt
    shapes = [(n, w) for w in (MIX_A, KV_A, KV_A, MIX_B, MIX_B, MIX_B)]
    shapes += [(n // dil, dil * MIX_B) for _ in range(3) for dil in DILS]
    full = lambda shape: pl.BlockSpec(shape, lambda i: (0,) * len(shape))
    return pl.pallas_call(
        _inproj_kernel,
        grid=(n // tt,),
        in_specs=[pl.BlockSpec((tt, d), lambda i: (i, 0)),
                  pl.BlockSpec((None, 6, d), lambda i: (i // tiles_per_seq, 0, 0)),
                  pl.BlockSpec((tt, 1), lambda i: (i, 0)),
                  full((1, d)), full((d, IN_WIDTH)), full((1, IN_WIDTH)),
                  full((1, LANES)), full((LANES, LANES))],
        out_specs=[pl.BlockSpec((tt * r // n, w), lambda i: (i, 0)) for r, w in shapes],
        out_shape=[jax.ShapeDtypeStruct(s, BF16) for s in shapes],
        scratch_shapes=[pltpu.VMEM((N_PAIRS, tt, LANES), F32)] * 3,
        compiler_params=_params("parallel"),
        name="inproj",
    )(xf, mod, pos, g_norm, w_in_p, gcol, invf, bd)


def _attn_kernel(*refs, kv_shared, max_dist, n_qblk, n_res, use_prev, has_sinks, want_lse):
    refs = list(refs)
    sink_ref = refs.pop(0) if has_sinks else None
    q_ref = refs.pop(0)
    kp_ref = refs.pop(0) if use_prev else None
    kc_ref = refs.pop(0)
    vp_ref = refs.pop(0) if use_prev else None
    vc_ref = refs.pop(0)
    o_ref = refs.pop(0)
    lse_ref = refs.pop(0) if want_lse else None

    first_step = pl.program_id(2) == 0
    nq = 2 * BLOCK
    kw = LANES if kv_shared else MIX_B

    def band(nk):
        qpos = lax.broadcasted_iota(I32, (nq, nk), 0) % BLOCK
        kpos = lax.broadcasted_iota(I32, (nq, nk), 1)
        dist = qpos + (nk - BLOCK) - kpos
        return (dist >= 0) & (dist <= max_dist), kpos

    band2, kpos2 = band(2 * BLOCK)
    band1, _ = band(BLOCK)
    lane = lax.broadcasted_iota(I32, (nq, LANES), 1)
    row = lax.broadcasted_iota(I32, (nq, LANES), 0)
    own_half = (lane < HEAD_DIM) == (row < BLOCK)
    left_lanes = lax.broadcasted_iota(I32, (BLOCK, LANES), 1) < HEAD_DIM
    lane8 = lax.broadcasted_iota(I32, (BLOCK, 2 * N_PAIRS), 1)

    for res in range(n_res):
        for qb in range(n_qblk):
            rows = slice(qb * BLOCK, (qb + 1) * BLOCK)
            lse_blk = jnp.zeros((BLOCK, 2 * N_PAIRS), F32)
            for p in range(N_PAIRS):
                cq = slice(res * MIX_B + p * LANES, res * MIX_B + (p + 1) * LANES)
                ck = slice(res * kw, res * kw + LANES) if kv_shared else slice(
                    res * kw + p * LANES, res * kw + (p + 1) * LANES)
                qp = q_ref[rows, cq]
                qs = jnp.concatenate([qp, qp], axis=0)
                qs = jnp.where(own_half, qs, jnp.zeros_like(qs))
                if qb > 0:
                    keys = slice((qb - 1) * BLOCK, (qb + 1) * BLOCK)
                    k, v, valid = kc_ref[keys, ck], vc_ref[keys, ck], band2
                elif use_prev:
                    k = jnp.concatenate([kp_ref[:, ck], kc_ref[rows, ck]], axis=0)
                    v = jnp.concatenate([vp_ref[:, ck], vc_ref[rows, ck]], axis=0)
                    valid = band2 & ((kpos2 >= BLOCK) | jnp.logical_not(first_step))
                else:
                    k, v, valid = kc_ref[rows, ck], vc_ref[rows, ck], band1
                s = lax.dot_general(qs, k, (((1,), (1,)), ((), ())), preferred_element_type=F32)
                s = jnp.where(valid, s, NEG)
                m = jnp.max(s, axis=-1, keepdims=True)
                if has_sinks:
                    rows1 = lax.broadcasted_iota(I32, (nq, 1), 0)
                    sink = jnp.where(rows1 < BLOCK, sink_ref[2 * p], sink_ref[2 * p + 1])
                    m = jnp.maximum(m, sink)
                e = jnp.exp(s - m)
                l = jnp.sum(e, axis=-1, keepdims=True)
                if has_sinks:
                    l = l + jnp.exp(sink - m)
                o = jnp.dot(e.astype(BF16), v, preferred_element_type=F32) / l
                o_ref[rows, cq] = jnp.where(left_lanes, o[:BLOCK], o[BLOCK:]).astype(BF16)
                if want_lse:
                    lse = m + jnp.log(l)
                    lse_blk = (lse_blk + jnp.where(lane8 == 2 * p, lse[:BLOCK], 0.0)
                               + jnp.where(lane8 == 2 * p + 1, lse[BLOCK:], 0.0))
            if want_lse:
                lse_ref[res, rows, :] = lse_blk


def _attention(q, k, v, *, nbatch, seq, dil, max_dist, kv_shared, sinks=None, want_lse):
    length = seq // dil
    nblk = length // BLOCK
    n_qblk = min(nblk, ATTN_BLOCKS_PER_STEP)
    n_res = min(dil, ATTN_BLOCKS_PER_STEP // n_qblk)
    steps = nblk // n_qblk
    use_prev = steps > 1
    kw = k.shape[1] // dil
    view = lambda t: t.reshape(nbatch, length, t.shape[1])
    cur = lambda b, r, i: (b, i, r)
    prev = lambda b, r, i: (b, jnp.maximum(i * n_qblk - 1, 0), r)
    in_specs, args = [], []
    if sinks is not None:
        in_specs.append(pl.BlockSpec(memory_space=pltpu.SMEM))
        args.append(sinks)
    in_specs.append(pl.BlockSpec((None, n_qblk * BLOCK, n_res * MIX_B), cur))
    args.append(view(q))
    for t in (k, v):
        if use_prev:
            in_specs.append(pl.BlockSpec((None, BLOCK, n_res * kw), prev))
            args.append(view(t))
        in_specs.append(pl.BlockSpec((None, n_qblk * BLOCK, n_res * kw), cur))
        args.append(view(t))
    out_specs = [pl.BlockSpec((None, n_qblk * BLOCK, n_res * MIX_B), cur)]
    out_shape = [jax.ShapeDtypeStruct((nbatch, length, dil * MIX_B), BF16)]
    if want_lse:
        out_specs.append(pl.BlockSpec((None, n_res, n_qblk * BLOCK, N_HEADS_B),
                                      lambda b, r, i: (b, r, i, 0)))
        out_shape.append(jax.ShapeDtypeStruct((nbatch, dil, length, N_HEADS_B), F32))
    outs = pl.pallas_call(
        functools.partial(_attn_kernel, kv_shared=kv_shared, max_dist=max_dist, n_qblk=n_qblk,
                          n_res=n_res, use_prev=use_prev, has_sinks=sinks is not None,
                          want_lse=want_lse),
        grid=(nbatch, dil // n_res, steps),
        in_specs=in_specs, out_specs=out_specs, out_shape=out_shape,
        compiler_params=_params("parallel", "parallel", "arbitrary"),
        name=f"attn_d{dil}" + ("_swa" if kv_shared else ""),
    )(*args)
    o = outs[0].reshape(nbatch * length, dil * MIX_B)
    if not want_lse:
        return o, None
    lse = outs[1].transpose(0, 2, 1, 3).reshape(nbatch * seq, N_HEADS_B)
    return o, lse


def _expand_heads(w, width):
    head = lax.broadcasted_iota(I32, (1, width), 1) // HEAD_DIM
    out = jnp.zeros((w.shape[0], width), F32)
    for hd in range(w.shape[1]):
        out = jnp.where(head == hd, w[:, hd:hd + 1], out)
    return out


def _outproj_kernel(x_ref, mod_ref, oa_ref, ob1_ref, ob2_ref, ob3_ref, l1_ref, l2_ref, l3_ref,
                    goa_ref, gob_ref, wo_ref, gf_ref, wgs_ref, wus_ref, wds_ref,
                    h2_ref, h2t_ref, base_ref, ob_scr):
    tt = x_ref.shape[0]
    gate_a = mod_ref[2:3, :]
    shift_m, scale_m, gate_m = mod_ref[3:4, :], mod_ref[4:5, :], mod_ref[5:6, :]

    def token_major(ref, dil):
        if dil == 1:
            return ref[...].astype(F32)
        for r in range(dil):
            for j in range(N_PAIRS):
                c = r * MIX_B + j * LANES
                ob_scr[j, pl.ds(r, tt // dil, stride=dil), :] = ref[:, c:c + LANES].astype(F32)
        return jnp.concatenate([ob_scr[j] for j in range(N_PAIRS)], axis=1)

    l1, l2, l3 = l1_ref[...], l2_ref[...], l3_ref[...]
    mx = jnp.maximum(jnp.maximum(l1, l2), l3)
    e1, e2, e3 = jnp.exp(l1 - mx), jnp.exp(l2 - mx), jnp.exp(l3 - mx)
    den = e1 + e2 + e3
    dils = [dil for _, dil in DILATED_BRANCHES]
    ob = _expand_heads(e1 / den, MIX_B) * token_major(ob1_ref, dils[0])
    ob = ob + _expand_heads(e2 / den, MIX_B) * token_major(ob2_ref, dils[1])
    ob = ob + _expand_heads(e3 / den, MIX_B) * token_major(ob3_ref, dils[2])
    ob = _rms_rows(ob) * gob_ref[...]
    oa = _rms_rows(oa_ref[...].astype(F32)) * goa_ref[...]
    y = (jnp.dot(oa.astype(BF16), wo_ref[0:MIX_A, :], preferred_element_type=F32)
         + jnp.dot(ob.astype(BF16), wo_ref[MIX_A:MIX_A + MIX_B, :], preferred_element_type=F32))
    x1 = x_ref[...] + gate_a * y
    h2 = _rms_rows(x1) * gf_ref[...] * (1.0 + scale_m) + shift_m
    h2_ref[...] = h2
    _store_row_tiles(h2t_ref, h2)
    hb = h2.astype(BF16)
    act = (_silu(jnp.dot(hb, wgs_ref[...], preferred_element_type=F32))
           * jnp.dot(hb, wus_ref[...], preferred_element_type=F32))
    shared = jnp.dot(act.astype(BF16), wds_ref[...], preferred_element_type=F32)
    base_ref[...] = x1 + gate_m * shared


def _outproj(xf, mod, oa, obs, lses, goa, gob, wo_p, gf, wgs, wus, wds, seq):
    n, d = xf.shape
    tt = TT_PROJ // 2
    tiles_per_seq = seq // tt
    tile = lambda w: pl.BlockSpec((tt, w), lambda i: (i, 0))
    full = lambda shape: pl.BlockSpec(shape, lambda i: (0,) * len(shape))
    sd = wgs.shape[1]
    dilated = [pl.BlockSpec((tt // dil, dil * MIX_B), lambda i: (i, 0)) for _, dil in DILATED_BRANCHES]
    return pl.pallas_call(
        _outproj_kernel,
        grid=(n // tt,),
        in_specs=[tile(d), pl.BlockSpec((None, 6, d), lambda i: (i // tiles_per_seq, 0, 0)),
                  tile(MIX_A), *dilated,
                  tile(N_HEADS_B), tile(N_HEADS_B), tile(N_HEADS_B),
                  full((1, MIX_A)), full((1, MIX_B)), full((MIX_A + MIX_B, d)), full((1, d)),
                  full((d, sd)), full((d, sd)), full((sd, d))],
        out_specs=[tile(d), pl.BlockSpec((tt * d // LANES, LANES), lambda i: (i, 0)), tile(d)],
        out_shape=[jax.ShapeDtypeStruct((n, d), F32), jax.ShapeDtypeStruct((n * d // LANES, LANES), F32),
                   jax.ShapeDtypeStruct((n, d), F32)],
        scratch_shapes=[pltpu.VMEM((N_PAIRS, tt, LANES), F32)],
        compiler_params=_params("parallel"),
        name="outproj",
    )(xf, mod, oa, *obs, *lses, goa, gob, wo_p, gf, wgs, wus, wds)


def _router_kernel(h_ref, wrt_ref, bias_ref, e_ref, g_ref, cnt_ref):
    tt = h_ref.shape[0]
    logits = lax.dot_general(wrt_ref[...], h_ref[...].astype(BF16), (((1,), (1,)), ((), ())),
                             preferred_element_type=F32)
    scores = 1.0 / (1.0 + jnp.exp(-logits))
    biased = scores + bias_ref[...]
    ninf = -jnp.inf

    j32 = lax.broadcasted_iota(I32, (GROUP_SIZE, tt), 0).astype(F32)
    grp = []
    for g in range(N_GROUPS):
        bg = biased[g * GROUP_SIZE:(g + 1) * GROUP_SIZE, :]
        m1 = jnp.max(bg, axis=0, keepdims=True)
        i1 = jnp.min(jnp.where(bg == m1, j32, float(GROUP_SIZE)), axis=0, keepdims=True)
        m2 = jnp.max(jnp.where(j32 == i1, ninf, bg), axis=0, keepdims=True)
        grp.append(m1 + m2)
    grp = jnp.concatenate(grp, axis=0)
    g8 = lax.broadcasted_iota(I32, (N_GROUPS, tt), 0).astype(F32)
    chosen = jnp.zeros((N_GROUPS, tt), F32)
    for _ in range(TOPK_GROUPS):
        gm = jnp.max(grp, axis=0, keepdims=True)
        gi = jnp.min(jnp.where(grp == gm, g8, float(N_GROUPS)), axis=0, keepdims=True)
        hit = g8 == gi
        chosen = jnp.where(hit, 1.0, chosen)
        grp = jnp.where(hit, ninf, grp)
    masked = jnp.concatenate(
        [jnp.where(chosen[g:g + 1, :] > 0.0, biased[g * GROUP_SIZE:(g + 1) * GROUP_SIZE, :], ninf)
         for g in range(N_GROUPS)], axis=0)

    eio = lax.broadcasted_iota(I32, (N_EXPERTS, tt), 0).astype(F32)
    picked = jnp.zeros((N_EXPERTS, tt), F32)
    es, gs = [], []
    for _ in range(TOP_K):
        m = jnp.max(masked, axis=0, keepdims=True)
        idx = jnp.min(jnp.where(masked == m, eio, float(N_EXPERTS)), axis=0, keepdims=True)
        hit = eio == idx
        gs.append(jnp.sum(jnp.where(hit, scores, 0.0), axis=0, keepdims=True))
        es.append(idx)
        picked = jnp.where(hit, 1.0, picked)
        masked = jnp.where(hit, ninf, masked)
    gates = jnp.concatenate(gs, axis=0)
    e_ref[...] = jnp.concatenate(es, axis=0).astype(I32)
    g_ref[...] = gates / jnp.sum(gates, axis=0, keepdims=True) * ROUTED_SCALE

    @pl.when(pl.program_id(0) == 0)
    def _():
        cnt_ref[...] = jnp.zeros_like(cnt_ref)
    cnt_ref[...] += jnp.sum(picked, axis=1, keepdims=True)


def _router(h2, wrt, bias_col):
    n, d = h2.shape
    tt = TT_ROUTE
    return pl.pallas_call(
        _router_kernel,
        grid=(n // tt,),
        in_specs=[pl.BlockSpec((tt, d), lambda i: (i, 0)),
                  pl.BlockSpec((N_EXPERTS, d), lambda i: (0, 0)),
                  pl.BlockSpec((N_EXPERTS, 1), lambda i: (0, 0))],
        out_specs=[pl.BlockSpec((TOP_K, tt), lambda i: (0, i)),
                   pl.BlockSpec((TOP_K, tt), lambda i: (0, i)),
                   pl.BlockSpec((N_EXPERTS, 1), lambda i: (0, 0))],
        out_shape=[jax.ShapeDtypeStruct((TOP_K, n), I32),
                   jax.ShapeDtypeStruct((TOP_K, n), F32),
                   jax.ShapeDtypeStruct((N_EXPERTS, 1), F32)],
        compiler_params=_params("arbitrary"),
        name="router",
    )(h2, wrt, bias_col)


def _rank_kernel(e_ref, pstart_ref, tri_ref, dest_ref, carry_ref):
    tt = e_ref.shape[1]

    @pl.when(pl.program_id(0) == 0)
    def _():
        carry_ref[...] = pstart_ref[...]

    e = e_ref[...]
    eio = lax.broadcasted_iota(I32, (N_EXPERTS, tt), 0)
    mask = jnp.zeros((N_EXPERTS, tt), F32)
    for k in range(TOP_K):
        mask = jnp.where(eio == e[k:k + 1, :], 1.0, mask)
    incl = jnp.dot(mask.astype(BF16), tri_ref[...], preferred_element_type=F32)
    pos = incl - mask + carry_ref[...]
    dest = [jnp.sum(jnp.where(eio == e[k:k + 1, :], pos, 0.0), axis=0, keepdims=True)
            for k in range(TOP_K)]
    dest_ref[...] = jnp.concatenate(dest, axis=0).astype(I32)
    carry_ref[...] += incl[:, tt - 1:tt]


def _rank(top_e_t, pstart_col, tri):
    n = top_e_t.shape[1]
    tt = TT_ROUTE
    return pl.pallas_call(
        _rank_kernel,
        grid=(n // tt,),
        in_specs=[pl.BlockSpec((TOP_K, tt), lambda i: (0, i)),
                  pl.BlockSpec((N_EXPERTS, 1), lambda i: (0, 0)),
                  pl.BlockSpec((tt, tt), lambda i: (0, 0))],
        out_specs=pl.BlockSpec((TOP_K, tt), lambda i: (0, i)),
        out_shape=jax.ShapeDtypeStruct((TOP_K, n), I32),
        scratch_shapes=[pltpu.VMEM((N_EXPERTS, 1), F32)],
        compiler_params=_params("arbitrary"),
        name="rank",
    )(top_e_t, pstart_col, tri)


def _row_copy(src_ref, src_row, dst_ref, dst_row, sem):
    return pltpu.make_async_copy(src_ref.at[pl.ds(src_row, 1)], dst_ref.at[pl.ds(dst_row, 1)], sem)


SC_CORES, SC_SUBCORES, SC_LANES = 2, 16, 16
SC_WORKERS = SC_CORES * SC_SUBCORES
INVERT_CHUNK = 8192


def _invert(dest_flat, n_tokens, rows):
    per = rows // SC_WORKERS
    n_assign = dest_flat.shape[0]
    assert rows % (SC_WORKERS * SC_LANES) == 0 and n_assign % INVERT_CHUNK == 0
    assert n_tokens & (n_tokens - 1) == 0

    @functools.partial(
        pl.kernel, mesh=plsc.VectorSubcoreMesh(core_axis_name="c", subcore_axis_name="s"),
        out_type=jax.ShapeDtypeStruct((rows,), I32),
        scratch_types=[pltpu.VMEM((INVERT_CHUNK,), I32), pltpu.VMEM((per,), I32)],
        compiler_params=pltpu.CompilerParams(needs_layout_passes=False))
    def invert(dest_hbm, out_hbm, staged, local):
        base = (lax.axis_index("s") * SC_CORES + lax.axis_index("c")) * per
        lane = lax.iota(I32, SC_LANES)

        @pl.loop(0, per, step=SC_LANES)
        def _(i):
            local[pl.ds(i, SC_LANES)] = (base + i + lane) & (n_tokens - 1)

        @pl.loop(0, n_assign // INVERT_CHUNK)
        def _(c):
            pltpu.sync_copy(dest_hbm.at[pl.ds(c * INVERT_CHUNK, INVERT_CHUNK)], staged)

            @pl.loop(0, INVERT_CHUNK, step=SC_LANES)
            def _(i):
                rel = staged[pl.ds(i, SC_LANES)] - base
                mine = (rel >= 0) & (rel < per)
                tok = (c * INVERT_CHUNK + i + lane) & (n_tokens - 1)
                plsc.store_scatter(local, [jnp.where(mine, rel, 0)], tok, mask=mine)

        pltpu.sync_copy(local, out_hbm.at[pl.ds(base, per)])

    return invert(dest_flat)


def _experts_kernel(rank_ref, eor_ref, nv_ref, nr_ref, tok_hbm, h_ref, wg_hbm, wu_hbm, wd_hbm,
                    ys_ref, *scratch):
    xbufs = scratch[:GATHER_DEPTH]
    wgf, wuf, wdf, wgb, wub, wdb, tok_s, sem, tok_sem, w_sem = scratch[GATHER_DEPTH:]
    j = pl.program_id(0)
    nv, n_ranks = nv_ref[0], nr_ref[0]
    per = wgb.shape[0] // LANES
    depth, bm = len(xbufs), xbufs[0].shape[0] // per
    ahead = depth - 1
    chunk = tok_s.shape[1]
    cb = chunk // bm
    n_chunks = tok_hbm.shape[0] // chunk
    n_blocks = n_chunks * cb

    def tok_copy(c):
        return pltpu.make_async_copy(tok_hbm.at[pl.ds(pl.multiple_of(c * chunk, chunk), chunk)],
                                     tok_s.at[c % 2], tok_sem.at[c % 2])

    def gather(block, ring, unrolled):
        block = jnp.minimum(block, n_blocks - 1)
        slot, off = (block // cb) % 2, (block % cb) * bm
        buf, buf_sem = xbufs[ring], sem.at[ring]

        def issue(i, carry=None):
            _tile_copy(h_ref, tok_s[slot, off + i], buf, i, per, buf_sem).start()
            return carry

        if unrolled:
            for i in range(bm):
                _tile_copy(h_ref, tok_s[slot, off + i], buf, i, per, buf_sem).start(
                    priority=int(i % ROW_QUEUE_SPLIT != 0))
        else:
            lax.fori_loop(0, bm, issue, 0, unroll=8)

    def wait_rows(ring):
        pltpu.make_async_copy(h_ref.at[pl.ds(0, bm * per)], xbufs[ring], sem.at[ring]).wait()

    def weight_copies(r):
        e, s = eor_ref[r], r % WEIGHT_DEPTH
        return [pltpu.make_async_copy(src.at[e], dst.at[s], w_sem.at[s])
                for src, dst in ((wg_hbm, wgf), (wu_hbm, wuf), (wd_hbm, wdf))]

    @pl.when(j == 0)
    def _():
        tok_copy(0).start()
        tok_copy(0).wait()
        tok_copy(1).start()
        for r in range(WEIGHT_DEPTH - 1):
            @pl.when(r < n_ranks)
            def _():
                for cp in weight_copies(r):
                    cp.start()
        for b in range(ahead):
            gather(b, b, unrolled=False)

    first = j + ahead
    c_need = first // cb

    @pl.when(jnp.logical_and(jnp.logical_and(j > 0, first % cb == 0), c_need < n_chunks))
    def _():
        tok_copy(c_need).wait()

        @pl.when(c_need + 1 < n_chunks)
        def _():
            tok_copy(c_need + 1).start()

    jb = jnp.minimum(j, n_blocks - 1)
    rank = rank_ref[jb]
    fresh = jnp.logical_or(j == 0, rank != rank_ref[jnp.maximum(jb - 1, 0)])

    @pl.when(jnp.logical_and(j < nv, fresh))
    def _():
        for cp in weight_copies(rank):
            cp.wait()
        s = rank % WEIGHT_DEPTH
        wgb[...] = wgf[s].astype(BF16)
        wub[...] = wuf[s].astype(BF16)
        wdb[...] = wdf[s].astype(BF16)
        nxt = rank + WEIGHT_DEPTH - 1

        @pl.when(nxt < n_ranks)
        def _():
            for cp in weight_copies(nxt):
                cp.start()

    for ring in range(depth):
        mine = j % depth == ring

        @pl.when(jnp.logical_and(j < nv, mine))
        def _():
            wait_rows(ring)
            gather(j + ahead, (ring + ahead) % depth, unrolled=True)
            xb = _load_row_tiles(xbufs[ring], bm, per, BF16)
            act = (_silu(jnp.dot(xb, wgb[...], preferred_element_type=F32))
                   * jnp.dot(xb, wub[...], preferred_element_type=F32))
            _store_row_tiles(ys_ref, jnp.dot(act.astype(BF16), wdb[...], preferred_element_type=F32))

        @pl.when(jnp.logical_and(jnp.logical_and(j >= nv, j < nv + ahead), mine))
        def _():
            wait_rows(ring)


def _experts(block_rank, expert_of_rank, n_valid, n_ranks, row_tok, h2_tiles, wg, wu, wd):
    rows = row_tok.shape[0]
    d = wg.shape[1]
    per = d // LANES
    bm = EXPERT_ROWS
    n_blocks = rows // bm
    assert n_blocks % TOK_CHUNK_BLOCKS == 0 and n_blocks // TOK_CHUNK_BLOCKS >= 2
    assert GATHER_DEPTH - 1 < TOK_CHUNK_BLOCKS
    f = wg.shape[2]
    hbm = pl.BlockSpec(memory_space=pl.ANY)
    return pl.pallas_call(
        _experts_kernel,
        grid_spec=pltpu.PrefetchScalarGridSpec(
            num_scalar_prefetch=4,
            grid=(n_blocks + GATHER_DEPTH,),
            in_specs=[hbm, hbm, hbm, hbm, hbm],
            out_specs=pl.BlockSpec((bm * per, LANES),
                                   lambda j, rk, eor, nv, nr: (jnp.minimum(j, nv[0] - 1), 0)),
            scratch_shapes=[pltpu.VMEM((bm * per, LANES), F32)] * GATHER_DEPTH + [
                            pltpu.VMEM((WEIGHT_DEPTH, d, f), F32), pltpu.VMEM((WEIGHT_DEPTH, d, f), F32),
                            pltpu.VMEM((WEIGHT_DEPTH, f, d), F32),
                            pltpu.VMEM((d, f), BF16), pltpu.VMEM((d, f), BF16),
                            pltpu.VMEM((f, d), BF16),
                            pltpu.SMEM((2, TOK_CHUNK_BLOCKS * bm), I32),
                            pltpu.SemaphoreType.DMA((GATHER_DEPTH,)), pltpu.SemaphoreType.DMA((2,)),
                            pltpu.SemaphoreType.DMA((WEIGHT_DEPTH,))]),
        out_shape=jax.ShapeDtypeStruct((rows * per, LANES), F32),
        compiler_params=_params("arbitrary"),
        name="experts",
    )(block_rank, expert_of_rank, n_valid, n_ranks, row_tok, h2_tiles, wg, wu, wd)


def _combine_kernel(dest_ref, gates_ref, base_ref, mod_ref, ys_ref, out_ref, buf, sem):
    tt, d = base_ref.shape
    per = d // LANES
    for t in range(tt):
        for k in range(TOP_K):
            _tile_copy(ys_ref, dest_ref[k, t], buf.at[k], t, per, sem).start(priority=k % 2)
    for k in range(TOP_K):
        pltpu.make_async_copy(ys_ref.at[pl.ds(0, tt * per)], buf.at[k], sem).wait()
    gates = gates_ref[...]
    routed = gates[:, 0:1] * _load_row_tiles(buf.at[0], tt, per, F32)
    for k in range(1, TOP_K):
        routed = routed + gates[:, k:k + 1] * _load_row_tiles(buf.at[k], tt, per, F32)
    out_ref[...] = base_ref[...] + mod_ref[5:6, :] * routed


def _combine(dest_t, gates, base, mod, ys, seq):
    n, d = base.shape
    tt = TT_COMBINE
    tiles_per_seq = seq // tt
    return pl.pallas_call(
        _combine_kernel,
        grid=(n // tt,),
        in_specs=[pl.BlockSpec((TOP_K, tt), lambda i: (0, i), memory_space=pltpu.SMEM),
                  pl.BlockSpec((tt, TOP_K), lambda i: (i, 0)),
                  pl.BlockSpec((tt, d), lambda i: (i, 0)),
                  pl.BlockSpec((None, 6, d), lambda i: (i // tiles_per_seq, 0, 0)),
                  pl.BlockSpec(memory_space=pl.ANY)],
        out_specs=pl.BlockSpec((tt, d), lambda i: (i, 0)),
        out_shape=jax.ShapeDtypeStruct((n, d), F32),
        scratch_shapes=[pltpu.VMEM((TOP_K, tt * d // LANES, LANES), F32), pltpu.SemaphoreType.DMA],
        compiler_params=_params("arbitrary"),
        name="combine",
    )(dest_t, gates, base, mod, ys)


def _layer(x, mod, pos, rope, p):
    nbatch, seq, d = x.shape
    n = nbatch * seq
    xf = x.reshape(n, d)
    invf, bd = rope

    perm = np.concatenate([np.arange(h * HEAD_DIM, (h + 1) * HEAD_DIM) for h in PAIR_ORDER_A])
    w_in = p["w_in"]
    w_in_p = jnp.concatenate([w_in[:, :MIX_A][:, perm], w_in[:, MIX_A:]], axis=1).astype(BF16)
    ones = lambda w: jnp.ones((w,), F32)
    qscale = HEAD_DIM ** -0.5
    gcol = jnp.concatenate([jnp.tile(p["g_q_a"], N_HEADS_A) * qscale, jnp.tile(p["g_k_a"], N_KV_A),
                            ones(KV_A), jnp.tile(p["g_q_b"], N_HEADS_B) * qscale,
                            jnp.tile(p["g_k_b"], N_HEADS_B), ones(MIX_B)]).reshape(1, IN_WIDTH)
    proj = _inproj(xf, mod, pos, p["g_norm_mix"].reshape(1, d), w_in_p, gcol, invf, bd, seq)
    qa, ka, va = proj[:3]
    qkv_b = {1: proj[3:6]}
    for j, dil in enumerate(DILS):
        qkv_b[dil] = [proj[6 + t * len(DILS) + j] for t in range(3)]

    sinks_p = p["sinks_a"][np.array(PAIR_ORDER_A)]
    oa, _ = _attention(qa, ka, va, nbatch=nbatch, seq=seq, dil=1, max_dist=WINDOW_A - 1,
                       kv_shared=True, sinks=sinks_p, want_lse=False)
    obs, lses = [], []
    for window, dil in DILATED_BRANCHES:
        o, lse = _attention(*qkv_b[dil], nbatch=nbatch, seq=seq, dil=dil, max_dist=window // dil,
                            kv_shared=False, want_lse=True)
        obs.append(o)
        lses.append(lse)

    goa = p["g_out_a"][perm].reshape(1, MIX_A)
    w_out = p["w_out"]
    wo_p = jnp.concatenate([w_out[:MIX_A][perm], w_out[MIX_A:]], axis=0).astype(BF16)
    h2, h2_tiles, base = _outproj(xf, mod, oa, obs, lses, goa, p["g_out_b"].reshape(1, MIX_B), wo_p,
                        p["g_norm_ffn"].reshape(1, d), p["w_gate_s"].astype(BF16),
                        p["w_up_s"].astype(BF16), p["w_down_s"].astype(BF16), seq)

    top_e_t, gates_t, counts = _router(h2, p["w_router"].T.astype(BF16),
                                       p["router_bias"].reshape(N_EXPERTS, 1))
    bm = EXPERT_ROWS
    counts = counts.reshape(N_EXPERTS).astype(I32)
    padded = (counts + bm - 1) // bm * bm
    pends = jnp.cumsum(padded)
    pstarts = pends - padded
    rows = n * TOP_K + N_EXPERTS * bm
    n_blocks = rows // bm
    n_valid = (pends[-1] // bm).astype(I32)
    blk = jnp.minimum(jnp.arange(n_blocks, dtype=I32), n_valid - 1)
    block_e = jnp.sum((pends[None, :] <= (blk * bm)[:, None]).astype(I32), axis=1)
    block_e = jnp.minimum(block_e, N_EXPERTS - 1)
    nonempty = counts > 0
    rank_of_e = jnp.cumsum(nonempty.astype(I32)) - 1
    e_ids = jnp.arange(N_EXPERTS, dtype=I32)
    block_rank = jnp.sum(jnp.where(block_e[:, None] == e_ids[None, :], rank_of_e[None, :], 0), axis=1)
    hit = (rank_of_e[None, :] == jnp.arange(n_blocks, dtype=I32)[:, None]) & nonempty[None, :]
    expert_of_rank = jnp.sum(jnp.where(hit, e_ids[None, :], 0), axis=1)
    n_ranks = jnp.sum(nonempty.astype(I32)).reshape(1)

    tri =(np.arange(TT_ROUTE)[:, None] <= np.arange(TT_ROUTE)[None, :])
    dest_t = _rank(top_e_t, pstarts.astype(F32).reshape(N_EXPERTS, 1), jnp.asarray(tri, BF16))
    n_valid = n_valid.reshape(1)
    row_tok = _invert(dest_t.reshape(n * TOP_K), n, rows)
    ys = _experts(block_rank.astype(I32), expert_of_rank.astype(I32), n_valid, n_ranks, row_tok, h2_tiles,
                  p["w_gate_e"], p["w_up_e"], p["w_down_e"])
    out = _combine(dest_t, gates_t.T, base, mod, ys, seq)
    return out.reshape(nbatch, seq, d)


def kernel(x, c, positions, w_ada, b_ada, g_norm_mix, w_in, g_q_a, g_k_a, sinks_a, g_q_b, g_k_b,
           g_out_a, g_out_b, w_out, g_norm_ffn, w_router, router_bias, w_gate_e, w_up_e, w_down_e,
           w_gate_s, w_up_s, w_down_s):
    nbatch, seq, d = x.shape
    depth = w_ada.shape[0]
    params = dict(g_norm_mix=g_norm_mix, w_in=w_in, g_q_a=g_q_a, g_k_a=g_k_a, sinks_a=sinks_a,
                  g_q_b=g_q_b, g_k_b=g_k_b, g_out_a=g_out_a, g_out_b=g_out_b, w_out=w_out,
                  g_norm_ffn=g_norm_ffn, w_router=w_router, router_bias=router_bias,
                  w_gate_e=w_gate_e, w_up_e=w_up_e, w_down_e=w_down_e, w_gate_s=w_gate_s,
                  w_up_s=w_up_s, w_down_s=w_down_s)
    j = np.arange(LANES) % HEAD_DIM
    inv = ROPE_THETA ** (-jnp.arange(0, ROT_DIM, 2, dtype=F32) / ROT_DIM)
    invf = jnp.where(j < ROT_DIM, inv[j % (ROT_DIM // 2)], 0.0).astype(F32).reshape(1, LANES)
    bd = jnp.asarray((np.arange(LANES)[:, None] // HEAD_DIM) == (np.arange(LANES)[None, :] // HEAD_DIM),
                     BF16)
    pos = positions.reshape(nbatch * seq, 1).astype(I32)
    for l in range(depth):
        mod = _adaln(c.astype(F32), w_ada[l], b_ada[l]).reshape(nbatch, 6, d)
        x = _layer(x, mod, pos, (invf, bd), {k: v[l] for k, v in params.items()})
    return x
```

```python
import functools

import numpy as np
import jax
import jax.numpy as jnp
from jax import lax
from jax.experimental import pallas as pl
from jax.experimental.pallas import tpu as pltpu
from jax.experimental.pallas import tpu_sc as plsc

F32 = jnp.float32
BF16 = jnp.bfloat16
I32 = jnp.int32

HEAD_DIM = 64
N_HEADS_A = 8
N_KV_A = 2
WINDOW_A = 128
N_HEADS_B = 8
DILATED_BRANCHES = ((128, 1), (512, 4), (2048, 16))
DILS = tuple(dil for _, dil in DILATED_BRANCHES if dil > 1)
BLOCK = 128
ROT_DIM = HEAD_DIM // 4
ROPE_THETA = 500000.0
MIX_A = N_HEADS_A * HEAD_DIM
KV_A = N_KV_A * HEAD_DIM
MIX_B = N_HEADS_B * HEAD_DIM
N_EXPERTS = 256
TOP_K = 8
N_GROUPS = 8
TOPK_GROUPS = 4
GROUP_SIZE = N_EXPERTS // N_GROUPS
ROUTED_SCALE = 2.5
EPS = 1e-6

LANES = 128
HEADS_PER_VREG = LANES // HEAD_DIM
N_PAIRS = MIX_A // LANES
NEG = -1e30
VMEM_LIMIT = 48 * 1024 * 1024

TT_PROJ = 512
ATTN_BLOCKS_PER_STEP = 4
TT_ROUTE = 256
TT_DISPATCH = 256
TT_COMBINE = 128
EXPERT_ROWS = 256
TOK_CHUNK_BLOCKS = 32
GATHER_DEPTH = 4
WEIGHT_DEPTH = 3
SUBLANES = 8
ROW_QUEUE_SPLIT = 8

PAIR_ORDER_A = tuple(h for p in range(N_PAIRS) for h in (p, p + N_HEADS_A // N_KV_A))


def _params(*sem):
    return pltpu.CompilerParams(dimension_semantics=sem, vmem_limit_bytes=VMEM_LIMIT)


def _silu(t):
    return t / (1.0 + jnp.exp(-t))


def _rms_rows(t):
    return t * lax.rsqrt(jnp.mean(t * t, axis=-1, keepdims=True) + EPS)


def _store_row_tiles(ref, value):
    rows, d = value.shape
    per = d // LANES
    for s in range(per):
        ref[pl.ds(s, rows, stride=per), :] = value[:, s * LANES:(s + 1) * LANES]


def _load_row_tiles(ref, rows, per, dtype):
    return jnp.concatenate([ref[pl.ds(s, rows, stride=per), :].astype(dtype) for s in range(per)], axis=1)


def _tile_copy(src_ref, src_row, dst_ref, dst_row, per, sem):
    start = lambda row: row * per if isinstance(row, int) else pl.multiple_of(row * per, per)
    return pltpu.make_async_copy(src_ref.at[pl.ds(start(src_row), per)],
                                 dst_ref.at[pl.ds(start(dst_row), per)], sem)


def _ada_kernel(c_ref, w_ref, b_ref, o_ref):
    cond = _silu(c_ref[...])
    o_ref[...] = jnp.dot(cond.astype(BF16), w_ref[...].astype(BF16),
                         preferred_element_type=F32) + b_ref[...]


def _adaln(c, w_ada, b_ada):
    nb, d = c.shape
    width = w_ada.shape[1]
    tn = 1024
    return pl.pallas_call(
        _ada_kernel,
        grid=(width // tn,),
        in_specs=[pl.BlockSpec((nb, d), lambda j: (0, 0)),
                  pl.BlockSpec((d, tn), lambda j: (0, j)),
                  pl.BlockSpec((1, tn), lambda j: (0, j))],
        out_specs=pl.BlockSpec((nb, tn), lambda j: (0, j)),
        out_shape=jax.ShapeDtypeStruct((nb, width), F32),
        compiler_params=_params("arbitrary"),
        name="adaln",
    )(c, w_ada, b_ada.reshape(1, width))


COL_QA, COL_KA, COL_VA = 0, MIX_A, MIX_A + KV_A
COL_QB = MIX_A + 2 * KV_A
COL_KB, COL_VB = COL_QB + MIX_B, COL_QB + 2 * MIX_B
IN_WIDTH = COL_VB + MIX_B


def _inproj_kernel(x_ref, mod_ref, pos_ref, gn_ref, w_ref, gcol_ref, invf_ref, bd_ref,
                   qa_ref, ka_ref, va_ref, qb_ref, kb_ref, vb_ref, *rest):
    n_dil = len(DILS)
    dil_refs = [rest[i * n_dil:(i + 1) * n_dil] for i in range(3)]
    qb_scr, kb_scr, vb_scr = rest[3 * n_dil:]
    tt = x_ref.shape[0]
    shift, scale = mod_ref[0:1, :], mod_ref[1:2, :]
    h = _rms_rows(x_ref[...]) * gn_ref[...] * (1.0 + scale) + shift
    proj = jnp.dot(h.astype(BF16), w_ref[...], preferred_element_type=F32)

    ang = pos_ref[...].astype(F32) * invf_ref[...]
    cs, sn = jnp.cos(ang), jnp.sin(ang)
    lane = lax.broadcasted_iota(I32, (1, LANES), 1) % HEAD_DIM
    s_lo = jnp.where(lane < ROT_DIM // 2, -sn, 0.0)
    s_hi = jnp.where((lane >= ROT_DIM // 2) & (lane < ROT_DIM), sn, 0.0)
    bd = bd_ref[...]

    def norm_rope(col0, width, out_ref, scr=None):
        for j in range(width // LANES):
            c = col0 + j * LANES
            t = proj[:, c:c + LANES]
            sq = t * t
            hi = sq.astype(BF16)
            lo = (sq - hi.astype(F32)).astype(BF16)
            ss = (jnp.dot(hi, bd, preferred_element_type=F32)
                  + jnp.dot(lo, bd, preferred_element_type=F32))
            t = t * lax.rsqrt(ss * (1.0 / HEAD_DIM) + EPS) * gcol_ref[:, c:c + LANES]
            t = (t * cs + pltpu.roll(t, LANES - ROT_DIM // 2, 1) * s_lo
                 + pltpu.roll(t, ROT_DIM // 2, 1) * s_hi)
            out_ref[:, j * LANES:(j + 1) * LANES] = t.astype(BF16)
            if scr is not None:
                scr[j] = t

    norm_rope(COL_QA, MIX_A, qa_ref)
    norm_rope(COL_KA, KV_A, ka_ref)
    norm_rope(COL_QB, MIX_B, qb_ref, qb_scr)
    norm_rope(COL_KB, MIX_B, kb_ref, kb_scr)
    va_ref[...] = proj[:, COL_VA:COL_VA + KV_A].astype(BF16)
    vb_ref[...] = proj[:, COL_VB:COL_VB + MIX_B].astype(BF16)
    for j in range(N_PAIRS):
        vb_scr[j] = proj[:, COL_VB + j * LANES:COL_VB + (j + 1) * LANES]
    for scr, outs in zip((qb_scr, kb_scr, vb_scr), dil_refs):
        for dil, out in zip(DILS, outs):
            for r in range(dil):
                for j in range(N_PAIRS):
                    c = r * MIX_B + j * LANES
                    out[:, c:c + LANES] = scr[j, pl.ds(r, tt // dil, stride=dil), :].astype(BF16)


def _inproj(xf, mod, pos, g_norm, w_in_p, gcol, invf, bd, seq):
    n, d = xf.shape
    tt = TT_PROJ
    tiles_per_seq = seq // tt
    shapes = [(n, w) for w in (MIX_A, KV_A, KV_A, MIX_B, MIX_B, MIX_B)]
    shapes += [(n // dil, dil * MIX_B) for _ in range(3) for dil in DILS]
    full = lambda shape: pl.BlockSpec(shape, lambda i: (0,) * len(shape))
    return pl.pallas_call(
        _inproj_kernel,
        grid=(n // tt,),
        in_specs=[pl.BlockSpec((tt, d), lambda i: (i, 0)),
                  pl.BlockSpec((None, 6, d), lambda i: (i // tiles_per_seq, 0, 0)),
                  pl.BlockSpec((tt, 1), lambda i: (i, 0)),
                  full((1, d)), full((d, IN_WIDTH)), full((1, IN_WIDTH)),
                  full((1, LANES)), full((LANES, LANES))],
        out_specs=[pl.BlockSpec((tt * r // n, w), lambda i: (i, 0)) for r, w in shapes],
        out_shape=[jax.ShapeDtypeStruct(s, BF16) for s in shapes],
        scratch_shapes=[pltpu.VMEM((N_PAIRS, tt, LANES), F32)] * 3,
        compiler_params=_params("parallel"),
        name="inproj",
    )(xf, mod, pos, g_norm, w_in_p, gcol, invf, bd)


def _attn_kernel(*refs, kv_shared, max_dist, n_qblk, n_res, use_prev, has_sinks, want_lse):
    refs = list(refs)
    sink_ref = refs.pop(0) if has_sinks else None
    q_ref = refs.pop(0)
    kp_ref = refs.pop(0) if use_prev else None
    kc_ref = refs.pop(0)
    vp_ref = refs.pop(0) if use_prev else None
    vc_ref = refs.pop(0)
    o_ref = refs.pop(0)
    lse_ref = refs.pop(0) if want_lse else None

    first_step = pl.program_id(2) == 0
    nq = 2 * BLOCK
    kw = LANES if kv_shared else MIX_B

    def band(nk):
        qpos = lax.broadcasted_iota(I32, (nq, nk), 0) % BLOCK
        kpos = lax.broadcasted_iota(I32, (nq, nk), 1)
        dist = qpos + (nk - BLOCK) - kpos
        return (dist >= 0) & (dist <= max_dist), kpos

    band2, kpos2 = band(2 * BLOCK)
    band1, _ = band(BLOCK)
    lane = lax.broadcasted_iota(I32, (nq, LANES), 1)
    row = lax.broadcasted_iota(I32, (nq, LANES), 0)
    own_half = (lane < HEAD_DIM) == (row < BLOCK)
    left_lanes = lax.broadcasted_iota(I32, (BLOCK, LANES), 1) < HEAD_DIM
    lane8 = lax.broadcasted_iota(I32, (BLOCK, 2 * N_PAIRS), 1)

    for res in range(n_res):
        for qb in range(n_qblk):
            rows = slice(qb * BLOCK, (qb + 1) * BLOCK)
            lse_blk = jnp.zeros((BLOCK, 2 * N_PAIRS), F32)
            for p in range(N_PAIRS):
                cq = slice(res * MIX_B + p * LANES, res * MIX_B + (p + 1) * LANES)
                ck = slice(res * kw, res * kw + LANES) if kv_shared else slice(
                    res * kw + p * LANES, res * kw + (p + 1) * LANES)
                qp = q_ref[rows, cq]
                qs = jnp.concatenate([qp, qp], axis=0)
                qs = jnp.where(own_half, qs, jnp.zeros_like(qs))
                if qb > 0:
                    keys = slice((qb - 1) * BLOCK, (qb + 1) * BLOCK)
                    k, v, valid = kc_ref[keys, ck], vc_ref[keys, ck], band2
                elif use_prev:
                    k = jnp.concatenate([kp_ref[:, ck], kc_ref[rows, ck]], axis=0)
                    v = jnp.concatenate([vp_ref[:, ck], vc_ref[rows, ck]], axis=0)
                    valid = band2 & ((kpos2 >= BLOCK) | jnp.logical_not(first_step))
                else:
                    k, v, valid = kc_ref[rows, ck], vc_ref[rows, ck], band1
                s = lax.dot_general(qs, k, (((1,), (1,)), ((), ())), preferred_element_type=F32)
                s = jnp.where(valid, s, NEG)
                m = jnp.max(s, axis=-1, keepdims=True)
                if has_sinks:
                    rows1 = lax.broadcasted_iota(I32, (nq, 1), 0)
                    sink = jnp.where(rows1 < BLOCK, sink_ref[2 * p], sink_ref[2 * p + 1])
                    m = jnp.maximum(m, sink)
                e = jnp.exp(s - m)
                l = jnp.sum(e, axis=-1, keepdims=True)
                if has_sinks:
                    l = l + jnp.exp(sink - m)
                o = jnp.dot(e.astype(BF16), v, preferred_element_type=F32) / l
                o_ref[rows, cq] = jnp.where(left_lanes, o[:BLOCK], o[BLOCK:]).astype(BF16)
                if want_lse:
                    lse = m + jnp.log(l)
                    lse_blk = (lse_blk + jnp.where(lane8 == 2 * p, lse[:BLOCK], 0.0)
                               + jnp.where(lane8 == 2 * p + 1, lse[BLOCK:], 0.0))
            if want_lse:
                lse_ref[res, rows, :] = lse_blk


def _attention(q, k, v, *, nbatch, seq, dil, max_dist, kv_shared, sinks=None, want_lse):
    length = seq // dil
    nblk = length // BLOCK
    n_qblk = min(nblk, ATTN_BLOCKS_PER_STEP)
    n_res = min(dil, ATTN_BLOCKS_PER_STEP // n_qblk)
    steps = nblk // n_qblk
    use_prev = steps > 1
    kw = k.shape[1] // dil
    view = lambda t: t.reshape(nbatch, length, t.shape[1])
    cur = lambda b, r, i: (b, i, r)
    prev = lambda b, r, i: (b, jnp.maximum(i * n_qblk - 1, 0), r)
    in_specs, args = [], []
    if sinks is not None:
        in_specs.append(pl.BlockSpec(memory_space=pltpu.SMEM))
        args.append(sinks)
    in_specs.append(pl.BlockSpec((None, n_qblk * BLOCK, n_res * MIX_B), cur))
    args.append(view(q))
    for t in (k, v):
        if use_prev:
            in_specs.append(pl.BlockSpec((None, BLOCK, n_res * kw), prev))
            args.append(view(t))
        in_specs.append(pl.BlockSpec((None, n_qblk * BLOCK, n_res * kw), cur))
        args.append(view(t))
    out_specs = [pl.BlockSpec((None, n_qblk * BLOCK, n_res * MIX_B), cur)]
    out_shape = [jax.ShapeDtypeStruct((nbatch, length, dil * MIX_B), BF16)]
    if want_lse:
        out_specs.append(pl.BlockSpec((None, n_res, n_qblk * BLOCK, N_HEADS_B),
                                      lambda b, r, i: (b, r, i, 0)))
        out_shape.append(jax.ShapeDtypeStruct((nbatch, dil, length, N_HEADS_B), F32))
    outs = pl.pallas_call(
        functools.partial(_attn_kernel, kv_shared=kv_shared, max_dist=max_dist, n_qblk=n_qblk,
                          n_res=n_res, use_prev=use_prev, has_sinks=sinks is not None,
                          want_lse=want_lse),
        grid=(nbatch, dil // n_res, steps),
        in_specs=in_specs, out_specs=out_specs, out_shape=out_shape,
        compiler_params=_params("parallel", "parallel", "arbitrary"),
        name=f"attn_d{dil}" + ("_swa" if kv_shared else ""),
    )(*args)
    o = outs[0].reshape(nbatch * length, dil * MIX_B)
    if not want_lse:
        return o, None
    lse = outs[1].transpose(0, 2, 1, 3).reshape(nbatch * seq, N_HEADS_B)
    return o, lse


def _expand_heads(w, width):
    head = lax.broadcasted_iota(I32, (1, width), 1) // HEAD_DIM
    out = jnp.zeros((w.shape[0], width), F32)
    for hd in range(w.shape[1]):
        out = jnp.where(head == hd, w[:, hd:hd + 1], out)
    return out


def _outproj_kernel(x_ref, mod_ref, oa_ref, ob1_ref, ob2_ref, ob3_ref, l1_ref, l2_ref, l3_ref,
                    goa_ref, gob_ref, wo_ref, gf_ref, wgs_ref, wus_ref, wds_ref,
                    h2_ref, h2t_ref, base_ref, ob_scr):
    tt = x_ref.shape[0]
    gate_a = mod_ref[2:3, :]
    shift_m, scale_m, gate_m = mod_ref[3:4, :], mod_ref[4:5, :], mod_ref[5:6, :]

    def token_major(ref, dil):
        if dil == 1:
            return ref[...].astype(F32)
        for r in range(dil):
            for j in range(N_PAIRS):
                c = r * MIX_B + j * LANES
                ob_scr[j, pl.ds(r, tt // dil, stride=dil), :] = ref[:, c:c + LANES].astype(F32)
        return jnp.concatenate([ob_scr[j] for j in range(N_PAIRS)], axis=1)

    l1, l2, l3 = l1_ref[...], l2_ref[...], l3_ref[...]
    mx = jnp.maximum(jnp.maximum(l1, l2), l3)
    e1, e2, e3 = jnp.exp(l1 - mx), jnp.exp(l2 - mx), jnp.exp(l3 - mx)
    den = e1 + e2 + e3
    dils = [dil for _, dil in DILATED_BRANCHES]
    ob = _expand_heads(e1 / den, MIX_B) * token_major(ob1_ref, dils[0])
    ob = ob + _expand_heads(e2 / den, MIX_B) * token_major(ob2_ref, dils[1])
    ob = ob + _expand_heads(e3 / den, MIX_B) * token_major(ob3_ref, dils[2])
    ob = _rms_rows(ob) * gob_ref[...]
    oa = _rms_rows(oa_ref[...].astype(F32)) * goa_ref[...]
    y = (jnp.dot(oa.astype(BF16), wo_ref[0:MIX_A, :], preferred_element_type=F32)
         + jnp.dot(ob.astype(BF16), wo_ref[MIX_A:MIX_A + MIX_B, :], preferred_element_type=F32))
    x1 = x_ref[...] + gate_a * y
    h2 = _rms_rows(x1) * gf_ref[...] * (1.0 + scale_m) + shift_m
    h2_ref[...] = h2
    _store_row_tiles(h2t_ref, h2)
    hb = h2.astype(BF16)
    act = (_silu(jnp.dot(hb, wgs_ref[...], preferred_element_type=F32))
           * jnp.dot(hb, wus_ref[...], preferred_element_type=F32))
    shared = jnp.dot(act.astype(BF16), wds_ref[...], preferred_element_type=F32)
    base_ref[...] = x1 + gate_m * shared


def _outproj(xf, mod, oa, obs, lses, goa, gob, wo_p, gf, wgs, wus, wds, seq):
    n, d = xf.shape
    tt = TT_PROJ // 2
    tiles_per_seq = seq // tt
    tile = lambda w: pl.BlockSpec((tt, w), lambda i: (i, 0))
    full = lambda shape: pl.BlockSpec(shape, lambda i: (0,) * len(shape))
    sd = wgs.shape[1]
    dilated = [pl.BlockSpec((tt // dil, dil * MIX_B), lambda i: (i, 0)) for _, dil in DILATED_BRANCHES]
    return pl.pallas_call(
        _outproj_kernel,
        grid=(n // tt,),
        in_specs=[tile(d), pl.BlockSpec((None, 6, d), lambda i: (i // tiles_per_seq, 0, 0)),
                  tile(MIX_A), *dilated,
                  tile(N_HEADS_B), tile(N_HEADS_B), tile(N_HEADS_B),
                  full((1, MIX_A)), full((1, MIX_B)), full((MIX_A + MIX_B, d)), full((1, d)),
                  full((d, sd)), full((d, sd)), full((sd, d))],
        out_specs=[tile(d), pl.BlockSpec((tt * d // LANES, LANES), lambda i: (i, 0)), tile(d)],
        out_shape=[jax.ShapeDtypeStruct((n, d), F32), jax.ShapeDtypeStruct((n * d // LANES, LANES), F32),
                   jax.ShapeDtypeStruct((n, d), F32)],
        scratch_shapes=[pltpu.VMEM((N_PAIRS, tt, LANES), F32)],
        compiler_params=_params("parallel"),
        name="outproj",
    )(xf, mod, oa, *obs, *lses, goa, gob, wo_p, gf, wgs, wus, wds)


def _router_kernel(h_ref, wrt_ref, bias_ref, e_ref, g_ref, cnt_ref):
    tt = h_ref.shape[0]
    logits = lax.dot_general(wrt_ref[...], h_ref[...].astype(BF16), (((1,), (1,)), ((), ())),
                             preferred_element_type=F32)
    scores = 1.0 / (1.0 + jnp.exp(-logits))
    biased = scores + bias_ref[...]
    ninf = -jnp.inf

    j32 = lax.broadcasted_iota(I32, (GROUP_SIZE, tt), 0).astype(F32)
    grp = []
    for g in range(N_GROUPS):
        bg = biased[g * GROUP_SIZE:(g + 1) * GROUP_SIZE, :]
        m1 = jnp.max(bg, axis=0, keepdims=True)
        i1 = jnp.min(jnp.where(bg == m1, j32, float(GROUP_SIZE)), axis=0, keepdims=True)
        m2 = jnp.max(jnp.where(j32 == i1, ninf, bg), axis=0, keepdims=True)
        grp.append(m1 + m2)
    grp = jnp.concatenate(grp, axis=0)
    g8 = lax.broadcasted_iota(I32, (N_GROUPS, tt), 0).astype(F32)
    chosen = jnp.zeros((N_GROUPS, tt), F32)
    for _ in range(TOPK_GROUPS):
        gm = jnp.max(grp, axis=0, keepdims=True)
        gi = jnp.min(jnp.where(grp == gm, g8, float(N_GROUPS)), axis=0, keepdims=True)
        hit = g8 == gi
        chosen = jnp.where(hit, 1.0, chosen)
        grp = jnp.where(hit, ninf, grp)
    masked = jnp.concatenate(
        [jnp.where(chosen[g:g + 1, :] > 0.0, biased[g * GROUP_SIZE:(g + 1) * GROUP_SIZE, :], ninf)
         for g in range(N_GROUPS)], axis=0)

    eio = lax.broadcasted_iota(I32, (N_EXPERTS, tt), 0).astype(F32)
    picked = jnp.zeros((N_EXPERTS, tt), F32)
    es, gs = [], []
    for _ in range(TOP_K):
        m = jnp.max(masked, axis=0, keepdims=True)
        idx = jnp.min(jnp.where(masked == m, eio, float(N_EXPERTS)), axis=0, keepdims=True)
        hit = eio == idx
        gs.append(jnp.sum(jnp.where(hit, scores, 0.0), axis=0, keepdims=True))
        es.append(idx)
        picked = jnp.where(hit, 1.0, picked)
        masked = jnp.where(hit, ninf, masked)
    gates = jnp.concatenate(gs, axis=0)
    e_ref[...] = jnp.concatenate(es, axis=0).astype(I32)
    g_ref[...] = gates / jnp.sum(gates, axis=0, keepdims=True) * ROUTED_SCALE

    @pl.when(pl.program_id(0) == 0)
    def _():
        cnt_ref[...] = jnp.zeros_like(cnt_ref)
    cnt_ref[...] += jnp.sum(picked, axis=1, keepdims=True)


def _router(h2, wrt, bias_col):
    n, d = h2.shape
    tt = TT_ROUTE
    return pl.pallas_call(
        _router_kernel,
        grid=(n // tt,),
        in_specs=[pl.BlockSpec((tt, d), lambda i: (i, 0)),
                  pl.BlockSpec((N_EXPERTS, d), lambda i: (0, 0)),
                  pl.BlockSpec((N_EXPERTS, 1), lambda i: (0, 0))],
        out_specs=[pl.BlockSpec((TOP_K, tt), lambda i: (0, i)),
                   pl.BlockSpec((TOP_K, tt), lambda i: (0, i)),
                   pl.BlockSpec((N_EXPERTS, 1), lambda i: (0, 0))],
        out_shape=[jax.ShapeDtypeStruct((TOP_K, n), I32),
                   jax.ShapeDtypeStruct((TOP_K, n), F32),
                   jax.ShapeDtypeStruct((N_EXPERTS, 1), F32)],
        compiler_params=_params("arbitrary"),
        name="router",
    )(h2, wrt, bias_col)


def _rank_kernel(e_ref, pstart_ref, tri_ref, dest_ref, carry_ref):
    tt = e_ref.shape[1]

    @pl.when(pl.program_id(0) == 0)
    def _():
        carry_ref[...] = pstart_ref[...]

    e = e_ref[...]
    eio = lax.broadcasted_iota(I32, (N_EXPERTS, tt), 0)
    mask = jnp.zeros((N_EXPERTS, tt), F32)
    for k in range(TOP_K):
        mask = jnp.where(eio == e[k:k + 1, :], 1.0, mask)
    incl = jnp.dot(mask.astype(BF16), tri_ref[...], preferred_element_type=F32)
    pos = incl - mask + carry_ref[...]
    dest = [jnp.sum(jnp.where(eio == e[k:k + 1, :], pos, 0.0), axis=0, keepdims=True)
            for k in range(TOP_K)]
    dest_ref[...] = jnp.concatenate(dest, axis=0).astype(I32)
    carry_ref[...] += incl[:, tt - 1:tt]


def _rank(top_e_t, pstart_col, tri):
    n = top_e_t.shape[1]
    tt = TT_ROUTE
    return pl.pallas_call(
        _rank_kernel,
        grid=(n // tt,),
        in_specs=[pl.BlockSpec((TOP_K, tt), lambda i: (0, i)),
                  pl.BlockSpec((N_EXPERTS, 1), lambda i: (0, 0)),
                  pl.BlockSpec((tt, tt), lambda i: (0, 0))],
        out_specs=pl.BlockSpec((TOP_K, tt), lambda i: (0, i)),
        out_shape=jax.ShapeDtypeStruct((TOP_K, n), I32),
        scratch_shapes=[pltpu.VMEM((N_EXPERTS, 1), F32)],
        compiler_params=_params("arbitrary"),
        name="rank",
    )(top_e_t, pstart_col, tri)


def _row_copy(src_ref, src_row, dst_ref, dst_row, sem):
    return pltpu.make_async_copy(src_ref.at[pl.ds(src_row, 1)], dst_ref.at[pl.ds(dst_row, 1)], sem)


SC_CORES, SC_SUBCORES, SC_LANES = 2, 16, 16
SC_WORKERS = SC_CORES * SC_SUBCORES
INVERT_CHUNK = 8192


def _invert(dest_flat, n_tokens, rows):
    per = rows // SC_WORKERS
    n_assign = dest_flat.shape[0]
    assert rows % (SC_WORKERS * SC_LANES) == 0 and n_assign % INVERT_CHUNK == 0
    assert n_tokens & (n_tokens - 1) == 0

    @functools.partial(
        pl.kernel, mesh=plsc.VectorSubcoreMesh(core_axis_name="c", subcore_axis_name="s"),
        out_type=jax.ShapeDtypeStruct((rows,), I32),
        scratch_types=[pltpu.VMEM((INVERT_CHUNK,), I32), pltpu.VMEM((per,), I32)],
        compiler_params=pltpu.CompilerParams(needs_layout_passes=False))
    def invert(dest_hbm, out_hbm, staged, local):
        base = (lax.axis_index("s") * SC_CORES + lax.axis_index("c")) * per
        lane = lax.iota(I32, SC_LANES)

        @pl.loop(0, per, step=SC_LANES)
        def _(i):
            local[pl.ds(i, SC_LANES)] = (base + i + lane) & (n_tokens - 1)

        @pl.loop(0, n_assign // INVERT_CHUNK)
        def _(c):
            pltpu.sync_copy(dest_hbm.at[pl.ds(c * INVERT_CHUNK, INVERT_CHUNK)], staged)

            @pl.loop(0, INVERT_CHUNK, step=SC_LANES)
            def _(i):
                rel = staged[pl.ds(i, SC_LANES)] - base
                mine = (rel >= 0) & (rel < per)
                tok = (c * INVERT_CHUNK + i + lane) & (n_tokens - 1)
                plsc.store_scatter(local, [jnp.where(mine, rel, 0)], tok, mask=mine)

        pltpu.sync_copy(local, out_hbm.at[pl.ds(base, per)])

    return invert(dest_flat)


def _experts_kernel(rank_ref, eor_ref, nv_ref, nr_ref, tok_hbm, h_ref, wg_hbm, wu_hbm, wd_hbm,
                    ys_ref, *scratch):
    xbufs = scratch[:GATHER_DEPTH]
    wgf, wuf, wdf, wgb, wub, wdb, tok_s, sem, tok_sem, w_sem = scratch[GATHER_DEPTH:]
    j = pl.program_id(0)
    nv, n_ranks = nv_ref[0], nr_ref[0]
    per = wgb.shape[0] // LANES
    depth, bm = len(xbufs), xbufs[0].shape[0] // per
    ahead = depth - 1
    chunk = tok_s.shape[0] // 2
    cb = chunk // bm
    n_chunks = tok_hbm.shape[0] // chunk
    n_blocks = n_chunks * cb

    def tok_copy(c):
        return pltpu.make_async_copy(
            tok_hbm.at[pl.ds(pl.multiple_of(c * chunk, chunk), chunk)],
            tok_s.at[pl.ds(pl.multiple_of((c % 2) * chunk, chunk), chunk)], tok_sem.at[c % 2])

    def gather(block, ring, unrolled):
        block = jnp.minimum(block, n_blocks - 1)
        base = ((block // cb) % 2) * chunk + (block % cb) * bm
        buf, buf_sem = xbufs[ring], sem.at[ring]

        def issue(i, carry=None):
            _tile_copy(h_ref, tok_s[base + i], buf, i, per, buf_sem).start()
            return carry

        if unrolled:
            for i in range(bm):
                _tile_copy(h_ref, tok_s[base + i], buf, i, per, buf_sem).start(
                    priority=int(i % ROW_QUEUE_SPLIT != 0))
        else:
            lax.fori_loop(0, bm, issue, 0, unroll=8)

    def wait_rows(ring):
        pltpu.make_async_copy(h_ref.at[pl.ds(0, bm * per)], xbufs[ring], sem.at[ring]).wait()

    def weight_copies(r):
        e, s = eor_ref[r], r % WEIGHT_DEPTH
        return [pltpu.make_async_copy(src.at[e], dst.at[s], w_sem.at[s])
                for src, dst in ((wg_hbm, wgf), (wu_hbm, wuf), (wd_hbm, wdf))]

    @pl.when(j == 0)
    def _():
        tok_copy(0).start()
        tok_copy(0).wait()
        tok_copy(1).start()
        for r in range(WEIGHT_DEPTH - 1):
            @pl.when(r < n_ranks)
            def _():
                for cp in weight_copies(r):
                    cp.start()
        for b in range(ahead):
            gather(b, b, unrolled=False)

    first = j + ahead
    c_need = first // cb

    @pl.when(jnp.logical_and(jnp.logical_and(j > 0, first % cb == 0), c_need < n_chunks))
    def _():
        tok_copy(c_need).wait()

        @pl.when(c_need + 1 < n_chunks)
        def _():
            tok_copy(c_need + 1).start()

    jb = jnp.minimum(j, n_blocks - 1)
    rank = rank_ref[jb]
    fresh = jnp.logical_or(j == 0, rank != rank_ref[jnp.maximum(jb - 1, 0)])

    @pl.when(jnp.logical_and(j < nv, fresh))
    def _():
        for cp in weight_copies(rank):
            cp.wait()
        s = rank % WEIGHT_DEPTH
        wgb[...] = wgf[s].astype(BF16)
        wub[...] = wuf[s].astype(BF16)
        wdb[...] = wdf[s].astype(BF16)
        nxt = rank + WEIGHT_DEPTH - 1

        @pl.when(nxt < n_ranks)
        def _():
            for cp in weight_copies(nxt):
                cp.start()

    for ring in range(depth):
        mine = j % depth == ring

        @pl.when(jnp.logical_and(j < nv, mine))
        def _():
            wait_rows(ring)
            gather(j + ahead, (ring + ahead) % depth, unrolled=True)
            xb = _load_row_tiles(xbufs[ring], bm, per, BF16)
            act = (_silu(jnp.dot(xb, wgb[...], preferred_element_type=F32))
                   * jnp.dot(xb, wub[...], preferred_element_type=F32))
            _store_row_tiles(ys_ref, jnp.dot(act.astype(BF16), wdb[...], preferred_element_type=F32))

        @pl.when(jnp.logical_and(jnp.logical_and(j >= nv, j < nv + ahead), mine))
        def _():
            wait_rows(ring)


def _experts(block_rank, expert_of_rank, n_valid, n_ranks, row_tok, h2_tiles, wg, wu, wd):
    rows = row_tok.shape[0]
    d = wg.shape[1]
    per = d // LANES
    bm = EXPERT_ROWS
    n_blocks = rows // bm
    assert n_blocks % TOK_CHUNK_BLOCKS == 0 and n_blocks // TOK_CHUNK_BLOCKS >= 2
    assert GATHER_DEPTH - 1 < TOK_CHUNK_BLOCKS
    f = wg.shape[2]
    hbm = pl.BlockSpec(memory_space=pl.ANY)
    return pl.pallas_call(
        _experts_kernel,
        grid_spec=pltpu.PrefetchScalarGridSpec(
            num_scalar_prefetch=4,
            grid=(n_blocks + GATHER_DEPTH,),
            in_specs=[hbm, hbm, hbm, hbm, hbm],
            out_specs=pl.BlockSpec((bm * per, LANES),
                                   lambda j, rk, eor, nv, nr: (jnp.minimum(j, nv[0] - 1), 0)),
            scratch_shapes=[pltpu.VMEM((bm * per, LANES), F32)] * GATHER_DEPTH + [
                            pltpu.VMEM((WEIGHT_DEPTH, d, f), F32), pltpu.VMEM((WEIGHT_DEPTH, d, f), F32),
                            pltpu.VMEM((WEIGHT_DEPTH, f, d), F32),
                            pltpu.VMEM((d, f), BF16), pltpu.VMEM((d, f), BF16),
                            pltpu.VMEM((f, d), BF16),
                            pltpu.SMEM((2 * TOK_CHUNK_BLOCKS * bm,), I32),
                            pltpu.SemaphoreType.DMA((GATHER_DEPTH,)), pltpu.SemaphoreType.DMA((2,)),
                            pltpu.SemaphoreType.DMA((WEIGHT_DEPTH,))]),
        out_shape=jax.ShapeDtypeStruct((rows * per, LANES), F32),
        compiler_params=_params("arbitrary"),
        name="experts",
    )(block_rank, expert_of_rank, n_valid, n_ranks, row_tok, h2_tiles, wg, wu, wd)


def _combine_kernel(dest_ref, gates_ref, base_ref, mod_ref, ys_ref, out_ref, buf, sem):
    tt, d = base_ref.shape
    per = d // LANES
    for t in range(tt):
        for k in range(TOP_K):
            _tile_copy(ys_ref, dest_ref[k, t], buf.at[k], t, per, sem).start(priority=k % 2)
    for k in range(TOP_K):
        pltpu.make_async_copy(ys_ref.at[pl.ds(0, tt * per)], buf.at[k], sem).wait()
    gates = gates_ref[...]
    routed = gates[:, 0:1] * _load_row_tiles(buf.at[0], tt, per, F32)
    for k in range(1, TOP_K):
        routed = routed + gates[:, k:k + 1] * _load_row_tiles(buf.at[k], tt, per, F32)
    out_ref[...] = base_ref[...] + mod_ref[5:6, :] * routed


def _combine(dest_t, gates, base, mod, ys, seq):
    n, d = base.shape
    tt = TT_COMBINE
    tiles_per_seq = seq // tt
    return pl.pallas_call(
        _combine_kernel,
        grid=(n // tt,),
        in_specs=[pl.BlockSpec((TOP_K, tt), lambda i: (0, i), memory_space=pltpu.SMEM),
                  pl.BlockSpec((tt, TOP_K), lambda i: (i, 0)),
                  pl.BlockSpec((tt, d), lambda i: (i, 0)),
                  pl.BlockSpec((None, 6, d), lambda i: (i // tiles_per_seq, 0, 0)),
                  pl.BlockSpec(memory_space=pl.ANY)],
        out_specs=pl.BlockSpec((tt, d), lambda i: (i, 0)),
        out_shape=jax.ShapeDtypeStruct((n, d), F32),
        scratch_shapes=[pltpu.VMEM((TOP_K, tt * d // LANES, LANES), F32), pltpu.SemaphoreType.DMA],
        compiler_params=_params("arbitrary"),
        name="combine",
    )(dest_t, gates, base, mod, ys)


def _layer(x, mod, pos, rope, p):
    nbatch, seq, d = x.shape
    n = nbatch * seq
    xf = x.reshape(n, d)
    invf, bd = rope

    perm = np.concatenate([np.arange(h * HEAD_DIM, (h + 1) * HEAD_DIM) for h in PAIR_ORDER_A])
    w_in = p["w_in"]
    w_in_p = jnp.concatenate([w_in[:, :MIX_A][:, perm], w_in[:, MIX_A:]], axis=1).astype(BF16)
    ones = lambda w: jnp.ones((w,), F32)
    qscale = HEAD_DIM ** -0.5
    gcol = jnp.concatenate([jnp.tile(p["g_q_a"], N_HEADS_A) * qscale, jnp.tile(p["g_k_a"], N_KV_A),
                            ones(KV_A), jnp.tile(p["g_q_b"], N_HEADS_B) * qscale,
                            jnp.tile(p["g_k_b"], N_HEADS_B), ones(MIX_B)]).reshape(1, IN_WIDTH)
    proj = _inproj(xf, mod, pos, p["g_norm_mix"].reshape(1, d), w_in_p, gcol, invf, bd, seq)
    qa, ka, va = proj[:3]
    qkv_b = {1: proj[3:6]}
    for j, dil in enumerate(DILS):
        qkv_b[dil] = [proj[6 + t * len(DILS) + j] for t in range(3)]

    sinks_p = p["sinks_a"][np.array(PAIR_ORDER_A)]
    oa, _ = _attention(qa, ka, va, nbatch=nbatch, seq=seq, dil=1, max_dist=WINDOW_A - 1,
                       kv_shared=True, sinks=sinks_p, want_lse=False)
    obs, lses = [], []
    for window, dil in DILATED_BRANCHES:
        o, lse = _attention(*qkv_b[dil], nbatch=nbatch, seq=seq, dil=dil, max_dist=window // dil,
                            kv_shared=False, want_lse=True)
        obs.append(o)
        lses.append(lse)

    goa = p["g_out_a"][perm].reshape(1, MIX_A)
    w_out = p["w_out"]
    wo_p = jnp.concatenate([w_out[:MIX_A][perm], w_out[MIX_A:]], axis=0).astype(BF16)
    h2, h2_tiles, base = _outproj(xf, mod, oa, obs, lses, goa, p["g_out_b"].reshape(1, MIX_B), wo_p,
                        p["g_norm_ffn"].reshape(1, d), p["w_gate_s"].astype(BF16),
                        p["w_up_s"].astype(BF16), p["w_down_s"].astype(BF16), seq)

    top_e_t, gates_t, counts = _router(h2, p["w_router"].T.astype(BF16),
                                       p["router_bias"].reshape(N_EXPERTS, 1))
    bm = EXPERT_ROWS
    counts = counts.reshape(N_EXPERTS).astype(I32)
    padded = (counts + bm - 1) // bm * bm
    pends = jnp.cumsum(padded)
    pstarts = pends - padded
    rows = n * TOP_K + N_EXPERTS * bm
    n_blocks = rows // bm
    n_valid = (pends[-1] // bm).astype(I32)
    blk = jnp.minimum(jnp.arange(n_blocks, dtype=I32), n_valid - 1)
    block_e = jnp.sum((pends[None, :] <= (blk * bm)[:, None]).astype(I32), axis=1)
    block_e = jnp.minimum(block_e, N_EXPERTS - 1)
    nonempty = counts > 0
    rank_of_e = jnp.cumsum(nonempty.astype(I32)) - 1
    e_ids = jnp.arange(N_EXPERTS, dtype=I32)
    block_rank = jnp.sum(jnp.where(block_e[:, None] == e_ids[None, :], rank_of_e[None, :], 0), axis=1)
    hit = (rank_of_e[None, :] == jnp.arange(n_blocks, dtype=I32)[:, None]) & nonempty[None, :]
    expert_of_rank = jnp.sum(jnp.where(hit, e_ids[None, :], 0), axis=1)
    n_ranks = jnp.sum(nonempty.astype(I32)).reshape(1)

    tri =(np.arange(TT_ROUTE)[:, None] <= np.arange(TT_ROUTE)[None, :])
    dest_t = _rank(top_e_t, pstarts.astype(F32).reshape(N_EXPERTS, 1), jnp.asarray(tri, BF16))
    n_valid = n_valid.reshape(1)
    row_tok = _invert(dest_t.reshape(n * TOP_K), n, rows)
    ys = _experts(block_rank.astype(I32), expert_of_rank.astype(I32), n_valid, n_ranks, row_tok, h2_tiles,
                  p["w_gate_e"], p["w_up_e"], p["w_down_e"])
    out = _combine(dest_t, gates_t.T, base, mod, ys, seq)
    return out.reshape(nbatch, seq, d)


def kernel(x, c, positions, w_ada, b_ada, g_norm_mix, w_in, g_q_a, g_k_a, sinks_a, g_q_b, g_k_b,
           g_out_a, g_out_b, w_out, g_norm_ffn, w_router, router_bias, w_gate_e, w_up_e, w_down_e,
           w_gate_s, w_up_s, w_down_s):
    nbatch, seq, d = x.shape
    depth = w_ada.shape[0]
    params = dict(g_norm_mix=g_norm_mix, w_in=w_in, g_q_a=g_q_a, g_k_a=g_k_a, sinks_a=sinks_a,
                  g_q_b=g_q_b, g_k_b=g_k_b, g_out_a=g_out_a, g_out_b=g_out_b, w_out=w_out,
                  g_norm_ffn=g_norm_ffn, w_router=w_router, router_bias=router_bias,
                  w_gate_e=w_gate_e, w_up_e=w_up_e, w_down_e=w_down_e, w_gate_s=w_gate_s,
                  w_up_s=w_up_s, w_down_s=w_down_s)
    j = np.arange(LANES) % HEAD_DIM
    inv = ROPE_THETA ** (-jnp.arange(0, ROT_DIM, 2, dtype=F32) / ROT_DIM)
    invf = jnp.where(j < ROT_DIM, inv[j % (ROT_DIM // 2)], 0.0).astype(F32).reshape(1, LANES)
    bd = jnp.asarray((np.arange(LANES)[:, None] // HEAD_DIM) == (np.arange(LANES)[None, :] // HEAD_DIM),
                     BF16)
    pos = positions.reshape(nbatch * seq, 1).astype(I32)
    for l in range(depth):
        mod = _adaln(c.astype(F32), w_ada[l], b_ada[l]).reshape(nbatch, 6, d)
        x = _layer(x, mod, pos, (invf, bd), {k: v[l] for k, v in params.items()})
    return x
```

```python
import functools

import numpy as np
import jax
import jax.numpy as jnp
from jax import lax
from jax.experimental import pallas as pl
from jax.experimental.pallas import tpu as pltpu
from jax.experimental.pallas import tpu_sc as plsc

F32 = jnp.float32
BF16 = jnp.bfloat16
I32 = jnp.int32

HEAD_DIM = 64
N_HEADS_A = 8
N_KV_A = 2
WINDOW_A = 128
N_HEADS_B = 8
DILATED_BRANCHES = ((128, 1), (512, 4), (2048, 16))
DILS = tuple(dil for _, dil in DILATED_BRANCHES if dil > 1)
BLOCK = 128
ROT_DIM = HEAD_DIM // 4
ROPE_THETA = 500000.0
MIX_A = N_HEADS_A * HEAD_DIM
KV_A = N_KV_A * HEAD_DIM
MIX_B = N_HEADS_B * HEAD_DIM
N_EXPERTS = 256
TOP_K = 8
N_GROUPS = 8
TOPK_GROUPS = 4
GROUP_SIZE = N_EXPERTS // N_GROUPS
ROUTED_SCALE = 2.5
EPS = 1e-6

LANES = 128
HEADS_PER_VREG = LANES // HEAD_DIM
N_PAIRS = MIX_A // LANES
NEG = -1e30
VMEM_LIMIT = 48 * 1024 * 1024

TT_PROJ = 512
ATTN_BLOCKS_PER_STEP = 8
TT_ROUTE = 256
TT_DISPATCH = 256
TT_COMBINE = 256
EXPERT_ROWS = 256
TOK_CHUNK_BLOCKS = 32
GATHER_DEPTH = 4
WEIGHT_DEPTH = 3
SUBLANES = 8
ROW_QUEUE_SPLIT = 2

PAIR_ORDER_A = tuple(h for p in range(N_PAIRS) for h in (p, p + N_HEADS_A // N_KV_A))


def _params(*sem):
    return pltpu.CompilerParams(dimension_semantics=sem, vmem_limit_bytes=VMEM_LIMIT)


def _silu(t):
    return t / (1.0 + jnp.exp(-t))


def _rms_rows(t):
    return t * lax.rsqrt(jnp.mean(t * t, axis=-1, keepdims=True) + EPS)


def _store_row_tiles(ref, value):
    rows, d = value.shape
    per = d // LANES
    for s in range(per):
        ref[pl.ds(s, rows, stride=per), :] = value[:, s * LANES:(s + 1) * LANES]


def _load_row_tiles(ref, rows, per, dtype):
    return jnp.concatenate([ref[pl.ds(s, rows, stride=per), :].astype(dtype) for s in range(per)], axis=1)


def _tile_copy(src_ref, src_row, dst_ref, dst_row, per, sem):
    start = lambda row: row * per if isinstance(row, int) else pl.multiple_of(row * per, per)
    return pltpu.make_async_copy(src_ref.at[pl.ds(start(src_row), per)],
                                 dst_ref.at[pl.ds(start(dst_row), per)], sem)


def _ada_kernel(c_ref, w_ref, b_ref, o_ref):
    cond = _silu(c_ref[...])
    o_ref[...] = jnp.dot(cond.astype(BF16), w_ref[...].astype(BF16),
                         preferred_element_type=F32) + b_ref[...]


def _adaln(c, w_ada, b_ada):
    nb, d = c.shape
    width = w_ada.shape[1]
    tn = 1024
    return pl.pallas_call(
        _ada_kernel,
        grid=(width // tn,),
        in_specs=[pl.BlockSpec((nb, d), lambda j: (0, 0)),
                  pl.BlockSpec((d, tn), lambda j: (0, j)),
                  pl.BlockSpec((1, tn), lambda j: (0, j))],
        out_specs=pl.BlockSpec((nb, tn), lambda j: (0, j)),
        out_shape=jax.ShapeDtypeStruct((nb, width), F32),
        compiler_params=_params("arbitrary"),
        name="adaln",
    )(c, w_ada, b_ada.reshape(1, width))


COL_QA, COL_KA, COL_VA = 0, MIX_A, MIX_A + KV_A
COL_QB = MIX_A + 2 * KV_A
COL_KB, COL_VB = COL_QB + MIX_B, COL_QB + 2 * MIX_B
IN_WIDTH = COL_VB + MIX_B


def _inproj_kernel(x_ref, mod_ref, pos_ref, gn_ref, w_ref, gcol_ref, invf_ref, bd_ref,
                   qa_ref, ka_ref, va_ref, qb_ref, kb_ref, vb_ref, *rest):
    n_dil = len(DILS)
    dil_refs = [rest[i * n_dil:(i + 1) * n_dil] for i in range(3)]
    qb_scr, kb_scr, vb_scr = rest[3 * n_dil:]
    tt = x_ref.shape[0]
    shift, scale = mod_ref[0:1, :], mod_ref[1:2, :]
    h = _rms_rows(x_ref[...]) * gn_ref[...] * (1.0 + scale) + shift
    proj = jnp.dot(h.astype(BF16), w_ref[...], preferred_element_type=F32)

    ang = pos_ref[...].astype(F32) * invf_ref[...]
    cs, sn = jnp.cos(ang), jnp.sin(ang)
    lane = lax.broadcasted_iota(I32, (1, LANES), 1) % HEAD_DIM
    s_lo = jnp.where(lane < ROT_DIM // 2, -sn, 0.0)
    s_hi = jnp.where((lane >= ROT_DIM // 2) & (lane < ROT_DIM), sn, 0.0)
    bd = bd_ref[...]

    def norm_rope(col0, width, out_ref, scr=None):
        for j in range(width // LANES):
            c = col0 + j * LANES
            t = proj[:, c:c + LANES]
            sq = t * t
            hi = sq.astype(BF16)
            lo = (sq - hi.astype(F32)).astype(BF16)
            ss = (jnp.dot(hi, bd, preferred_element_type=F32)
                  + jnp.dot(lo, bd, preferred_element_type=F32))
            t = t * lax.rsqrt(ss * (1.0 / HEAD_DIM) + EPS) * gcol_ref[:, c:c + LANES]
            t = (t * cs + pltpu.roll(t, LANES - ROT_DIM // 2, 1) * s_lo
                 + pltpu.roll(t, ROT_DIM // 2, 1) * s_hi)
            out_ref[:, j * LANES:(j + 1) * LANES] = t.astype(BF16)
            if scr is not None:
                scr[j] = t

    norm_rope(COL_QA, MIX_A, qa_ref)
    norm_rope(COL_KA, KV_A, ka_ref)
    norm_rope(COL_QB, MIX_B, qb_ref, qb_scr)
    norm_rope(COL_KB, MIX_B, kb_ref, kb_scr)
    va_ref[...] = proj[:, COL_VA:COL_VA + KV_A].astype(BF16)
    vb_ref[...] = proj[:, COL_VB:COL_VB + MIX_B].astype(BF16)
    for j in range(N_PAIRS):
        vb_scr[j] = proj[:, COL_VB + j * LANES:COL_VB + (j + 1) * LANES]
    for scr, outs in zip((qb_scr, kb_scr, vb_scr), dil_refs):
        for dil, out in zip(DILS, outs):
            for r in range(dil):
                for j in range(N_PAIRS):
                    c = r * MIX_B + j * LANES
                    out[:, c:c + LANES] = scr[j, pl.ds(r, tt // dil, stride=dil), :].astype(BF16)


def _inproj(xf, mod, pos, g_norm, w_in_p, gcol, invf, bd, seq):
    n, d = xf.shape
    tt = TT_PROJ
    tiles_per_seq = seq // tt
    shapes = [(n, w) for w in (MIX_A, KV_A, KV_A, MIX_B, MIX_B, MIX_B)]
    shapes += [(n // dil, dil * MIX_B) for _ in range(3) for dil in DILS]
    full = lambda shape: pl.BlockSpec(shape, lambda i: (0,) * len(shape))
    return pl.pallas_call(
        _inproj_kernel,
        grid=(n // tt,),
        in_specs=[pl.BlockSpec((tt, d), lambda i: (i, 0)),
                  pl.BlockSpec((None, 6, d), lambda i: (i // tiles_per_seq, 0, 0)),
                  pl.BlockSpec((tt, 1), lambda i: (i, 0)),
                  full((1, d)), full((d, IN_WIDTH)), full((1, IN_WIDTH)),
                  full((1, LANES)), full((LANES, LANES))],
        out_specs=[pl.BlockSpec((tt * r // n, w), lambda i: (i, 0)) for r, w in shapes],
        out_shape=[jax.ShapeDtypeStruct(s, BF16) for s in shapes],
        scratch_shapes=[pltpu.VMEM((N_PAIRS, tt, LANES), F32)] * 3,
        compiler_params=_params("parallel"),
        name="inproj",
    )(xf, mod, pos, g_norm, w_in_p, gcol, invf, bd)


def _attn_kernel(*refs, kv_shared, max_dist, n_qblk, n_res, use_prev, has_sinks, want_lse):
    refs = list(refs)
    sink_ref = refs.pop(0) if has_sinks else None
    q_ref = refs.pop(0)
    kp_ref = refs.pop(0) if use_prev else None
    kc_ref = refs.pop(0)
    vp_ref = refs.pop(0) if use_prev else None
    vc_ref = refs.pop(0)
    o_ref = refs.pop(0)
    lse_ref = refs.pop(0) if want_lse else None

    first_step = pl.program_id(2) == 0
    nq = 2 * BLOCK
    kw = LANES if kv_shared else MIX_B

    def band(nk):
        qpos = lax.broadcasted_iota(I32, (nq, nk), 0) % BLOCK
        kpos = lax.broadcasted_iota(I32, (nq, nk), 1)
        dist = qpos + (nk - BLOCK) - kpos
        return (dist >= 0) & (dist <= max_dist), kpos

    band2, kpos2 = band(2 * BLOCK)
    band1, _ = band(BLOCK)
    lane = lax.broadcasted_iota(I32, (nq, LANES), 1)
    row = lax.broadcasted_iota(I32, (nq, LANES), 0)
    own_half = (lane < HEAD_DIM) == (row < BLOCK)
    left_lanes = lax.broadcasted_iota(I32, (BLOCK, LANES), 1) < HEAD_DIM
    lane8 = lax.broadcasted_iota(I32, (BLOCK, 2 * N_PAIRS), 1)

    for res in range(n_res):
        for qb in range(n_qblk):
            rows = slice(qb * BLOCK, (qb + 1) * BLOCK)
            lse_blk = jnp.zeros((BLOCK, 2 * N_PAIRS), F32)
            for p in range(N_PAIRS):
                cq = slice(res * MIX_B + p * LANES, res * MIX_B + (p + 1) * LANES)
                ck = slice(res * kw, res * kw + LANES) if kv_shared else slice(
                    res * kw + p * LANES, res * kw + (p + 1) * LANES)
                qp = q_ref[rows, cq]
                qs = jnp.concatenate([qp, qp], axis=0)
                qs = jnp.where(own_half, qs, jnp.zeros_like(qs))
                if qb > 0:
                    keys = slice((qb - 1) * BLOCK, (qb + 1) * BLOCK)
                    k, v, valid = kc_ref[keys, ck], vc_ref[keys, ck], band2
                elif use_prev:
                    k = jnp.concatenate([kp_ref[:, ck], kc_ref[rows, ck]], axis=0)
                    v = jnp.concatenate([vp_ref[:, ck], vc_ref[rows, ck]], axis=0)
                    valid = band2 & ((kpos2 >= BLOCK) | jnp.logical_not(first_step))
                else:
                    k, v, valid = kc_ref[rows, ck], vc_ref[rows, ck], band1
                s = lax.dot_general(qs, k, (((1,), (1,)), ((), ())), preferred_element_type=F32)
                s = jnp.where(valid, s, NEG)
                m = jnp.max(s, axis=-1, keepdims=True)
                if has_sinks:
                    rows1 = lax.broadcasted_iota(I32, (nq, 1), 0)
                    sink = jnp.where(rows1 < BLOCK, sink_ref[2 * p], sink_ref[2 * p + 1])
                    m = jnp.maximum(m, sink)
                e = jnp.exp(s - m)
                l = jnp.sum(e, axis=-1, keepdims=True)
                if has_sinks:
                    l = l + jnp.exp(sink - m)
                o = jnp.dot(e.astype(BF16), v, preferred_element_type=F32) / l
                o_ref[rows, cq] = jnp.where(left_lanes, o[:BLOCK], o[BLOCK:]).astype(BF16)
                if want_lse:
                    lse = m + jnp.log(l)
                    lse_blk = (lse_blk + jnp.where(lane8 == 2 * p, lse[:BLOCK], 0.0)
                               + jnp.where(lane8 == 2 * p + 1, lse[BLOCK:], 0.0))
            if want_lse:
                lse_ref[res, rows, :] = lse_blk


def _attention(q, k, v, *, nbatch, seq, dil, max_dist, kv_shared, sinks=None, want_lse):
    length = seq // dil
    nblk = length // BLOCK
    n_qblk = min(nblk, ATTN_BLOCKS_PER_STEP)
    n_res = min(dil, ATTN_BLOCKS_PER_STEP // n_qblk)
    steps = nblk // n_qblk
    use_prev = steps > 1
    kw = k.shape[1] // dil
    view = lambda t: t.reshape(nbatch, length, t.shape[1])
    cur = lambda b, r, i: (b, i, r)
    prev = lambda b, r, i: (b, jnp.maximum(i * n_qblk - 1, 0), r)
    in_specs, args = [], []
    if sinks is not None:
        in_specs.append(pl.BlockSpec(memory_space=pltpu.SMEM))
        args.append(sinks)
    in_specs.append(pl.BlockSpec((None, n_qblk * BLOCK, n_res * MIX_B), cur))
    args.append(view(q))
    for t in (k, v):
        if use_prev:
            in_specs.append(pl.BlockSpec((None, BLOCK, n_res * kw), prev))
            args.append(view(t))
        in_specs.append(pl.BlockSpec((None, n_qblk * BLOCK, n_res * kw), cur))
        args.append(view(t))
    out_specs = [pl.BlockSpec((None, n_qblk * BLOCK, n_res * MIX_B), cur)]
    out_shape = [jax.ShapeDtypeStruct((nbatch, length, dil * MIX_B), BF16)]
    if want_lse:
        out_specs.append(pl.BlockSpec((None, n_res, n_qblk * BLOCK, N_HEADS_B),
                                      lambda b, r, i: (b, r, i, 0)))
        out_shape.append(jax.ShapeDtypeStruct((nbatch, dil, length, N_HEADS_B), F32))
    outs = pl.pallas_call(
        functools.partial(_attn_kernel, kv_shared=kv_shared, max_dist=max_dist, n_qblk=n_qblk,
                          n_res=n_res, use_prev=use_prev, has_sinks=sinks is not None,
                          want_lse=want_lse),
        grid=(nbatch, dil // n_res, steps),
        in_specs=in_specs, out_specs=out_specs, out_shape=out_shape,
        compiler_params=_params("parallel", "parallel", "arbitrary"),
        name=f"attn_d{dil}" + ("_swa" if kv_shared else ""),
    )(*args)
    o = outs[0].reshape(nbatch * length, dil * MIX_B)
    if not want_lse:
        return o, None
    lse = outs[1].transpose(0, 2, 1, 3).reshape(nbatch * seq, N_HEADS_B)
    return o, lse


def _expand_heads(w, width):
    head = lax.broadcasted_iota(I32, (1, width), 1) // HEAD_DIM
    out = jnp.zeros((w.shape[0], width), F32)
    for hd in range(w.shape[1]):
        out = jnp.where(head == hd, w[:, hd:hd + 1], out)
    return out


def _outproj_kernel(x_ref, mod_ref, oa_ref, ob1_ref, ob2_ref, ob3_ref, l1_ref, l2_ref, l3_ref,
                    goa_ref, gob_ref, wo_ref, gf_ref, wgs_ref, wus_ref, wds_ref,
                    h2_ref, h2t_ref, base_ref, ob_scr):
    tt = x_ref.shape[0]
    gate_a = mod_ref[2:3, :]
    shift_m, scale_m, gate_m = mod_ref[3:4, :], mod_ref[4:5, :], mod_ref[5:6, :]

    def token_major(ref, dil):
        if dil == 1:
            return ref[...].astype(F32)
        for r in range(dil):
            for j in range(N_PAIRS):
                c = r * MIX_B + j * LANES
                ob_scr[j, pl.ds(r, tt // dil, stride=dil), :] = ref[:, c:c + LANES].astype(F32)
        return jnp.concatenate([ob_scr[j] for j in range(N_PAIRS)], axis=1)

    l1, l2, l3 = l1_ref[...], l2_ref[...], l3_ref[...]
    mx = jnp.maximum(jnp.maximum(l1, l2), l3)
    e1, e2, e3 = jnp.exp(l1 - mx), jnp.exp(l2 - mx), jnp.exp(l3 - mx)
    den = e1 + e2 + e3
    dils = [dil for _, dil in DILATED_BRANCHES]
    ob = _expand_heads(e1 / den, MIX_B) * token_major(ob1_ref, dils[0])
    ob = ob + _expand_heads(e2 / den, MIX_B) * token_major(ob2_ref, dils[1])
    ob = ob + _expand_heads(e3 / den, MIX_B) * token_major(ob3_ref, dils[2])
    ob = _rms_rows(ob) * gob_ref[...]
    oa = _rms_rows(oa_ref[...].astype(F32)) * goa_ref[...]
    y = (jnp.dot(oa.astype(BF16), wo_ref[0:MIX_A, :], preferred_element_type=F32)
         + jnp.dot(ob.astype(BF16), wo_ref[MIX_A:MIX_A + MIX_B, :], preferred_element_type=F32))
    x1 = x_ref[...] + gate_a * y
    h2 = _rms_rows(x1) * gf_ref[...] * (1.0 + scale_m) + shift_m
    h2_ref[...] = h2
    _store_row_tiles(h2t_ref, h2)
    hb = h2.astype(BF16)
    act = (_silu(jnp.dot(hb, wgs_ref[...], preferred_element_type=F32))
           * jnp.dot(hb, wus_ref[...], preferred_element_type=F32))
    shared = jnp.dot(act.astype(BF16), wds_ref[...], preferred_element_type=F32)
    base_ref[...] = x1 + gate_m * shared


def _outproj(xf, mod, oa, obs, lses, goa, gob, wo_p, gf, wgs, wus, wds, seq):
    n, d = xf.shape
    tt = TT_PROJ // 2
    tiles_per_seq = seq // tt
    tile = lambda w: pl.BlockSpec((tt, w), lambda i: (i, 0))
    full = lambda shape: pl.BlockSpec(shape, lambda i: (0,) * len(shape))
    sd = wgs.shape[1]
    dilated = [pl.BlockSpec((tt // dil, dil * MIX_B), lambda i: (i, 0)) for _, dil in DILATED_BRANCHES]
    return pl.pallas_call(
        _outproj_kernel,
        grid=(n // tt,),
        in_specs=[tile(d), pl.BlockSpec((None, 6, d), lambda i: (i // tiles_per_seq, 0, 0)),
                  tile(MIX_A), *dilated,
                  tile(N_HEADS_B), tile(N_HEADS_B), tile(N_HEADS_B),
                  full((1, MIX_A)), full((1, MIX_B)), full((MIX_A + MIX_B, d)), full((1, d)),
                  full((d, sd)), full((d, sd)), full((sd, d))],
        out_specs=[tile(d), pl.BlockSpec((tt * d // LANES, LANES), lambda i: (i, 0)), tile(d)],
        out_shape=[jax.ShapeDtypeStruct((n, d), F32), jax.ShapeDtypeStruct((n * d // LANES, LANES), F32),
                   jax.ShapeDtypeStruct((n, d), F32)],
        scratch_shapes=[pltpu.VMEM((N_PAIRS, tt, LANES), F32)],
        compiler_params=_params("parallel"),
        name="outproj",
    )(xf, mod, oa, *obs, *lses, goa, gob, wo_p, gf, wgs, wus, wds)


def _router_kernel(h_ref, wrt_ref, bias_ref, e_ref, g_ref, cnt_ref):
    tt = h_ref.shape[0]
    logits = lax.dot_general(wrt_ref[...], h_ref[...].astype(BF16), (((1,), (1,)), ((), ())),
                             preferred_element_type=F32)
    scores = 1.0 / (1.0 + jnp.exp(-logits))
    biased = scores + bias_ref[...]
    ninf = -jnp.inf

    j32 = lax.broadcasted_iota(I32, (GROUP_SIZE, tt), 0).astype(F32)
    grp = []
    for g in range(N_GROUPS):
        bg = biased[g * GROUP_SIZE:(g + 1) * GROUP_SIZE, :]
        m1 = jnp.max(bg, axis=0, keepdims=True)
        i1 = jnp.min(jnp.where(bg == m1, j32, float(GROUP_SIZE)), axis=0, keepdims=True)
        m2 = jnp.max(jnp.where(j32 == i1, ninf, bg), axis=0, keepdims=True)
        grp.append(m1 + m2)
    grp = jnp.concatenate(grp, axis=0)
    g8 = lax.broadcasted_iota(I32, (N_GROUPS, tt), 0).astype(F32)
    chosen = jnp.zeros((N_GROUPS, tt), F32)
    for _ in range(TOPK_GROUPS):
        gm = jnp.max(grp, axis=0, keepdims=True)
        gi = jnp.min(jnp.where(grp == gm, g8, float(N_GROUPS)), axis=0, keepdims=True)
        hit = g8 == gi
        chosen = jnp.where(hit, 1.0, chosen)
        grp = jnp.where(hit, ninf, grp)
    masked = jnp.concatenate(
        [jnp.where(chosen[g:g + 1, :] > 0.0, biased[g * GROUP_SIZE:(g + 1) * GROUP_SIZE, :], ninf)
         for g in range(N_GROUPS)], axis=0)

    eio = lax.broadcasted_iota(I32, (N_EXPERTS, tt), 0).astype(F32)
    picked = jnp.zeros((N_EXPERTS, tt), F32)
    es, gs = [], []
    for _ in range(TOP_K):
        m = jnp.max(masked, axis=0, keepdims=True)
        idx = jnp.min(jnp.where(masked == m, eio, float(N_EXPERTS)), axis=0, keepdims=True)
        hit = eio == idx
        gs.append(jnp.sum(jnp.where(hit, scores, 0.0), axis=0, keepdims=True))
        es.append(idx)
        picked = jnp.where(hit, 1.0, picked)
        masked = jnp.where(hit, ninf, masked)
    gates = jnp.concatenate(gs, axis=0)
    e_ref[...] = jnp.concatenate(es, axis=0).astype(I32)
    g_ref[...] = gates / jnp.sum(gates, axis=0, keepdims=True) * ROUTED_SCALE

    @pl.when(pl.program_id(0) == 0)
    def _():
        cnt_ref[...] = jnp.zeros_like(cnt_ref)
    cnt_ref[...] += jnp.sum(picked, axis=1, keepdims=True)


def _router(h2, wrt, bias_col):
    n, d = h2.shape
    tt = TT_ROUTE
    return pl.pallas_call(
        _router_kernel,
        grid=(n // tt,),
        in_specs=[pl.BlockSpec((tt, d), lambda i: (i, 0)),
                  pl.BlockSpec((N_EXPERTS, d), lambda i: (0, 0)),
                  pl.BlockSpec((N_EXPERTS, 1), lambda i: (0, 0))],
        out_specs=[pl.BlockSpec((TOP_K, tt), lambda i: (0, i)),
                   pl.BlockSpec((TOP_K, tt), lambda i: (0, i)),
                   pl.BlockSpec((N_EXPERTS, 1), lambda i: (0, 0))],
        out_shape=[jax.ShapeDtypeStruct((TOP_K, n), I32),
                   jax.ShapeDtypeStruct((TOP_K, n), F32),
                   jax.ShapeDtypeStruct((N_EXPERTS, 1), F32)],
        compiler_params=_params("arbitrary"),
        name="router",
    )(h2, wrt, bias_col)


def _rank_kernel(e_ref, pstart_ref, tri_ref, dest_ref, carry_ref):
    tt = e_ref.shape[1]

    @pl.when(pl.program_id(0) == 0)
    def _():
        carry_ref[...] = pstart_ref[...]

    e = e_ref[...]
    eio = lax.broadcasted_iota(I32, (N_EXPERTS, tt), 0)
    mask = jnp.zeros((N_EXPERTS, tt), F32)
    for k in range(TOP_K):
        mask = jnp.where(eio == e[k:k + 1, :], 1.0, mask)
    incl = jnp.dot(mask.astype(BF16), tri_ref[...], preferred_element_type=F32)
    pos = incl - mask + carry_ref[...]
    dest = [jnp.sum(jnp.where(eio == e[k:k + 1, :], pos, 0.0), axis=0, keepdims=True)
            for k in range(TOP_K)]
    dest_ref[...] = jnp.concatenate(dest, axis=0).astype(I32)
    carry_ref[...] += incl[:, tt - 1:tt]


def _rank(top_e_t, pstart_col, tri):
    n = top_e_t.shape[1]
    tt = TT_ROUTE
    return pl.pallas_call(
        _rank_kernel,
        grid=(n // tt,),
        in_specs=[pl.BlockSpec((TOP_K, tt), lambda i: (0, i)),
                  pl.BlockSpec((N_EXPERTS, 1), lambda i: (0, 0)),
                  pl.BlockSpec((tt, tt), lambda i: (0, 0))],
        out_specs=pl.BlockSpec((TOP_K, tt), lambda i: (0, i)),
        out_shape=jax.ShapeDtypeStruct((TOP_K, n), I32),
        scratch_shapes=[pltpu.VMEM((N_EXPERTS, 1), F32)],
        compiler_params=_params("arbitrary"),
        name="rank",
    )(top_e_t, pstart_col, tri)


def _row_copy(src_ref, src_row, dst_ref, dst_row, sem):
    return pltpu.make_async_copy(src_ref.at[pl.ds(src_row, 1)], dst_ref.at[pl.ds(dst_row, 1)], sem)


SC_CORES, SC_SUBCORES, SC_LANES = 2, 16, 16
SC_WORKERS = SC_CORES * SC_SUBCORES
INVERT_CHUNK = 8192


def _invert(dest_flat, n_tokens, rows):
    per = rows // SC_WORKERS
    n_assign = dest_flat.shape[0]
    assert rows % (SC_WORKERS * SC_LANES) == 0 and n_assign % INVERT_CHUNK == 0
    assert n_tokens & (n_tokens - 1) == 0

    @functools.partial(
        pl.kernel, mesh=plsc.VectorSubcoreMesh(core_axis_name="c", subcore_axis_name="s"),
        out_type=jax.ShapeDtypeStruct((rows,), I32),
        scratch_types=[pltpu.VMEM((INVERT_CHUNK,), I32), pltpu.VMEM((per,), I32)],
        compiler_params=pltpu.CompilerParams(needs_layout_passes=False))
    def invert(dest_hbm, out_hbm, staged, local):
        base = (lax.axis_index("s") * SC_CORES + lax.axis_index("c")) * per
        lane = lax.iota(I32, SC_LANES)

        @pl.loop(0, per, step=SC_LANES)
        def _(i):
            local[pl.ds(i, SC_LANES)] = (base + i + lane) & (n_tokens - 1)

        @pl.loop(0, n_assign // INVERT_CHUNK)
        def _(c):
            pltpu.sync_copy(dest_hbm.at[pl.ds(c * INVERT_CHUNK, INVERT_CHUNK)], staged)

            @pl.loop(0, INVERT_CHUNK, step=SC_LANES)
            def _(i):
                rel = staged[pl.ds(i, SC_LANES)] - base
                mine = (rel >= 0) & (rel < per)
                tok = (c * INVERT_CHUNK + i + lane) & (n_tokens - 1)
                plsc.store_scatter(local, [jnp.where(mine, rel, 0)], tok, mask=mine)

        pltpu.sync_copy(local, out_hbm.at[pl.ds(base, per)])

    return invert(dest_flat)


def _experts_kernel(rank_ref, eor_ref, nv_ref, nr_ref, tok_hbm, h_ref, wg_hbm, wu_hbm, wd_hbm,
                    ys_ref, *scratch):
    xbufs = scratch[:GATHER_DEPTH]
    wgf, wuf, wdf, wgb, wub, wdb, tok_s, sem, tok_sem, w_sem = scratch[GATHER_DEPTH:]
    j = pl.program_id(0)
    nv, n_ranks = nv_ref[0], nr_ref[0]
    per = wgb.shape[0] // LANES
    depth, bm = len(xbufs), xbufs[0].shape[0] // per
    ahead = depth - 1
    chunk = tok_s.shape[0] // 2
    cb = chunk // bm
    n_chunks = tok_hbm.shape[0] // chunk
    n_blocks = n_chunks * cb

    def tok_copy(c):
        return pltpu.make_async_copy(
            tok_hbm.at[pl.ds(pl.multiple_of(c * chunk, chunk), chunk)],
            tok_s.at[pl.ds(pl.multiple_of((c % 2) * chunk, chunk), chunk)], tok_sem.at[c % 2])

    def gather(block, ring, unrolled):
        block = jnp.minimum(block, n_blocks - 1)
        base = ((block // cb) % 2) * chunk + (block % cb) * bm
        buf, buf_sem = xbufs[ring], sem.at[ring]

        def issue(i, carry=None):
            _tile_copy(h_ref, tok_s[base + i], buf, i, per, buf_sem).start()
            return carry

        if unrolled:
            for i in range(bm):
                _tile_copy(h_ref, tok_s[base + i], buf, i, per, buf_sem).start(
                    priority=int(i % ROW_QUEUE_SPLIT != 0))
        else:
            lax.fori_loop(0, bm, issue, 0, unroll=8)

    def wait_rows(ring):
        pltpu.make_async_copy(h_ref.at[pl.ds(0, bm * per)], xbufs[ring], sem.at[ring]).wait()

    def weight_copies(r):
        e, s = eor_ref[r], r % WEIGHT_DEPTH
        return [pltpu.make_async_copy(src.at[e], dst.at[s], w_sem.at[s])
                for src, dst in ((wg_hbm, wgf), (wu_hbm, wuf), (wd_hbm, wdf))]

    @pl.when(j == 0)
    def _():
        tok_copy(0).start()
        tok_copy(0).wait()
        tok_copy(1).start()
        for r in range(WEIGHT_DEPTH - 1):
            @pl.when(r < n_ranks)
            def _():
                for cp in weight_copies(r):
                    cp.start()
        for b in range(ahead):
            gather(b, b, unrolled=False)

    first = j + ahead
    c_need = first // cb

    @pl.when(jnp.logical_and(jnp.logical_and(j > 0, first % cb == 0), c_need < n_chunks))
    def _():
        tok_copy(c_need).wait()

        @pl.when(c_need + 1 < n_chunks)
        def _():
            tok_copy(c_need + 1).start()

    jb = jnp.minimum(j, n_blocks - 1)
    rank = rank_ref[jb]
    fresh = jnp.logical_or(j == 0, rank != rank_ref[jnp.maximum(jb - 1, 0)])

    @pl.when(jnp.logical_and(j < nv, fresh))
    def _():
        for cp in weight_copies(rank):
            cp.wait()
        s = rank % WEIGHT_DEPTH
        wgb[...] = wgf[s].astype(BF16)
        wub[...] = wuf[s].astype(BF16)
        wdb[...] = wdf[s].astype(BF16)
        nxt = rank + WEIGHT_DEPTH - 1

        @pl.when(nxt < n_ranks)
        def _():
            for cp in weight_copies(nxt):
                cp.start()

    for ring in range(depth):
        mine = j % depth == ring

        @pl.when(jnp.logical_and(j < nv, mine))
        def _():
            wait_rows(ring)
            gather(j + ahead, (ring + ahead) % depth, unrolled=True)
            xb = _load_row_tiles(xbufs[ring], bm, per, BF16)
            act = (_silu(jnp.dot(xb, wgb[...], preferred_element_type=F32))
                   * jnp.dot(xb, wub[...], preferred_element_type=F32))
            _store_row_tiles(ys_ref, jnp.dot(act.astype(BF16), wdb[...], preferred_element_type=F32))

        @pl.when(jnp.logical_and(jnp.logical_and(j >= nv, j < nv + ahead), mine))
        def _():
            wait_rows(ring)


def _experts(block_rank, expert_of_rank, n_valid, n_ranks, row_tok, h2_tiles, wg, wu, wd):
    rows = row_tok.shape[0]
    d = wg.shape[1]
    per = d // LANES
    bm = EXPERT_ROWS
    n_blocks = rows // bm
    assert n_blocks % TOK_CHUNK_BLOCKS == 0 and n_blocks // TOK_CHUNK_BLOCKS >= 2
    assert GATHER_DEPTH - 1 < TOK_CHUNK_BLOCKS
    f = wg.shape[2]
    hbm = pl.BlockSpec(memory_space=pl.ANY)
    return pl.pallas_call(
        _experts_kernel,
        grid_spec=pltpu.PrefetchScalarGridSpec(
            num_scalar_prefetch=4,
            grid=(n_blocks + GATHER_DEPTH,),
            in_specs=[hbm, hbm, hbm, hbm, hbm],
            out_specs=pl.BlockSpec((bm * per, LANES),
                                   lambda j, rk, eor, nv, nr: (jnp.minimum(j, nv[0] - 1), 0)),
            scratch_shapes=[pltpu.VMEM((bm * per, LANES), F32)] * GATHER_DEPTH + [
                            pltpu.VMEM((WEIGHT_DEPTH, d, f), F32), pltpu.VMEM((WEIGHT_DEPTH, d, f), F32),
                            pltpu.VMEM((WEIGHT_DEPTH, f, d), F32),
                            pltpu.VMEM((d, f), BF16), pltpu.VMEM((d, f), BF16),
                            pltpu.VMEM((f, d), BF16),
                            pltpu.SMEM((2 * TOK_CHUNK_BLOCKS * bm,), I32),
                            pltpu.SemaphoreType.DMA((GATHER_DEPTH,)), pltpu.SemaphoreType.DMA((2,)),
                            pltpu.SemaphoreType.DMA((WEIGHT_DEPTH,))]),
        out_shape=jax.ShapeDtypeStruct((rows * per, LANES), F32),
        compiler_params=_params("arbitrary"),
        name="experts",
    )(block_rank, expert_of_rank, n_valid, n_ranks, row_tok, h2_tiles, wg, wu, wd)


def _combine_kernel(dest_ref, gates_ref, base_ref, mod_ref, ys_ref, out_ref, buf, sem):
    tt, d = base_ref.shape
    per = d // LANES
    for t in range(tt):
        for k in range(TOP_K):
            _tile_copy(ys_ref, dest_ref[k, t], buf.at[k], t, per, sem).start(priority=k % 2)
    for k in range(TOP_K):
        pltpu.make_async_copy(ys_ref.at[pl.ds(0, tt * per)], buf.at[k], sem).wait()
    gates = gates_ref[...]
    routed = gates[:, 0:1] * _load_row_tiles(buf.at[0], tt, per, F32)
    for k in range(1, TOP_K):
        routed = routed + gates[:, k:k + 1] * _load_row_tiles(buf.at[k], tt, per, F32)
    out_ref[...] = base_ref[...] + mod_ref[5:6, :] * routed


def _combine(dest_t, gates, base, mod, ys, seq):
    n, d = base.shape
    tt = TT_COMBINE
    tiles_per_seq = seq // tt
    return pl.pallas_call(
        _combine_kernel,
        grid=(n // tt,),
        in_specs=[pl.BlockSpec((TOP_K, tt), lambda i: (0, i), memory_space=pltpu.SMEM),
                  pl.BlockSpec((tt, TOP_K), lambda i: (i, 0)),
                  pl.BlockSpec((tt, d), lambda i: (i, 0)),
                  pl.BlockSpec((None, 6, d), lambda i: (i // tiles_per_seq, 0, 0)),
                  pl.BlockSpec(memory_space=pl.ANY)],
        out_specs=pl.BlockSpec((tt, d), lambda i: (i, 0)),
        out_shape=jax.ShapeDtypeStruct((n, d), F32),
        scratch_shapes=[pltpu.VMEM((TOP_K, tt * d // LANES, LANES), F32), pltpu.SemaphoreType.DMA],
        compiler_params=_params("arbitrary"),
        name="combine",
    )(dest_t, gates, base, mod, ys)


def _layer(x, mod, pos, rope, p):
    nbatch, seq, d = x.shape
    n = nbatch * seq
    xf = x.reshape(n, d)
    invf, bd = rope

    perm = np.concatenate([np.arange(h * HEAD_DIM, (h + 1) * HEAD_DIM) for h in PAIR_ORDER_A])
    w_in = p["w_in"]
    w_in_p = jnp.concatenate([w_in[:, :MIX_A][:, perm], w_in[:, MIX_A:]], axis=1).astype(BF16)
    ones = lambda w: jnp.ones((w,), F32)
    qscale = HEAD_DIM ** -0.5
    gcol = jnp.concatenate([jnp.tile(p["g_q_a"], N_HEADS_A) * qscale, jnp.tile(p["g_k_a"], N_KV_A),
                            ones(KV_A), jnp.tile(p["g_q_b"], N_HEADS_B) * qscale,
                            jnp.tile(p["g_k_b"], N_HEADS_B), ones(MIX_B)]).reshape(1, IN_WIDTH)
    proj = _inproj(xf, mod, pos, p["g_norm_mix"].reshape(1, d), w_in_p, gcol, invf, bd, seq)
    qa, ka, va = proj[:3]
    qkv_b = {1: proj[3:6]}
    for j, dil in enumerate(DILS):
        qkv_b[dil] = [proj[6 + t * len(DILS) + j] for t in range(3)]

    sinks_p = p["sinks_a"][np.array(PAIR_ORDER_A)]
    oa, _ = _attention(qa, ka, va, nbatch=nbatch, seq=seq, dil=1, max_dist=WINDOW_A - 1,
                       kv_shared=True, sinks=sinks_p, want_lse=False)
    obs, lses = [], []
    for window, dil in DILATED_BRANCHES:
        o, lse = _attention(*qkv_b[dil], nbatch=nbatch, seq=seq, dil=dil, max_dist=window // dil,
                            kv_shared=False, want_lse=True)
        obs.append(o)
        lses.append(lse)

    goa = p["g_out_a"][perm].reshape(1, MIX_A)
    w_out = p["w_out"]
    wo_p = jnp.concatenate([w_out[:MIX_A][perm], w_out[MIX_A:]], axis=0).astype(BF16)
    h2, h2_tiles, base = _outproj(xf, mod, oa, obs, lses, goa, p["g_out_b"].reshape(1, MIX_B), wo_p,
                        p["g_norm_ffn"].reshape(1, d), p["w_gate_s"].astype(BF16),
                        p["w_up_s"].astype(BF16), p["w_down_s"].astype(BF16), seq)

    top_e_t, gates_t, counts = _router(h2, p["w_router"].T.astype(BF16),
                                       p["router_bias"].reshape(N_EXPERTS, 1))
    bm = EXPERT_ROWS
    counts = counts.reshape(N_EXPERTS).astype(I32)
    padded = (counts + bm - 1) // bm * bm
    pends = jnp.cumsum(padded)
    pstarts = pends - padded
    rows = n * TOP_K + N_EXPERTS * bm
    n_blocks = rows // bm
    n_valid = (pends[-1] // bm).astype(I32)
    blk = jnp.minimum(jnp.arange(n_blocks, dtype=I32), n_valid - 1)
    block_e = jnp.sum((pends[None, :] <= (blk * bm)[:, None]).astype(I32), axis=1)
    block_e = jnp.minimum(block_e, N_EXPERTS - 1)
    nonempty = counts > 0
    rank_of_e = jnp.cumsum(nonempty.astype(I32)) - 1
    e_ids = jnp.arange(N_EXPERTS, dtype=I32)
    block_rank = jnp.sum(jnp.where(block_e[:, None] == e_ids[None, :], rank_of_e[None, :], 0), axis=1)
    hit = (rank_of_e[None, :] == jnp.arange(n_blocks, dtype=I32)[:, None]) & nonempty[None, :]
    expert_of_rank = jnp.sum(jnp.where(hit, e_ids[None, :], 0), axis=1)
    n_ranks = jnp.sum(nonempty.astype(I32)).reshape(1)

    tri =(np.arange(TT_ROUTE)[:, None] <= np.arange(TT_ROUTE)[None, :])
    dest_t = _rank(top_e_t, pstarts.astype(F32).reshape(N_EXPERTS, 1), jnp.asarray(tri, BF16))
    n_valid = n_valid.reshape(1)
    row_tok = _invert(dest_t.reshape(n * TOP_K), n, rows)
    ys = _experts(block_rank.astype(I32), expert_of_rank.astype(I32), n_valid, n_ranks, row_tok, h2_tiles,
                  p["w_gate_e"], p["w_up_e"], p["w_down_e"])
    out = _combine(dest_t, gates_t.T, base, mod, ys, seq)
    return out.reshape(nbatch, seq, d)


def kernel(x, c, positions, w_ada, b_ada, g_norm_mix, w_in, g_q_a, g_k_a, sinks_a, g_q_b, g_k_b,
           g_out_a, g_out_b, w_out, g_norm_ffn, w_router, router_bias, w_gate_e, w_up_e, w_down_e,
           w_gate_s, w_up_s, w_down_s):
    nbatch, seq, d = x.shape
    depth = w_ada.shape[0]
    params = dict(g_norm_mix=g_norm_mix, w_in=w_in, g_q_a=g_q_a, g_k_a=g_k_a, sinks_a=sinks_a,
                  g_q_b=g_q_b, g_k_b=g_k_b, g_out_a=g_out_a, g_out_b=g_out_b, w_out=w_out,
                  g_norm_ffn=g_norm_ffn, w_router=w_router, router_bias=router_bias,
                  w_gate_e=w_gate_e, w_up_e=w_up_e, w_down_e=w_down_e, w_gate_s=w_gate_s,
                  w_up_s=w_up_s, w_down_s=w_down_s)
    j = np.arange(LANES) % HEAD_DIM
    inv = ROPE_THETA ** (-jnp.arange(0, ROT_DIM, 2, dtype=F32) / ROT_DIM)
    invf = jnp.where(j < ROT_DIM, inv[j % (ROT_DIM // 2)], 0.0).astype(F32).reshape(1, LANES)
    bd = jnp.asarray((np.arange(LANES)[:, None] // HEAD_DIM) == (np.arange(LANES)[None, :] // HEAD_DIM),
                     BF16)
    pos = positions.reshape(nbatch * seq, 1).astype(I32)
    for l in range(depth):
        mod = _adaln(c.astype(F32), w_ada[l], b_ada[l]).reshape(nbatch, 6, d)
        x = _layer(x, mod, pos, (invf, bd), {k: v[l] for k, v in params.items()})
    return x
```

```python
import functools

import numpy as np
import jax
import jax.numpy as jnp
from jax import lax
from jax.experimental import pallas as pl
from jax.experimental.pallas import tpu as pltpu
from jax.experimental.pallas import tpu_sc as plsc

F32 = jnp.float32
BF16 = jnp.bfloat16
I32 = jnp.int32

HEAD_DIM = 64
N_HEADS_A = 8
N_KV_A = 2
WINDOW_A = 128
N_HEADS_B = 8
DILATED_BRANCHES = ((128, 1), (512, 4), (2048, 16))
DILS = tuple(dil for _, dil in DILATED_BRANCHES if dil > 1)
BLOCK = 128
ROT_DIM = HEAD_DIM // 4
ROPE_THETA = 500000.0
MIX_A = N_HEADS_A * HEAD_DIM
KV_A = N_KV_A * HEAD_DIM
MIX_B = N_HEADS_B * HEAD_DIM
N_EXPERTS = 256
TOP_K = 8
N_GROUPS = 8
TOPK_GROUPS = 4
GROUP_SIZE = N_EXPERTS // N_GROUPS
ROUTED_SCALE = 2.5
EPS = 1e-6

LANES = 128
HEADS_PER_VREG = LANES // HEAD_DIM
N_PAIRS = MIX_A // LANES
NEG = -1e30
VMEM_LIMIT = 48 * 1024 * 1024

TT_PROJ = 512
ATTN_BLOCKS_PER_STEP = 8
TT_ROUTE = 256
TT_DISPATCH = 256
TT_COMBINE = 256
EXPERT_ROWS = 256
TOK_CHUNK_BLOCKS = 32
GATHER_DEPTH = 4
WEIGHT_DEPTH = 3
BLOCKS_PER_STEP = 2
SUBLANES = 8
ROW_QUEUE_SPLIT = 2

PAIR_ORDER_A = tuple(h for p in range(N_PAIRS) for h in (p, p + N_HEADS_A // N_KV_A))


def _params(*sem):
    return pltpu.CompilerParams(dimension_semantics=sem, vmem_limit_bytes=VMEM_LIMIT)


def _silu(t):
    return t / (1.0 + jnp.exp(-t))


def _rms_rows(t):
    return t * lax.rsqrt(jnp.mean(t * t, axis=-1, keepdims=True) + EPS)


def _store_row_tiles(ref, value, row0=0):
    rows, d = value.shape
    per = d // LANES
    for s in range(per):
        ref[pl.ds(row0 + s, rows, stride=per), :] = value[:, s * LANES:(s + 1) * LANES]


def _load_row_tiles(ref, rows, per, dtype):
    return jnp.concatenate([ref[pl.ds(s, rows, stride=per), :].astype(dtype) for s in range(per)], axis=1)


def _tile_copy(src_ref, src_row, dst_ref, dst_row, per, sem):
    start = lambda row: row * per if isinstance(row, int) else pl.multiple_of(row * per, per)
    return pltpu.make_async_copy(src_ref.at[pl.ds(start(src_row), per)],
                                 dst_ref.at[pl.ds(start(dst_row), per)], sem)


def _ada_kernel(c_ref, w_ref, b_ref, o_ref):
    cond = _silu(c_ref[...])
    o_ref[...] = jnp.dot(cond.astype(BF16), w_ref[...].astype(BF16),
                         preferred_element_type=F32) + b_ref[...]


def _adaln(c, w_ada, b_ada):
    nb, d = c.shape
    width = w_ada.shape[1]
    tn = 1024
    return pl.pallas_call(
        _ada_kernel,
        grid=(width // tn,),
        in_specs=[pl.BlockSpec((nb, d), lambda j: (0, 0)),
                  pl.BlockSpec((d, tn), lambda j: (0, j)),
                  pl.BlockSpec((1, tn), lambda j: (0, j))],
        out_specs=pl.BlockSpec((nb, tn), lambda j: (0, j)),
        out_shape=jax.ShapeDtypeStruct((nb, width), F32),
        compiler_params=_params("arbitrary"),
        name="adaln",
    )(c, w_ada, b_ada.reshape(1, width))


COL_QA, COL_KA, COL_VA = 0, MIX_A, MIX_A + KV_A
COL_QB = MIX_A + 2 * KV_A
COL_KB, COL_VB = COL_QB + MIX_B, COL_QB + 2 * MIX_B
IN_WIDTH = COL_VB + MIX_B


def _inproj_kernel(x_ref, mod_ref, pos_ref, gn_ref, w_ref, gcol_ref, invf_ref, bd_ref,
                   qa_ref, ka_ref, va_ref, qb_ref, kb_ref, vb_ref, *rest):
    n_dil = len(DILS)
    dil_refs = [rest[i * n_dil:(i + 1) * n_dil] for i in range(3)]
    qb_scr, kb_scr, vb_scr = rest[3 * n_dil:]
    tt = x_ref.shape[0]
    shift, scale = mod_ref[0:1, :], mod_ref[1:2, :]
    h = _rms_rows(x_ref[...]) * gn_ref[...] * (1.0 + scale) + shift
    proj = jnp.dot(h.astype(BF16), w_ref[...], preferred_element_type=F32)

    ang = pos_ref[...].astype(F32) * invf_ref[...]
    cs, sn = jnp.cos(ang), jnp.sin(ang)
    lane = lax.broadcasted_iota(I32, (1, LANES), 1) % HEAD_DIM
    s_lo = jnp.where(lane < ROT_DIM // 2, -sn, 0.0)
    s_hi = jnp.where((lane >= ROT_DIM // 2) & (lane < ROT_DIM), sn, 0.0)
    bd = bd_ref[...]

    def norm_rope(col0, width, out_ref, scr=None):
        for j in range(width // LANES):
            c = col0 + j * LANES
            t = proj[:, c:c + LANES]
            sq = t * t
            hi = sq.astype(BF16)
            lo = (sq - hi.astype(F32)).astype(BF16)
            ss = (jnp.dot(hi, bd, preferred_element_type=F32)
                  + jnp.dot(lo, bd, preferred_element_type=F32))
            t = t * lax.rsqrt(ss * (1.0 / HEAD_DIM) + EPS) * gcol_ref[:, c:c + LANES]
            t = (t * cs + pltpu.roll(t, LANES - ROT_DIM // 2, 1) * s_lo
                 + pltpu.roll(t, ROT_DIM // 2, 1) * s_hi)
            out_ref[:, j * LANES:(j + 1) * LANES] = t.astype(BF16)
            if scr is not None:
                scr[j] = t

    norm_rope(COL_QA, MIX_A, qa_ref)
    norm_rope(COL_KA, KV_A, ka_ref)
    norm_rope(COL_QB, MIX_B, qb_ref, qb_scr)
    norm_rope(COL_KB, MIX_B, kb_ref, kb_scr)
    va_ref[...] = proj[:, COL_VA:COL_VA + KV_A].astype(BF16)
    vb_ref[...] = proj[:, COL_VB:COL_VB + MIX_B].astype(BF16)
    for j in range(N_PAIRS):
        vb_scr[j] = proj[:, COL_VB + j * LANES:COL_VB + (j + 1) * LANES]
    for scr, outs in zip((qb_scr, kb_scr, vb_scr), dil_refs):
        for dil, out in zip(DILS, outs):
            for r in range(dil):
                for j in range(N_PAIRS):
                    c = r * MIX_B + j * LANES
                    out[:, c:c + LANES] = scr[j, pl.ds(r, tt // dil, stride=dil), :].astype(BF16)


def _inproj(xf, mod, pos, g_norm, w_in_p, gcol, invf, bd, seq):
    n, d = xf.shape
    tt = TT_PROJ
    tiles_per_seq = seq // tt
    shapes = [(n, w) for w in (MIX_A, KV_A, KV_A, MIX_B, MIX_B, MIX_B)]
    shapes += [(n // dil, dil * MIX_B) for _ in range(3) for dil in DILS]
    full = lambda shape: pl.BlockSpec(shape, lambda i: (0,) * len(shape))
    return pl.pallas_call(
        _inproj_kernel,
        grid=(n // tt,),
        in_specs=[pl.BlockSpec((tt, d), lambda i: (i, 0)),
                  pl.BlockSpec((None, 6, d), lambda i: (i // tiles_per_seq, 0, 0)),
                  pl.BlockSpec((tt, 1), lambda i: (i, 0)),
                  full((1, d)), full((d, IN_WIDTH)), full((1, IN_WIDTH)),
                  full((1, LANES)), full((LANES, LANES))],
        out_specs=[pl.BlockSpec((tt * r // n, w), lambda i: (i, 0)) for r, w in shapes],
        out_shape=[jax.ShapeDtypeStruct(s, BF16) for s in shapes],
        scratch_shapes=[pltpu.VMEM((N_PAIRS, tt, LANES), F32)] * 3,
        compiler_params=_params("parallel"),
        name="inproj",
    )(xf, mod, pos, g_norm, w_in_p, gcol, invf, bd)


def _attn_kernel(*refs, kv_shared, max_dist, n_qblk, n_res, use_prev, has_sinks, want_lse):
    refs = list(refs)
    sink_ref = refs.pop(0) if has_sinks else None
    q_ref = refs.pop(0)
    kp_ref = refs.pop(0) if use_prev else None
    kc_ref = refs.pop(0)
    vp_ref = refs.pop(0) if use_prev else None
    vc_ref = refs.pop(0)
    o_ref = refs.pop(0)
    lse_ref = refs.pop(0) if want_lse else None

    first_step = pl.program_id(2) == 0
    nq = 2 * BLOCK
    kw = LANES if kv_shared else MIX_B

    def band(nk):
        qpos = lax.broadcasted_iota(I32, (nq, nk), 0) % BLOCK
        kpos = lax.broadcasted_iota(I32, (nq, nk), 1)
        dist = qpos + (nk - BLOCK) - kpos
        return (dist >= 0) & (dist <= max_dist), kpos

    band2, kpos2 = band(2 * BLOCK)
    band1, _ = band(BLOCK)
    lane = lax.broadcasted_iota(I32, (nq, LANES), 1)
    row = lax.broadcasted_iota(I32, (nq, LANES), 0)
    own_half = (lane < HEAD_DIM) == (row < BLOCK)
    left_lanes = lax.broadcasted_iota(I32, (BLOCK, LANES), 1) < HEAD_DIM
    lane8 = lax.broadcasted_iota(I32, (BLOCK, 2 * N_PAIRS), 1)

    for res in range(n_res):
        for qb in range(n_qblk):
            rows = slice(qb * BLOCK, (qb + 1) * BLOCK)
            lse_blk = jnp.zeros((BLOCK, 2 * N_PAIRS), F32)
            for p in range(N_PAIRS):
                cq = slice(res * MIX_B + p * LANES, res * MIX_B + (p + 1) * LANES)
                ck = slice(res * kw, res * kw + LANES) if kv_shared else slice(
                    res * kw + p * LANES, res * kw + (p + 1) * LANES)
                qp = q_ref[rows, cq]
                qs = jnp.concatenate([qp, qp], axis=0)
                qs = jnp.where(own_half, qs, jnp.zeros_like(qs))
                if qb > 0:
                    keys = slice((qb - 1) * BLOCK, (qb + 1) * BLOCK)
                    k, v, valid = kc_ref[keys, ck], vc_ref[keys, ck], band2
                elif use_prev:
                    k = jnp.concatenate([kp_ref[:, ck], kc_ref[rows, ck]], axis=0)
                    v = jnp.concatenate([vp_ref[:, ck], vc_ref[rows, ck]], axis=0)
                    valid = band2 & ((kpos2 >= BLOCK) | jnp.logical_not(first_step))
                else:
                    k, v, valid = kc_ref[rows, ck], vc_ref[rows, ck], band1
                s = lax.dot_general(qs, k, (((1,), (1,)), ((), ())), preferred_element_type=F32)
                s = jnp.where(valid, s, NEG)
                m = jnp.max(s, axis=-1, keepdims=True)
                if has_sinks:
                    rows1 = lax.broadcasted_iota(I32, (nq, 1), 0)
                    sink = jnp.where(rows1 < BLOCK, sink_ref[2 * p], sink_ref[2 * p + 1])
                    m = jnp.maximum(m, sink)
                e = jnp.exp(s - m)
                l = jnp.sum(e, axis=-1, keepdims=True)
                if has_sinks:
                    l = l + jnp.exp(sink - m)
                o = jnp.dot(e.astype(BF16), v, preferred_element_type=F32) / l
                o_ref[rows, cq] = jnp.where(left_lanes, o[:BLOCK], o[BLOCK:]).astype(BF16)
                if want_lse:
                    lse = m + jnp.log(l)
                    lse_blk = (lse_blk + jnp.where(lane8 == 2 * p, lse[:BLOCK], 0.0)
                               + jnp.where(lane8 == 2 * p + 1, lse[BLOCK:], 0.0))
            if want_lse:
                lse_ref[res, rows, :] = lse_blk


def _attention(q, k, v, *, nbatch, seq, dil, max_dist, kv_shared, sinks=None, want_lse):
    length = seq // dil
    nblk = length // BLOCK
    n_qblk = min(nblk, ATTN_BLOCKS_PER_STEP)
    n_res = min(dil, ATTN_BLOCKS_PER_STEP // n_qblk)
    steps = nblk // n_qblk
    use_prev = steps > 1
    kw = k.shape[1] // dil
    view = lambda t: t.reshape(nbatch, length, t.shape[1])
    cur = lambda b, r, i: (b, i, r)
    prev = lambda b, r, i: (b, jnp.maximum(i * n_qblk - 1, 0), r)
    in_specs, args = [], []
    if sinks is not None:
        in_specs.append(pl.BlockSpec(memory_space=pltpu.SMEM))
        args.append(sinks)
    in_specs.append(pl.BlockSpec((None, n_qblk * BLOCK, n_res * MIX_B), cur))
    args.append(view(q))
    for t in (k, v):
        if use_prev:
            in_specs.append(pl.BlockSpec((None, BLOCK, n_res * kw), prev))
            args.append(view(t))
        in_specs.append(pl.BlockSpec((None, n_qblk * BLOCK, n_res * kw), cur))
        args.append(view(t))
    out_specs = [pl.BlockSpec((None, n_qblk * BLOCK, n_res * MIX_B), cur)]
    out_shape = [jax.ShapeDtypeStruct((nbatch, length, dil * MIX_B), BF16)]
    if want_lse:
        out_specs.append(pl.BlockSpec((None, n_res, n_qblk * BLOCK, N_HEADS_B),
                                      lambda b, r, i: (b, r, i, 0)))
        out_shape.append(jax.ShapeDtypeStruct((nbatch, dil, length, N_HEADS_B), F32))
    outs = pl.pallas_call(
        functools.partial(_attn_kernel, kv_shared=kv_shared, max_dist=max_dist, n_qblk=n_qblk,
                          n_res=n_res, use_prev=use_prev, has_sinks=sinks is not None,
                          want_lse=want_lse),
        grid=(nbatch, dil // n_res, steps),
        in_specs=in_specs, out_specs=out_specs, out_shape=out_shape,
        compiler_params=_params("parallel", "parallel", "arbitrary"),
        name=f"attn_d{dil}" + ("_swa" if kv_shared else ""),
    )(*args)
    o = outs[0].reshape(nbatch * length, dil * MIX_B)
    if not want_lse:
        return o, None
    lse = outs[1].transpose(0, 2, 1, 3).reshape(nbatch * seq, N_HEADS_B)
    return o, lse


def _expand_heads(w, width):
    head = lax.broadcasted_iota(I32, (1, width), 1) // HEAD_DIM
    out = jnp.zeros((w.shape[0], width), F32)
    for hd in range(w.shape[1]):
        out = jnp.where(head == hd, w[:, hd:hd + 1], out)
    return out


def _outproj_kernel(x_ref, mod_ref, oa_ref, ob1_ref, ob2_ref, ob3_ref, l1_ref, l2_ref, l3_ref,
                    goa_ref, gob_ref, wo_ref, gf_ref, wgs_ref, wus_ref, wds_ref,
                    h2_ref, h2t_ref, base_ref, ob_scr):
    tt = x_ref.shape[0]
    gate_a = mod_ref[2:3, :]
    shift_m, scale_m, gate_m = mod_ref[3:4, :], mod_ref[4:5, :], mod_ref[5:6, :]

    def token_major(ref, dil):
        if dil == 1:
            return ref[...].astype(F32)
        for r in range(dil):
            for j in range(N_PAIRS):
                c = r * MIX_B + j * LANES
                ob_scr[j, pl.ds(r, tt // dil, stride=dil), :] = ref[:, c:c + LANES].astype(F32)
        return jnp.concatenate([ob_scr[j] for j in range(N_PAIRS)], axis=1)

    l1, l2, l3 = l1_ref[...], l2_ref[...], l3_ref[...]
    mx = jnp.maximum(jnp.maximum(l1, l2), l3)
    e1, e2, e3 = jnp.exp(l1 - mx), jnp.exp(l2 - mx), jnp.exp(l3 - mx)
    den = e1 + e2 + e3
    dils = [dil for _, dil in DILATED_BRANCHES]
    ob = _expand_heads(e1 / den, MIX_B) * token_major(ob1_ref, dils[0])
    ob = ob + _expand_heads(e2 / den, MIX_B) * token_major(ob2_ref, dils[1])
    ob = ob + _expand_heads(e3 / den, MIX_B) * token_major(ob3_ref, dils[2])
    ob = _rms_rows(ob) * gob_ref[...]
    oa = _rms_rows(oa_ref[...].astype(F32)) * goa_ref[...]
    y = (jnp.dot(oa.astype(BF16), wo_ref[0:MIX_A, :], preferred_element_type=F32)
         + jnp.dot(ob.astype(BF16), wo_ref[MIX_A:MIX_A + MIX_B, :], preferred_element_type=F32))
    x1 = x_ref[...] + gate_a * y
    h2 = _rms_rows(x1) * gf_ref[...] * (1.0 + scale_m) + shift_m
    h2_ref[...] = h2
    _store_row_tiles(h2t_ref, h2)
    hb = h2.astype(BF16)
    act = (_silu(jnp.dot(hb, wgs_ref[...], preferred_element_type=F32))
           * jnp.dot(hb, wus_ref[...], preferred_element_type=F32))
    shared = jnp.dot(act.astype(BF16), wds_ref[...], preferred_element_type=F32)
    base_ref[...] = x1 + gate_m * shared


def _outproj(xf, mod, oa, obs, lses, goa, gob, wo_p, gf, wgs, wus, wds, seq):
    n, d = xf.shape
    tt = TT_PROJ // 2
    tiles_per_seq = seq // tt
    tile = lambda w: pl.BlockSpec((tt, w), lambda i: (i, 0))
    full = lambda shape: pl.BlockSpec(shape, lambda i: (0,) * len(shape))
    sd = wgs.shape[1]
    dilated = [pl.BlockSpec((tt // dil, dil * MIX_B), lambda i: (i, 0)) for _, dil in DILATED_BRANCHES]
    return pl.pallas_call(
        _outproj_kernel,
        grid=(n // tt,),
        in_specs=[tile(d), pl.BlockSpec((None, 6, d), lambda i: (i // tiles_per_seq, 0, 0)),
                  tile(MIX_A), *dilated,
                  tile(N_HEADS_B), tile(N_HEADS_B), tile(N_HEADS_B),
                  full((1, MIX_A)), full((1, MIX_B)), full((MIX_A + MIX_B, d)), full((1, d)),
                  full((d, sd)), full((d, sd)), full((sd, d))],
        out_specs=[tile(d), pl.BlockSpec((tt * d // LANES, LANES), lambda i: (i, 0)), tile(d)],
        out_shape=[jax.ShapeDtypeStruct((n, d), F32), jax.ShapeDtypeStruct((n * d // LANES, LANES), F32),
                   jax.ShapeDtypeStruct((n, d), F32)],
        scratch_shapes=[pltpu.VMEM((N_PAIRS, tt, LANES), F32)],
        compiler_params=_params("parallel"),
        name="outproj",
    )(xf, mod, oa, *obs, *lses, goa, gob, wo_p, gf, wgs, wus, wds)


def _router_kernel(h_ref, wrt_ref, bias_ref, e_ref, g_ref, cnt_ref):
    tt = h_ref.shape[0]
    logits = lax.dot_general(wrt_ref[...], h_ref[...].astype(BF16), (((1,), (1,)), ((), ())),
                             preferred_element_type=F32)
    scores = 1.0 / (1.0 + jnp.exp(-logits))
    biased = scores + bias_ref[...]
    ninf = -jnp.inf

    j32 = lax.broadcasted_iota(I32, (GROUP_SIZE, tt), 0).astype(F32)
    grp = []
    for g in range(N_GROUPS):
        bg = biased[g * GROUP_SIZE:(g + 1) * GROUP_SIZE, :]
        m1 = jnp.max(bg, axis=0, keepdims=True)
        i1 = jnp.min(jnp.where(bg == m1, j32, float(GROUP_SIZE)), axis=0, keepdims=True)
        m2 = jnp.max(jnp.where(j32 == i1, ninf, bg), axis=0, keepdims=True)
        grp.append(m1 + m2)
    grp = jnp.concatenate(grp, axis=0)
    g8 = lax.broadcasted_iota(I32, (N_GROUPS, tt), 0).astype(F32)
    chosen = jnp.zeros((N_GROUPS, tt), F32)
    for _ in range(TOPK_GROUPS):
        gm = jnp.max(grp, axis=0, keepdims=True)
        gi = jnp.min(jnp.where(grp == gm, g8, float(N_GROUPS)), axis=0, keepdims=True)
        hit = g8 == gi
        chosen = jnp.where(hit, 1.0, chosen)
        grp = jnp.where(hit, ninf, grp)
    masked = jnp.concatenate(
        [jnp.where(chosen[g:g + 1, :] > 0.0, biased[g * GROUP_SIZE:(g + 1) * GROUP_SIZE, :], ninf)
         for g in range(N_GROUPS)], axis=0)

    eio = lax.broadcasted_iota(I32, (N_EXPERTS, tt), 0).astype(F32)
    picked = jnp.zeros((N_EXPERTS, tt), F32)
    es, gs = [], []
    for _ in range(TOP_K):
        m = jnp.max(masked, axis=0, keepdims=True)
        idx = jnp.min(jnp.where(masked == m, eio, float(N_EXPERTS)), axis=0, keepdims=True)
        hit = eio == idx
        gs.append(jnp.sum(jnp.where(hit, scores, 0.0), axis=0, keepdims=True))
        es.append(idx)
        picked = jnp.where(hit, 1.0, picked)
        masked = jnp.where(hit, ninf, masked)
    gates = jnp.concatenate(gs, axis=0)
    e_ref[...] = jnp.concatenate(es, axis=0).astype(I32)
    g_ref[...] = gates / jnp.sum(gates, axis=0, keepdims=True) * ROUTED_SCALE

    @pl.when(pl.program_id(0) == 0)
    def _():
        cnt_ref[...] = jnp.zeros_like(cnt_ref)
    cnt_ref[...] += jnp.sum(picked, axis=1, keepdims=True)


def _router(h2, wrt, bias_col):
    n, d = h2.shape
    tt = TT_ROUTE
    return pl.pallas_call(
        _router_kernel,
        grid=(n // tt,),
        in_specs=[pl.BlockSpec((tt, d), lambda i: (i, 0)),
                  pl.BlockSpec((N_EXPERTS, d), lambda i: (0, 0)),
                  pl.BlockSpec((N_EXPERTS, 1), lambda i: (0, 0))],
        out_specs=[pl.BlockSpec((TOP_K, tt), lambda i: (0, i)),
                   pl.BlockSpec((TOP_K, tt), lambda i: (0, i)),
                   pl.BlockSpec((N_EXPERTS, 1), lambda i: (0, 0))],
        out_shape=[jax.ShapeDtypeStruct((TOP_K, n), I32),
                   jax.ShapeDtypeStruct((TOP_K, n), F32),
                   jax.ShapeDtypeStruct((N_EXPERTS, 1), F32)],
        compiler_params=_params("arbitrary"),
        name="router",
    )(h2, wrt, bias_col)


def _rank_kernel(e_ref, pstart_ref, tri_ref, dest_ref, carry_ref):
    tt = e_ref.shape[1]

    @pl.when(pl.program_id(0) == 0)
    def _():
        carry_ref[...] = pstart_ref[...]

    e = e_ref[...]
    eio = lax.broadcasted_iota(I32, (N_EXPERTS, tt), 0)
    mask = jnp.zeros((N_EXPERTS, tt), F32)
    for k in range(TOP_K):
        mask = jnp.where(eio == e[k:k + 1, :], 1.0, mask)
    incl = jnp.dot(mask.astype(BF16), tri_ref[...], preferred_element_type=F32)
    pos = incl - mask + carry_ref[...]
    dest = [jnp.sum(jnp.where(eio == e[k:k + 1, :], pos, 0.0), axis=0, keepdims=True)
            for k in range(TOP_K)]
    dest_ref[...] = jnp.concatenate(dest, axis=0).astype(I32)
    carry_ref[...] += incl[:, tt - 1:tt]


def _rank(top_e_t, pstart_col, tri):
    n = top_e_t.shape[1]
    tt = TT_ROUTE
    return pl.pallas_call(
        _rank_kernel,
        grid=(n // tt,),
        in_specs=[pl.BlockSpec((TOP_K, tt), lambda i: (0, i)),
                  pl.BlockSpec((N_EXPERTS, 1), lambda i: (0, 0)),
                  pl.BlockSpec((tt, tt), lambda i: (0, 0))],
        out_specs=pl.BlockSpec((TOP_K, tt), lambda i: (0, i)),
        out_shape=jax.ShapeDtypeStruct((TOP_K, n), I32),
        scratch_shapes=[pltpu.VMEM((N_EXPERTS, 1), F32)],
        compiler_params=_params("arbitrary"),
        name="rank",
    )(top_e_t, pstart_col, tri)


def _row_copy(src_ref, src_row, dst_ref, dst_row, sem):
    return pltpu.make_async_copy(src_ref.at[pl.ds(src_row, 1)], dst_ref.at[pl.ds(dst_row, 1)], sem)


SC_CORES, SC_SUBCORES, SC_LANES = 2, 16, 16
SC_WORKERS = SC_CORES * SC_SUBCORES
INVERT_CHUNK = 8192


def _invert(dest_flat, n_tokens, rows):
    per = rows // SC_WORKERS
    n_assign = dest_flat.shape[0]
    assert rows % (SC_WORKERS * SC_LANES) == 0 and n_assign % INVERT_CHUNK == 0
    assert n_tokens & (n_tokens - 1) == 0

    @functools.partial(
        pl.kernel, mesh=plsc.VectorSubcoreMesh(core_axis_name="c", subcore_axis_name="s"),
        out_type=jax.ShapeDtypeStruct((rows,), I32),
        scratch_types=[pltpu.VMEM((INVERT_CHUNK,), I32), pltpu.VMEM((per,), I32)],
        compiler_params=pltpu.CompilerParams(needs_layout_passes=False))
    def invert(dest_hbm, out_hbm, staged, local):
        base = (lax.axis_index("s") * SC_CORES + lax.axis_index("c")) * per
        lane = lax.iota(I32, SC_LANES)

        @pl.loop(0, per, step=SC_LANES)
        def _(i):
            local[pl.ds(i, SC_LANES)] = (base + i + lane) & (n_tokens - 1)

        @pl.loop(0, n_assign // INVERT_CHUNK)
        def _(c):
            pltpu.sync_copy(dest_hbm.at[pl.ds(c * INVERT_CHUNK, INVERT_CHUNK)], staged)

            @pl.loop(0, INVERT_CHUNK, step=SC_LANES)
            def _(i):
                rel = staged[pl.ds(i, SC_LANES)] - base
                mine = (rel >= 0) & (rel < per)
                tok = (c * INVERT_CHUNK + i + lane) & (n_tokens - 1)
                plsc.store_scatter(local, [jnp.where(mine, rel, 0)], tok, mask=mine)

        pltpu.sync_copy(local, out_hbm.at[pl.ds(base, per)])

    return invert(dest_flat)


def _experts_kernel(rank_ref, eor_ref, nv_ref, nr_ref, tok_hbm, h_ref, wg_hbm, wu_hbm, wd_hbm,
                    ys_ref, *scratch):
    xbufs = scratch[:GATHER_DEPTH]
    wgf, wuf, wdf, wgb, wub, wdb, tok_s, sem, tok_sem, w_sem = scratch[GATHER_DEPTH:]
    step = pl.program_id(0)
    nv, n_ranks = nv_ref[0], nr_ref[0]
    per = wgb.shape[0] // LANES
    depth, bm = len(xbufs), xbufs[0].shape[0] // per
    ahead = depth - 1
    chunk = tok_s.shape[0] // 2
    cb = chunk // bm
    n_chunks = tok_hbm.shape[0] // chunk
    n_blocks = n_chunks * cb

    def tok_copy(c):
        return pltpu.make_async_copy(
            tok_hbm.at[pl.ds(pl.multiple_of(c * chunk, chunk), chunk)],
            tok_s.at[pl.ds(pl.multiple_of((c % 2) * chunk, chunk), chunk)], tok_sem.at[c % 2])

    def gather(block, ring, unrolled):
        block = jnp.minimum(block, n_blocks - 1)
        base = ((block // cb) % 2) * chunk + (block % cb) * bm
        buf, buf_sem = xbufs[ring], sem.at[ring]

        def issue(i, carry=None):
            _tile_copy(h_ref, tok_s[base + i], buf, i, per, buf_sem).start()
            return carry

        if unrolled:
            for i in range(bm):
                _tile_copy(h_ref, tok_s[base + i], buf, i, per, buf_sem).start(
                    priority=int(i % ROW_QUEUE_SPLIT != 0))
        else:
            lax.fori_loop(0, bm, issue, 0, unroll=8)

    def wait_rows(ring):
        pltpu.make_async_copy(h_ref.at[pl.ds(0, bm * per)], xbufs[ring], sem.at[ring]).wait()

    def weight_copies(r):
        e, s = eor_ref[r], r % WEIGHT_DEPTH
        return [pltpu.make_async_copy(src.at[e], dst.at[s], w_sem.at[s])
                for src, dst in ((wg_hbm, wgf), (wu_hbm, wuf), (wd_hbm, wdf))]

    @pl.when(step == 0)
    def _():
        tok_copy(0).start()
        tok_copy(0).wait()
        tok_copy(1).start()
        for r in range(WEIGHT_DEPTH - 1):
            @pl.when(r < n_ranks)
            def _():
                for cp in weight_copies(r):
                    cp.start()
        for b in range(ahead):
            gather(b, b, unrolled=False)

    def block(j, ring, out_row0):
        first = j + ahead
        c_need = first // cb

        @pl.when(jnp.logical_and(jnp.logical_and(j > 0, first % cb == 0), c_need < n_chunks))
        def _():
            tok_copy(c_need).wait()

            @pl.when(c_need + 1 < n_chunks)
            def _():
                tok_copy(c_need + 1).start()

        jb = jnp.minimum(j, n_blocks - 1)
        rank = rank_ref[jb]
        fresh = jnp.logical_or(j == 0, rank != rank_ref[jnp.maximum(jb - 1, 0)])

        @pl.when(jnp.logical_and(j < nv, fresh))
        def _():
            for cp in weight_copies(rank):
                cp.wait()
            s = rank % WEIGHT_DEPTH
            wgb[...] = wgf[s].astype(BF16)
            wub[...] = wuf[s].astype(BF16)
            wdb[...] = wdf[s].astype(BF16)
            nxt = rank + WEIGHT_DEPTH - 1

            @pl.when(nxt < n_ranks)
            def _():
                for cp in weight_copies(nxt):
                    cp.start()

        @pl.when(j < nv)
        def _():
            wait_rows(ring)
            gather(j + ahead, (ring + ahead) % depth, unrolled=True)
            xb = _load_row_tiles(xbufs[ring], bm, per, BF16)
            act = (_silu(jnp.dot(xb, wgb[...], preferred_element_type=F32))
                   * jnp.dot(xb, wub[...], preferred_element_type=F32))
            _store_row_tiles(ys_ref, jnp.dot(act.astype(BF16), wdb[...], preferred_element_type=F32),
                             out_row0)

        @pl.when(jnp.logical_and(j >= nv, j < nv + ahead))
        def _():
            wait_rows(ring)

    half = BLOCKS_PER_STEP
    for parity in range(depth // half):
        @pl.when(step % (depth // half) == parity)
        def _():
            for h in range(half):
                block(step * half + h, parity * half + h, h * bm * per)


def _experts(block_rank, expert_of_rank, n_valid, n_ranks, row_tok, h2_tiles, wg, wu, wd):
    rows = row_tok.shape[0]
    d = wg.shape[1]
    per = d // LANES
    bm = EXPERT_ROWS
    n_blocks = rows // bm
    assert n_blocks % TOK_CHUNK_BLOCKS == 0 and n_blocks // TOK_CHUNK_BLOCKS >= 2
    assert GATHER_DEPTH - 1 < TOK_CHUNK_BLOCKS
    bps = BLOCKS_PER_STEP
    assert GATHER_DEPTH % bps == 0 and n_blocks % bps == 0
    f = wg.shape[2]
    hbm = pl.BlockSpec(memory_space=pl.ANY)
    return pl.pallas_call(
        _experts_kernel,
        grid_spec=pltpu.PrefetchScalarGridSpec(
            num_scalar_prefetch=4,
            grid=((n_blocks + GATHER_DEPTH) // bps,),
            in_specs=[hbm, hbm, hbm, hbm, hbm],
            out_specs=pl.BlockSpec((bps * bm * per, LANES),
                                   lambda s, rk, eor, nv, nr: (jnp.minimum(s, (nv[0] - 1) // bps), 0)),
            scratch_shapes=[pltpu.VMEM((bm * per, LANES), F32)] * GATHER_DEPTH + [
                            pltpu.VMEM((WEIGHT_DEPTH, d, f), F32), pltpu.VMEM((WEIGHT_DEPTH, d, f), F32),
                            pltpu.VMEM((WEIGHT_DEPTH, f, d), F32),
                            pltpu.VMEM((d, f), BF16), pltpu.VMEM((d, f), BF16),
                            pltpu.VMEM((f, d), BF16),
                            pltpu.SMEM((2 * TOK_CHUNK_BLOCKS * bm,), I32),
                            pltpu.SemaphoreType.DMA((GATHER_DEPTH,)), pltpu.SemaphoreType.DMA((2,)),
                            pltpu.SemaphoreType.DMA((WEIGHT_DEPTH,))]),
        out_shape=jax.ShapeDtypeStruct((rows * per, LANES), F32),
        compiler_params=_params("arbitrary"),
        name="experts",
    )(block_rank, expert_of_rank, n_valid, n_ranks, row_tok, h2_tiles, wg, wu, wd)


def _combine_kernel(dest_ref, gates_ref, base_ref, mod_ref, ys_ref, out_ref, buf, sem):
    tt, d = base_ref.shape
    per = d // LANES
    for t in range(tt):
        for k in range(TOP_K):
            _tile_copy(ys_ref, dest_ref[k, t], buf.at[k], t, per, sem).start(priority=k % 2)
    for k in range(TOP_K):
        pltpu.make_async_copy(ys_ref.at[pl.ds(0, tt * per)], buf.at[k], sem).wait()
    gates = gates_ref[...]
    routed = gates[:, 0:1] * _load_row_tiles(buf.at[0], tt, per, F32)
    for k in range(1, TOP_K):
        routed = routed + gates[:, k:k + 1] * _load_row_tiles(buf.at[k], tt, per, F32)
    out_ref[...] = base_ref[...] + mod_ref[5:6, :] * routed


def _combine(dest_t, gates, base, mod, ys, seq):
    n, d = base.shape
    tt = TT_COMBINE
    tiles_per_seq = seq // tt
    return pl.pallas_call(
        _combine_kernel,
        grid=(n // tt,),
        in_specs=[pl.BlockSpec((TOP_K, tt), lambda i: (0, i), memory_space=pltpu.SMEM),
                  pl.BlockSpec((tt, TOP_K), lambda i: (i, 0)),
                  pl.BlockSpec((tt, d), lambda i: (i, 0)),
                  pl.BlockSpec((None, 6, d), lambda i: (i // tiles_per_seq, 0, 0)),
                  pl.BlockSpec(memory_space=pl.ANY)],
        out_specs=pl.BlockSpec((tt, d), lambda i: (i, 0)),
        out_shape=jax.ShapeDtypeStruct((n, d), F32),
        scratch_shapes=[pltpu.VMEM((TOP_K, tt * d // LANES, LANES), F32), pltpu.SemaphoreType.DMA],
        compiler_params=_params("arbitrary"),
        name="combine",
    )(dest_t, gates, base, mod, ys)


def _layer(x, mod, pos, rope, p):
    nbatch, seq, d = x.shape
    n = nbatch * seq
    xf = x.reshape(n, d)
    invf, bd = rope

    perm = np.concatenate([np.arange(h * HEAD_DIM, (h + 1) * HEAD_DIM) for h in PAIR_ORDER_A])
    w_in = p["w_in"]
    w_in_p = jnp.concatenate([w_in[:, :MIX_A][:, perm], w_in[:, MIX_A:]], axis=1).astype(BF16)
    ones = lambda w: jnp.ones((w,), F32)
    qscale = HEAD_DIM ** -0.5
    gcol = jnp.concatenate([jnp.tile(p["g_q_a"], N_HEADS_A) * qscale, jnp.tile(p["g_k_a"], N_KV_A),
                            ones(KV_A), jnp.tile(p["g_q_b"], N_HEADS_B) * qscale,
                            jnp.tile(p["g_k_b"], N_HEADS_B), ones(MIX_B)]).reshape(1, IN_WIDTH)
    proj = _inproj(xf, mod, pos, p["g_norm_mix"].reshape(1, d), w_in_p, gcol, invf, bd, seq)
    qa, ka, va = proj[:3]
    qkv_b = {1: proj[3:6]}
    for j, dil in enumerate(DILS):
        qkv_b[dil] = [proj[6 + t * len(DILS) + j] for t in range(3)]

    sinks_p = p["sinks_a"][np.array(PAIR_ORDER_A)]
    oa, _ = _attention(qa, ka, va, nbatch=nbatch, seq=seq, dil=1, max_dist=WINDOW_A - 1,
                       kv_shared=True, sinks=sinks_p, want_lse=False)
    obs, lses = [], []
    for window, dil in DILATED_BRANCHES:
        o, lse = _attention(*qkv_b[dil], nbatch=nbatch, seq=seq, dil=dil, max_dist=window // dil,
                            kv_shared=False, want_lse=True)
        obs.append(o)
        lses.append(lse)

    goa = p["g_out_a"][perm].reshape(1, MIX_A)
    w_out = p["w_out"]
    wo_p = jnp.concatenate([w_out[:MIX_A][perm], w_out[MIX_A:]], axis=0).astype(BF16)
    h2, h2_tiles, base = _outproj(xf, mod, oa, obs, lses, goa, p["g_out_b"].reshape(1, MIX_B), wo_p,
                        p["g_norm_ffn"].reshape(1, d), p["w_gate_s"].astype(BF16),
                        p["w_up_s"].astype(BF16), p["w_down_s"].astype(BF16), seq)

    top_e_t, gates_t, counts = _router(h2, p["w_router"].T.astype(BF16),
                                       p["router_bias"].reshape(N_EXPERTS, 1))
    bm = EXPERT_ROWS
    counts = counts.reshape(N_EXPERTS).astype(I32)
    padded = (counts + bm - 1) // bm * bm
    pends = jnp.cumsum(padded)
    pstarts = pends - padded
    rows = n * TOP_K + N_EXPERTS * bm
    n_blocks = rows // bm
    n_valid = (pends[-1] // bm).astype(I32)
    blk = jnp.minimum(jnp.arange(n_blocks, dtype=I32), n_valid - 1)
    block_e = jnp.sum((pends[None, :] <= (blk * bm)[:, None]).astype(I32), axis=1)
    block_e = jnp.minimum(block_e, N_EXPERTS - 1)
    nonempty = counts > 0
    rank_of_e = jnp.cumsum(nonempty.astype(I32)) - 1
    e_ids = jnp.arange(N_EXPERTS, dtype=I32)
    block_rank = jnp.sum(jnp.where(block_e[:, None] == e_ids[None, :], rank_of_e[None, :], 0), axis=1)
    hit = (rank_of_e[None, :] == jnp.arange(n_blocks, dtype=I32)[:, None]) & nonempty[None, :]
    expert_of_rank = jnp.sum(jnp.where(hit, e_ids[None, :], 0), axis=1)
    n_ranks = jnp.sum(nonempty.astype(I32)).reshape(1)

    tri =(np.arange(TT_ROUTE)[:, None] <= np.arange(TT_ROUTE)[None, :])
    dest_t = _rank(top_e_t, pstarts.astype(F32).reshape(N_EXPERTS, 1), jnp.asarray(tri, BF16))
    n_valid = n_valid.reshape(1)
    row_tok = _invert(dest_t.reshape(n * TOP_K), n, rows)
    ys = _experts(block_rank.astype(I32), expert_of_rank.astype(I32), n_valid, n_ranks, row_tok, h2_tiles,
                  p["w_gate_e"], p["w_up_e"], p["w_down_e"])
    out = _combine(dest_t, gates_t.T, base, mod, ys, seq)
    return out.reshape(nbatch, seq, d)


def kernel(x, c, positions, w_ada, b_ada, g_norm_mix, w_in, g_q_a, g_k_a, sinks_a, g_q_b, g_k_b,
           g_out_a, g_out_b, w_out, g_norm_ffn, w_router, router_bias, w_gate_e, w_up_e, w_down_e,
           w_gate_s, w_up_s, w_down_s):
    nbatch, seq, d = x.shape
    depth = w_ada.shape[0]
    params = dict(g_norm_mix=g_norm_mix, w_in=w_in, g_q_a=g_q_a, g_k_a=g_k_a, sinks_a=sinks_a,
                  g_q_b=g_q_b, g_k_b=g_k_b, g_out_a=g_out_a, g_out_b=g_out_b, w_out=w_out,
                  g_norm_ffn=g_norm_ffn, w_router=w_router, router_bias=router_bias,
                  w_gate_e=w_gate_e, w_up_e=w_up_e, w_down_e=w_down_e, w_gate_s=w_gate_s,
                  w_up_s=w_up_s, w_down_s=w_down_s)
    j = np.arange(LANES) % HEAD_DIM
    inv = ROPE_THETA ** (-jnp.arange(0, ROT_DIM, 2, dtype=F32) / ROT_DIM)
    invf = jnp.where(j < ROT_DIM, inv[j % (ROT_DIM // 2)], 0.0).astype(F32).reshape(1, LANES)
    bd = jnp.asarray((np.arange(LANES)[:, None] // HEAD_DIM) == (np.arange(LANES)[None, :] // HEAD_DIM),
                     BF16)
    pos = positions.reshape(nbatch * seq, 1).astype(I32)
    for l in range(depth):
        mod = _adaln(c.astype(F32), w_ada[l], b_ada[l]).reshape(nbatch, 6, d)
        x = _layer(x, mod, pos, (invf, bd), {k: v[l] for k, v in params.items()})
    return x
```

```python
import functools

import numpy as np
import jax
import jax.numpy as jnp
from jax import lax
from jax.experimental import pallas as pl
from jax.experimental.pallas import tpu as pltpu
from jax.experimental.pallas import tpu_sc as plsc

F32 = jnp.float32
BF16 = jnp.bfloat16
I32 = jnp.int32

HEAD_DIM = 64
N_HEADS_A = 8
N_KV_A = 2
WINDOW_A = 128
N_HEADS_B = 8
DILATED_BRANCHES = ((128, 1), (512, 4), (2048, 16))
DILS = tuple(dil for _, dil in DILATED_BRANCHES if dil > 1)
BLOCK = 128
ROT_DIM = HEAD_DIM // 4
ROPE_THETA = 500000.0
MIX_A = N_HEADS_A * HEAD_DIM
KV_A = N_KV_A * HEAD_DIM
MIX_B = N_HEADS_B * HEAD_DIM
N_EXPERTS = 256
TOP_K = 8
N_GROUPS = 8
TOPK_GROUPS = 4
GROUP_SIZE = N_EXPERTS // N_GROUPS
ROUTED_SCALE = 2.5
EPS = 1e-6

LANES = 128
HEADS_PER_VREG = LANES // HEAD_DIM
N_PAIRS = MIX_A // LANES
NEG = -1e30
VMEM_LIMIT = 48 * 1024 * 1024

TT_PROJ = 512
ATTN_BLOCKS_PER_STEP = 8
TT_ROUTE = 512
TT_DISPATCH = 256
TT_COMBINE = 512
EXPERT_ROWS = 256
TOK_CHUNK_BLOCKS = 32
GATHER_DEPTH = 4
WEIGHT_DEPTH = 3
BLOCKS_PER_STEP = 2
SUBLANES = 8
ROW_QUEUE_SPLIT = 2

PAIR_ORDER_A = tuple(h for p in range(N_PAIRS) for h in (p, p + N_HEADS_A // N_KV_A))


def _params(*sem):
    return pltpu.CompilerParams(dimension_semantics=sem, vmem_limit_bytes=VMEM_LIMIT)


def _silu(t):
    return t / (1.0 + jnp.exp(-t))


def _rms_rows(t):
    return t * lax.rsqrt(jnp.mean(t * t, axis=-1, keepdims=True) + EPS)


def _store_row_tiles(ref, value, row0=0):
    rows, d = value.shape
    per = d // LANES
    for s in range(per):
        ref[pl.ds(row0 + s, rows, stride=per), :] = value[:, s * LANES:(s + 1) * LANES]


def _load_row_tiles(ref, rows, per, dtype):
    return jnp.concatenate([ref[pl.ds(s, rows, stride=per), :].astype(dtype) for s in range(per)], axis=1)


def _tile_copy(src_ref, src_row, dst_ref, dst_row, per, sem):
    start = lambda row: row * per if isinstance(row, int) else pl.multiple_of(row * per, per)
    return pltpu.make_async_copy(src_ref.at[pl.ds(start(src_row), per)],
                                 dst_ref.at[pl.ds(start(dst_row), per)], sem)


def _ada_kernel(c_ref, w_ref, b_ref, o_ref):
    cond = _silu(c_ref[...])
    o_ref[...] = jnp.dot(cond.astype(BF16), w_ref[...].astype(BF16),
                         preferred_element_type=F32) + b_ref[...]


def _adaln(c, w_ada, b_ada):
    nb, d = c.shape
    width = w_ada.shape[1]
    tn = 1024
    return pl.pallas_call(
        _ada_kernel,
        grid=(width // tn,),
        in_specs=[pl.BlockSpec((nb, d), lambda j: (0, 0)),
                  pl.BlockSpec((d, tn), lambda j: (0, j)),
                  pl.BlockSpec((1, tn), lambda j: (0, j))],
        out_specs=pl.BlockSpec((nb, tn), lambda j: (0, j)),
        out_shape=jax.ShapeDtypeStruct((nb, width), F32),
        compiler_params=_params("arbitrary"),
        name="adaln",
    )(c, w_ada, b_ada.reshape(1, width))


COL_QA, COL_KA, COL_VA = 0, MIX_A, MIX_A + KV_A
COL_QB = MIX_A + 2 * KV_A
COL_KB, COL_VB = COL_QB + MIX_B, COL_QB + 2 * MIX_B
IN_WIDTH = COL_VB + MIX_B


def _inproj_kernel(x_ref, mod_ref, pos_ref, gn_ref, w_ref, gcol_ref, invf_ref, bd_ref,
                   qa_ref, ka_ref, va_ref, qb_ref, kb_ref, vb_ref, *rest):
    n_dil = len(DILS)
    dil_refs = [rest[i * n_dil:(i + 1) * n_dil] for i in range(3)]
    qb_scr, kb_scr, vb_scr = rest[3 * n_dil:]
    tt = x_ref.shape[0]
    shift, scale = mod_ref[0:1, :], mod_ref[1:2, :]
    h = _rms_rows(x_ref[...]) * gn_ref[...] * (1.0 + scale) + shift
    proj = jnp.dot(h.astype(BF16), w_ref[...], preferred_element_type=F32)

    ang = pos_ref[...].astype(F32) * invf_ref[...]
    cs, sn = jnp.cos(ang), jnp.sin(ang)
    lane = lax.broadcasted_iota(I32, (1, LANES), 1) % HEAD_DIM
    s_lo = jnp.where(lane < ROT_DIM // 2, -sn, 0.0)
    s_hi = jnp.where((lane >= ROT_DIM // 2) & (lane < ROT_DIM), sn, 0.0)
    bd = bd_ref[...]

    def norm_rope(col0, width, out_ref, scr=None):
        for j in range(width // LANES):
            c = col0 + j * LANES
            t = proj[:, c:c + LANES]
            sq = t * t
            hi = sq.astype(BF16)
            lo = (sq - hi.astype(F32)).astype(BF16)
            ss = (jnp.dot(hi, bd, preferred_element_type=F32)
                  + jnp.dot(lo, bd, preferred_element_type=F32))
            t = t * lax.rsqrt(ss * (1.0 / HEAD_DIM) + EPS) * gcol_ref[:, c:c + LANES]
            t = (t * cs + pltpu.roll(t, LANES - ROT_DIM // 2, 1) * s_lo
                 + pltpu.roll(t, ROT_DIM // 2, 1) * s_hi)
            out_ref[:, j * LANES:(j + 1) * LANES] = t.astype(BF16)
            if scr is not None:
                scr[j] = t

    norm_rope(COL_QA, MIX_A, qa_ref)
    norm_rope(COL_KA, KV_A, ka_ref)
    norm_rope(COL_QB, MIX_B, qb_ref, qb_scr)
    norm_rope(COL_KB, MIX_B, kb_ref, kb_scr)
    va_ref[...] = proj[:, COL_VA:COL_VA + KV_A].astype(BF16)
    vb_ref[...] = proj[:, COL_VB:COL_VB + MIX_B].astype(BF16)
    for j in range(N_PAIRS):
        vb_scr[j] = proj[:, COL_VB + j * LANES:COL_VB + (j + 1) * LANES]
    for scr, outs in zip((qb_scr, kb_scr, vb_scr), dil_refs):
        for dil, out in zip(DILS, outs):
            for r in range(dil):
                for j in range(N_PAIRS):
                    c = r * MIX_B + j * LANES
                    out[:, c:c + LANES] = scr[j, pl.ds(r, tt // dil, stride=dil), :].astype(BF16)


def _inproj(xf, mod, pos, g_norm, w_in_p, gcol, invf, bd, seq):
    n, d = xf.shape
    tt = TT_PROJ
    tiles_per_seq = seq // tt
    shapes = [(n, w) for w in (MIX_A, KV_A, KV_A, MIX_B, MIX_B, MIX_B)]
    shapes += [(n // dil, dil * MIX_B) for _ in range(3) for dil in DILS]
    full = lambda shape: pl.BlockSpec(shape, lambda i: (0,) * len(shape))
    return pl.pallas_call(
        _inproj_kernel,
        grid=(n // tt,),
        in_specs=[pl.BlockSpec((tt, d), lambda i: (i, 0)),
                  pl.BlockSpec((None, 6, d), lambda i: (i // tiles_per_seq, 0, 0)),
                  pl.BlockSpec((tt, 1), lambda i: (i, 0)),
                  full((1, d)), full((d, IN_WIDTH)), full((1, IN_WIDTH)),
                  full((1, LANES)), full((LANES, LANES))],
        out_specs=[pl.BlockSpec((tt * r // n, w), lambda i: (i, 0)) for r, w in shapes],
        out_shape=[jax.ShapeDtypeStruct(s, BF16) for s in shapes],
        scratch_shapes=[pltpu.VMEM((N_PAIRS, tt, LANES), F32)] * 3,
        compiler_params=_params("parallel"),
        name="inproj",
    )(xf, mod, pos, g_norm, w_in_p, gcol, invf, bd)


def _attn_kernel(*refs, kv_shared, max_dist, n_qblk, n_res, use_prev, has_sinks, want_lse):
    refs = list(refs)
    sink_ref = refs.pop(0) if has_sinks else None
    q_ref = refs.pop(0)
    kp_ref = refs.pop(0) if use_prev else None
    kc_ref = refs.pop(0)
    vp_ref = refs.pop(0) if use_prev else None
    vc_ref = refs.pop(0)
    o_ref = refs.pop(0)
    lse_ref = refs.pop(0) if want_lse else None

    first_step = pl.program_id(2) == 0
    nq = 2 * BLOCK
    kw = LANES if kv_shared else MIX_B

    def band(nk):
        qpos = lax.broadcasted_iota(I32, (nq, nk), 0) % BLOCK
        kpos = lax.broadcasted_iota(I32, (nq, nk), 1)
        dist = qpos + (nk - BLOCK) - kpos
        return (dist >= 0) & (dist <= max_dist), kpos

    band2, kpos2 = band(2 * BLOCK)
    band1, _ = band(BLOCK)
    lane = lax.broadcasted_iota(I32, (nq, LANES), 1)
    row = lax.broadcasted_iota(I32, (nq, LANES), 0)
    own_half = (lane < HEAD_DIM) == (row < BLOCK)
    left_lanes = lax.broadcasted_iota(I32, (BLOCK, LANES), 1) < HEAD_DIM
    lane8 = lax.broadcasted_iota(I32, (BLOCK, 2 * N_PAIRS), 1)

    for res in range(n_res):
        for qb in range(n_qblk):
            rows = slice(qb * BLOCK, (qb + 1) * BLOCK)
            lse_blk = jnp.zeros((BLOCK, 2 * N_PAIRS), F32)
            for p in range(N_PAIRS):
                cq = slice(res * MIX_B + p * LANES, res * MIX_B + (p + 1) * LANES)
                ck = slice(res * kw, res * kw + LANES) if kv_shared else slice(
                    res * kw + p * LANES, res * kw + (p + 1) * LANES)
                qp = q_ref[rows, cq]
                qs = jnp.concatenate([qp, qp], axis=0)
                qs = jnp.where(own_half, qs, jnp.zeros_like(qs))
                if qb > 0:
                    keys = slice((qb - 1) * BLOCK, (qb + 1) * BLOCK)
                    k, v, valid = kc_ref[keys, ck], vc_ref[keys, ck], band2
                elif use_prev:
                    k = jnp.concatenate([kp_ref[:, ck], kc_ref[rows, ck]], axis=0)
                    v = jnp.concatenate([vp_ref[:, ck], vc_ref[rows, ck]], axis=0)
                    valid = band2 & ((kpos2 >= BLOCK) | jnp.logical_not(first_step))
                else:
                    k, v, valid = kc_ref[rows, ck], vc_ref[rows, ck], band1
                s = lax.dot_general(qs, k, (((1,), (1,)), ((), ())), preferred_element_type=F32)
                s = jnp.where(valid, s, NEG)
                m = jnp.max(s, axis=-1, keepdims=True)
                if has_sinks:
                    rows1 = lax.broadcasted_iota(I32, (nq, 1), 0)
                    sink = jnp.where(rows1 < BLOCK, sink_ref[2 * p], sink_ref[2 * p + 1])
                    m = jnp.maximum(m, sink)
                e = jnp.exp(s - m)
                l = jnp.sum(e, axis=-1, keepdims=True)
                if has_sinks:
                    l = l + jnp.exp(sink - m)
                o = jnp.dot(e.astype(BF16), v, preferred_element_type=F32) / l
                o_ref[rows, cq] = jnp.where(left_lanes, o[:BLOCK], o[BLOCK:]).astype(BF16)
                if want_lse:
                    lse = m + jnp.log(l)
                    lse_blk = (lse_blk + jnp.where(lane8 == 2 * p, lse[:BLOCK], 0.0)
                               + jnp.where(lane8 == 2 * p + 1, lse[BLOCK:], 0.0))
            if want_lse:
                lse_ref[res, rows, :] = lse_blk


def _attention(q, k, v, *, nbatch, seq, dil, max_dist, kv_shared, sinks=None, want_lse):
    length = seq // dil
    nblk = length // BLOCK
    n_qblk = min(nblk, ATTN_BLOCKS_PER_STEP)
    n_res = min(dil, ATTN_BLOCKS_PER_STEP // n_qblk)
    steps = nblk // n_qblk
    use_prev = steps > 1
    kw = k.shape[1] // dil
    view = lambda t: t.reshape(nbatch, length, t.shape[1])
    cur = lambda b, r, i: (b, i, r)
    prev = lambda b, r, i: (b, jnp.maximum(i * n_qblk - 1, 0), r)
    in_specs, args = [], []
    if sinks is not None:
        in_specs.append(pl.BlockSpec(memory_space=pltpu.SMEM))
        args.append(sinks)
    in_specs.append(pl.BlockSpec((None, n_qblk * BLOCK, n_res * MIX_B), cur))
    args.append(view(q))
    for t in (k, v):
        if use_prev:
            in_specs.append(pl.BlockSpec((None, BLOCK, n_res * kw), prev))
            args.append(view(t))
        in_specs.append(pl.BlockSpec((None, n_qblk * BLOCK, n_res * kw), cur))
        args.append(view(t))
    out_specs = [pl.BlockSpec((None, n_qblk * BLOCK, n_res * MIX_B), cur)]
    out_shape = [jax.ShapeDtypeStruct((nbatch, length, dil * MIX_B), BF16)]
    if want_lse:
        out_specs.append(pl.BlockSpec((None, n_res, n_qblk * BLOCK, N_HEADS_B),
                                      lambda b, r, i: (b, r, i, 0)))
        out_shape.append(jax.ShapeDtypeStruct((nbatch, dil, length, N_HEADS_B), F32))
    outs = pl.pallas_call(
        functools.partial(_attn_kernel, kv_shared=kv_shared, max_dist=max_dist, n_qblk=n_qblk,
                          n_res=n_res, use_prev=use_prev, has_sinks=sinks is not None,
                          want_lse=want_lse),
        grid=(nbatch, dil // n_res, steps),
        in_specs=in_specs, out_specs=out_specs, out_shape=out_shape,
        compiler_params=_params("parallel", "parallel", "arbitrary"),
        name=f"attn_d{dil}" + ("_swa" if kv_shared else ""),
    )(*args)
    o = outs[0].reshape(nbatch * length, dil * MIX_B)
    if not want_lse:
        return o, None
    lse = outs[1].transpose(0, 2, 1, 3).reshape(nbatch * seq, N_HEADS_B)
    return o, lse


def _expand_heads(w, width):
    head = lax.broadcasted_iota(I32, (1, width), 1) // HEAD_DIM
    out = jnp.zeros((w.shape[0], width), F32)
    for hd in range(w.shape[1]):
        out = jnp.where(head == hd, w[:, hd:hd + 1], out)
    return out


def _outproj_kernel(x_ref, mod_ref, oa_ref, ob1_ref, ob2_ref, ob3_ref, l1_ref, l2_ref, l3_ref,
                    goa_ref, gob_ref, wo_ref, gf_ref, wgs_ref, wus_ref, wds_ref,
                    h2_ref, h2t_ref, base_ref, ob_scr):
    tt = x_ref.shape[0]
    gate_a = mod_ref[2:3, :]
    shift_m, scale_m, gate_m = mod_ref[3:4, :], mod_ref[4:5, :], mod_ref[5:6, :]

    def token_major(ref, dil):
        if dil == 1:
            return ref[...].astype(F32)
        for r in range(dil):
            for j in range(N_PAIRS):
                c = r * MIX_B + j * LANES
                ob_scr[j, pl.ds(r, tt // dil, stride=dil), :] = ref[:, c:c + LANES].astype(F32)
        return jnp.concatenate([ob_scr[j] for j in range(N_PAIRS)], axis=1)

    l1, l2, l3 = l1_ref[...], l2_ref[...], l3_ref[...]
    mx = jnp.maximum(jnp.maximum(l1, l2), l3)
    e1, e2, e3 = jnp.exp(l1 - mx), jnp.exp(l2 - mx), jnp.exp(l3 - mx)
    den = e1 + e2 + e3
    dils = [dil for _, dil in DILATED_BRANCHES]
    ob = _expand_heads(e1 / den, MIX_B) * token_major(ob1_ref, dils[0])
    ob = ob + _expand_heads(e2 / den, MIX_B) * token_major(ob2_ref, dils[1])
    ob = ob + _expand_heads(e3 / den, MIX_B) * token_major(ob3_ref, dils[2])
    ob = _rms_rows(ob) * gob_ref[...]
    oa = _rms_rows(oa_ref[...].astype(F32)) * goa_ref[...]
    y = (jnp.dot(oa.astype(BF16), wo_ref[0:MIX_A, :], preferred_element_type=F32)
         + jnp.dot(ob.astype(BF16), wo_ref[MIX_A:MIX_A + MIX_B, :], preferred_element_type=F32))
    x1 = x_ref[...] + gate_a * y
    h2 = _rms_rows(x1) * gf_ref[...] * (1.0 + scale_m) + shift_m
    h2_ref[...] = h2
    _store_row_tiles(h2t_ref, h2)
    hb = h2.astype(BF16)
    act = (_silu(jnp.dot(hb, wgs_ref[...], preferred_element_type=F32))
           * jnp.dot(hb, wus_ref[...], preferred_element_type=F32))
    shared = jnp.dot(act.astype(BF16), wds_ref[...], preferred_element_type=F32)
    base_ref[...] = x1 + gate_m * shared


def _outproj(xf, mod, oa, obs, lses, goa, gob, wo_p, gf, wgs, wus, wds, seq):
    n, d = xf.shape
    tt = TT_PROJ
    tiles_per_seq = seq // tt
    tile = lambda w: pl.BlockSpec((tt, w), lambda i: (i, 0))
    full = lambda shape: pl.BlockSpec(shape, lambda i: (0,) * len(shape))
    sd = wgs.shape[1]
    dilated = [pl.BlockSpec((tt // dil, dil * MIX_B), lambda i: (i, 0)) for _, dil in DILATED_BRANCHES]
    return pl.pallas_call(
        _outproj_kernel,
        grid=(n // tt,),
        in_specs=[tile(d), pl.BlockSpec((None, 6, d), lambda i: (i // tiles_per_seq, 0, 0)),
                  tile(MIX_A), *dilated,
                  tile(N_HEADS_B), tile(N_HEADS_B), tile(N_HEADS_B),
                  full((1, MIX_A)), full((1, MIX_B)), full((MIX_A + MIX_B, d)), full((1, d)),
                  full((d, sd)), full((d, sd)), full((sd, d))],
        out_specs=[tile(d), pl.BlockSpec((tt * d // LANES, LANES), lambda i: (i, 0)), tile(d)],
        out_shape=[jax.ShapeDtypeStruct((n, d), F32), jax.ShapeDtypeStruct((n * d // LANES, LANES), F32),
                   jax.ShapeDtypeStruct((n, d), F32)],
        scratch_shapes=[pltpu.VMEM((N_PAIRS, tt, LANES), F32)],
        compiler_params=_params("parallel"),
        name="outproj",
    )(xf, mod, oa, *obs, *lses, goa, gob, wo_p, gf, wgs, wus, wds)


def _router_kernel(h_ref, wrt_ref, bias_ref, e_ref, g_ref, cnt_ref):
    tt = h_ref.shape[0]
    logits = lax.dot_general(wrt_ref[...], h_ref[...].astype(BF16), (((1,), (1,)), ((), ())),
                             preferred_element_type=F32)
    scores = 1.0 / (1.0 + jnp.exp(-logits))
    biased = scores + bias_ref[...]
    ninf = -jnp.inf

    j32 = lax.broadcasted_iota(I32, (GROUP_SIZE, tt), 0).astype(F32)
    grp = []
    for g in range(N_GROUPS):
        bg = biased[g * GROUP_SIZE:(g + 1) * GROUP_SIZE, :]
        m1 = jnp.max(bg, axis=0, keepdims=True)
        i1 = jnp.min(jnp.where(bg == m1, j32, float(GROUP_SIZE)), axis=0, keepdims=True)
        m2 = jnp.max(jnp.where(j32 == i1, ninf, bg), axis=0, keepdims=True)
        grp.append(m1 + m2)
    grp = jnp.concatenate(grp, axis=0)
    g8 = lax.broadcasted_iota(I32, (N_GROUPS, tt), 0).astype(F32)
    chosen = jnp.zeros((N_GROUPS, tt), F32)
    for _ in range(TOPK_GROUPS):
        gm = jnp.max(grp, axis=0, keepdims=True)
        gi = jnp.min(jnp.where(grp == gm, g8, float(N_GROUPS)), axis=0, keepdims=True)
        hit = g8 == gi
        chosen = jnp.where(hit, 1.0, chosen)
        grp = jnp.where(hit, ninf, grp)
    masked = jnp.concatenate(
        [jnp.where(chosen[g:g + 1, :] > 0.0, biased[g * GROUP_SIZE:(g + 1) * GROUP_SIZE, :], ninf)
         for g in range(N_GROUPS)], axis=0)

    eio = lax.broadcasted_iota(I32, (N_EXPERTS, tt), 0).astype(F32)
    picked = jnp.zeros((N_EXPERTS, tt), F32)
    es, gs = [], []
    for _ in range(TOP_K):
        m = jnp.max(masked, axis=0, keepdims=True)
        idx = jnp.min(jnp.where(masked == m, eio, float(N_EXPERTS)), axis=0, keepdims=True)
        hit = eio == idx
        gs.append(jnp.sum(jnp.where(hit, scores, 0.0), axis=0, keepdims=True))
        es.append(idx)
        picked = jnp.where(hit, 1.0, picked)
        masked = jnp.where(hit, ninf, masked)
    gates = jnp.concatenate(gs, axis=0)
    e_ref[...] = jnp.concatenate(es, axis=0).astype(I32)
    g_ref[...] = gates / jnp.sum(gates, axis=0, keepdims=True) * ROUTED_SCALE

    @pl.when(pl.program_id(0) == 0)
    def _():
        cnt_ref[...] = jnp.zeros_like(cnt_ref)
    cnt_ref[...] += jnp.sum(picked, axis=1, keepdims=True)


def _router(h2, wrt, bias_col):
    n, d = h2.shape
    tt = TT_ROUTE
    return pl.pallas_call(
        _router_kernel,
        grid=(n // tt,),
        in_specs=[pl.BlockSpec((tt, d), lambda i: (i, 0)),
                  pl.BlockSpec((N_EXPERTS, d), lambda i: (0, 0)),
                  pl.BlockSpec((N_EXPERTS, 1), lambda i: (0, 0))],
        out_specs=[pl.BlockSpec((TOP_K, tt), lambda i: (0, i)),
                   pl.BlockSpec((TOP_K, tt), lambda i: (0, i)),
                   pl.BlockSpec((N_EXPERTS, 1), lambda i: (0, 0))],
        out_shape=[jax.ShapeDtypeStruct((TOP_K, n), I32),
                   jax.ShapeDtypeStruct((TOP_K, n), F32),
                   jax.ShapeDtypeStruct((N_EXPERTS, 1), F32)],
        compiler_params=_params("arbitrary"),
        name="router",
    )(h2, wrt, bias_col)


def _rank_kernel(e_ref, pstart_ref, tri_ref, dest_ref, carry_ref):
    tt = e_ref.shape[1]

    @pl.when(pl.program_id(0) == 0)
    def _():
        carry_ref[...] = pstart_ref[...]

    e = e_ref[...]
    eio = lax.broadcasted_iota(I32, (N_EXPERTS, tt), 0)
    mask = jnp.zeros((N_EXPERTS, tt), F32)
    for k in range(TOP_K):
        mask = jnp.where(eio == e[k:k + 1, :], 1.0, mask)
    incl = jnp.dot(mask.astype(BF16), tri_ref[...], preferred_element_type=F32)
    pos = incl - mask + carry_ref[...]
    dest = [jnp.sum(jnp.where(eio == e[k:k + 1, :], pos, 0.0), axis=0, keepdims=True)
            for k in range(TOP_K)]
    dest_ref[...] = jnp.concatenate(dest, axis=0).astype(I32)
    carry_ref[...] += incl[:, tt - 1:tt]


def _rank(top_e_t, pstart_col, tri):
    n = top_e_t.shape[1]
    tt = TT_ROUTE
    return pl.pallas_call(
        _rank_kernel,
        grid=(n // tt,),
        in_specs=[pl.BlockSpec((TOP_K, tt), lambda i: (0, i)),
                  pl.BlockSpec((N_EXPERTS, 1), lambda i: (0, 0)),
                  pl.BlockSpec((tt, tt), lambda i: (0, 0))],
        out_specs=pl.BlockSpec((TOP_K, tt), lambda i: (0, i)),
        out_shape=jax.ShapeDtypeStruct((TOP_K, n), I32),
        scratch_shapes=[pltpu.VMEM((N_EXPERTS, 1), F32)],
        compiler_params=_params("arbitrary"),
        name="rank",
    )(top_e_t, pstart_col, tri)


def _row_copy(src_ref, src_row, dst_ref, dst_row, sem):
    return pltpu.make_async_copy(src_ref.at[pl.ds(src_row, 1)], dst_ref.at[pl.ds(dst_row, 1)], sem)


SC_CORES, SC_SUBCORES, SC_LANES = 2, 16, 16
SC_WORKERS = SC_CORES * SC_SUBCORES
INVERT_CHUNK = 8192


def _invert(dest_flat, n_tokens, rows):
    per = rows // SC_WORKERS
    n_assign = dest_flat.shape[0]
    assert rows % (SC_WORKERS * SC_LANES) == 0 and n_assign % INVERT_CHUNK == 0
    assert n_tokens & (n_tokens - 1) == 0

    @functools.partial(
        pl.kernel, mesh=plsc.VectorSubcoreMesh(core_axis_name="c", subcore_axis_name="s"),
        out_type=jax.ShapeDtypeStruct((rows,), I32),
        scratch_types=[pltpu.VMEM((INVERT_CHUNK,), I32), pltpu.VMEM((per,), I32)],
        compiler_params=pltpu.CompilerParams(needs_layout_passes=False))
    def invert(dest_hbm, out_hbm, staged, local):
        base = (lax.axis_index("s") * SC_CORES + lax.axis_index("c")) * per
        lane = lax.iota(I32, SC_LANES)

        @pl.loop(0, per, step=SC_LANES)
        def _(i):
            local[pl.ds(i, SC_LANES)] = (base + i + lane) & (n_tokens - 1)

        @pl.loop(0, n_assign // INVERT_CHUNK)
        def _(c):
            pltpu.sync_copy(dest_hbm.at[pl.ds(c * INVERT_CHUNK, INVERT_CHUNK)], staged)

            @pl.loop(0, INVERT_CHUNK, step=SC_LANES)
            def _(i):
                rel = staged[pl.ds(i, SC_LANES)] - base
                mine = (rel >= 0) & (rel < per)
                tok = (c * INVERT_CHUNK + i + lane) & (n_tokens - 1)
                plsc.store_scatter(local, [jnp.where(mine, rel, 0)], tok, mask=mine)

        pltpu.sync_copy(local, out_hbm.at[pl.ds(base, per)])

    return invert(dest_flat)


def _experts_kernel(rank_ref, eor_ref, nv_ref, nr_ref, tok_hbm, h_ref, wg_hbm, wu_hbm, wd_hbm,
                    ys_ref, *scratch):
    xbufs = scratch[:GATHER_DEPTH]
    wgf, wuf, wdf, wgb, wub, wdb, tok_s, sem, tok_sem, w_sem = scratch[GATHER_DEPTH:]
    step = pl.program_id(0)
    nv, n_ranks = nv_ref[0], nr_ref[0]
    per = wgb.shape[0] // LANES
    depth, bm = len(xbufs), xbufs[0].shape[0] // per
    ahead = depth - 1
    chunk = tok_s.shape[0] // 2
    cb = chunk // bm
    n_chunks = tok_hbm.shape[0] // chunk
    n_blocks = n_chunks * cb

    def tok_copy(c):
        return pltpu.make_async_copy(
            tok_hbm.at[pl.ds(pl.multiple_of(c * chunk, chunk), chunk)],
            tok_s.at[pl.ds(pl.multiple_of((c % 2) * chunk, chunk), chunk)], tok_sem.at[c % 2])

    def gather(block, ring, unrolled):
        block = jnp.minimum(block, n_blocks - 1)
        base = ((block // cb) % 2) * chunk + (block % cb) * bm
        buf, buf_sem = xbufs[ring], sem.at[ring]

        def issue(i, carry=None):
            _tile_copy(h_ref, tok_s[base + i], buf, i, per, buf_sem).start()
            return carry

        if unrolled:
            for i in range(bm):
                _tile_copy(h_ref, tok_s[base + i], buf, i, per, buf_sem).start(
                    priority=int(i % ROW_QUEUE_SPLIT != 0))
        else:
            lax.fori_loop(0, bm, issue, 0, unroll=8)

    def wait_rows(ring):
        pltpu.make_async_copy(h_ref.at[pl.ds(0, bm * per)], xbufs[ring], sem.at[ring]).wait()

    def weight_copies(r):
        e, s = eor_ref[r], r % WEIGHT_DEPTH
        return [pltpu.make_async_copy(src.at[e], dst.at[s], w_sem.at[s])
                for src, dst in ((wg_hbm, wgf), (wu_hbm, wuf), (wd_hbm, wdf))]

    @pl.when(step == 0)
    def _():
        tok_copy(0).start()
        tok_copy(0).wait()
        tok_copy(1).start()
        for r in range(WEIGHT_DEPTH - 1):
            @pl.when(r < n_ranks)
            def _():
                for cp in weight_copies(r):
                    cp.start()
        for b in range(ahead):
            gather(b, b, unrolled=False)

    def block(j, ring, out_row0):
        first = j + ahead
        c_need = first // cb

        @pl.when(jnp.logical_and(jnp.logical_and(j > 0, first % cb == 0), c_need < n_chunks))
        def _():
            tok_copy(c_need).wait()

            @pl.when(c_need + 1 < n_chunks)
            def _():
                tok_copy(c_need + 1).start()

        jb = jnp.minimum(j, n_blocks - 1)
        rank = rank_ref[jb]
        fresh = jnp.logical_or(j == 0, rank != rank_ref[jnp.maximum(jb - 1, 0)])

        @pl.when(jnp.logical_and(j < nv, fresh))
        def _():
            for cp in weight_copies(rank):
                cp.wait()
            s = rank % WEIGHT_DEPTH
            wgb[...] = wgf[s].astype(BF16)
            wub[...] = wuf[s].astype(BF16)
            wdb[...] = wdf[s].astype(BF16)
            nxt = rank + WEIGHT_DEPTH - 1

            @pl.when(nxt < n_ranks)
            def _():
                for cp in weight_copies(nxt):
                    cp.start()

        @pl.when(j < nv)
        def _():
            wait_rows(ring)
            gather(j + ahead, (ring + ahead) % depth, unrolled=True)
            xb = _load_row_tiles(xbufs[ring], bm, per, BF16)
            act = (_silu(jnp.dot(xb, wgb[...], preferred_element_type=F32))
                   * jnp.dot(xb, wub[...], preferred_element_type=F32))
            _store_row_tiles(ys_ref, jnp.dot(act.astype(BF16), wdb[...], preferred_element_type=F32),
                             out_row0)

        @pl.when(jnp.logical_and(j >= nv, j < nv + ahead))
        def _():
            wait_rows(ring)

    half = BLOCKS_PER_STEP
    for parity in range(depth // half):
        @pl.when(step % (depth // half) == parity)
        def _():
            for h in range(half):
                block(step * half + h, parity * half + h, h * bm * per)


def _experts(block_rank, expert_of_rank, n_valid, n_ranks, row_tok, h2_tiles, wg, wu, wd):
    rows = row_tok.shape[0]
    d = wg.shape[1]
    per = d // LANES
    bm = EXPERT_ROWS
    n_blocks = rows // bm
    assert n_blocks % TOK_CHUNK_BLOCKS == 0 and n_blocks // TOK_CHUNK_BLOCKS >= 2
    assert GATHER_DEPTH - 1 < TOK_CHUNK_BLOCKS
    bps = BLOCKS_PER_STEP
    assert GATHER_DEPTH % bps == 0 and n_blocks % bps == 0
    f = wg.shape[2]
    hbm = pl.BlockSpec(memory_space=pl.ANY)
    return pl.pallas_call(
        _experts_kernel,
        grid_spec=pltpu.PrefetchScalarGridSpec(
            num_scalar_prefetch=4,
            grid=((n_blocks + GATHER_DEPTH) // bps,),
            in_specs=[hbm, hbm, hbm, hbm, hbm],
            out_specs=pl.BlockSpec((bps * bm * per, LANES),
                                   lambda s, rk, eor, nv, nr: (jnp.minimum(s, (nv[0] - 1) // bps), 0)),
            scratch_shapes=[pltpu.VMEM((bm * per, LANES), F32)] * GATHER_DEPTH + [
                            pltpu.VMEM((WEIGHT_DEPTH, d, f), F32), pltpu.VMEM((WEIGHT_DEPTH, d, f), F32),
                            pltpu.VMEM((WEIGHT_DEPTH, f, d), F32),
                            pltpu.VMEM((d, f), BF16), pltpu.VMEM((d, f), BF16),
                            pltpu.VMEM((f, d), BF16),
                            pltpu.SMEM((2 * TOK_CHUNK_BLOCKS * bm,), I32),
                            pltpu.SemaphoreType.DMA((GATHER_DEPTH,)), pltpu.SemaphoreType.DMA((2,)),
                            pltpu.SemaphoreType.DMA((WEIGHT_DEPTH,))]),
        out_shape=jax.ShapeDtypeStruct((rows * per, LANES), F32),
        compiler_params=_params("arbitrary"),
        name="experts",
    )(block_rank, expert_of_rank, n_valid, n_ranks, row_tok, h2_tiles, wg, wu, wd)


def _combine_kernel(dest_ref, gates_ref, base_ref, mod_ref, ys_ref, out_ref, buf, sem):
    tt, d = base_ref.shape
    per = d // LANES
    for t in range(tt):
        for k in range(TOP_K):
            _tile_copy(ys_ref, dest_ref[k, t], buf.at[k], t, per, sem).start(priority=k % 2)
    for k in range(TOP_K):
        pltpu.make_async_copy(ys_ref.at[pl.ds(0, tt * per)], buf.at[k], sem).wait()
    gates = gates_ref[...]
    routed = gates[:, 0:1] * _load_row_tiles(buf.at[0], tt, per, F32)
    for k in range(1, TOP_K):
        routed = routed + gates[:, k:k + 1] * _load_row_tiles(buf.at[k], tt, per, F32)
    out_ref[...] = base_ref[...] + mod_ref[5:6, :] * routed


def _combine(dest_t, gates, base, mod, ys, seq):
    n, d = base.shape
    tt = TT_COMBINE
    tiles_per_seq = seq // tt
    return pl.pallas_call(
        _combine_kernel,
        grid=(n // tt,),
        in_specs=[pl.BlockSpec((TOP_K, tt), lambda i: (0, i), memory_space=pltpu.SMEM),
                  pl.BlockSpec((tt, TOP_K), lambda i: (i, 0)),
                  pl.BlockSpec((tt, d), lambda i: (i, 0)),
                  pl.BlockSpec((None, 6, d), lambda i: (i // tiles_per_seq, 0, 0)),
                  pl.BlockSpec(memory_space=pl.ANY)],
        out_specs=pl.BlockSpec((tt, d), lambda i: (i, 0)),
        out_shape=jax.ShapeDtypeStruct((n, d), F32),
        scratch_shapes=[pltpu.VMEM((TOP_K, tt * d // LANES, LANES), F32), pltpu.SemaphoreType.DMA],
        compiler_params=_params("arbitrary"),
        name="combine",
    )(dest_t, gates, base, mod, ys)


def _layer(x, mod, pos, rope, p):
    nbatch, seq, d = x.shape
    n = nbatch * seq
    xf = x.reshape(n, d)
    invf, bd = rope

    perm = np.concatenate([np.arange(h * HEAD_DIM, (h + 1) * HEAD_DIM) for h in PAIR_ORDER_A])
    w_in = p["w_in"]
    w_in_p = jnp.concatenate([w_in[:, :MIX_A][:, perm], w_in[:, MIX_A:]], axis=1).astype(BF16)
    ones = lambda w: jnp.ones((w,), F32)
    qscale = HEAD_DIM ** -0.5
    gcol = jnp.concatenate([jnp.tile(p["g_q_a"], N_HEADS_A) * qscale, jnp.tile(p["g_k_a"], N_KV_A),
                            ones(KV_A), jnp.tile(p["g_q_b"], N_HEADS_B) * qscale,
                            jnp.tile(p["g_k_b"], N_HEADS_B), ones(MIX_B)]).reshape(1, IN_WIDTH)
    proj = _inproj(xf, mod, pos, p["g_norm_mix"].reshape(1, d), w_in_p, gcol, invf, bd, seq)
    qa, ka, va = proj[:3]
    qkv_b = {1: proj[3:6]}
    for j, dil in enumerate(DILS):
        qkv_b[dil] = [proj[6 + t * len(DILS) + j] for t in range(3)]

    sinks_p = p["sinks_a"][np.array(PAIR_ORDER_A)]
    oa, _ = _attention(qa, ka, va, nbatch=nbatch, seq=seq, dil=1, max_dist=WINDOW_A - 1,
                       kv_shared=True, sinks=sinks_p, want_lse=False)
    obs, lses = [], []
    for window, dil in DILATED_BRANCHES:
        o, lse = _attention(*qkv_b[dil], nbatch=nbatch, seq=seq, dil=dil, max_dist=window // dil,
                            kv_shared=False, want_lse=True)
        obs.append(o)
        lses.append(lse)

    goa = p["g_out_a"][perm].reshape(1, MIX_A)
    w_out = p["w_out"]
    wo_p = jnp.concatenate([w_out[:MIX_A][perm], w_out[MIX_A:]], axis=0).astype(BF16)
    h2, h2_tiles, base = _outproj(xf, mod, oa, obs, lses, goa, p["g_out_b"].reshape(1, MIX_B), wo_p,
                        p["g_norm_ffn"].reshape(1, d), p["w_gate_s"].astype(BF16),
                        p["w_up_s"].astype(BF16), p["w_down_s"].astype(BF16), seq)

    top_e_t, gates_t, counts = _router(h2, p["w_router"].T.astype(BF16),
                                       p["router_bias"].reshape(N_EXPERTS, 1))
    bm = EXPERT_ROWS
    counts = counts.reshape(N_EXPERTS).astype(I32)
    padded = (counts + bm - 1) // bm * bm
    pends = jnp.cumsum(padded)
    pstarts = pends - padded
    rows = n * TOP_K + N_EXPERTS * bm
    n_blocks = rows // bm
    n_valid = (pends[-1] // bm).astype(I32)
    blk = jnp.minimum(jnp.arange(n_blocks, dtype=I32), n_valid - 1)
    block_e = jnp.sum((pends[None, :] <= (blk * bm)[:, None]).astype(I32), axis=1)
    block_e = jnp.minimum(block_e, N_EXPERTS - 1)
    nonempty = counts > 0
    rank_of_e = jnp.cumsum(nonempty.astype(I32)) - 1
    e_ids = jnp.arange(N_EXPERTS, dtype=I32)
    block_rank = jnp.sum(jnp.where(block_e[:, None] == e_ids[None, :], rank_of_e[None, :], 0), axis=1)
    hit = (rank_of_e[None, :] == jnp.arange(n_blocks, dtype=I32)[:, None]) & nonempty[None, :]
    expert_of_rank = jnp.sum(jnp.where(hit, e_ids[None, :], 0), axis=1)
    n_ranks = jnp.sum(nonempty.astype(I32)).reshape(1)

    tri =(np.arange(TT_ROUTE)[:, None] <= np.arange(TT_ROUTE)[None, :])
    dest_t = _rank(top_e_t, pstarts.astype(F32).reshape(N_EXPERTS, 1), jnp.asarray(tri, BF16))
    n_valid = n_valid.reshape(1)
    row_tok = _invert(dest_t.reshape(n * TOP_K), n, rows)
    ys = _experts(block_rank.astype(I32), expert_of_rank.astype(I32), n_valid, n_ranks, row_tok, h2_tiles,
                  p["w_gate_e"], p["w_up_e"], p["w_down_e"])
    out = _combine(dest_t, gates_t.T, base, mod, ys, seq)
    return out.reshape(nbatch, seq, d)


def kernel(x, c, positions, w_ada, b_ada, g_norm_mix, w_in, g_q_a, g_k_a, sinks_a, g_q_b, g_k_b,
           g_out_a, g_out_b, w_out, g_norm_ffn, w_router, router_bias, w_gate_e, w_up_e, w_down_e,
           w_gate_s, w_up_s, w_down_s):
    nbatch, seq, d = x.shape
    depth = w_ada.shape[0]
    params = dict(g_norm_mix=g_norm_mix, w_in=w_in, g_q_a=g_q_a, g_k_a=g_k_a, sinks_a=sinks_a,
                  g_q_b=g_q_b, g_k_b=g_k_b, g_out_a=g_out_a, g_out_b=g_out_b, w_out=w_out,
                  g_norm_ffn=g_norm_ffn, w_router=w_router, router_bias=router_bias,
                  w_gate_e=w_gate_e, w_up_e=w_up_e, w_down_e=w_down_e, w_gate_s=w_gate_s,
                  w_up_s=w_up_s, w_down_s=w_down_s)
    j = np.arange(LANES) % HEAD_DIM
    inv = ROPE_THETA ** (-jnp.arange(0, ROT_DIM, 2, dtype=F32) / ROT_DIM)
    invf = jnp.where(j < ROT_DIM, inv[j % (ROT_DIM // 2)], 0.0).astype(F32).reshape(1, LANES)
    bd = jnp.asarray((np.arange(LANES)[:, None] // HEAD_DIM) == (np.arange(LANES)[None, :] // HEAD_DIM),
                     BF16)
    pos = positions.reshape(nbatch * seq, 1).astype(I32)
    for l in range(depth):
        mod = _adaln(c.astype(F32), w_ada[l], b_ada[l]).reshape(nbatch, 6, d)
        x = _layer(x, mod, pos, (invf, bd), {k: v[l] for k, v in params.items()})
    return x
```

```python
import functools

import numpy as np
import jax
import jax.numpy as jnp
from jax import lax
from jax.experimental import pallas as pl
from jax.experimental.pallas import tpu as pltpu
from jax.experimental.pallas import tpu_sc as plsc

F32 = jnp.float32
BF16 = jnp.bfloat16
I32 = jnp.int32

HEAD_DIM = 64
N_HEADS_A = 8
N_KV_A = 2
WINDOW_A = 128
N_HEADS_B = 8
DILATED_BRANCHES = ((128, 1), (512, 4), (2048, 16))
DILS = tuple(dil for _, dil in DILATED_BRANCHES if dil > 1)
BLOCK = 128
ROT_DIM = HEAD_DIM // 4
ROPE_THETA = 500000.0
MIX_A = N_HEADS_A * HEAD_DIM
KV_A = N_KV_A * HEAD_DIM
MIX_B = N_HEADS_B * HEAD_DIM
N_EXPERTS = 256
TOP_K = 8
N_GROUPS = 8
TOPK_GROUPS = 4
GROUP_SIZE = N_EXPERTS // N_GROUPS
ROUTED_SCALE = 2.5
EPS = 1e-6

LANES = 128
HEADS_PER_VREG = LANES // HEAD_DIM
N_PAIRS = MIX_A // LANES
NEG = -1e30
VMEM_LIMIT = 48 * 1024 * 1024

TT_PROJ = 512
ATTN_BLOCKS_PER_STEP = 8
TT_ROUTE = 512
TT_DISPATCH = 256
TT_COMBINE = 512
EXPERT_ROWS = 256
TOK_CHUNK_BLOCKS = 32
GATHER_DEPTH = 4
WEIGHT_DEPTH = 3
BLOCKS_PER_STEP = 2
SUBLANES = 8
ROW_QUEUE_SPLIT = 2

PAIR_ORDER_A = tuple(h for p in range(N_PAIRS) for h in (p, p + N_HEADS_A // N_KV_A))


def _params(*sem):
    return pltpu.CompilerParams(dimension_semantics=sem, vmem_limit_bytes=VMEM_LIMIT)


def _silu(t):
    return t / (1.0 + jnp.exp(-t))


def _rms_rows(t):
    return t * lax.rsqrt(jnp.mean(t * t, axis=-1, keepdims=True) + EPS)


def _store_row_tiles(ref, value, row0=0):
    rows, d = value.shape
    per = d // LANES
    for s in range(per):
        ref[pl.ds(row0 + s, rows, stride=per), :] = value[:, s * LANES:(s + 1) * LANES]


def _load_row_tiles(ref, rows, per, dtype):
    return jnp.concatenate([ref[pl.ds(s, rows, stride=per), :].astype(dtype) for s in range(per)], axis=1)


def _tile_copy(src_ref, src_row, dst_ref, dst_row, per, sem):
    start = lambda row: row * per if isinstance(row, int) else pl.multiple_of(row * per, per)
    return pltpu.make_async_copy(src_ref.at[pl.ds(start(src_row), per)],
                                 dst_ref.at[pl.ds(start(dst_row), per)], sem)


def _ada_kernel(c_ref, w_ref, b_ref, o_ref):
    cond = _silu(c_ref[...])
    o_ref[...] = jnp.dot(cond.astype(BF16), w_ref[...].astype(BF16),
                         preferred_element_type=F32) + b_ref[...]


def _adaln(c, w_ada, b_ada):
    nb, d = c.shape
    width = w_ada.shape[1]
    tn = 1024
    return pl.pallas_call(
        _ada_kernel,
        grid=(width // tn,),
        in_specs=[pl.BlockSpec((nb, d), lambda j: (0, 0)),
                  pl.BlockSpec((d, tn), lambda j: (0, j)),
                  pl.BlockSpec((1, tn), lambda j: (0, j))],
        out_specs=pl.BlockSpec((nb, tn), lambda j: (0, j)),
        out_shape=jax.ShapeDtypeStruct((nb, width), F32),
        compiler_params=_params("arbitrary"),
        name="adaln",
    )(c, w_ada, b_ada.reshape(1, width))


COL_QA, COL_KA, COL_VA = 0, MIX_A, MIX_A + KV_A
COL_QB = MIX_A + 2 * KV_A
COL_KB, COL_VB = COL_QB + MIX_B, COL_QB + 2 * MIX_B
IN_WIDTH = COL_VB + MIX_B


def _inproj_kernel(x_ref, mod_ref, pos_ref, gn_ref, w_ref, gcol_ref, invf_ref, spread_ref, bd_ref,
                   qa_ref, ka_ref, va_ref, qb_ref, kb_ref, vb_ref, *rest):
    n_dil = len(DILS)
    dil_refs = [rest[i * n_dil:(i + 1) * n_dil] for i in range(3)]
    qb_scr, kb_scr, vb_scr = rest[3 * n_dil:]
    tt = x_ref.shape[0]
    shift, scale = mod_ref[0:1, :], mod_ref[1:2, :]
    h = _rms_rows(x_ref[...]) * gn_ref[...] * (1.0 + scale) + shift
    proj = jnp.dot(h.astype(BF16), w_ref[...], preferred_element_type=F32)

    ang = pos_ref[...].astype(F32) * invf_ref[...]
    lane = lax.broadcasted_iota(I32, (1, LANES), 1) % HEAD_DIM

    def spread(table):
        hi = table.astype(BF16)
        lo = (table - hi.astype(F32)).astype(BF16)
        to_lanes = lambda a: lax.dot_general(a, spread_ref[...], (((0,), (0,)), ((), ())),
                                             preferred_element_type=F32)
        return to_lanes(hi) + to_lanes(lo)

    cs = spread(jnp.cos(ang)) + jnp.where(lane < ROT_DIM, 0.0, 1.0)
    sn = spread(jnp.sin(ang))
    s_lo = jnp.where(lane < ROT_DIM // 2, -sn, 0.0)
    s_hi = jnp.where((lane >= ROT_DIM // 2) & (lane < ROT_DIM), sn, 0.0)
    bd = bd_ref[...]

    def norm_rope(col0, width, out_ref, scr=None):
        for j in range(width // LANES):
            c = col0 + j * LANES
            t = proj[:, c:c + LANES]
            sq = t * t
            hi = sq.astype(BF16)
            lo = (sq - hi.astype(F32)).astype(BF16)
            ss = (jnp.dot(hi, bd, preferred_element_type=F32)
                  + jnp.dot(lo, bd, preferred_element_type=F32))
            t = t * lax.rsqrt(ss * (1.0 / HEAD_DIM) + EPS) * gcol_ref[:, c:c + LANES]
            t = (t * cs + pltpu.roll(t, LANES - ROT_DIM // 2, 1) * s_lo
                 + pltpu.roll(t, ROT_DIM // 2, 1) * s_hi)
            out_ref[:, j * LANES:(j + 1) * LANES] = t.astype(BF16)
            if scr is not None:
                scr[j] = t

    norm_rope(COL_QA, MIX_A, qa_ref)
    norm_rope(COL_KA, KV_A, ka_ref)
    norm_rope(COL_QB, MIX_B, qb_ref, qb_scr)
    norm_rope(COL_KB, MIX_B, kb_ref, kb_scr)
    va_ref[...] = proj[:, COL_VA:COL_VA + KV_A].astype(BF16)
    vb_ref[...] = proj[:, COL_VB:COL_VB + MIX_B].astype(BF16)
    for j in range(N_PAIRS):
        vb_scr[j] = proj[:, COL_VB + j * LANES:COL_VB + (j + 1) * LANES]
    for scr, outs in zip((qb_scr, kb_scr, vb_scr), dil_refs):
        for dil, out in zip(DILS, outs):
            for r in range(dil):
                for j in range(N_PAIRS):
                    c = r * MIX_B + j * LANES
                    out[:, c:c + LANES] = scr[j, pl.ds(r, tt // dil, stride=dil), :].astype(BF16)


def _inproj(xf, mod, pos, g_norm, w_in_p, gcol, invf, spread, bd, seq):
    n, d = xf.shape
    tt = TT_PROJ
    tiles_per_seq = seq // tt
    shapes = [(n, w) for w in (MIX_A, KV_A, KV_A, MIX_B, MIX_B, MIX_B)]
    shapes += [(n // dil, dil * MIX_B) for _ in range(3) for dil in DILS]
    full = lambda shape: pl.BlockSpec(shape, lambda i: (0,) * len(shape))
    return pl.pallas_call(
        _inproj_kernel,
        grid=(n // tt,),
        in_specs=[pl.BlockSpec((tt, d), lambda i: (i, 0)),
                  pl.BlockSpec((None, 6, d), lambda i: (i // tiles_per_seq, 0, 0)),
                  pl.BlockSpec((1, tt), lambda i: (0, i)),
                  full((1, d)), full((d, IN_WIDTH)), full((1, IN_WIDTH)),
                  full((ROT_DIM // 2, 1)), full((ROT_DIM // 2, LANES)), full((LANES, LANES))],
        out_specs=[pl.BlockSpec((tt * r // n, w), lambda i: (i, 0)) for r, w in shapes],
        out_shape=[jax.ShapeDtypeStruct(s, BF16) for s in shapes],
        scratch_shapes=[pltpu.VMEM((N_PAIRS, tt, LANES), F32)] * 3,
        compiler_params=_params("parallel"),
        name="inproj",
    )(xf, mod, pos, g_norm, w_in_p, gcol, invf, spread, bd)


def _attn_kernel(*refs, kv_shared, max_dist, n_qblk, n_res, use_prev, has_sinks, want_lse):
    refs = list(refs)
    sink_ref = refs.pop(0) if has_sinks else None
    q_ref = refs.pop(0)
    kp_ref = refs.pop(0) if use_prev else None
    kc_ref = refs.pop(0)
    vp_ref = refs.pop(0) if use_prev else None
    vc_ref = refs.pop(0)
    o_ref = refs.pop(0)
    lse_ref = refs.pop(0) if want_lse else None

    first_step = pl.program_id(2) == 0
    nq = 2 * BLOCK
    kw = LANES if kv_shared else MIX_B

    def band(nk):
        qpos = lax.broadcasted_iota(I32, (nq, nk), 0) % BLOCK
        kpos = lax.broadcasted_iota(I32, (nq, nk), 1)
        dist = qpos + (nk - BLOCK) - kpos
        return (dist >= 0) & (dist <= max_dist), kpos

    band2, kpos2 = band(2 * BLOCK)
    band1, _ = band(BLOCK)
    lane = lax.broadcasted_iota(I32, (nq, LANES), 1)
    row = lax.broadcasted_iota(I32, (nq, LANES), 0)
    own_half = (lane < HEAD_DIM) == (row < BLOCK)
    left_lanes = lax.broadcasted_iota(I32, (BLOCK, LANES), 1) < HEAD_DIM
    lane8 = lax.broadcasted_iota(I32, (BLOCK, 2 * N_PAIRS), 1)

    for res in range(n_res):
        for qb in range(n_qblk):
            rows = slice(qb * BLOCK, (qb + 1) * BLOCK)
            lse_blk = jnp.zeros((BLOCK, 2 * N_PAIRS), F32)
            for p in range(N_PAIRS):
                cq = slice(res * MIX_B + p * LANES, res * MIX_B + (p + 1) * LANES)
                ck = slice(res * kw, res * kw + LANES) if kv_shared else slice(
                    res * kw + p * LANES, res * kw + (p + 1) * LANES)
                qp = q_ref[rows, cq]
                qs = jnp.concatenate([qp, qp], axis=0)
                qs = jnp.where(own_half, qs, jnp.zeros_like(qs))
                if qb > 0:
                    keys = slice((qb - 1) * BLOCK, (qb + 1) * BLOCK)
                    k, v, valid = kc_ref[keys, ck], vc_ref[keys, ck], band2
                elif use_prev:
                    k = jnp.concatenate([kp_ref[:, ck], kc_ref[rows, ck]], axis=0)
                    v = jnp.concatenate([vp_ref[:, ck], vc_ref[rows, ck]], axis=0)
                    valid = band2 & ((kpos2 >= BLOCK) | jnp.logical_not(first_step))
                else:
                    k, v, valid = kc_ref[rows, ck], vc_ref[rows, ck], band1
                s = lax.dot_general(qs, k, (((1,), (1,)), ((), ())), preferred_element_type=F32)
                s = jnp.where(valid, s, NEG)
                m = jnp.max(s, axis=-1, keepdims=True)
                if has_sinks:
                    rows1 = lax.broadcasted_iota(I32, (nq, 1), 0)
                    sink = jnp.where(rows1 < BLOCK, sink_ref[2 * p], sink_ref[2 * p + 1])
                    m = jnp.maximum(m, sink)
                e = jnp.exp(s - m)
                l = jnp.sum(e, axis=-1, keepdims=True)
                if has_sinks:
                    l = l + jnp.exp(sink - m)
                o = jnp.dot(e.astype(BF16), v, preferred_element_type=F32) / l
                o_ref[rows, cq] = jnp.where(left_lanes, o[:BLOCK], o[BLOCK:]).astype(BF16)
                if want_lse:
                    lse = m + jnp.log(l)
                    lse_blk = (lse_blk + jnp.where(lane8 == 2 * p, lse[:BLOCK], 0.0)
                               + jnp.where(lane8 == 2 * p + 1, lse[BLOCK:], 0.0))
            if want_lse:
                lse_ref[res, rows, :] = lse_blk


def _attention(q, k, v, *, nbatch, seq, dil, max_dist, kv_shared, sinks=None, want_lse):
    length = seq // dil
    nblk = length // BLOCK
    n_qblk = min(nblk, ATTN_BLOCKS_PER_STEP)
    n_res = min(dil, ATTN_BLOCKS_PER_STEP // n_qblk)
    steps = nblk // n_qblk
    use_prev = steps > 1
    kw = k.shape[1] // dil
    view = lambda t: t.reshape(nbatch, length, t.shape[1])
    cur = lambda b, r, i: (b, i, r)
    prev = lambda b, r, i: (b, jnp.maximum(i * n_qblk - 1, 0), r)
    in_specs, args = [], []
    if sinks is not None:
        in_specs.append(pl.BlockSpec(memory_space=pltpu.SMEM))
        args.append(sinks)
    in_specs.append(pl.BlockSpec((None, n_qblk * BLOCK, n_res * MIX_B), cur))
    args.append(view(q))
    for t in (k, v):
        if use_prev:
            in_specs.append(pl.BlockSpec((None, BLOCK, n_res * kw), prev))
            args.append(view(t))
        in_specs.append(pl.BlockSpec((None, n_qblk * BLOCK, n_res * kw), cur))
        args.append(view(t))
    out_specs = [pl.BlockSpec((None, n_qblk * BLOCK, n_res * MIX_B), cur)]
    out_shape = [jax.ShapeDtypeStruct((nbatch, length, dil * MIX_B), BF16)]
    if want_lse:
        out_specs.append(pl.BlockSpec((None, n_res, n_qblk * BLOCK, N_HEADS_B),
                                      lambda b, r, i: (b, r, i, 0)))
        out_shape.append(jax.ShapeDtypeStruct((nbatch, dil, length, N_HEADS_B), F32))
    outs = pl.pallas_call(
        functools.partial(_attn_kernel, kv_shared=kv_shared, max_dist=max_dist, n_qblk=n_qblk,
                          n_res=n_res, use_prev=use_prev, has_sinks=sinks is not None,
                          want_lse=want_lse),
        grid=(nbatch, dil // n_res, steps),
        in_specs=in_specs, out_specs=out_specs, out_shape=out_shape,
        compiler_params=_params("parallel", "parallel", "arbitrary"),
        name=f"attn_d{dil}" + ("_swa" if kv_shared else ""),
    )(*args)
    o = outs[0].reshape(nbatch * length, dil * MIX_B)
    if not want_lse:
        return o, None
    lse = outs[1].transpose(0, 2, 1, 3).reshape(nbatch * seq, N_HEADS_B)
    return o, lse


def _expand_heads(w, width):
    head = lax.broadcasted_iota(I32, (1, width), 1) // HEAD_DIM
    out = jnp.zeros((w.shape[0], width), F32)
    for hd in range(w.shape[1]):
        out = jnp.where(head == hd, w[:, hd:hd + 1], out)
    return out


def _outproj_kernel(x_ref, mod_ref, oa_ref, ob1_ref, ob2_ref, ob3_ref, l1_ref, l2_ref, l3_ref,
                    goa_ref, gob_ref, wo_ref, gf_ref, wgs_ref, wus_ref, wds_ref,
                    h2_ref, h2t_ref, base_ref, ob_scr):
    tt = x_ref.shape[0]
    gate_a = mod_ref[2:3, :]
    shift_m, scale_m, gate_m = mod_ref[3:4, :], mod_ref[4:5, :], mod_ref[5:6, :]

    def token_major(ref, dil):
        if dil == 1:
            return ref[...].astype(F32)
        for r in range(dil):
            for j in range(N_PAIRS):
                c = r * MIX_B + j * LANES
                ob_scr[j, pl.ds(r, tt // dil, stride=dil), :] = ref[:, c:c + LANES].astype(F32)
        return jnp.concatenate([ob_scr[j] for j in range(N_PAIRS)], axis=1)

    l1, l2, l3 = l1_ref[...], l2_ref[...], l3_ref[...]
    mx = jnp.maximum(jnp.maximum(l1, l2), l3)
    e1, e2, e3 = jnp.exp(l1 - mx), jnp.exp(l2 - mx), jnp.exp(l3 - mx)
    den = e1 + e2 + e3
    dils = [dil for _, dil in DILATED_BRANCHES]
    ob = _expand_heads(e1 / den, MIX_B) * token_major(ob1_ref, dils[0])
    ob = ob + _expand_heads(e2 / den, MIX_B) * token_major(ob2_ref, dils[1])
    ob = ob + _expand_heads(e3 / den, MIX_B) * token_major(ob3_ref, dils[2])
    ob = _rms_rows(ob) * gob_ref[...]
    oa = _rms_rows(oa_ref[...].astype(F32)) * goa_ref[...]
    y = (jnp.dot(oa.astype(BF16), wo_ref[0:MIX_A, :], preferred_element_type=F32)
         + jnp.dot(ob.astype(BF16), wo_ref[MIX_A:MIX_A + MIX_B, :], preferred_element_type=F32))
    x1 = x_ref[...] + gate_a * y
    h2 = _rms_rows(x1) * gf_ref[...] * (1.0 + scale_m) + shift_m
    h2_ref[...] = h2
    _store_row_tiles(h2t_ref, h2)
    hb = h2.astype(BF16)
    act = (_silu(jnp.dot(hb, wgs_ref[...], preferred_element_type=F32))
           * jnp.dot(hb, wus_ref[...], preferred_element_type=F32))
    shared = jnp.dot(act.astype(BF16), wds_ref[...], preferred_element_type=F32)
    base_ref[...] = x1 + gate_m * shared


def _outproj(xf, mod, oa, obs, lses, goa, gob, wo_p, gf, wgs, wus, wds, seq):
    n, d = xf.shape
    tt = TT_PROJ
    tiles_per_seq = seq // tt
    tile = lambda w: pl.BlockSpec((tt, w), lambda i: (i, 0))
    full = lambda shape: pl.BlockSpec(shape, lambda i: (0,) * len(shape))
    sd = wgs.shape[1]
    dilated = [pl.BlockSpec((tt // dil, dil * MIX_B), lambda i: (i, 0)) for _, dil in DILATED_BRANCHES]
    return pl.pallas_call(
        _outproj_kernel,
        grid=(n // tt,),
        in_specs=[tile(d), pl.BlockSpec((None, 6, d), lambda i: (i // tiles_per_seq, 0, 0)),
                  tile(MIX_A), *dilated,
                  tile(N_HEADS_B), tile(N_HEADS_B), tile(N_HEADS_B),
                  full((1, MIX_A)), full((1, MIX_B)), full((MIX_A + MIX_B, d)), full((1, d)),
                  full((d, sd)), full((d, sd)), full((sd, d))],
        out_specs=[tile(d), pl.BlockSpec((tt * d // LANES, LANES), lambda i: (i, 0)), tile(d)],
        out_shape=[jax.ShapeDtypeStruct((n, d), F32), jax.ShapeDtypeStruct((n * d // LANES, LANES), F32),
                   jax.ShapeDtypeStruct((n, d), F32)],
        scratch_shapes=[pltpu.VMEM((N_PAIRS, tt, LANES), F32)],
        compiler_params=_params("parallel"),
        name="outproj",
    )(xf, mod, oa, *obs, *lses, goa, gob, wo_p, gf, wgs, wus, wds)


def _router_kernel(h_ref, wrt_ref, bias_ref, e_ref, g_ref, cnt_ref):
    tt = h_ref.shape[0]
    logits = lax.dot_general(wrt_ref[...], h_ref[...].astype(BF16), (((1,), (1,)), ((), ())),
                             preferred_element_type=F32)
    scores = 1.0 / (1.0 + jnp.exp(-logits))
    biased = scores + bias_ref[...]
    ninf = -jnp.inf

    j32 = lax.broadcasted_iota(I32, (GROUP_SIZE, tt), 0).astype(F32)
    grp = []
    for g in range(N_GROUPS):
        bg = biased[g * GROUP_SIZE:(g + 1) * GROUP_SIZE, :]
        m1 = jnp.max(bg, axis=0, keepdims=True)
        i1 = jnp.min(jnp.where(bg == m1, j32, float(GROUP_SIZE)), axis=0, keepdims=True)
        m2 = jnp.max(jnp.where(j32 == i1, ninf, bg), axis=0, keepdims=True)
        grp.append(m1 + m2)
    grp = jnp.concatenate(grp, axis=0)
    g8 = lax.broadcasted_iota(I32, (N_GROUPS, tt), 0).astype(F32)
    chosen = jnp.zeros((N_GROUPS, tt), F32)
    for _ in range(TOPK_GROUPS):
        gm = jnp.max(grp, axis=0, keepdims=True)
        gi = jnp.min(jnp.where(grp == gm, g8, float(N_GROUPS)), axis=0, keepdims=True)
        hit = g8 == gi
        chosen = jnp.where(hit, 1.0, chosen)
        grp = jnp.where(hit, ninf, grp)
    masked = jnp.concatenate(
        [jnp.where(chosen[g:g + 1, :] > 0.0, biased[g * GROUP_SIZE:(g + 1) * GROUP_SIZE, :], ninf)
         for g in range(N_GROUPS)], axis=0)

    eio = lax.broadcasted_iota(I32, (N_EXPERTS, tt), 0).astype(F32)
    picked = jnp.zeros((N_EXPERTS, tt), F32)
    es, gs = [], []
    for _ in range(TOP_K):
        m = jnp.max(masked, axis=0, keepdims=True)
        idx = jnp.min(jnp.where(masked == m, eio, float(N_EXPERTS)), axis=0, keepdims=True)
        hit = eio == idx
        gs.append(jnp.sum(jnp.where(hit, scores, 0.0), axis=0, keepdims=True))
        es.append(idx)
        picked = jnp.where(hit, 1.0, picked)
        masked = jnp.where(hit, ninf, masked)
    gates = jnp.concatenate(gs, axis=0)
    e_ref[...] = jnp.concatenate(es, axis=0).astype(I32)
    g_ref[...] = gates / jnp.sum(gates, axis=0, keepdims=True) * ROUTED_SCALE

    @pl.when(pl.program_id(0) == 0)
    def _():
        cnt_ref[...] = jnp.zeros_like(cnt_ref)
    cnt_ref[...] += jnp.sum(picked, axis=1, keepdims=True)


def _router(h2, wrt, bias_col):
    n, d = h2.shape
    tt = TT_ROUTE
    return pl.pallas_call(
        _router_kernel,
        grid=(n // tt,),
        in_specs=[pl.BlockSpec((tt, d), lambda i: (i, 0)),
                  pl.BlockSpec((N_EXPERTS, d), lambda i: (0, 0)),
                  pl.BlockSpec((N_EXPERTS, 1), lambda i: (0, 0))],
        out_specs=[pl.BlockSpec((TOP_K, tt), lambda i: (0, i)),
                   pl.BlockSpec((TOP_K, tt), lambda i: (0, i)),
                   pl.BlockSpec((N_EXPERTS, 1), lambda i: (0, 0))],
        out_shape=[jax.ShapeDtypeStruct((TOP_K, n), I32),
                   jax.ShapeDtypeStruct((TOP_K, n), F32),
                   jax.ShapeDtypeStruct((N_EXPERTS, 1), F32)],
        compiler_params=_params("arbitrary"),
        name="router",
    )(h2, wrt, bias_col)


def _rank_kernel(e_ref, pstart_ref, tri_ref, dest_ref, carry_ref):
    tt = e_ref.shape[1]

    @pl.when(pl.program_id(0) == 0)
    def _():
        carry_ref[...] = pstart_ref[...]

    e = e_ref[...]
    eio = lax.broadcasted_iota(I32, (N_EXPERTS, tt), 0)
    mask = jnp.zeros((N_EXPERTS, tt), F32)
    for k in range(TOP_K):
        mask = jnp.where(eio == e[k:k + 1, :], 1.0, mask)
    incl = jnp.dot(mask.astype(BF16), tri_ref[...], preferred_element_type=F32)
    pos = incl - mask + carry_ref[...]
    dest = [jnp.sum(jnp.where(eio == e[k:k + 1, :], pos, 0.0), axis=0, keepdims=True)
            for k in range(TOP_K)]
    dest_ref[...] = jnp.concatenate(dest, axis=0).astype(I32)
    carry_ref[...] += incl[:, tt - 1:tt]


def _rank(top_e_t, pstart_col, tri):
    n = top_e_t.shape[1]
    tt = TT_ROUTE
    return pl.pallas_call(
        _rank_kernel,
        grid=(n // tt,),
        in_specs=[pl.BlockSpec((TOP_K, tt), lambda i: (0, i)),
                  pl.BlockSpec((N_EXPERTS, 1), lambda i: (0, 0)),
                  pl.BlockSpec((tt, tt), lambda i: (0, 0))],
        out_specs=pl.BlockSpec((TOP_K, tt), lambda i: (0, i)),
        out_shape=jax.ShapeDtypeStruct((TOP_K, n), I32),
        scratch_shapes=[pltpu.VMEM((N_EXPERTS, 1), F32)],
        compiler_params=_params("arbitrary"),
        name="rank",
    )(top_e_t, pstart_col, tri)


def _row_copy(src_ref, src_row, dst_ref, dst_row, sem):
    return pltpu.make_async_copy(src_ref.at[pl.ds(src_row, 1)], dst_ref.at[pl.ds(dst_row, 1)], sem)


SC_CORES, SC_SUBCORES, SC_LANES = 2, 16, 16
SC_WORKERS = SC_CORES * SC_SUBCORES
INVERT_CHUNK = 8192


def _invert(dest_flat, n_tokens, rows):
    per = rows // SC_WORKERS
    n_assign = dest_flat.shape[0]
    assert rows % (SC_WORKERS * SC_LANES) == 0 and n_assign % INVERT_CHUNK == 0
    assert n_tokens & (n_tokens - 1) == 0

    @functools.partial(
        pl.kernel, mesh=plsc.VectorSubcoreMesh(core_axis_name="c", subcore_axis_name="s"),
        out_type=jax.ShapeDtypeStruct((rows,), I32),
        scratch_types=[pltpu.VMEM((INVERT_CHUNK,), I32), pltpu.VMEM((per,), I32)],
        compiler_params=pltpu.CompilerParams(needs_layout_passes=False))
    def invert(dest_hbm, out_hbm, staged, local):
        base = (lax.axis_index("s") * SC_CORES + lax.axis_index("c")) * per
        lane = lax.iota(I32, SC_LANES)

        @pl.loop(0, per, step=SC_LANES)
        def _(i):
            local[pl.ds(i, SC_LANES)] = (base + i + lane) & (n_tokens - 1)

        @pl.loop(0, n_assign // INVERT_CHUNK)
        def _(c):
            pltpu.sync_copy(dest_hbm.at[pl.ds(c * INVERT_CHUNK, INVERT_CHUNK)], staged)

            @pl.loop(0, INVERT_CHUNK, step=SC_LANES)
            def _(i):
                rel = staged[pl.ds(i, SC_LANES)] - base
                mine = (rel >= 0) & (rel < per)
                tok = (c * INVERT_CHUNK + i + lane) & (n_tokens - 1)
                plsc.store_scatter(local, [jnp.where(mine, rel, 0)], tok, mask=mine)

        pltpu.sync_copy(local, out_hbm.at[pl.ds(base, per)])

    return invert(dest_flat)


def _experts_kernel(rank_ref, eor_ref, nv_ref, nr_ref, tok_hbm, h_ref, wg_hbm, wu_hbm, wd_hbm,
                    ys_ref, *scratch):
    xbufs = scratch[:GATHER_DEPTH]
    wgf, wuf, wdf, wgb, wub, wdb, tok_s, sem, tok_sem, w_sem = scratch[GATHER_DEPTH:]
    step = pl.program_id(0)
    nv, n_ranks = nv_ref[0], nr_ref[0]
    per = wgb.shape[0] // LANES
    depth, bm = len(xbufs), xbufs[0].shape[0] // per
    ahead = depth - 1
    chunk = tok_s.shape[0] // 2
    cb = chunk // bm
    n_chunks = tok_hbm.shape[0] // chunk
    n_blocks = n_chunks * cb

    def tok_copy(c):
        return pltpu.make_async_copy(
            tok_hbm.at[pl.ds(pl.multiple_of(c * chunk, chunk), chunk)],
            tok_s.at[pl.ds(pl.multiple_of((c % 2) * chunk, chunk), chunk)], tok_sem.at[c % 2])

    def gather(block, ring, unrolled):
        block = jnp.minimum(block, n_blocks - 1)
        base = ((block // cb) % 2) * chunk + (block % cb) * bm
        buf, buf_sem = xbufs[ring], sem.at[ring]

        def issue(i, carry=None):
            _tile_copy(h_ref, tok_s[base + i], buf, i, per, buf_sem).start()
            return carry

        if unrolled:
            for i in range(bm):
                _tile_copy(h_ref, tok_s[base + i], buf, i, per, buf_sem).start(
                    priority=int(i % ROW_QUEUE_SPLIT != 0))
        else:
            lax.fori_loop(0, bm, issue, 0, unroll=8)

    def wait_rows(ring):
        pltpu.make_async_copy(h_ref.at[pl.ds(0, bm * per)], xbufs[ring], sem.at[ring]).wait()

    def weight_copies(r):
        e, s = eor_ref[r], r % WEIGHT_DEPTH
        return [pltpu.make_async_copy(src.at[e], dst.at[s], w_sem.at[s])
                for src, dst in ((wg_hbm, wgf), (wu_hbm, wuf), (wd_hbm, wdf))]

    @pl.when(step == 0)
    def _():
        tok_copy(0).start()
        tok_copy(0).wait()
        tok_copy(1).start()
        for r in range(WEIGHT_DEPTH - 1):
            @pl.when(r < n_ranks)
            def _():
                for cp in weight_copies(r):
                    cp.start()
        for b in range(ahead):
            gather(b, b, unrolled=False)

    def block(j, ring, out_row0):
        first = j + ahead
        c_need = first // cb

        @pl.when(jnp.logical_and(jnp.logical_and(j > 0, first % cb == 0), c_need < n_chunks))
        def _():
            tok_copy(c_need).wait()

            @pl.when(c_need + 1 < n_chunks)
            def _():
                tok_copy(c_need + 1).start()

        jb = jnp.minimum(j, n_blocks - 1)
        rank = rank_ref[jb]
        fresh = jnp.logical_or(j == 0, rank != rank_ref[jnp.maximum(jb - 1, 0)])

        @pl.when(jnp.logical_and(j < nv, fresh))
        def _():
            for cp in weight_copies(rank):
                cp.wait()
            s = rank % WEIGHT_DEPTH
            wgb[...] = wgf[s].astype(BF16)
            wub[...] = wuf[s].astype(BF16)
            wdb[...] = wdf[s].astype(BF16)
            nxt = rank + WEIGHT_DEPTH - 1

            @pl.when(nxt < n_ranks)
            def _():
                for cp in weight_copies(nxt):
                    cp.start()

        @pl.when(j < nv)
        def _():
            wait_rows(ring)
            gather(j + ahead, (ring + ahead) % depth, unrolled=True)
            xb = _load_row_tiles(xbufs[ring], bm, per, BF16)
            act = (_silu(jnp.dot(xb, wgb[...], preferred_element_type=F32))
                   * jnp.dot(xb, wub[...], preferred_element_type=F32))
            _store_row_tiles(ys_ref, jnp.dot(act.astype(BF16), wdb[...], preferred_element_type=F32),
                             out_row0)

        @pl.when(jnp.logical_and(j >= nv, j < nv + ahead))
        def _():
            wait_rows(ring)

    half = BLOCKS_PER_STEP
    for parity in range(depth // half):
        @pl.when(step % (depth // half) == parity)
        def _():
            for h in range(half):
                block(step * half + h, parity * half + h, h * bm * per)


def _experts(block_rank, expert_of_rank, n_valid, n_ranks, row_tok, h2_tiles, wg, wu, wd):
    rows = row_tok.shape[0]
    d = wg.shape[1]
    per = d // LANES
    bm = EXPERT_ROWS
    n_blocks = rows // bm
    assert n_blocks % TOK_CHUNK_BLOCKS == 0 and n_blocks // TOK_CHUNK_BLOCKS >= 2
    assert GATHER_DEPTH - 1 < TOK_CHUNK_BLOCKS
    bps = BLOCKS_PER_STEP
    assert GATHER_DEPTH % bps == 0 and n_blocks % bps == 0
    f = wg.shape[2]
    hbm = pl.BlockSpec(memory_space=pl.ANY)
    return pl.pallas_call(
        _experts_kernel,
        grid_spec=pltpu.PrefetchScalarGridSpec(
            num_scalar_prefetch=4,
            grid=((n_blocks + GATHER_DEPTH) // bps,),
            in_specs=[hbm, hbm, hbm, hbm, hbm],
            out_specs=pl.BlockSpec((bps * bm * per, LANES),
                                   lambda s, rk, eor, nv, nr: (jnp.minimum(s, (nv[0] - 1) // bps), 0)),
            scratch_shapes=[pltpu.VMEM((bm * per, LANES), F32)] * GATHER_DEPTH + [
                            pltpu.VMEM((WEIGHT_DEPTH, d, f), F32), pltpu.VMEM((WEIGHT_DEPTH, d, f), F32),
                            pltpu.VMEM((WEIGHT_DEPTH, f, d), F32),
                            pltpu.VMEM((d, f), BF16), pltpu.VMEM((d, f), BF16),
                            pltpu.VMEM((f, d), BF16),
                            pltpu.SMEM((2 * TOK_CHUNK_BLOCKS * bm,), I32),
                            pltpu.SemaphoreType.DMA((GATHER_DEPTH,)), pltpu.SemaphoreType.DMA((2,)),
                            pltpu.SemaphoreType.DMA((WEIGHT_DEPTH,))]),
        out_shape=jax.ShapeDtypeStruct((rows * per, LANES), F32),
        compiler_params=_params("arbitrary"),
        name="experts",
    )(block_rank, expert_of_rank, n_valid, n_ranks, row_tok, h2_tiles, wg, wu, wd)


def _combine_kernel(dest_ref, gates_ref, base_ref, mod_ref, ys_ref, out_ref, buf, sem):
    tt, d = base_ref.shape
    per = d // LANES
    for t in range(tt):
        for k in range(TOP_K):
            _tile_copy(ys_ref, dest_ref[k, t], buf.at[k], t, per, sem).start(priority=k % 2)
    for k in range(TOP_K):
        pltpu.make_async_copy(ys_ref.at[pl.ds(0, tt * per)], buf.at[k], sem).wait()
    gates = gates_ref[...]
    routed = gates[:, 0:1] * _load_row_tiles(buf.at[0], tt, per, F32)
    for k in range(1, TOP_K):
        routed = routed + gates[:, k:k + 1] * _load_row_tiles(buf.at[k], tt, per, F32)
    out_ref[...] = base_ref[...] + mod_ref[5:6, :] * routed


def _combine(dest_t, gates, base, mod, ys, seq):
    n, d = base.shape
    tt = TT_COMBINE
    tiles_per_seq = seq // tt
    return pl.pallas_call(
        _combine_kernel,
        grid=(n // tt,),
        in_specs=[pl.BlockSpec((TOP_K, tt), lambda i: (0, i), memory_space=pltpu.SMEM),
                  pl.BlockSpec((tt, TOP_K), lambda i: (i, 0)),
                  pl.BlockSpec((tt, d), lambda i: (i, 0)),
                  pl.BlockSpec((None, 6, d), lambda i: (i // tiles_per_seq, 0, 0)),
                  pl.BlockSpec(memory_space=pl.ANY)],
        out_specs=pl.BlockSpec((tt, d), lambda i: (i, 0)),
        out_shape=jax.ShapeDtypeStruct((n, d), F32),
        scratch_shapes=[pltpu.VMEM((TOP_K, tt * d // LANES, LANES), F32), pltpu.SemaphoreType.DMA],
        compiler_params=_params("arbitrary"),
        name="combine",
    )(dest_t, gates, base, mod, ys)


def _layer(x, mod, pos, rope, p):
    nbatch, seq, d = x.shape
    n = nbatch * seq
    xf = x.reshape(n, d)
    invf, spread, bd = rope

    perm = np.concatenate([np.arange(h * HEAD_DIM, (h + 1) * HEAD_DIM) for h in PAIR_ORDER_A])
    w_in = p["w_in"]
    w_in_p = jnp.concatenate([w_in[:, :MIX_A][:, perm], w_in[:, MIX_A:]], axis=1).astype(BF16)
    ones = lambda w: jnp.ones((w,), F32)
    qscale = HEAD_DIM ** -0.5
    gcol = jnp.concatenate([jnp.tile(p["g_q_a"], N_HEADS_A) * qscale, jnp.tile(p["g_k_a"], N_KV_A),
                            ones(KV_A), jnp.tile(p["g_q_b"], N_HEADS_B) * qscale,
                            jnp.tile(p["g_k_b"], N_HEADS_B), ones(MIX_B)]).reshape(1, IN_WIDTH)
    proj = _inproj(xf, mod, pos, p["g_norm_mix"].reshape(1, d), w_in_p, gcol, invf, spread, bd, seq)
    qa, ka, va = proj[:3]
    qkv_b = {1: proj[3:6]}
    for j, dil in enumerate(DILS):
        qkv_b[dil] = [proj[6 + t * len(DILS) + j] for t in range(3)]

    sinks_p = p["sinks_a"][np.array(PAIR_ORDER_A)]
    oa, _ = _attention(qa, ka, va, nbatch=nbatch, seq=seq, dil=1, max_dist=WINDOW_A - 1,
                       kv_shared=True, sinks=sinks_p, want_lse=False)
    obs, lses = [], []
    for window, dil in DILATED_BRANCHES:
        o, lse = _attention(*qkv_b[dil], nbatch=nbatch, seq=seq, dil=dil, max_dist=window // dil,
                            kv_shared=False, want_lse=True)
        obs.append(o)
        lses.append(lse)

    goa = p["g_out_a"][perm].reshape(1, MIX_A)
    w_out = p["w_out"]
    wo_p = jnp.concatenate([w_out[:MIX_A][perm], w_out[MIX_A:]], axis=0).astype(BF16)
    h2, h2_tiles, base = _outproj(xf, mod, oa, obs, lses, goa, p["g_out_b"].reshape(1, MIX_B), wo_p,
                        p["g_norm_ffn"].reshape(1, d), p["w_gate_s"].astype(BF16),
                        p["w_up_s"].astype(BF16), p["w_down_s"].astype(BF16), seq)

    top_e_t, gates_t, counts = _router(h2, p["w_router"].T.astype(BF16),
                                       p["router_bias"].reshape(N_EXPERTS, 1))
    bm = EXPERT_ROWS
    counts = counts.reshape(N_EXPERTS).astype(I32)
    padded = (counts + bm - 1) // bm * bm
    pends = jnp.cumsum(padded)
    pstarts = pends - padded
    rows = n * TOP_K + N_EXPERTS * bm
    n_blocks = rows // bm
    n_valid = (pends[-1] // bm).astype(I32)
    blk = jnp.minimum(jnp.arange(n_blocks, dtype=I32), n_valid - 1)
    block_e = jnp.sum((pends[None, :] <= (blk * bm)[:, None]).astype(I32), axis=1)
    block_e = jnp.minimum(block_e, N_EXPERTS - 1)
    nonempty = counts > 0
    rank_of_e = jnp.cumsum(nonempty.astype(I32)) - 1
    e_ids = jnp.arange(N_EXPERTS, dtype=I32)
    block_rank = jnp.sum(jnp.where(block_e[:, None] == e_ids[None, :], rank_of_e[None, :], 0), axis=1)
    hit = (rank_of_e[None, :] == jnp.arange(n_blocks, dtype=I32)[:, None]) & nonempty[None, :]
    expert_of_rank = jnp.sum(jnp.where(hit, e_ids[None, :], 0), axis=1)
    n_ranks = jnp.sum(nonempty.astype(I32)).reshape(1)

    tri =(np.arange(TT_ROUTE)[:, None] <= np.arange(TT_ROUTE)[None, :])
    dest_t = _rank(top_e_t, pstarts.astype(F32).reshape(N_EXPERTS, 1), jnp.asarray(tri, BF16))
    n_valid = n_valid.reshape(1)
    row_tok = _invert(dest_t.reshape(n * TOP_K), n, rows)
    ys = _experts(block_rank.astype(I32), expert_of_rank.astype(I32), n_valid, n_ranks, row_tok, h2_tiles,
                  p["w_gate_e"], p["w_up_e"], p["w_down_e"])
    out = _combine(dest_t, gates_t.T, base, mod, ys, seq)
    return out.reshape(nbatch, seq, d)


def kernel(x, c, positions, w_ada, b_ada, g_norm_mix, w_in, g_q_a, g_k_a, sinks_a, g_q_b, g_k_b,
           g_out_a, g_out_b, w_out, g_norm_ffn, w_router, router_bias, w_gate_e, w_up_e, w_down_e,
           w_gate_s, w_up_s, w_down_s):
    nbatch, seq, d = x.shape
    depth = w_ada.shape[0]
    params = dict(g_norm_mix=g_norm_mix, w_in=w_in, g_q_a=g_q_a, g_k_a=g_k_a, sinks_a=sinks_a,
                  g_q_b=g_q_b, g_k_b=g_k_b, g_out_a=g_out_a, g_out_b=g_out_b, w_out=w_out,
                  g_norm_ffn=g_norm_ffn, w_router=w_router, router_bias=router_bias,
                  w_gate_e=w_gate_e, w_up_e=w_up_e, w_down_e=w_down_e, w_gate_s=w_gate_s,
                  w_up_s=w_up_s, w_down_s=w_down_s)
    j = np.arange(LANES) % HEAD_DIM
    invf = (ROPE_THETA ** (-jnp.arange(0, ROT_DIM, 2, dtype=F32) / ROT_DIM)).reshape(ROT_DIM // 2, 1)
    spread = jnp.asarray((j[None, :] < ROT_DIM)
                         & (j[None, :] % (ROT_DIM // 2) == np.arange(ROT_DIM // 2)[:, None]), BF16)
    bd = jnp.asarray((np.arange(LANES)[:, None] // HEAD_DIM) == (np.arange(LANES)[None, :] // HEAD_DIM),
                     BF16)
    pos = positions.reshape(1, nbatch * seq).astype(I32)
    for l in range(depth):
        mod = _adaln(c.astype(F32), w_ada[l], b_ada[l]).reshape(nbatch, 6, d)
        x = _layer(x, mod, pos, (invf, spread, bd), {k: v[l] for k, v in params.items()})
    return x
```

```python
import functools

import numpy as np
import jax
import jax.numpy as jnp
from jax import lax
from jax.experimental import pallas as pl
from jax.experimental.pallas import tpu as pltpu
from jax.experimental.pallas import tpu_sc as plsc

F32 = jnp.float32
BF16 = jnp.bfloat16
I32 = jnp.int32

HEAD_DIM = 64
N_HEADS_A = 8
N_KV_A = 2
WINDOW_A = 128
N_HEADS_B = 8
DILATED_BRANCHES = ((128, 1), (512, 4), (2048, 16))
DILS = tuple(dil for _, dil in DILATED_BRANCHES if dil > 1)
BLOCK = 128
ROT_DIM = HEAD_DIM // 4
ROPE_THETA = 500000.0
MIX_A = N_HEADS_A * HEAD_DIM
KV_A = N_KV_A * HEAD_DIM
MIX_B = N_HEADS_B * HEAD_DIM
N_EXPERTS = 256
TOP_K = 8
N_GROUPS = 8
TOPK_GROUPS = 4
GROUP_SIZE = N_EXPERTS // N_GROUPS
ROUTED_SCALE = 2.5
EPS = 1e-6

LANES = 128
N_PAIRS = MIX_A // LANES
NEG = -1e30
VMEM_LIMIT = 48 * 1024 * 1024

TT_PROJ = 512
ATTN_BLOCKS_PER_STEP = 8
TT_ROUTE = 512
TT_COMBINE = 512
EXPERT_ROWS = 256
TOK_CHUNK_BLOCKS = 32
GATHER_DEPTH = 4
WEIGHT_DEPTH = 3
BLOCKS_PER_STEP = 2
ROW_QUEUE_SPLIT = 2

PAIR_ORDER_A = tuple(h for p in range(N_PAIRS) for h in (p, p + N_HEADS_A // N_KV_A))


def _params(*sem):
    return pltpu.CompilerParams(dimension_semantics=sem, vmem_limit_bytes=VMEM_LIMIT)


def _silu(t):
    return t / (1.0 + jnp.exp(-t))


def _rms_rows(t):
    return t * lax.rsqrt(jnp.mean(t * t, axis=-1, keepdims=True) + EPS)


def _store_row_tiles(ref, value, row0=0):
    rows, d = value.shape
    per = d // LANES
    for s in range(per):
        ref[pl.ds(row0 + s, rows, stride=per), :] = value[:, s * LANES:(s + 1) * LANES]


def _load_row_tiles(ref, rows, per, dtype):
    return jnp.concatenate([ref[pl.ds(s, rows, stride=per), :].astype(dtype) for s in range(per)], axis=1)


def _tile_copy(src_ref, src_row, dst_ref, dst_row, per, sem):
    start = lambda row: row * per if isinstance(row, int) else pl.multiple_of(row * per, per)
    return pltpu.make_async_copy(src_ref.at[pl.ds(start(src_row), per)],
                                 dst_ref.at[pl.ds(start(dst_row), per)], sem)


def _ada_kernel(c_ref, w_ref, b_ref, o_ref):
    cond = _silu(c_ref[...])
    o_ref[...] = jnp.dot(cond.astype(BF16), w_ref[...].astype(BF16),
                         preferred_element_type=F32) + b_ref[...]


def _adaln(c, w_ada, b_ada):
    nb, d = c.shape
    width = w_ada.shape[1]
    tn = 1024
    return pl.pallas_call(
        _ada_kernel,
        grid=(width // tn,),
        in_specs=[pl.BlockSpec((nb, d), lambda j: (0, 0)),
                  pl.BlockSpec((d, tn), lambda j: (0, j)),
                  pl.BlockSpec((1, tn), lambda j: (0, j))],
        out_specs=pl.BlockSpec((nb, tn), lambda j: (0, j)),
        out_shape=jax.ShapeDtypeStruct((nb, width), F32),
        compiler_params=_params("arbitrary"),
        name="adaln",
    )(c, w_ada, b_ada.reshape(1, width))


COL_QA, COL_KA, COL_VA = 0, MIX_A, MIX_A + KV_A
COL_QB = MIX_A + 2 * KV_A
COL_KB, COL_VB = COL_QB + MIX_B, COL_QB + 2 * MIX_B
IN_WIDTH = COL_VB + MIX_B


def _inproj_kernel(x_ref, mod_ref, pos_ref, gn_ref, w_ref, gcol_ref, invf_ref, spread_ref, bd_ref,
                   qa_ref, ka_ref, va_ref, qb_ref, kb_ref, vb_ref, *rest):
    n_dil = len(DILS)
    dil_refs = [rest[i * n_dil:(i + 1) * n_dil] for i in range(3)]
    qb_scr, kb_scr, vb_scr = rest[3 * n_dil:]
    tt = x_ref.shape[0]
    shift, scale = mod_ref[0:1, :], mod_ref[1:2, :]
    h = _rms_rows(x_ref[...]) * gn_ref[...] * (1.0 + scale) + shift
    proj = jnp.dot(h.astype(BF16), w_ref[...], preferred_element_type=F32)

    ang = pos_ref[...].astype(F32) * invf_ref[...]
    lane = lax.broadcasted_iota(I32, (1, LANES), 1) % HEAD_DIM

    def spread(table):
        hi = table.astype(BF16)
        lo = (table - hi.astype(F32)).astype(BF16)
        to_lanes = lambda a: lax.dot_general(a, spread_ref[...], (((0,), (0,)), ((), ())),
                                             preferred_element_type=F32)
        return to_lanes(hi) + to_lanes(lo)

    cs = spread(jnp.cos(ang)) + jnp.where(lane < ROT_DIM, 0.0, 1.0)
    sn = spread(jnp.sin(ang))
    s_lo = jnp.where(lane < ROT_DIM // 2, -sn, 0.0)
    s_hi = jnp.where((lane >= ROT_DIM // 2) & (lane < ROT_DIM), sn, 0.0)
    bd = bd_ref[...]

    def norm_rope(col0, width, out_ref, scr=None):
        for j in range(width // LANES):
            c = col0 + j * LANES
            t = proj[:, c:c + LANES]
            sq = t * t
            hi = sq.astype(BF16)
            lo = (sq - hi.astype(F32)).astype(BF16)
            ss = (jnp.dot(hi, bd, preferred_element_type=F32)
                  + jnp.dot(lo, bd, preferred_element_type=F32))
            t = t * lax.rsqrt(ss * (1.0 / HEAD_DIM) + EPS) * gcol_ref[:, c:c + LANES]
            t = (t * cs + pltpu.roll(t, LANES - ROT_DIM // 2, 1) * s_lo
                 + pltpu.roll(t, ROT_DIM // 2, 1) * s_hi)
            out_ref[:, j * LANES:(j + 1) * LANES] = t.astype(BF16)
            if scr is not None:
                scr[j] = t

    norm_rope(COL_QA, MIX_A, qa_ref)
    norm_rope(COL_KA, KV_A, ka_ref)
    norm_rope(COL_QB, MIX_B, qb_ref, qb_scr)
    norm_rope(COL_KB, MIX_B, kb_ref, kb_scr)
    va_ref[...] = proj[:, COL_VA:COL_VA + KV_A].astype(BF16)
    vb_ref[...] = proj[:, COL_VB:COL_VB + MIX_B].astype(BF16)
    for j in range(N_PAIRS):
        vb_scr[j] = proj[:, COL_VB + j * LANES:COL_VB + (j + 1) * LANES]
    for scr, outs in zip((qb_scr, kb_scr, vb_scr), dil_refs):
        for dil, out in zip(DILS, outs):
            for r in range(dil):
                for j in range(N_PAIRS):
                    c = r * MIX_B + j * LANES
                    out[:, c:c + LANES] = scr[j, pl.ds(r, tt // dil, stride=dil), :].astype(BF16)


def _inproj(xf, mod, pos, g_norm, w_in_p, gcol, invf, spread, bd, seq):
    n, d = xf.shape
    tt = TT_PROJ
    tiles_per_seq = seq // tt
    shapes = [(n, w) for w in (MIX_A, KV_A, KV_A, MIX_B, MIX_B, MIX_B)]
    shapes += [(n // dil, dil * MIX_B) for _ in range(3) for dil in DILS]
    full = lambda shape: pl.BlockSpec(shape, lambda i: (0,) * len(shape))
    return pl.pallas_call(
        _inproj_kernel,
        grid=(n // tt,),
        in_specs=[pl.BlockSpec((tt, d), lambda i: (i, 0)),
                  pl.BlockSpec((None, 6, d), lambda i: (i // tiles_per_seq, 0, 0)),
                  pl.BlockSpec((1, tt), lambda i: (0, i)),
                  full((1, d)), full((d, IN_WIDTH)), full((1, IN_WIDTH)),
                  full((ROT_DIM // 2, 1)), full((ROT_DIM // 2, LANES)), full((LANES, LANES))],
        out_specs=[pl.BlockSpec((tt * r // n, w), lambda i: (i, 0)) for r, w in shapes],
        out_shape=[jax.ShapeDtypeStruct(s, BF16) for s in shapes],
        scratch_shapes=[pltpu.VMEM((N_PAIRS, tt, LANES), F32)] * 3,
        compiler_params=_params("parallel"),
        name="inproj",
    )(xf, mod, pos, g_norm, w_in_p, gcol, invf, spread, bd)


def _attn_kernel(*refs, kv_shared, max_dist, n_qblk, n_res, use_prev, has_sinks, want_lse):
    refs = list(refs)
    sink_ref = refs.pop(0) if has_sinks else None
    q_ref = refs.pop(0)
    kp_ref = refs.pop(0) if use_prev else None
    kc_ref = refs.pop(0)
    vp_ref = refs.pop(0) if use_prev else None
    vc_ref = refs.pop(0)
    o_ref = refs.pop(0)
    lse_ref = refs.pop(0) if want_lse else None

    first_step = pl.program_id(2) == 0
    nq = 2 * BLOCK
    kw = LANES if kv_shared else MIX_B

    def band(nk):
        qpos = lax.broadcasted_iota(I32, (nq, nk), 0) % BLOCK
        kpos = lax.broadcasted_iota(I32, (nq, nk), 1)
        dist = qpos + (nk - BLOCK) - kpos
        return (dist >= 0) & (dist <= max_dist), kpos

    band2, kpos2 = band(2 * BLOCK)
    band1, _ = band(BLOCK)
    lane = lax.broadcasted_iota(I32, (nq, LANES), 1)
    row = lax.broadcasted_iota(I32, (nq, LANES), 0)
    own_half = (lane < HEAD_DIM) == (row < BLOCK)
    left_lanes = lax.broadcasted_iota(I32, (BLOCK, LANES), 1) < HEAD_DIM
    lane8 = lax.broadcasted_iota(I32, (BLOCK, 2 * N_PAIRS), 1)

    for res in range(n_res):
        for qb in range(n_qblk):
            rows = slice(qb * BLOCK, (qb + 1) * BLOCK)
            lse_blk = jnp.zeros((BLOCK, 2 * N_PAIRS), F32)
            for p in range(N_PAIRS):
                cq = slice(res * MIX_B + p * LANES, res * MIX_B + (p + 1) * LANES)
                ck = slice(res * kw, res * kw + LANES) if kv_shared else slice(
                    res * kw + p * LANES, res * kw + (p + 1) * LANES)
                qp = q_ref[rows, cq]
                qs = jnp.concatenate([qp, qp], axis=0)
                qs = jnp.where(own_half, qs, jnp.zeros_like(qs))
                if qb > 0:
                    keys = slice((qb - 1) * BLOCK, (qb + 1) * BLOCK)
                    k, v, valid = kc_ref[keys, ck], vc_ref[keys, ck], band2
                elif use_prev:
                    k = jnp.concatenate([kp_ref[:, ck], kc_ref[rows, ck]], axis=0)
                    v = jnp.concatenate([vp_ref[:, ck], vc_ref[rows, ck]], axis=0)
                    valid = band2 & ((kpos2 >= BLOCK) | jnp.logical_not(first_step))
                else:
                    k, v, valid = kc_ref[rows, ck], vc_ref[rows, ck], band1
                s = lax.dot_general(qs, k, (((1,), (1,)), ((), ())), preferred_element_type=F32)
                s = jnp.where(valid, s, NEG)
                m = jnp.max(s, axis=-1, keepdims=True)
                if has_sinks:
                    rows1 = lax.broadcasted_iota(I32, (nq, 1), 0)
                    sink = jnp.where(rows1 < BLOCK, sink_ref[2 * p], sink_ref[2 * p + 1])
                    m = jnp.maximum(m, sink)
                e = jnp.exp(s - m)
                l = jnp.sum(e, axis=-1, keepdims=True)
                if has_sinks:
                    l = l + jnp.exp(sink - m)
                o = jnp.dot(e.astype(BF16), v, preferred_element_type=F32) / l
                o_ref[rows, cq] = jnp.where(left_lanes, o[:BLOCK], o[BLOCK:]).astype(BF16)
                if want_lse:
                    lse = m + jnp.log(l)
                    lse_blk = (lse_blk + jnp.where(lane8 == 2 * p, lse[:BLOCK], 0.0)
                               + jnp.where(lane8 == 2 * p + 1, lse[BLOCK:], 0.0))
            if want_lse:
                lse_ref[res, rows, :] = lse_blk


def _attention(q, k, v, *, nbatch, seq, dil, max_dist, kv_shared, sinks=None, want_lse):
    length = seq // dil
    nblk = length // BLOCK
    n_qblk = min(nblk, ATTN_BLOCKS_PER_STEP)
    n_res = min(dil, ATTN_BLOCKS_PER_STEP // n_qblk)
    steps = nblk // n_qblk
    use_prev = steps > 1
    kw = k.shape[1] // dil
    view = lambda t: t.reshape(nbatch, length, t.shape[1])
    cur = lambda b, r, i: (b, i, r)
    prev = lambda b, r, i: (b, jnp.maximum(i * n_qblk - 1, 0), r)
    in_specs, args = [], []
    if sinks is not None:
        in_specs.append(pl.BlockSpec(memory_space=pltpu.SMEM))
        args.append(sinks)
    in_specs.append(pl.BlockSpec((None, n_qblk * BLOCK, n_res * MIX_B), cur))
    args.append(view(q))
    for t in (k, v):
        if use_prev:
            in_specs.append(pl.BlockSpec((None, BLOCK, n_res * kw), prev))
            args.append(view(t))
        in_specs.append(pl.BlockSpec((None, n_qblk * BLOCK, n_res * kw), cur))
        args.append(view(t))
    out_specs = [pl.BlockSpec((None, n_qblk * BLOCK, n_res * MIX_B), cur)]
    out_shape = [jax.ShapeDtypeStruct((nbatch, length, dil * MIX_B), BF16)]
    if want_lse:
        out_specs.append(pl.BlockSpec((None, n_res, n_qblk * BLOCK, N_HEADS_B),
                                      lambda b, r, i: (b, r, i, 0)))
        out_shape.append(jax.ShapeDtypeStruct((nbatch, dil, length, N_HEADS_B), F32))
    outs = pl.pallas_call(
        functools.partial(_attn_kernel, kv_shared=kv_shared, max_dist=max_dist, n_qblk=n_qblk,
                          n_res=n_res, use_prev=use_prev, has_sinks=sinks is not None,
                          want_lse=want_lse),
        grid=(nbatch, dil // n_res, steps),
        in_specs=in_specs, out_specs=out_specs, out_shape=out_shape,
        compiler_params=_params("parallel", "parallel", "arbitrary"),
        name=f"attn_d{dil}" + ("_swa" if kv_shared else ""),
    )(*args)
    o = outs[0].reshape(nbatch * length, dil * MIX_B)
    if not want_lse:
        return o, None
    lse = outs[1].transpose(0, 2, 1, 3).reshape(nbatch * seq, N_HEADS_B)
    return o, lse


def _expand_heads(w, width):
    head = lax.broadcasted_iota(I32, (1, width), 1) // HEAD_DIM
    out = jnp.zeros((w.shape[0], width), F32)
    for hd in range(w.shape[1]):
        out = jnp.where(head == hd, w[:, hd:hd + 1], out)
    return out


def _outproj_kernel(x_ref, mod_ref, oa_ref, ob1_ref, ob2_ref, ob3_ref, l1_ref, l2_ref, l3_ref,
                    goa_ref, gob_ref, wo_ref, gf_ref, wgs_ref, wus_ref, wds_ref,
                    h2_ref, h2t_ref, base_ref, ob_scr):
    tt = x_ref.shape[0]
    gate_a = mod_ref[2:3, :]
    shift_m, scale_m, gate_m = mod_ref[3:4, :], mod_ref[4:5, :], mod_ref[5:6, :]

    def token_major(ref, dil):
        if dil == 1:
            return ref[...].astype(F32)
        for r in range(dil):
            for j in range(N_PAIRS):
                c = r * MIX_B + j * LANES
                ob_scr[j, pl.ds(r, tt // dil, stride=dil), :] = ref[:, c:c + LANES].astype(F32)
        return jnp.concatenate([ob_scr[j] for j in range(N_PAIRS)], axis=1)

    l1, l2, l3 = l1_ref[...], l2_ref[...], l3_ref[...]
    mx = jnp.maximum(jnp.maximum(l1, l2), l3)
    e1, e2, e3 = jnp.exp(l1 - mx), jnp.exp(l2 - mx), jnp.exp(l3 - mx)
    den = e1 + e2 + e3
    dils = [dil for _, dil in DILATED_BRANCHES]
    ob = _expand_heads(e1 / den, MIX_B) * token_major(ob1_ref, dils[0])
    ob = ob + _expand_heads(e2 / den, MIX_B) * token_major(ob2_ref, dils[1])
    ob = ob + _expand_heads(e3 / den, MIX_B) * token_major(ob3_ref, dils[2])
    ob = _rms_rows(ob) * gob_ref[...]
    oa = _rms_rows(oa_ref[...].astype(F32)) * goa_ref[...]
    y = (jnp.dot(oa.astype(BF16), wo_ref[0:MIX_A, :], preferred_element_type=F32)
         + jnp.dot(ob.astype(BF16), wo_ref[MIX_A:MIX_A + MIX_B, :], preferred_element_type=F32))
    x1 = x_ref[...] + gate_a * y
    h2 = _rms_rows(x1) * gf_ref[...] * (1.0 + scale_m) + shift_m
    h2_ref[...] = h2
    _store_row_tiles(h2t_ref, h2)
    hb = h2.astype(BF16)
    act = (_silu(jnp.dot(hb, wgs_ref[...], preferred_element_type=F32))
           * jnp.dot(hb, wus_ref[...], preferred_element_type=F32))
    shared = jnp.dot(act.astype(BF16), wds_ref[...], preferred_element_type=F32)
    base_ref[...] = x1 + gate_m * shared


def _outproj(xf, mod, oa, obs, lses, goa, gob, wo_p, gf, wgs, wus, wds, seq):
    n, d = xf.shape
    tt = TT_PROJ
    tiles_per_seq = seq // tt
    tile = lambda w: pl.BlockSpec((tt, w), lambda i: (i, 0))
    full = lambda shape: pl.BlockSpec(shape, lambda i: (0,) * len(shape))
    sd = wgs.shape[1]
    dilated = [pl.BlockSpec((tt // dil, dil * MIX_B), lambda i: (i, 0)) for _, dil in DILATED_BRANCHES]
    return pl.pallas_call(
        _outproj_kernel,
        grid=(n // tt,),
        in_specs=[tile(d), pl.BlockSpec((None, 6, d), lambda i: (i // tiles_per_seq, 0, 0)),
                  tile(MIX_A), *dilated,
                  tile(N_HEADS_B), tile(N_HEADS_B), tile(N_HEADS_B),
                  full((1, MIX_A)), full((1, MIX_B)), full((MIX_A + MIX_B, d)), full((1, d)),
                  full((d, sd)), full((d, sd)), full((sd, d))],
        out_specs=[tile(d), pl.BlockSpec((tt * d // LANES, LANES), lambda i: (i, 0)), tile(d)],
        out_shape=[jax.ShapeDtypeStruct((n, d), F32), jax.ShapeDtypeStruct((n * d // LANES, LANES), F32),
                   jax.ShapeDtypeStruct((n, d), F32)],
        scratch_shapes=[pltpu.VMEM((N_PAIRS, tt, LANES), F32)],
        compiler_params=_params("parallel"),
        name="outproj",
    )(xf, mod, oa, *obs, *lses, goa, gob, wo_p, gf, wgs, wus, wds)


def _router_kernel(h_ref, wrt_ref, bias_ref, e_ref, g_ref, cnt_ref):
    tt = h_ref.shape[0]
    logits = lax.dot_general(wrt_ref[...], h_ref[...].astype(BF16), (((1,), (1,)), ((), ())),
                             preferred_element_type=F32)
    scores = 1.0 / (1.0 + jnp.exp(-logits))
    biased = scores + bias_ref[...]
    ninf = -jnp.inf

    j32 = lax.broadcasted_iota(I32, (GROUP_SIZE, tt), 0).astype(F32)
    grp = []
    for g in range(N_GROUPS):
        bg = biased[g * GROUP_SIZE:(g + 1) * GROUP_SIZE, :]
        m1 = jnp.max(bg, axis=0, keepdims=True)
        i1 = jnp.min(jnp.where(bg == m1, j32, float(GROUP_SIZE)), axis=0, keepdims=True)
        m2 = jnp.max(jnp.where(j32 == i1, ninf, bg), axis=0, keepdims=True)
        grp.append(m1 + m2)
    grp = jnp.concatenate(grp, axis=0)
    g8 = lax.broadcasted_iota(I32, (N_GROUPS, tt), 0).astype(F32)
    chosen = jnp.zeros((N_GROUPS, tt), F32)
    for _ in range(TOPK_GROUPS):
        gm = jnp.max(grp, axis=0, keepdims=True)
        gi = jnp.min(jnp.where(grp == gm, g8, float(N_GROUPS)), axis=0, keepdims=True)
        hit = g8 == gi
        chosen = jnp.where(hit, 1.0, chosen)
        grp = jnp.where(hit, ninf, grp)
    masked = jnp.concatenate(
        [jnp.where(chosen[g:g + 1, :] > 0.0, biased[g * GROUP_SIZE:(g + 1) * GROUP_SIZE, :], ninf)
         for g in range(N_GROUPS)], axis=0)

    eio = lax.broadcasted_iota(I32, (N_EXPERTS, tt), 0).astype(F32)
    picked = jnp.zeros((N_EXPERTS, tt), F32)
    es, gs = [], []
    for _ in range(TOP_K):
        m = jnp.max(masked, axis=0, keepdims=True)
        idx = jnp.min(jnp.where(masked == m, eio, float(N_EXPERTS)), axis=0, keepdims=True)
        hit = eio == idx
        gs.append(jnp.sum(jnp.where(hit, scores, 0.0), axis=0, keepdims=True))
        es.append(idx)
        picked = jnp.where(hit, 1.0, picked)
        masked = jnp.where(hit, ninf, masked)
    gates = jnp.concatenate(gs, axis=0)
    e_ref[...] = jnp.concatenate(es, axis=0).astype(I32)
    g_ref[...] = gates / jnp.sum(gates, axis=0, keepdims=True) * ROUTED_SCALE

    @pl.when(pl.program_id(0) == 0)
    def _():
        cnt_ref[...] = jnp.zeros_like(cnt_ref)
    cnt_ref[...] += jnp.sum(picked, axis=1, keepdims=True)


def _router(h2, wrt, bias_col):
    n, d = h2.shape
    tt = TT_ROUTE
    return pl.pallas_call(
        _router_kernel,
        grid=(n // tt,),
        in_specs=[pl.BlockSpec((tt, d), lambda i: (i, 0)),
                  pl.BlockSpec((N_EXPERTS, d), lambda i: (0, 0)),
                  pl.BlockSpec((N_EXPERTS, 1), lambda i: (0, 0))],
        out_specs=[pl.BlockSpec((TOP_K, tt), lambda i: (0, i)),
                   pl.BlockSpec((TOP_K, tt), lambda i: (0, i)),
                   pl.BlockSpec((N_EXPERTS, 1), lambda i: (0, 0))],
        out_shape=[jax.ShapeDtypeStruct((TOP_K, n), I32),
                   jax.ShapeDtypeStruct((TOP_K, n), F32),
                   jax.ShapeDtypeStruct((N_EXPERTS, 1), F32)],
        compiler_params=_params("arbitrary"),
        name="router",
    )(h2, wrt, bias_col)


def _rank_kernel(e_ref, pstart_ref, tri_ref, dest_ref, carry_ref):
    tt = e_ref.shape[1]

    @pl.when(pl.program_id(0) == 0)
    def _():
        carry_ref[...] = pstart_ref[...]

    e = e_ref[...]
    eio = lax.broadcasted_iota(I32, (N_EXPERTS, tt), 0)
    mask = jnp.zeros((N_EXPERTS, tt), F32)
    for k in range(TOP_K):
        mask = jnp.where(eio == e[k:k + 1, :], 1.0, mask)
    incl = jnp.dot(mask.astype(BF16), tri_ref[...], preferred_element_type=F32)
    pos = incl - mask + carry_ref[...]
    dest = [jnp.sum(jnp.where(eio == e[k:k + 1, :], pos, 0.0), axis=0, keepdims=True)
            for k in range(TOP_K)]
    dest_ref[...] = jnp.concatenate(dest, axis=0).astype(I32)
    carry_ref[...] += incl[:, tt - 1:tt]


def _rank(top_e_t, pstart_col, tri):
    n = top_e_t.shape[1]
    tt = TT_ROUTE
    return pl.pallas_call(
        _rank_kernel,
        grid=(n // tt,),
        in_specs=[pl.BlockSpec((TOP_K, tt), lambda i: (0, i)),
                  pl.BlockSpec((N_EXPERTS, 1), lambda i: (0, 0)),
                  pl.BlockSpec((tt, tt), lambda i: (0, 0))],
        out_specs=pl.BlockSpec((TOP_K, tt), lambda i: (0, i)),
        out_shape=jax.ShapeDtypeStruct((TOP_K, n), I32),
        scratch_shapes=[pltpu.VMEM((N_EXPERTS, 1), F32)],
        compiler_params=_params("arbitrary"),
        name="rank",
    )(top_e_t, pstart_col, tri)


SC_CORES, SC_SUBCORES, SC_LANES = 2, 16, 16
SC_WORKERS = SC_CORES * SC_SUBCORES
INVERT_CHUNK = 8192


def _invert(dest_flat, n_tokens, rows):
    per = rows // SC_WORKERS
    n_assign = dest_flat.shape[0]
    assert rows % (SC_WORKERS * SC_LANES) == 0 and n_assign % INVERT_CHUNK == 0
    assert n_tokens & (n_tokens - 1) == 0

    @functools.partial(
        pl.kernel, mesh=plsc.VectorSubcoreMesh(core_axis_name="c", subcore_axis_name="s"),
        out_type=jax.ShapeDtypeStruct((rows,), I32),
        scratch_types=[pltpu.VMEM((INVERT_CHUNK,), I32), pltpu.VMEM((per,), I32)],
        compiler_params=pltpu.CompilerParams(needs_layout_passes=False))
    def invert(dest_hbm, out_hbm, staged, local):
        base = (lax.axis_index("s") * SC_CORES + lax.axis_index("c")) * per
        lane = lax.iota(I32, SC_LANES)

        @pl.loop(0, per, step=SC_LANES)
        def _(i):
            local[pl.ds(i, SC_LANES)] = (base + i + lane) & (n_tokens - 1)

        @pl.loop(0, n_assign // INVERT_CHUNK)
        def _(c):
            pltpu.sync_copy(dest_hbm.at[pl.ds(c * INVERT_CHUNK, INVERT_CHUNK)], staged)

            @pl.loop(0, INVERT_CHUNK, step=SC_LANES)
            def _(i):
                rel = staged[pl.ds(i, SC_LANES)] - base
                mine = (rel >= 0) & (rel < per)
                tok = (c * INVERT_CHUNK + i + lane) & (n_tokens - 1)
                plsc.store_scatter(local, [jnp.where(mine, rel, 0)], tok, mask=mine)

        pltpu.sync_copy(local, out_hbm.at[pl.ds(base, per)])

    return invert(dest_flat)


def _experts_kernel(rank_ref, eor_ref, nv_ref, nr_ref, tok_hbm, h_ref, wg_hbm, wu_hbm, wd_hbm,
                    ys_ref, *scratch):
    xbufs = scratch[:GATHER_DEPTH]
    wgf, wuf, wdf, wgb, wub, wdb, tok_s, sem, tok_sem, w_sem = scratch[GATHER_DEPTH:]
    step = pl.program_id(0)
    nv, n_ranks = nv_ref[0], nr_ref[0]
    per = wgb.shape[0] // LANES
    depth, bm = len(xbufs), xbufs[0].shape[0] // per
    ahead = depth - 1
    chunk = tok_s.shape[0] // 2
    cb = chunk // bm
    n_chunks = tok_hbm.shape[0] // chunk
    n_blocks = n_chunks * cb

    def tok_copy(c):
        return pltpu.make_async_copy(
            tok_hbm.at[pl.ds(pl.multiple_of(c * chunk, chunk), chunk)],
            tok_s.at[pl.ds(pl.multiple_of((c % 2) * chunk, chunk), chunk)], tok_sem.at[c % 2])

    def gather(block, ring, unrolled):
        block = jnp.minimum(block, n_blocks - 1)
        base = ((block // cb) % 2) * chunk + (block % cb) * bm
        buf, buf_sem = xbufs[ring], sem.at[ring]

        def issue(i, carry=None):
            _tile_copy(h_ref, tok_s[base + i], buf, i, per, buf_sem).start()
            return carry

        if unrolled:
            for i in range(bm):
                _tile_copy(h_ref, tok_s[base + i], buf, i, per, buf_sem).start(
                    priority=int(i % ROW_QUEUE_SPLIT != 0))
        else:
            lax.fori_loop(0, bm, issue, 0, unroll=8)

    def wait_rows(ring):
        pltpu.make_async_copy(h_ref.at[pl.ds(0, bm * per)], xbufs[ring], sem.at[ring]).wait()

    def weight_copies(r):
        e, s = eor_ref[r], r % WEIGHT_DEPTH
        return [pltpu.make_async_copy(src.at[e], dst.at[s], w_sem.at[s])
                for src, dst in ((wg_hbm, wgf), (wu_hbm, wuf), (wd_hbm, wdf))]

    @pl.when(step == 0)
    def _():
        tok_copy(0).start()
        tok_copy(0).wait()
        tok_copy(1).start()
        for r in range(WEIGHT_DEPTH - 1):
            @pl.when(r < n_ranks)
            def _():
                for cp in weight_copies(r):
                    cp.start()
        for b in range(ahead):
            gather(b, b, unrolled=False)

    def block(j, ring, out_row0):
        first = j + ahead
        c_need = first // cb

        @pl.when(jnp.logical_and(jnp.logical_and(j > 0, first % cb == 0), c_need < n_chunks))
        def _():
            tok_copy(c_need).wait()

            @pl.when(c_need + 1 < n_chunks)
            def _():
                tok_copy(c_need + 1).start()

        jb = jnp.minimum(j, n_blocks - 1)
        rank = rank_ref[jb]
        fresh = jnp.logical_or(j == 0, rank != rank_ref[jnp.maximum(jb - 1, 0)])

        @pl.when(jnp.logical_and(j < nv, fresh))
        def _():
            for cp in weight_copies(rank):
                cp.wait()
            s = rank % WEIGHT_DEPTH
            wgb[...] = wgf[s].astype(BF16)
            wub[...] = wuf[s].astype(BF16)
            wdb[...] = wdf[s].astype(BF16)
            nxt = rank + WEIGHT_DEPTH - 1

            @pl.when(nxt < n_ranks)
            def _():
                for cp in weight_copies(nxt):
                    cp.start()

        @pl.when(j < nv)
        def _():
            wait_rows(ring)
            gather(j + ahead, (ring + ahead) % depth, unrolled=True)
            xb = _load_row_tiles(xbufs[ring], bm, per, BF16)
            act = (_silu(jnp.dot(xb, wgb[...], preferred_element_type=F32))
                   * jnp.dot(xb, wub[...], preferred_element_type=F32))
            _store_row_tiles(ys_ref, jnp.dot(act.astype(BF16), wdb[...], preferred_element_type=F32),
                             out_row0)

        @pl.when(jnp.logical_and(j >= nv, j < nv + ahead))
        def _():
            wait_rows(ring)

    half = BLOCKS_PER_STEP
    for parity in range(depth // half):
        @pl.when(step % (depth // half) == parity)
        def _():
            for h in range(half):
                block(step * half + h, parity * half + h, h * bm * per)


def _experts(block_rank, expert_of_rank, n_valid, n_ranks, row_tok, h2_tiles, wg, wu, wd):
    rows = row_tok.shape[0]
    d = wg.shape[1]
    per = d // LANES
    bm = EXPERT_ROWS
    n_blocks = rows // bm
    assert n_blocks % TOK_CHUNK_BLOCKS == 0 and n_blocks // TOK_CHUNK_BLOCKS >= 2
    assert GATHER_DEPTH - 1 < TOK_CHUNK_BLOCKS
    bps = BLOCKS_PER_STEP
    assert GATHER_DEPTH % bps == 0 and n_blocks % bps == 0
    f = wg.shape[2]
    hbm = pl.BlockSpec(memory_space=pl.ANY)
    return pl.pallas_call(
        _experts_kernel,
        grid_spec=pltpu.PrefetchScalarGridSpec(
            num_scalar_prefetch=4,
            grid=((n_blocks + GATHER_DEPTH) // bps,),
            in_specs=[hbm, hbm, hbm, hbm, hbm],
            out_specs=pl.BlockSpec((bps * bm * per, LANES),
                                   lambda s, rk, eor, nv, nr: (jnp.minimum(s, (nv[0] - 1) // bps), 0)),
            scratch_shapes=[pltpu.VMEM((bm * per, LANES), F32)] * GATHER_DEPTH + [
                            pltpu.VMEM((WEIGHT_DEPTH, d, f), F32), pltpu.VMEM((WEIGHT_DEPTH, d, f), F32),
                            pltpu.VMEM((WEIGHT_DEPTH, f, d), F32),
                            pltpu.VMEM((d, f), BF16), pltpu.VMEM((d, f), BF16),
                            pltpu.VMEM((f, d), BF16),
                            pltpu.SMEM((2 * TOK_CHUNK_BLOCKS * bm,), I32),
                            pltpu.SemaphoreType.DMA((GATHER_DEPTH,)), pltpu.SemaphoreType.DMA((2,)),
                            pltpu.SemaphoreType.DMA((WEIGHT_DEPTH,))]),
        out_shape=jax.ShapeDtypeStruct((rows * per, LANES), F32),
        compiler_params=_params("arbitrary"),
        name="experts",
    )(block_rank, expert_of_rank, n_valid, n_ranks, row_tok, h2_tiles, wg, wu, wd)


def _combine_kernel(dest_ref, gates_ref, base_ref, mod_ref, ys_ref, out_ref, buf, sem):
    tt, d = base_ref.shape
    per = d // LANES
    for t in range(tt):
        for k in range(TOP_K):
            _tile_copy(ys_ref, dest_ref[k, t], buf.at[k], t, per, sem).start(priority=k % 2)
    for k in range(TOP_K):
        pltpu.make_async_copy(ys_ref.at[pl.ds(0, tt * per)], buf.at[k], sem).wait()
    gates = gates_ref[...]
    routed = gates[:, 0:1] * _load_row_tiles(buf.at[0], tt, per, F32)
    for k in range(1, TOP_K):
        routed = routed + gates[:, k:k + 1] * _load_row_tiles(buf.at[k], tt, per, F32)
    out_ref[...] = base_ref[...] + mod_ref[5:6, :] * routed


def _combine(dest_t, gates, base, mod, ys, seq):
    n, d = base.shape
    tt = TT_COMBINE
    tiles_per_seq = seq // tt
    return pl.pallas_call(
        _combine_kernel,
        grid=(n // tt,),
        in_specs=[pl.BlockSpec((TOP_K, tt), lambda i: (0, i), memory_space=pltpu.SMEM),
                  pl.BlockSpec((tt, TOP_K), lambda i: (i, 0)),
                  pl.BlockSpec((tt, d), lambda i: (i, 0)),
                  pl.BlockSpec((None, 6, d), lambda i: (i // tiles_per_seq, 0, 0)),
                  pl.BlockSpec(memory_space=pl.ANY)],
        out_specs=pl.BlockSpec((tt, d), lambda i: (i, 0)),
        out_shape=jax.ShapeDtypeStruct((n, d), F32),
        scratch_shapes=[pltpu.VMEM((TOP_K, tt * d // LANES, LANES), F32), pltpu.SemaphoreType.DMA],
        compiler_params=_params("arbitrary"),
        name="combine",
    )(dest_t, gates, base, mod, ys)


def _layer(x, mod, pos, rope, p):
    nbatch, seq, d = x.shape
    n = nbatch * seq
    xf = x.reshape(n, d)
    invf, spread, bd = rope

    perm = np.concatenate([np.arange(h * HEAD_DIM, (h + 1) * HEAD_DIM) for h in PAIR_ORDER_A])
    w_in = p["w_in"]
    w_in_p = jnp.concatenate([w_in[:, :MIX_A][:, perm], w_in[:, MIX_A:]], axis=1).astype(BF16)
    ones = lambda w: jnp.ones((w,), F32)
    qscale = HEAD_DIM ** -0.5
    gcol = jnp.concatenate([jnp.tile(p["g_q_a"], N_HEADS_A) * qscale, jnp.tile(p["g_k_a"], N_KV_A),
                            ones(KV_A), jnp.tile(p["g_q_b"], N_HEADS_B) * qscale,
                            jnp.tile(p["g_k_b"], N_HEADS_B), ones(MIX_B)]).reshape(1, IN_WIDTH)
    proj = _inproj(xf, mod, pos, p["g_norm_mix"].reshape(1, d), w_in_p, gcol, invf, spread, bd, seq)
    qa, ka, va = proj[:3]
    qkv_b = {1: proj[3:6]}
    for j, dil in enumerate(DILS):
        qkv_b[dil] = [proj[6 + t * len(DILS) + j] for t in range(3)]

    sinks_p = p["sinks_a"][np.array(PAIR_ORDER_A)]
    oa, _ = _attention(qa, ka, va, nbatch=nbatch, seq=seq, dil=1, max_dist=WINDOW_A - 1,
                       kv_shared=True, sinks=sinks_p, want_lse=False)
    obs, lses = [], []
    for window, dil in DILATED_BRANCHES:
        o, lse = _attention(*qkv_b[dil], nbatch=nbatch, seq=seq, dil=dil, max_dist=window // dil,
                            kv_shared=False, want_lse=True)
        obs.append(o)
        lses.append(lse)

    goa = p["g_out_a"][perm].reshape(1, MIX_A)
    w_out = p["w_out"]
    wo_p = jnp.concatenate([w_out[:MIX_A][perm], w_out[MIX_A:]], axis=0).astype(BF16)
    h2, h2_tiles, base = _outproj(xf, mod, oa, obs, lses, goa, p["g_out_b"].reshape(1, MIX_B), wo_p,
                        p["g_norm_ffn"].reshape(1, d), p["w_gate_s"].astype(BF16),
                        p["w_up_s"].astype(BF16), p["w_down_s"].astype(BF16), seq)

    top_e_t, gates_t, counts = _router(h2, p["w_router"].T.astype(BF16),
                                       p["router_bias"].reshape(N_EXPERTS, 1))
    bm = EXPERT_ROWS
    counts = counts.reshape(N_EXPERTS).astype(I32)
    padded = (counts + bm - 1) // bm * bm
    pends = jnp.cumsum(padded)
    pstarts = pends - padded
    rows = n * TOP_K + N_EXPERTS * bm
    n_blocks = rows // bm
    n_valid = (pends[-1] // bm).astype(I32)
    blk = jnp.minimum(jnp.arange(n_blocks, dtype=I32), n_valid - 1)
    block_e = jnp.sum((pends[None, :] <= (blk * bm)[:, None]).astype(I32), axis=1)
    block_e = jnp.minimum(block_e, N_EXPERTS - 1)
    nonempty = counts > 0
    rank_of_e = jnp.cumsum(nonempty.astype(I32)) - 1
    e_ids = jnp.arange(N_EXPERTS, dtype=I32)
    block_rank = jnp.sum(jnp.where(block_e[:, None] == e_ids[None, :], rank_of_e[None, :], 0), axis=1)
    hit = (rank_of_e[None, :] == jnp.arange(n_blocks, dtype=I32)[:, None]) & nonempty[None, :]
    expert_of_rank = jnp.sum(jnp.where(hit, e_ids[None, :], 0), axis=1)
    n_ranks = jnp.sum(nonempty.astype(I32)).reshape(1)

    tri =(np.arange(TT_ROUTE)[:, None] <= np.arange(TT_ROUTE)[None, :])
    dest_t = _rank(top_e_t, pstarts.astype(F32).reshape(N_EXPERTS, 1), jnp.asarray(tri, BF16))
    n_valid = n_valid.reshape(1)
    row_tok = _invert(dest_t.reshape(n * TOP_K), n, rows)
    ys = _experts(block_rank.astype(I32), expert_of_rank.astype(I32), n_valid, n_ranks, row_tok, h2_tiles,
                  p["w_gate_e"], p["w_up_e"], p["w_down_e"])
    out = _combine(dest_t, gates_t.T, base, mod, ys, seq)
    return out.reshape(nbatch, seq, d)


def kernel(x, c, positions, w_ada, b_ada, g_norm_mix, w_in, g_q_a, g_k_a, sinks_a, g_q_b, g_k_b,
           g_out_a, g_out_b, w_out, g_norm_ffn, w_router, router_bias, w_gate_e, w_up_e, w_down_e,
           w_gate_s, w_up_s, w_down_s):
    nbatch, seq, d = x.shape
    depth = w_ada.shape[0]
    params = dict(g_norm_mix=g_norm_mix, w_in=w_in, g_q_a=g_q_a, g_k_a=g_k_a, sinks_a=sinks_a,
                  g_q_b=g_q_b, g_k_b=g_k_b, g_out_a=g_out_a, g_out_b=g_out_b, w_out=w_out,
                  g_norm_ffn=g_norm_ffn, w_router=w_router, router_bias=router_bias,
                  w_gate_e=w_gate_e, w_up_e=w_up_e, w_down_e=w_down_e, w_gate_s=w_gate_s,
                  w_up_s=w_up_s, w_down_s=w_down_s)
    j = np.arange(LANES) % HEAD_DIM
    invf = (ROPE_THETA ** (-jnp.arange(0, ROT_DIM, 2, dtype=F32) / ROT_DIM)).reshape(ROT_DIM // 2, 1)
    spread = jnp.asarray((j[None, :] < ROT_DIM)
                         & (j[None, :] % (ROT_DIM // 2) == np.arange(ROT_DIM // 2)[:, None]), BF16)
    bd = jnp.asarray((np.arange(LANES)[:, None] // HEAD_DIM) == (np.arange(LANES)[None, :] // HEAD_DIM),
                     BF16)
    pos = positions.reshape(1, nbatch * seq).astype(I32)
    for l in range(depth):
        mod = _adaln(c.astype(F32), w_ada[l], b_ada[l]).reshape(nbatch, 6, d)
        x = _layer(x, mod, pos, (invf, spread, bd), {k: v[l] for k, v in params.items()})
    return x
```

```python
import functools

import numpy as np
import jax
import jax.numpy as jnp
from jax import lax
from jax.experimental import pallas as pl
from jax.experimental.pallas import tpu as pltpu
from jax.experimental.pallas import tpu_sc as plsc

F32 = jnp.float32
BF16 = jnp.bfloat16
I32 = jnp.int32

HEAD_DIM = 64
N_HEADS_A = 8
N_KV_A = 2
WINDOW_A = 128
N_HEADS_B = 8
DILATED_BRANCHES = ((128, 1), (512, 4), (2048, 16))
DILS = tuple(dil for _, dil in DILATED_BRANCHES if dil > 1)
BLOCK = 128
ROT_DIM = HEAD_DIM // 4
ROPE_THETA = 500000.0
MIX_A = N_HEADS_A * HEAD_DIM
KV_A = N_KV_A * HEAD_DIM
MIX_B = N_HEADS_B * HEAD_DIM
N_EXPERTS = 256
TOP_K = 8
N_GROUPS = 8
TOPK_GROUPS = 4
GROUP_SIZE = N_EXPERTS // N_GROUPS
ROUTED_SCALE = 2.5
EPS = 1e-6

LANES = 128
N_PAIRS = MIX_A // LANES
NEG = -1e30
VMEM_LIMIT = 48 * 1024 * 1024

TT_PROJ = 512
ATTN_BLOCKS_PER_STEP = 8
TT_ROUTE = 512
TT_COMBINE = 512
EXPERT_ROWS = 256
TOK_CHUNK_BLOCKS = 32
GATHER_DEPTH = 4
WEIGHT_DEPTH = 3
BLOCKS_PER_STEP = 2
ROW_QUEUE_SPLIT = 2

PAIR_ORDER_A = tuple(h for p in range(N_PAIRS) for h in (p, p + N_HEADS_A // N_KV_A))


def _params(*sem):
    return pltpu.CompilerParams(dimension_semantics=sem, vmem_limit_bytes=VMEM_LIMIT)


def _silu(t):
    return t / (1.0 + jnp.exp(-t))


def _rms_rows(t):
    return t * lax.rsqrt(jnp.mean(t * t, axis=-1, keepdims=True) + EPS)


def _store_row_tiles(ref, value, row0=0):
    rows, d = value.shape
    per = d // LANES
    for s in range(per):
        ref[pl.ds(row0 + s, rows, stride=per), :] = value[:, s * LANES:(s + 1) * LANES]


def _load_row_tiles(ref, rows, per, dtype):
    return jnp.concatenate([ref[pl.ds(s, rows, stride=per), :].astype(dtype) for s in range(per)], axis=1)


def _tile_copy(src_ref, src_row, dst_ref, dst_row, per, sem):
    start = lambda row: row * per if isinstance(row, int) else pl.multiple_of(row * per, per)
    return pltpu.make_async_copy(src_ref.at[pl.ds(start(src_row), per)],
                                 dst_ref.at[pl.ds(start(dst_row), per)], sem)


def _ada_kernel(c_ref, w_ref, b_ref, o_ref):
    cond = _silu(c_ref[...])
    o_ref[...] = jnp.dot(cond.astype(BF16), w_ref[...].astype(BF16),
                         preferred_element_type=F32) + b_ref[...]


def _adaln(c, w_ada, b_ada):
    nb, d = c.shape
    width = w_ada.shape[1]
    tn = 1024
    return pl.pallas_call(
        _ada_kernel,
        grid=(width // tn,),
        in_specs=[pl.BlockSpec((nb, d), lambda j: (0, 0)),
                  pl.BlockSpec((d, tn), lambda j: (0, j)),
                  pl.BlockSpec((1, tn), lambda j: (0, j))],
        out_specs=pl.BlockSpec((nb, tn), lambda j: (0, j)),
        out_shape=jax.ShapeDtypeStruct((nb, width), F32),
        compiler_params=_params("arbitrary"),
        name="adaln",
    )(c, w_ada, b_ada.reshape(1, width))


COL_QA, COL_KA, COL_VA = 0, MIX_A, MIX_A + KV_A
COL_QB = MIX_A + 2 * KV_A
COL_KB, COL_VB = COL_QB + MIX_B, COL_QB + 2 * MIX_B
IN_WIDTH = COL_VB + MIX_B


def _inproj_kernel(x_ref, mod_ref, pos_ref, gn_ref, w_ref, gcol_ref, invf_ref, spread_ref, bd_ref,
                   qa_ref, ka_ref, va_ref, qb_ref, kb_ref, vb_ref, *rest):
    n_dil = len(DILS)
    dil_refs = [rest[i * n_dil:(i + 1) * n_dil] for i in range(3)]
    qb_scr, kb_scr, vb_scr = rest[3 * n_dil:]
    tt = x_ref.shape[0]
    shift, scale = mod_ref[0:1, :], mod_ref[1:2, :]
    h = _rms_rows(x_ref[...]) * gn_ref[...] * (1.0 + scale) + shift
    proj = jnp.dot(h.astype(BF16), w_ref[...], preferred_element_type=F32)

    ang = pos_ref[...].astype(F32) * invf_ref[...]
    lane = lax.broadcasted_iota(I32, (1, LANES), 1) % HEAD_DIM

    def spread(table):
        hi = table.astype(BF16)
        lo = (table - hi.astype(F32)).astype(BF16)
        to_lanes = lambda a: lax.dot_general(a, spread_ref[...], (((0,), (0,)), ((), ())),
                                             preferred_element_type=F32)
        return to_lanes(hi) + to_lanes(lo)

    cs = spread(jnp.cos(ang)) + jnp.where(lane < ROT_DIM, 0.0, 1.0)
    sn = spread(jnp.sin(ang))
    s_lo = jnp.where(lane < ROT_DIM // 2, -sn, 0.0)
    s_hi = jnp.where((lane >= ROT_DIM // 2) & (lane < ROT_DIM), sn, 0.0)
    bd = bd_ref[...]

    def norm_rope(col0, width, out_ref, scr=None):
        for j in range(width // LANES):
            c = col0 + j * LANES
            t = proj[:, c:c + LANES]
            sq = t * t
            hi = sq.astype(BF16)
            lo = (sq - hi.astype(F32)).astype(BF16)
            ss = (jnp.dot(hi, bd, preferred_element_type=F32)
                  + jnp.dot(lo, bd, preferred_element_type=F32))
            t = t * lax.rsqrt(ss * (1.0 / HEAD_DIM) + EPS) * gcol_ref[:, c:c + LANES]
            t = (t * cs + pltpu.roll(t, LANES - ROT_DIM // 2, 1) * s_lo
                 + pltpu.roll(t, ROT_DIM // 2, 1) * s_hi)
            out_ref[:, j * LANES:(j + 1) * LANES] = t.astype(BF16)
            if scr is not None:
                scr[j] = t

    norm_rope(COL_QA, MIX_A, qa_ref)
    norm_rope(COL_KA, KV_A, ka_ref)
    norm_rope(COL_QB, MIX_B, qb_ref, qb_scr)
    norm_rope(COL_KB, MIX_B, kb_ref, kb_scr)
    va_ref[...] = proj[:, COL_VA:COL_VA + KV_A].astype(BF16)
    vb_ref[...] = proj[:, COL_VB:COL_VB + MIX_B].astype(BF16)
    for j in range(N_PAIRS):
        vb_scr[j] = proj[:, COL_VB + j * LANES:COL_VB + (j + 1) * LANES]
    for scr, outs in zip((qb_scr, kb_scr, vb_scr), dil_refs):
        for dil, out in zip(DILS, outs):
            for r in range(dil):
                for j in range(N_PAIRS):
                    c = r * MIX_B + j * LANES
                    out[:, c:c + LANES] = scr[j, pl.ds(r, tt // dil, stride=dil), :].astype(BF16)


def _inproj(xf, mod, pos, g_norm, w_in_p, gcol, invf, spread, bd, seq):
    n, d = xf.shape
    tt = TT_PROJ
    tiles_per_seq = seq // tt
    shapes = [(n, w) for w in (MIX_A, KV_A, KV_A, MIX_B, MIX_B, MIX_B)]
    shapes += [(n // dil, dil * MIX_B) for _ in range(3) for dil in DILS]
    full = lambda shape: pl.BlockSpec(shape, lambda i: (0,) * len(shape))
    return pl.pallas_call(
        _inproj_kernel,
        grid=(n // tt,),
        in_specs=[pl.BlockSpec((tt, d), lambda i: (i, 0)),
                  pl.BlockSpec((None, 6, d), lambda i: (i // tiles_per_seq, 0, 0)),
                  pl.BlockSpec((1, tt), lambda i: (0, i)),
                  full((1, d)), full((d, IN_WIDTH)), full((1, IN_WIDTH)),
                  full((ROT_DIM // 2, 1)), full((ROT_DIM // 2, LANES)), full((LANES, LANES))],
        out_specs=[pl.BlockSpec((tt * r // n, w), lambda i: (i, 0)) for r, w in shapes],
        out_shape=[jax.ShapeDtypeStruct(s, BF16) for s in shapes],
        scratch_shapes=[pltpu.VMEM((N_PAIRS, tt, LANES), F32)] * 3,
        compiler_params=_params("parallel"),
        name="inproj",
    )(xf, mod, pos, g_norm, w_in_p, gcol, invf, spread, bd)


def _attn_kernel(*refs, kv_shared, max_dist, n_qblk, n_res, use_prev, has_sinks, want_lse):
    refs = list(refs)
    sink_ref = refs.pop(0) if has_sinks else None
    q_ref = refs.pop(0)
    kp_ref = refs.pop(0) if use_prev else None
    kc_ref = refs.pop(0)
    vp_ref = refs.pop(0) if use_prev else None
    vc_ref = refs.pop(0)
    o_ref = refs.pop(0)
    lse_ref = refs.pop(0) if want_lse else None

    first_step = pl.program_id(2) == 0
    nq = 2 * BLOCK
    kw = LANES if kv_shared else MIX_B

    def band(nk):
        qpos = lax.broadcasted_iota(I32, (nq, nk), 0) % BLOCK
        kpos = lax.broadcasted_iota(I32, (nq, nk), 1)
        dist = qpos + (nk - BLOCK) - kpos
        return (dist >= 0) & (dist <= max_dist), kpos

    band2, kpos2 = band(2 * BLOCK)
    band1, _ = band(BLOCK)
    lane = lax.broadcasted_iota(I32, (nq, LANES), 1)
    row = lax.broadcasted_iota(I32, (nq, LANES), 0)
    own_half = (lane < HEAD_DIM) == (row < BLOCK)
    left_lanes = lax.broadcasted_iota(I32, (BLOCK, LANES), 1) < HEAD_DIM
    lane8 = lax.broadcasted_iota(I32, (BLOCK, 2 * N_PAIRS), 1)

    for res in range(n_res):
        for qb in range(n_qblk):
            rows = slice(qb * BLOCK, (qb + 1) * BLOCK)
            lse_blk = jnp.zeros((BLOCK, 2 * N_PAIRS), F32)
            for p in range(N_PAIRS):
                cq = slice(res * MIX_B + p * LANES, res * MIX_B + (p + 1) * LANES)
                ck = slice(res * kw, res * kw + LANES) if kv_shared else slice(
                    res * kw + p * LANES, res * kw + (p + 1) * LANES)
                qp = q_ref[rows, cq]
                qs = jnp.concatenate([qp, qp], axis=0)
                qs = jnp.where(own_half, qs, jnp.zeros_like(qs))
                if qb > 0:
                    keys = slice((qb - 1) * BLOCK, (qb + 1) * BLOCK)
                    k, v, valid = kc_ref[keys, ck], vc_ref[keys, ck], band2
                elif use_prev:
                    k = jnp.concatenate([kp_ref[:, ck], kc_ref[rows, ck]], axis=0)
                    v = jnp.concatenate([vp_ref[:, ck], vc_ref[rows, ck]], axis=0)
                    valid = band2 & ((kpos2 >= BLOCK) | jnp.logical_not(first_step))
                else:
                    k, v, valid = kc_ref[rows, ck], vc_ref[rows, ck], band1
                s = lax.dot_general(qs, k, (((1,), (1,)), ((), ())), preferred_element_type=F32)
                s = jnp.where(valid, s, NEG)
                m = jnp.max(s, axis=-1, keepdims=True)
                if has_sinks:
                    rows1 = lax.broadcasted_iota(I32, (nq, 1), 0)
                    sink = jnp.where(rows1 < BLOCK, sink_ref[2 * p], sink_ref[2 * p + 1])
                    m = jnp.maximum(m, sink)
                e = jnp.exp(s - m)
                l = jnp.sum(e, axis=-1, keepdims=True)
                if has_sinks:
                    l = l + jnp.exp(sink - m)
                o = jnp.dot(e.astype(BF16), v, preferred_element_type=F32) / l
                o_ref[rows, cq] = jnp.where(left_lanes, o[:BLOCK], o[BLOCK:]).astype(BF16)
                if want_lse:
                    lse = m + jnp.log(l)
                    lse_blk = (lse_blk + jnp.where(lane8 == 2 * p, lse[:BLOCK], 0.0)
                               + jnp.where(lane8 == 2 * p + 1, lse[BLOCK:], 0.0))
            if want_lse:
                lse_ref[res, rows, :] = lse_blk


def _attention(q, k, v, *, nbatch, seq, dil, max_dist, kv_shared, sinks=None, want_lse):
    length = seq // dil
    nblk = length // BLOCK
    n_qblk = min(nblk, ATTN_BLOCKS_PER_STEP)
    n_res = min(dil, ATTN_BLOCKS_PER_STEP // n_qblk)
    steps = nblk // n_qblk
    use_prev = steps > 1
    kw = k.shape[1] // dil
    view = lambda t: t.reshape(nbatch, length, t.shape[1])
    cur = lambda b, r, i: (b, i, r)
    prev = lambda b, r, i: (b, jnp.maximum(i * n_qblk - 1, 0), r)
    in_specs, args = [], []
    if sinks is not None:
        in_specs.append(pl.BlockSpec(memory_space=pltpu.SMEM))
        args.append(sinks)
    in_specs.append(pl.BlockSpec((None, n_qblk * BLOCK, n_res * MIX_B), cur))
    args.append(view(q))
    for t in (k, v):
        if use_prev:
            in_specs.append(pl.BlockSpec((None, BLOCK, n_res * kw), prev))
            args.append(view(t))
        in_specs.append(pl.BlockSpec((None, n_qblk * BLOCK, n_res * kw), cur))
        args.append(view(t))
    out_specs = [pl.BlockSpec((None, n_qblk * BLOCK, n_res * MIX_B), cur)]
    out_shape = [jax.ShapeDtypeStruct((nbatch, length, dil * MIX_B), BF16)]
    if want_lse:
        out_specs.append(pl.BlockSpec((None, n_res, n_qblk * BLOCK, N_HEADS_B),
                                      lambda b, r, i: (b, r, i, 0)))
        out_shape.append(jax.ShapeDtypeStruct((nbatch, dil, length, N_HEADS_B), F32))
    outs = pl.pallas_call(
        functools.partial(_attn_kernel, kv_shared=kv_shared, max_dist=max_dist, n_qblk=n_qblk,
                          n_res=n_res, use_prev=use_prev, has_sinks=sinks is not None,
                          want_lse=want_lse),
        grid=(nbatch, dil // n_res, steps),
        in_specs=in_specs, out_specs=out_specs, out_shape=out_shape,
        compiler_params=_params("parallel", "parallel", "arbitrary"),
        name=f"attn_d{dil}" + ("_swa" if kv_shared else ""),
    )(*args)
    o = outs[0].reshape(nbatch * length, dil * MIX_B)
    if not want_lse:
        return o, None
    lse = outs[1].transpose(0, 2, 1, 3).reshape(nbatch * seq, N_HEADS_B)
    return o, lse


def _expand_heads(w, width):
    head = lax.broadcasted_iota(I32, (1, width), 1) // HEAD_DIM
    out = jnp.zeros((w.shape[0], width), F32)
    for hd in range(w.shape[1]):
        out = jnp.where(head == hd, w[:, hd:hd + 1], out)
    return out


def _outproj_kernel(x_ref, mod_ref, oa_ref, ob1_ref, ob2_ref, ob3_ref, l1_ref, l2_ref, l3_ref,
                    goa_ref, gob_ref, wo_ref, gf_ref, h2_ref, h2t_ref, x1_ref, ob_scr):
    tt = x_ref.shape[0]
    gate_a = mod_ref[2:3, :]
    shift_m, scale_m = mod_ref[3:4, :], mod_ref[4:5, :]

    def token_major(ref, dil):
        if dil == 1:
            return ref[...].astype(F32)
        for r in range(dil):
            for j in range(N_PAIRS):
                c = r * MIX_B + j * LANES
                ob_scr[j, pl.ds(r, tt // dil, stride=dil), :] = ref[:, c:c + LANES].astype(F32)
        return jnp.concatenate([ob_scr[j] for j in range(N_PAIRS)], axis=1)

    l1, l2, l3 = l1_ref[...], l2_ref[...], l3_ref[...]
    mx = jnp.maximum(jnp.maximum(l1, l2), l3)
    e1, e2, e3 = jnp.exp(l1 - mx), jnp.exp(l2 - mx), jnp.exp(l3 - mx)
    den = e1 + e2 + e3
    dils = [dil for _, dil in DILATED_BRANCHES]
    ob = _expand_heads(e1 / den, MIX_B) * token_major(ob1_ref, dils[0])
    ob = ob + _expand_heads(e2 / den, MIX_B) * token_major(ob2_ref, dils[1])
    ob = ob + _expand_heads(e3 / den, MIX_B) * token_major(ob3_ref, dils[2])
    ob = _rms_rows(ob) * gob_ref[...]
    oa = _rms_rows(oa_ref[...].astype(F32)) * goa_ref[...]
    y = (jnp.dot(oa.astype(BF16), wo_ref[0:MIX_A, :], preferred_element_type=F32)
         + jnp.dot(ob.astype(BF16), wo_ref[MIX_A:MIX_A + MIX_B, :], preferred_element_type=F32))
    x1 = x_ref[...] + gate_a * y
    h2 = _rms_rows(x1) * gf_ref[...] * (1.0 + scale_m) + shift_m
    h2_ref[...] = h2
    _store_row_tiles(h2t_ref, h2)
    x1_ref[...] = x1


def _outproj(xf, mod, oa, obs, lses, goa, gob, wo_p, gf, seq):
    n, d = xf.shape
    tt = TT_PROJ
    tiles_per_seq = seq // tt
    tile = lambda w: pl.BlockSpec((tt, w), lambda i: (i, 0))
    full = lambda shape: pl.BlockSpec(shape, lambda i: (0,) * len(shape))
    dilated = [pl.BlockSpec((tt // dil, dil * MIX_B), lambda i: (i, 0)) for _, dil in DILATED_BRANCHES]
    return pl.pallas_call(
        _outproj_kernel,
        grid=(n // tt,),
        in_specs=[tile(d), pl.BlockSpec((None, 6, d), lambda i: (i // tiles_per_seq, 0, 0)),
                  tile(MIX_A), *dilated,
                  tile(N_HEADS_B), tile(N_HEADS_B), tile(N_HEADS_B),
                  full((1, MIX_A)), full((1, MIX_B)), full((MIX_A + MIX_B, d)), full((1, d))],
        out_specs=[tile(d), pl.BlockSpec((tt * d // LANES, LANES), lambda i: (i, 0)), tile(d)],
        out_shape=[jax.ShapeDtypeStruct((n, d), F32), jax.ShapeDtypeStruct((n * d // LANES, LANES), F32),
                   jax.ShapeDtypeStruct((n, d), F32)],
        scratch_shapes=[pltpu.VMEM((N_PAIRS, tt, LANES), F32)],
        compiler_params=_params("parallel"),
        name="outproj",
    )(xf, mod, oa, *obs, *lses, goa, gob, wo_p, gf)


def _router_kernel(h_ref, wrt_ref, bias_ref, e_ref, g_ref, cnt_ref):
    tt = h_ref.shape[0]
    logits = lax.dot_general(wrt_ref[...], h_ref[...].astype(BF16), (((1,), (1,)), ((), ())),
                             preferred_element_type=F32)
    scores = 1.0 / (1.0 + jnp.exp(-logits))
    biased = scores + bias_ref[...]
    ninf = -jnp.inf

    j32 = lax.broadcasted_iota(I32, (GROUP_SIZE, tt), 0).astype(F32)
    grp = []
    for g in range(N_GROUPS):
        bg = biased[g * GROUP_SIZE:(g + 1) * GROUP_SIZE, :]
        m1 = jnp.max(bg, axis=0, keepdims=True)
        i1 = jnp.min(jnp.where(bg == m1, j32, float(GROUP_SIZE)), axis=0, keepdims=True)
        m2 = jnp.max(jnp.where(j32 == i1, ninf, bg), axis=0, keepdims=True)
        grp.append(m1 + m2)
    grp = jnp.concatenate(grp, axis=0)
    g8 = lax.broadcasted_iota(I32, (N_GROUPS, tt), 0).astype(F32)
    chosen = jnp.zeros((N_GROUPS, tt), F32)
    for _ in range(TOPK_GROUPS):
        gm = jnp.max(grp, axis=0, keepdims=True)
        gi = jnp.min(jnp.where(grp == gm, g8, float(N_GROUPS)), axis=0, keepdims=True)
        hit = g8 == gi
        chosen = jnp.where(hit, 1.0, chosen)
        grp = jnp.where(hit, ninf, grp)
    masked = jnp.concatenate(
        [jnp.where(chosen[g:g + 1, :] > 0.0, biased[g * GROUP_SIZE:(g + 1) * GROUP_SIZE, :], ninf)
         for g in range(N_GROUPS)], axis=0)

    eio = lax.broadcasted_iota(I32, (N_EXPERTS, tt), 0).astype(F32)
    picked = jnp.zeros((N_EXPERTS, tt), F32)
    es, gs = [], []
    for _ in range(TOP_K):
        m = jnp.max(masked, axis=0, keepdims=True)
        idx = jnp.min(jnp.where(masked == m, eio, float(N_EXPERTS)), axis=0, keepdims=True)
        hit = eio == idx
        gs.append(jnp.sum(jnp.where(hit, scores, 0.0), axis=0, keepdims=True))
        es.append(idx)
        picked = jnp.where(hit, 1.0, picked)
        masked = jnp.where(hit, ninf, masked)
    gates = jnp.concatenate(gs, axis=0)
    e_ref[...] = jnp.concatenate(es, axis=0).astype(I32)
    g_ref[...] = gates / jnp.sum(gates, axis=0, keepdims=True) * ROUTED_SCALE

    @pl.when(pl.program_id(0) == 0)
    def _():
        cnt_ref[...] = jnp.zeros_like(cnt_ref)
    cnt_ref[...] += jnp.sum(picked, axis=1, keepdims=True)


def _router(h2, wrt, bias_col):
    n, d = h2.shape
    tt = TT_ROUTE
    return pl.pallas_call(
        _router_kernel,
        grid=(n // tt,),
        in_specs=[pl.BlockSpec((tt, d), lambda i: (i, 0)),
                  pl.BlockSpec((N_EXPERTS, d), lambda i: (0, 0)),
                  pl.BlockSpec((N_EXPERTS, 1), lambda i: (0, 0))],
        out_specs=[pl.BlockSpec((TOP_K, tt), lambda i: (0, i)),
                   pl.BlockSpec((TOP_K, tt), lambda i: (0, i)),
                   pl.BlockSpec((N_EXPERTS, 1), lambda i: (0, 0))],
        out_shape=[jax.ShapeDtypeStruct((TOP_K, n), I32),
                   jax.ShapeDtypeStruct((TOP_K, n), F32),
                   jax.ShapeDtypeStruct((N_EXPERTS, 1), F32)],
        compiler_params=_params("arbitrary"),
        name="router",
    )(h2, wrt, bias_col)


def _rank_kernel(e_ref, pstart_ref, tri_ref, dest_ref, carry_ref):
    tt = e_ref.shape[1]

    @pl.when(pl.program_id(0) == 0)
    def _():
        carry_ref[...] = pstart_ref[...]

    e = e_ref[...]
    eio = lax.broadcasted_iota(I32, (N_EXPERTS, tt), 0)
    mask = jnp.zeros((N_EXPERTS, tt), F32)
    for k in range(TOP_K):
        mask = jnp.where(eio == e[k:k + 1, :], 1.0, mask)
    incl = jnp.dot(mask.astype(BF16), tri_ref[...], preferred_element_type=F32)
    pos = incl - mask + carry_ref[...]
    dest = [jnp.sum(jnp.where(eio == e[k:k + 1, :], pos, 0.0), axis=0, keepdims=True)
            for k in range(TOP_K)]
    dest_ref[...] = jnp.concatenate(dest, axis=0).astype(I32)
    carry_ref[...] += incl[:, tt - 1:tt]


def _rank(top_e_t, pstart_col, tri):
    n = top_e_t.shape[1]
    tt = TT_ROUTE
    return pl.pallas_call(
        _rank_kernel,
        grid=(n // tt,),
        in_specs=[pl.BlockSpec((TOP_K, tt), lambda i: (0, i)),
                  pl.BlockSpec((N_EXPERTS, 1), lambda i: (0, 0)),
                  pl.BlockSpec((tt, tt), lambda i: (0, 0))],
        out_specs=pl.BlockSpec((TOP_K, tt), lambda i: (0, i)),
        out_shape=jax.ShapeDtypeStruct((TOP_K, n), I32),
        scratch_shapes=[pltpu.VMEM((N_EXPERTS, 1), F32)],
        compiler_params=_params("arbitrary"),
        name="rank",
    )(top_e_t, pstart_col, tri)


SC_CORES, SC_SUBCORES, SC_LANES = 2, 16, 16
SC_WORKERS = SC_CORES * SC_SUBCORES
INVERT_CHUNK = 8192


def _invert(dest_flat, n_tokens, rows):
    per = rows // SC_WORKERS
    n_assign = dest_flat.shape[0]
    assert rows % (SC_WORKERS * SC_LANES) == 0 and n_assign % INVERT_CHUNK == 0
    assert n_tokens & (n_tokens - 1) == 0

    @functools.partial(
        pl.kernel, mesh=plsc.VectorSubcoreMesh(core_axis_name="c", subcore_axis_name="s"),
        out_type=jax.ShapeDtypeStruct((rows,), I32),
        scratch_types=[pltpu.VMEM((INVERT_CHUNK,), I32), pltpu.VMEM((per,), I32)],
        compiler_params=pltpu.CompilerParams(needs_layout_passes=False))
    def invert(dest_hbm, out_hbm, staged, local):
        base = (lax.axis_index("s") * SC_CORES + lax.axis_index("c")) * per
        lane = lax.iota(I32, SC_LANES)

        @pl.loop(0, per, step=SC_LANES)
        def _(i):
            local[pl.ds(i, SC_LANES)] = (base + i + lane) & (n_tokens - 1)

        @pl.loop(0, n_assign // INVERT_CHUNK)
        def _(c):
            pltpu.sync_copy(dest_hbm.at[pl.ds(c * INVERT_CHUNK, INVERT_CHUNK)], staged)

            @pl.loop(0, INVERT_CHUNK, step=SC_LANES)
            def _(i):
                rel = staged[pl.ds(i, SC_LANES)] - base
                mine = (rel >= 0) & (rel < per)
                tok = (c * INVERT_CHUNK + i + lane) & (n_tokens - 1)
                plsc.store_scatter(local, [jnp.where(mine, rel, 0)], tok, mask=mine)

        pltpu.sync_copy(local, out_hbm.at[pl.ds(base, per)])

    return invert(dest_flat)


def _experts_kernel(rank_ref, eor_ref, nv_ref, nr_ref, tok_hbm, h_ref, wg_hbm, wu_hbm, wd_hbm,
                    ys_ref, *scratch):
    xbufs = scratch[:GATHER_DEPTH]
    wgf, wuf, wdf, wgb, wub, wdb, tok_s, sem, tok_sem, w_sem = scratch[GATHER_DEPTH:]
    step = pl.program_id(0)
    nv, n_ranks = nv_ref[0], nr_ref[0]
    per = wgb.shape[0] // LANES
    depth, bm = len(xbufs), xbufs[0].shape[0] // per
    ahead = depth - 1
    chunk = tok_s.shape[0] // 2
    cb = chunk // bm
    n_chunks = tok_hbm.shape[0] // chunk
    n_blocks = n_chunks * cb

    def tok_copy(c):
        return pltpu.make_async_copy(
            tok_hbm.at[pl.ds(pl.multiple_of(c * chunk, chunk), chunk)],
            tok_s.at[pl.ds(pl.multiple_of((c % 2) * chunk, chunk), chunk)], tok_sem.at[c % 2])

    def gather(block, ring, unrolled):
        block = jnp.minimum(block, n_blocks - 1)
        base = ((block // cb) % 2) * chunk + (block % cb) * bm
        buf, buf_sem = xbufs[ring], sem.at[ring]

        def issue(i, carry=None):
            _tile_copy(h_ref, tok_s[base + i], buf, i, per, buf_sem).start()
            return carry

        if unrolled:
            for i in range(bm):
                _tile_copy(h_ref, tok_s[base + i], buf, i, per, buf_sem).start(
                    priority=int(i % ROW_QUEUE_SPLIT != 0))
        else:
            lax.fori_loop(0, bm, issue, 0, unroll=8)

    def wait_rows(ring):
        pltpu.make_async_copy(h_ref.at[pl.ds(0, bm * per)], xbufs[ring], sem.at[ring]).wait()

    def weight_copies(r):
        e, s = eor_ref[r], r % WEIGHT_DEPTH
        return [pltpu.make_async_copy(src.at[e], dst.at[s], w_sem.at[s])
                for src, dst in ((wg_hbm, wgf), (wu_hbm, wuf), (wd_hbm, wdf))]

    @pl.when(step == 0)
    def _():
        tok_copy(0).start()
        tok_copy(0).wait()
        tok_copy(1).start()
        for r in range(WEIGHT_DEPTH - 1):
            @pl.when(r < n_ranks)
            def _():
                for cp in weight_copies(r):
                    cp.start()
        for b in range(ahead):
            gather(b, b, unrolled=False)

    def block(j, ring, out_row0):
        first = j + ahead
        c_need = first // cb

        @pl.when(jnp.logical_and(jnp.logical_and(j > 0, first % cb == 0), c_need < n_chunks))
        def _():
            tok_copy(c_need).wait()

            @pl.when(c_need + 1 < n_chunks)
            def _():
                tok_copy(c_need + 1).start()

        jb = jnp.minimum(j, n_blocks - 1)
        rank = rank_ref[jb]
        fresh = jnp.logical_or(j == 0, rank != rank_ref[jnp.maximum(jb - 1, 0)])

        @pl.when(jnp.logical_and(j < nv, fresh))
        def _():
            for cp in weight_copies(rank):
                cp.wait()
            s = rank % WEIGHT_DEPTH
            wgb[...] = wgf[s].astype(BF16)
            wub[...] = wuf[s].astype(BF16)
            wdb[...] = wdf[s].astype(BF16)
            nxt = rank + WEIGHT_DEPTH - 1

            @pl.when(nxt < n_ranks)
            def _():
                for cp in weight_copies(nxt):
                    cp.start()

        @pl.when(j < nv)
        def _():
            wait_rows(ring)
            gather(j + ahead, (ring + ahead) % depth, unrolled=True)
            xb = _load_row_tiles(xbufs[ring], bm, per, BF16)
            act = (_silu(jnp.dot(xb, wgb[...], preferred_element_type=F32))
                   * jnp.dot(xb, wub[...], preferred_element_type=F32))
            _store_row_tiles(ys_ref, jnp.dot(act.astype(BF16), wdb[...], preferred_element_type=F32),
                             out_row0)

        @pl.when(jnp.logical_and(j >= nv, j < nv + ahead))
        def _():
            wait_rows(ring)

    half = BLOCKS_PER_STEP
    for parity in range(depth // half):
        @pl.when(step % (depth // half) == parity)
        def _():
            for h in range(half):
                block(step * half + h, parity * half + h, h * bm * per)


def _experts(block_rank, expert_of_rank, n_valid, n_ranks, row_tok, h2_tiles, wg, wu, wd):
    rows = row_tok.shape[0]
    d = wg.shape[1]
    per = d // LANES
    bm = EXPERT_ROWS
    n_blocks = rows // bm
    assert n_blocks % TOK_CHUNK_BLOCKS == 0 and n_blocks // TOK_CHUNK_BLOCKS >= 2
    assert GATHER_DEPTH - 1 < TOK_CHUNK_BLOCKS
    bps = BLOCKS_PER_STEP
    assert GATHER_DEPTH % bps == 0 and n_blocks % bps == 0
    f = wg.shape[2]
    hbm = pl.BlockSpec(memory_space=pl.ANY)
    return pl.pallas_call(
        _experts_kernel,
        grid_spec=pltpu.PrefetchScalarGridSpec(
            num_scalar_prefetch=4,
            grid=((n_blocks + GATHER_DEPTH) // bps,),
            in_specs=[hbm, hbm, hbm, hbm, hbm],
            out_specs=pl.BlockSpec((bps * bm * per, LANES),
                                   lambda s, rk, eor, nv, nr: (jnp.minimum(s, (nv[0] - 1) // bps), 0)),
            scratch_shapes=[pltpu.VMEM((bm * per, LANES), F32)] * GATHER_DEPTH + [
                            pltpu.VMEM((WEIGHT_DEPTH, d, f), F32), pltpu.VMEM((WEIGHT_DEPTH, d, f), F32),
                            pltpu.VMEM((WEIGHT_DEPTH, f, d), F32),
                            pltpu.VMEM((d, f), BF16), pltpu.VMEM((d, f), BF16),
                            pltpu.VMEM((f, d), BF16),
                            pltpu.SMEM((2 * TOK_CHUNK_BLOCKS * bm,), I32),
                            pltpu.SemaphoreType.DMA((GATHER_DEPTH,)), pltpu.SemaphoreType.DMA((2,)),
                            pltpu.SemaphoreType.DMA((WEIGHT_DEPTH,))]),
        out_shape=jax.ShapeDtypeStruct((rows * per, LANES), F32),
        compiler_params=_params("arbitrary"),
        name="experts",
    )(block_rank, expert_of_rank, n_valid, n_ranks, row_tok, h2_tiles, wg, wu, wd)


def _combine_kernel(dest_ref, gates_ref, x1_ref, h2_ref, mod_ref, wgs_ref, wus_ref, wds_ref, ys_ref,
                    out_ref, buf, sem):
    tt, d = x1_ref.shape
    per = d // LANES
    for t in range(tt):
        for k in range(TOP_K):
            _tile_copy(ys_ref, dest_ref[k, t], buf.at[k], t, per, sem).start(priority=k % 2)
    hb = h2_ref[...].astype(BF16)
    act = (_silu(jnp.dot(hb, wgs_ref[...], preferred_element_type=F32))
           * jnp.dot(hb, wus_ref[...], preferred_element_type=F32))
    shared = jnp.dot(act.astype(BF16), wds_ref[...], preferred_element_type=F32)
    for k in range(TOP_K):
        pltpu.make_async_copy(ys_ref.at[pl.ds(0, tt * per)], buf.at[k], sem).wait()
    gates = gates_ref[...]
    routed = gates[:, 0:1] * _load_row_tiles(buf.at[0], tt, per, F32)
    for k in range(1, TOP_K):
        routed = routed + gates[:, k:k + 1] * _load_row_tiles(buf.at[k], tt, per, F32)
    out_ref[...] = x1_ref[...] + mod_ref[5:6, :] * (routed + shared)


def _combine(dest_t, gates, x1, h2, mod, wgs, wus, wds, ys, seq):
    n, d = x1.shape
    tt = TT_COMBINE
    tiles_per_seq = seq // tt
    sd = wgs.shape[1]
    tile = pl.BlockSpec((tt, d), lambda i: (i, 0))
    full = lambda shape: pl.BlockSpec(shape, lambda i: (0,) * len(shape))
    return pl.pallas_call(
        _combine_kernel,
        grid=(n // tt,),
        in_specs=[pl.BlockSpec((TOP_K, tt), lambda i: (0, i), memory_space=pltpu.SMEM),
                  pl.BlockSpec((tt, TOP_K), lambda i: (i, 0)),
                  tile, tile,
                  pl.BlockSpec((None, 6, d), lambda i: (i // tiles_per_seq, 0, 0)),
                  full((d, sd)), full((d, sd)), full((sd, d)),
                  pl.BlockSpec(memory_space=pl.ANY)],
        out_specs=tile,
        out_shape=jax.ShapeDtypeStruct((n, d), F32),
        scratch_shapes=[pltpu.VMEM((TOP_K, tt * d // LANES, LANES), F32), pltpu.SemaphoreType.DMA],
        compiler_params=_params("arbitrary"),
        name="combine",
    )(dest_t, gates, x1, h2, mod, wgs, wus, wds, ys)


def _layer(x, mod, pos, rope, p):
    nbatch, seq, d = x.shape
    n = nbatch * seq
    xf = x.reshape(n, d)
    invf, spread, bd = rope

    perm = np.concatenate([np.arange(h * HEAD_DIM, (h + 1) * HEAD_DIM) for h in PAIR_ORDER_A])
    w_in = p["w_in"]
    w_in_p = jnp.concatenate([w_in[:, :MIX_A][:, perm], w_in[:, MIX_A:]], axis=1).astype(BF16)
    ones = lambda w: jnp.ones((w,), F32)
    qscale = HEAD_DIM ** -0.5
    gcol = jnp.concatenate([jnp.tile(p["g_q_a"], N_HEADS_A) * qscale, jnp.tile(p["g_k_a"], N_KV_A),
                            ones(KV_A), jnp.tile(p["g_q_b"], N_HEADS_B) * qscale,
                            jnp.tile(p["g_k_b"], N_HEADS_B), ones(MIX_B)]).reshape(1, IN_WIDTH)
    proj = _inproj(xf, mod, pos, p["g_norm_mix"].reshape(1, d), w_in_p, gcol, invf, spread, bd, seq)
    qa, ka, va = proj[:3]
    qkv_b = {1: proj[3:6]}
    for j, dil in enumerate(DILS):
        qkv_b[dil] = [proj[6 + t * len(DILS) + j] for t in range(3)]

    sinks_p = p["sinks_a"][np.array(PAIR_ORDER_A)]
    oa, _ = _attention(qa, ka, va, nbatch=nbatch, seq=seq, dil=1, max_dist=WINDOW_A - 1,
                       kv_shared=True, sinks=sinks_p, want_lse=False)
    obs, lses = [], []
    for window, dil in DILATED_BRANCHES:
        o, lse = _attention(*qkv_b[dil], nbatch=nbatch, seq=seq, dil=dil, max_dist=window // dil,
                            kv_shared=False, want_lse=True)
        obs.append(o)
        lses.append(lse)

    goa = p["g_out_a"][perm].reshape(1, MIX_A)
    w_out = p["w_out"]
    wo_p = jnp.concatenate([w_out[:MIX_A][perm], w_out[MIX_A:]], axis=0).astype(BF16)
    h2, h2_tiles, x1 = _outproj(xf, mod, oa, obs, lses, goa, p["g_out_b"].reshape(1, MIX_B), wo_p,
                                p["g_norm_ffn"].reshape(1, d), seq)

    top_e_t, gates_t, counts = _router(h2, p["w_router"].T.astype(BF16),
                                       p["router_bias"].reshape(N_EXPERTS, 1))
    bm = EXPERT_ROWS
    counts = counts.reshape(N_EXPERTS).astype(I32)
    padded = (counts + bm - 1) // bm * bm
    pends = jnp.cumsum(padded)
    pstarts = pends - padded
    rows = n * TOP_K + N_EXPERTS * bm
    n_blocks = rows // bm
    n_valid = (pends[-1] // bm).astype(I32)
    blk = jnp.minimum(jnp.arange(n_blocks, dtype=I32), n_valid - 1)
    block_e = jnp.sum((pends[None, :] <= (blk * bm)[:, None]).astype(I32), axis=1)
    block_e = jnp.minimum(block_e, N_EXPERTS - 1)
    nonempty = counts > 0
    rank_of_e = jnp.cumsum(nonempty.astype(I32)) - 1
    e_ids = jnp.arange(N_EXPERTS, dtype=I32)
    block_rank = jnp.sum(jnp.where(block_e[:, None] == e_ids[None, :], rank_of_e[None, :], 0), axis=1)
    hit = (rank_of_e[None, :] == jnp.arange(n_blocks, dtype=I32)[:, None]) & nonempty[None, :]
    expert_of_rank = jnp.sum(jnp.where(hit, e_ids[None, :], 0), axis=1)
    n_ranks = jnp.sum(nonempty.astype(I32)).reshape(1)

    tri =(np.arange(TT_ROUTE)[:, None] <= np.arange(TT_ROUTE)[None, :])
    dest_t = _rank(top_e_t, pstarts.astype(F32).reshape(N_EXPERTS, 1), jnp.asarray(tri, BF16))
    n_valid = n_valid.reshape(1)
    row_tok = _invert(dest_t.reshape(n * TOP_K), n, rows)
    ys = _experts(block_rank.astype(I32), expert_of_rank.astype(I32), n_valid, n_ranks, row_tok, h2_tiles,
                  p["w_gate_e"], p["w_up_e"], p["w_down_e"])
    out = _combine(dest_t, gates_t.T, x1, h2, mod, p["w_gate_s"].astype(BF16), p["w_up_s"].astype(BF16),
                   p["w_down_s"].astype(BF16), ys, seq)
    return out.reshape(nbatch, seq, d)


def kernel(x, c, positions, w_ada, b_ada, g_norm_mix, w_in, g_q_a, g_k_a, sinks_a, g_q_b, g_k_b,
           g_out_a, g_out_b, w_out, g_norm_ffn, w_router, router_bias, w_gate_e, w_up_e, w_down_e,
           w_gate_s, w_up_s, w_down_s):
    nbatch, seq, d = x.shape
    depth = w_ada.shape[0]
    params = dict(g_norm_mix=g_norm_mix, w_in=w_in, g_q_a=g_q_a, g_k_a=g_k_a, sinks_a=sinks_a,
                  g_q_b=g_q_b, g_k_b=g_k_b, g_out_a=g_out_a, g_out_b=g_out_b, w_out=w_out,
                  g_norm_ffn=g_norm_ffn, w_router=w_router, router_bias=router_bias,
                  w_gate_e=w_gate_e, w_up_e=w_up_e, w_down_e=w_down_e, w_gate_s=w_gate_s,
                  w_up_s=w_up_s, w_down_s=w_down_s)
    j = np.arange(LANES) % HEAD_DIM
    invf = (ROPE_THETA ** (-jnp.arange(0, ROT_DIM, 2, dtype=F32) / ROT_DIM)).reshape(ROT_DIM // 2, 1)
    spread = jnp.asarray((j[None, :] < ROT_DIM)
                         & (j[None, :] % (ROT_DIM // 2) == np.arange(ROT_DIM // 2)[:, None]), BF16)
    bd = jnp.asarray((np.arange(LANES)[:, None] // HEAD_DIM) == (np.arange(LANES)[None, :] // HEAD_DIM),
                     BF16)
    pos = positions.reshape(1, nbatch * seq).astype(I32)
    for l in range(depth):
        mod = _adaln(c.astype(F32), w_ada[l], b_ada[l]).reshape(nbatch, 6, d)
        x = _layer(x, mod, pos, (invf, spread, bd), {k: v[l] for k, v in params.items()})
    return x
```

```python
import functools

import numpy as np
import jax
import jax.numpy as jnp
from jax import lax
from jax.experimental import pallas as pl
from jax.experimental.pallas import tpu as pltpu
from jax.experimental.pallas import tpu_sc as plsc

F32 = jnp.float32
BF16 = jnp.bfloat16
I32 = jnp.int32

HEAD_DIM = 64
N_HEADS_A = 8
N_KV_A = 2
WINDOW_A = 128
N_HEADS_B = 8
DILATED_BRANCHES = ((128, 1), (512, 4), (2048, 16))
DILS = tuple(dil for _, dil in DILATED_BRANCHES if dil > 1)
BLOCK = 128
ROT_DIM = HEAD_DIM // 4
ROPE_THETA = 500000.0
MIX_A = N_HEADS_A * HEAD_DIM
KV_A = N_KV_A * HEAD_DIM
MIX_B = N_HEADS_B * HEAD_DIM
N_EXPERTS = 256
TOP_K = 8
N_GROUPS = 8
TOPK_GROUPS = 4
GROUP_SIZE = N_EXPERTS // N_GROUPS
ROUTED_SCALE = 2.5
EPS = 1e-6

LANES = 128
N_PAIRS = MIX_A // LANES
NEG = -1e30
VMEM_LIMIT = 48 * 1024 * 1024

TT_PROJ = 512
ATTN_BLOCKS_PER_STEP = 8
TT_ROUTE = 512
TT_COMBINE = 512
EXPERT_ROWS = 256
TOK_CHUNK_BLOCKS = 32
GATHER_DEPTH = 4
WEIGHT_DEPTH = 3
BLOCKS_PER_STEP = 2
ROW_QUEUE_SPLIT = 2

PAIR_ORDER_A = tuple(h for p in range(N_PAIRS) for h in (p, p + N_HEADS_A // N_KV_A))


def _params(*sem):
    return pltpu.CompilerParams(dimension_semantics=sem, vmem_limit_bytes=VMEM_LIMIT)


def _silu(t):
    return t / (1.0 + jnp.exp(-t))


def _rms_rows(t):
    return t * lax.rsqrt(jnp.mean(t * t, axis=-1, keepdims=True) + EPS)


def _store_row_tiles(ref, value, row0=0):
    rows, d = value.shape
    per = d // LANES
    for s in range(per):
        ref[pl.ds(row0 + s, rows, stride=per), :] = value[:, s * LANES:(s + 1) * LANES]


def _load_row_tiles(ref, rows, per, dtype):
    return jnp.concatenate([ref[pl.ds(s, rows, stride=per), :].astype(dtype) for s in range(per)], axis=1)


def _tile_copy(src_ref, src_row, dst_ref, dst_row, per, sem):
    start = lambda row: row * per if isinstance(row, int) else pl.multiple_of(row * per, per)
    return pltpu.make_async_copy(src_ref.at[pl.ds(start(src_row), per)],
                                 dst_ref.at[pl.ds(start(dst_row), per)], sem)


def _ada_kernel(c_ref, w_ref, b_ref, o_ref):
    cond = _silu(c_ref[...])
    o_ref[...] = jnp.dot(cond.astype(BF16), w_ref[...].astype(BF16),
                         preferred_element_type=F32) + b_ref[...]


def _adaln(c, w_ada, b_ada):
    nb, d = c.shape
    width = w_ada.shape[1]
    tn = 1024
    return pl.pallas_call(
        _ada_kernel,
        grid=(width // tn,),
        in_specs=[pl.BlockSpec((nb, d), lambda j: (0, 0)),
                  pl.BlockSpec((d, tn), lambda j: (0, j)),
                  pl.BlockSpec((1, tn), lambda j: (0, j))],
        out_specs=pl.BlockSpec((nb, tn), lambda j: (0, j)),
        out_shape=jax.ShapeDtypeStruct((nb, width), F32),
        compiler_params=_params("arbitrary"),
        name="adaln",
    )(c, w_ada, b_ada.reshape(1, width))


COL_QA, COL_KA, COL_VA = 0, MIX_A, MIX_A + KV_A
COL_QB = MIX_A + 2 * KV_A
COL_KB, COL_VB = COL_QB + MIX_B, COL_QB + 2 * MIX_B
IN_WIDTH = COL_VB + MIX_B


def _inproj_kernel(x_ref, mod_ref, pos_ref, gn_ref, w_ref, gcol_ref, invf_ref, spread_ref, bd_ref,
                   qa_ref, ka_ref, va_ref, qb_ref, kb_ref, vb_ref, *rest):
    n_dil = len(DILS)
    dil_refs = [rest[i * n_dil:(i + 1) * n_dil] for i in range(3)]
    qb_scr, kb_scr, vb_scr = rest[3 * n_dil:]
    tt = x_ref.shape[0]
    shift, scale = mod_ref[0:1, :], mod_ref[1:2, :]
    h = _rms_rows(x_ref[...]) * gn_ref[...] * (1.0 + scale) + shift
    proj = jnp.dot(h.astype(BF16), w_ref[...], preferred_element_type=F32)

    ang = pos_ref[...].astype(F32) * invf_ref[...]
    lane = lax.broadcasted_iota(I32, (1, LANES), 1) % HEAD_DIM

    def spread(table):
        hi = table.astype(BF16)
        lo = (table - hi.astype(F32)).astype(BF16)
        to_lanes = lambda a: lax.dot_general(a, spread_ref[...], (((0,), (0,)), ((), ())),
                                             preferred_element_type=F32)
        return to_lanes(hi) + to_lanes(lo)

    cs = spread(jnp.cos(ang)) + jnp.where(lane < ROT_DIM, 0.0, 1.0)
    sn = spread(jnp.sin(ang))
    s_lo = jnp.where(lane < ROT_DIM // 2, -sn, 0.0)
    s_hi = jnp.where((lane >= ROT_DIM // 2) & (lane < ROT_DIM), sn, 0.0)
    bd = bd_ref[...]

    def norm_rope(col0, width, out_ref, scr=None):
        for j in range(width // LANES):
            c = col0 + j * LANES
            t = proj[:, c:c + LANES]
            sq = t * t
            hi = sq.astype(BF16)
            lo = (sq - hi.astype(F32)).astype(BF16)
            ss = (jnp.dot(hi, bd, preferred_element_type=F32)
                  + jnp.dot(lo, bd, preferred_element_type=F32))
            t = t * lax.rsqrt(ss * (1.0 / HEAD_DIM) + EPS) * gcol_ref[:, c:c + LANES]
            t = (t * cs + pltpu.roll(t, LANES - ROT_DIM // 2, 1) * s_lo
                 + pltpu.roll(t, ROT_DIM // 2, 1) * s_hi)
            out_ref[:, j * LANES:(j + 1) * LANES] = t.astype(BF16)
            if scr is not None:
                scr[j] = t

    norm_rope(COL_QA, MIX_A, qa_ref)
    norm_rope(COL_KA, KV_A, ka_ref)
    norm_rope(COL_QB, MIX_B, qb_ref, qb_scr)
    norm_rope(COL_KB, MIX_B, kb_ref, kb_scr)
    va_ref[...] = proj[:, COL_VA:COL_VA + KV_A].astype(BF16)
    vb_ref[...] = proj[:, COL_VB:COL_VB + MIX_B].astype(BF16)
    for j in range(N_PAIRS):
        vb_scr[j] = proj[:, COL_VB + j * LANES:COL_VB + (j + 1) * LANES]
    for scr, outs in zip((qb_scr, kb_scr, vb_scr), dil_refs):
        for dil, out in zip(DILS, outs):
            for r in range(dil):
                for j in range(N_PAIRS):
                    c = r * MIX_B + j * LANES
                    out[:, c:c + LANES] = scr[j, pl.ds(r, tt // dil, stride=dil), :].astype(BF16)


def _inproj(xf, mod, pos, g_norm, w_in_p, gcol, invf, spread, bd, seq):
    n, d = xf.shape
    tt = TT_PROJ
    tiles_per_seq = seq // tt
    shapes = [(n, w) for w in (MIX_A, KV_A, KV_A, MIX_B, MIX_B, MIX_B)]
    shapes += [(n // dil, dil * MIX_B) for _ in range(3) for dil in DILS]
    full = lambda shape: pl.BlockSpec(shape, lambda i: (0,) * len(shape))
    return pl.pallas_call(
        _inproj_kernel,
        grid=(n // tt,),
        in_specs=[pl.BlockSpec((tt, d), lambda i: (i, 0)),
                  pl.BlockSpec((None, 6, d), lambda i: (i // tiles_per_seq, 0, 0)),
                  pl.BlockSpec((1, tt), lambda i: (0, i)),
                  full((1, d)), full((d, IN_WIDTH)), full((1, IN_WIDTH)),
                  full((ROT_DIM // 2, 1)), full((ROT_DIM // 2, LANES)), full((LANES, LANES))],
        out_specs=[pl.BlockSpec((tt * r // n, w), lambda i: (i, 0)) for r, w in shapes],
        out_shape=[jax.ShapeDtypeStruct(s, BF16) for s in shapes],
        scratch_shapes=[pltpu.VMEM((N_PAIRS, tt, LANES), F32)] * 3,
        compiler_params=_params("parallel"),
        name="inproj",
    )(xf, mod, pos, g_norm, w_in_p, gcol, invf, spread, bd)


def _attn_kernel(*refs, kv_shared, max_dist, n_qblk, n_res, use_prev, has_sinks, want_lse):
    refs = list(refs)
    sink_ref = refs.pop(0) if has_sinks else None
    q_ref = refs.pop(0)
    kp_ref = refs.pop(0) if use_prev else None
    kc_ref = refs.pop(0)
    vp_ref = refs.pop(0) if use_prev else None
    vc_ref = refs.pop(0)
    o_ref = refs.pop(0)
    lse_ref = refs.pop(0) if want_lse else None

    first_step = pl.program_id(2) == 0
    nq = 2 * BLOCK
    kw = LANES if kv_shared else MIX_B

    def band(nk):
        qpos = lax.broadcasted_iota(I32, (nq, nk), 0) % BLOCK
        kpos = lax.broadcasted_iota(I32, (nq, nk), 1)
        dist = qpos + (nk - BLOCK) - kpos
        return (dist >= 0) & (dist <= max_dist), kpos

    band2, kpos2 = band(2 * BLOCK)
    band1, _ = band(BLOCK)
    lane = lax.broadcasted_iota(I32, (nq, LANES), 1)
    row = lax.broadcasted_iota(I32, (nq, LANES), 0)
    own_half = (lane < HEAD_DIM) == (row < BLOCK)
    left_lanes = lax.broadcasted_iota(I32, (BLOCK, LANES), 1) < HEAD_DIM
    lane8 = lax.broadcasted_iota(I32, (BLOCK, 2 * N_PAIRS), 1)

    for res in range(n_res):
        for qb in range(n_qblk):
            rows = slice(qb * BLOCK, (qb + 1) * BLOCK)
            lse_blk = jnp.zeros((BLOCK, 2 * N_PAIRS), F32)
            for p in range(N_PAIRS):
                cq = slice(res * MIX_B + p * LANES, res * MIX_B + (p + 1) * LANES)
                ck = slice(res * kw, res * kw + LANES) if kv_shared else slice(
                    res * kw + p * LANES, res * kw + (p + 1) * LANES)
                qp = q_ref[rows, cq]
                qs = jnp.concatenate([qp, qp], axis=0)
                qs = jnp.where(own_half, qs, jnp.zeros_like(qs))
                if qb > 0:
                    keys = slice((qb - 1) * BLOCK, (qb + 1) * BLOCK)
                    k, v, valid = kc_ref[keys, ck], vc_ref[keys, ck], band2
                elif use_prev:
                    k = jnp.concatenate([kp_ref[:, ck], kc_ref[rows, ck]], axis=0)
                    v = jnp.concatenate([vp_ref[:, ck], vc_ref[rows, ck]], axis=0)
                    valid = band2 & ((kpos2 >= BLOCK) | jnp.logical_not(first_step))
                else:
                    k, v, valid = kc_ref[rows, ck], vc_ref[rows, ck], band1
                s = lax.dot_general(qs, k, (((1,), (1,)), ((), ())), preferred_element_type=F32)
                s = jnp.where(valid, s, NEG)
                m = jnp.max(s, axis=-1, keepdims=True)
                if has_sinks:
                    rows1 = lax.broadcasted_iota(I32, (nq, 1), 0)
                    sink = jnp.where(rows1 < BLOCK, sink_ref[2 * p], sink_ref[2 * p + 1])
                    m = jnp.maximum(m, sink)
                e = jnp.exp(s - m)
                l = jnp.sum(e, axis=-1, keepdims=True)
                if has_sinks:
                    l = l + jnp.exp(sink - m)
                o = jnp.dot(e.astype(BF16), v, preferred_element_type=F32) / l
                o_ref[rows, cq] = jnp.where(left_lanes, o[:BLOCK], o[BLOCK:]).astype(BF16)
                if want_lse:
                    lse = m + jnp.log(l)
                    lse_blk = (lse_blk + jnp.where(lane8 == 2 * p, lse[:BLOCK], 0.0)
                               + jnp.where(lane8 == 2 * p + 1, lse[BLOCK:], 0.0))
            if want_lse:
                lse_ref[res, rows, :] = lse_blk


def _attention(q, k, v, *, nbatch, seq, dil, max_dist, kv_shared, sinks=None, want_lse):
    length = seq // dil
    nblk = length // BLOCK
    n_qblk = min(nblk, ATTN_BLOCKS_PER_STEP)
    n_res = min(dil, ATTN_BLOCKS_PER_STEP // n_qblk)
    steps = nblk // n_qblk
    use_prev = steps > 1
    kw = k.shape[1] // dil
    view = lambda t: t.reshape(nbatch, length, t.shape[1])
    cur = lambda b, r, i: (b, i, r)
    prev = lambda b, r, i: (b, jnp.maximum(i * n_qblk - 1, 0), r)
    in_specs, args = [], []
    if sinks is not None:
        in_specs.append(pl.BlockSpec(memory_space=pltpu.SMEM))
        args.append(sinks)
    in_specs.append(pl.BlockSpec((None, n_qblk * BLOCK, n_res * MIX_B), cur))
    args.append(view(q))
    for t in (k, v):
        if use_prev:
            in_specs.append(pl.BlockSpec((None, BLOCK, n_res * kw), prev))
            args.append(view(t))
        in_specs.append(pl.BlockSpec((None, n_qblk * BLOCK, n_res * kw), cur))
        args.append(view(t))
    out_specs = [pl.BlockSpec((None, n_qblk * BLOCK, n_res * MIX_B), cur)]
    out_shape = [jax.ShapeDtypeStruct((nbatch, length, dil * MIX_B), BF16)]
    if want_lse:
        out_specs.append(pl.BlockSpec((None, n_res, n_qblk * BLOCK, N_HEADS_B),
                                      lambda b, r, i: (b, r, i, 0)))
        out_shape.append(jax.ShapeDtypeStruct((nbatch, dil, length, N_HEADS_B), F32))
    outs = pl.pallas_call(
        functools.partial(_attn_kernel, kv_shared=kv_shared, max_dist=max_dist, n_qblk=n_qblk,
                          n_res=n_res, use_prev=use_prev, has_sinks=sinks is not None,
                          want_lse=want_lse),
        grid=(nbatch, dil // n_res, steps),
        in_specs=in_specs, out_specs=out_specs, out_shape=out_shape,
        compiler_params=_params("parallel", "parallel", "arbitrary"),
        name=f"attn_d{dil}" + ("_swa" if kv_shared else ""),
    )(*args)
    o = outs[0].reshape(nbatch * length, dil * MIX_B)
    if not want_lse:
        return o, None
    lse = outs[1].transpose(0, 2, 1, 3).reshape(nbatch * seq, N_HEADS_B)
    return o, lse


def _expand_heads(w, width):
    head = lax.broadcasted_iota(I32, (1, width), 1) // HEAD_DIM
    out = jnp.zeros((w.shape[0], width), F32)
    for hd in range(w.shape[1]):
        out = jnp.where(head == hd, w[:, hd:hd + 1], out)
    return out


def _outproj_kernel(x_ref, mod_ref, oa_ref, ob1_ref, ob2_ref, ob3_ref, l1_ref, l2_ref, l3_ref,
                    goa_ref, gob_ref, wo_ref, gf_ref, h2_ref, h2t_ref, x1_ref, ob_scr):
    tt = x_ref.shape[0]
    gate_a = mod_ref[2:3, :]
    shift_m, scale_m = mod_ref[3:4, :], mod_ref[4:5, :]

    def token_major(ref, dil):
        if dil == 1:
            return ref[...].astype(F32)
        for r in range(dil):
            for j in range(N_PAIRS):
                c = r * MIX_B + j * LANES
                ob_scr[j, pl.ds(r, tt // dil, stride=dil), :] = ref[:, c:c + LANES].astype(F32)
        return jnp.concatenate([ob_scr[j] for j in range(N_PAIRS)], axis=1)

    l1, l2, l3 = l1_ref[...], l2_ref[...], l3_ref[...]
    mx = jnp.maximum(jnp.maximum(l1, l2), l3)
    e1, e2, e3 = jnp.exp(l1 - mx), jnp.exp(l2 - mx), jnp.exp(l3 - mx)
    den = e1 + e2 + e3
    dils = [dil for _, dil in DILATED_BRANCHES]
    ob = _expand_heads(e1 / den, MIX_B) * token_major(ob1_ref, dils[0])
    ob = ob + _expand_heads(e2 / den, MIX_B) * token_major(ob2_ref, dils[1])
    ob = ob + _expand_heads(e3 / den, MIX_B) * token_major(ob3_ref, dils[2])
    ob = _rms_rows(ob) * gob_ref[...]
    oa = _rms_rows(oa_ref[...].astype(F32)) * goa_ref[...]
    y = (jnp.dot(oa.astype(BF16), wo_ref[0:MIX_A, :], preferred_element_type=F32)
         + jnp.dot(ob.astype(BF16), wo_ref[MIX_A:MIX_A + MIX_B, :], preferred_element_type=F32))
    x1 = x_ref[...] + gate_a * y
    h2 = _rms_rows(x1) * gf_ref[...] * (1.0 + scale_m) + shift_m
    h2_ref[...] = h2
    _store_row_tiles(h2t_ref, h2)
    x1_ref[...] = x1


def _outproj(xf, mod, oa, obs, lses, goa, gob, wo_p, gf, seq):
    n, d = xf.shape
    tt = TT_PROJ
    tiles_per_seq = seq // tt
    tile = lambda w: pl.BlockSpec((tt, w), lambda i: (i, 0))
    full = lambda shape: pl.BlockSpec(shape, lambda i: (0,) * len(shape))
    dilated = [pl.BlockSpec((tt // dil, dil * MIX_B), lambda i: (i, 0)) for _, dil in DILATED_BRANCHES]
    return pl.pallas_call(
        _outproj_kernel,
        grid=(n // tt,),
        in_specs=[tile(d), pl.BlockSpec((None, 6, d), lambda i: (i // tiles_per_seq, 0, 0)),
                  tile(MIX_A), *dilated,
                  tile(N_HEADS_B), tile(N_HEADS_B), tile(N_HEADS_B),
                  full((1, MIX_A)), full((1, MIX_B)), full((MIX_A + MIX_B, d)), full((1, d))],
        out_specs=[tile(d), pl.BlockSpec((tt * d // LANES, LANES), lambda i: (i, 0)), tile(d)],
        out_shape=[jax.ShapeDtypeStruct((n, d), F32), jax.ShapeDtypeStruct((n * d // LANES, LANES), F32),
                   jax.ShapeDtypeStruct((n, d), F32)],
        scratch_shapes=[pltpu.VMEM((N_PAIRS, tt, LANES), F32)],
        compiler_params=_params("parallel"),
        name="outproj",
    )(xf, mod, oa, *obs, *lses, goa, gob, wo_p, gf)


def _router_kernel(h_ref, wrt_ref, bias_ref, e_ref, g_ref, cnt_ref):
    tt = h_ref.shape[0]
    logits = lax.dot_general(wrt_ref[...], h_ref[...].astype(BF16), (((1,), (1,)), ((), ())),
                             preferred_element_type=F32)
    scores = 1.0 / (1.0 + jnp.exp(-logits))
    biased = scores + bias_ref[...]
    ninf = -jnp.inf

    j32 = lax.broadcasted_iota(I32, (GROUP_SIZE, tt), 0).astype(F32)
    grp = []
    for g in range(N_GROUPS):
        bg = biased[g * GROUP_SIZE:(g + 1) * GROUP_SIZE, :]
        m1 = jnp.max(bg, axis=0, keepdims=True)
        i1 = jnp.min(jnp.where(bg == m1, j32, float(GROUP_SIZE)), axis=0, keepdims=True)
        m2 = jnp.max(jnp.where(j32 == i1, ninf, bg), axis=0, keepdims=True)
        grp.append(m1 + m2)
    grp = jnp.concatenate(grp, axis=0)
    g8 = lax.broadcasted_iota(I32, (N_GROUPS, tt), 0).astype(F32)
    chosen = jnp.zeros((N_GROUPS, tt), F32)
    for _ in range(TOPK_GROUPS):
        gm = jnp.max(grp, axis=0, keepdims=True)
        gi = jnp.min(jnp.where(grp == gm, g8, float(N_GROUPS)), axis=0, keepdims=True)
        hit = g8 == gi
        chosen = jnp.where(hit, 1.0, chosen)
        grp = jnp.where(hit, ninf, grp)
    masked = jnp.concatenate(
        [jnp.where(chosen[g:g + 1, :] > 0.0, biased[g * GROUP_SIZE:(g + 1) * GROUP_SIZE, :], ninf)
         for g in range(N_GROUPS)], axis=0)

    eio = lax.broadcasted_iota(I32, (N_EXPERTS, tt), 0).astype(F32)
    picked = jnp.zeros((N_EXPERTS, tt), F32)
    es, gs = [], []
    for _ in range(TOP_K):
        m = jnp.max(masked, axis=0, keepdims=True)
        idx = jnp.min(jnp.where(masked == m, eio, float(N_EXPERTS)), axis=0, keepdims=True)
        hit = eio == idx
        gs.append(jnp.sum(jnp.where(hit, scores, 0.0), axis=0, keepdims=True))
        es.append(idx)
        picked = jnp.where(hit, 1.0, picked)
        masked = jnp.where(hit, ninf, masked)
    gates = jnp.concatenate(gs, axis=0)
    e_ref[...] = jnp.concatenate(es, axis=0).astype(I32)
    g_ref[...] = gates / jnp.sum(gates, axis=0, keepdims=True) * ROUTED_SCALE

    @pl.when(pl.program_id(0) == 0)
    def _():
        cnt_ref[...] = jnp.zeros_like(cnt_ref)
    cnt_ref[...] += jnp.sum(picked, axis=1, keepdims=True)


def _router(h2, wrt, bias_col):
    n, d = h2.shape
    tt = TT_ROUTE
    return pl.pallas_call(
        _router_kernel,
        grid=(n // tt,),
        in_specs=[pl.BlockSpec((tt, d), lambda i: (i, 0)),
                  pl.BlockSpec((N_EXPERTS, d), lambda i: (0, 0)),
                  pl.BlockSpec((N_EXPERTS, 1), lambda i: (0, 0))],
        out_specs=[pl.BlockSpec((TOP_K, tt), lambda i: (0, i)),
                   pl.BlockSpec((TOP_K, tt), lambda i: (0, i)),
                   pl.BlockSpec((N_EXPERTS, 1), lambda i: (0, 0))],
        out_shape=[jax.ShapeDtypeStruct((TOP_K, n), I32),
                   jax.ShapeDtypeStruct((TOP_K, n), F32),
                   jax.ShapeDtypeStruct((N_EXPERTS, 1), F32)],
        compiler_params=_params("arbitrary"),
        name="router",
    )(h2, wrt, bias_col)


def _rank_kernel(e_ref, pstart_ref, tri_ref, dest_ref, carry_ref):
    tt = e_ref.shape[1]

    @pl.when(pl.program_id(0) == 0)
    def _():
        carry_ref[...] = pstart_ref[...]

    e = e_ref[...]
    eio = lax.broadcasted_iota(I32, (N_EXPERTS, tt), 0)
    mask = jnp.zeros((N_EXPERTS, tt), F32)
    for k in range(TOP_K):
        mask = jnp.where(eio == e[k:k + 1, :], 1.0, mask)
    incl = jnp.dot(mask.astype(BF16), tri_ref[...], preferred_element_type=F32)
    pos = incl - mask + carry_ref[...]
    dest = [jnp.sum(jnp.where(eio == e[k:k + 1, :], pos, 0.0), axis=0, keepdims=True)
            for k in range(TOP_K)]
    dest_ref[...] = jnp.concatenate(dest, axis=0).astype(I32)
    carry_ref[...] += incl[:, tt - 1:tt]


def _rank(top_e_t, pstart_col, tri):
    n = top_e_t.shape[1]
    tt = TT_ROUTE
    return pl.pallas_call(
        _rank_kernel,
        grid=(n // tt,),
        in_specs=[pl.BlockSpec((TOP_K, tt), lambda i: (0, i)),
                  pl.BlockSpec((N_EXPERTS, 1), lambda i: (0, 0)),
                  pl.BlockSpec((tt, tt), lambda i: (0, 0))],
        out_specs=pl.BlockSpec((TOP_K, tt), lambda i: (0, i)),
        out_shape=jax.ShapeDtypeStruct((TOP_K, n), I32),
        scratch_shapes=[pltpu.VMEM((N_EXPERTS, 1), F32)],
        compiler_params=_params("arbitrary"),
        name="rank",
    )(top_e_t, pstart_col, tri)


SC_CORES, SC_SUBCORES, SC_LANES = 2, 16, 16
SC_WORKERS = SC_CORES * SC_SUBCORES
INVERT_CHUNK = 8192


def _invert(dest_flat, n_tokens, rows):
    per = rows // SC_WORKERS
    n_assign = dest_flat.shape[0]
    assert rows % (SC_WORKERS * SC_LANES) == 0 and n_assign % INVERT_CHUNK == 0
    assert n_tokens & (n_tokens - 1) == 0

    @functools.partial(
        pl.kernel, mesh=plsc.VectorSubcoreMesh(core_axis_name="c", subcore_axis_name="s"),
        out_type=jax.ShapeDtypeStruct((rows,), I32),
        scratch_types=[pltpu.VMEM((INVERT_CHUNK,), I32), pltpu.VMEM((per,), I32)],
        compiler_params=pltpu.CompilerParams(needs_layout_passes=False))
    def invert(dest_hbm, out_hbm, staged, local):
        base = (lax.axis_index("s") * SC_CORES + lax.axis_index("c")) * per
        lane = lax.iota(I32, SC_LANES)

        @pl.loop(0, per, step=SC_LANES)
        def _(i):
            local[pl.ds(i, SC_LANES)] = (base + i + lane) & (n_tokens - 1)

        @pl.loop(0, n_assign // INVERT_CHUNK)
        def _(c):
            pltpu.sync_copy(dest_hbm.at[pl.ds(c * INVERT_CHUNK, INVERT_CHUNK)], staged)

            @pl.loop(0, INVERT_CHUNK, step=SC_LANES)
            def _(i):
                rel = staged[pl.ds(i, SC_LANES)] - base
                mine = (rel >= 0) & (rel < per)
                tok = (c * INVERT_CHUNK + i + lane) & (n_tokens - 1)
                plsc.store_scatter(local, [jnp.where(mine, rel, 0)], tok, mask=mine)

        pltpu.sync_copy(local, out_hbm.at[pl.ds(base, per)])

    return invert(dest_flat)


def _experts_kernel(rank_ref, eor_ref, nv_ref, nr_ref, tok_hbm, h_ref, wg_hbm, wu_hbm, wd_hbm,
                    ys_ref, *scratch):
    xbufs = scratch[:GATHER_DEPTH]
    wgf, wuf, wdf, wgb, wub, wdb, tok_s, sem, tok_sem, w_sem = scratch[GATHER_DEPTH:]
    step = pl.program_id(0)
    nv, n_ranks = nv_ref[0], nr_ref[0]
    per = wgb.shape[0] // LANES
    depth, bm = len(xbufs), xbufs[0].shape[0] // per
    ahead = depth - 1
    chunk = tok_s.shape[0] // 2
    cb = chunk // bm
    n_chunks = tok_hbm.shape[0] // chunk
    n_blocks = n_chunks * cb

    def tok_copy(c):
        return pltpu.make_async_copy(
            tok_hbm.at[pl.ds(pl.multiple_of(c * chunk, chunk), chunk)],
            tok_s.at[pl.ds(pl.multiple_of((c % 2) * chunk, chunk), chunk)], tok_sem.at[c % 2])

    def gather(block, ring, unrolled):
        block = jnp.minimum(block, n_blocks - 1)
        base = ((block // cb) % 2) * chunk + (block % cb) * bm
        buf, buf_sem = xbufs[ring], sem.at[ring]

        def issue(i, carry=None):
            _tile_copy(h_ref, tok_s[base + i], buf, i, per, buf_sem).start()
            return carry

        if unrolled:
            for i in range(bm):
                _tile_copy(h_ref, tok_s[base + i], buf, i, per, buf_sem).start(
                    priority=int(i % ROW_QUEUE_SPLIT != 0))
        else:
            lax.fori_loop(0, bm, issue, 0, unroll=8)

    def wait_rows(ring):
        pltpu.make_async_copy(h_ref.at[pl.ds(0, bm * per)], xbufs[ring], sem.at[ring]).wait()

    def weight_copies(r):
        e, s = eor_ref[r], r % WEIGHT_DEPTH
        return [pltpu.make_async_copy(src.at[e], dst.at[s], w_sem.at[s])
                for src, dst in ((wg_hbm, wgf), (wu_hbm, wuf), (wd_hbm, wdf))]

    @pl.when(step == 0)
    def _():
        tok_copy(0).start()
        tok_copy(0).wait()
        tok_copy(1).start()
        for r in range(WEIGHT_DEPTH - 1):
            @pl.when(r < n_ranks)
            def _():
                for cp in weight_copies(r):
                    cp.start()
        for b in range(ahead):
            gather(b, b, unrolled=False)

    def block(j, ring, out_row0):
        first = j + ahead
        c_need = first // cb

        @pl.when(jnp.logical_and(jnp.logical_and(j > 0, first % cb == 0), c_need < n_chunks))
        def _():
            tok_copy(c_need).wait()

            @pl.when(c_need + 1 < n_chunks)
            def _():
                tok_copy(c_need + 1).start()

        jb = jnp.minimum(j, n_blocks - 1)
        rank = rank_ref[jb]
        fresh = jnp.logical_or(j == 0, rank != rank_ref[jnp.maximum(jb - 1, 0)])

        @pl.when(jnp.logical_and(j < nv, fresh))
        def _():
            for cp in weight_copies(rank):
                cp.wait()
            s = rank % WEIGHT_DEPTH
            wgb[...] = wgf[s].astype(BF16)
            wub[...] = wuf[s].astype(BF16)
            wdb[...] = wdf[s].astype(BF16)
            nxt = rank + WEIGHT_DEPTH - 1

            @pl.when(nxt < n_ranks)
            def _():
                for cp in weight_copies(nxt):
                    cp.start()

        @pl.when(j < nv)
        def _():
            wait_rows(ring)
            gather(j + ahead, (ring + ahead) % depth, unrolled=True)
            xb = _load_row_tiles(xbufs[ring], bm, per, BF16)
            act = (_silu(jnp.dot(xb, wgb[...], preferred_element_type=F32))
                   * jnp.dot(xb, wub[...], preferred_element_type=F32))
            _store_row_tiles(ys_ref, jnp.dot(act.astype(BF16), wdb[...], preferred_element_type=F32),
                             out_row0)

        @pl.when(jnp.logical_and(j >= nv, j < nv + ahead))
        def _():
            wait_rows(ring)

    half = BLOCKS_PER_STEP
    for parity in range(depth // half):
        @pl.when(step % (depth // half) == parity)
        def _():
            for h in range(half):
                block(step * half + h, parity * half + h, h * bm * per)


def _experts(block_rank, expert_of_rank, n_valid, n_ranks, row_tok, h2_tiles, wg, wu, wd):
    rows = row_tok.shape[0]
    d = wg.shape[1]
    per = d // LANES
    bm = EXPERT_ROWS
    n_blocks = rows // bm
    assert n_blocks % TOK_CHUNK_BLOCKS == 0 and n_blocks // TOK_CHUNK_BLOCKS >= 2
    assert GATHER_DEPTH - 1 < TOK_CHUNK_BLOCKS
    bps = BLOCKS_PER_STEP
    assert GATHER_DEPTH % bps == 0 and n_blocks % bps == 0
    f = wg.shape[2]
    hbm = pl.BlockSpec(memory_space=pl.ANY)
    return pl.pallas_call(
        _experts_kernel,
        grid_spec=pltpu.PrefetchScalarGridSpec(
            num_scalar_prefetch=4,
            grid=((n_blocks + GATHER_DEPTH) // bps,),
            in_specs=[hbm, hbm, hbm, hbm, hbm],
            out_specs=pl.BlockSpec((bps * bm * per, LANES),
                                   lambda s, rk, eor, nv, nr: (jnp.minimum(s, (nv[0] - 1) // bps), 0)),
            scratch_shapes=[pltpu.VMEM((bm * per, LANES), F32)] * GATHER_DEPTH + [
                            pltpu.VMEM((WEIGHT_DEPTH, d, f), F32), pltpu.VMEM((WEIGHT_DEPTH, d, f), F32),
                            pltpu.VMEM((WEIGHT_DEPTH, f, d), F32),
                            pltpu.VMEM((d, f), BF16), pltpu.VMEM((d, f), BF16),
                            pltpu.VMEM((f, d), BF16),
                            pltpu.SMEM((2 * TOK_CHUNK_BLOCKS * bm,), I32),
                            pltpu.SemaphoreType.DMA((GATHER_DEPTH,)), pltpu.SemaphoreType.DMA((2,)),
                            pltpu.SemaphoreType.DMA((WEIGHT_DEPTH,))]),
        out_shape=jax.ShapeDtypeStruct((rows * per, LANES), F32),
        compiler_params=_params("arbitrary"),
        name="experts",
    )(block_rank, expert_of_rank, n_valid, n_ranks, row_tok, h2_tiles, wg, wu, wd)


def _shared_kernel(x1_ref, h2_ref, mod_ref, wgs_ref, wus_ref, wds_ref, base_ref):
    hb = h2_ref[...].astype(BF16)
    act = (_silu(jnp.dot(hb, wgs_ref[...], preferred_element_type=F32))
           * jnp.dot(hb, wus_ref[...], preferred_element_type=F32))
    shared = jnp.dot(act.astype(BF16), wds_ref[...], preferred_element_type=F32)
    base_ref[...] = x1_ref[...] + mod_ref[5:6, :] * shared


def _shared(x1, h2, mod, wgs, wus, wds, seq):
    n, d = x1.shape
    tt = TT_PROJ
    tiles_per_seq = seq // tt
    sd = wgs.shape[1]
    tile = pl.BlockSpec((tt, d), lambda i: (i, 0))
    full = lambda shape: pl.BlockSpec(shape, lambda i: (0,) * len(shape))
    return pl.pallas_call(
        _shared_kernel,
        grid=(n // tt,),
        in_specs=[tile, tile, pl.BlockSpec((None, 6, d), lambda i: (i // tiles_per_seq, 0, 0)),
                  full((d, sd)), full((d, sd)), full((sd, d))],
        out_specs=tile,
        out_shape=jax.ShapeDtypeStruct((n, d), F32),
        compiler_params=_params("parallel"),
        name="shared",
    )(x1, h2, mod, wgs, wus, wds)


def _combine_kernel(dest_ref, gates_ref, base_ref, mod_ref, ys_ref, out_ref, buf, sem):
    tt, d = base_ref.shape
    per = d // LANES
    for t in range(tt):
        for k in range(TOP_K):
            _tile_copy(ys_ref, dest_ref[k, t], buf.at[k], t, per, sem).start(priority=k % 2)
    for k in range(TOP_K):
        pltpu.make_async_copy(ys_ref.at[pl.ds(0, tt * per)], buf.at[k], sem).wait()
    gates = gates_ref[...]
    routed = gates[:, 0:1] * _load_row_tiles(buf.at[0], tt, per, F32)
    for k in range(1, TOP_K):
        routed = routed + gates[:, k:k + 1] * _load_row_tiles(buf.at[k], tt, per, F32)
    out_ref[...] = base_ref[...] + mod_ref[5:6, :] * routed


def _combine(dest_t, gates, base, mod, ys, seq):
    n, d = base.shape
    tt = TT_COMBINE
    tiles_per_seq = seq // tt
    tile = pl.BlockSpec((tt, d), lambda i: (i, 0))
    return pl.pallas_call(
        _combine_kernel,
        grid=(n // tt,),
        in_specs=[pl.BlockSpec((TOP_K, tt), lambda i: (0, i), memory_space=pltpu.SMEM),
                  pl.BlockSpec((tt, TOP_K), lambda i: (i, 0)),
                  tile,
                  pl.BlockSpec((None, 6, d), lambda i: (i // tiles_per_seq, 0, 0)),
                  pl.BlockSpec(memory_space=pl.ANY)],
        out_specs=tile,
        out_shape=jax.ShapeDtypeStruct((n, d), F32),
        scratch_shapes=[pltpu.VMEM((TOP_K, tt * d // LANES, LANES), F32), pltpu.SemaphoreType.DMA],
        compiler_params=_params("arbitrary"),
        name="combine",
    )(dest_t, gates, base, mod, ys)


def _layer(x, mod, pos, rope, p):
    nbatch, seq, d = x.shape
    n = nbatch * seq
    xf = x.reshape(n, d)
    invf, spread, bd = rope

    perm = np.concatenate([np.arange(h * HEAD_DIM, (h + 1) * HEAD_DIM) for h in PAIR_ORDER_A])
    w_in = p["w_in"]
    w_in_p = jnp.concatenate([w_in[:, :MIX_A][:, perm], w_in[:, MIX_A:]], axis=1).astype(BF16)
    ones = lambda w: jnp.ones((w,), F32)
    qscale = HEAD_DIM ** -0.5
    gcol = jnp.concatenate([jnp.tile(p["g_q_a"], N_HEADS_A) * qscale, jnp.tile(p["g_k_a"], N_KV_A),
                            ones(KV_A), jnp.tile(p["g_q_b"], N_HEADS_B) * qscale,
                            jnp.tile(p["g_k_b"], N_HEADS_B), ones(MIX_B)]).reshape(1, IN_WIDTH)
    proj = _inproj(xf, mod, pos, p["g_norm_mix"].reshape(1, d), w_in_p, gcol, invf, spread, bd, seq)
    qa, ka, va = proj[:3]
    qkv_b = {1: proj[3:6]}
    for j, dil in enumerate(DILS):
        qkv_b[dil] = [proj[6 + t * len(DILS) + j] for t in range(3)]

    sinks_p = p["sinks_a"][np.array(PAIR_ORDER_A)]
    oa, _ = _attention(qa, ka, va, nbatch=nbatch, seq=seq, dil=1, max_dist=WINDOW_A - 1,
                       kv_shared=True, sinks=sinks_p, want_lse=False)
    obs, lses = [], []
    for window, dil in DILATED_BRANCHES:
        o, lse = _attention(*qkv_b[dil], nbatch=nbatch, seq=seq, dil=dil, max_dist=window // dil,
                            kv_shared=False, want_lse=True)
        obs.append(o)
        lses.append(lse)

    goa = p["g_out_a"][perm].reshape(1, MIX_A)
    w_out = p["w_out"]
    wo_p = jnp.concatenate([w_out[:MIX_A][perm], w_out[MIX_A:]], axis=0).astype(BF16)
    h2, h2_tiles, x1 = _outproj(xf, mod, oa, obs, lses, goa, p["g_out_b"].reshape(1, MIX_B), wo_p,
                                p["g_norm_ffn"].reshape(1, d), seq)

    top_e_t, gates_t, counts = _router(h2, p["w_router"].T.astype(BF16),
                                       p["router_bias"].reshape(N_EXPERTS, 1))
    bm = EXPERT_ROWS
    counts = counts.reshape(N_EXPERTS).astype(I32)
    padded = (counts + bm - 1) // bm * bm
    pends = jnp.cumsum(padded)
    pstarts = pends - padded
    rows = n * TOP_K + N_EXPERTS * bm
    n_blocks = rows // bm
    n_valid = (pends[-1] // bm).astype(I32)
    blk = jnp.minimum(jnp.arange(n_blocks, dtype=I32), n_valid - 1)
    block_e = jnp.sum((pends[None, :] <= (blk * bm)[:, None]).astype(I32), axis=1)
    block_e = jnp.minimum(block_e, N_EXPERTS - 1)
    nonempty = counts > 0
    rank_of_e = jnp.cumsum(nonempty.astype(I32)) - 1
    e_ids = jnp.arange(N_EXPERTS, dtype=I32)
    block_rank = jnp.sum(jnp.where(block_e[:, None] == e_ids[None, :], rank_of_e[None, :], 0), axis=1)
    hit = (rank_of_e[None, :] == jnp.arange(n_blocks, dtype=I32)[:, None]) & nonempty[None, :]
    expert_of_rank = jnp.sum(jnp.where(hit, e_ids[None, :], 0), axis=1)
    n_ranks = jnp.sum(nonempty.astype(I32)).reshape(1)

    tri =(np.arange(TT_ROUTE)[:, None] <= np.arange(TT_ROUTE)[None, :])
    dest_t = _rank(top_e_t, pstarts.astype(F32).reshape(N_EXPERTS, 1), jnp.asarray(tri, BF16))
    n_valid = n_valid.reshape(1)
    row_tok = _invert(dest_t.reshape(n * TOP_K), n, rows)
    base = _shared(x1, h2, mod, p["w_gate_s"].astype(BF16), p["w_up_s"].astype(BF16),
                   p["w_down_s"].astype(BF16), seq)
    ys = _experts(block_rank.astype(I32), expert_of_rank.astype(I32), n_valid, n_ranks, row_tok, h2_tiles,
                  p["w_gate_e"], p["w_up_e"], p["w_down_e"])
    out = _combine(dest_t, gates_t.T, base, mod, ys, seq)
    return out.reshape(nbatch, seq, d)


def kernel(x, c, positions, w_ada, b_ada, g_norm_mix, w_in, g_q_a, g_k_a, sinks_a, g_q_b, g_k_b,
           g_out_a, g_out_b, w_out, g_norm_ffn, w_router, router_bias, w_gate_e, w_up_e, w_down_e,
           w_gate_s, w_up_s, w_down_s):
    nbatch, seq, d = x.shape
    depth = w_ada.shape[0]
    params = dict(g_norm_mix=g_norm_mix, w_in=w_in, g_q_a=g_q_a, g_k_a=g_k_a, sinks_a=sinks_a,
                  g_q_b=g_q_b, g_k_b=g_k_b, g_out_a=g_out_a, g_out_b=g_out_b, w_out=w_out,
                  g_norm_ffn=g_norm_ffn, w_router=w_router, router_bias=router_bias,
                  w_gate_e=w_gate_e, w_up_e=w_up_e, w_down_e=w_down_e, w_gate_s=w_gate_s,
                  w_up_s=w_up_s, w_down_s=w_down_s)
    j = np.arange(LANES) % HEAD_DIM
    invf = (ROPE_THETA ** (-jnp.arange(0, ROT_DIM, 2, dtype=F32) / ROT_DIM)).reshape(ROT_DIM // 2, 1)
    spread = jnp.asarray((j[None, :] < ROT_DIM)
                         & (j[None, :] % (ROT_DIM // 2) == np.arange(ROT_DIM // 2)[:, None]), BF16)
    bd = jnp.asarray((np.arange(LANES)[:, None] // HEAD_DIM) == (np.arange(LANES)[None, :] // HEAD_DIM),
                     BF16)
    pos = positions.reshape(1, nbatch * seq).astype(I32)
    for l in range(depth):
        mod = _adaln(c.astype(F32), w_ada[l], b_ada[l]).reshape(nbatch, 6, d)
        x = _layer(x, mod, pos, (invf, spread, bd), {k: v[l] for k, v in params.items()})
    return x
```

```python
import functools

import numpy as np
import jax
import jax.numpy as jnp
from jax import lax
from jax.experimental import pallas as pl
from jax.experimental.pallas import tpu as pltpu
from jax.experimental.pallas import tpu_sc as plsc

F32 = jnp.float32
BF16 = jnp.bfloat16
I32 = jnp.int32

HEAD_DIM = 64
N_HEADS_A = 8
N_KV_A = 2
WINDOW_A = 128
N_HEADS_B = 8
DILATED_BRANCHES = ((128, 1), (512, 4), (2048, 16))
DILS = tuple(dil for _, dil in DILATED_BRANCHES if dil > 1)
BLOCK = 128
ROT_DIM = HEAD_DIM // 4
ROPE_THETA = 500000.0
MIX_A = N_HEADS_A * HEAD_DIM
KV_A = N_KV_A * HEAD_DIM
MIX_B = N_HEADS_B * HEAD_DIM
N_EXPERTS = 256
TOP_K = 8
N_GROUPS = 8
TOPK_GROUPS = 4
GROUP_SIZE = N_EXPERTS // N_GROUPS
ROUTED_SCALE = 2.5
EPS = 1e-6

LANES = 128
N_PAIRS = MIX_A // LANES
NEG = -1e30
VMEM_LIMIT = 48 * 1024 * 1024

TT_PROJ = 512
ATTN_BLOCKS_PER_STEP = 8
TT_ROUTE = 512
TT_COMBINE = 512
EXPERT_ROWS = 256
TOK_CHUNK_BLOCKS = 32
GATHER_DEPTH = 4
WEIGHT_DEPTH = 3
BLOCKS_PER_STEP = 2
ROW_QUEUE_SPLIT = 2

PAIR_ORDER_A = tuple(h for p in range(N_PAIRS) for h in (p, p + N_HEADS_A // N_KV_A))


def _params(*sem):
    return pltpu.CompilerParams(dimension_semantics=sem, vmem_limit_bytes=VMEM_LIMIT)


def _silu(t):
    return t / (1.0 + jnp.exp(-t))


def _rms_rows(t):
    return t * lax.rsqrt(jnp.mean(t * t, axis=-1, keepdims=True) + EPS)


def _store_row_tiles(ref, value, row0=0):
    rows, d = value.shape
    per = d // LANES
    for s in range(per):
        ref[pl.ds(row0 + s, rows, stride=per), :] = value[:, s * LANES:(s + 1) * LANES]


def _load_row_tiles(ref, rows, per, dtype):
    return jnp.concatenate([ref[pl.ds(s, rows, stride=per), :].astype(dtype) for s in range(per)], axis=1)


def _tile_copy(src_ref, src_row, dst_ref, dst_row, per, sem):
    start = lambda row: row * per if isinstance(row, int) else pl.multiple_of(row * per, per)
    return pltpu.make_async_copy(src_ref.at[pl.ds(start(src_row), per)],
                                 dst_ref.at[pl.ds(start(dst_row), per)], sem)


def _ada_kernel(c_ref, w_ref, b_ref, o_ref):
    cond = _silu(c_ref[...])
    o_ref[...] = jnp.dot(cond.astype(BF16), w_ref[...].astype(BF16),
                         preferred_element_type=F32) + b_ref[...]


def _adaln(c, w_ada, b_ada):
    nb, d = c.shape
    width = w_ada.shape[1]
    tn = 1024
    return pl.pallas_call(
        _ada_kernel,
        grid=(width // tn,),
        in_specs=[pl.BlockSpec((nb, d), lambda j: (0, 0)),
                  pl.BlockSpec((d, tn), lambda j: (0, j)),
                  pl.BlockSpec((1, tn), lambda j: (0, j))],
        out_specs=pl.BlockSpec((nb, tn), lambda j: (0, j)),
        out_shape=jax.ShapeDtypeStruct((nb, width), F32),
        compiler_params=_params("arbitrary"),
        name="adaln",
    )(c, w_ada, b_ada.reshape(1, width))


COL_QA, COL_KA, COL_VA = 0, MIX_A, MIX_A + KV_A
COL_QB = MIX_A + 2 * KV_A
COL_KB, COL_VB = COL_QB + MIX_B, COL_QB + 2 * MIX_B
IN_WIDTH = COL_VB + MIX_B


def _inproj_kernel(x_ref, mod_ref, pos_ref, gn_ref, w_ref, gcol_ref, invf_ref, spread_ref, bd_ref,
                   qa_ref, ka_ref, va_ref, qb_ref, kb_ref, vb_ref, *rest):
    n_dil = len(DILS)
    dil_refs = [rest[i * n_dil:(i + 1) * n_dil] for i in range(3)]
    qb_scr, kb_scr, vb_scr = rest[3 * n_dil:]
    tt = x_ref.shape[0]
    shift, scale = mod_ref[0:1, :], mod_ref[1:2, :]
    h = _rms_rows(x_ref[...]) * gn_ref[...] * (1.0 + scale) + shift
    proj = jnp.dot(h.astype(BF16), w_ref[...], preferred_element_type=F32)

    ang = pos_ref[...].astype(F32) * invf_ref[...]
    lane = lax.broadcasted_iota(I32, (1, LANES), 1) % HEAD_DIM

    def spread(table):
        hi = table.astype(BF16)
        lo = (table - hi.astype(F32)).astype(BF16)
        to_lanes = lambda a: lax.dot_general(a, spread_ref[...], (((0,), (0,)), ((), ())),
                                             preferred_element_type=F32)
        return to_lanes(hi) + to_lanes(lo)

    cs = spread(jnp.cos(ang)) + jnp.where(lane < ROT_DIM, 0.0, 1.0)
    sn = spread(jnp.sin(ang))
    s_lo = jnp.where(lane < ROT_DIM // 2, -sn, 0.0)
    s_hi = jnp.where((lane >= ROT_DIM // 2) & (lane < ROT_DIM), sn, 0.0)
    bd = bd_ref[...]

    def norm_rope(col0, width, out_ref, scr=None):
        for j in range(width // LANES):
            c = col0 + j * LANES
            t = proj[:, c:c + LANES]
            sq = t * t
            hi = sq.astype(BF16)
            lo = (sq - hi.astype(F32)).astype(BF16)
            ss = (jnp.dot(hi, bd, preferred_element_type=F32)
                  + jnp.dot(lo, bd, preferred_element_type=F32))
            t = t * lax.rsqrt(ss * (1.0 / HEAD_DIM) + EPS) * gcol_ref[:, c:c + LANES]
            t = (t * cs + pltpu.roll(t, LANES - ROT_DIM // 2, 1) * s_lo
                 + pltpu.roll(t, ROT_DIM // 2, 1) * s_hi)
            out_ref[:, j * LANES:(j + 1) * LANES] = t.astype(BF16)
            if scr is not None:
                scr[j] = t

    norm_rope(COL_QA, MIX_A, qa_ref)
    norm_rope(COL_KA, KV_A, ka_ref)
    norm_rope(COL_QB, MIX_B, qb_ref, qb_scr)
    norm_rope(COL_KB, MIX_B, kb_ref, kb_scr)
    va_ref[...] = proj[:, COL_VA:COL_VA + KV_A].astype(BF16)
    vb_ref[...] = proj[:, COL_VB:COL_VB + MIX_B].astype(BF16)
    for j in range(N_PAIRS):
        vb_scr[j] = proj[:, COL_VB + j * LANES:COL_VB + (j + 1) * LANES]
    for scr, outs in zip((qb_scr, kb_scr, vb_scr), dil_refs):
        for dil, out in zip(DILS, outs):
            for r in range(dil):
                for j in range(N_PAIRS):
                    c = r * MIX_B + j * LANES
                    out[:, c:c + LANES] = scr[j, pl.ds(r, tt // dil, stride=dil), :].astype(BF16)


def _inproj(xf, mod, pos, g_norm, w_in_p, gcol, invf, spread, bd, seq):
    n, d = xf.shape
    tt = TT_PROJ
    tiles_per_seq = seq // tt
    shapes = [(n, w) for w in (MIX_A, KV_A, KV_A, MIX_B, MIX_B, MIX_B)]
    shapes += [(n // dil, dil * MIX_B) for _ in range(3) for dil in DILS]
    full = lambda shape: pl.BlockSpec(shape, lambda i: (0,) * len(shape))
    return pl.pallas_call(
        _inproj_kernel,
        grid=(n // tt,),
        in_specs=[pl.BlockSpec((tt, d), lambda i: (i, 0)),
                  pl.BlockSpec((None, 6, d), lambda i: (i // tiles_per_seq, 0, 0)),
                  pl.BlockSpec((1, tt), lambda i: (0, i)),
                  full((1, d)), full((d, IN_WIDTH)), full((1, IN_WIDTH)),
                  full((ROT_DIM // 2, 1)), full((ROT_DIM // 2, LANES)), full((LANES, LANES))],
        out_specs=[pl.BlockSpec((tt * r // n, w), lambda i: (i, 0)) for r, w in shapes],
        out_shape=[jax.ShapeDtypeStruct(s, BF16) for s in shapes],
        scratch_shapes=[pltpu.VMEM((N_PAIRS, tt, LANES), F32)] * 3,
        compiler_params=_params("parallel"),
        name="inproj",
    )(xf, mod, pos, g_norm, w_in_p, gcol, invf, spread, bd)


def _attn_kernel(*refs, kv_shared, max_dist, n_qblk, n_res, use_prev, has_sinks, want_lse):
    refs = list(refs)
    sink_ref = refs.pop(0) if has_sinks else None
    q_ref = refs.pop(0)
    kp_ref = refs.pop(0) if use_prev else None
    kc_ref = refs.pop(0)
    vp_ref = refs.pop(0) if use_prev else None
    vc_ref = refs.pop(0)
    o_ref = refs.pop(0)
    lse_ref = refs.pop(0) if want_lse else None

    first_step = pl.program_id(2) == 0
    nq = 2 * BLOCK
    kw = LANES if kv_shared else MIX_B

    def band(nk):
        qpos = lax.broadcasted_iota(I32, (nq, nk), 0) % BLOCK
        kpos = lax.broadcasted_iota(I32, (nq, nk), 1)
        dist = qpos + (nk - BLOCK) - kpos
        return (dist >= 0) & (dist <= max_dist), kpos

    band2, kpos2 = band(2 * BLOCK)
    band1, _ = band(BLOCK)
    lane = lax.broadcasted_iota(I32, (nq, LANES), 1)
    row = lax.broadcasted_iota(I32, (nq, LANES), 0)
    own_half = (lane < HEAD_DIM) == (row < BLOCK)
    left_lanes = lax.broadcasted_iota(I32, (BLOCK, LANES), 1) < HEAD_DIM
    lane8 = lax.broadcasted_iota(I32, (BLOCK, 2 * N_PAIRS), 1)

    for res in range(n_res):
        for qb in range(n_qblk):
            rows = slice(qb * BLOCK, (qb + 1) * BLOCK)
            lse_blk = jnp.zeros((BLOCK, 2 * N_PAIRS), F32)
            for p in range(N_PAIRS):
                cq = slice(res * MIX_B + p * LANES, res * MIX_B + (p + 1) * LANES)
                ck = slice(res * kw, res * kw + LANES) if kv_shared else slice(
                    res * kw + p * LANES, res * kw + (p + 1) * LANES)
                qp = q_ref[rows, cq]
                qs = jnp.concatenate([qp, qp], axis=0)
                qs = jnp.where(own_half, qs, jnp.zeros_like(qs))
                if qb > 0:
                    keys = slice((qb - 1) * BLOCK, (qb + 1) * BLOCK)
                    k, v, valid = kc_ref[keys, ck], vc_ref[keys, ck], band2
                elif use_prev:
                    k = jnp.concatenate([kp_ref[:, ck], kc_ref[rows, ck]], axis=0)
                    v = jnp.concatenate([vp_ref[:, ck], vc_ref[rows, ck]], axis=0)
                    valid = band2 & ((kpos2 >= BLOCK) | jnp.logical_not(first_step))
                else:
                    k, v, valid = kc_ref[rows, ck], vc_ref[rows, ck], band1
                s = lax.dot_general(qs, k, (((1,), (1,)), ((), ())), preferred_element_type=F32)
                s = jnp.where(valid, s, NEG)
                m = jnp.max(s, axis=-1, keepdims=True)
                if has_sinks:
                    rows1 = lax.broadcasted_iota(I32, (nq, 1), 0)
                    sink = jnp.where(rows1 < BLOCK, sink_ref[2 * p], sink_ref[2 * p + 1])
                    m = jnp.maximum(m, sink)
                e = jnp.exp(s - m)
                l = jnp.sum(e, axis=-1, keepdims=True)
                if has_sinks:
                    l = l + jnp.exp(sink - m)
                o = jnp.dot(e.astype(BF16), v, preferred_element_type=F32) / l
                o_ref[rows, cq] = jnp.where(left_lanes, o[:BLOCK], o[BLOCK:]).astype(BF16)
                if want_lse:
                    lse = m + jnp.log(l)
                    lse_blk = (lse_blk + jnp.where(lane8 == 2 * p, lse[:BLOCK], 0.0)
                               + jnp.where(lane8 == 2 * p + 1, lse[BLOCK:], 0.0))
            if want_lse:
                lse_ref[res, rows, :] = lse_blk


def _attention(q, k, v, *, nbatch, seq, dil, max_dist, kv_shared, sinks=None, want_lse):
    length = seq // dil
    nblk = length // BLOCK
    n_qblk = min(nblk, ATTN_BLOCKS_PER_STEP)
    n_res = min(dil, ATTN_BLOCKS_PER_STEP // n_qblk)
    steps = nblk // n_qblk
    use_prev = steps > 1
    kw = k.shape[1] // dil
    view = lambda t: t.reshape(nbatch, length, t.shape[1])
    cur = lambda b, r, i: (b, i, r)
    prev = lambda b, r, i: (b, jnp.maximum(i * n_qblk - 1, 0), r)
    in_specs, args = [], []
    if sinks is not None:
        in_specs.append(pl.BlockSpec(memory_space=pltpu.SMEM))
        args.append(sinks)
    in_specs.append(pl.BlockSpec((None, n_qblk * BLOCK, n_res * MIX_B), cur))
    args.append(view(q))
    for t in (k, v):
        if use_prev:
            in_specs.append(pl.BlockSpec((None, BLOCK, n_res * kw), prev))
            args.append(view(t))
        in_specs.append(pl.BlockSpec((None, n_qblk * BLOCK, n_res * kw), cur))
        args.append(view(t))
    out_specs = [pl.BlockSpec((None, n_qblk * BLOCK, n_res * MIX_B), cur)]
    out_shape = [jax.ShapeDtypeStruct((nbatch, length, dil * MIX_B), BF16)]
    if want_lse:
        out_specs.append(pl.BlockSpec((None, n_res, n_qblk * BLOCK, N_HEADS_B),
                                      lambda b, r, i: (b, r, i, 0)))
        out_shape.append(jax.ShapeDtypeStruct((nbatch, dil, length, N_HEADS_B), F32))
    outs = pl.pallas_call(
        functools.partial(_attn_kernel, kv_shared=kv_shared, max_dist=max_dist, n_qblk=n_qblk,
                          n_res=n_res, use_prev=use_prev, has_sinks=sinks is not None,
                          want_lse=want_lse),
        grid=(nbatch, dil // n_res, steps),
        in_specs=in_specs, out_specs=out_specs, out_shape=out_shape,
        compiler_params=_params("parallel", "parallel", "arbitrary"),
        name=f"attn_d{dil}" + ("_swa" if kv_shared else ""),
    )(*args)
    o = outs[0].reshape(nbatch * length, dil * MIX_B)
    if not want_lse:
        return o, None
    lse = outs[1].transpose(0, 2, 1, 3).reshape(nbatch * seq, N_HEADS_B)
    return o, lse


def _expand_heads(w, width):
    head = lax.broadcasted_iota(I32, (1, width), 1) // HEAD_DIM
    out = jnp.zeros((w.shape[0], width), F32)
    for hd in range(w.shape[1]):
        out = jnp.where(head == hd, w[:, hd:hd + 1], out)
    return out


def _outproj_kernel(x_ref, mod_ref, oa_ref, ob1_ref, ob2_ref, ob3_ref, l1_ref, l2_ref, l3_ref,
                    goa_ref, gob_ref, wo_ref, gf_ref, h2_ref, h2t_ref, x1_ref, ob_scr):
    tt = x_ref.shape[0]
    gate_a = mod_ref[2:3, :]
    shift_m, scale_m = mod_ref[3:4, :], mod_ref[4:5, :]

    def token_major(ref, dil):
        if dil == 1:
            return ref[...].astype(F32)
        for r in range(dil):
            for j in range(N_PAIRS):
                c = r * MIX_B + j * LANES
                ob_scr[j, pl.ds(r, tt // dil, stride=dil), :] = ref[:, c:c + LANES].astype(F32)
        return jnp.concatenate([ob_scr[j] for j in range(N_PAIRS)], axis=1)

    l1, l2, l3 = l1_ref[...], l2_ref[...], l3_ref[...]
    mx = jnp.maximum(jnp.maximum(l1, l2), l3)
    e1, e2, e3 = jnp.exp(l1 - mx), jnp.exp(l2 - mx), jnp.exp(l3 - mx)
    den = e1 + e2 + e3
    dils = [dil for _, dil in DILATED_BRANCHES]
    ob = _expand_heads(e1 / den, MIX_B) * token_major(ob1_ref, dils[0])
    ob = ob + _expand_heads(e2 / den, MIX_B) * token_major(ob2_ref, dils[1])
    ob = ob + _expand_heads(e3 / den, MIX_B) * token_major(ob3_ref, dils[2])
    ob = _rms_rows(ob) * gob_ref[...]
    oa = _rms_rows(oa_ref[...].astype(F32)) * goa_ref[...]
    y = (jnp.dot(oa.astype(BF16), wo_ref[0:MIX_A, :], preferred_element_type=F32)
         + jnp.dot(ob.astype(BF16), wo_ref[MIX_A:MIX_A + MIX_B, :], preferred_element_type=F32))
    x1 = x_ref[...] + gate_a * y
    h2 = _rms_rows(x1) * gf_ref[...] * (1.0 + scale_m) + shift_m
    h2_ref[...] = h2
    _store_row_tiles(h2t_ref, h2)
    x1_ref[...] = x1


def _outproj(xf, mod, oa, obs, lses, goa, gob, wo_p, gf, seq):
    n, d = xf.shape
    tt = TT_PROJ
    tiles_per_seq = seq // tt
    tile = lambda w: pl.BlockSpec((tt, w), lambda i: (i, 0))
    full = lambda shape: pl.BlockSpec(shape, lambda i: (0,) * len(shape))
    dilated = [pl.BlockSpec((tt // dil, dil * MIX_B), lambda i: (i, 0)) for _, dil in DILATED_BRANCHES]
    return pl.pallas_call(
        _outproj_kernel,
        grid=(n // tt,),
        in_specs=[tile(d), pl.BlockSpec((None, 6, d), lambda i: (i // tiles_per_seq, 0, 0)),
                  tile(MIX_A), *dilated,
                  tile(N_HEADS_B), tile(N_HEADS_B), tile(N_HEADS_B),
                  full((1, MIX_A)), full((1, MIX_B)), full((MIX_A + MIX_B, d)), full((1, d))],
        out_specs=[tile(d), pl.BlockSpec((tt * d // LANES, LANES), lambda i: (i, 0)), tile(d)],
        out_shape=[jax.ShapeDtypeStruct((n, d), F32), jax.ShapeDtypeStruct((n * d // LANES, LANES), F32),
                   jax.ShapeDtypeStruct((n, d), F32)],
        scratch_shapes=[pltpu.VMEM((N_PAIRS, tt, LANES), F32)],
        compiler_params=_params("parallel"),
        name="outproj",
    )(xf, mod, oa, *obs, *lses, goa, gob, wo_p, gf)


def _router_kernel(h_ref, wrt_ref, bias_ref, e_ref, g_ref, cnt_ref):
    tt = h_ref.shape[0]
    logits = lax.dot_general(wrt_ref[...], h_ref[...].astype(BF16), (((1,), (1,)), ((), ())),
                             preferred_element_type=F32)
    scores = 1.0 / (1.0 + jnp.exp(-logits))
    biased = scores + bias_ref[...]
    ninf = -jnp.inf

    j32 = lax.broadcasted_iota(I32, (GROUP_SIZE, tt), 0).astype(F32)
    grp = []
    for g in range(N_GROUPS):
        bg = biased[g * GROUP_SIZE:(g + 1) * GROUP_SIZE, :]
        m1 = jnp.max(bg, axis=0, keepdims=True)
        i1 = jnp.min(jnp.where(bg == m1, j32, float(GROUP_SIZE)), axis=0, keepdims=True)
        m2 = jnp.max(jnp.where(j32 == i1, ninf, bg), axis=0, keepdims=True)
        grp.append(m1 + m2)
    grp = jnp.concatenate(grp, axis=0)
    g8 = lax.broadcasted_iota(I32, (N_GROUPS, tt), 0).astype(F32)
    chosen = jnp.zeros((N_GROUPS, tt), F32)
    for _ in range(TOPK_GROUPS):
        gm = jnp.max(grp, axis=0, keepdims=True)
        gi = jnp.min(jnp.where(grp == gm, g8, float(N_GROUPS)), axis=0, keepdims=True)
        hit = g8 == gi
        chosen = jnp.where(hit, 1.0, chosen)
        grp = jnp.where(hit, ninf, grp)
    masked = jnp.concatenate(
        [jnp.where(chosen[g:g + 1, :] > 0.0, biased[g * GROUP_SIZE:(g + 1) * GROUP_SIZE, :], ninf)
         for g in range(N_GROUPS)], axis=0)

    eio = lax.broadcasted_iota(I32, (N_EXPERTS, tt), 0).astype(F32)
    picked = jnp.zeros((N_EXPERTS, tt), F32)
    es, gs = [], []
    for _ in range(TOP_K):
        m = jnp.max(masked, axis=0, keepdims=True)
        idx = jnp.min(jnp.where(masked == m, eio, float(N_EXPERTS)), axis=0, keepdims=True)
        hit = eio == idx
        gs.append(jnp.sum(jnp.where(hit, scores, 0.0), axis=0, keepdims=True))
        es.append(idx)
        picked = jnp.where(hit, 1.0, picked)
        masked = jnp.where(hit, ninf, masked)
    gates = jnp.concatenate(gs, axis=0)
    e_ref[...] = jnp.concatenate(es, axis=0).astype(I32)
    g_ref[...] = gates / jnp.sum(gates, axis=0, keepdims=True) * ROUTED_SCALE

    @pl.when(pl.program_id(0) == 0)
    def _():
        cnt_ref[...] = jnp.zeros_like(cnt_ref)
    cnt_ref[...] += jnp.sum(picked, axis=1, keepdims=True)


def _router(h2, wrt, bias_col):
    n, d = h2.shape
    tt = TT_ROUTE
    return pl.pallas_call(
        _router_kernel,
        grid=(n // tt,),
        in_specs=[pl.BlockSpec((tt, d), lambda i: (i, 0)),
                  pl.BlockSpec((N_EXPERTS, d), lambda i: (0, 0)),
                  pl.BlockSpec((N_EXPERTS, 1), lambda i: (0, 0))],
        out_specs=[pl.BlockSpec((TOP_K, tt), lambda i: (0, i)),
                   pl.BlockSpec((TOP_K, tt), lambda i: (0, i)),
                   pl.BlockSpec((N_EXPERTS, 1), lambda i: (0, 0))],
        out_shape=[jax.ShapeDtypeStruct((TOP_K, n), I32),
                   jax.ShapeDtypeStruct((TOP_K, n), F32),
                   jax.ShapeDtypeStruct((N_EXPERTS, 1), F32)],
        compiler_params=_params("arbitrary"),
        name="router",
    )(h2, wrt, bias_col)


def _rank_kernel(e_ref, pstart_ref, tri_ref, dest_ref, carry_ref):
    tt = e_ref.shape[1]

    @pl.when(pl.program_id(0) == 0)
    def _():
        carry_ref[...] = pstart_ref[...]

    e = e_ref[...]
    eio = lax.broadcasted_iota(I32, (N_EXPERTS, tt), 0)
    mask = jnp.zeros((N_EXPERTS, tt), F32)
    for k in range(TOP_K):
        mask = jnp.where(eio == e[k:k + 1, :], 1.0, mask)
    incl = jnp.dot(mask.astype(BF16), tri_ref[...], preferred_element_type=F32)
    pos = incl - mask + carry_ref[...]
    dest = [jnp.sum(jnp.where(eio == e[k:k + 1, :], pos, 0.0), axis=0, keepdims=True)
            for k in range(TOP_K)]
    dest_ref[...] = jnp.concatenate(dest, axis=0).astype(I32)
    carry_ref[...] += incl[:, tt - 1:tt]


def _rank(top_e_t, pstart_col, tri):
    n = top_e_t.shape[1]
    tt = TT_ROUTE
    return pl.pallas_call(
        _rank_kernel,
        grid=(n // tt,),
        in_specs=[pl.BlockSpec((TOP_K, tt), lambda i: (0, i)),
                  pl.BlockSpec((N_EXPERTS, 1), lambda i: (0, 0)),
                  pl.BlockSpec((tt, tt), lambda i: (0, 0))],
        out_specs=pl.BlockSpec((TOP_K, tt), lambda i: (0, i)),
        out_shape=jax.ShapeDtypeStruct((TOP_K, n), I32),
        scratch_shapes=[pltpu.VMEM((N_EXPERTS, 1), F32)],
        compiler_params=_params("arbitrary"),
        name="rank",
    )(top_e_t, pstart_col, tri)


SC_CORES, SC_SUBCORES, SC_LANES = 2, 16, 16
SC_WORKERS = SC_CORES * SC_SUBCORES
INVERT_CHUNK = 8192


def _invert(dest_flat, n_tokens, rows):
    per = rows // SC_WORKERS
    n_assign = dest_flat.shape[0]
    assert rows % (SC_WORKERS * SC_LANES) == 0 and n_assign % INVERT_CHUNK == 0
    assert n_tokens & (n_tokens - 1) == 0

    @functools.partial(
        pl.kernel, mesh=plsc.VectorSubcoreMesh(core_axis_name="c", subcore_axis_name="s"),
        out_type=jax.ShapeDtypeStruct((rows,), I32),
        scratch_types=[pltpu.VMEM((INVERT_CHUNK,), I32), pltpu.VMEM((per,), I32)],
        compiler_params=pltpu.CompilerParams(needs_layout_passes=False))
    def invert(dest_hbm, out_hbm, staged, local):
        base = (lax.axis_index("s") * SC_CORES + lax.axis_index("c")) * per
        lane = lax.iota(I32, SC_LANES)

        @pl.loop(0, per, step=SC_LANES)
        def _(i):
            local[pl.ds(i, SC_LANES)] = (base + i + lane) & (n_tokens - 1)

        @pl.loop(0, n_assign // INVERT_CHUNK)
        def _(c):
            pltpu.sync_copy(dest_hbm.at[pl.ds(c * INVERT_CHUNK, INVERT_CHUNK)], staged)

            @pl.loop(0, INVERT_CHUNK, step=SC_LANES)
            def _(i):
                rel = staged[pl.ds(i, SC_LANES)] - base
                mine = (rel >= 0) & (rel < per)
                tok = (c * INVERT_CHUNK + i + lane) & (n_tokens - 1)
                plsc.store_scatter(local, [jnp.where(mine, rel, 0)], tok, mask=mine)

        pltpu.sync_copy(local, out_hbm.at[pl.ds(base, per)])

    return invert(dest_flat)


def _experts_kernel(rank_ref, eor_ref, nv_ref, nr_ref, tok_hbm, h_ref, wg_hbm, wu_hbm, wd_hbm,
                    ys_ref, *scratch):
    xbufs = scratch[:GATHER_DEPTH]
    wgf, wuf, wdf, wgb, wub, wdb, tok_s, sem, tok_sem, w_sem = scratch[GATHER_DEPTH:]
    step = pl.program_id(0)
    nv, n_ranks = nv_ref[0], nr_ref[0]
    per = wgb.shape[0] // LANES
    depth, bm = len(xbufs), xbufs[0].shape[0] // per
    ahead = depth - 1
    chunk = tok_s.shape[0] // 2
    cb = chunk // bm
    n_chunks = tok_hbm.shape[0] // chunk
    n_blocks = n_chunks * cb

    def tok_copy(c):
        return pltpu.make_async_copy(
            tok_hbm.at[pl.ds(pl.multiple_of(c * chunk, chunk), chunk)],
            tok_s.at[pl.ds(pl.multiple_of((c % 2) * chunk, chunk), chunk)], tok_sem.at[c % 2])

    def gather(block, ring, unrolled):
        block = jnp.minimum(block, n_blocks - 1)
        base = ((block // cb) % 2) * chunk + (block % cb) * bm
        buf, buf_sem = xbufs[ring], sem.at[ring]

        def issue(i, carry=None):
            _tile_copy(h_ref, tok_s[base + i], buf, i, per, buf_sem).start()
            return carry

        if unrolled:
            for i in range(bm):
                _tile_copy(h_ref, tok_s[base + i], buf, i, per, buf_sem).start(
                    priority=int(i % ROW_QUEUE_SPLIT != 0))
        else:
            lax.fori_loop(0, bm, issue, 0, unroll=8)

    def wait_rows(ring):
        pltpu.make_async_copy(h_ref.at[pl.ds(0, bm * per)], xbufs[ring], sem.at[ring]).wait()

    def weight_copies(r):
        e, s = eor_ref[r], r % WEIGHT_DEPTH
        return [pltpu.make_async_copy(src.at[e], dst.at[s], w_sem.at[s])
                for src, dst in ((wg_hbm, wgf), (wu_hbm, wuf), (wd_hbm, wdf))]

    @pl.when(step == 0)
    def _():
        tok_copy(0).start()
        tok_copy(0).wait()
        tok_copy(1).start()
        for r in range(WEIGHT_DEPTH - 1):
            @pl.when(r < n_ranks)
            def _():
                for cp in weight_copies(r):
                    cp.start()
        for b in range(ahead):
            gather(b, b, unrolled=False)

    def block(j, ring, out_row0):
        first = j + ahead
        c_need = first // cb

        @pl.when(jnp.logical_and(jnp.logical_and(j > 0, first % cb == 0), c_need < n_chunks))
        def _():
            tok_copy(c_need).wait()

            @pl.when(c_need + 1 < n_chunks)
            def _():
                tok_copy(c_need + 1).start()

        jb = jnp.minimum(j, n_blocks - 1)
        rank = rank_ref[jb]
        fresh = jnp.logical_or(j == 0, rank != rank_ref[jnp.maximum(jb - 1, 0)])

        @pl.when(jnp.logical_and(j < nv, fresh))
        def _():
            for cp in weight_copies(rank):
                cp.wait()
            s = rank % WEIGHT_DEPTH
            wgb[...] = wgf[s].astype(BF16)
            wub[...] = wuf[s].astype(BF16)
            wdb[...] = wdf[s].astype(BF16)
            nxt = rank + WEIGHT_DEPTH - 1

            @pl.when(nxt < n_ranks)
            def _():
                for cp in weight_copies(nxt):
                    cp.start()

        @pl.when(j < nv)
        def _():
            wait_rows(ring)
            gather(j + ahead, (ring + ahead) % depth, unrolled=True)
            xb = _load_row_tiles(xbufs[ring], bm, per, BF16)
            act = (_silu(jnp.dot(xb, wgb[...], preferred_element_type=F32))
                   * jnp.dot(xb, wub[...], preferred_element_type=F32))
            _store_row_tiles(ys_ref, jnp.dot(act.astype(BF16), wdb[...], preferred_element_type=F32),
                             out_row0)

        @pl.when(jnp.logical_and(j >= nv, j < nv + ahead))
        def _():
            wait_rows(ring)

    half = BLOCKS_PER_STEP
    for parity in range(depth // half):
        @pl.when(step % (depth // half) == parity)
        def _():
            for h in range(half):
                block(step * half + h, parity * half + h, h * bm * per)


def _experts(block_rank, expert_of_rank, n_valid, n_ranks, row_tok, h2_tiles, wg, wu, wd):
    rows = row_tok.shape[0]
    d = wg.shape[1]
    per = d // LANES
    bm = EXPERT_ROWS
    n_blocks = rows // bm
    assert n_blocks % TOK_CHUNK_BLOCKS == 0 and n_blocks // TOK_CHUNK_BLOCKS >= 2
    assert GATHER_DEPTH - 1 < TOK_CHUNK_BLOCKS
    bps = BLOCKS_PER_STEP
    assert GATHER_DEPTH % bps == 0 and n_blocks % bps == 0
    f = wg.shape[2]
    hbm = pl.BlockSpec(memory_space=pl.ANY)
    return pl.pallas_call(
        _experts_kernel,
        grid_spec=pltpu.PrefetchScalarGridSpec(
            num_scalar_prefetch=4,
            grid=((n_blocks + GATHER_DEPTH) // bps,),
            in_specs=[hbm, hbm, hbm, hbm, hbm],
            out_specs=pl.BlockSpec((bps * bm * per, LANES),
                                   lambda s, rk, eor, nv, nr: (jnp.minimum(s, (nv[0] - 1) // bps), 0)),
            scratch_shapes=[pltpu.VMEM((bm * per, LANES), F32)] * GATHER_DEPTH + [
                            pltpu.VMEM((WEIGHT_DEPTH, d, f), F32), pltpu.VMEM((WEIGHT_DEPTH, d, f), F32),
                            pltpu.VMEM((WEIGHT_DEPTH, f, d), F32),
                            pltpu.VMEM((d, f), BF16), pltpu.VMEM((d, f), BF16),
                            pltpu.VMEM((f, d), BF16),
                            pltpu.SMEM((2 * TOK_CHUNK_BLOCKS * bm,), I32),
                            pltpu.SemaphoreType.DMA((GATHER_DEPTH,)), pltpu.SemaphoreType.DMA((2,)),
                            pltpu.SemaphoreType.DMA((WEIGHT_DEPTH,))]),
        out_shape=jax.ShapeDtypeStruct((rows * per, LANES), F32),
        compiler_params=_params("arbitrary"),
        name="experts",
    )(block_rank, expert_of_rank, n_valid, n_ranks, row_tok, h2_tiles, wg, wu, wd)


def _shared_kernel(h2_ref, wgs_ref, wus_ref, wds_ref, out_ref):
    hb = h2_ref[...].astype(BF16)
    act = (_silu(jnp.dot(hb, wgs_ref[...], preferred_element_type=F32))
           * jnp.dot(hb, wus_ref[...], preferred_element_type=F32))
    out_ref[...] = jnp.dot(act.astype(BF16), wds_ref[...], preferred_element_type=F32).astype(BF16)


def _shared(h2, wgs, wus, wds):
    n, d = h2.shape
    tt = TT_PROJ
    sd = wgs.shape[1]
    tile = pl.BlockSpec((tt, d), lambda i: (i, 0))
    full = lambda shape: pl.BlockSpec(shape, lambda i: (0,) * len(shape))
    return pl.pallas_call(
        _shared_kernel,
        grid=(n // tt,),
        in_specs=[tile, full((d, sd)), full((d, sd)), full((sd, d))],
        out_specs=tile,
        out_shape=jax.ShapeDtypeStruct((n, d), BF16),
        compiler_params=_params("parallel"),
        name="shared",
    )(h2, wgs, wus, wds)


def _combine_kernel(dest_ref, gates_ref, x1_ref, shared_ref, mod_ref, ys_ref, out_ref, buf, sem):
    tt, d = x1_ref.shape
    per = d // LANES
    for t in range(tt):
        for k in range(TOP_K):
            _tile_copy(ys_ref, dest_ref[k, t], buf.at[k], t, per, sem).start(priority=k % 2)
    for k in range(TOP_K):
        pltpu.make_async_copy(ys_ref.at[pl.ds(0, tt * per)], buf.at[k], sem).wait()
    gates = gates_ref[...]
    routed = gates[:, 0:1] * _load_row_tiles(buf.at[0], tt, per, F32)
    for k in range(1, TOP_K):
        routed = routed + gates[:, k:k + 1] * _load_row_tiles(buf.at[k], tt, per, F32)
    out_ref[...] = x1_ref[...] + mod_ref[5:6, :] * (routed + shared_ref[...].astype(F32))


def _combine(dest_t, gates, x1, shared, mod, ys, seq):
    n, d = x1.shape
    tt = TT_COMBINE
    tiles_per_seq = seq // tt
    tile = pl.BlockSpec((tt, d), lambda i: (i, 0))
    return pl.pallas_call(
        _combine_kernel,
        grid=(n // tt,),
        in_specs=[pl.BlockSpec((TOP_K, tt), lambda i: (0, i), memory_space=pltpu.SMEM),
                  pl.BlockSpec((tt, TOP_K), lambda i: (i, 0)),
                  tile, tile,
                  pl.BlockSpec((None, 6, d), lambda i: (i // tiles_per_seq, 0, 0)),
                  pl.BlockSpec(memory_space=pl.ANY)],
        out_specs=tile,
        out_shape=jax.ShapeDtypeStruct((n, d), F32),
        scratch_shapes=[pltpu.VMEM((TOP_K, tt * d // LANES, LANES), F32), pltpu.SemaphoreType.DMA],
        compiler_params=_params("arbitrary"),
        name="combine",
    )(dest_t, gates, x1, shared, mod, ys)


def _layer(x, mod, pos, rope, p):
    nbatch, seq, d = x.shape
    n = nbatch * seq
    xf = x.reshape(n, d)
    invf, spread, bd = rope

    perm = np.concatenate([np.arange(h * HEAD_DIM, (h + 1) * HEAD_DIM) for h in PAIR_ORDER_A])
    w_in = p["w_in"]
    w_in_p = jnp.concatenate([w_in[:, :MIX_A][:, perm], w_in[:, MIX_A:]], axis=1).astype(BF16)
    ones = lambda w: jnp.ones((w,), F32)
    qscale = HEAD_DIM ** -0.5
    gcol = jnp.concatenate([jnp.tile(p["g_q_a"], N_HEADS_A) * qscale, jnp.tile(p["g_k_a"], N_KV_A),
                            ones(KV_A), jnp.tile(p["g_q_b"], N_HEADS_B) * qscale,
                            jnp.tile(p["g_k_b"], N_HEADS_B), ones(MIX_B)]).reshape(1, IN_WIDTH)
    proj = _inproj(xf, mod, pos, p["g_norm_mix"].reshape(1, d), w_in_p, gcol, invf, spread, bd, seq)
    qa, ka, va = proj[:3]
    qkv_b = {1: proj[3:6]}
    for j, dil in enumerate(DILS):
        qkv_b[dil] = [proj[6 + t * len(DILS) + j] for t in range(3)]

    sinks_p = p["sinks_a"][np.array(PAIR_ORDER_A)]
    oa, _ = _attention(qa, ka, va, nbatch=nbatch, seq=seq, dil=1, max_dist=WINDOW_A - 1,
                       kv_shared=True, sinks=sinks_p, want_lse=False)
    obs, lses = [], []
    for window, dil in DILATED_BRANCHES:
        o, lse = _attention(*qkv_b[dil], nbatch=nbatch, seq=seq, dil=dil, max_dist=window // dil,
                            kv_shared=False, want_lse=True)
        obs.append(o)
        lses.append(lse)

    goa = p["g_out_a"][perm].reshape(1, MIX_A)
    w_out = p["w_out"]
    wo_p = jnp.concatenate([w_out[:MIX_A][perm], w_out[MIX_A:]], axis=0).astype(BF16)
    h2, h2_tiles, x1 = _outproj(xf, mod, oa, obs, lses, goa, p["g_out_b"].reshape(1, MIX_B), wo_p,
                                p["g_norm_ffn"].reshape(1, d), seq)

    top_e_t, gates_t, counts = _router(h2, p["w_router"].T.astype(BF16),
                                       p["router_bias"].reshape(N_EXPERTS, 1))
    bm = EXPERT_ROWS
    counts = counts.reshape(N_EXPERTS).astype(I32)
    padded = (counts + bm - 1) // bm * bm
    pends = jnp.cumsum(padded)
    pstarts = pends - padded
    rows = n * TOP_K + N_EXPERTS * bm
    n_blocks = rows // bm
    n_valid = (pends[-1] // bm).astype(I32)
    blk = jnp.minimum(jnp.arange(n_blocks, dtype=I32), n_valid - 1)
    block_e = jnp.sum((pends[None, :] <= (blk * bm)[:, None]).astype(I32), axis=1)
    block_e = jnp.minimum(block_e, N_EXPERTS - 1)
    nonempty = counts > 0
    rank_of_e = jnp.cumsum(nonempty.astype(I32)) - 1
    e_ids = jnp.arange(N_EXPERTS, dtype=I32)
    block_rank = jnp.sum(jnp.where(block_e[:, None] == e_ids[None, :], rank_of_e[None, :], 0), axis=1)
    hit = (rank_of_e[None, :] == jnp.arange(n_blocks, dtype=I32)[:, None]) & nonempty[None, :]
    expert_of_rank = jnp.sum(jnp.where(hit, e_ids[None, :], 0), axis=1)
    n_ranks = jnp.sum(nonempty.astype(I32)).reshape(1)

    tri =(np.arange(TT_ROUTE)[:, None] <= np.arange(TT_ROUTE)[None, :])
    dest_t = _rank(top_e_t, pstarts.astype(F32).reshape(N_EXPERTS, 1), jnp.asarray(tri, BF16))
    n_valid = n_valid.reshape(1)
    row_tok = _invert(dest_t.reshape(n * TOP_K), n, rows)
    shared = _shared(h2, p["w_gate_s"].astype(BF16), p["w_up_s"].astype(BF16), p["w_down_s"].astype(BF16))
    ys = _experts(block_rank.astype(I32), expert_of_rank.astype(I32), n_valid, n_ranks, row_tok, h2_tiles,
                  p["w_gate_e"], p["w_up_e"], p["w_down_e"])
    out = _combine(dest_t, gates_t.T, x1, shared, mod, ys, seq)
    return out.reshape(nbatch, seq, d)


def kernel(x, c, positions, w_ada, b_ada, g_norm_mix, w_in, g_q_a, g_k_a, sinks_a, g_q_b, g_k_b,
           g_out_a, g_out_b, w_out, g_norm_ffn, w_router, router_bias, w_gate_e, w_up_e, w_down_e,
           w_gate_s, w_up_s, w_down_s):
    nbatch, seq, d = x.shape
    depth = w_ada.shape[0]
    params = dict(g_norm_mix=g_norm_mix, w_in=w_in, g_q_a=g_q_a, g_k_a=g_k_a, sinks_a=sinks_a,
                  g_q_b=g_q_b, g_k_b=g_k_b, g_out_a=g_out_a, g_out_b=g_out_b, w_out=w_out,
                  g_norm_ffn=g_norm_ffn, w_router=w_router, router_bias=router_bias,
                  w_gate_e=w_gate_e, w_up_e=w_up_e, w_down_e=w_down_e, w_gate_s=w_gate_s,
                  w_up_s=w_up_s, w_down_s=w_down_s)
    j = np.arange(LANES) % HEAD_DIM
    invf = (ROPE_THETA ** (-jnp.arange(0, ROT_DIM, 2, dtype=F32) / ROT_DIM)).reshape(ROT_DIM // 2, 1)
    spread = jnp.asarray((j[None, :] < ROT_DIM)
                         & (j[None, :] % (ROT_DIM // 2) == np.arange(ROT_DIM // 2)[:, None]), BF16)
    bd = jnp.asarray((np.arange(LANES)[:, None] // HEAD_DIM) == (np.arange(LANES)[None, :] // HEAD_DIM),
                     BF16)
    pos = positions.reshape(1, nbatch * seq).astype(I32)
    for l in range(depth):
        mod = _adaln(c.astype(F32), w_ada[l], b_ada[l]).reshape(nbatch, 6, d)
        x = _layer(x, mod, pos, (invf, spread, bd), {k: v[l] for k, v in params.items()})
    return x
```

```python
import functools

import numpy as np
import jax
import jax.numpy as jnp
from jax import lax
from jax.experimental import pallas as pl
from jax.experimental.pallas import tpu as pltpu
from jax.experimental.pallas import tpu_sc as plsc

F32 = jnp.float32
BF16 = jnp.bfloat16
I32 = jnp.int32

HEAD_DIM = 64
N_HEADS_A = 8
N_KV_A = 2
WINDOW_A = 128
N_HEADS_B = 8
DILATED_BRANCHES = ((128, 1), (512, 4), (2048, 16))
DILS = tuple(dil for _, dil in DILATED_BRANCHES if dil > 1)
BLOCK = 128
ROT_DIM = HEAD_DIM // 4
ROPE_THETA = 500000.0
MIX_A = N_HEADS_A * HEAD_DIM
KV_A = N_KV_A * HEAD_DIM
MIX_B = N_HEADS_B * HEAD_DIM
N_EXPERTS = 256
TOP_K = 8
N_GROUPS = 8
TOPK_GROUPS = 4
GROUP_SIZE = N_EXPERTS // N_GROUPS
ROUTED_SCALE = 2.5
EPS = 1e-6

LANES = 128
N_PAIRS = MIX_A // LANES
NEG = -1e30
VMEM_LIMIT = 48 * 1024 * 1024

TT_PROJ = 512
ATTN_BLOCKS_PER_STEP = 8
TT_ROUTE = 1024
TT_RANK = 512
TT_COMBINE = 512
EXPERT_ROWS = 256
TOK_CHUNK_BLOCKS = 32
GATHER_DEPTH = 4
WEIGHT_DEPTH = 3
BLOCKS_PER_STEP = 2
ROW_QUEUE_SPLIT = 2

PAIR_ORDER_A = tuple(h for p in range(N_PAIRS) for h in (p, p + N_HEADS_A // N_KV_A))


def _params(*sem):
    return pltpu.CompilerParams(dimension_semantics=sem, vmem_limit_bytes=VMEM_LIMIT)


def _silu(t):
    return t / (1.0 + jnp.exp(-t))


def _rms_rows(t):
    return t * lax.rsqrt(jnp.mean(t * t, axis=-1, keepdims=True) + EPS)


def _store_row_tiles(ref, value, row0=0):
    rows, d = value.shape
    per = d // LANES
    for s in range(per):
        ref[pl.ds(row0 + s, rows, stride=per), :] = value[:, s * LANES:(s + 1) * LANES]


def _load_row_tiles(ref, rows, per, dtype):
    return jnp.concatenate([ref[pl.ds(s, rows, stride=per), :].astype(dtype) for s in range(per)], axis=1)


def _tile_copy(src_ref, src_row, dst_ref, dst_row, per, sem):
    start = lambda row: row * per if isinstance(row, int) else pl.multiple_of(row * per, per)
    return pltpu.make_async_copy(src_ref.at[pl.ds(start(src_row), per)],
                                 dst_ref.at[pl.ds(start(dst_row), per)], sem)


def _ada_kernel(c_ref, w_ref, b_ref, o_ref):
    cond = _silu(c_ref[...])
    o_ref[...] = jnp.dot(cond.astype(BF16), w_ref[...].astype(BF16),
                         preferred_element_type=F32) + b_ref[...]


def _adaln(c, w_ada, b_ada):
    nb, d = c.shape
    width = w_ada.shape[1]
    tn = 1024
    return pl.pallas_call(
        _ada_kernel,
        grid=(width // tn,),
        in_specs=[pl.BlockSpec((nb, d), lambda j: (0, 0)),
                  pl.BlockSpec((d, tn), lambda j: (0, j)),
                  pl.BlockSpec((1, tn), lambda j: (0, j))],
        out_specs=pl.BlockSpec((nb, tn), lambda j: (0, j)),
        out_shape=jax.ShapeDtypeStruct((nb, width), F32),
        compiler_params=_params("arbitrary"),
        name="adaln",
    )(c, w_ada, b_ada.reshape(1, width))


COL_QA, COL_KA, COL_VA = 0, MIX_A, MIX_A + KV_A
COL_QB = MIX_A + 2 * KV_A
COL_KB, COL_VB = COL_QB + MIX_B, COL_QB + 2 * MIX_B
IN_WIDTH = COL_VB + MIX_B


def _inproj_kernel(x_ref, mod_ref, pos_ref, gn_ref, w_ref, gcol_ref, invf_ref, spread_ref, bd_ref,
                   qa_ref, ka_ref, va_ref, qb_ref, kb_ref, vb_ref, *rest):
    n_dil = len(DILS)
    dil_refs = [rest[i * n_dil:(i + 1) * n_dil] for i in range(3)]
    qb_scr, kb_scr, vb_scr = rest[3 * n_dil:]
    tt = x_ref.shape[0]
    shift, scale = mod_ref[0:1, :], mod_ref[1:2, :]
    h = _rms_rows(x_ref[...]) * gn_ref[...] * (1.0 + scale) + shift
    proj = jnp.dot(h.astype(BF16), w_ref[...], preferred_element_type=F32)

    ang = pos_ref[...].astype(F32) * invf_ref[...]
    lane = lax.broadcasted_iota(I32, (1, LANES), 1) % HEAD_DIM

    def spread(table):
        hi = table.astype(BF16)
        lo = (table - hi.astype(F32)).astype(BF16)
        to_lanes = lambda a: lax.dot_general(a, spread_ref[...], (((0,), (0,)), ((), ())),
                                             preferred_element_type=F32)
        return to_lanes(hi) + to_lanes(lo)

    cs = spread(jnp.cos(ang)) + jnp.where(lane < ROT_DIM, 0.0, 1.0)
    sn = spread(jnp.sin(ang))
    s_lo = jnp.where(lane < ROT_DIM // 2, -sn, 0.0)
    s_hi = jnp.where((lane >= ROT_DIM // 2) & (lane < ROT_DIM), sn, 0.0)
    bd = bd_ref[...]

    def norm_rope(col0, width, out_ref, scr=None):
        for j in range(width // LANES):
            c = col0 + j * LANES
            t = proj[:, c:c + LANES]
            sq = t * t
            hi = sq.astype(BF16)
            lo = (sq - hi.astype(F32)).astype(BF16)
            ss = (jnp.dot(hi, bd, preferred_element_type=F32)
                  + jnp.dot(lo, bd, preferred_element_type=F32))
            t = t * lax.rsqrt(ss * (1.0 / HEAD_DIM) + EPS) * gcol_ref[:, c:c + LANES]
            t = (t * cs + pltpu.roll(t, LANES - ROT_DIM // 2, 1) * s_lo
                 + pltpu.roll(t, ROT_DIM // 2, 1) * s_hi)
            out_ref[:, j * LANES:(j + 1) * LANES] = t.astype(BF16)
            if scr is not None:
                scr[j] = t

    norm_rope(COL_QA, MIX_A, qa_ref)
    norm_rope(COL_KA, KV_A, ka_ref)
    norm_rope(COL_QB, MIX_B, qb_ref, qb_scr)
    norm_rope(COL_KB, MIX_B, kb_ref, kb_scr)
    va_ref[...] = proj[:, COL_VA:COL_VA + KV_A].astype(BF16)
    vb_ref[...] = proj[:, COL_VB:COL_VB + MIX_B].astype(BF16)
    for j in range(N_PAIRS):
        vb_scr[j] = proj[:, COL_VB + j * LANES:COL_VB + (j + 1) * LANES]
    for scr, outs in zip((qb_scr, kb_scr, vb_scr), dil_refs):
        for dil, out in zip(DILS, outs):
            for r in range(dil):
                for j in range(N_PAIRS):
                    c = r * MIX_B + j * LANES
                    out[:, c:c + LANES] = scr[j, pl.ds(r, tt // dil, stride=dil), :].astype(BF16)


def _inproj(xf, mod, pos, g_norm, w_in_p, gcol, invf, spread, bd, seq):
    n, d = xf.shape
    tt = TT_PROJ
    tiles_per_seq = seq // tt
    shapes = [(n, w) for w in (MIX_A, KV_A, KV_A, MIX_B, MIX_B, MIX_B)]
    shapes += [(n // dil, dil * MIX_B) for _ in range(3) for dil in DILS]
    full = lambda shape: pl.BlockSpec(shape, lambda i: (0,) * len(shape))
    return pl.pallas_call(
        _inproj_kernel,
        grid=(n // tt,),
        in_specs=[pl.BlockSpec((tt, d), lambda i: (i, 0)),
                  pl.BlockSpec((None, 6, d), lambda i: (i // tiles_per_seq, 0, 0)),
                  pl.BlockSpec((1, tt), lambda i: (0, i)),
                  full((1, d)), full((d, IN_WIDTH)), full((1, IN_WIDTH)),
                  full((ROT_DIM // 2, 1)), full((ROT_DIM // 2, LANES)), full((LANES, LANES))],
        out_specs=[pl.BlockSpec((tt * r // n, w), lambda i: (i, 0)) for r, w in shapes],
        out_shape=[jax.ShapeDtypeStruct(s, BF16) for s in shapes],
        scratch_shapes=[pltpu.VMEM((N_PAIRS, tt, LANES), F32)] * 3,
        compiler_params=_params("parallel"),
        name="inproj",
    )(xf, mod, pos, g_norm, w_in_p, gcol, invf, spread, bd)


def _attn_kernel(*refs, kv_shared, max_dist, n_qblk, n_res, use_prev, has_sinks, want_lse):
    refs = list(refs)
    sink_ref = refs.pop(0) if has_sinks else None
    q_ref = refs.pop(0)
    kp_ref = refs.pop(0) if use_prev else None
    kc_ref = refs.pop(0)
    vp_ref = refs.pop(0) if use_prev else None
    vc_ref = refs.pop(0)
    o_ref = refs.pop(0)
    lse_ref = refs.pop(0) if want_lse else None

    first_step = pl.program_id(2) == 0
    nq = 2 * BLOCK
    kw = LANES if kv_shared else MIX_B

    def band(nk):
        qpos = lax.broadcasted_iota(I32, (nq, nk), 0) % BLOCK
        kpos = lax.broadcasted_iota(I32, (nq, nk), 1)
        dist = qpos + (nk - BLOCK) - kpos
        return (dist >= 0) & (dist <= max_dist), kpos

    band2, kpos2 = band(2 * BLOCK)
    band1, _ = band(BLOCK)
    lane = lax.broadcasted_iota(I32, (nq, LANES), 1)
    row = lax.broadcasted_iota(I32, (nq, LANES), 0)
    own_half = (lane < HEAD_DIM) == (row < BLOCK)
    left_lanes = lax.broadcasted_iota(I32, (BLOCK, LANES), 1) < HEAD_DIM
    lane8 = lax.broadcasted_iota(I32, (BLOCK, 2 * N_PAIRS), 1)

    for res in range(n_res):
        for qb in range(n_qblk):
            rows = slice(qb * BLOCK, (qb + 1) * BLOCK)
            lse_blk = jnp.zeros((BLOCK, 2 * N_PAIRS), F32)
            for p in range(N_PAIRS):
                cq = slice(res * MIX_B + p * LANES, res * MIX_B + (p + 1) * LANES)
                ck = slice(res * kw, res * kw + LANES) if kv_shared else slice(
                    res * kw + p * LANES, res * kw + (p + 1) * LANES)
                qp = q_ref[rows, cq]
                qs = jnp.concatenate([qp, qp], axis=0)
                qs = jnp.where(own_half, qs, jnp.zeros_like(qs))
                if qb > 0:
                    keys = slice((qb - 1) * BLOCK, (qb + 1) * BLOCK)
                    k, v, valid = kc_ref[keys, ck], vc_ref[keys, ck], band2
                elif use_prev:
                    k = jnp.concatenate([kp_ref[:, ck], kc_ref[rows, ck]], axis=0)
                    v = jnp.concatenate([vp_ref[:, ck], vc_ref[rows, ck]], axis=0)
                    valid = band2 & ((kpos2 >= BLOCK) | jnp.logical_not(first_step))
                else:
                    k, v, valid = kc_ref[rows, ck], vc_ref[rows, ck], band1
                s = lax.dot_general(qs, k, (((1,), (1,)), ((), ())), preferred_element_type=F32)
                s = jnp.where(valid, s, NEG)
                m = jnp.max(s, axis=-1, keepdims=True)
                if has_sinks:
                    rows1 = lax.broadcasted_iota(I32, (nq, 1), 0)
                    sink = jnp.where(rows1 < BLOCK, sink_ref[2 * p], sink_ref[2 * p + 1])
                    m = jnp.maximum(m, sink)
                e = jnp.exp(s - m)
                l = jnp.sum(e, axis=-1, keepdims=True)
                if has_sinks:
                    l = l + jnp.exp(sink - m)
                o = jnp.dot(e.astype(BF16), v, preferred_element_type=F32) / l
                o_ref[rows, cq] = jnp.where(left_lanes, o[:BLOCK], o[BLOCK:]).astype(BF16)
                if want_lse:
                    lse = m + jnp.log(l)
                    lse_blk = (lse_blk + jnp.where(lane8 == 2 * p, lse[:BLOCK], 0.0)
                               + jnp.where(lane8 == 2 * p + 1, lse[BLOCK:], 0.0))
            if want_lse:
                lse_ref[res, rows, :] = lse_blk


def _attention(q, k, v, *, nbatch, seq, dil, max_dist, kv_shared, sinks=None, want_lse):
    length = seq // dil
    nblk = length // BLOCK
    n_qblk = min(nblk, ATTN_BLOCKS_PER_STEP)
    n_res = min(dil, ATTN_BLOCKS_PER_STEP // n_qblk)
    steps = nblk // n_qblk
    use_prev = steps > 1
    kw = k.shape[1] // dil
    view = lambda t: t.reshape(nbatch, length, t.shape[1])
    cur = lambda b, r, i: (b, i, r)
    prev = lambda b, r, i: (b, jnp.maximum(i * n_qblk - 1, 0), r)
    in_specs, args = [], []
    if sinks is not None:
        in_specs.append(pl.BlockSpec(memory_space=pltpu.SMEM))
        args.append(sinks)
    in_specs.append(pl.BlockSpec((None, n_qblk * BLOCK, n_res * MIX_B), cur))
    args.append(view(q))
    for t in (k, v):
        if use_prev:
            in_specs.append(pl.BlockSpec((None, BLOCK, n_res * kw), prev))
            args.append(view(t))
        in_specs.append(pl.BlockSpec((None, n_qblk * BLOCK, n_res * kw), cur))
        args.append(view(t))
    out_specs = [pl.BlockSpec((None, n_qblk * BLOCK, n_res * MIX_B), cur)]
    out_shape = [jax.ShapeDtypeStruct((nbatch, length, dil * MIX_B), BF16)]
    if want_lse:
        out_specs.append(pl.BlockSpec((None, n_res, n_qblk * BLOCK, N_HEADS_B),
                                      lambda b, r, i: (b, r, i, 0)))
        out_shape.append(jax.ShapeDtypeStruct((nbatch, dil, length, N_HEADS_B), F32))
    outs = pl.pallas_call(
        functools.partial(_attn_kernel, kv_shared=kv_shared, max_dist=max_dist, n_qblk=n_qblk,
                          n_res=n_res, use_prev=use_prev, has_sinks=sinks is not None,
                          want_lse=want_lse),
        grid=(nbatch, dil // n_res, steps),
        in_specs=in_specs, out_specs=out_specs, out_shape=out_shape,
        compiler_params=_params("parallel", "parallel", "arbitrary"),
        name=f"attn_d{dil}" + ("_swa" if kv_shared else ""),
    )(*args)
    o = outs[0].reshape(nbatch * length, dil * MIX_B)
    if not want_lse:
        return o, None
    lse = outs[1].transpose(0, 2, 1, 3).reshape(nbatch * seq, N_HEADS_B)
    return o, lse


def _expand_heads(w, width):
    head = lax.broadcasted_iota(I32, (1, width), 1) // HEAD_DIM
    out = jnp.zeros((w.shape[0], width), F32)
    for hd in range(w.shape[1]):
        out = jnp.where(head == hd, w[:, hd:hd + 1], out)
    return out


def _outproj_kernel(x_ref, mod_ref, oa_ref, ob1_ref, ob2_ref, ob3_ref, l1_ref, l2_ref, l3_ref,
                    goa_ref, gob_ref, wo_ref, gf_ref, h2_ref, h2t_ref, x1_ref, ob_scr):
    tt = x_ref.shape[0]
    gate_a = mod_ref[2:3, :]
    shift_m, scale_m = mod_ref[3:4, :], mod_ref[4:5, :]

    def token_major(ref, dil):
        if dil == 1:
            return ref[...].astype(F32)
        for r in range(dil):
            for j in range(N_PAIRS):
                c = r * MIX_B + j * LANES
                ob_scr[j, pl.ds(r, tt // dil, stride=dil), :] = ref[:, c:c + LANES].astype(F32)
        return jnp.concatenate([ob_scr[j] for j in range(N_PAIRS)], axis=1)

    l1, l2, l3 = l1_ref[...], l2_ref[...], l3_ref[...]
    mx = jnp.maximum(jnp.maximum(l1, l2), l3)
    e1, e2, e3 = jnp.exp(l1 - mx), jnp.exp(l2 - mx), jnp.exp(l3 - mx)
    den = e1 + e2 + e3
    dils = [dil for _, dil in DILATED_BRANCHES]
    ob = _expand_heads(e1 / den, MIX_B) * token_major(ob1_ref, dils[0])
    ob = ob + _expand_heads(e2 / den, MIX_B) * token_major(ob2_ref, dils[1])
    ob = ob + _expand_heads(e3 / den, MIX_B) * token_major(ob3_ref, dils[2])
    ob = _rms_rows(ob) * gob_ref[...]
    oa = _rms_rows(oa_ref[...].astype(F32)) * goa_ref[...]
    y = (jnp.dot(oa.astype(BF16), wo_ref[0:MIX_A, :], preferred_element_type=F32)
         + jnp.dot(ob.astype(BF16), wo_ref[MIX_A:MIX_A + MIX_B, :], preferred_element_type=F32))
    x1 = x_ref[...] + gate_a * y
    h2 = _rms_rows(x1) * gf_ref[...] * (1.0 + scale_m) + shift_m
    h2_ref[...] = h2
    _store_row_tiles(h2t_ref, h2)
    x1_ref[...] = x1


def _outproj(xf, mod, oa, obs, lses, goa, gob, wo_p, gf, seq):
    n, d = xf.shape
    tt = TT_PROJ
    tiles_per_seq = seq // tt
    tile = lambda w: pl.BlockSpec((tt, w), lambda i: (i, 0))
    full = lambda shape: pl.BlockSpec(shape, lambda i: (0,) * len(shape))
    dilated = [pl.BlockSpec((tt // dil, dil * MIX_B), lambda i: (i, 0)) for _, dil in DILATED_BRANCHES]
    return pl.pallas_call(
        _outproj_kernel,
        grid=(n // tt,),
        in_specs=[tile(d), pl.BlockSpec((None, 6, d), lambda i: (i // tiles_per_seq, 0, 0)),
                  tile(MIX_A), *dilated,
                  tile(N_HEADS_B), tile(N_HEADS_B), tile(N_HEADS_B),
                  full((1, MIX_A)), full((1, MIX_B)), full((MIX_A + MIX_B, d)), full((1, d))],
        out_specs=[tile(d), pl.BlockSpec((tt * d // LANES, LANES), lambda i: (i, 0)), tile(d)],
        out_shape=[jax.ShapeDtypeStruct((n, d), F32), jax.ShapeDtypeStruct((n * d // LANES, LANES), F32),
                   jax.ShapeDtypeStruct((n, d), F32)],
        scratch_shapes=[pltpu.VMEM((N_PAIRS, tt, LANES), F32)],
        compiler_params=_params("parallel"),
        name="outproj",
    )(xf, mod, oa, *obs, *lses, goa, gob, wo_p, gf)


def _router_kernel(h_ref, wrt_ref, bias_ref, e_ref, g_ref, cnt_ref):
    tt = h_ref.shape[0]
    logits = lax.dot_general(wrt_ref[...], h_ref[...].astype(BF16), (((1,), (1,)), ((), ())),
                             preferred_element_type=F32)
    scores = 1.0 / (1.0 + jnp.exp(-logits))
    biased = scores + bias_ref[...]
    ninf = -jnp.inf

    j32 = lax.broadcasted_iota(I32, (GROUP_SIZE, tt), 0).astype(F32)
    grp = []
    for g in range(N_GROUPS):
        bg = biased[g * GROUP_SIZE:(g + 1) * GROUP_SIZE, :]
        m1 = jnp.max(bg, axis=0, keepdims=True)
        i1 = jnp.min(jnp.where(bg == m1, j32, float(GROUP_SIZE)), axis=0, keepdims=True)
        m2 = jnp.max(jnp.where(j32 == i1, ninf, bg), axis=0, keepdims=True)
        grp.append(m1 + m2)
    grp = jnp.concatenate(grp, axis=0)
    g8 = lax.broadcasted_iota(I32, (N_GROUPS, tt), 0).astype(F32)
    chosen = jnp.zeros((N_GROUPS, tt), F32)
    for _ in range(TOPK_GROUPS):
        gm = jnp.max(grp, axis=0, keepdims=True)
        gi = jnp.min(jnp.where(grp == gm, g8, float(N_GROUPS)), axis=0, keepdims=True)
        hit = g8 == gi
        chosen = jnp.where(hit, 1.0, chosen)
        grp = jnp.where(hit, ninf, grp)
    masked = jnp.concatenate(
        [jnp.where(chosen[g:g + 1, :] > 0.0, biased[g * GROUP_SIZE:(g + 1) * GROUP_SIZE, :], ninf)
         for g in range(N_GROUPS)], axis=0)

    eio = lax.broadcasted_iota(I32, (N_EXPERTS, tt), 0).astype(F32)
    picked = jnp.zeros((N_EXPERTS, tt), F32)
    es, gs = [], []
    for _ in range(TOP_K):
        m = jnp.max(masked, axis=0, keepdims=True)
        idx = jnp.min(jnp.where(masked == m, eio, float(N_EXPERTS)), axis=0, keepdims=True)
        hit = eio == idx
        gs.append(jnp.sum(jnp.where(hit, scores, 0.0), axis=0, keepdims=True))
        es.append(idx)
        picked = jnp.where(hit, 1.0, picked)
        masked = jnp.where(hit, ninf, masked)
    gates = jnp.concatenate(gs, axis=0)
    e_ref[...] = jnp.concatenate(es, axis=0).astype(I32)
    g_ref[...] = gates / jnp.sum(gates, axis=0, keepdims=True) * ROUTED_SCALE

    @pl.when(pl.program_id(0) == 0)
    def _():
        cnt_ref[...] = jnp.zeros_like(cnt_ref)
    cnt_ref[...] += jnp.sum(picked, axis=1, keepdims=True)


def _router(h2, wrt, bias_col):
    n, d = h2.shape
    tt = TT_ROUTE
    return pl.pallas_call(
        _router_kernel,
        grid=(n // tt,),
        in_specs=[pl.BlockSpec((tt, d), lambda i: (i, 0)),
                  pl.BlockSpec((N_EXPERTS, d), lambda i: (0, 0)),
                  pl.BlockSpec((N_EXPERTS, 1), lambda i: (0, 0))],
        out_specs=[pl.BlockSpec((TOP_K, tt), lambda i: (0, i)),
                   pl.BlockSpec((TOP_K, tt), lambda i: (0, i)),
                   pl.BlockSpec((N_EXPERTS, 1), lambda i: (0, 0))],
        out_shape=[jax.ShapeDtypeStruct((TOP_K, n), I32),
                   jax.ShapeDtypeStruct((TOP_K, n), F32),
                   jax.ShapeDtypeStruct((N_EXPERTS, 1), F32)],
        compiler_params=_params("arbitrary"),
        name="router",
    )(h2, wrt, bias_col)


def _rank_kernel(e_ref, pstart_ref, tri_ref, dest_ref, carry_ref):
    tt = e_ref.shape[1]

    @pl.when(pl.program_id(0) == 0)
    def _():
        carry_ref[...] = pstart_ref[...]

    e = e_ref[...]
    eio = lax.broadcasted_iota(I32, (N_EXPERTS, tt), 0)
    mask = jnp.zeros((N_EXPERTS, tt), F32)
    for k in range(TOP_K):
        mask = jnp.where(eio == e[k:k + 1, :], 1.0, mask)
    incl = jnp.dot(mask.astype(BF16), tri_ref[...], preferred_element_type=F32)
    pos = incl - mask + carry_ref[...]
    dest = [jnp.sum(jnp.where(eio == e[k:k + 1, :], pos, 0.0), axis=0, keepdims=True)
            for k in range(TOP_K)]
    dest_ref[...] = jnp.concatenate(dest, axis=0).astype(I32)
    carry_ref[...] += incl[:, tt - 1:tt]


def _rank(top_e_t, pstart_col, tri):
    n = top_e_t.shape[1]
    tt = TT_RANK
    return pl.pallas_call(
        _rank_kernel,
        grid=(n // tt,),
        in_specs=[pl.BlockSpec((TOP_K, tt), lambda i: (0, i)),
                  pl.BlockSpec((N_EXPERTS, 1), lambda i: (0, 0)),
                  pl.BlockSpec((tt, tt), lambda i: (0, 0))],
        out_specs=pl.BlockSpec((TOP_K, tt), lambda i: (0, i)),
        out_shape=jax.ShapeDtypeStruct((TOP_K, n), I32),
        scratch_shapes=[pltpu.VMEM((N_EXPERTS, 1), F32)],
        compiler_params=_params("arbitrary"),
        name="rank",
    )(top_e_t, pstart_col, tri)


SC_CORES, SC_SUBCORES, SC_LANES = 2, 16, 16
SC_WORKERS = SC_CORES * SC_SUBCORES
INVERT_CHUNK = 8192


def _invert(dest_flat, n_tokens, rows):
    per = rows // SC_WORKERS
    n_assign = dest_flat.shape[0]
    assert rows % (SC_WORKERS * SC_LANES) == 0 and n_assign % INVERT_CHUNK == 0
    assert n_tokens & (n_tokens - 1) == 0

    @functools.partial(
        pl.kernel, mesh=plsc.VectorSubcoreMesh(core_axis_name="c", subcore_axis_name="s"),
        out_type=jax.ShapeDtypeStruct((rows,), I32),
        scratch_types=[pltpu.VMEM((INVERT_CHUNK,), I32), pltpu.VMEM((per,), I32)],
        compiler_params=pltpu.CompilerParams(needs_layout_passes=False))
    def invert(dest_hbm, out_hbm, staged, local):
        base = (lax.axis_index("s") * SC_CORES + lax.axis_index("c")) * per
        lane = lax.iota(I32, SC_LANES)

        @pl.loop(0, per, step=SC_LANES)
        def _(i):
            local[pl.ds(i, SC_LANES)] = (base + i + lane) & (n_tokens - 1)

        @pl.loop(0, n_assign // INVERT_CHUNK)
        def _(c):
            pltpu.sync_copy(dest_hbm.at[pl.ds(c * INVERT_CHUNK, INVERT_CHUNK)], staged)

            @pl.loop(0, INVERT_CHUNK, step=SC_LANES)
            def _(i):
                rel = staged[pl.ds(i, SC_LANES)] - base
                mine = (rel >= 0) & (rel < per)
                tok = (c * INVERT_CHUNK + i + lane) & (n_tokens - 1)
                plsc.store_scatter(local, [jnp.where(mine, rel, 0)], tok, mask=mine)

        pltpu.sync_copy(local, out_hbm.at[pl.ds(base, per)])

    return invert(dest_flat)


def _experts_kernel(rank_ref, eor_ref, nv_ref, nr_ref, tok_hbm, h_ref, wg_hbm, wu_hbm, wd_hbm,
                    ys_ref, *scratch):
    xbufs = scratch[:GATHER_DEPTH]
    wgf, wuf, wdf, wgb, wub, wdb, tok_s, sem, tok_sem, w_sem = scratch[GATHER_DEPTH:]
    step = pl.program_id(0)
    nv, n_ranks = nv_ref[0], nr_ref[0]
    per = wgb.shape[0] // LANES
    depth, bm = len(xbufs), xbufs[0].shape[0] // per
    ahead = depth - 1
    chunk = tok_s.shape[0] // 2
    cb = chunk // bm
    n_chunks = tok_hbm.shape[0] // chunk
    n_blocks = n_chunks * cb

    def tok_copy(c):
        return pltpu.make_async_copy(
            tok_hbm.at[pl.ds(pl.multiple_of(c * chunk, chunk), chunk)],
            tok_s.at[pl.ds(pl.multiple_of((c % 2) * chunk, chunk), chunk)], tok_sem.at[c % 2])

    def gather(block, ring, unrolled):
        block = jnp.minimum(block, n_blocks - 1)
        base = ((block // cb) % 2) * chunk + (block % cb) * bm
        buf, buf_sem = xbufs[ring], sem.at[ring]

        def issue(i, carry=None):
            _tile_copy(h_ref, tok_s[base + i], buf, i, per, buf_sem).start()
            return carry

        if unrolled:
            for i in range(bm):
                _tile_copy(h_ref, tok_s[base + i], buf, i, per, buf_sem).start(
                    priority=int(i % ROW_QUEUE_SPLIT != 0))
        else:
            lax.fori_loop(0, bm, issue, 0, unroll=8)

    def wait_rows(ring):
        pltpu.make_async_copy(h_ref.at[pl.ds(0, bm * per)], xbufs[ring], sem.at[ring]).wait()

    def weight_copies(r):
        e, s = eor_ref[r], r % WEIGHT_DEPTH
        return [pltpu.make_async_copy(src.at[e], dst.at[s], w_sem.at[s])
                for src, dst in ((wg_hbm, wgf), (wu_hbm, wuf), (wd_hbm, wdf))]

    @pl.when(step == 0)
    def _():
        tok_copy(0).start()
        tok_copy(0).wait()
        tok_copy(1).start()
        for r in range(WEIGHT_DEPTH - 1):
            @pl.when(r < n_ranks)
            def _():
                for cp in weight_copies(r):
                    cp.start()
        for b in range(ahead):
            gather(b, b, unrolled=False)

    def block(j, ring, out_row0):
        first = j + ahead
        c_need = first // cb

        @pl.when(jnp.logical_and(jnp.logical_and(j > 0, first % cb == 0), c_need < n_chunks))
        def _():
            tok_copy(c_need).wait()

            @pl.when(c_need + 1 < n_chunks)
            def _():
                tok_copy(c_need + 1).start()

        jb = jnp.minimum(j, n_blocks - 1)
        rank = rank_ref[jb]
        fresh = jnp.logical_or(j == 0, rank != rank_ref[jnp.maximum(jb - 1, 0)])

        @pl.when(jnp.logical_and(j < nv, fresh))
        def _():
            for cp in weight_copies(rank):
                cp.wait()
            s = rank % WEIGHT_DEPTH
            wgb[...] = wgf[s].astype(BF16)
            wub[...] = wuf[s].astype(BF16)
            wdb[...] = wdf[s].astype(BF16)
            nxt = rank + WEIGHT_DEPTH - 1

            @pl.when(nxt < n_ranks)
            def _():
                for cp in weight_copies(nxt):
                    cp.start()

        @pl.when(j < nv)
        def _():
            wait_rows(ring)
            gather(j + ahead, (ring + ahead) % depth, unrolled=True)
            xb = _load_row_tiles(xbufs[ring], bm, per, BF16)
            act = (_silu(jnp.dot(xb, wgb[...], preferred_element_type=F32))
                   * jnp.dot(xb, wub[...], preferred_element_type=F32))
            _store_row_tiles(ys_ref, jnp.dot(act.astype(BF16), wdb[...], preferred_element_type=F32),
                             out_row0)

        @pl.when(jnp.logical_and(j >= nv, j < nv + ahead))
        def _():
            wait_rows(ring)

    half = BLOCKS_PER_STEP
    for parity in range(depth // half):
        @pl.when(step % (depth // half) == parity)
        def _():
            for h in range(half):
                block(step * half + h, parity * half + h, h * bm * per)


def _experts(block_rank, expert_of_rank, n_valid, n_ranks, row_tok, h2_tiles, wg, wu, wd):
    rows = row_tok.shape[0]
    d = wg.shape[1]
    per = d // LANES
    bm = EXPERT_ROWS
    n_blocks = rows // bm
    assert n_blocks % TOK_CHUNK_BLOCKS == 0 and n_blocks // TOK_CHUNK_BLOCKS >= 2
    assert GATHER_DEPTH - 1 < TOK_CHUNK_BLOCKS
    bps = BLOCKS_PER_STEP
    assert GATHER_DEPTH % bps == 0 and n_blocks % bps == 0
    f = wg.shape[2]
    hbm = pl.BlockSpec(memory_space=pl.ANY)
    return pl.pallas_call(
        _experts_kernel,
        grid_spec=pltpu.PrefetchScalarGridSpec(
            num_scalar_prefetch=4,
            grid=((n_blocks + GATHER_DEPTH) // bps,),
            in_specs=[hbm, hbm, hbm, hbm, hbm],
            out_specs=pl.BlockSpec((bps * bm * per, LANES),
                                   lambda s, rk, eor, nv, nr: (jnp.minimum(s, (nv[0] - 1) // bps), 0)),
            scratch_shapes=[pltpu.VMEM((bm * per, LANES), F32)] * GATHER_DEPTH + [
                            pltpu.VMEM((WEIGHT_DEPTH, d, f), F32), pltpu.VMEM((WEIGHT_DEPTH, d, f), F32),
                            pltpu.VMEM((WEIGHT_DEPTH, f, d), F32),
                            pltpu.VMEM((d, f), BF16), pltpu.VMEM((d, f), BF16),
                            pltpu.VMEM((f, d), BF16),
                            pltpu.SMEM((2 * TOK_CHUNK_BLOCKS * bm,), I32),
                            pltpu.SemaphoreType.DMA((GATHER_DEPTH,)), pltpu.SemaphoreType.DMA((2,)),
                            pltpu.SemaphoreType.DMA((WEIGHT_DEPTH,))]),
        out_shape=jax.ShapeDtypeStruct((rows * per, LANES), F32),
        compiler_params=_params("arbitrary"),
        name="experts",
    )(block_rank, expert_of_rank, n_valid, n_ranks, row_tok, h2_tiles, wg, wu, wd)


def _combine_kernel(dest_ref, gates_ref, x1_ref, h2_ref, mod_ref, wgs_ref, wus_ref, wds_ref, ys_ref,
                    out_ref, buf, sem):
    tt, d = x1_ref.shape
    per = d // LANES
    for t in range(tt):
        for k in range(TOP_K):
            _tile_copy(ys_ref, dest_ref[k, t], buf.at[k], t, per, sem).start(priority=k % 2)
    hb = h2_ref[...].astype(BF16)
    act = (_silu(jnp.dot(hb, wgs_ref[...], preferred_element_type=F32))
           * jnp.dot(hb, wus_ref[...], preferred_element_type=F32))
    shared = jnp.dot(act.astype(BF16), wds_ref[...], preferred_element_type=F32)
    for k in range(TOP_K):
        pltpu.make_async_copy(ys_ref.at[pl.ds(0, tt * per)], buf.at[k], sem).wait()
    gates = gates_ref[...]
    routed = gates[:, 0:1] * _load_row_tiles(buf.at[0], tt, per, F32)
    for k in range(1, TOP_K):
        routed = routed + gates[:, k:k + 1] * _load_row_tiles(buf.at[k], tt, per, F32)
    out_ref[...] = x1_ref[...] + mod_ref[5:6, :] * (routed + shared)


def _combine(dest_t, gates, x1, h2, mod, wgs, wus, wds, ys, seq):
    n, d = x1.shape
    tt = TT_COMBINE
    tiles_per_seq = seq // tt
    sd = wgs.shape[1]
    tile = pl.BlockSpec((tt, d), lambda i: (i, 0))
    full = lambda shape: pl.BlockSpec(shape, lambda i: (0,) * len(shape))
    return pl.pallas_call(
        _combine_kernel,
        grid=(n // tt,),
        in_specs=[pl.BlockSpec((TOP_K, tt), lambda i: (0, i), memory_space=pltpu.SMEM),
                  pl.BlockSpec((tt, TOP_K), lambda i: (i, 0)),
                  tile, tile,
                  pl.BlockSpec((None, 6, d), lambda i: (i // tiles_per_seq, 0, 0)),
                  full((d, sd)), full((d, sd)), full((sd, d)),
                  pl.BlockSpec(memory_space=pl.ANY)],
        out_specs=tile,
        out_shape=jax.ShapeDtypeStruct((n, d), F32),
        scratch_shapes=[pltpu.VMEM((TOP_K, tt * d // LANES, LANES), F32), pltpu.SemaphoreType.DMA],
        compiler_params=_params("arbitrary"),
        name="combine",
    )(dest_t, gates, x1, h2, mod, wgs, wus, wds, ys)


def _layer(x, mod, pos, rope, p):
    nbatch, seq, d = x.shape
    n = nbatch * seq
    xf = x.reshape(n, d)
    invf, spread, bd = rope

    perm = np.concatenate([np.arange(h * HEAD_DIM, (h + 1) * HEAD_DIM) for h in PAIR_ORDER_A])
    w_in = p["w_in"]
    w_in_p = jnp.concatenate([w_in[:, :MIX_A][:, perm], w_in[:, MIX_A:]], axis=1).astype(BF16)
    ones = lambda w: jnp.ones((w,), F32)
    qscale = HEAD_DIM ** -0.5
    gcol = jnp.concatenate([jnp.tile(p["g_q_a"], N_HEADS_A) * qscale, jnp.tile(p["g_k_a"], N_KV_A),
                            ones(KV_A), jnp.tile(p["g_q_b"], N_HEADS_B) * qscale,
                            jnp.tile(p["g_k_b"], N_HEADS_B), ones(MIX_B)]).reshape(1, IN_WIDTH)
    proj = _inproj(xf, mod, pos, p["g_norm_mix"].reshape(1, d), w_in_p, gcol, invf, spread, bd, seq)
    qa, ka, va = proj[:3]
    qkv_b = {1: proj[3:6]}
    for j, dil in enumerate(DILS):
        qkv_b[dil] = [proj[6 + t * len(DILS) + j] for t in range(3)]

    sinks_p = p["sinks_a"][np.array(PAIR_ORDER_A)]
    oa, _ = _attention(qa, ka, va, nbatch=nbatch, seq=seq, dil=1, max_dist=WINDOW_A - 1,
                       kv_shared=True, sinks=sinks_p, want_lse=False)
    obs, lses = [], []
    for window, dil in DILATED_BRANCHES:
        o, lse = _attention(*qkv_b[dil], nbatch=nbatch, seq=seq, dil=dil, max_dist=window // dil,
                            kv_shared=False, want_lse=True)
        obs.append(o)
        lses.append(lse)

    goa = p["g_out_a"][perm].reshape(1, MIX_A)
    w_out = p["w_out"]
    wo_p = jnp.concatenate([w_out[:MIX_A][perm], w_out[MIX_A:]], axis=0).astype(BF16)
    h2, h2_tiles, x1 = _outproj(xf, mod, oa, obs, lses, goa, p["g_out_b"].reshape(1, MIX_B), wo_p,
                                p["g_norm_ffn"].reshape(1, d), seq)

    top_e_t, gates_t, counts = _router(h2, p["w_router"].T.astype(BF16),
                                       p["router_bias"].reshape(N_EXPERTS, 1))
    bm = EXPERT_ROWS
    counts = counts.reshape(N_EXPERTS).astype(I32)
    padded = (counts + bm - 1) // bm * bm
    pends = jnp.cumsum(padded)
    pstarts = pends - padded
    rows = n * TOP_K + N_EXPERTS * bm
    n_blocks = rows // bm
    n_valid = (pends[-1] // bm).astype(I32)
    blk = jnp.minimum(jnp.arange(n_blocks, dtype=I32), n_valid - 1)
    block_e = jnp.sum((pends[None, :] <= (blk * bm)[:, None]).astype(I32), axis=1)
    block_e = jnp.minimum(block_e, N_EXPERTS - 1)
    nonempty = counts > 0
    rank_of_e = jnp.cumsum(nonempty.astype(I32)) - 1
    e_ids = jnp.arange(N_EXPERTS, dtype=I32)
    block_rank = jnp.sum(jnp.where(block_e[:, None] == e_ids[None, :], rank_of_e[None, :], 0), axis=1)
    hit = (rank_of_e[None, :] == jnp.arange(n_blocks, dtype=I32)[:, None]) & nonempty[None, :]
    expert_of_rank = jnp.sum(jnp.where(hit, e_ids[None, :], 0), axis=1)
    n_ranks = jnp.sum(nonempty.astype(I32)).reshape(1)

    tri = (np.arange(TT_RANK)[:, None] <= np.arange(TT_RANK)[None, :])
    dest_t = _rank(top_e_t, pstarts.astype(F32).reshape(N_EXPERTS, 1), jnp.asarray(tri, BF16))
    n_valid = n_valid.reshape(1)
    row_tok = _invert(dest_t.reshape(n * TOP_K), n, rows)
    ys = _experts(block_rank.astype(I32), expert_of_rank.astype(I32), n_valid, n_ranks, row_tok, h2_tiles,
                  p["w_gate_e"], p["w_up_e"], p["w_down_e"])
    out = _combine(dest_t, gates_t.T, x1, h2, mod, p["w_gate_s"].astype(BF16), p["w_up_s"].astype(BF16),
                   p["w_down_s"].astype(BF16), ys, seq)
    return out.reshape(nbatch, seq, d)


def kernel(x, c, positions, w_ada, b_ada, g_norm_mix, w_in, g_q_a, g_k_a, sinks_a, g_q_b, g_k_b,
           g_out_a, g_out_b, w_out, g_norm_ffn, w_router, router_bias, w_gate_e, w_up_e, w_down_e,
           w_gate_s, w_up_s, w_down_s):
    nbatch, seq, d = x.shape
    depth = w_ada.shape[0]
    params = dict(g_norm_mix=g_norm_mix, w_in=w_in, g_q_a=g_q_a, g_k_a=g_k_a, sinks_a=sinks_a,
                  g_q_b=g_q_b, g_k_b=g_k_b, g_out_a=g_out_a, g_out_b=g_out_b, w_out=w_out,
                  g_norm_ffn=g_norm_ffn, w_router=w_router, router_bias=router_bias,
                  w_gate_e=w_gate_e, w_up_e=w_up_e, w_down_e=w_down_e, w_gate_s=w_gate_s,
                  w_up_s=w_up_s, w_down_s=w_down_s)
    j = np.arange(LANES) % HEAD_DIM
    invf = (ROPE_THETA ** (-jnp.arange(0, ROT_DIM, 2, dtype=F32) / ROT_DIM)).reshape(ROT_DIM // 2, 1)
    spread = jnp.asarray((j[None, :] < ROT_DIM)
                         & (j[None, :] % (ROT_DIM // 2) == np.arange(ROT_DIM // 2)[:, None]), BF16)
    bd = jnp.asarray((np.arange(LANES)[:, None] // HEAD_DIM) == (np.arange(LANES)[None, :] // HEAD_DIM),
                     BF16)
    pos = positions.reshape(1, nbatch * seq).astype(I32)
    for l in range(depth):
        mod = _adaln(c.astype(F32), w_ada[l], b_ada[l]).reshape(nbatch, 6, d)
        x = _layer(x, mod, pos, (invf, spread, bd), {k: v[l] for k, v in params.items()})
    return x
```

```python
import functools

import numpy as np
import jax
import jax.numpy as jnp
from jax import lax
from jax.experimental import pallas as pl
from jax.experimental.pallas import tpu as pltpu
from jax.experimental.pallas import tpu_sc as plsc

F32 = jnp.float32
BF16 = jnp.bfloat16
I32 = jnp.int32

HEAD_DIM = 64
N_HEADS_A = 8
N_KV_A = 2
WINDOW_A = 128
N_HEADS_B = 8
DILATED_BRANCHES = ((128, 1), (512, 4), (2048, 16))
DILS = tuple(dil for _, dil in DILATED_BRANCHES if dil > 1)
BLOCK = 128
ROT_DIM = HEAD_DIM // 4
ROPE_THETA = 500000.0
MIX_A = N_HEADS_A * HEAD_DIM
KV_A = N_KV_A * HEAD_DIM
MIX_B = N_HEADS_B * HEAD_DIM
N_EXPERTS = 256
TOP_K = 8
N_GROUPS = 8
TOPK_GROUPS = 4
GROUP_SIZE = N_EXPERTS // N_GROUPS
ROUTED_SCALE = 2.5
EPS = 1e-6

LANES = 128
N_PAIRS = MIX_A // LANES
NEG = -1e30
VMEM_LIMIT = 48 * 1024 * 1024

TT_PROJ = 512
ATTN_BLOCKS_PER_STEP = 8
TT_ROUTE = 1024
TT_RANK = 512
TT_COMBINE = 512
EXPERT_ROWS = 256
TOK_CHUNK_BLOCKS = 32
GATHER_DEPTH = 6
WEIGHT_DEPTH = 3
BLOCKS_PER_STEP = 2
ROW_QUEUE_SPLIT = 2

PAIR_ORDER_A = tuple(h for p in range(N_PAIRS) for h in (p, p + N_HEADS_A // N_KV_A))


def _params(*sem):
    return pltpu.CompilerParams(dimension_semantics=sem, vmem_limit_bytes=VMEM_LIMIT)


def _silu(t):
    return t / (1.0 + jnp.exp(-t))


def _rms_rows(t):
    return t * lax.rsqrt(jnp.mean(t * t, axis=-1, keepdims=True) + EPS)


def _store_row_tiles(ref, value, row0=0):
    rows, d = value.shape
    per = d // LANES
    for s in range(per):
        ref[pl.ds(row0 + s, rows, stride=per), :] = value[:, s * LANES:(s + 1) * LANES]


def _load_row_tiles(ref, rows, per, dtype):
    return jnp.concatenate([ref[pl.ds(s, rows, stride=per), :].astype(dtype) for s in range(per)], axis=1)


def _tile_copy(src_ref, src_row, dst_ref, dst_row, per, sem):
    start = lambda row: row * per if isinstance(row, int) else pl.multiple_of(row * per, per)
    return pltpu.make_async_copy(src_ref.at[pl.ds(start(src_row), per)],
                                 dst_ref.at[pl.ds(start(dst_row), per)], sem)


def _ada_kernel(c_ref, w_ref, b_ref, o_ref):
    cond = _silu(c_ref[...])
    o_ref[...] = jnp.dot(cond.astype(BF16), w_ref[...].astype(BF16),
                         preferred_element_type=F32) + b_ref[...]


def _adaln(c, w_ada, b_ada):
    nb, d = c.shape
    width = w_ada.shape[1]
    tn = 1024
    return pl.pallas_call(
        _ada_kernel,
        grid=(width // tn,),
        in_specs=[pl.BlockSpec((nb, d), lambda j: (0, 0)),
                  pl.BlockSpec((d, tn), lambda j: (0, j)),
                  pl.BlockSpec((1, tn), lambda j: (0, j))],
        out_specs=pl.BlockSpec((nb, tn), lambda j: (0, j)),
        out_shape=jax.ShapeDtypeStruct((nb, width), F32),
        compiler_params=_params("arbitrary"),
        name="adaln",
    )(c, w_ada, b_ada.reshape(1, width))


COL_QA, COL_KA, COL_VA = 0, MIX_A, MIX_A + KV_A
COL_QB = MIX_A + 2 * KV_A
COL_KB, COL_VB = COL_QB + MIX_B, COL_QB + 2 * MIX_B
IN_WIDTH = COL_VB + MIX_B


def _inproj_kernel(x_ref, mod_ref, pos_ref, gn_ref, w_ref, gcol_ref, invf_ref, spread_ref, bd_ref,
                   qa_ref, ka_ref, va_ref, qb_ref, kb_ref, vb_ref, *rest):
    n_dil = len(DILS)
    dil_refs = [rest[i * n_dil:(i + 1) * n_dil] for i in range(3)]
    qb_scr, kb_scr, vb_scr = rest[3 * n_dil:]
    tt = x_ref.shape[0]
    shift, scale = mod_ref[0:1, :], mod_ref[1:2, :]
    h = _rms_rows(x_ref[...]) * gn_ref[...] * (1.0 + scale) + shift
    proj = jnp.dot(h.astype(BF16), w_ref[...], preferred_element_type=F32)

    ang = pos_ref[...].astype(F32) * invf_ref[...]
    lane = lax.broadcasted_iota(I32, (1, LANES), 1) % HEAD_DIM

    def spread(table):
        hi = table.astype(BF16)
        lo = (table - hi.astype(F32)).astype(BF16)
        to_lanes = lambda a: lax.dot_general(a, spread_ref[...], (((0,), (0,)), ((), ())),
                                             preferred_element_type=F32)
        return to_lanes(hi) + to_lanes(lo)

    cs = spread(jnp.cos(ang)) + jnp.where(lane < ROT_DIM, 0.0, 1.0)
    sn = spread(jnp.sin(ang))
    s_lo = jnp.where(lane < ROT_DIM // 2, -sn, 0.0)
    s_hi = jnp.where((lane >= ROT_DIM // 2) & (lane < ROT_DIM), sn, 0.0)
    bd = bd_ref[...]

    def norm_rope(col0, width, out_ref, scr=None):
        for j in range(width // LANES):
            c = col0 + j * LANES
            t = proj[:, c:c + LANES]
            sq = t * t
            hi = sq.astype(BF16)
            lo = (sq - hi.astype(F32)).astype(BF16)
            ss = (jnp.dot(hi, bd, preferred_element_type=F32)
                  + jnp.dot(lo, bd, preferred_element_type=F32))
            t = t * lax.rsqrt(ss * (1.0 / HEAD_DIM) + EPS) * gcol_ref[:, c:c + LANES]
            t = (t * cs + pltpu.roll(t, LANES - ROT_DIM // 2, 1) * s_lo
                 + pltpu.roll(t, ROT_DIM // 2, 1) * s_hi)
            out_ref[:, j * LANES:(j + 1) * LANES] = t.astype(BF16)
            if scr is not None:
                scr[j] = t

    norm_rope(COL_QA, MIX_A, qa_ref)
    norm_rope(COL_KA, KV_A, ka_ref)
    norm_rope(COL_QB, MIX_B, qb_ref, qb_scr)
    norm_rope(COL_KB, MIX_B, kb_ref, kb_scr)
    va_ref[...] = proj[:, COL_VA:COL_VA + KV_A].astype(BF16)
    vb_ref[...] = proj[:, COL_VB:COL_VB + MIX_B].astype(BF16)
    for j in range(N_PAIRS):
        vb_scr[j] = proj[:, COL_VB + j * LANES:COL_VB + (j + 1) * LANES]
    for scr, outs in zip((qb_scr, kb_scr, vb_scr), dil_refs):
        for dil, out in zip(DILS, outs):
            for r in range(dil):
                for j in range(N_PAIRS):
                    c = r * MIX_B + j * LANES
                    out[:, c:c + LANES] = scr[j, pl.ds(r, tt // dil, stride=dil), :].astype(BF16)


def _inproj(xf, mod, pos, g_norm, w_in_p, gcol, invf, spread, bd, seq):
    n, d = xf.shape
    tt = TT_PROJ
    tiles_per_seq = seq // tt
    shapes = [(n, w) for w in (MIX_A, KV_A, KV_A, MIX_B, MIX_B, MIX_B)]
    shapes += [(n // dil, dil * MIX_B) for _ in range(3) for dil in DILS]
    full = lambda shape: pl.BlockSpec(shape, lambda i: (0,) * len(shape))
    return pl.pallas_call(
        _inproj_kernel,
        grid=(n // tt,),
        in_specs=[pl.BlockSpec((tt, d), lambda i: (i, 0)),
                  pl.BlockSpec((None, 6, d), lambda i: (i // tiles_per_seq, 0, 0)),
                  pl.BlockSpec((1, tt), lambda i: (0, i)),
                  full((1, d)), full((d, IN_WIDTH)), full((1, IN_WIDTH)),
                  full((ROT_DIM // 2, 1)), full((ROT_DIM // 2, LANES)), full((LANES, LANES))],
        out_specs=[pl.BlockSpec((tt * r // n, w), lambda i: (i, 0)) for r, w in shapes],
        out_shape=[jax.ShapeDtypeStruct(s, BF16) for s in shapes],
        scratch_shapes=[pltpu.VMEM((N_PAIRS, tt, LANES), F32)] * 3,
        compiler_params=_params("parallel"),
        name="inproj",
    )(xf, mod, pos, g_norm, w_in_p, gcol, invf, spread, bd)


def _attn_kernel(*refs, kv_shared, max_dist, n_qblk, n_res, use_prev, has_sinks, want_lse):
    refs = list(refs)
    sink_ref = refs.pop(0) if has_sinks else None
    q_ref = refs.pop(0)
    kp_ref = refs.pop(0) if use_prev else None
    kc_ref = refs.pop(0)
    vp_ref = refs.pop(0) if use_prev else None
    vc_ref = refs.pop(0)
    o_ref = refs.pop(0)
    lse_ref = refs.pop(0) if want_lse else None

    first_step = pl.program_id(2) == 0
    nq = 2 * BLOCK
    kw = LANES if kv_shared else MIX_B

    def band(nk):
        qpos = lax.broadcasted_iota(I32, (nq, nk), 0) % BLOCK
        kpos = lax.broadcasted_iota(I32, (nq, nk), 1)
        dist = qpos + (nk - BLOCK) - kpos
        return (dist >= 0) & (dist <= max_dist), kpos

    band2, kpos2 = band(2 * BLOCK)
    band1, _ = band(BLOCK)
    lane = lax.broadcasted_iota(I32, (nq, LANES), 1)
    row = lax.broadcasted_iota(I32, (nq, LANES), 0)
    own_half = (lane < HEAD_DIM) == (row < BLOCK)
    left_lanes = lax.broadcasted_iota(I32, (BLOCK, LANES), 1) < HEAD_DIM
    lane8 = lax.broadcasted_iota(I32, (BLOCK, 2 * N_PAIRS), 1)

    for res in range(n_res):
        for qb in range(n_qblk):
            rows = slice(qb * BLOCK, (qb + 1) * BLOCK)
            lse_blk = jnp.zeros((BLOCK, 2 * N_PAIRS), F32)
            for p in range(N_PAIRS):
                cq = slice(res * MIX_B + p * LANES, res * MIX_B + (p + 1) * LANES)
                ck = slice(res * kw, res * kw + LANES) if kv_shared else slice(
                    res * kw + p * LANES, res * kw + (p + 1) * LANES)
                qp = q_ref[rows, cq]
                qs = jnp.concatenate([qp, qp], axis=0)
                qs = jnp.where(own_half, qs, jnp.zeros_like(qs))
                if qb > 0:
                    keys = slice((qb - 1) * BLOCK, (qb + 1) * BLOCK)
                    k, v, valid = kc_ref[keys, ck], vc_ref[keys, ck], band2
                elif use_prev:
                    k = jnp.concatenate([kp_ref[:, ck], kc_ref[rows, ck]], axis=0)
                    v = jnp.concatenate([vp_ref[:, ck], vc_ref[rows, ck]], axis=0)
                    valid = band2 & ((kpos2 >= BLOCK) | jnp.logical_not(first_step))
                else:
                    k, v, valid = kc_ref[rows, ck], vc_ref[rows, ck], band1
                s = lax.dot_general(qs, k, (((1,), (1,)), ((), ())), preferred_element_type=F32)
                s = jnp.where(valid, s, NEG)
                m = jnp.max(s, axis=-1, keepdims=True)
                if has_sinks:
                    rows1 = lax.broadcasted_iota(I32, (nq, 1), 0)
                    sink = jnp.where(rows1 < BLOCK, sink_ref[2 * p], sink_ref[2 * p + 1])
                    m = jnp.maximum(m, sink)
                e = jnp.exp(s - m)
                l = jnp.sum(e, axis=-1, keepdims=True)
                if has_sinks:
                    l = l + jnp.exp(sink - m)
                o = jnp.dot(e.astype(BF16), v, preferred_element_type=F32) / l
                o_ref[rows, cq] = jnp.where(left_lanes, o[:BLOCK], o[BLOCK:]).astype(BF16)
                if want_lse:
                    lse = m + jnp.log(l)
                    lse_blk = (lse_blk + jnp.where(lane8 == 2 * p, lse[:BLOCK], 0.0)
                               + jnp.where(lane8 == 2 * p + 1, lse[BLOCK:], 0.0))
            if want_lse:
                lse_ref[res, rows, :] = lse_blk


def _attention(q, k, v, *, nbatch, seq, dil, max_dist, kv_shared, sinks=None, want_lse):
    length = seq // dil
    nblk = length // BLOCK
    n_qblk = min(nblk, ATTN_BLOCKS_PER_STEP)
    n_res = min(dil, ATTN_BLOCKS_PER_STEP // n_qblk)
    steps = nblk // n_qblk
    use_prev = steps > 1
    kw = k.shape[1] // dil
    view = lambda t: t.reshape(nbatch, length, t.shape[1])
    cur = lambda b, r, i: (b, i, r)
    prev = lambda b, r, i: (b, jnp.maximum(i * n_qblk - 1, 0), r)
    in_specs, args = [], []
    if sinks is not None:
        in_specs.append(pl.BlockSpec(memory_space=pltpu.SMEM))
        args.append(sinks)
    in_specs.append(pl.BlockSpec((None, n_qblk * BLOCK, n_res * MIX_B), cur))
    args.append(view(q))
    for t in (k, v):
        if use_prev:
            in_specs.append(pl.BlockSpec((None, BLOCK, n_res * kw), prev))
            args.append(view(t))
        in_specs.append(pl.BlockSpec((None, n_qblk * BLOCK, n_res * kw), cur))
        args.append(view(t))
    out_specs = [pl.BlockSpec((None, n_qblk * BLOCK, n_res * MIX_B), cur)]
    out_shape = [jax.ShapeDtypeStruct((nbatch, length, dil * MIX_B), BF16)]
    if want_lse:
        out_specs.append(pl.BlockSpec((None, n_res, n_qblk * BLOCK, N_HEADS_B),
                                      lambda b, r, i: (b, r, i, 0)))
        out_shape.append(jax.ShapeDtypeStruct((nbatch, dil, length, N_HEADS_B), F32))
    outs = pl.pallas_call(
        functools.partial(_attn_kernel, kv_shared=kv_shared, max_dist=max_dist, n_qblk=n_qblk,
                          n_res=n_res, use_prev=use_prev, has_sinks=sinks is not None,
                          want_lse=want_lse),
        grid=(nbatch, dil // n_res, steps),
        in_specs=in_specs, out_specs=out_specs, out_shape=out_shape,
        compiler_params=_params("parallel", "parallel", "arbitrary"),
        name=f"attn_d{dil}" + ("_swa" if kv_shared else ""),
    )(*args)
    o = outs[0].reshape(nbatch * length, dil * MIX_B)
    if not want_lse:
        return o, None
    lse = outs[1].transpose(0, 2, 1, 3).reshape(nbatch * seq, N_HEADS_B)
    return o, lse


def _expand_heads(w, width):
    head = lax.broadcasted_iota(I32, (1, width), 1) // HEAD_DIM
    out = jnp.zeros((w.shape[0], width), F32)
    for hd in range(w.shape[1]):
        out = jnp.where(head == hd, w[:, hd:hd + 1], out)
    return out


def _outproj_kernel(x_ref, mod_ref, oa_ref, ob1_ref, ob2_ref, ob3_ref, l1_ref, l2_ref, l3_ref,
                    goa_ref, gob_ref, wo_ref, gf_ref, h2_ref, h2t_ref, x1_ref, ob_scr):
    tt = x_ref.shape[0]
    gate_a = mod_ref[2:3, :]
    shift_m, scale_m = mod_ref[3:4, :], mod_ref[4:5, :]

    def token_major(ref, dil):
        if dil == 1:
            return ref[...].astype(F32)
        for r in range(dil):
            for j in range(N_PAIRS):
                c = r * MIX_B + j * LANES
                ob_scr[j, pl.ds(r, tt // dil, stride=dil), :] = ref[:, c:c + LANES].astype(F32)
        return jnp.concatenate([ob_scr[j] for j in range(N_PAIRS)], axis=1)

    l1, l2, l3 = l1_ref[...], l2_ref[...], l3_ref[...]
    mx = jnp.maximum(jnp.maximum(l1, l2), l3)
    e1, e2, e3 = jnp.exp(l1 - mx), jnp.exp(l2 - mx), jnp.exp(l3 - mx)
    den = e1 + e2 + e3
    dils = [dil for _, dil in DILATED_BRANCHES]
    ob = _expand_heads(e1 / den, MIX_B) * token_major(ob1_ref, dils[0])
    ob = ob + _expand_heads(e2 / den, MIX_B) * token_major(ob2_ref, dils[1])
    ob = ob + _expand_heads(e3 / den, MIX_B) * token_major(ob3_ref, dils[2])
    ob = _rms_rows(ob) * gob_ref[...]
    oa = _rms_rows(oa_ref[...].astype(F32)) * goa_ref[...]
    y = (jnp.dot(oa.astype(BF16), wo_ref[0:MIX_A, :], preferred_element_type=F32)
         + jnp.dot(ob.astype(BF16), wo_ref[MIX_A:MIX_A + MIX_B, :], preferred_element_type=F32))
    x1 = x_ref[...] + gate_a * y
    h2 = _rms_rows(x1) * gf_ref[...] * (1.0 + scale_m) + shift_m
    h2_ref[...] = h2
    _store_row_tiles(h2t_ref, h2)
    x1_ref[...] = x1


def _outproj(xf, mod, oa, obs, lses, goa, gob, wo_p, gf, seq):
    n, d = xf.shape
    tt = TT_PROJ
    tiles_per_seq = seq // tt
    tile = lambda w: pl.BlockSpec((tt, w), lambda i: (i, 0))
    full = lambda shape: pl.BlockSpec(shape, lambda i: (0,) * len(shape))
    dilated = [pl.BlockSpec((tt // dil, dil * MIX_B), lambda i: (i, 0)) for _, dil in DILATED_BRANCHES]
    return pl.pallas_call(
        _outproj_kernel,
        grid=(n // tt,),
        in_specs=[tile(d), pl.BlockSpec((None, 6, d), lambda i: (i // tiles_per_seq, 0, 0)),
                  tile(MIX_A), *dilated,
                  tile(N_HEADS_B), tile(N_HEADS_B), tile(N_HEADS_B),
                  full((1, MIX_A)), full((1, MIX_B)), full((MIX_A + MIX_B, d)), full((1, d))],
        out_specs=[tile(d), pl.BlockSpec((tt * d // LANES, LANES), lambda i: (i, 0)), tile(d)],
        out_shape=[jax.ShapeDtypeStruct((n, d), F32), jax.ShapeDtypeStruct((n * d // LANES, LANES), F32),
                   jax.ShapeDtypeStruct((n, d), F32)],
        scratch_shapes=[pltpu.VMEM((N_PAIRS, tt, LANES), F32)],
        compiler_params=_params("parallel"),
        name="outproj",
    )(xf, mod, oa, *obs, *lses, goa, gob, wo_p, gf)


def _router_kernel(h_ref, wrt_ref, bias_ref, e_ref, g_ref, cnt_ref):
    tt = h_ref.shape[0]
    logits = lax.dot_general(wrt_ref[...], h_ref[...].astype(BF16), (((1,), (1,)), ((), ())),
                             preferred_element_type=F32)
    scores = 1.0 / (1.0 + jnp.exp(-logits))
    biased = scores + bias_ref[...]
    ninf = -jnp.inf

    j32 = lax.broadcasted_iota(I32, (GROUP_SIZE, tt), 0).astype(F32)
    grp = []
    for g in range(N_GROUPS):
        bg = biased[g * GROUP_SIZE:(g + 1) * GROUP_SIZE, :]
        m1 = jnp.max(bg, axis=0, keepdims=True)
        i1 = jnp.min(jnp.where(bg == m1, j32, float(GROUP_SIZE)), axis=0, keepdims=True)
        m2 = jnp.max(jnp.where(j32 == i1, ninf, bg), axis=0, keepdims=True)
        grp.append(m1 + m2)
    grp = jnp.concatenate(grp, axis=0)
    g8 = lax.broadcasted_iota(I32, (N_GROUPS, tt), 0).astype(F32)
    chosen = jnp.zeros((N_GROUPS, tt), F32)
    for _ in range(TOPK_GROUPS):
        gm = jnp.max(grp, axis=0, keepdims=True)
        gi = jnp.min(jnp.where(grp == gm, g8, float(N_GROUPS)), axis=0, keepdims=True)
        hit = g8 == gi
        chosen = jnp.where(hit, 1.0, chosen)
        grp = jnp.where(hit, ninf, grp)
    masked = jnp.concatenate(
        [jnp.where(chosen[g:g + 1, :] > 0.0, biased[g * GROUP_SIZE:(g + 1) * GROUP_SIZE, :], ninf)
         for g in range(N_GROUPS)], axis=0)

    eio = lax.broadcasted_iota(I32, (N_EXPERTS, tt), 0).astype(F32)
    picked = jnp.zeros((N_EXPERTS, tt), F32)
    es, gs = [], []
    for _ in range(TOP_K):
        m = jnp.max(masked, axis=0, keepdims=True)
        idx = jnp.min(jnp.where(masked == m, eio, float(N_EXPERTS)), axis=0, keepdims=True)
        hit = eio == idx
        gs.append(jnp.sum(jnp.where(hit, scores, 0.0), axis=0, keepdims=True))
        es.append(idx)
        picked = jnp.where(hit, 1.0, picked)
        masked = jnp.where(hit, ninf, masked)
    gates = jnp.concatenate(gs, axis=0)
    e_ref[...] = jnp.concatenate(es, axis=0).astype(I32)
    g_ref[...] = gates / jnp.sum(gates, axis=0, keepdims=True) * ROUTED_SCALE

    @pl.when(pl.program_id(0) == 0)
    def _():
        cnt_ref[...] = jnp.zeros_like(cnt_ref)
    cnt_ref[...] += jnp.sum(picked, axis=1, keepdims=True)


def _router(h2, wrt, bias_col):
    n, d = h2.shape
    tt = TT_ROUTE
    return pl.pallas_call(
        _router_kernel,
        grid=(n // tt,),
        in_specs=[pl.BlockSpec((tt, d), lambda i: (i, 0)),
                  pl.BlockSpec((N_EXPERTS, d), lambda i: (0, 0)),
                  pl.BlockSpec((N_EXPERTS, 1), lambda i: (0, 0))],
        out_specs=[pl.BlockSpec((TOP_K, tt), lambda i: (0, i)),
                   pl.BlockSpec((TOP_K, tt), lambda i: (0, i)),
                   pl.BlockSpec((N_EXPERTS, 1), lambda i: (0, 0))],
        out_shape=[jax.ShapeDtypeStruct((TOP_K, n), I32),
                   jax.ShapeDtypeStruct((TOP_K, n), F32),
                   jax.ShapeDtypeStruct((N_EXPERTS, 1), F32)],
        compiler_params=_params("arbitrary"),
        name="router",
    )(h2, wrt, bias_col)


def _rank_kernel(e_ref, pstart_ref, tri_ref, dest_ref, carry_ref):
    tt = e_ref.shape[1]

    @pl.when(pl.program_id(0) == 0)
    def _():
        carry_ref[...] = pstart_ref[...]

    e = e_ref[...]
    eio = lax.broadcasted_iota(I32, (N_EXPERTS, tt), 0)
    mask = jnp.zeros((N_EXPERTS, tt), F32)
    for k in range(TOP_K):
        mask = jnp.where(eio == e[k:k + 1, :], 1.0, mask)
    incl = jnp.dot(mask.astype(BF16), tri_ref[...], preferred_element_type=F32)
    pos = incl - mask + carry_ref[...]
    dest = [jnp.sum(jnp.where(eio == e[k:k + 1, :], pos, 0.0), axis=0, keepdims=True)
            for k in range(TOP_K)]
    dest_ref[...] = jnp.concatenate(dest, axis=0).astype(I32)
    carry_ref[...] += incl[:, tt - 1:tt]


def _rank(top_e_t, pstart_col, tri):
    n = top_e_t.shape[1]
    tt = TT_RANK
    return pl.pallas_call(
        _rank_kernel,
        grid=(n // tt,),
        in_specs=[pl.BlockSpec((TOP_K, tt), lambda i: (0, i)),
                  pl.BlockSpec((N_EXPERTS, 1), lambda i: (0, 0)),
                  pl.BlockSpec((tt, tt), lambda i: (0, 0))],
        out_specs=pl.BlockSpec((TOP_K, tt), lambda i: (0, i)),
        out_shape=jax.ShapeDtypeStruct((TOP_K, n), I32),
        scratch_shapes=[pltpu.VMEM((N_EXPERTS, 1), F32)],
        compiler_params=_params("arbitrary"),
        name="rank",
    )(top_e_t, pstart_col, tri)


SC_CORES, SC_SUBCORES, SC_LANES = 2, 16, 16
SC_WORKERS = SC_CORES * SC_SUBCORES
INVERT_CHUNK = 8192


def _invert(dest_flat, n_tokens, rows):
    per = rows // SC_WORKERS
    n_assign = dest_flat.shape[0]
    assert rows % (SC_WORKERS * SC_LANES) == 0 and n_assign % INVERT_CHUNK == 0
    assert n_tokens & (n_tokens - 1) == 0

    @functools.partial(
        pl.kernel, mesh=plsc.VectorSubcoreMesh(core_axis_name="c", subcore_axis_name="s"),
        out_type=jax.ShapeDtypeStruct((rows,), I32),
        scratch_types=[pltpu.VMEM((INVERT_CHUNK,), I32), pltpu.VMEM((per,), I32)],
        compiler_params=pltpu.CompilerParams(needs_layout_passes=False))
    def invert(dest_hbm, out_hbm, staged, local):
        base = (lax.axis_index("s") * SC_CORES + lax.axis_index("c")) * per
        lane = lax.iota(I32, SC_LANES)

        @pl.loop(0, per, step=SC_LANES)
        def _(i):
            local[pl.ds(i, SC_LANES)] = (base + i + lane) & (n_tokens - 1)

        @pl.loop(0, n_assign // INVERT_CHUNK)
        def _(c):
            pltpu.sync_copy(dest_hbm.at[pl.ds(c * INVERT_CHUNK, INVERT_CHUNK)], staged)

            @pl.loop(0, INVERT_CHUNK, step=SC_LANES)
            def _(i):
                rel = staged[pl.ds(i, SC_LANES)] - base
                mine = (rel >= 0) & (rel < per)
                tok = (c * INVERT_CHUNK + i + lane) & (n_tokens - 1)
                plsc.store_scatter(local, [jnp.where(mine, rel, 0)], tok, mask=mine)

        pltpu.sync_copy(local, out_hbm.at[pl.ds(base, per)])

    return invert(dest_flat)


def _experts_kernel(rank_ref, eor_ref, nv_ref, nr_ref, tok_hbm, h_ref, wg_hbm, wu_hbm, wd_hbm,
                    ys_ref, *scratch):
    xbufs = scratch[:GATHER_DEPTH]
    wgf, wuf, wdf, wgb, wub, wdb, tok_s, sem, tok_sem, w_sem = scratch[GATHER_DEPTH:]
    step = pl.program_id(0)
    nv, n_ranks = nv_ref[0], nr_ref[0]
    per = wgb.shape[0] // LANES
    depth, bm = len(xbufs), xbufs[0].shape[0] // per
    ahead = depth - 1
    chunk = tok_s.shape[0] // 2
    cb = chunk // bm
    n_chunks = tok_hbm.shape[0] // chunk
    n_blocks = n_chunks * cb

    def tok_copy(c):
        return pltpu.make_async_copy(
            tok_hbm.at[pl.ds(pl.multiple_of(c * chunk, chunk), chunk)],
            tok_s.at[pl.ds(pl.multiple_of((c % 2) * chunk, chunk), chunk)], tok_sem.at[c % 2])

    def gather(block, ring, unrolled):
        block = jnp.minimum(block, n_blocks - 1)
        base = ((block // cb) % 2) * chunk + (block % cb) * bm
        buf, buf_sem = xbufs[ring], sem.at[ring]

        def issue(i, carry=None):
            _tile_copy(h_ref, tok_s[base + i], buf, i, per, buf_sem).start()
            return carry

        if unrolled:
            for i in range(bm):
                _tile_copy(h_ref, tok_s[base + i], buf, i, per, buf_sem).start(
                    priority=int(i % ROW_QUEUE_SPLIT != 0))
        else:
            lax.fori_loop(0, bm, issue, 0, unroll=8)

    def wait_rows(ring):
        pltpu.make_async_copy(h_ref.at[pl.ds(0, bm * per)], xbufs[ring], sem.at[ring]).wait()

    def weight_copies(r):
        e, s = eor_ref[r], r % WEIGHT_DEPTH
        return [pltpu.make_async_copy(src.at[e], dst.at[s], w_sem.at[s])
                for src, dst in ((wg_hbm, wgf), (wu_hbm, wuf), (wd_hbm, wdf))]

    @pl.when(step == 0)
    def _():
        tok_copy(0).start()
        tok_copy(0).wait()
        tok_copy(1).start()
        for r in range(WEIGHT_DEPTH - 1):
            @pl.when(r < n_ranks)
            def _():
                for cp in weight_copies(r):
                    cp.start()
        for b in range(ahead):
            gather(b, b, unrolled=False)

    def block(j, ring, out_row0):
        first = j + ahead
        c_need = first // cb

        @pl.when(jnp.logical_and(jnp.logical_and(j > 0, first % cb == 0), c_need < n_chunks))
        def _():
            tok_copy(c_need).wait()

            @pl.when(c_need + 1 < n_chunks)
            def _():
                tok_copy(c_need + 1).start()

        jb = jnp.minimum(j, n_blocks - 1)
        rank = rank_ref[jb]
        fresh = jnp.logical_or(j == 0, rank != rank_ref[jnp.maximum(jb - 1, 0)])

        @pl.when(jnp.logical_and(j < nv, fresh))
        def _():
            for cp in weight_copies(rank):
                cp.wait()
            s = rank % WEIGHT_DEPTH
            wgb[...] = wgf[s].astype(BF16)
            wub[...] = wuf[s].astype(BF16)
            wdb[...] = wdf[s].astype(BF16)
            nxt = rank + WEIGHT_DEPTH - 1

            @pl.when(nxt < n_ranks)
            def _():
                for cp in weight_copies(nxt):
                    cp.start()

        @pl.when(j < nv)
        def _():
            wait_rows(ring)
            gather(j + ahead, (ring + ahead) % depth, unrolled=True)
            xb = _load_row_tiles(xbufs[ring], bm, per, BF16)
            act = (_silu(jnp.dot(xb, wgb[...], preferred_element_type=F32))
                   * jnp.dot(xb, wub[...], preferred_element_type=F32))
            _store_row_tiles(ys_ref, jnp.dot(act.astype(BF16), wdb[...], preferred_element_type=F32),
                             out_row0)

        @pl.when(jnp.logical_and(j >= nv, j < nv + ahead))
        def _():
            wait_rows(ring)

    half = BLOCKS_PER_STEP
    for parity in range(depth // half):
        @pl.when(step % (depth // half) == parity)
        def _():
            for h in range(half):
                block(step * half + h, parity * half + h, h * bm * per)


def _experts(block_rank, expert_of_rank, n_valid, n_ranks, row_tok, h2_tiles, wg, wu, wd):
    rows = row_tok.shape[0]
    d = wg.shape[1]
    per = d // LANES
    bm = EXPERT_ROWS
    n_blocks = rows // bm
    assert n_blocks % TOK_CHUNK_BLOCKS == 0 and n_blocks // TOK_CHUNK_BLOCKS >= 2
    assert GATHER_DEPTH - 1 < TOK_CHUNK_BLOCKS
    bps = BLOCKS_PER_STEP
    assert GATHER_DEPTH % bps == 0 and n_blocks % bps == 0
    f = wg.shape[2]
    hbm = pl.BlockSpec(memory_space=pl.ANY)
    return pl.pallas_call(
        _experts_kernel,
        grid_spec=pltpu.PrefetchScalarGridSpec(
            num_scalar_prefetch=4,
            grid=((n_blocks + GATHER_DEPTH) // bps,),
            in_specs=[hbm, hbm, hbm, hbm, hbm],
            out_specs=pl.BlockSpec((bps * bm * per, LANES),
                                   lambda s, rk, eor, nv, nr: (jnp.minimum(s, (nv[0] - 1) // bps), 0)),
            scratch_shapes=[pltpu.VMEM((bm * per, LANES), F32)] * GATHER_DEPTH + [
                            pltpu.VMEM((WEIGHT_DEPTH, d, f), F32), pltpu.VMEM((WEIGHT_DEPTH, d, f), F32),
                            pltpu.VMEM((WEIGHT_DEPTH, f, d), F32),
                            pltpu.VMEM((d, f), BF16), pltpu.VMEM((d, f), BF16),
                            pltpu.VMEM((f, d), BF16),
                            pltpu.SMEM((2 * TOK_CHUNK_BLOCKS * bm,), I32),
                            pltpu.SemaphoreType.DMA((GATHER_DEPTH,)), pltpu.SemaphoreType.DMA((2,)),
                            pltpu.SemaphoreType.DMA((WEIGHT_DEPTH,))]),
        out_shape=jax.ShapeDtypeStruct((rows * per, LANES), F32),
        compiler_params=_params("arbitrary"),
        name="experts",
    )(block_rank, expert_of_rank, n_valid, n_ranks, row_tok, h2_tiles, wg, wu, wd)


def _combine_kernel(dest_ref, gates_ref, x1_ref, h2_ref, mod_ref, wgs_ref, wus_ref, wds_ref, ys_ref,
                    out_ref, buf, sem):
    tt, d = x1_ref.shape
    per = d // LANES
    for t in range(tt):
        for k in range(TOP_K):
            _tile_copy(ys_ref, dest_ref[k, t], buf.at[k], t, per, sem).start(priority=k % 2)
    hb = h2_ref[...].astype(BF16)
    act = (_silu(jnp.dot(hb, wgs_ref[...], preferred_element_type=F32))
           * jnp.dot(hb, wus_ref[...], preferred_element_type=F32))
    shared = jnp.dot(act.astype(BF16), wds_ref[...], preferred_element_type=F32)
    for k in range(TOP_K):
        pltpu.make_async_copy(ys_ref.at[pl.ds(0, tt * per)], buf.at[k], sem).wait()
    gates = gates_ref[...]
    routed = gates[:, 0:1] * _load_row_tiles(buf.at[0], tt, per, F32)
    for k in range(1, TOP_K):
        routed = routed + gates[:, k:k + 1] * _load_row_tiles(buf.at[k], tt, per, F32)
    out_ref[...] = x1_ref[...] + mod_ref[5:6, :] * (routed + shared)


def _combine(dest_t, gates, x1, h2, mod, wgs, wus, wds, ys, seq):
    n, d = x1.shape
    tt = TT_COMBINE
    tiles_per_seq = seq // tt
    sd = wgs.shape[1]
    tile = pl.BlockSpec((tt, d), lambda i: (i, 0))
    full = lambda shape: pl.BlockSpec(shape, lambda i: (0,) * len(shape))
    return pl.pallas_call(
        _combine_kernel,
        grid=(n // tt,),
        in_specs=[pl.BlockSpec((TOP_K, tt), lambda i: (0, i), memory_space=pltpu.SMEM),
                  pl.BlockSpec((tt, TOP_K), lambda i: (i, 0)),
                  tile, tile,
                  pl.BlockSpec((None, 6, d), lambda i: (i // tiles_per_seq, 0, 0)),
                  full((d, sd)), full((d, sd)), full((sd, d)),
                  pl.BlockSpec(memory_space=pl.ANY)],
        out_specs=tile,
        out_shape=jax.ShapeDtypeStruct((n, d), F32),
        scratch_shapes=[pltpu.VMEM((TOP_K, tt * d // LANES, LANES), F32), pltpu.SemaphoreType.DMA],
        compiler_params=_params("arbitrary"),
        name="combine",
    )(dest_t, gates, x1, h2, mod, wgs, wus, wds, ys)


def _layer(x, mod, pos, rope, p):
    nbatch, seq, d = x.shape
    n = nbatch * seq
    xf = x.reshape(n, d)
    invf, spread, bd = rope

    perm = np.concatenate([np.arange(h * HEAD_DIM, (h + 1) * HEAD_DIM) for h in PAIR_ORDER_A])
    w_in = p["w_in"]
    w_in_p = jnp.concatenate([w_in[:, :MIX_A][:, perm], w_in[:, MIX_A:]], axis=1).astype(BF16)
    ones = lambda w: jnp.ones((w,), F32)
    qscale = HEAD_DIM ** -0.5
    gcol = jnp.concatenate([jnp.tile(p["g_q_a"], N_HEADS_A) * qscale, jnp.tile(p["g_k_a"], N_KV_A),
                            ones(KV_A), jnp.tile(p["g_q_b"], N_HEADS_B) * qscale,
                            jnp.tile(p["g_k_b"], N_HEADS_B), ones(MIX_B)]).reshape(1, IN_WIDTH)
    proj = _inproj(xf, mod, pos, p["g_norm_mix"].reshape(1, d), w_in_p, gcol, invf, spread, bd, seq)
    qa, ka, va = proj[:3]
    qkv_b = {1: proj[3:6]}
    for j, dil in enumerate(DILS):
        qkv_b[dil] = [proj[6 + t * len(DILS) + j] for t in range(3)]

    sinks_p = p["sinks_a"][np.array(PAIR_ORDER_A)]
    oa, _ = _attention(qa, ka, va, nbatch=nbatch, seq=seq, dil=1, max_dist=WINDOW_A - 1,
                       kv_shared=True, sinks=sinks_p, want_lse=False)
    obs, lses = [], []
    for window, dil in DILATED_BRANCHES:
        o, lse = _attention(*qkv_b[dil], nbatch=nbatch, seq=seq, dil=dil, max_dist=window // dil,
                            kv_shared=False, want_lse=True)
        obs.append(o)
        lses.append(lse)

    goa = p["g_out_a"][perm].reshape(1, MIX_A)
    w_out = p["w_out"]
    wo_p = jnp.concatenate([w_out[:MIX_A][perm], w_out[MIX_A:]], axis=0).astype(BF16)
    h2, h2_tiles, x1 = _outproj(xf, mod, oa, obs, lses, goa, p["g_out_b"].reshape(1, MIX_B), wo_p,
                                p["g_norm_ffn"].reshape(1, d), seq)

    top_e_t, gates_t, counts = _router(h2, p["w_router"].T.astype(BF16),
                                       p["router_bias"].reshape(N_EXPERTS, 1))
    bm = EXPERT_ROWS
    counts = counts.reshape(N_EXPERTS).astype(I32)
    padded = (counts + bm - 1) // bm * bm
    pends = jnp.cumsum(padded)
    pstarts = pends - padded
    rows = n * TOP_K + N_EXPERTS * bm
    n_blocks = rows // bm
    n_valid = (pends[-1] // bm).astype(I32)
    blk = jnp.minimum(jnp.arange(n_blocks, dtype=I32), n_valid - 1)
    block_e = jnp.sum((pends[None, :] <= (blk * bm)[:, None]).astype(I32), axis=1)
    block_e = jnp.minimum(block_e, N_EXPERTS - 1)
    nonempty = counts > 0
    rank_of_e = jnp.cumsum(nonempty.astype(I32)) - 1
    e_ids = jnp.arange(N_EXPERTS, dtype=I32)
    block_rank = jnp.sum(jnp.where(block_e[:, None] == e_ids[None, :], rank_of_e[None, :], 0), axis=1)
    hit = (rank_of_e[None, :] == jnp.arange(n_blocks, dtype=I32)[:, None]) & nonempty[None, :]
    expert_of_rank = jnp.sum(jnp.where(hit, e_ids[None, :], 0), axis=1)
    n_ranks = jnp.sum(nonempty.astype(I32)).reshape(1)

    tri = (np.arange(TT_RANK)[:, None] <= np.arange(TT_RANK)[None, :])
    dest_t = _rank(top_e_t, pstarts.astype(F32).reshape(N_EXPERTS, 1), jnp.asarray(tri, BF16))
    n_valid = n_valid.reshape(1)
    row_tok = _invert(dest_t.reshape(n * TOP_K), n, rows)
    ys = _experts(block_rank.astype(I32), expert_of_rank.astype(I32), n_valid, n_ranks, row_tok, h2_tiles,
                  p["w_gate_e"], p["w_up_e"], p["w_down_e"])
    out = _combine(dest_t, gates_t.T, x1, h2, mod, p["w_gate_s"].astype(BF16), p["w_up_s"].astype(BF16),
                   p["w_down_s"].astype(BF16), ys, seq)
    return out.reshape(nbatch, seq, d)


def kernel(x, c, positions, w_ada, b_ada, g_norm_mix, w_in, g_q_a, g_k_a, sinks_a, g_q_b, g_k_b,
           g_out_a, g_out_b, w_out, g_norm_ffn, w_router, router_bias, w_gate_e, w_up_e, w_down_e,
           w_gate_s, w_up_s, w_down_s):
    nbatch, seq, d = x.shape
    depth = w_ada.shape[0]
    params = dict(g_norm_mix=g_norm_mix, w_in=w_in, g_q_a=g_q_a, g_k_a=g_k_a, sinks_a=sinks_a,
                  g_q_b=g_q_b, g_k_b=g_k_b, g_out_a=g_out_a, g_out_b=g_out_b, w_out=w_out,
                  g_norm_ffn=g_norm_ffn, w_router=w_router, router_bias=router_bias,
                  w_gate_e=w_gate_e, w_up_e=w_up_e, w_down_e=w_down_e, w_gate_s=w_gate_s,
                  w_up_s=w_up_s, w_down_s=w_down_s)
    j = np.arange(LANES) % HEAD_DIM
    invf = (ROPE_THETA ** (-jnp.arange(0, ROT_DIM, 2, dtype=F32) / ROT_DIM)).reshape(ROT_DIM // 2, 1)
    spread = jnp.asarray((j[None, :] < ROT_DIM)
                         & (j[None, :] % (ROT_DIM // 2) == np.arange(ROT_DIM // 2)[:, None]), BF16)
    bd = jnp.asarray((np.arange(LANES)[:, None] // HEAD_DIM) == (np.arange(LANES)[None, :] // HEAD_DIM),
                     BF16)
    pos = positions.reshape(1, nbatch * seq).astype(I32)
    for l in range(depth):
        mod = _adaln(c.astype(F32), w_ada[l], b_ada[l]).reshape(nbatch, 6, d)
        x = _layer(x, mod, pos, (invf, spread, bd), {k: v[l] for k, v in params.items()})
    return x
```
